```python
import jax, jax.numpy as jnp
from jax import lax
import numpy as np

D_MODEL = 2048
BATCH = 8
SEQ = 4096
DEPTH = 1

N_META = 16
HEAD_DIM = 64
RWKV_HEADS = 16
RWKV_WIDTH = RWKV_HEADS * HEAD_DIM
FOX_HEADS = 16
FOX_WIDTH = FOX_HEADS * HEAD_DIM
DECAY_LORA = 96
AAA_LORA = 96
GATE_LORA = 256
D_FF = -(-8 * D_MODEL // (3 * 256)) * 256
Q_BLOCK = 128
RMS_EPS = 1e-6
GN_EPS = 64e-5
ATTN_SCALE = HEAD_DIM ** -0.5

RWKV_SPLITS = (RWKV_WIDTH, 2 * RWKV_WIDTH, 3 * RWKV_WIDTH,
               3 * RWKV_WIDTH + DECAY_LORA, 3 * RWKV_WIDTH + DECAY_LORA + AAA_LORA)
RWKV_COLS = 3 * RWKV_WIDTH + DECAY_LORA + AAA_LORA + GATE_LORA
FOX_SPLITS = (FOX_WIDTH, 2 * FOX_WIDTH, 3 * FOX_WIDTH)
FOX_COLS = 3 * FOX_WIDTH + FOX_HEADS
N_IN = RWKV_COLS + FOX_COLS + 2 * D_MODEL

kernel_name = 'hybrid_rwkv7_fox_meta_gated_block'

F32 = jnp.float32


def _rms(x, g):
    xf = x.astype(F32)
    xf = xf * lax.rsqrt(jnp.mean(xf * xf, axis=-1, keepdims=True) + RMS_EPS)
    return xf.astype(x.dtype) * g


def _wkv7_scan(r, w, k, v, a, b):
    B, L, H, N = r.shape

    def step(S, inp):
        r_t, w_t, k_t, v_t, a_t, b_t = inp
        sa = jnp.einsum('bhvk,bhk->bhv', S, a_t)
        S = (S * w_t[:, :, None, :] + sa[..., None] * b_t[:, :, None, :]
             + v_t[..., None] * k_t[:, :, None, :])
        return S, jnp.einsum('bhvk,bhk->bhv', S, r_t)

    xs = tuple(jnp.swapaxes(t, 0, 1) for t in (r, w, k, v, a, b))
    _, y = lax.scan(step, jnp.zeros((B, H, N, N), F32), xs)
    return jnp.swapaxes(y, 0, 1)


def _rwkv7(z, w0, w2, a0, a2, g2, k_k, k_a, r_k, gn_w, gn_b):
    B, L, _ = z.shape
    r, k, v, wd, ad, gd = jnp.split(z, RWKV_SPLITS, axis=-1)
    w_log = -jax.nn.softplus(-(w0 + jnp.tanh(wd) @ w2).astype(F32)) - 0.5
    decay = jnp.exp(-jnp.exp(w_log))
    a = jax.nn.sigmoid((a0 + ad @ a2).astype(F32))
    g = jax.nn.sigmoid(gd) @ g2
    heads = lambda t: t.reshape(B, L, RWKV_HEADS, HEAD_DIM)
    kk = heads((k * k_k).astype(F32))
    kk = kk / jnp.maximum(jnp.sqrt(jnp.sum(kk * kk, axis=-1, keepdims=True)), 1e-12)
    kf = k.astype(F32) * (1.0 + (a - 1.0) * k_a.astype(F32))
    rh, kh, vh, ah, dh = heads(r.astype(F32)), heads(kf), heads(v.astype(F32)), heads(a), heads(decay)
    y = _wkv7_scan(rh, dh, kh, vh, -kk, kk * ah)
    mu = jnp.mean(y, axis=-1, keepdims=True)
    var = jnp.mean(jnp.square(y - mu), axis=-1, keepdims=True)
    y = (y - mu) * lax.rsqrt(var + GN_EPS)
    y = y * gn_w.astype(F32).reshape(RWKV_HEADS, HEAD_DIM) + gn_b.astype(F32).reshape(RWKV_HEADS, HEAD_DIM)
    y = y + jnp.sum(rh * kh * r_k.astype(F32), axis=-1, keepdims=True) * vh
    return y.reshape(B, L, RWKV_WIDTH).astype(z.dtype) * g


def _fox(z, q_g, k_g, f_bias):
    B, L, _ = z.shape
    q, k, v, fl = jnp.split(z, FOX_SPLITS, axis=-1)
    q = _rms(q.reshape(B, L, FOX_HEADS, HEAD_DIM), q_g)
    k = _rms(k.reshape(B, L, FOX_HEADS, HEAD_DIM), k_g)
    v = v.reshape(B, L, FOX_HEADS, HEAD_DIM)
    log_f = jax.nn.log_sigmoid(fl.astype(F32) + f_bias.astype(F32))
    c = jnp.swapaxes(jnp.cumsum(log_f, axis=1), 1, 2)
    bounds = [(0, N_META)] + [(s, min(s + Q_BLOCK, L)) for s in range(N_META, L, Q_BLOCK)]
    outs = []
    for s, e in bounds:
        sc = jnp.einsum('bqhd,bkhd->bhqk', q[:, s:e], k[:, :e]).astype(F32) * ATTN_SCALE
        sc = sc + c[:, :, s:e, None] - c[:, :, None, :e]
        causal = jnp.arange(s, e)[:, None] >= jnp.arange(e)[None, :]
        p = jax.nn.softmax(jnp.where(causal, sc, -jnp.inf), axis=-1)
        outs.append(jnp.einsum('bhqk,bkhd->bqhd', p.astype(v.dtype), v[:, :e]))
    return jnp.concatenate(outs, axis=1).reshape(B, L, FOX_WIDTH)


def _layer(h, n1, w_in, mu, w0, w2, a0, a2, g2, k_k, k_a, r_k, gn_w, gn_b,
           q_g, k_g, f_bias, w_a, w_b, w_o, n2, w_gu, w_dn):
    xn = _rms(h, n1)
    proj = xn @ w_in
    z_rwkv, z_fox, z_gate = jnp.split(proj, (RWKV_COLS, RWKV_COLS + FOX_COLS), axis=-1)
    z_prev = jnp.pad(z_rwkv, ((0, 0), (1, 0), (0, 0)))[:, :-1]
    z_rwkv = z_rwkv + (z_prev - z_rwkv) * mu
    y_a = _rwkv7(z_rwkv, w0, w2, a0, a2, g2, k_k, k_a, r_k, gn_w, gn_b)
    y_b = _fox(z_fox, q_g, k_g, f_bias)
    gates = jax.nn.sigmoid(z_gate.astype(F32)).astype(h.dtype)
    g_a, g_b = jnp.split(gates, 2, axis=-1)
    merged = g_a * (y_a @ w_a) + g_b * (y_b @ w_b)
    h = h + merged @ w_o
    gate, up = jnp.split(_rms(h, n2) @ w_gu, 2, axis=-1)
    return h + (jax.nn.silu(gate) * up) @ w_dn


def _fwd_setup_inputs(seed: int = 0) -> dict:
    key = jax.random.key(seed)
    ks = jax.random.split(key, 24)
    nrm = lambda k, shape, scale: jax.random.normal(k, shape, F32) * scale
    Dp = DEPTH
    return {
        'x': nrm(ks[0], (BATCH, SEQ, D_MODEL), 1.0),
        'meta_tokens': nrm(ks[1], (N_META, D_MODEL), 1.0),
        'norm1_g': 1.0 + nrm(ks[2], (Dp, D_MODEL), 0.05),
        'w_in': nrm(ks[3], (Dp, D_MODEL, N_IN), D_MODEL ** -0.5),
        'rwkv_mu': jax.random.uniform(ks[4], (Dp, RWKV_COLS), F32, 0.0, 1.0),
        'rwkv_w0': jax.random.uniform(ks[5], (Dp, RWKV_WIDTH), F32, -5.0, -1.0),
        'rwkv_w2': nrm(ks[6], (Dp, DECAY_LORA, RWKV_WIDTH), 0.5 * DECAY_LORA ** -0.5),
        'rwkv_a0': nrm(ks[7], (Dp, RWKV_WIDTH), 0.1),
        'rwkv_a2': nrm(ks[8], (Dp, AAA_LORA, RWKV_WIDTH), 0.5 * AAA_LORA ** -0.5),
        'rwkv_g2': nrm(ks[9], (Dp, GATE_LORA, RWKV_WIDTH), GATE_LORA ** -0.5),
        'rwkv_k_k': 0.85 + nrm(ks[10], (Dp, RWKV_WIDTH), 0.05),
        'rwkv_k_a': 1.0 + nrm(ks[11], (Dp, RWKV_WIDTH), 0.05),
        'rwkv_r_k': nrm(ks[12], (Dp, RWKV_HEADS, HEAD_DIM), 0.1),
        'rwkv_gn_w': 1.0 + nrm(ks[13], (Dp, RWKV_WIDTH), 0.05),
        'rwkv_gn_b': nrm(ks[14], (Dp, RWKV_WIDTH), 0.01),
        'fox_q_norm_g': 1.0 + nrm(ks[15], (Dp, HEAD_DIM), 0.05),
        'fox_k_norm_g': 1.0 + nrm(ks[16], (Dp, HEAD_DIM), 0.05),
        'fox_f_bias': jax.random.uniform(ks[17], (Dp, FOX_HEADS), F32, 1.0, 4.0),
        'w_branch_a': nrm(ks[18], (Dp, RWKV_WIDTH, D_MODEL), RWKV_WIDTH ** -0.5),
        'w_branch_b': nrm(ks[19], (Dp, FOX_WIDTH, D_MODEL), FOX_WIDTH ** -0.5),
        'w_o': nrm(ks[20], (Dp, D_MODEL, D_MODEL), D_MODEL ** -0.5),
        'norm2_g': 1.0 + nrm(ks[21], (Dp, D_MODEL), 0.05),
        'w_gate_up': nrm(ks[22], (Dp, D_MODEL, 2 * D_FF), D_MODEL ** -0.5),
        'w_down': nrm(ks[23], (Dp, D_FF, D_MODEL), D_FF ** -0.5),
    }


def _fwd_reference(x, meta_tokens, norm1_g, w_in, rwkv_mu, rwkv_w0, rwkv_w2, rwkv_a0, rwkv_a2,
              rwkv_g2, rwkv_k_k, rwkv_k_a, rwkv_r_k, rwkv_gn_w, rwkv_gn_b, fox_q_norm_g,
              fox_k_norm_g, fox_f_bias, w_branch_a, w_branch_b, w_o, norm2_g, w_gate_up, w_down):
    B = x.shape[0]
    meta = jnp.broadcast_to(meta_tokens.astype(x.dtype)[None], (B, N_META, D_MODEL))
    h = jnp.concatenate([meta, x], axis=1)
    for l in range(DEPTH):
        h = _layer(h, norm1_g[l], w_in[l], rwkv_mu[l], rwkv_w0[l], rwkv_w2[l], rwkv_a0[l],
                   rwkv_a2[l], rwkv_g2[l], rwkv_k_k[l], rwkv_k_a[l], rwkv_r_k[l], rwkv_gn_w[l],
                   rwkv_gn_b[l], fox_q_norm_g[l], fox_k_norm_g[l], fox_f_bias[l], w_branch_a[l],
                   w_branch_b[l], w_o[l], norm2_g[l], w_gate_up[l], w_down[l])
    return h[:, N_META:]


import jax as _jax
import jax.numpy as _jnp

TWIN_FORMAT = 'train_step'
FWD_PARAMS = ['x', 'meta_tokens', 'norm1_g', 'w_in', 'rwkv_mu', 'rwkv_w0', 'rwkv_w2', 'rwkv_a0', 'rwkv_a2', 'rwkv_g2', 'rwkv_k_k', 'rwkv_k_a', 'rwkv_r_k', 'rwkv_gn_w', 'rwkv_gn_b', 'fox_q_norm_g', 'fox_k_norm_g', 'fox_f_bias', 'w_branch_a', 'w_branch_b', 'w_o', 'norm2_g', 'w_gate_up', 'w_down']
TWIN_WEIGHTS = ['meta_tokens', 'norm1_g', 'w_in', 'rwkv_mu', 'rwkv_w0', 'rwkv_w2', 'rwkv_a0', 'rwkv_a2', 'rwkv_g2', 'rwkv_k_k', 'rwkv_k_a', 'rwkv_r_k', 'rwkv_gn_w', 'rwkv_gn_b', 'fox_q_norm_g', 'fox_k_norm_g', 'fox_f_bias', 'w_branch_a', 'w_branch_b', 'w_o', 'norm2_g', 'w_gate_up', 'w_down']
TWIN_DIFF_INPUT = 'x'
TWIN_INPUTS = ['x', 'meta_tokens', 'norm1_g', 'w_in', 'rwkv_mu', 'rwkv_w0', 'rwkv_w2', 'rwkv_a0', 'rwkv_a2', 'rwkv_g2', 'rwkv_k_k', 'rwkv_k_a', 'rwkv_r_k', 'rwkv_gn_w', 'rwkv_gn_b', 'fox_q_norm_g', 'fox_k_norm_g', 'fox_f_bias', 'w_branch_a', 'w_branch_b', 'w_o', 'norm2_g', 'w_gate_up', 'w_down', 'loss_target', 'm_meta_tokens', 'm_norm1_g', 'm_w_in', 'm_rwkv_mu', 'm_rwkv_w0', 'm_rwkv_w2', 'm_rwkv_a0', 'm_rwkv_a2', 'm_rwkv_g2', 'm_rwkv_k_k', 'm_rwkv_k_a', 'm_rwkv_r_k', 'm_rwkv_gn_w', 'm_rwkv_gn_b', 'm_fox_q_norm_g', 'm_fox_k_norm_g', 'm_fox_f_bias', 'm_w_branch_a', 'm_w_branch_b', 'm_w_o', 'm_norm2_g', 'm_w_gate_up', 'm_w_down', 'v_meta_tokens', 'v_norm1_g', 'v_w_in', 'v_rwkv_mu', 'v_rwkv_w0', 'v_rwkv_w2', 'v_rwkv_a0', 'v_rwkv_a2', 'v_rwkv_g2', 'v_rwkv_k_k', 'v_rwkv_k_a', 'v_rwkv_r_k', 'v_rwkv_gn_w', 'v_rwkv_gn_b', 'v_fox_q_norm_g', 'v_fox_k_norm_g', 'v_fox_f_bias', 'v_w_branch_a', 'v_w_branch_b', 'v_w_o', 'v_norm2_g', 'v_w_gate_up', 'v_w_down']
TWIN_OUTPUTS = ['loss', 'grad_x', 'grad_meta_tokens', 'grad_norm1_g', 'grad_w_in', 'grad_rwkv_mu', 'grad_rwkv_w0', 'grad_rwkv_w2', 'grad_rwkv_a0', 'grad_rwkv_a2', 'grad_rwkv_g2', 'grad_rwkv_k_k', 'grad_rwkv_k_a', 'grad_rwkv_r_k', 'grad_rwkv_gn_w', 'grad_rwkv_gn_b', 'grad_fox_q_norm_g', 'grad_fox_k_norm_g', 'grad_fox_f_bias', 'grad_w_branch_a', 'grad_w_branch_b', 'grad_w_o', 'grad_norm2_g', 'grad_w_gate_up', 'grad_w_down', 'delta_meta_tokens', 'delta_norm1_g', 'delta_w_in', 'delta_rwkv_mu', 'delta_rwkv_w0', 'delta_rwkv_w2', 'delta_rwkv_a0', 'delta_rwkv_a2', 'delta_rwkv_g2', 'delta_rwkv_k_k', 'delta_rwkv_k_a', 'delta_rwkv_r_k', 'delta_rwkv_gn_w', 'delta_rwkv_gn_b', 'delta_fox_q_norm_g', 'delta_fox_k_norm_g', 'delta_fox_f_bias', 'delta_w_branch_a', 'delta_w_branch_b', 'delta_w_o', 'delta_norm2_g', 'delta_w_gate_up', 'delta_w_down', 'new_m_meta_tokens', 'new_m_norm1_g', 'new_m_w_in', 'new_m_rwkv_mu', 'new_m_rwkv_w0', 'new_m_rwkv_w2', 'new_m_rwkv_a0', 'new_m_rwkv_a2', 'new_m_rwkv_g2', 'new_m_rwkv_k_k', 'new_m_rwkv_k_a', 'new_m_rwkv_r_k', 'new_m_rwkv_gn_w', 'new_m_rwkv_gn_b', 'new_m_fox_q_norm_g', 'new_m_fox_k_norm_g', 'new_m_fox_f_bias', 'new_m_w_branch_a', 'new_m_w_branch_b', 'new_m_w_o', 'new_m_norm2_g', 'new_m_w_gate_up', 'new_m_w_down', 'new_v_meta_tokens', 'new_v_norm1_g', 'new_v_w_in', 'new_v_rwkv_mu', 'new_v_rwkv_w0', 'new_v_rwkv_w2', 'new_v_rwkv_a0', 'new_v_rwkv_a2', 'new_v_rwkv_g2', 'new_v_rwkv_k_k', 'new_v_rwkv_k_a', 'new_v_rwkv_r_k', 'new_v_rwkv_gn_w', 'new_v_rwkv_gn_b', 'new_v_fox_q_norm_g', 'new_v_fox_k_norm_g', 'new_v_fox_f_bias', 'new_v_w_branch_a', 'new_v_w_branch_b', 'new_v_w_o', 'new_v_norm2_g', 'new_v_w_gate_up', 'new_v_w_down']
TWIN_LEAF_KINDS = {'loss': 'loss', 'grad_x': 'grad_x', 'grad_meta_tokens': 'grad_w', 'grad_norm1_g': 'grad_w', 'grad_w_in': 'grad_w', 'grad_rwkv_mu': 'grad_w', 'grad_rwkv_w0': 'grad_w', 'grad_rwkv_w2': 'grad_w', 'grad_rwkv_a0': 'grad_w', 'grad_rwkv_a2': 'grad_w', 'grad_rwkv_g2': 'grad_w', 'grad_rwkv_k_k': 'grad_w', 'grad_rwkv_k_a': 'grad_w', 'grad_rwkv_r_k': 'grad_w', 'grad_rwkv_gn_w': 'grad_w', 'grad_rwkv_gn_b': 'grad_w', 'grad_fox_q_norm_g': 'grad_w', 'grad_fox_k_norm_g': 'grad_w', 'grad_fox_f_bias': 'grad_w', 'grad_w_branch_a': 'grad_w', 'grad_w_branch_b': 'grad_w', 'grad_w_o': 'grad_w', 'grad_norm2_g': 'grad_w', 'grad_w_gate_up': 'grad_w', 'grad_w_down': 'grad_w', 'delta_meta_tokens': 'delta_w', 'delta_norm1_g': 'delta_w', 'delta_w_in': 'delta_w', 'delta_rwkv_mu': 'delta_w', 'delta_rwkv_w0': 'delta_w', 'delta_rwkv_w2': 'delta_w', 'delta_rwkv_a0': 'delta_w', 'delta_rwkv_a2': 'delta_w', 'delta_rwkv_g2': 'delta_w', 'delta_rwkv_k_k': 'delta_w', 'delta_rwkv_k_a': 'delta_w', 'delta_rwkv_r_k': 'delta_w', 'delta_rwkv_gn_w': 'delta_w', 'delta_rwkv_gn_b': 'delta_w', 'delta_fox_q_norm_g': 'delta_w', 'delta_fox_k_norm_g': 'delta_w', 'delta_fox_f_bias': 'delta_w', 'delta_w_branch_a': 'delta_w', 'delta_w_branch_b': 'delta_w', 'delta_w_o': 'delta_w', 'delta_norm2_g': 'delta_w', 'delta_w_gate_up': 'delta_w', 'delta_w_down': 'delta_w', 'new_m_meta_tokens': 'new_m', 'new_m_norm1_g': 'new_m', 'new_m_w_in': 'new_m', 'new_m_rwkv_mu': 'new_m', 'new_m_rwkv_w0': 'new_m', 'new_m_rwkv_w2': 'new_m', 'new_m_rwkv_a0': 'new_m', 'new_m_rwkv_a2': 'new_m', 'new_m_rwkv_g2': 'new_m', 'new_m_rwkv_k_k': 'new_m', 'new_m_rwkv_k_a': 'new_m', 'new_m_rwkv_r_k': 'new_m', 'new_m_rwkv_gn_w': 'new_m', 'new_m_rwkv_gn_b': 'new_m', 'new_m_fox_q_norm_g': 'new_m', 'new_m_fox_k_norm_g': 'new_m', 'new_m_fox_f_bias': 'new_m', 'new_m_w_branch_a': 'new_m', 'new_m_w_branch_b': 'new_m', 'new_m_w_o': 'new_m', 'new_m_norm2_g': 'new_m', 'new_m_w_gate_up': 'new_m', 'new_m_w_down': 'new_m', 'new_v_meta_tokens': 'new_v', 'new_v_norm1_g': 'new_v', 'new_v_w_in': 'new_v', 'new_v_rwkv_mu': 'new_v', 'new_v_rwkv_w0': 'new_v', 'new_v_rwkv_w2': 'new_v', 'new_v_rwkv_a0': 'new_v', 'new_v_rwkv_a2': 'new_v', 'new_v_rwkv_g2': 'new_v', 'new_v_rwkv_k_k': 'new_v', 'new_v_rwkv_k_a': 'new_v', 'new_v_rwkv_r_k': 'new_v', 'new_v_rwkv_gn_w': 'new_v', 'new_v_rwkv_gn_b': 'new_v', 'new_v_fox_q_norm_g': 'new_v', 'new_v_fox_k_norm_g': 'new_v', 'new_v_fox_f_bias': 'new_v', 'new_v_w_branch_a': 'new_v', 'new_v_w_branch_b': 'new_v', 'new_v_w_o': 'new_v', 'new_v_norm2_g': 'new_v', 'new_v_w_gate_up': 'new_v', 'new_v_w_down': 'new_v'}


def _forward(args):
    return _fwd_reference(*[args[k] for k in FWD_PARAMS])


def _output_shape():
    def fwd():
        inp = _fwd_setup_inputs(0)
        return _fwd_reference(*[inp[k] for k in FWD_PARAMS])
    out = _jax.eval_shape(fwd)
    return out.shape, out.dtype

N_MICROBATCH = 1
ADAM_LR = 0.001
ADAM_B1 = 0.9
ADAM_B2 = 0.999
ADAM_EPS = 1e-08
ADAM_WD = 0.01
ADAM_STEP = 10
PER_EXAMPLE_BATCH_AXIS = {'x': 0, 'loss_target': 0}
SHARED_INPUTS = []
_WEIGHT_DTYPES = {'meta_tokens': _jnp.float32, 'norm1_g': _jnp.float32, 'w_in': _jnp.float32, 'rwkv_mu': _jnp.float32, 'rwkv_w0': _jnp.float32, 'rwkv_w2': _jnp.float32, 'rwkv_a0': _jnp.float32, 'rwkv_a2': _jnp.float32, 'rwkv_g2': _jnp.float32, 'rwkv_k_k': _jnp.float32, 'rwkv_k_a': _jnp.float32, 'rwkv_r_k': _jnp.float32, 'rwkv_gn_w': _jnp.float32, 'rwkv_gn_b': _jnp.float32, 'fox_q_norm_g': _jnp.float32, 'fox_k_norm_g': _jnp.float32, 'fox_f_bias': _jnp.float32, 'w_branch_a': _jnp.float32, 'w_branch_b': _jnp.float32, 'w_o': _jnp.float32, 'norm2_g': _jnp.float32, 'w_gate_up': _jnp.float32, 'w_down': _jnp.float32}
MOMENT_SCALE = {'meta_tokens': 6.831388e-03, 'norm1_g': 1.211523e+00, 'w_in': 6.324476e-02, 'rwkv_mu': 1.214848e+00, 'rwkv_w0': 4.320380e-02, 'rwkv_w2': 5.067400e-03, 'rwkv_a0': 1.547177e-01, 'rwkv_a2': 3.481021e-02, 'rwkv_g2': 2.467270e+00, 'rwkv_k_k': 8.751392e-02, 'rwkv_k_a': 3.122836e-01, 'rwkv_r_k': 1.882777e+00, 'rwkv_gn_w': 4.150972e+00, 'rwkv_gn_b': 8.284332e-01, 'fox_q_norm_g': 7.551591e+00, 'fox_k_norm_g': 7.528346e+00, 'fox_f_bias': 2.536209e+01, 'w_branch_a': 8.874107e-02, 'w_branch_b': 4.698630e-02, 'w_o': 9.062966e-02, 'norm2_g': 1.247322e+01, 'w_gate_up': 7.424181e-02, 'w_down': 1.147352e-01}


def _to_microbatches(a, axis):
    t = _jnp.moveaxis(a, axis, 0)
    t = t.reshape((N_MICROBATCH, t.shape[0] // N_MICROBATCH) + t.shape[1:])
    return _jnp.moveaxis(t, 1, axis + 1)


def setup_inputs(seed: int = 0) -> dict:
    inp = _fwd_setup_inputs(seed)
    key = _jax.random.fold_in(_jax.random.key(seed), 7919)
    shape, _ = _output_shape()
    out = dict(inp)
    out["loss_target"] = _jax.random.normal(_jax.random.fold_in(key, 0), shape, _jnp.float32)
    for i, name in enumerate(TWIN_WEIGHTS):
        w = inp[name].astype(_jnp.float32)
        if MOMENT_SCALE is None:
            s = _jnp.sqrt(_jnp.mean(_jnp.square(w)) + 1e-30)
        else:
            s = MOMENT_SCALE[name]
        km, kv = _jax.random.split(_jax.random.fold_in(key, i + 1))
        out[name] = w
        out["m_" + name] = s * _jax.random.normal(km, w.shape, _jnp.float32)
        out["v_" + name] = (s * s) * _jax.random.uniform(kv, w.shape, _jnp.float32, 0.5, 1.5)
    if N_MICROBATCH > 1:
        for name, axis in PER_EXAMPLE_BATCH_AXIS.items():
            out[name] = _to_microbatches(out[name], axis)
    return {'x': out['x'], 'meta_tokens': out['meta_tokens'], 'norm1_g': out['norm1_g'], 'w_in': out['w_in'], 'rwkv_mu': out['rwkv_mu'], 'rwkv_w0': out['rwkv_w0'], 'rwkv_w2': out['rwkv_w2'], 'rwkv_a0': out['rwkv_a0'], 'rwkv_a2': out['rwkv_a2'], 'rwkv_g2': out['rwkv_g2'], 'rwkv_k_k': out['rwkv_k_k'], 'rwkv_k_a': out['rwkv_k_a'], 'rwkv_r_k': out['rwkv_r_k'], 'rwkv_gn_w': out['rwkv_gn_w'], 'rwkv_gn_b': out['rwkv_gn_b'], 'fox_q_norm_g': out['fox_q_norm_g'], 'fox_k_norm_g': out['fox_k_norm_g'], 'fox_f_bias': out['fox_f_bias'], 'w_branch_a': out['w_branch_a'], 'w_branch_b': out['w_branch_b'], 'w_o': out['w_o'], 'norm2_g': out['norm2_g'], 'w_gate_up': out['w_gate_up'], 'w_down': out['w_down'], 'loss_target': out['loss_target'], 'm_meta_tokens': out['m_meta_tokens'], 'm_norm1_g': out['m_norm1_g'], 'm_w_in': out['m_w_in'], 'm_rwkv_mu': out['m_rwkv_mu'], 'm_rwkv_w0': out['m_rwkv_w0'], 'm_rwkv_w2': out['m_rwkv_w2'], 'm_rwkv_a0': out['m_rwkv_a0'], 'm_rwkv_a2': out['m_rwkv_a2'], 'm_rwkv_g2': out['m_rwkv_g2'], 'm_rwkv_k_k': out['m_rwkv_k_k'], 'm_rwkv_k_a': out['m_rwkv_k_a'], 'm_rwkv_r_k': out['m_rwkv_r_k'], 'm_rwkv_gn_w': out['m_rwkv_gn_w'], 'm_rwkv_gn_b': out['m_rwkv_gn_b'], 'm_fox_q_norm_g': out['m_fox_q_norm_g'], 'm_fox_k_norm_g': out['m_fox_k_norm_g'], 'm_fox_f_bias': out['m_fox_f_bias'], 'm_w_branch_a': out['m_w_branch_a'], 'm_w_branch_b': out['m_w_branch_b'], 'm_w_o': out['m_w_o'], 'm_norm2_g': out['m_norm2_g'], 'm_w_gate_up': out['m_w_gate_up'], 'm_w_down': out['m_w_down'], 'v_meta_tokens': out['v_meta_tokens'], 'v_norm1_g': out['v_norm1_g'], 'v_w_in': out['v_w_in'], 'v_rwkv_mu': out['v_rwkv_mu'], 'v_rwkv_w0': out['v_rwkv_w0'], 'v_rwkv_w2': out['v_rwkv_w2'], 'v_rwkv_a0': out['v_rwkv_a0'], 'v_rwkv_a2': out['v_rwkv_a2'], 'v_rwkv_g2': out['v_rwkv_g2'], 'v_rwkv_k_k': out['v_rwkv_k_k'], 'v_rwkv_k_a': out['v_rwkv_k_a'], 'v_rwkv_r_k': out['v_rwkv_r_k'], 'v_rwkv_gn_w': out['v_rwkv_gn_w'], 'v_rwkv_gn_b': out['v_rwkv_gn_b'], 'v_fox_q_norm_g': out['v_fox_q_norm_g'], 'v_fox_k_norm_g': out['v_fox_k_norm_g'], 'v_fox_f_bias': out['v_fox_f_bias'], 'v_w_branch_a': out['v_w_branch_a'], 'v_w_branch_b': out['v_w_branch_b'], 'v_w_o': out['v_w_o'], 'v_norm2_g': out['v_norm2_g'], 'v_w_gate_up': out['v_w_gate_up'], 'v_w_down': out['v_w_down']}


def _loss(weights, diff, rest, loss_target):
    with _jax.named_scope("forward"):
        args = {**rest, TWIN_DIFF_INPUT: diff, **{k: w.astype(_WEIGHT_DTYPES[k]) for k, w in weights.items()}}
        y = _forward(args)
    with _jax.named_scope("loss_head"):
        err = _jnp.square(y.astype(_jnp.float32) - loss_target)
        return 0.5 * _jnp.sum(_jnp.mean(err, axis=-1)) if err.ndim else 0.5 * err


def _adamw(w, g, m, v):
    m = ADAM_B1 * m + (1.0 - ADAM_B1) * g
    v = ADAM_B2 * v + (1.0 - ADAM_B2) * _jnp.square(g)
    m_hat = m / (1.0 - ADAM_B1 ** ADAM_STEP)
    v_hat = v / (1.0 - ADAM_B2 ** ADAM_STEP)
    delta = -ADAM_LR * (m_hat / (_jnp.sqrt(v_hat) + ADAM_EPS) + ADAM_WD * w)
    return delta, m, v


def reference(x, meta_tokens, norm1_g, w_in, rwkv_mu, rwkv_w0, rwkv_w2, rwkv_a0, rwkv_a2, rwkv_g2, rwkv_k_k, rwkv_k_a, rwkv_r_k, rwkv_gn_w, rwkv_gn_b, fox_q_norm_g, fox_k_norm_g, fox_f_bias, w_branch_a, w_branch_b, w_o, norm2_g, w_gate_up, w_down, loss_target, m_meta_tokens, m_norm1_g, m_w_in, m_rwkv_mu, m_rwkv_w0, m_rwkv_w2, m_rwkv_a0, m_rwkv_a2, m_rwkv_g2, m_rwkv_k_k, m_rwkv_k_a, m_rwkv_r_k, m_rwkv_gn_w, m_rwkv_gn_b, m_fox_q_norm_g, m_fox_k_norm_g, m_fox_f_bias, m_w_branch_a, m_w_branch_b, m_w_o, m_norm2_g, m_w_gate_up, m_w_down, v_meta_tokens, v_norm1_g, v_w_in, v_rwkv_mu, v_rwkv_w0, v_rwkv_w2, v_rwkv_a0, v_rwkv_a2, v_rwkv_g2, v_rwkv_k_k, v_rwkv_k_a, v_rwkv_r_k, v_rwkv_gn_w, v_rwkv_gn_b, v_fox_q_norm_g, v_fox_k_norm_g, v_fox_f_bias, v_w_branch_a, v_w_branch_b, v_w_o, v_norm2_g, v_w_gate_up, v_w_down):
    given = dict(x=x, meta_tokens=meta_tokens, norm1_g=norm1_g, w_in=w_in, rwkv_mu=rwkv_mu, rwkv_w0=rwkv_w0, rwkv_w2=rwkv_w2, rwkv_a0=rwkv_a0, rwkv_a2=rwkv_a2, rwkv_g2=rwkv_g2, rwkv_k_k=rwkv_k_k, rwkv_k_a=rwkv_k_a, rwkv_r_k=rwkv_r_k, rwkv_gn_w=rwkv_gn_w, rwkv_gn_b=rwkv_gn_b, fox_q_norm_g=fox_q_norm_g, fox_k_norm_g=fox_k_norm_g, fox_f_bias=fox_f_bias, w_branch_a=w_branch_a, w_branch_b=w_branch_b, w_o=w_o, norm2_g=norm2_g, w_gate_up=w_gate_up, w_down=w_down, loss_target=loss_target, m_meta_tokens=m_meta_tokens, m_norm1_g=m_norm1_g, m_w_in=m_w_in, m_rwkv_mu=m_rwkv_mu, m_rwkv_w0=m_rwkv_w0, m_rwkv_w2=m_rwkv_w2, m_rwkv_a0=m_rwkv_a0, m_rwkv_a2=m_rwkv_a2, m_rwkv_g2=m_rwkv_g2, m_rwkv_k_k=m_rwkv_k_k, m_rwkv_k_a=m_rwkv_k_a, m_rwkv_r_k=m_rwkv_r_k, m_rwkv_gn_w=m_rwkv_gn_w, m_rwkv_gn_b=m_rwkv_gn_b, m_fox_q_norm_g=m_fox_q_norm_g, m_fox_k_norm_g=m_fox_k_norm_g, m_fox_f_bias=m_fox_f_bias, m_w_branch_a=m_w_branch_a, m_w_branch_b=m_w_branch_b, m_w_o=m_w_o, m_norm2_g=m_norm2_g, m_w_gate_up=m_w_gate_up, m_w_down=m_w_down, v_meta_tokens=v_meta_tokens, v_norm1_g=v_norm1_g, v_w_in=v_w_in, v_rwkv_mu=v_rwkv_mu, v_rwkv_w0=v_rwkv_w0, v_rwkv_w2=v_rwkv_w2, v_rwkv_a0=v_rwkv_a0, v_rwkv_a2=v_rwkv_a2, v_rwkv_g2=v_rwkv_g2, v_rwkv_k_k=v_rwkv_k_k, v_rwkv_k_a=v_rwkv_k_a, v_rwkv_r_k=v_rwkv_r_k, v_rwkv_gn_w=v_rwkv_gn_w, v_rwkv_gn_b=v_rwkv_gn_b, v_fox_q_norm_g=v_fox_q_norm_g, v_fox_k_norm_g=v_fox_k_norm_g, v_fox_f_bias=v_fox_f_bias, v_w_branch_a=v_w_branch_a, v_w_branch_b=v_w_branch_b, v_w_o=v_w_o, v_norm2_g=v_norm2_g, v_w_gate_up=v_w_gate_up, v_w_down=v_w_down)
    weights = {n: given[n] for n in TWIN_WEIGHTS}
    shared = {n: given[n] for n in SHARED_INPUTS}
    per_example = {n: given[n] for n in ['x']}
    grad_fn = _jax.value_and_grad(_loss, argnums=(0, 1))

    def one_microbatch(ex, loss_target):
        ex = dict(ex)
        diff = ex.pop(TWIN_DIFF_INPUT)
        return grad_fn(weights, diff, {**shared, **ex}, loss_target)

    if N_MICROBATCH == 1:
        loss, (grad_w, grad_x) = one_microbatch(per_example, given["loss_target"])
    else:
        def body(carry, xs):
            loss_sum, grad_sum = carry
            l_k, (gw_k, gx_k) = one_microbatch(xs[0], xs[1])
            with _jax.named_scope("update"):
                return (loss_sum + l_k, _jax.tree.map(_jnp.add, grad_sum, gw_k)), gx_k

        init = (_jnp.zeros((), _jnp.float32), _jax.tree.map(_jnp.zeros_like, weights))
        (loss, grad_w), grad_x = _jax.lax.scan(body, init, (per_example, given["loss_target"]))
    with _jax.named_scope("update"):
        delta_w, new_m, new_v = {}, {}, {}
        for n in TWIN_WEIGHTS:
            delta_w[n], new_m[n], new_v[n] = _adamw(weights[n], grad_w[n], given["m_" + n], given["v_" + n])
    return (loss, grad_x, *[grad_w[n] for n in TWIN_WEIGHTS], *[delta_w[n] for n in TWIN_WEIGHTS],
            *[new_m[n] for n in TWIN_WEIGHTS], *[new_v[n] for n in TWIN_WEIGHTS])
```

```python
import functools

import jax
import jax.numpy as jnp
from jax import lax
from jax.experimental import pallas as pl
from jax.experimental.pallas import tpu as pltpu

F32 = jnp.float32
BF16 = jnp.bfloat16

N_DEV = 8
MESH_AXES = ("x", "y", "c")
HEAD_DIM = 64
TOKEN_TILE = 128
WKV_CHUNK = 64
ATTN_BLOCK = 128
RMS_EPS = 1e-6
GN_EPS = 64e-5
L2_FLOOR = 1e-12
NEG_BIG = -1e30
ADAM_LR, ADAM_B1, ADAM_B2, ADAM_EPS, ADAM_WD, ADAM_STEP = 0.001, 0.9, 0.999, 1e-08, 0.01, 10
VMEM_BYTES_V7X = 64 * 1024 * 1024
VMEM_LIMIT_CAP = 56 * 1024 * 1024
VMEM_LIMIT_FLOOR = 32 * 1024 * 1024


def _vmem_limit(estimate_bytes):
    return int(min(max(estimate_bytes * 5 // 4, VMEM_LIMIT_FLOOR), VMEM_LIMIT_CAP))


def _pick(dim, cands):
    for c in cands:
        if dim % c == 0:
            return c
    return dim


def _row_tile(rows, width, itemsize=4, budget=2 * 1024 * 1024):
    for c in (1408, 1024, 704, 512, 384, 256, 128, 64, 32, 16, 8):
        if rows % c == 0 and c * width * itemsize <= budget:
            return c
    return rows


def _dg(a, b, ta, tb):
    dims = (((0 if ta else 1,), (1 if tb else 0,)), ((), ()))
    return lax.dot_general(a, b, dims, preferred_element_type=F32)


def _split(x, n):
    parts = []
    for _ in range(n):
        h = x.astype(BF16)
        parts.append(h)
        x = x - h.astype(F32)
    return parts


def _mm(a, b, ta=False, tb=False):
    return _dg(a.astype(BF16), b.astype(BF16), ta, tb)


def _matmul(a, b, ta=False, tb=False, out_dtype=F32, name="matmul"):
    if ta:
        kdim, m = a.shape
    else:
        m, kdim = a.shape
    if tb:
        n, k2 = b.shape
    else:
        k2, n = b.shape
    assert kdim == k2, (a.shape, b.shape, ta, tb)
    lane_tiles = (1024, 640, 512, 384, 256, 128)
    sublane_tiles = (1024, 704, 512, 384, 256, 128)
    tm = _pick(m, lane_tiles if ta else sublane_tiles)
    tn = _pick(n, lane_tiles)
    tk = _pick(kdim, sublane_tiles if ta else lane_tiles)
    nk = kdim // tk

    def body(a_ref, b_ref, o_ref, acc):
        kk = pl.program_id(2)

        @pl.when(kk == 0)
        def _():
            acc[...] = jnp.zeros_like(acc)

        acc[...] += _dg(a_ref[...].astype(BF16), b_ref[...].astype(BF16), ta, tb)

        @pl.when(kk == nk - 1)
        def _():
            o_ref[...] = acc[...].astype(o_ref.dtype)

    a_spec = pl.BlockSpec((tk, tm), lambda i, j, k: (k, i)) if ta else pl.BlockSpec((tm, tk), lambda i, j, k: (i, k))
    b_spec = pl.BlockSpec((tn, tk), lambda i, j, k: (j, k)) if tb else pl.BlockSpec((tk, tn), lambda i, j, k: (k, j))
    est = 2 * (tm * tk * a.dtype.itemsize + tk * tn * b.dtype.itemsize + tm * tn * jnp.dtype(out_dtype).itemsize) + tm * tn * 4
    return pl.pallas_call(
        body, name=name,
        grid=(m // tm, n // tn, nk),
        in_specs=[a_spec, b_spec],
        out_specs=pl.BlockSpec((tm, tn), lambda i, j, k: (i, j)),
        out_shape=jax.ShapeDtypeStruct((m, n), out_dtype),
        scratch_shapes=[pltpu.VMEM((tm, tn), F32)],
        compiler_params=pltpu.CompilerParams(dimension_semantics=("parallel", "parallel", "arbitrary"),
                                             vmem_limit_bytes=_vmem_limit(est)),
    )(a, b)


@jax.custom_vjp
def dense(x, w):
    return _matmul(x, w, name="dense_fwd")


def _dense_fwd(x, w):
    return _matmul(x, w, name="dense_fwd"), (x, w)


def _dense_bwd(res, dy):
    x, w = res
    dx = _matmul(dy, w, tb=True, out_dtype=x.dtype, name="dense_dx")
    dw = _matmul(x, dy, ta=True, out_dtype=w.dtype, name="dense_dw")
    return dx, dw


dense.defvjp(_dense_fwd, _dense_bwd)


def _rms_fwd_call(x, g):
    rows, d = x.shape
    tr = _row_tile(rows, d)

    def body(x_ref, g_ref, y_ref):
        xv = x_ref[...]
        rstd = lax.rsqrt(jnp.mean(xv * xv, axis=1, keepdims=True) + RMS_EPS)
        y_ref[...] = (xv * rstd) * g_ref[...]

    return pl.pallas_call(
        body, name="rms_fwd", grid=(rows // tr,),
        in_specs=[pl.BlockSpec((tr, d), lambda i: (i, 0)), pl.BlockSpec((1, d), lambda i: (0, 0))],
        out_specs=pl.BlockSpec((tr, d), lambda i: (i, 0)),
        out_shape=jax.ShapeDtypeStruct((rows, d), F32),
        compiler_params=pltpu.CompilerParams(dimension_semantics=("parallel",)),
    )(x, g)


def _rms_bwd_call(x, g, dy):
    rows, d = x.shape
    tr = _row_tile(rows, d)

    def body(x_ref, g_ref, dy_ref, dx_ref, dg_ref):
        @pl.when(pl.program_id(0) == 0)
        def _():
            dg_ref[...] = jnp.zeros_like(dg_ref)

        xv = x_ref[...]
        dyv = dy_ref[...]
        rstd = lax.rsqrt(jnp.mean(xv * xv, axis=1, keepdims=True) + RMS_EPS)
        xhat = xv * rstd
        dxhat = dyv * g_ref[...]
        dx_ref[...] = rstd * (dxhat - xhat * jnp.mean(dxhat * xhat, axis=1, keepdims=True))
        dg_ref[...] += jnp.sum(dyv * xhat, axis=0, keepdims=True)

    return pl.pallas_call(
        body, name="rms_bwd", grid=(rows // tr,),
        in_specs=[pl.BlockSpec((tr, d), lambda i: (i, 0)), pl.BlockSpec((1, d), lambda i: (0, 0)),
                  pl.BlockSpec((tr, d), lambda i: (i, 0))],
        out_specs=[pl.BlockSpec((tr, d), lambda i: (i, 0)), pl.BlockSpec((1, d), lambda i: (0, 0))],
        out_shape=[jax.ShapeDtypeStruct((rows, d), F32), jax.ShapeDtypeStruct((1, d), F32)],
        compiler_params=pltpu.CompilerParams(dimension_semantics=("arbitrary",)),
    )(x, g, dy)


@jax.custom_vjp
def rmsnorm(x, g):
    return _rms_fwd_call(x, g)


rmsnorm.defvjp(lambda x, g: (_rms_fwd_call(x, g), (x, g)), lambda res, dy: tuple(_rms_bwd_call(res[0], res[1], dy)))


def _bcast_call(x, p, mul):
    rows, d = x.shape
    tr = _row_tile(rows, d)

    def body(x_ref, p_ref, y_ref):
        y_ref[...] = x_ref[...] * p_ref[...] if mul else x_ref[...] + p_ref[...]

    return pl.pallas_call(
        body, name="bcast_mul" if mul else "bcast_add", grid=(rows // tr,),
        in_specs=[pl.BlockSpec((tr, d), lambda i: (i, 0)), pl.BlockSpec((1, d), lambda i: (0, 0))],
        out_specs=pl.BlockSpec((tr, d), lambda i: (i, 0)),
        out_shape=jax.ShapeDtypeStruct((rows, d), F32),
        compiler_params=pltpu.CompilerParams(dimension_semantics=("parallel",)),
    )(x, p)


def _colsum_call(a, b=None):
    rows, d = a.shape
    tr = _row_tile(rows, d)
    ops = (a,) if b is None else (a, b)

    def body(*refs):
        o_ref = refs[-1]

        @pl.when(pl.program_id(0) == 0)
        def _():
            o_ref[...] = jnp.zeros_like(o_ref)

        v = refs[0][...] if b is None else refs[0][...] * refs[1][...]
        o_ref[...] += jnp.sum(v, axis=0, keepdims=True)

    return pl.pallas_call(
        body, name="colsum", grid=(rows // tr,),
        in_specs=[pl.BlockSpec((tr, d), lambda i: (i, 0))] * len(ops),
        out_specs=pl.BlockSpec((1, d), lambda i: (0, 0)),
        out_shape=jax.ShapeDtypeStruct((1, d), F32),
        compiler_params=pltpu.CompilerParams(dimension_semantics=("arbitrary",)),
    )(*ops)


@jax.custom_vjp
def bmul(x, p):
    return _bcast_call(x, p, True)


bmul.defvjp(lambda x, p: (_bcast_call(x, p, True), (x, p)),
            lambda res, dy: (_bcast_call(dy, res[1], True), _colsum_call(dy, res[0])))


@jax.custom_vjp
def badd(x, p):
    return _bcast_call(x, p, False)


badd.defvjp(lambda x, p: (_bcast_call(x, p, False), None), lambda res, dy: (dy, _colsum_call(dy)))


def _heads(width):
    return [slice(h * HEAD_DIM, (h + 1) * HEAD_DIM) for h in range(width // HEAD_DIM)]


def _head_rms_fwd_call(x, g):
    rows, w = x.shape
    tr = _row_tile(rows, w, budget=1024 * 1024)

    def body(x_ref, g_ref, y_ref):
        for sl in _heads(w):
            xv = x_ref[:, sl]
            rstd = lax.rsqrt(jnp.mean(xv * xv, axis=1, keepdims=True) + RMS_EPS)
            y_ref[:, sl] = (xv * rstd) * g_ref[...]

    return pl.pallas_call(
        body, name="head_rms_fwd", grid=(rows // tr,),
        in_specs=[pl.BlockSpec((tr, w), lambda i: (i, 0)), pl.BlockSpec((1, HEAD_DIM), lambda i: (0, 0))],
        out_specs=pl.BlockSpec((tr, w), lambda i: (i, 0)),
        out_shape=jax.ShapeDtypeStruct((rows, w), F32),
        compiler_params=pltpu.CompilerParams(dimension_semantics=("parallel",)),
    )(x, g)


def _head_rms_bwd_call(x, g, dy):
    rows, w = x.shape
    tr = _row_tile(rows, w, budget=1024 * 1024)

    def body(x_ref, g_ref, dy_ref, dx_ref, dg_ref):
        @pl.when(pl.program_id(0) == 0)
        def _():
            dg_ref[...] = jnp.zeros_like(dg_ref)

        dg = jnp.zeros((1, HEAD_DIM), F32)
        for sl in _heads(w):
            xv = x_ref[:, sl]
            dyv = dy_ref[:, sl]
            rstd = lax.rsqrt(jnp.mean(xv * xv, axis=1, keepdims=True) + RMS_EPS)
            xhat = xv * rstd
            dxhat = dyv * g_ref[...]
            dx_ref[:, sl] = rstd * (dxhat - xhat * jnp.mean(dxhat * xhat, axis=1, keepdims=True))
            dg = dg + jnp.sum(dyv * xhat, axis=0, keepdims=True)
        dg_ref[...] += dg

    return pl.pallas_call(
        body, name="head_rms_bwd", grid=(rows // tr,),
        in_specs=[pl.BlockSpec((tr, w), lambda i: (i, 0)), pl.BlockSpec((1, HEAD_DIM), lambda i: (0, 0)),
                  pl.BlockSpec((tr, w), lambda i: (i, 0))],
        out_specs=[pl.BlockSpec((tr, w), lambda i: (i, 0)), pl.BlockSpec((1, HEAD_DIM), lambda i: (0, 0))],
        out_shape=[jax.ShapeDtypeStruct((rows, w), F32), jax.ShapeDtypeStruct((1, HEAD_DIM), F32)],
        compiler_params=pltpu.CompilerParams(dimension_semantics=("arbitrary",)),
    )(x, g, dy)


@jax.custom_vjp
def head_rms(x, g):
    return _head_rms_fwd_call(x, g)


head_rms.defvjp(lambda x, g: (_head_rms_fwd_call(x, g), (x, g)),
                lambda res, dy: tuple(_head_rms_bwd_call(res[0], res[1], dy)))


def _head_l2_call(x, dy=None):
    rows, w = x.shape
    tr = _row_tile(rows, w, budget=1024 * 1024)
    ops = (x,) if dy is None else (x, dy)

    def body(*refs):
        o_ref = refs[-1]
        for sl in _heads(w):
            xv = refs[0][:, sl]
            nrm = jnp.sqrt(jnp.sum(xv * xv, axis=1, keepdims=True))
            live = nrm > L2_FLOOR
            inv = 1.0 / jnp.maximum(nrm, L2_FLOOR)
            y = xv * inv
            if dy is None:
                o_ref[:, sl] = y
            else:
                dyv = refs[1][:, sl]
                proj = jnp.where(live, jnp.sum(dyv * y, axis=1, keepdims=True), 0.0)
                o_ref[:, sl] = (dyv - y * proj) * inv

    return pl.pallas_call(
        body, name="head_l2_fwd" if dy is None else "head_l2_bwd", grid=(rows // tr,),
        in_specs=[pl.BlockSpec((tr, w), lambda i: (i, 0))] * len(ops),
        out_specs=pl.BlockSpec((tr, w), lambda i: (i, 0)),
        out_shape=jax.ShapeDtypeStruct((rows, w), F32),
        compiler_params=pltpu.CompilerParams(dimension_semantics=("parallel",)),
    )(*ops)


@jax.custom_vjp
def head_l2norm(x):
    return _head_l2_call(x)


head_l2norm.defvjp(lambda x: (_head_l2_call(x), x), lambda x, dy: (_head_l2_call(x, dy),))


def _gn_fwd_call(y, r, kf, v, gw, gb, rk):
    rows, w = y.shape
    tr = _row_tile(rows, w, budget=512 * 1024)

    def body(y_ref, r_ref, kf_ref, v_ref, gw_ref, gb_ref, rk_ref, o_ref):
        for sl in _heads(w):
            yv = y_ref[:, sl]
            yc = yv - jnp.mean(yv, axis=1, keepdims=True)
            rstd = lax.rsqrt(jnp.mean(yc * yc, axis=1, keepdims=True) + GN_EPS)
            s = jnp.sum(r_ref[:, sl] * kf_ref[:, sl] * rk_ref[:, sl], axis=1, keepdims=True)
            o_ref[:, sl] = (yc * rstd) * gw_ref[:, sl] + gb_ref[:, sl] + s * v_ref[:, sl]

    tok = pl.BlockSpec((tr, w), lambda i: (i, 0))
    par = pl.BlockSpec((1, w), lambda i: (0, 0))
    return pl.pallas_call(
        body, name="gn_bonus_fwd", grid=(rows // tr,),
        in_specs=[tok] * 4 + [par] * 3, out_specs=tok,
        out_shape=jax.ShapeDtypeStruct((rows, w), F32),
        compiler_params=pltpu.CompilerParams(dimension_semantics=("parallel",)),
    )(y, r, kf, v, gw, gb, rk)


def _gn_bwd_call(y, r, kf, v, gw, gb, rk, do):
    rows, w = y.shape
    tr = _row_tile(rows, w, budget=512 * 1024)

    def body(y_ref, r_ref, kf_ref, v_ref, gw_ref, rk_ref, do_ref,
             dy_ref, dr_ref, dkf_ref, dv_ref, dgw_ref, dgb_ref, drk_ref):
        @pl.when(pl.program_id(0) == 0)
        def _():
            dgw_ref[...] = jnp.zeros_like(dgw_ref)
            dgb_ref[...] = jnp.zeros_like(dgb_ref)
            drk_ref[...] = jnp.zeros_like(drk_ref)

        for sl in _heads(w):
            yv, rv, kv, vv, dov = y_ref[:, sl], r_ref[:, sl], kf_ref[:, sl], v_ref[:, sl], do_ref[:, sl]
            yc = yv - jnp.mean(yv, axis=1, keepdims=True)
            rstd = lax.rsqrt(jnp.mean(yc * yc, axis=1, keepdims=True) + GN_EPS)
            yhat = yc * rstd
            dyhat = dov * gw_ref[:, sl]
            dy_ref[:, sl] = rstd * (dyhat - jnp.mean(dyhat, axis=1, keepdims=True)
                                    - yhat * jnp.mean(dyhat * yhat, axis=1, keepdims=True))
            rkv = rk_ref[:, sl]
            s = jnp.sum(rv * kv * rkv, axis=1, keepdims=True)
            ds = jnp.sum(dov * vv, axis=1, keepdims=True)
            dv_ref[:, sl] = s * dov
            dr_ref[:, sl] = ds * kv * rkv
            dkf_ref[:, sl] = ds * rv * rkv
            dgw_ref[:, sl] += jnp.sum(dov * yhat, axis=0, keepdims=True)
            dgb_ref[:, sl] += jnp.sum(dov, axis=0, keepdims=True)
            drk_ref[:, sl] += jnp.sum(ds * rv * kv, axis=0, keepdims=True)

    tok = pl.BlockSpec((tr, w), lambda i: (i, 0))
    par = pl.BlockSpec((1, w), lambda i: (0, 0))
    tshape = jax.ShapeDtypeStruct((rows, w), F32)
    pshape = jax.ShapeDtypeStruct((1, w), F32)
    return pl.pallas_call(
        body, name="gn_bonus_bwd", grid=(rows // tr,),
        in_specs=[tok] * 4 + [par] * 2 + [tok], out_specs=[tok] * 4 + [par] * 3,
        out_shape=[tshape] * 4 + [pshape] * 3,
        compiler_params=pltpu.CompilerParams(dimension_semantics=("arbitrary",)),
    )(y, r, kf, v, gw, rk, do)


@jax.custom_vjp
def gn_bonus(y, r, kf, v, gw, gb, rk):
    return _gn_fwd_call(y, r, kf, v, gw, gb, rk)


def _gn_bwd(res, do):
    y, r, kf, v, gw, gb, rk = res
    dy, dr, dkf, dv, dgw, dgb, drk = _gn_bwd_call(y, r, kf, v, gw, gb, rk, do)
    return dy, dr, dkf, dv, dgw, dgb, drk


gn_bonus.defvjp(lambda *a: (_gn_fwd_call(*a), a), _gn_bwd)


def _tri_masks(c):
    i = lax.broadcasted_iota(jnp.int32, (c, c), 0)
    j = lax.broadcasted_iota(jnp.int32, (c, c), 1)
    return i > j, i >= j, i == j


def _wkv_chunk_common(r, lw, k, a, b):
    c = r.shape[0]
    strict, incl, diag = _tri_masks(c)
    tri = jnp.where(incl, 1.0, 0.0).astype(BF16)
    lc = sum(_dg(tri, part, False, False) for part in _split(lw, 3))
    lend = lc[c - 1:c, :]
    rt = r * jnp.exp(lc)
    at = a * jnp.exp(lc - lw)
    pinv = jnp.exp(-lc)
    kt = k * pinv
    bt = b * pinv
    e = jnp.exp(lend - lc)
    ktp = k * e
    btp = b * e
    zero = jnp.zeros((c, c), F32)
    a_ab = jnp.where(strict, _mm(at, bt, tb=True), zero)
    a_ak = jnp.where(strict, _mm(at, kt, tb=True), zero)
    a_rb = jnp.where(incl, _mm(rt, bt, tb=True), zero)
    a_rk = jnp.where(incl, _mm(rt, kt, tb=True), zero)
    eye = jnp.where(diag, 1.0, 0.0).astype(F32)
    t = eye + a_ab
    xp = a_ab
    n = 2
    while n < c:
        xp = _mm(xp, xp)
        t = t + _mm(t, xp)
        n *= 2
    pend_col = jnp.sum(eye * jnp.exp(lend), axis=1, keepdims=True)
    return dict(rt=rt, at=at, kt=kt, bt=bt, ktp=ktp, btp=btp, a_ak=a_ak, a_rb=a_rb, a_rk=a_rk, t=t,
                pend_col=pend_col, lend=lend, lc=lc, strict=strict, incl=incl, tri=tri)


def _wkv_fwd_call(r, lw, k, v, a, b):
    tokens, width = r.shape
    c = WKV_CHUNK
    nc = tokens // c
    hd = HEAD_DIM

    def body(r_ref, lw_ref, k_ref, v_ref, a_ref, b_ref, y_ref, s_ref, st):
        @pl.when(pl.program_id(1) == 0)
        def _():
            st[...] = jnp.zeros_like(st)

        s_ref[0] = st[...]
        for hh in range(2):
            sl = slice(hh * hd, (hh + 1) * hd)
            rv, lwv, kv, vv, av, bv = (ref[:, sl] for ref in (r_ref, lw_ref, k_ref, v_ref, a_ref, b_ref))
            s0 = st[:, sl]
            q = _wkv_chunk_common(rv, lwv, kv, av, bv)
            w1 = _mm(q["at"], s0) + _mm(q["a_ak"], vv)
            u = _mm(q["t"], w1)
            y_ref[:, sl] = _mm(q["rt"], s0) + _mm(q["a_rb"], u) + _mm(q["a_rk"], vv)
            st[:, sl] = q["pend_col"] * s0 + _mm(q["btp"], u, ta=True) + _mm(q["ktp"], vv, ta=True)

    tok = pl.BlockSpec((c, 2 * hd), lambda hp, ci: (ci, hp))
    return pl.pallas_call(
        body, name="wkv_fwd", grid=(width // (2 * hd), nc),
        in_specs=[tok] * 6,
        out_specs=[tok, pl.BlockSpec((1, hd, 2 * hd), lambda hp, ci: (ci, 0, hp))],
        out_shape=[jax.ShapeDtypeStruct((tokens, width), F32), jax.ShapeDtypeStruct((nc, hd, width), F32)],
        scratch_shapes=[pltpu.VMEM((hd, 2 * hd), F32)],
        compiler_params=pltpu.CompilerParams(dimension_semantics=("parallel", "arbitrary")),
    )(r, lw, k, v, a, b)


def _wkv_bwd_call(r, lw, k, v, a, b, s, dy):
    tokens, width = r.shape
    c = WKV_CHUNK
    nc = tokens // c
    hd = HEAD_DIM

    def body(r_ref, lw_ref, k_ref, v_ref, a_ref, b_ref, s_ref, dy_ref,
             dr_ref, dlw_ref, dk_ref, dv_ref, da_ref, db_ref, dst):
        @pl.when(pl.program_id(1) == 0)
        def _():
            dst[...] = jnp.zeros_like(dst)

        for hh in range(2):
            sl = slice(hh * hd, (hh + 1) * hd)
            rv, lwv, kv, vv, av, bv, dyv = (ref[:, sl] for ref in (r_ref, lw_ref, k_ref, v_ref, a_ref, b_ref, dy_ref))
            s0 = s_ref[0, :, sl]
            dsc = dst[:, sl]
            q = _wkv_chunk_common(rv, lwv, kv, av, bv)
            rt, at, kt, bt, ktp, btp, t = (q[n] for n in ("rt", "at", "kt", "bt", "ktp", "btp", "t"))
            w1 = _mm(at, s0) + _mm(q["a_ak"], vv)
            u = _mm(t, w1)
            du = _mm(q["a_rb"], dyv, ta=True) + _mm(btp, dsc)
            dw1 = _mm(t, du, ta=True)
            dv_ref[:, sl] = _mm(q["a_rk"], dyv, ta=True) + _mm(ktp, dsc) + _mm(q["a_ak"], dw1, ta=True)
            zero = jnp.zeros((c, c), F32)
            da_ab = jnp.where(q["strict"], _mm(dw1, u, tb=True), zero)
            da_ak = jnp.where(q["strict"], _mm(dw1, vv, tb=True), zero)
            da_rb = jnp.where(q["incl"], _mm(dyv, u, tb=True), zero)
            da_rk = jnp.where(q["incl"], _mm(dyv, vv, tb=True), zero)
            d_rt = _mm(dyv, s0, tb=True) + _mm(da_rb, bt) + _mm(da_rk, kt)
            d_at = _mm(dw1, s0, tb=True) + _mm(da_ab, bt) + _mm(da_ak, kt)
            d_bt = _mm(da_ab, at, ta=True) + _mm(da_rb, rt, ta=True)
            d_kt = _mm(da_ak, at, ta=True) + _mm(da_rk, rt, ta=True)
            d_btp = _mm(u, dsc, tb=True)
            d_ktp = _mm(vv, dsc, tb=True)
            prod = dsc * s0
            ones = jnp.ones((8, hd), BF16)
            dpend = sum(_dg(ones, part, False, True) for part in _split(prod, 3))[0:1, :] * jnp.exp(q["lend"])
            dst[:, sl] = q["pend_col"] * dsc + _mm(rt, dyv, ta=True) + _mm(at, dw1, ta=True)
            lc_e = d_ktp * ktp + d_btp * btp
            dlend = jnp.sum(lc_e, axis=0, keepdims=True) + dpend
            last = lax.broadcasted_iota(jnp.int32, (c, hd), 0) == c - 1
            dlc = d_rt * rt - d_kt * kt - d_bt * bt - lc_e + jnp.where(last, dlend, 0.0)
            dlp = d_at * at
            dlw_ref[:, sl] = sum(_dg(q["tri"], part, True, False) for part in _split(dlc + dlp, 3)) - dlp
            lc = q["lc"]
            pinv = jnp.exp(-lc)
            e = jnp.exp(q["lend"] - lc)
            dr_ref[:, sl] = d_rt * jnp.exp(lc)
            da_ref[:, sl] = d_at * jnp.exp(lc - lwv)
            dk_ref[:, sl] = d_kt * pinv + d_ktp * e
            db_ref[:, sl] = d_bt * pinv + d_btp * e

    tok = pl.BlockSpec((c, 2 * hd), lambda hp, ci: (nc - 1 - ci, hp))
    tshape = jax.ShapeDtypeStruct((tokens, width), F32)
    return pl.pallas_call(
        body, name="wkv_bwd", grid=(width // (2 * hd), nc),
        in_specs=[tok] * 6 + [pl.BlockSpec((1, hd, 2 * hd), lambda hp, ci: (nc - 1 - ci, 0, hp)), tok],
        out_specs=[tok] * 6, out_shape=[tshape] * 6,
        scratch_shapes=[pltpu.VMEM((hd, 2 * hd), F32)],
        compiler_params=pltpu.CompilerParams(dimension_semantics=("parallel", "arbitrary")),
    )(r, lw, k, v, a, b, s, dy)


@jax.custom_vjp
def wkv7(r, lw, k, v, a, b):
    return _wkv_fwd_call(r, lw, k, v, a, b)[0]


def _wkv7_fwd(r, lw, k, v, a, b):
    y, s = _wkv_fwd_call(r, lw, k, v, a, b)
    return y, (r, lw, k, v, a, b, s)


wkv7.defvjp(_wkv7_fwd, lambda res, dy: tuple(_wkv_bwd_call(*res, dy)))


def _fox_layouts(cum):
    tokens, heads = cum.shape
    t = ATTN_BLOCK
    cq = cum.reshape(tokens, heads // 2, 2).transpose(1, 0, 2)
    ck = cum.T.reshape(heads // 2, 2, tokens // t, t).transpose(0, 2, 1, 3)
    return cq, ck


def _fox_fwd_call(q, k, v, cq, ck):
    tokens, width = q.shape
    t = ATTN_BLOCK
    nb = tokens // t
    hd = HEAD_DIM
    npair = width // (2 * hd)

    def body(q_ref, k_ref, v_ref, cq_ref, ck_ref, o_ref, lse_ref):
        i = pl.program_id(1)
        row = i * t + lax.broadcasted_iota(jnp.int32, (t, t), 0)
        col = lax.broadcasted_iota(jnp.int32, (t, t), 1)
        qs = [q_ref[:, hh * hd:(hh + 1) * hd].astype(BF16) for hh in range(2)]
        cqs = [cq_ref[0, :, hh:hh + 1] for hh in range(2)]

        def step(j, carry):
            off = pl.multiple_of(j * t, t)
            ckj = ck_ref[0, j]
            keep = row >= j * t + col
            out = []
            for hh in range(2):
                m, l, acc = carry[hh]
                sl = slice(hh * hd, (hh + 1) * hd)
                kj = k_ref[pl.ds(off, t), sl].astype(BF16)
                vj = v_ref[pl.ds(off, t), sl].astype(BF16)
                s = _dg(qs[hh], kj, False, True) + (cqs[hh] - ckj[hh:hh + 1, :])
                s = jnp.where(keep, s, NEG_BIG)
                m_new = jnp.maximum(m, jnp.max(s, axis=1, keepdims=True))
                alpha = jnp.exp(m - m_new)
                p = jnp.exp(s - m_new)
                l = alpha * l + jnp.sum(p, axis=1, keepdims=True)
                acc = alpha * acc + _dg(p.astype(BF16), vj, False, False)
                out.append((m_new, l, acc))
            return tuple(out)

        init = tuple((jnp.full((t, 1), NEG_BIG, F32), jnp.zeros((t, 1), F32), jnp.zeros((t, hd), F32)) for _ in range(2))
        res = lax.fori_loop(0, i + 1, step, init)
        for hh in range(2):
            m, l, acc = res[hh]
            o_ref[:, hh * hd:(hh + 1) * hd] = acc / l
            lse_ref[0, :, hh:hh + 1] = m + jnp.log(l)

    blk = pl.BlockSpec((t, 2 * hd), lambda hp, i: (i, hp))
    full = pl.BlockSpec((tokens, 2 * hd), lambda hp, i: (0, hp))
    cq_spec = pl.BlockSpec((1, t, 2), lambda hp, i: (hp, i, 0))
    ck_spec = pl.BlockSpec((1, nb, 2, t), lambda hp, i: (hp, 0, 0, 0))
    return pl.pallas_call(
        body, name="fox_fwd", grid=(npair, nb),
        in_specs=[blk, full, full, cq_spec, ck_spec],
        out_specs=[blk, cq_spec],
        out_shape=[jax.ShapeDtypeStruct((tokens, width), F32), jax.ShapeDtypeStruct((npair, tokens, 2), F32)],
        compiler_params=pltpu.CompilerParams(dimension_semantics=("parallel", "arbitrary")),
    )(q, k, v, cq, ck)


def _fox_bwd_call(q, k, v, cq, ck, o, lse, do):
    tokens, width = q.shape
    t = ATTN_BLOCK
    nb = tokens // t
    hd = HEAD_DIM
    npair = width // (2 * hd)

    def body(q_ref, k_ref, v_ref, cq_ref, ck_ref, o_ref, lse_ref, do_ref, dq_ref, dk_ref, dv_ref, dck_ref, dcq_ref):
        i = pl.program_id(1)

        @pl.when(i == 0)
        def _():
            dk_ref[...] = jnp.zeros_like(dk_ref)
            dv_ref[...] = jnp.zeros_like(dv_ref)
            dck_ref[...] = jnp.zeros_like(dck_ref)

        row = i * t + lax.broadcasted_iota(jnp.int32, (t, t), 0)
        col = lax.broadcasted_iota(jnp.int32, (t, t), 1)
        sls = [slice(hh * hd, (hh + 1) * hd) for hh in range(2)]
        qs = [q_ref[:, sl].astype(BF16) for sl in sls]
        dos = [do_ref[:, sl].astype(BF16) for sl in sls]
        deltas = [jnp.sum(dos[hh].astype(F32) * o_ref[:, sls[hh]], axis=1, keepdims=True) for hh in range(2)]
        bias = [cq_ref[0, :, hh:hh + 1] - lse_ref[0, :, hh:hh + 1] for hh in range(2)]

        def step(j, carry):
            off = pl.multiple_of(j * t, t)
            ckj = ck_ref[0, j]
            keep = row >= j * t + col
            out = []
            for hh in range(2):
                sl = sls[hh]
                kj = k_ref[pl.ds(off, t), sl].astype(BF16)
                vj = v_ref[pl.ds(off, t), sl].astype(BF16)
                s = _dg(qs[hh], kj, False, True) + (bias[hh] - ckj[hh:hh + 1, :])
                p = jnp.exp(jnp.where(keep, s, NEG_BIG))
                dp = _dg(dos[hh], vj, False, True)
                ds = p * (dp - deltas[hh])
                dsb = ds.astype(BF16)
                dq, rowsum = carry[hh]
                out.append((dq + _dg(dsb, kj, False, False), rowsum + jnp.sum(ds, axis=1, keepdims=True)))
                dk_ref[pl.ds(off, t), sl] += _dg(dsb, qs[hh], True, False)
                dv_ref[pl.ds(off, t), sl] += _dg(p.astype(BF16), dos[hh], True, False)
                dck_ref[0, j, hh:hh + 1, :] -= jnp.sum(ds, axis=0, keepdims=True)
            return tuple(out)

        res = lax.fori_loop(0, i + 1, step, tuple((jnp.zeros((t, hd), F32), jnp.zeros((t, 1), F32)) for _ in range(2)))
        for hh in range(2):
            dq_ref[:, sls[hh]] = res[hh][0]
            dcq_ref[0, :, hh:hh + 1] = res[hh][1]

    blk = pl.BlockSpec((t, 2 * hd), lambda hp, i: (i, hp))
    full = pl.BlockSpec((tokens, 2 * hd), lambda hp, i: (0, hp))
    cq_spec = pl.BlockSpec((1, t, 2), lambda hp, i: (hp, i, 0))
    ck_spec = pl.BlockSpec((1, nb, 2, t), lambda hp, i: (hp, 0, 0, 0))
    tshape = jax.ShapeDtypeStruct((tokens, width), F32)
    return pl.pallas_call(
        body, name="fox_bwd", grid=(npair, nb),
        in_specs=[blk, full, full, cq_spec, ck_spec, blk, cq_spec, blk],
        out_specs=[blk, full, full, ck_spec, cq_spec],
        out_shape=[tshape, tshape, tshape, jax.ShapeDtypeStruct((npair, nb, 2, t), F32),
                   jax.ShapeDtypeStruct((npair, tokens, 2), F32)],
        compiler_params=pltpu.CompilerParams(dimension_semantics=("parallel", "arbitrary")),
    )(q, k, v, cq, ck, o, lse, do)


@jax.custom_vjp
def fox_attention(q, k, v, cum):
    return _fox_fwd_call(q, k, v, *_fox_layouts(cum))[0]


def _fox_fwd(q, k, v, cum):
    cq, ck = _fox_layouts(cum)
    o, lse = _fox_fwd_call(q, k, v, cq, ck)
    return o, (q, k, v, cq, ck, o, lse)


def _fox_bwd(res, do):
    q, k, v, cq, ck, o, lse = res
    dq, dk, dv, dck, dcq = _fox_bwd_call(q, k, v, cq, ck, o, lse, do)
    npair, nb, _, t = dck.shape
    dcum = dck.transpose(0, 2, 1, 3).reshape(2 * npair, nb * t).T + dcq.transpose(1, 0, 2).reshape(nb * t, 2 * npair)
    return dq, dk, dv, dcum


fox_attention.defvjp(_fox_fwd, _fox_bwd)


def _loss_call(y, target):
    rows, d = y.shape
    tr = _row_tile(rows, d)

    def body(y_ref, t_ref, loss_ref, dy_ref):
        @pl.when(pl.program_id(0) == 0)
        def _():
            loss_ref[...] = jnp.zeros_like(loss_ref)

        diff = y_ref[...] - t_ref[...]
        dy_ref[...] = diff * (1.0 / d)
        loss_ref[...] += (0.5 / d) * jnp.sum(jnp.sum(diff * diff, axis=1, keepdims=True), axis=0, keepdims=True)

    return pl.pallas_call(
        body, name="loss", grid=(rows // tr,),
        in_specs=[pl.BlockSpec((tr, d), lambda i: (i, 0))] * 2,
        out_specs=[pl.BlockSpec((1, 1), lambda i: (0, 0)), pl.BlockSpec((tr, d), lambda i: (i, 0))],
        out_shape=[jax.ShapeDtypeStruct((1, 1), F32), jax.ShapeDtypeStruct((rows, d), F32)],
        compiler_params=pltpu.CompilerParams(dimension_semantics=("arbitrary",)),
    )(y, target)


def _adamw_call(w, g, m, v):
    rows, cols = w.shape
    tr = _row_tile(rows, cols, budget=1024 * 1024)
    c1 = 1.0 / (1.0 - ADAM_B1 ** ADAM_STEP)
    c2 = 1.0 / (1.0 - ADAM_B2 ** ADAM_STEP)

    def body(w_ref, g_ref, m_ref, v_ref, d_ref, nm_ref, nv_ref):
        gv = g_ref[...]
        nm = ADAM_B1 * m_ref[...] + (1.0 - ADAM_B1) * gv
        nv = ADAM_B2 * v_ref[...] + (1.0 - ADAM_B2) * (gv * gv)
        nm_ref[...] = nm
        nv_ref[...] = nv
        d_ref[...] = -ADAM_LR * ((nm * c1) / (jnp.sqrt(nv * c2) + ADAM_EPS) + ADAM_WD * w_ref[...])

    spec = pl.BlockSpec((tr, cols), lambda i: (i, 0))
    shape = jax.ShapeDtypeStruct((rows, cols), F32)
    return pl.pallas_call(
        body, name="adamw", grid=(rows // tr,),
        in_specs=[spec] * 4, out_specs=[spec] * 3, out_shape=[shape] * 3,
        compiler_params=pltpu.CompilerParams(dimension_semantics=("parallel",)),
    )(w, g, m, v)


def _my_place():
    return lax.axis_index("x"), lax.axis_index("y"), lax.axis_index("c")


def _place_index(px, py, pc):
    return 4 * px + 2 * py + pc


HBM_SPEC = pl.BlockSpec(memory_space=pltpu.HBM)


def _all_gather_call(block):
    def body(x_ref, out_ref, send_sems, recv_sems, local_sem):
        x, y, c = _my_place()
        me, sibling = (x, y, c), (x, y, 1 - c)
        chips = [(1 - x, y), (x, 1 - y), (1 - x, 1 - y)]

        def slot(px, py, pc):
            return out_ref.at[_place_index(px, py, pc)]

        def copy(k, blk, to, src=None):
            return pltpu.make_async_remote_copy(
                src_ref=slot(*blk) if src is None else src, dst_ref=slot(*blk),
                send_sem=send_sems.at[k], recv_sem=recv_sems.at[k],
                device_id=to, device_id_type=pl.DeviceIdType.MESH)

        mine = pltpu.make_async_copy(x_ref, slot(*me), local_sem)
        mine.start()
        first = [copy(0, me, sibling, src=x_ref)]
        first += [copy(1 + j, me, (*chip, c), src=x_ref) for j, chip in enumerate(chips)]
        for cp in first:
            cp.start()
        passed = [copy(4 + j, (*chip, c), sibling) for j, chip in enumerate(chips)]
        for j, chip in enumerate(chips):
            copy(1 + j, (*chip, c), me).wait_recv()
            passed[j].start()
        copy(0, sibling, me).wait_recv()
        for j, chip in enumerate(chips):
            copy(4 + j, (*chip, 1 - c), me).wait_recv()
        for cp in first + passed:
            cp.wait_send()
        mine.wait()

    return pl.pallas_call(
        body, name="all_gather",
        out_shape=jax.ShapeDtypeStruct((N_DEV,) + block.shape, block.dtype),
        in_specs=[HBM_SPEC], out_specs=HBM_SPEC,
        scratch_shapes=[pltpu.SemaphoreType.DMA((7,)), pltpu.SemaphoreType.DMA((7,)), pltpu.SemaphoreType.DMA],
    )(block)


def _exchange_call(blocks):
    def body(g_ref, out_ref, send_sems, recv_sems, local_sem):
        x, y, c = _my_place()
        mine_idx = _place_index(x, y, c)
        mine = pltpu.make_async_copy(g_ref.at[mine_idx], out_ref.at[mine_idx], local_sem)
        mine.start()
        copies, arrivals = [], []
        for k in range(1, N_DEV):
            peer = (x ^ (k >> 2), y ^ ((k >> 1) & 1), c ^ (k & 1))
            peer_idx = _place_index(*peer)
            copies.append(pltpu.make_async_remote_copy(
                src_ref=g_ref.at[peer_idx], dst_ref=out_ref.at[mine_idx],
                send_sem=send_sems.at[k - 1], recv_sem=recv_sems.at[k - 1],
                device_id=peer, device_id_type=pl.DeviceIdType.MESH))
            arrivals.append(pltpu.make_async_remote_copy(
                src_ref=g_ref.at[mine_idx], dst_ref=out_ref.at[peer_idx],
                send_sem=send_sems.at[k - 1], recv_sem=recv_sems.at[k - 1],
                device_id=peer, device_id_type=pl.DeviceIdType.MESH))
        for cp in copies:
            cp.start()
        for cp in arrivals:
            cp.wait_recv()
        for cp in copies:
            cp.wait_send()
        mine.wait()

    return pl.pallas_call(
        body, name="grad_exchange",
        out_shape=jax.ShapeDtypeStruct(blocks.shape, blocks.dtype),
        in_specs=[HBM_SPEC], out_specs=HBM_SPEC,
        scratch_shapes=[pltpu.SemaphoreType.DMA((7,)), pltpu.SemaphoreType.DMA((7,)), pltpu.SemaphoreType.DMA],
    )(blocks)


def _sum_slots_call(slots):
    _, rows, cols = slots.shape
    tr = _row_tile(rows, cols, budget=512 * 1024)

    def body(s_ref, o_ref):
        acc = s_ref[0].astype(F32)
        for j in range(1, N_DEV):
            acc = acc + s_ref[j].astype(F32)
        o_ref[...] = acc

    return pl.pallas_call(
        body, name="sum_slots", grid=(rows // tr,),
        in_specs=[pl.BlockSpec((N_DEV, tr, cols), lambda i: (0, i, 0))],
        out_specs=pl.BlockSpec((tr, cols), lambda i: (i, 0)),
        out_shape=jax.ShapeDtypeStruct((rows, cols), F32),
        compiler_params=pltpu.CompilerParams(dimension_semantics=("parallel",)),
    )(slots)


def _make_gather(payload_dtype):
    @jax.custom_vjp
    def gather(block):
        return _all_gather_call(block.astype(payload_dtype))

    def bwd(_, dall):
        return (_sum_slots_call(_exchange_call(dall)),)

    gather.defvjp(lambda block: (_all_gather_call(block.astype(payload_dtype)), None), bwd)
    return gather


gather_bf16 = _make_gather(BF16)
gather_f32 = _make_gather(F32)


def _pack(vectors, width):
    flat = jnp.concatenate([v.reshape(-1) for v in vectors])
    return jnp.pad(flat, (0, width - flat.shape[0])).reshape(width // 128, 128)


def _unpack(packed, like):
    flat = packed.reshape(-1)
    out, at = [], 0
    for v in like:
        out.append(flat[at:at + v.size].reshape(v.shape))
        at += v.size
    return tuple(out)


@jax.custom_vjp
def replicated(params):
    return params


def _replicated_bwd(like, grads):
    n = sum(v.size for v in like)
    width = -(-n // 1024) * 1024
    total = _sum_slots_call(_all_gather_call(_pack(grads, width)))
    return (_unpack(total, like),)


replicated.defvjp(lambda params: (params, params), _replicated_bwd)


def _cols_from_slots(slots):
    n, rows, cols = slots.shape
    return slots.transpose(1, 0, 2).reshape(rows, n * cols)


def _forward(sharded, small, x, dims):
    n_meta, seq, lp = dims["n_meta"], dims["seq"], dims["lp"]
    (meta_s, w_in_s, w2_s, a2_s, g2_s, w_a_s, w_b_s, w_o_s, w_gu_s, w_dn_s) = sharded
    (n1, mu, w0, a0, k_k, k_a, r_k, gn_w, gn_b, q_g, k_g, f_bias, n2) = replicated(small)

    meta = _cols_from_slots(gather_f32(meta_s))
    w_in = _cols_from_slots(gather_bf16(w_in_s))
    w2 = _cols_from_slots(gather_bf16(w2_s))
    a2 = _cols_from_slots(gather_bf16(a2_s))
    g2 = _cols_from_slots(gather_bf16(g2_s))
    w_a = _cols_from_slots(gather_bf16(w_a_s))
    w_b = _cols_from_slots(gather_bf16(w_b_s))
    w_o = gather_bf16(w_o_s).reshape(-1, w_o_s.shape[1])
    w_gu = _cols_from_slots(gather_bf16(w_gu_s))
    w_dn = gather_bf16(w_dn_s).reshape(-1, w_dn_s.shape[1])

    d = x.shape[1]
    rw, fw = w_a.shape[0], w_b.shape[0]
    dl, al, gl = w2.shape[0], a2.shape[0], g2.shape[0]
    rcols = 3 * rw + dl + al + gl
    fheads = fw // HEAD_DIM
    fcols = 3 * fw + fheads
    pad128 = lambda n: -(-n // 128) * 128
    rpad, fpad = pad128(rcols), pad128(fcols)
    padc = lambda w, n: jnp.pad(w, ((0, 0), (0, n - w.shape[1])))
    w_cat = jnp.concatenate([padc(w_in[:, :rcols], rpad), padc(w_in[:, rcols:rcols + fcols], fpad),
                             w_in[:, rcols + fcols:]], axis=1)

    h0 = jnp.concatenate([meta, x, jnp.zeros((lp - n_meta - seq, d), F32)], axis=0)
    proj = dense(rmsnorm(h0, n1), w_cat)
    z_r, z_f, z_g = proj[:, :rcols], proj[:, rpad:rpad + fcols], proj[:, rpad + fpad:]

    z_prev = jnp.pad(z_r, ((1, 0), (0, 0)))[:-1]
    z = z_r + bmul(z_prev - z_r, mu)
    r, k, v = z[:, :rw], z[:, rw:2 * rw], z[:, 2 * rw:3 * rw]
    wd, ad, gd = z[:, 3 * rw:3 * rw + dl], z[:, 3 * rw + dl:3 * rw + dl + al], z[:, 3 * rw + dl + al:]
    w_log = -jax.nn.softplus(-badd(dense(jnp.tanh(wd), w2), w0)) - 0.5
    lw = -jnp.exp(w_log)
    a_sig = jax.nn.sigmoid(badd(dense(ad, a2), a0))
    g = dense(jax.nn.sigmoid(gd), g2)
    kk = head_l2norm(bmul(k, k_k))
    kf = k * (1.0 + bmul(a_sig - 1.0, k_a))
    y = wkv7(r, lw, kf, v, -kk, kk * a_sig)
    y_a = gn_bonus(y, r, kf, v, gn_w, gn_b, r_k.reshape(1, rw)) * g

    fq, fk, fv, fl = z_f[:, :fw], z_f[:, fw:2 * fw], z_f[:, 2 * fw:3 * fw], z_f[:, 3 * fw:]
    fq = head_rms(fq, q_g) * (HEAD_DIM ** -0.5)
    fk = head_rms(fk, k_g)
    cum = jnp.cumsum(jax.nn.log_sigmoid(badd(fl, f_bias)), axis=0)
    y_b = fox_attention(fq, fk, fv, cum)

    gates = jax.nn.sigmoid(z_g)
    merged = gates[:, :d] * dense(y_a, w_a) + gates[:, d:] * dense(y_b, w_b)
    h1 = h0 + dense(merged, w_o)
    gu = dense(rmsnorm(h1, n2), w_gu)
    dff = w_dn.shape[0]
    return h1 + dense(jax.nn.silu(gu[:, :dff]) * gu[:, dff:], w_dn)


SHARDED = ("meta_tokens", "w_in", "rwkv_w2", "rwkv_a2", "rwkv_g2", "w_branch_a", "w_branch_b", "w_o", "w_gate_up", "w_down")
SMALL = ("norm1_g", "rwkv_mu", "rwkv_w0", "rwkv_a0", "rwkv_k_k", "rwkv_k_a", "rwkv_r_k", "rwkv_gn_w", "rwkv_gn_b",
         "fox_q_norm_g", "fox_k_norm_g", "fox_f_bias", "norm2_g")
WEIGHTS = ("meta_tokens", "norm1_g", "w_in", "rwkv_mu", "rwkv_w0", "rwkv_w2", "rwkv_a0", "rwkv_a2", "rwkv_g2", "rwkv_k_k",
           "rwkv_k_a", "rwkv_r_k", "rwkv_gn_w", "rwkv_gn_b", "fox_q_norm_g", "fox_k_norm_g", "fox_f_bias", "w_branch_a",
           "w_branch_b", "w_o", "norm2_g", "w_gate_up", "w_down")


def _as2d(a):
    return a.reshape(-1, a.shape[-1])


def kernel(x, meta_tokens, norm1_g, w_in, rwkv_mu, rwkv_w0, rwkv_w2, rwkv_a0, rwkv_a2, rwkv_g2, rwkv_k_k, rwkv_k_a, rwkv_r_k, rwkv_gn_w, rwkv_gn_b, fox_q_norm_g, fox_k_norm_g, fox_f_bias, w_branch_a, w_branch_b, w_o, norm2_g, w_gate_up, w_down, loss_target, m_meta_tokens, m_norm1_g, m_w_in, m_rwkv_mu, m_rwkv_w0, m_rwkv_w2, m_rwkv_a0, m_rwkv_a2, m_rwkv_g2, m_rwkv_k_k, m_rwkv_k_a, m_rwkv_r_k, m_rwkv_gn_w, m_rwkv_gn_b, m_fox_q_norm_g, m_fox_k_norm_g, m_fox_f_bias, m_w_branch_a, m_w_branch_b, m_w_o, m_norm2_g, m_w_gate_up, m_w_down, v_meta_tokens, v_norm1_g, v_w_in, v_rwkv_mu, v_rwkv_w0, v_rwkv_w2, v_rwkv_a0, v_rwkv_a2, v_rwkv_g2, v_rwkv_k_k, v_rwkv_k_a, v_rwkv_r_k, v_rwkv_gn_w, v_rwkv_gn_b, v_fox_q_norm_g, v_fox_k_norm_g, v_fox_f_bias, v_w_branch_a, v_w_branch_b, v_w_o, v_norm2_g, v_w_gate_up, v_w_down):
    given = dict(locals())
    w = {n: given[n] for n in WEIGHTS}
    assert rwkv_r_k.shape[-1] == HEAD_DIM
    n_meta, seq = meta_tokens.shape[0], x.shape[1]
    tokens = n_meta + seq
    dims = dict(n_meta=n_meta, seq=seq, lp=-(-tokens // TOKEN_TILE) * TOKEN_TILE)

    sharded = tuple(_as2d(w[n]) for n in SHARDED)
    small = tuple(_as2d(w[n]) for n in SMALL)
    y, vjp = jax.vjp(lambda sh, sm, xs: _forward(sh, sm, xs, dims), sharded, small, x[0])
    loss_part, dy_real = _loss_call(y[n_meta:tokens], loss_target[0])
    dy = jnp.pad(dy_real, ((n_meta, dims["lp"] - tokens), (0, 0)))
    g_sharded, g_small, g_x = vjp(dy)
    loss = lax.psum(loss_part[0, 0], MESH_AXES)

    grads = {n: g.reshape(w[n].shape) for n, g in zip(SHARDED, g_sharded)}
    grads.update({n: g.reshape(w[n].shape) for n, g in zip(SMALL, g_small)})

    delta, new_m, new_v = {}, {}, {}
    for n in SHARDED:
        d_, m_, v_ = _adamw_call(_as2d(w[n]), _as2d(grads[n]), _as2d(given["m_" + n]), _as2d(given["v_" + n]))
        delta[n], new_m[n], new_v[n] = (t.reshape(w[n].shape) for t in (d_, m_, v_))
    n_small = sum(w[n].size for n in SMALL)
    width = -(-n_small // 1024) * 1024
    packs = [_pack([src[n] if p == "" else given[p + n] for n in SMALL], width)
             for p, src in (("", w), ("", grads), ("m_", None), ("v_", None))]
    like = [w[n] for n in SMALL]
    for out, packed in zip((delta, new_m, new_v), _adamw_call(*packs)):
        out.update(dict(zip(SMALL, _unpack(packed, like))))

    return (loss, g_x[None], *[grads[n] for n in WEIGHTS], *[delta[n] for n in WEIGHTS],
            *[new_m[n] for n in WEIGHTS], *[new_v[n] for n in WEIGHTS])
```

```python
import functools

import jax
import jax.numpy as jnp
from jax import lax
from jax.experimental import pallas as pl
from jax.experimental.pallas import tpu as pltpu

F32 = jnp.float32
BF16 = jnp.bfloat16

N_DEV = 8
MESH_AXES = ("x", "y", "c")
HEAD_DIM = 64
TOKEN_TILE = 128
WKV_CHUNK = 64
ATTN_BLOCK = 128
ATTN_BLOCK_BIG = 384
RMS_EPS = 1e-6
GN_EPS = 64e-5
L2_FLOOR = 1e-12
NEG_BIG = -1e30
ADAM_LR, ADAM_B1, ADAM_B2, ADAM_EPS, ADAM_WD, ADAM_STEP = 0.001, 0.9, 0.999, 1e-08, 0.01, 10
VMEM_BYTES_V7X = 64 * 1024 * 1024
VMEM_LIMIT_CAP = 56 * 1024 * 1024
VMEM_LIMIT_FLOOR = 32 * 1024 * 1024


def _vmem_limit(estimate_bytes):
    return int(min(max(estimate_bytes * 5 // 4, VMEM_LIMIT_FLOOR), VMEM_LIMIT_CAP))


def _pick(dim, cands):
    for c in cands:
        if dim % c == 0:
            return c
    return dim


def _row_tile(rows, width, itemsize=4, budget=2 * 1024 * 1024):
    for c in (1408, 1024, 704, 512, 384, 256, 128, 64, 32, 16, 8):
        if rows % c == 0 and c * width * itemsize <= budget:
            return c
    return rows


def _dg(a, b, ta, tb):
    dims = (((0 if ta else 1,), (1 if tb else 0,)), ((), ()))
    return lax.dot_general(a, b, dims, preferred_element_type=F32)


def _split(x, n):
    parts = []
    for _ in range(n):
        h = x.astype(BF16)
        parts.append(h)
        x = x - h.astype(F32)
    return parts


def _mm(a, b, ta=False, tb=False):
    return _dg(a.astype(BF16), b.astype(BF16), ta, tb)


def _matmul(a, b, ta=False, tb=False, out_dtype=F32, name="matmul"):
    if ta:
        kdim, m = a.shape
    else:
        m, kdim = a.shape
    if tb:
        n, k2 = b.shape
    else:
        k2, n = b.shape
    assert kdim == k2, (a.shape, b.shape, ta, tb)
    lane_tiles = (1024, 640, 512, 384, 256, 128)
    sublane_tiles = (1024, 704, 512, 384, 256, 128)
    tm = _pick(m, lane_tiles if ta else sublane_tiles)
    tn = _pick(n, lane_tiles)
    tk = _pick(kdim, sublane_tiles if ta else lane_tiles)
    nk = kdim // tk

    def body(a_ref, b_ref, o_ref, acc):
        kk = pl.program_id(2)

        @pl.when(kk == 0)
        def _():
            acc[...] = jnp.zeros_like(acc)

        acc[...] += _dg(a_ref[...].astype(BF16), b_ref[...].astype(BF16), ta, tb)

        @pl.when(kk == nk - 1)
        def _():
            o_ref[...] = acc[...].astype(o_ref.dtype)

    a_spec = pl.BlockSpec((tk, tm), lambda i, j, k: (k, i)) if ta else pl.BlockSpec((tm, tk), lambda i, j, k: (i, k))
    b_spec = pl.BlockSpec((tn, tk), lambda i, j, k: (j, k)) if tb else pl.BlockSpec((tk, tn), lambda i, j, k: (k, j))
    est = 2 * (tm * tk * a.dtype.itemsize + tk * tn * b.dtype.itemsize + tm * tn * jnp.dtype(out_dtype).itemsize) + tm * tn * 4
    return pl.pallas_call(
        body, name=name,
        grid=(m // tm, n // tn, nk),
        in_specs=[a_spec, b_spec],
        out_specs=pl.BlockSpec((tm, tn), lambda i, j, k: (i, j)),
        out_shape=jax.ShapeDtypeStruct((m, n), out_dtype),
        scratch_shapes=[pltpu.VMEM((tm, tn), F32)],
        compiler_params=pltpu.CompilerParams(dimension_semantics=("parallel", "parallel", "arbitrary"),
                                             vmem_limit_bytes=_vmem_limit(est)),
    )(a, b)


@jax.custom_vjp
def dense(x, w):
    return _matmul(x, w, name="dense_fwd")


def _dense_fwd(x, w):
    return _matmul(x, w, name="dense_fwd"), (x, w)


def _dense_bwd(res, dy):
    x, w = res
    dx = _matmul(dy, w, tb=True, out_dtype=x.dtype, name="dense_dx")
    dw = _matmul(x, dy, ta=True, out_dtype=w.dtype, name="dense_dw")
    return dx, dw


dense.defvjp(_dense_fwd, _dense_bwd)


def _rms_fwd_call(x, g):
    rows, d = x.shape
    tr = _row_tile(rows, d)

    def body(x_ref, g_ref, y_ref):
        xv = x_ref[...]
        rstd = lax.rsqrt(jnp.mean(xv * xv, axis=1, keepdims=True) + RMS_EPS)
        y_ref[...] = (xv * rstd) * g_ref[...]

    return pl.pallas_call(
        body, name="rms_fwd", grid=(rows // tr,),
        in_specs=[pl.BlockSpec((tr, d), lambda i: (i, 0)), pl.BlockSpec((1, d), lambda i: (0, 0))],
        out_specs=pl.BlockSpec((tr, d), lambda i: (i, 0)),
        out_shape=jax.ShapeDtypeStruct((rows, d), F32),
        compiler_params=pltpu.CompilerParams(dimension_semantics=("parallel",)),
    )(x, g)


def _rms_bwd_call(x, g, dy):
    rows, d = x.shape
    tr = _row_tile(rows, d)

    def body(x_ref, g_ref, dy_ref, dx_ref, dg_ref):
        @pl.when(pl.program_id(0) == 0)
        def _():
            dg_ref[...] = jnp.zeros_like(dg_ref)

        xv = x_ref[...]
        dyv = dy_ref[...]
        rstd = lax.rsqrt(jnp.mean(xv * xv, axis=1, keepdims=True) + RMS_EPS)
        xhat = xv * rstd
        dxhat = dyv * g_ref[...]
        dx_ref[...] = rstd * (dxhat - xhat * jnp.mean(dxhat * xhat, axis=1, keepdims=True))
        dg_ref[...] += jnp.sum(dyv * xhat, axis=0, keepdims=True)

    return pl.pallas_call(
        body, name="rms_bwd", grid=(rows // tr,),
        in_specs=[pl.BlockSpec((tr, d), lambda i: (i, 0)), pl.BlockSpec((1, d), lambda i: (0, 0)),
                  pl.BlockSpec((tr, d), lambda i: (i, 0))],
        out_specs=[pl.BlockSpec((tr, d), lambda i: (i, 0)), pl.BlockSpec((1, d), lambda i: (0, 0))],
        out_shape=[jax.ShapeDtypeStruct((rows, d), F32), jax.ShapeDtypeStruct((1, d), F32)],
        compiler_params=pltpu.CompilerParams(dimension_semantics=("arbitrary",)),
    )(x, g, dy)


@jax.custom_vjp
def rmsnorm(x, g):
    return _rms_fwd_call(x, g)


rmsnorm.defvjp(lambda x, g: (_rms_fwd_call(x, g), (x, g)), lambda res, dy: tuple(_rms_bwd_call(res[0], res[1], dy)))


def _bcast_call(x, p, mul):
    rows, d = x.shape
    tr = _row_tile(rows, d)

    def body(x_ref, p_ref, y_ref):
        y_ref[...] = x_ref[...] * p_ref[...] if mul else x_ref[...] + p_ref[...]

    return pl.pallas_call(
        body, name="bcast_mul" if mul else "bcast_add", grid=(rows // tr,),
        in_specs=[pl.BlockSpec((tr, d), lambda i: (i, 0)), pl.BlockSpec((1, d), lambda i: (0, 0))],
        out_specs=pl.BlockSpec((tr, d), lambda i: (i, 0)),
        out_shape=jax.ShapeDtypeStruct((rows, d), F32),
        compiler_params=pltpu.CompilerParams(dimension_semantics=("parallel",)),
    )(x, p)


def _colsum_call(a, b=None):
    rows, d = a.shape
    tr = _row_tile(rows, d)
    ops = (a,) if b is None else (a, b)

    def body(*refs):
        o_ref = refs[-1]

        @pl.when(pl.program_id(0) == 0)
        def _():
            o_ref[...] = jnp.zeros_like(o_ref)

        v = refs[0][...] if b is None else refs[0][...] * refs[1][...]
        o_ref[...] += jnp.sum(v, axis=0, keepdims=True)

    return pl.pallas_call(
        body, name="colsum", grid=(rows // tr,),
        in_specs=[pl.BlockSpec((tr, d), lambda i: (i, 0))] * len(ops),
        out_specs=pl.BlockSpec((1, d), lambda i: (0, 0)),
        out_shape=jax.ShapeDtypeStruct((1, d), F32),
        compiler_params=pltpu.CompilerParams(dimension_semantics=("arbitrary",)),
    )(*ops)


@jax.custom_vjp
def bmul(x, p):
    return _bcast_call(x, p, True)


bmul.defvjp(lambda x, p: (_bcast_call(x, p, True), (x, p)),
            lambda res, dy: (_bcast_call(dy, res[1], True), _colsum_call(dy, res[0])))


@jax.custom_vjp
def badd(x, p):
    return _bcast_call(x, p, False)


badd.defvjp(lambda x, p: (_bcast_call(x, p, False), None), lambda res, dy: (dy, _colsum_call(dy)))


def _heads(width):
    return [slice(h * HEAD_DIM, (h + 1) * HEAD_DIM) for h in range(width // HEAD_DIM)]


def _head_rms_fwd_call(x, g):
    rows, w = x.shape
    tr = _row_tile(rows, w, budget=1024 * 1024)

    def body(x_ref, g_ref, y_ref):
        for sl in _heads(w):
            xv = x_ref[:, sl]
            rstd = lax.rsqrt(jnp.mean(xv * xv, axis=1, keepdims=True) + RMS_EPS)
            y_ref[:, sl] = (xv * rstd) * g_ref[...]

    return pl.pallas_call(
        body, name="head_rms_fwd", grid=(rows // tr,),
        in_specs=[pl.BlockSpec((tr, w), lambda i: (i, 0)), pl.BlockSpec((1, HEAD_DIM), lambda i: (0, 0))],
        out_specs=pl.BlockSpec((tr, w), lambda i: (i, 0)),
        out_shape=jax.ShapeDtypeStruct((rows, w), F32),
        compiler_params=pltpu.CompilerParams(dimension_semantics=("parallel",)),
    )(x, g)


def _head_rms_bwd_call(x, g, dy):
    rows, w = x.shape
    tr = _row_tile(rows, w, budget=1024 * 1024)

    def body(x_ref, g_ref, dy_ref, dx_ref, dg_ref):
        @pl.when(pl.program_id(0) == 0)
        def _():
            dg_ref[...] = jnp.zeros_like(dg_ref)

        dg = jnp.zeros((1, HEAD_DIM), F32)
        for sl in _heads(w):
            xv = x_ref[:, sl]
            dyv = dy_ref[:, sl]
            rstd = lax.rsqrt(jnp.mean(xv * xv, axis=1, keepdims=True) + RMS_EPS)
            xhat = xv * rstd
            dxhat = dyv * g_ref[...]
            dx_ref[:, sl] = rstd * (dxhat - xhat * jnp.mean(dxhat * xhat, axis=1, keepdims=True))
            dg = dg + jnp.sum(dyv * xhat, axis=0, keepdims=True)
        dg_ref[...] += dg

    return pl.pallas_call(
        body, name="head_rms_bwd", grid=(rows // tr,),
        in_specs=[pl.BlockSpec((tr, w), lambda i: (i, 0)), pl.BlockSpec((1, HEAD_DIM), lambda i: (0, 0)),
                  pl.BlockSpec((tr, w), lambda i: (i, 0))],
        out_specs=[pl.BlockSpec((tr, w), lambda i: (i, 0)), pl.BlockSpec((1, HEAD_DIM), lambda i: (0, 0))],
        out_shape=[jax.ShapeDtypeStruct((rows, w), F32), jax.ShapeDtypeStruct((1, HEAD_DIM), F32)],
        compiler_params=pltpu.CompilerParams(dimension_semantics=("arbitrary",)),
    )(x, g, dy)


@jax.custom_vjp
def head_rms(x, g):
    return _head_rms_fwd_call(x, g)


head_rms.defvjp(lambda x, g: (_head_rms_fwd_call(x, g), (x, g)),
                lambda res, dy: tuple(_head_rms_bwd_call(res[0], res[1], dy)))


def _head_l2_call(x, dy=None):
    rows, w = x.shape
    tr = _row_tile(rows, w, budget=1024 * 1024)
    ops = (x,) if dy is None else (x, dy)

    def body(*refs):
        o_ref = refs[-1]
        for sl in _heads(w):
            xv = refs[0][:, sl]
            nrm = jnp.sqrt(jnp.sum(xv * xv, axis=1, keepdims=True))
            live = nrm > L2_FLOOR
            inv = 1.0 / jnp.maximum(nrm, L2_FLOOR)
            y = xv * inv
            if dy is None:
                o_ref[:, sl] = y
            else:
                dyv = refs[1][:, sl]
                proj = jnp.where(live, jnp.sum(dyv * y, axis=1, keepdims=True), 0.0)
                o_ref[:, sl] = (dyv - y * proj) * inv

    return pl.pallas_call(
        body, name="head_l2_fwd" if dy is None else "head_l2_bwd", grid=(rows // tr,),
        in_specs=[pl.BlockSpec((tr, w), lambda i: (i, 0))] * len(ops),
        out_specs=pl.BlockSpec((tr, w), lambda i: (i, 0)),
        out_shape=jax.ShapeDtypeStruct((rows, w), F32),
        compiler_params=pltpu.CompilerParams(dimension_semantics=("parallel",)),
    )(*ops)


@jax.custom_vjp
def head_l2norm(x):
    return _head_l2_call(x)


head_l2norm.defvjp(lambda x: (_head_l2_call(x), x), lambda x, dy: (_head_l2_call(x, dy),))


def _gn_fwd_call(y, r, kf, v, gw, gb, rk):
    rows, w = y.shape
    tr = _row_tile(rows, w, budget=512 * 1024)

    def body(y_ref, r_ref, kf_ref, v_ref, gw_ref, gb_ref, rk_ref, o_ref):
        for sl in _heads(w):
            yv = y_ref[:, sl]
            yc = yv - jnp.mean(yv, axis=1, keepdims=True)
            rstd = lax.rsqrt(jnp.mean(yc * yc, axis=1, keepdims=True) + GN_EPS)
            s = jnp.sum(r_ref[:, sl] * kf_ref[:, sl] * rk_ref[:, sl], axis=1, keepdims=True)
            o_ref[:, sl] = (yc * rstd) * gw_ref[:, sl] + gb_ref[:, sl] + s * v_ref[:, sl]

    tok = pl.BlockSpec((tr, w), lambda i: (i, 0))
    par = pl.BlockSpec((1, w), lambda i: (0, 0))
    return pl.pallas_call(
        body, name="gn_bonus_fwd", grid=(rows // tr,),
        in_specs=[tok] * 4 + [par] * 3, out_specs=tok,
        out_shape=jax.ShapeDtypeStruct((rows, w), F32),
        compiler_params=pltpu.CompilerParams(dimension_semantics=("parallel",)),
    )(y, r, kf, v, gw, gb, rk)


def _gn_bwd_call(y, r, kf, v, gw, gb, rk, do):
    rows, w = y.shape
    tr = _row_tile(rows, w, budget=512 * 1024)

    def body(y_ref, r_ref, kf_ref, v_ref, gw_ref, rk_ref, do_ref,
             dy_ref, dr_ref, dkf_ref, dv_ref, dgw_ref, dgb_ref, drk_ref):
        @pl.when(pl.program_id(0) == 0)
        def _():
            dgw_ref[...] = jnp.zeros_like(dgw_ref)
            dgb_ref[...] = jnp.zeros_like(dgb_ref)
            drk_ref[...] = jnp.zeros_like(drk_ref)

        for sl in _heads(w):
            yv, rv, kv, vv, dov = y_ref[:, sl], r_ref[:, sl], kf_ref[:, sl], v_ref[:, sl], do_ref[:, sl]
            yc = yv - jnp.mean(yv, axis=1, keepdims=True)
            rstd = lax.rsqrt(jnp.mean(yc * yc, axis=1, keepdims=True) + GN_EPS)
            yhat = yc * rstd
            dyhat = dov * gw_ref[:, sl]
            dy_ref[:, sl] = rstd * (dyhat - jnp.mean(dyhat, axis=1, keepdims=True)
                                    - yhat * jnp.mean(dyhat * yhat, axis=1, keepdims=True))
            rkv = rk_ref[:, sl]
            s = jnp.sum(rv * kv * rkv, axis=1, keepdims=True)
            ds = jnp.sum(dov * vv, axis=1, keepdims=True)
            dv_ref[:, sl] = s * dov
            dr_ref[:, sl] = ds * kv * rkv
            dkf_ref[:, sl] = ds * rv * rkv
            dgw_ref[:, sl] += jnp.sum(dov * yhat, axis=0, keepdims=True)
            dgb_ref[:, sl] += jnp.sum(dov, axis=0, keepdims=True)
            drk_ref[:, sl] += jnp.sum(ds * rv * kv, axis=0, keepdims=True)

    tok = pl.BlockSpec((tr, w), lambda i: (i, 0))
    par = pl.BlockSpec((1, w), lambda i: (0, 0))
    tshape = jax.ShapeDtypeStruct((rows, w), F32)
    pshape = jax.ShapeDtypeStruct((1, w), F32)
    return pl.pallas_call(
        body, name="gn_bonus_bwd", grid=(rows // tr,),
        in_specs=[tok] * 4 + [par] * 2 + [tok], out_specs=[tok] * 4 + [par] * 3,
        out_shape=[tshape] * 4 + [pshape] * 3,
        compiler_params=pltpu.CompilerParams(dimension_semantics=("arbitrary",)),
    )(y, r, kf, v, gw, rk, do)


@jax.custom_vjp
def gn_bonus(y, r, kf, v, gw, gb, rk):
    return _gn_fwd_call(y, r, kf, v, gw, gb, rk)


def _gn_bwd(res, do):
    y, r, kf, v, gw, gb, rk = res
    dy, dr, dkf, dv, dgw, dgb, drk = _gn_bwd_call(y, r, kf, v, gw, gb, rk, do)
    return dy, dr, dkf, dv, dgw, dgb, drk


gn_bonus.defvjp(lambda *a: (_gn_fwd_call(*a), a), _gn_bwd)


def _tri_masks(c):
    i = lax.broadcasted_iota(jnp.int32, (c, c), 0)
    j = lax.broadcasted_iota(jnp.int32, (c, c), 1)
    return i > j, i >= j, i == j


def _wkv_chunk_common(r, lw, k, a, b):
    c = r.shape[0]
    strict, incl, diag = _tri_masks(c)
    tri = jnp.where(incl, 1.0, 0.0).astype(BF16)
    lc = sum(_dg(tri, part, False, False) for part in _split(lw, 3))
    lend = lc[c - 1:c, :]
    rt = r * jnp.exp(lc)
    at = a * jnp.exp(lc - lw)
    pinv = jnp.exp(-lc)
    kt = k * pinv
    bt = b * pinv
    e = jnp.exp(lend - lc)
    ktp = k * e
    btp = b * e
    zero = jnp.zeros((c, c), F32)
    a_ab = jnp.where(strict, _mm(at, bt, tb=True), zero)
    a_ak = jnp.where(strict, _mm(at, kt, tb=True), zero)
    a_rb = jnp.where(incl, _mm(rt, bt, tb=True), zero)
    a_rk = jnp.where(incl, _mm(rt, kt, tb=True), zero)
    eye = jnp.where(diag, 1.0, 0.0).astype(F32)
    t = eye + a_ab
    xp = a_ab
    n = 2
    while n < c:
        xp = _mm(xp, xp)
        t = t + _mm(t, xp)
        n *= 2
    pend_col = jnp.sum(eye * jnp.exp(lend), axis=1, keepdims=True)
    return dict(rt=rt, at=at, kt=kt, bt=bt, ktp=ktp, btp=btp, a_ak=a_ak, a_rb=a_rb, a_rk=a_rk, t=t,
                pend_col=pend_col, lend=lend, lc=lc, strict=strict, incl=incl, tri=tri)


def _wkv_fwd_call(r, lw, k, v, a, b):
    tokens, width = r.shape
    c = WKV_CHUNK
    nc = tokens // c
    hd = HEAD_DIM

    def body(r_ref, lw_ref, k_ref, v_ref, a_ref, b_ref, y_ref, s_ref, st):
        @pl.when(pl.program_id(1) == 0)
        def _():
            st[...] = jnp.zeros_like(st)

        s_ref[0] = st[...]
        for hh in range(2):
            sl = slice(hh * hd, (hh + 1) * hd)
            rv, lwv, kv, vv, av, bv = (ref[:, sl] for ref in (r_ref, lw_ref, k_ref, v_ref, a_ref, b_ref))
            s0 = st[:, sl]
            q = _wkv_chunk_common(rv, lwv, kv, av, bv)
            w1 = _mm(q["at"], s0) + _mm(q["a_ak"], vv)
            u = _mm(q["t"], w1)
            y_ref[:, sl] = _mm(q["rt"], s0) + _mm(q["a_rb"], u) + _mm(q["a_rk"], vv)
            st[:, sl] = q["pend_col"] * s0 + _mm(q["btp"], u, ta=True) + _mm(q["ktp"], vv, ta=True)

    tok = pl.BlockSpec((c, 2 * hd), lambda hp, ci: (ci, hp))
    return pl.pallas_call(
        body, name="wkv_fwd", grid=(width // (2 * hd), nc),
        in_specs=[tok] * 6,
        out_specs=[tok, pl.BlockSpec((1, hd, 2 * hd), lambda hp, ci: (ci, 0, hp))],
        out_shape=[jax.ShapeDtypeStruct((tokens, width), F32), jax.ShapeDtypeStruct((nc, hd, width), F32)],
        scratch_shapes=[pltpu.VMEM((hd, 2 * hd), F32)],
        compiler_params=pltpu.CompilerParams(dimension_semantics=("parallel", "arbitrary")),
    )(r, lw, k, v, a, b)


def _wkv_bwd_call(r, lw, k, v, a, b, s, dy):
    tokens, width = r.shape
    c = WKV_CHUNK
    nc = tokens // c
    hd = HEAD_DIM

    def body(r_ref, lw_ref, k_ref, v_ref, a_ref, b_ref, s_ref, dy_ref,
             dr_ref, dlw_ref, dk_ref, dv_ref, da_ref, db_ref, dst):
        @pl.when(pl.program_id(1) == 0)
        def _():
            dst[...] = jnp.zeros_like(dst)

        for hh in range(2):
            sl = slice(hh * hd, (hh + 1) * hd)
            rv, lwv, kv, vv, av, bv, dyv = (ref[:, sl] for ref in (r_ref, lw_ref, k_ref, v_ref, a_ref, b_ref, dy_ref))
            s0 = s_ref[0, :, sl]
            dsc = dst[:, sl]
            q = _wkv_chunk_common(rv, lwv, kv, av, bv)
            rt, at, kt, bt, ktp, btp, t = (q[n] for n in ("rt", "at", "kt", "bt", "ktp", "btp", "t"))
            w1 = _mm(at, s0) + _mm(q["a_ak"], vv)
            u = _mm(t, w1)
            du = _mm(q["a_rb"], dyv, ta=True) + _mm(btp, dsc)
            dw1 = _mm(t, du, ta=True)
            dv_ref[:, sl] = _mm(q["a_rk"], dyv, ta=True) + _mm(ktp, dsc) + _mm(q["a_ak"], dw1, ta=True)
            zero = jnp.zeros((c, c), F32)
            da_ab = jnp.where(q["strict"], _mm(dw1, u, tb=True), zero)
            da_ak = jnp.where(q["strict"], _mm(dw1, vv, tb=True), zero)
            da_rb = jnp.where(q["incl"], _mm(dyv, u, tb=True), zero)
            da_rk = jnp.where(q["incl"], _mm(dyv, vv, tb=True), zero)
            d_rt = _mm(dyv, s0, tb=True) + _mm(da_rb, bt) + _mm(da_rk, kt)
            d_at = _mm(dw1, s0, tb=True) + _mm(da_ab, bt) + _mm(da_ak, kt)
            d_bt = _mm(da_ab, at, ta=True) + _mm(da_rb, rt, ta=True)
            d_kt = _mm(da_ak, at, ta=True) + _mm(da_rk, rt, ta=True)
            d_btp = _mm(u, dsc, tb=True)
            d_ktp = _mm(vv, dsc, tb=True)
            prod = dsc * s0
            ones = jnp.ones((8, hd), BF16)
            dpend = sum(_dg(ones, part, False, True) for part in _split(prod, 3))[0:1, :] * jnp.exp(q["lend"])
            dst[:, sl] = q["pend_col"] * dsc + _mm(rt, dyv, ta=True) + _mm(at, dw1, ta=True)
            lc_e = d_ktp * ktp + d_btp * btp
            dlend = jnp.sum(lc_e, axis=0, keepdims=True) + dpend
            last = lax.broadcasted_iota(jnp.int32, (c, hd), 0) == c - 1
            dlc = d_rt * rt - d_kt * kt - d_bt * bt - lc_e + jnp.where(last, dlend, 0.0)
            dlp = d_at * at
            dlw_ref[:, sl] = sum(_dg(q["tri"], part, True, False) for part in _split(dlc + dlp, 3)) - dlp
            lc = q["lc"]
            pinv = jnp.exp(-lc)
            e = jnp.exp(q["lend"] - lc)
            dr_ref[:, sl] = d_rt * jnp.exp(lc)
            da_ref[:, sl] = d_at * jnp.exp(lc - lwv)
            dk_ref[:, sl] = d_kt * pinv + d_ktp * e
            db_ref[:, sl] = d_bt * pinv + d_btp * e

    tok = pl.BlockSpec((c, 2 * hd), lambda hp, ci: (nc - 1 - ci, hp))
    tshape = jax.ShapeDtypeStruct((tokens, width), F32)
    return pl.pallas_call(
        body, name="wkv_bwd", grid=(width // (2 * hd), nc),
        in_specs=[tok] * 6 + [pl.BlockSpec((1, hd, 2 * hd), lambda hp, ci: (nc - 1 - ci, 0, hp)), tok],
        out_specs=[tok] * 6, out_shape=[tshape] * 6,
        scratch_shapes=[pltpu.VMEM((hd, 2 * hd), F32)],
        compiler_params=pltpu.CompilerParams(dimension_semantics=("parallel", "arbitrary")),
    )(r, lw, k, v, a, b, s, dy)


@jax.custom_vjp
def wkv7(r, lw, k, v, a, b):
    return _wkv_fwd_call(r, lw, k, v, a, b)[0]


def _wkv7_fwd(r, lw, k, v, a, b):
    y, s = _wkv_fwd_call(r, lw, k, v, a, b)
    return y, (r, lw, k, v, a, b, s)


wkv7.defvjp(_wkv7_fwd, lambda res, dy: tuple(_wkv_bwd_call(*res, dy)))


def _attn_block(tokens):
    return ATTN_BLOCK_BIG if tokens % ATTN_BLOCK_BIG == 0 else ATTN_BLOCK


def _fox_layouts(cum):
    tokens, heads = cum.shape
    t = _attn_block(tokens)
    cq = cum.reshape(tokens, heads // 2, 2).transpose(1, 0, 2)
    ck = cum.T.reshape(heads // 2, 2, tokens // t, t).transpose(0, 2, 1, 3)
    return cq, ck


def _head_lane_masks(rows):
    lane = lax.broadcasted_iota(jnp.int32, (rows, 2 * HEAD_DIM), 1)
    return [lane < HEAD_DIM, lane >= HEAD_DIM]


def _fox_fwd_call(q, k, v, cq, ck):
    tokens, width = q.shape
    t = _attn_block(tokens)
    nb = tokens // t
    hd = HEAD_DIM
    npair = width // (2 * hd)

    def body(q_ref, k_ref, v_ref, cq_ref, ck_ref, o_ref, lse_ref):
        i = pl.program_id(1)
        masks = _head_lane_masks(t)
        q2 = q_ref[...]
        qs = [jnp.where(mk, q2, 0.0).astype(BF16) for mk in masks]
        cqs = [cq_ref[0, :, hh:hh + 1] for hh in range(2)]

        def block(j, carry, diagonal):
            off = pl.multiple_of(j * t, t)
            ckj = ck_ref[0, j]
            k2 = k_ref[pl.ds(off, t), :].astype(BF16)
            v2 = v_ref[pl.ds(off, t), :].astype(BF16)
            out = []
            for hh in range(2):
                m, l, acc = carry[hh]
                s = _dg(qs[hh], k2, False, True) + (cqs[hh] - ckj[hh:hh + 1, :])
                if diagonal:
                    keep = lax.broadcasted_iota(jnp.int32, (t, t), 0) >= lax.broadcasted_iota(jnp.int32, (t, t), 1)
                    s = jnp.where(keep, s, NEG_BIG)
                m_new = jnp.maximum(m, jnp.max(s, axis=1, keepdims=True))
                alpha = jnp.exp(m - m_new)
                p = jnp.exp(s - m_new)
                l = alpha * l + jnp.sum(p, axis=1, keepdims=True)
                acc = alpha * acc + _dg(p.astype(BF16), v2, False, False)
                out.append((m_new, l, acc))
            return tuple(out)

        init = tuple((jnp.full((t, 1), NEG_BIG, F32), jnp.zeros((t, 1), F32), jnp.zeros((t, 2 * hd), F32)) for _ in range(2))
        res = lax.fori_loop(0, i, lambda j, c: block(j, c, False), init)
        res = block(i, res, True)
        o_ref[...] = jnp.where(masks[0], res[0][2] / res[0][1], res[1][2] / res[1][1])
        for hh in range(2):
            lse_ref[0, :, hh:hh + 1] = res[hh][0] + jnp.log(res[hh][1])

    blk = pl.BlockSpec((t, 2 * hd), lambda hp, i: (i, hp))
    full = pl.BlockSpec((tokens, 2 * hd), lambda hp, i: (0, hp))
    cq_spec = pl.BlockSpec((1, t, 2), lambda hp, i: (hp, i, 0))
    ck_spec = pl.BlockSpec((1, nb, 2, t), lambda hp, i: (hp, 0, 0, 0))
    return pl.pallas_call(
        body, name="fox_fwd", grid=(npair, nb),
        in_specs=[blk, full, full, cq_spec, ck_spec],
        out_specs=[blk, cq_spec],
        out_shape=[jax.ShapeDtypeStruct((tokens, width), F32), jax.ShapeDtypeStruct((npair, tokens, 2), F32)],
        compiler_params=pltpu.CompilerParams(dimension_semantics=("parallel", "arbitrary")),
    )(q, k, v, cq, ck)


def _fox_bwd_call(q, k, v, cq, ck, o, lse, do):
    tokens, width = q.shape
    t = _attn_block(tokens)
    nb = tokens // t
    hd = HEAD_DIM
    npair = width // (2 * hd)

    def body(q_ref, k_ref, v_ref, cq_ref, ck_ref, o_ref, lse_ref, do_ref, dq_ref, dk_ref, dv_ref, dck_ref, dcq_ref):
        i = pl.program_id(1)

        @pl.when(i == 0)
        def _():
            dk_ref[...] = jnp.zeros_like(dk_ref)
            dv_ref[...] = jnp.zeros_like(dv_ref)
            dck_ref[...] = jnp.zeros_like(dck_ref)

        masks = _head_lane_masks(t)
        q2, do2, o2 = q_ref[...], do_ref[...], o_ref[...]
        qs = [jnp.where(mk, q2, 0.0).astype(BF16) for mk in masks]
        dos = [jnp.where(mk, do2, 0.0).astype(BF16) for mk in masks]
        deltas = [jnp.sum(dos[hh].astype(F32) * o2, axis=1, keepdims=True) for hh in range(2)]
        bias = [cq_ref[0, :, hh:hh + 1] - lse_ref[0, :, hh:hh + 1] for hh in range(2)]

        def block(j, carry, diagonal):
            off = pl.multiple_of(j * t, t)
            ckj = ck_ref[0, j]
            k2 = k_ref[pl.ds(off, t), :].astype(BF16)
            v2 = v_ref[pl.ds(off, t), :].astype(BF16)
            out = []
            dk2 = jnp.zeros((t, 2 * hd), F32)
            dv2 = jnp.zeros((t, 2 * hd), F32)
            for hh in range(2):
                s = _dg(qs[hh], k2, False, True) + (bias[hh] - ckj[hh:hh + 1, :])
                if diagonal:
                    keep = lax.broadcasted_iota(jnp.int32, (t, t), 0) >= lax.broadcasted_iota(jnp.int32, (t, t), 1)
                    s = jnp.where(keep, s, NEG_BIG)
                p = jnp.exp(s)
                dp = _dg(dos[hh], v2, False, True)
                ds = p * (dp - deltas[hh])
                dsb = ds.astype(BF16)
                dq, rowsum = carry[hh]
                out.append((dq + _dg(dsb, k2, False, False), rowsum + jnp.sum(ds, axis=1, keepdims=True)))
                dk2 = dk2 + _dg(dsb, qs[hh], True, False)
                dv2 = dv2 + _dg(p.astype(BF16), dos[hh], True, False)
                dck_ref[0, j, hh:hh + 1, :] -= jnp.sum(ds, axis=0, keepdims=True)
            dk_ref[pl.ds(off, t), :] += dk2
            dv_ref[pl.ds(off, t), :] += dv2
            return tuple(out)

        init = tuple((jnp.zeros((t, 2 * hd), F32), jnp.zeros((t, 1), F32)) for _ in range(2))
        res = lax.fori_loop(0, i, lambda j, c: block(j, c, False), init)
        res = block(i, res, True)
        dq_ref[...] = jnp.where(masks[0], res[0][0], res[1][0])
        for hh in range(2):
            dcq_ref[0, :, hh:hh + 1] = res[hh][1]

    blk = pl.BlockSpec((t, 2 * hd), lambda hp, i: (i, hp))
    full = pl.BlockSpec((tokens, 2 * hd), lambda hp, i: (0, hp))
    cq_spec = pl.BlockSpec((1, t, 2), lambda hp, i: (hp, i, 0))
    ck_spec = pl.BlockSpec((1, nb, 2, t), lambda hp, i: (hp, 0, 0, 0))
    tshape = jax.ShapeDtypeStruct((tokens, width), F32)
    return pl.pallas_call(
        body, name="fox_bwd", grid=(npair, nb),
        in_specs=[blk, full, full, cq_spec, ck_spec, blk, cq_spec, blk],
        out_specs=[blk, full, full, ck_spec, cq_spec],
        out_shape=[tshape, tshape, tshape, jax.ShapeDtypeStruct((npair, nb, 2, t), F32),
                   jax.ShapeDtypeStruct((npair, tokens, 2), F32)],
        compiler_params=pltpu.CompilerParams(dimension_semantics=("parallel", "arbitrary")),
    )(q, k, v, cq, ck, o, lse, do)


@jax.custom_vjp
def fox_attention(q, k, v, cum):
    return _fox_fwd_call(q, k, v, *_fox_layouts(cum))[0]


def _fox_fwd(q, k, v, cum):
    cq, ck = _fox_layouts(cum)
    o, lse = _fox_fwd_call(q, k, v, cq, ck)
    return o, (q, k, v, cq, ck, o, lse)


def _fox_bwd(res, do):
    q, k, v, cq, ck, o, lse = res
    dq, dk, dv, dck, dcq = _fox_bwd_call(q, k, v, cq, ck, o, lse, do)
    npair, nb, _, t = dck.shape
    dcum = dck.transpose(0, 2, 1, 3).reshape(2 * npair, nb * t).T + dcq.transpose(1, 0, 2).reshape(nb * t, 2 * npair)
    return dq, dk, dv, dcum


fox_attention.defvjp(_fox_fwd, _fox_bwd)


def _loss_call(y, target):
    rows, d = y.shape
    tr = _row_tile(rows, d)

    def body(y_ref, t_ref, loss_ref, dy_ref):
        @pl.when(pl.program_id(0) == 0)
        def _():
            loss_ref[...] = jnp.zeros_like(loss_ref)

        diff = y_ref[...] - t_ref[...]
        dy_ref[...] = diff * (1.0 / d)
        loss_ref[...] += (0.5 / d) * jnp.sum(jnp.sum(diff * diff, axis=1, keepdims=True), axis=0, keepdims=True)

    return pl.pallas_call(
        body, name="loss", grid=(rows // tr,),
        in_specs=[pl.BlockSpec((tr, d), lambda i: (i, 0))] * 2,
        out_specs=[pl.BlockSpec((1, 1), lambda i: (0, 0)), pl.BlockSpec((tr, d), lambda i: (i, 0))],
        out_shape=[jax.ShapeDtypeStruct((1, 1), F32), jax.ShapeDtypeStruct((rows, d), F32)],
        compiler_params=pltpu.CompilerParams(dimension_semantics=("arbitrary",)),
    )(y, target)


def _adamw_call(w, g, m, v):
    rows, cols = w.shape
    tr = _row_tile(rows, cols, budget=1024 * 1024)
    c1 = 1.0 / (1.0 - ADAM_B1 ** ADAM_STEP)
    c2 = 1.0 / (1.0 - ADAM_B2 ** ADAM_STEP)

    def body(w_ref, g_ref, m_ref, v_ref, d_ref, nm_ref, nv_ref):
        gv = g_ref[...]
        nm = ADAM_B1 * m_ref[...] + (1.0 - ADAM_B1) * gv
        nv = ADAM_B2 * v_ref[...] + (1.0 - ADAM_B2) * (gv * gv)
        nm_ref[...] = nm
        nv_ref[...] = nv
        d_ref[...] = -ADAM_LR * ((nm * c1) / (jnp.sqrt(nv * c2) + ADAM_EPS) + ADAM_WD * w_ref[...])

    spec = pl.BlockSpec((tr, cols), lambda i: (i, 0))
    shape = jax.ShapeDtypeStruct((rows, cols), F32)
    return pl.pallas_call(
        body, name="adamw", grid=(rows // tr,),
        in_specs=[spec] * 4, out_specs=[spec] * 3, out_shape=[shape] * 3,
        compiler_params=pltpu.CompilerParams(dimension_semantics=("parallel",)),
    )(w, g, m, v)


def _my_place():
    return lax.axis_index("x"), lax.axis_index("y"), lax.axis_index("c")


def _place_index(px, py, pc):
    return 4 * px + 2 * py + pc


HBM_SPEC = pl.BlockSpec(memory_space=pltpu.HBM)


def _all_gather_call(block):
    def body(x_ref, out_ref, send_sems, recv_sems, local_sem):
        x, y, c = _my_place()
        me, sibling = (x, y, c), (x, y, 1 - c)
        chips = [(1 - x, y), (x, 1 - y), (1 - x, 1 - y)]

        def slot(px, py, pc):
            return out_ref.at[_place_index(px, py, pc)]

        def copy(k, blk, to, src=None):
            return pltpu.make_async_remote_copy(
                src_ref=slot(*blk) if src is None else src, dst_ref=slot(*blk),
                send_sem=send_sems.at[k], recv_sem=recv_sems.at[k],
                device_id=to, device_id_type=pl.DeviceIdType.MESH)

        mine = pltpu.make_async_copy(x_ref, slot(*me), local_sem)
        mine.start()
        first = [copy(0, me, sibling, src=x_ref)]
        first += [copy(1 + j, me, (*chip, c), src=x_ref) for j, chip in enumerate(chips)]
        for cp in first:
            cp.start()
        passed = [copy(4 + j, (*chip, c), sibling) for j, chip in enumerate(chips)]
        for j, chip in enumerate(chips):
            copy(1 + j, (*chip, c), me).wait_recv()
            passed[j].start()
        copy(0, sibling, me).wait_recv()
        for j, chip in enumerate(chips):
            copy(4 + j, (*chip, 1 - c), me).wait_recv()
        for cp in first + passed:
            cp.wait_send()
        mine.wait()

    return pl.pallas_call(
        body, name="all_gather",
        out_shape=jax.ShapeDtypeStruct((N_DEV,) + block.shape, block.dtype),
        in_specs=[HBM_SPEC], out_specs=HBM_SPEC,
        scratch_shapes=[pltpu.SemaphoreType.DMA((7,)), pltpu.SemaphoreType.DMA((7,)), pltpu.SemaphoreType.DMA],
    )(block)


def _exchange_call(blocks):
    def body(g_ref, out_ref, send_sems, recv_sems, local_sem):
        x, y, c = _my_place()
        mine_idx = _place_index(x, y, c)
        mine = pltpu.make_async_copy(g_ref.at[mine_idx], out_ref.at[mine_idx], local_sem)
        mine.start()
        copies, arrivals = [], []
        for k in range(1, N_DEV):
            peer = (x ^ (k >> 2), y ^ ((k >> 1) & 1), c ^ (k & 1))
            peer_idx = _place_index(*peer)
            copies.append(pltpu.make_async_remote_copy(
                src_ref=g_ref.at[peer_idx], dst_ref=out_ref.at[mine_idx],
                send_sem=send_sems.at[k - 1], recv_sem=recv_sems.at[k - 1],
                device_id=peer, device_id_type=pl.DeviceIdType.MESH))
            arrivals.append(pltpu.make_async_remote_copy(
                src_ref=g_ref.at[mine_idx], dst_ref=out_ref.at[peer_idx],
                send_sem=send_sems.at[k - 1], recv_sem=recv_sems.at[k - 1],
                device_id=peer, device_id_type=pl.DeviceIdType.MESH))
        for cp in copies:
            cp.start()
        for cp in arrivals:
            cp.wait_recv()
        for cp in copies:
            cp.wait_send()
        mine.wait()

    return pl.pallas_call(
        body, name="grad_exchange",
        out_shape=jax.ShapeDtypeStruct(blocks.shape, blocks.dtype),
        in_specs=[HBM_SPEC], out_specs=HBM_SPEC,
        scratch_shapes=[pltpu.SemaphoreType.DMA((7,)), pltpu.SemaphoreType.DMA((7,)), pltpu.SemaphoreType.DMA],
    )(blocks)


def _sum_slots_call(slots):
    _, rows, cols = slots.shape
    tr = _row_tile(rows, cols, budget=512 * 1024)

    def body(s_ref, o_ref):
        acc = s_ref[0].astype(F32)
        for j in range(1, N_DEV):
            acc = acc + s_ref[j].astype(F32)
        o_ref[...] = acc

    return pl.pallas_call(
        body, name="sum_slots", grid=(rows // tr,),
        in_specs=[pl.BlockSpec((N_DEV, tr, cols), lambda i: (0, i, 0))],
        out_specs=pl.BlockSpec((tr, cols), lambda i: (i, 0)),
        out_shape=jax.ShapeDtypeStruct((rows, cols), F32),
        compiler_params=pltpu.CompilerParams(dimension_semantics=("parallel",)),
    )(slots)


def _make_gather(payload_dtype):
    @jax.custom_vjp
    def gather(block):
        return _all_gather_call(block.astype(payload_dtype))

    def bwd(_, dall):
        return (_sum_slots_call(_exchange_call(dall)),)

    gather.defvjp(lambda block: (_all_gather_call(block.astype(payload_dtype)), None), bwd)
    return gather


gather_bf16 = _make_gather(BF16)
gather_f32 = _make_gather(F32)


def _pack(vectors, width):
    flat = jnp.concatenate([v.reshape(-1) for v in vectors])
    return jnp.pad(flat, (0, width - flat.shape[0])).reshape(width // 128, 128)


def _unpack(packed, like):
    flat = packed.reshape(-1)
    out, at = [], 0
    for v in like:
        out.append(flat[at:at + v.size].reshape(v.shape))
        at += v.size
    return tuple(out)


@jax.custom_vjp
def replicated(params):
    return params


def _replicated_bwd(like, grads):
    n = sum(v.size for v in like)
    width = -(-n // 1024) * 1024
    total = _sum_slots_call(_all_gather_call(_pack(grads, width)))
    return (_unpack(total, like),)


replicated.defvjp(lambda params: (params, params), _replicated_bwd)


def _cols_from_slots(slots):
    n, rows, cols = slots.shape
    return slots.transpose(1, 0, 2).reshape(rows, n * cols)


def _forward(sharded, small, x, dims):
    n_meta, seq, lp = dims["n_meta"], dims["seq"], dims["lp"]
    (meta_s, w_in_s, w2_s, a2_s, g2_s, w_a_s, w_b_s, w_o_s, w_gu_s, w_dn_s) = sharded
    (n1, mu, w0, a0, k_k, k_a, r_k, gn_w, gn_b, q_g, k_g, f_bias, n2) = replicated(small)

    meta = _cols_from_slots(gather_f32(meta_s))
    w_in = _cols_from_slots(gather_bf16(w_in_s))
    w2 = _cols_from_slots(gather_bf16(w2_s))
    a2 = _cols_from_slots(gather_bf16(a2_s))
    g2 = _cols_from_slots(gather_bf16(g2_s))
    w_a = _cols_from_slots(gather_bf16(w_a_s))
    w_b = _cols_from_slots(gather_bf16(w_b_s))
    w_o = gather_bf16(w_o_s).reshape(-1, w_o_s.shape[1])
    w_gu = _cols_from_slots(gather_bf16(w_gu_s))
    w_dn = gather_bf16(w_dn_s).reshape(-1, w_dn_s.shape[1])

    d = x.shape[1]
    rw, fw = w_a.shape[0], w_b.shape[0]
    dl, al, gl = w2.shape[0], a2.shape[0], g2.shape[0]
    rcols = 3 * rw + dl + al + gl
    fheads = fw // HEAD_DIM
    fcols = 3 * fw + fheads
    pad128 = lambda n: -(-n // 128) * 128
    rpad, fpad = pad128(rcols), pad128(fcols)
    padc = lambda w, n: jnp.pad(w, ((0, 0), (0, n - w.shape[1])))
    w_cat = jnp.concatenate([padc(w_in[:, :rcols], rpad), padc(w_in[:, rcols:rcols + fcols], fpad),
                             w_in[:, rcols + fcols:]], axis=1)

    h0 = jnp.concatenate([meta, x, jnp.zeros((lp - n_meta - seq, d), F32)], axis=0)
    proj = dense(rmsnorm(h0, n1), w_cat)
    z_r, z_f, z_g = proj[:, :rcols], proj[:, rpad:rpad + fcols], proj[:, rpad + fpad:]

    z_prev = jnp.pad(z_r, ((1, 0), (0, 0)))[:-1]
    z = z_r + bmul(z_prev - z_r, mu)
    r, k, v = z[:, :rw], z[:, rw:2 * rw], z[:, 2 * rw:3 * rw]
    wd, ad, gd = z[:, 3 * rw:3 * rw + dl], z[:, 3 * rw + dl:3 * rw + dl + al], z[:, 3 * rw + dl + al:]
    w_log = -jax.nn.softplus(-badd(dense(jnp.tanh(wd), w2), w0)) - 0.5
    lw = -jnp.exp(w_log)
    a_sig = jax.nn.sigmoid(badd(dense(ad, a2), a0))
    g = dense(jax.nn.sigmoid(gd), g2)
    kk = head_l2norm(bmul(k, k_k))
    kf = k * (1.0 + bmul(a_sig - 1.0, k_a))
    y = wkv7(r, lw, kf, v, -kk, kk * a_sig)
    y_a = gn_bonus(y, r, kf, v, gn_w, gn_b, r_k.reshape(1, rw)) * g

    fq, fk, fv, fl = z_f[:, :fw], z_f[:, fw:2 * fw], z_f[:, 2 * fw:3 * fw], z_f[:, 3 * fw:]
    fq = head_rms(fq, q_g) * (HEAD_DIM ** -0.5)
    fk = head_rms(fk, k_g)
    cum = jnp.cumsum(jax.nn.log_sigmoid(badd(fl, f_bias)), axis=0)
    y_b = fox_attention(fq, fk, fv, cum)

    gates = jax.nn.sigmoid(z_g)
    merged = gates[:, :d] * dense(y_a, w_a) + gates[:, d:] * dense(y_b, w_b)
    h1 = h0 + dense(merged, w_o)
    gu = dense(rmsnorm(h1, n2), w_gu)
    dff = w_dn.shape[0]
    return h1 + dense(jax.nn.silu(gu[:, :dff]) * gu[:, dff:], w_dn)


SHARDED = ("meta_tokens", "w_in", "rwkv_w2", "rwkv_a2", "rwkv_g2", "w_branch_a", "w_branch_b", "w_o", "w_gate_up", "w_down")
SMALL = ("norm1_g", "rwkv_mu", "rwkv_w0", "rwkv_a0", "rwkv_k_k", "rwkv_k_a", "rwkv_r_k", "rwkv_gn_w", "rwkv_gn_b",
         "fox_q_norm_g", "fox_k_norm_g", "fox_f_bias", "norm2_g")
WEIGHTS = ("meta_tokens", "norm1_g", "w_in", "rwkv_mu", "rwkv_w0", "rwkv_w2", "rwkv_a0", "rwkv_a2", "rwkv_g2", "rwkv_k_k",
           "rwkv_k_a", "rwkv_r_k", "rwkv_gn_w", "rwkv_gn_b", "fox_q_norm_g", "fox_k_norm_g", "fox_f_bias", "w_branch_a",
           "w_branch_b", "w_o", "norm2_g", "w_gate_up", "w_down")


def _as2d(a):
    return a.reshape(-1, a.shape[-1])


def kernel(x, meta_tokens, norm1_g, w_in, rwkv_mu, rwkv_w0, rwkv_w2, rwkv_a0, rwkv_a2, rwkv_g2, rwkv_k_k, rwkv_k_a, rwkv_r_k, rwkv_gn_w, rwkv_gn_b, fox_q_norm_g, fox_k_norm_g, fox_f_bias, w_branch_a, w_branch_b, w_o, norm2_g, w_gate_up, w_down, loss_target, m_meta_tokens, m_norm1_g, m_w_in, m_rwkv_mu, m_rwkv_w0, m_rwkv_w2, m_rwkv_a0, m_rwkv_a2, m_rwkv_g2, m_rwkv_k_k, m_rwkv_k_a, m_rwkv_r_k, m_rwkv_gn_w, m_rwkv_gn_b, m_fox_q_norm_g, m_fox_k_norm_g, m_fox_f_bias, m_w_branch_a, m_w_branch_b, m_w_o, m_norm2_g, m_w_gate_up, m_w_down, v_meta_tokens, v_norm1_g, v_w_in, v_rwkv_mu, v_rwkv_w0, v_rwkv_w2, v_rwkv_a0, v_rwkv_a2, v_rwkv_g2, v_rwkv_k_k, v_rwkv_k_a, v_rwkv_r_k, v_rwkv_gn_w, v_rwkv_gn_b, v_fox_q_norm_g, v_fox_k_norm_g, v_fox_f_bias, v_w_branch_a, v_w_branch_b, v_w_o, v_norm2_g, v_w_gate_up, v_w_down):
    given = dict(locals())
    w = {n: given[n] for n in WEIGHTS}
    assert rwkv_r_k.shape[-1] == HEAD_DIM
    n_meta, seq = meta_tokens.shape[0], x.shape[1]
    tokens = n_meta + seq
    dims = dict(n_meta=n_meta, seq=seq, lp=-(-tokens // TOKEN_TILE) * TOKEN_TILE)

    sharded = tuple(_as2d(w[n]) for n in SHARDED)
    small = tuple(_as2d(w[n]) for n in SMALL)
    y, vjp = jax.vjp(lambda sh, sm, xs: _forward(sh, sm, xs, dims), sharded, small, x[0])
    loss_part, dy_real = _loss_call(y[n_meta:tokens], loss_target[0])
    dy = jnp.pad(dy_real, ((n_meta, dims["lp"] - tokens), (0, 0)))
    g_sharded, g_small, g_x = vjp(dy)
    loss = lax.psum(loss_part[0, 0], MESH_AXES)

    grads = {n: g.reshape(w[n].shape) for n, g in zip(SHARDED, g_sharded)}
    grads.update({n: g.reshape(w[n].shape) for n, g in zip(SMALL, g_small)})

    delta, new_m, new_v = {}, {}, {}
    for n in SHARDED:
        d_, m_, v_ = _adamw_call(_as2d(w[n]), _as2d(grads[n]), _as2d(given["m_" + n]), _as2d(given["v_" + n]))
        delta[n], new_m[n], new_v[n] = (t.reshape(w[n].shape) for t in (d_, m_, v_))
    n_small = sum(w[n].size for n in SMALL)
    width = -(-n_small // 1024) * 1024
    packs = [_pack([src[n] if p == "" else given[p + n] for n in SMALL], width)
             for p, src in (("", w), ("", grads), ("m_", None), ("v_", None))]
    like = [w[n] for n in SMALL]
    for out, packed in zip((delta, new_m, new_v), _adamw_call(*packs)):
        out.update(dict(zip(SMALL, _unpack(packed, like))))

    return (loss, g_x[None], *[grads[n] for n in WEIGHTS], *[delta[n] for n in WEIGHTS],
            *[new_m[n] for n in WEIGHTS], *[new_v[n] for n in WEIGHTS])
```

```python
import functools

import jax
import jax.numpy as jnp
from jax import lax
from jax.experimental import pallas as pl
from jax.experimental.pallas import tpu as pltpu

F32 = jnp.float32
BF16 = jnp.bfloat16

N_DEV = 8
MESH_AXES = ("x", "y", "c")
HEAD_DIM = 64
TOKEN_TILE = 128
WKV_CHUNK = 64
WKV_PAIRS_PER_STEP = 4
PAIR = 2 * HEAD_DIM
ATTN_BLOCK = 128
ATTN_BLOCK_BIG = 384
RMS_EPS = 1e-6
GN_EPS = 64e-5
L2_FLOOR = 1e-12
NEG_BIG = -1e30
ADAM_LR, ADAM_B1, ADAM_B2, ADAM_EPS, ADAM_WD, ADAM_STEP = 0.001, 0.9, 0.999, 1e-08, 0.01, 10
VMEM_BYTES_V7X = 64 * 1024 * 1024
VMEM_LIMIT_CAP = 56 * 1024 * 1024
VMEM_LIMIT_FLOOR = 32 * 1024 * 1024
MATMUL_VMEM_BUDGET = 36 * 1024 * 1024
GRID_STEP_BYTES = 1024 * 1024


def _vmem_limit(estimate_bytes):
    return int(min(max(estimate_bytes * 5 // 4, VMEM_LIMIT_FLOOR), VMEM_LIMIT_CAP))


def _pick(dim, cands):
    for c in cands:
        if dim % c == 0:
            return c
    return dim


def _row_tile(rows, width, itemsize=4, budget=2 * 1024 * 1024):
    for c in (1408, 1024, 704, 512, 384, 256, 128, 64, 32, 16, 8):
        if rows % c == 0 and c * width * itemsize <= budget:
            return c
    return rows


def _dg(a, b, ta, tb):
    dims = (((0 if ta else 1,), (1 if tb else 0,)), ((), ()))
    return lax.dot_general(a, b, dims, preferred_element_type=F32)


def _split(x, n):
    parts = []
    for _ in range(n):
        h = x.astype(BF16)
        parts.append(h)
        x = x - h.astype(F32)
    return parts


def _mm(a, b, ta=False, tb=False):
    return _dg(a.astype(BF16), b.astype(BF16), ta, tb)


def _matmul(a, b, ta=False, tb=False, out_dtype=F32, name="matmul"):
    if ta:
        kdim, m = a.shape
    else:
        m, kdim = a.shape
    if tb:
        n, k2 = b.shape
    else:
        k2, n = b.shape
    assert kdim == k2, (a.shape, b.shape, ta, tb)
    sa, sb, so = a.dtype.itemsize, b.dtype.itemsize, jnp.dtype(out_dtype).itemsize
    tm, tn, tk = _matmul_tiles(m, n, kdim, ta, sa, sb, so)
    nk = kdim // tk

    def body(a_ref, b_ref, o_ref, *acc):
        part = _dg(a_ref[...].astype(BF16), b_ref[...].astype(BF16), ta, tb)
        if nk == 1:
            o_ref[...] = part.astype(o_ref.dtype)
            return
        kk = pl.program_id(2)

        @pl.when(kk == 0)
        def _():
            acc[0][...] = part

        @pl.when(kk > 0)
        def _():
            acc[0][...] += part

        @pl.when(kk == nk - 1)
        def _():
            o_ref[...] = acc[0][...].astype(o_ref.dtype)

    a_spec = pl.BlockSpec((tk, tm), lambda i, j, k: (k, i)) if ta else pl.BlockSpec((tm, tk), lambda i, j, k: (i, k))
    b_spec = pl.BlockSpec((tn, tk), lambda i, j, k: (j, k)) if tb else pl.BlockSpec((tk, tn), lambda i, j, k: (k, j))
    return pl.pallas_call(
        body, name=name,
        grid=(m // tm, n // tn, nk),
        in_specs=[a_spec, b_spec],
        out_specs=pl.BlockSpec((tm, tn), lambda i, j, k: (i, j)),
        out_shape=jax.ShapeDtypeStruct((m, n), out_dtype),
        scratch_shapes=[pltpu.VMEM((tm, tn), F32)] if nk > 1 else [],
        compiler_params=pltpu.CompilerParams(dimension_semantics=("parallel", "parallel", "arbitrary"),
                                             vmem_limit_bytes=_vmem_limit(_matmul_vmem(tm, tn, tk, nk, sa, sb, so))),
    )(a, b)


def _matmul_vmem(tm, tn, tk, nk, sa, sb, so):
    return 2 * (tm * tk * sa + tk * tn * sb + tm * tn * so) + tm * tn * 4 + (tm * tn * 4 if nk > 1 else 0)


def _matmul_tiles(m, n, kdim, ta, sa, sb, so):
    lane = (2048, 1024, 640, 512, 384, 256, 128)
    sublane = (2048, 1408, 1024, 704, 512, 384, 256, 128)
    divs = lambda dim, cands: [c for c in cands if dim % c == 0] or [dim]
    best = None
    for tm in divs(m, lane if ta else sublane):
        for tn in divs(n, lane):
            for tk in divs(kdim, sublane if ta else lane) + ([kdim] if kdim <= 2048 else []):
                nk, nm, nn = kdim // tk, m // tm, n // tn
                if _matmul_vmem(tm, tn, tk, nk, sa, sb, so) > MATMUL_VMEM_BUDGET:
                    continue
                a_bytes = m * kdim * sa * (nn if nk > 1 else 1)
                b_bytes = kdim * n * sb * (1 if (nk == 1 and nn == 1) else nm)
                cost = a_bytes + b_bytes + m * n * so + nm * nn * nk * GRID_STEP_BYTES
                if best is None or cost < best[0]:
                    best = (cost, tm, tn, tk)
    return best[1:]


@jax.custom_vjp
def dense(x, w):
    return _matmul(x.astype(BF16), w, name="dense_fwd")


def _dense_fwd(x, w):
    assert x.dtype == F32
    xb = x.astype(BF16)
    return _matmul(xb, w, name="dense_fwd"), (xb, w)


def _dense_bwd(res, dy):
    xb, w = res
    dyb = dy.astype(BF16)
    dx = _matmul(dyb, w, tb=True, out_dtype=F32, name="dense_dx")
    dw = _matmul(xb, dyb, ta=True, out_dtype=w.dtype, name="dense_dw")
    return dx, dw


dense.defvjp(_dense_fwd, _dense_bwd)


def _rms_fwd_call(x, g):
    rows, d = x.shape
    tr = _row_tile(rows, d)

    def body(x_ref, g_ref, y_ref):
        xv = x_ref[...]
        rstd = lax.rsqrt(jnp.mean(xv * xv, axis=1, keepdims=True) + RMS_EPS)
        y_ref[...] = (xv * rstd) * g_ref[...]

    return pl.pallas_call(
        body, name="rms_fwd", grid=(rows // tr,),
        in_specs=[pl.BlockSpec((tr, d), lambda i: (i, 0)), pl.BlockSpec((1, d), lambda i: (0, 0))],
        out_specs=pl.BlockSpec((tr, d), lambda i: (i, 0)),
        out_shape=jax.ShapeDtypeStruct((rows, d), F32),
        compiler_params=pltpu.CompilerParams(dimension_semantics=("parallel",)),
    )(x, g)


def _rms_bwd_call(x, g, dy):
    rows, d = x.shape
    tr = _row_tile(rows, d)

    def body(x_ref, g_ref, dy_ref, dx_ref, dg_ref):
        @pl.when(pl.program_id(0) == 0)
        def _():
            dg_ref[...] = jnp.zeros_like(dg_ref)

        xv = x_ref[...]
        dyv = dy_ref[...]
        rstd = lax.rsqrt(jnp.mean(xv * xv, axis=1, keepdims=True) + RMS_EPS)
        xhat = xv * rstd
        dxhat = dyv * g_ref[...]
        dx_ref[...] = rstd * (dxhat - xhat * jnp.mean(dxhat * xhat, axis=1, keepdims=True))
        dg_ref[...] += jnp.sum(dyv * xhat, axis=0, keepdims=True)

    return pl.pallas_call(
        body, name="rms_bwd", grid=(rows // tr,),
        in_specs=[pl.BlockSpec((tr, d), lambda i: (i, 0)), pl.BlockSpec((1, d), lambda i: (0, 0)),
                  pl.BlockSpec((tr, d), lambda i: (i, 0))],
        out_specs=[pl.BlockSpec((tr, d), lambda i: (i, 0)), pl.BlockSpec((1, d), lambda i: (0, 0))],
        out_shape=[jax.ShapeDtypeStruct((rows, d), F32), jax.ShapeDtypeStruct((1, d), F32)],
        compiler_params=pltpu.CompilerParams(dimension_semantics=("arbitrary",)),
    )(x, g, dy)


@jax.custom_vjp
def rmsnorm(x, g):
    return _rms_fwd_call(x, g)


rmsnorm.defvjp(lambda x, g: (_rms_fwd_call(x, g), (x, g)), lambda res, dy: tuple(_rms_bwd_call(res[0], res[1], dy)))


def _bcast_call(x, p, mul):
    rows, d = x.shape
    tr = _row_tile(rows, d)

    def body(x_ref, p_ref, y_ref):
        y_ref[...] = x_ref[...] * p_ref[...] if mul else x_ref[...] + p_ref[...]

    return pl.pallas_call(
        body, name="bcast_mul" if mul else "bcast_add", grid=(rows // tr,),
        in_specs=[pl.BlockSpec((tr, d), lambda i: (i, 0)), pl.BlockSpec((1, d), lambda i: (0, 0))],
        out_specs=pl.BlockSpec((tr, d), lambda i: (i, 0)),
        out_shape=jax.ShapeDtypeStruct((rows, d), F32),
        compiler_params=pltpu.CompilerParams(dimension_semantics=("parallel",)),
    )(x, p)


def _colsum_call(a, b=None):
    rows, d = a.shape
    tr = _row_tile(rows, d)
    ops = (a,) if b is None else (a, b)

    def body(*refs):
        o_ref = refs[-1]

        @pl.when(pl.program_id(0) == 0)
        def _():
            o_ref[...] = jnp.zeros_like(o_ref)

        v = refs[0][...] if b is None else refs[0][...] * refs[1][...]
        o_ref[...] += jnp.sum(v, axis=0, keepdims=True)

    return pl.pallas_call(
        body, name="colsum", grid=(rows // tr,),
        in_specs=[pl.BlockSpec((tr, d), lambda i: (i, 0))] * len(ops),
        out_specs=pl.BlockSpec((1, d), lambda i: (0, 0)),
        out_shape=jax.ShapeDtypeStruct((1, d), F32),
        compiler_params=pltpu.CompilerParams(dimension_semantics=("arbitrary",)),
    )(*ops)


@jax.custom_vjp
def bmul(x, p):
    return _bcast_call(x, p, True)


bmul.defvjp(lambda x, p: (_bcast_call(x, p, True), (x, p)),
            lambda res, dy: (_bcast_call(dy, res[1], True), _colsum_call(dy, res[0])))


@jax.custom_vjp
def badd(x, p):
    return _bcast_call(x, p, False)


badd.defvjp(lambda x, p: (_bcast_call(x, p, False), None), lambda res, dy: (dy, _colsum_call(dy)))


def _heads(width):
    return [slice(h * HEAD_DIM, (h + 1) * HEAD_DIM) for h in range(width // HEAD_DIM)]


def _head_rms_fwd_call(x, g):
    rows, w = x.shape
    tr = _row_tile(rows, w, budget=1024 * 1024)

    def body(x_ref, g_ref, y_ref):
        for sl in _heads(w):
            xv = x_ref[:, sl]
            rstd = lax.rsqrt(jnp.mean(xv * xv, axis=1, keepdims=True) + RMS_EPS)
            y_ref[:, sl] = (xv * rstd) * g_ref[...]

    return pl.pallas_call(
        body, name="head_rms_fwd", grid=(rows // tr,),
        in_specs=[pl.BlockSpec((tr, w), lambda i: (i, 0)), pl.BlockSpec((1, HEAD_DIM), lambda i: (0, 0))],
        out_specs=pl.BlockSpec((tr, w), lambda i: (i, 0)),
        out_shape=jax.ShapeDtypeStruct((rows, w), F32),
        compiler_params=pltpu.CompilerParams(dimension_semantics=("parallel",)),
    )(x, g)


def _head_rms_bwd_call(x, g, dy):
    rows, w = x.shape
    tr = _row_tile(rows, w, budget=1024 * 1024)

    def body(x_ref, g_ref, dy_ref, dx_ref, dg_ref):
        @pl.when(pl.program_id(0) == 0)
        def _():
            dg_ref[...] = jnp.zeros_like(dg_ref)

        dg = jnp.zeros((1, HEAD_DIM), F32)
        for sl in _heads(w):
            xv = x_ref[:, sl]
            dyv = dy_ref[:, sl]
            rstd = lax.rsqrt(jnp.mean(xv * xv, axis=1, keepdims=True) + RMS_EPS)
            xhat = xv * rstd
            dxhat = dyv * g_ref[...]
            dx_ref[:, sl] = rstd * (dxhat - xhat * jnp.mean(dxhat * xhat, axis=1, keepdims=True))
            dg = dg + jnp.sum(dyv * xhat, axis=0, keepdims=True)
        dg_ref[...] += dg

    return pl.pallas_call(
        body, name="head_rms_bwd", grid=(rows // tr,),
        in_specs=[pl.BlockSpec((tr, w), lambda i: (i, 0)), pl.BlockSpec((1, HEAD_DIM), lambda i: (0, 0)),
                  pl.BlockSpec((tr, w), lambda i: (i, 0))],
        out_specs=[pl.BlockSpec((tr, w), lambda i: (i, 0)), pl.BlockSpec((1, HEAD_DIM), lambda i: (0, 0))],
        out_shape=[jax.ShapeDtypeStruct((rows, w), F32), jax.ShapeDtypeStruct((1, HEAD_DIM), F32)],
        compiler_params=pltpu.CompilerParams(dimension_semantics=("arbitrary",)),
    )(x, g, dy)


@jax.custom_vjp
def head_rms(x, g):
    return _head_rms_fwd_call(x, g)


head_rms.defvjp(lambda x, g: (_head_rms_fwd_call(x, g), (x, g)),
                lambda res, dy: tuple(_head_rms_bwd_call(res[0], res[1], dy)))


def _head_l2_call(x, dy=None):
    rows, w = x.shape
    tr = _row_tile(rows, w, budget=1024 * 1024)
    ops = (x,) if dy is None else (x, dy)

    def body(*refs):
        o_ref = refs[-1]
        for sl in _heads(w):
            xv = refs[0][:, sl]
            nrm = jnp.sqrt(jnp.sum(xv * xv, axis=1, keepdims=True))
            live = nrm > L2_FLOOR
            inv = 1.0 / jnp.maximum(nrm, L2_FLOOR)
            y = xv * inv
            if dy is None:
                o_ref[:, sl] = y
            else:
                dyv = refs[1][:, sl]
                proj = jnp.where(live, jnp.sum(dyv * y, axis=1, keepdims=True), 0.0)
                o_ref[:, sl] = (dyv - y * proj) * inv

    return pl.pallas_call(
        body, name="head_l2_fwd" if dy is None else "head_l2_bwd", grid=(rows // tr,),
        in_specs=[pl.BlockSpec((tr, w), lambda i: (i, 0))] * len(ops),
        out_specs=pl.BlockSpec((tr, w), lambda i: (i, 0)),
        out_shape=jax.ShapeDtypeStruct((rows, w), F32),
        compiler_params=pltpu.CompilerParams(dimension_semantics=("parallel",)),
    )(*ops)


@jax.custom_vjp
def head_l2norm(x):
    return _head_l2_call(x)


head_l2norm.defvjp(lambda x: (_head_l2_call(x), x), lambda x, dy: (_head_l2_call(x, dy),))


def _gn_fwd_call(y, r, kf, v, gw, gb, rk):
    rows, w = y.shape
    tr = _row_tile(rows, w, budget=512 * 1024)

    def body(y_ref, r_ref, kf_ref, v_ref, gw_ref, gb_ref, rk_ref, o_ref):
        for sl in _heads(w):
            yv = y_ref[:, sl]
            yc = yv - jnp.mean(yv, axis=1, keepdims=True)
            rstd = lax.rsqrt(jnp.mean(yc * yc, axis=1, keepdims=True) + GN_EPS)
            s = jnp.sum(r_ref[:, sl] * kf_ref[:, sl] * rk_ref[:, sl], axis=1, keepdims=True)
            o_ref[:, sl] = (yc * rstd) * gw_ref[:, sl] + gb_ref[:, sl] + s * v_ref[:, sl]

    tok = pl.BlockSpec((tr, w), lambda i: (i, 0))
    par = pl.BlockSpec((1, w), lambda i: (0, 0))
    return pl.pallas_call(
        body, name="gn_bonus_fwd", grid=(rows // tr,),
        in_specs=[tok] * 4 + [par] * 3, out_specs=tok,
        out_shape=jax.ShapeDtypeStruct((rows, w), F32),
        compiler_params=pltpu.CompilerParams(dimension_semantics=("parallel",)),
    )(y, r, kf, v, gw, gb, rk)


def _gn_bwd_call(y, r, kf, v, gw, gb, rk, do):
    rows, w = y.shape
    tr = _row_tile(rows, w, budget=512 * 1024)

    def body(y_ref, r_ref, kf_ref, v_ref, gw_ref, rk_ref, do_ref,
             dy_ref, dr_ref, dkf_ref, dv_ref, dgw_ref, dgb_ref, drk_ref):
        @pl.when(pl.program_id(0) == 0)
        def _():
            dgw_ref[...] = jnp.zeros_like(dgw_ref)
            dgb_ref[...] = jnp.zeros_like(dgb_ref)
            drk_ref[...] = jnp.zeros_like(drk_ref)

        for sl in _heads(w):
            yv, rv, kv, vv, dov = y_ref[:, sl], r_ref[:, sl], kf_ref[:, sl], v_ref[:, sl], do_ref[:, sl]
            yc = yv - jnp.mean(yv, axis=1, keepdims=True)
            rstd = lax.rsqrt(jnp.mean(yc * yc, axis=1, keepdims=True) + GN_EPS)
            yhat = yc * rstd
            dyhat = dov * gw_ref[:, sl]
            dy_ref[:, sl] = rstd * (dyhat - jnp.mean(dyhat, axis=1, keepdims=True)
                                    - yhat * jnp.mean(dyhat * yhat, axis=1, keepdims=True))
            rkv = rk_ref[:, sl]
            s = jnp.sum(rv * kv * rkv, axis=1, keepdims=True)
            ds = jnp.sum(dov * vv, axis=1, keepdims=True)
            dv_ref[:, sl] = s * dov
            dr_ref[:, sl] = ds * kv * rkv
            dkf_ref[:, sl] = ds * rv * rkv
            dgw_ref[:, sl] += jnp.sum(dov * yhat, axis=0, keepdims=True)
            dgb_ref[:, sl] += jnp.sum(dov, axis=0, keepdims=True)
            drk_ref[:, sl] += jnp.sum(ds * rv * kv, axis=0, keepdims=True)

    tok = pl.BlockSpec((tr, w), lambda i: (i, 0))
    par = pl.BlockSpec((1, w), lambda i: (0, 0))
    tshape = jax.ShapeDtypeStruct((rows, w), F32)
    pshape = jax.ShapeDtypeStruct((1, w), F32)
    return pl.pallas_call(
        body, name="gn_bonus_bwd", grid=(rows // tr,),
        in_specs=[tok] * 4 + [par] * 2 + [tok], out_specs=[tok] * 4 + [par] * 3,
        out_shape=[tshape] * 4 + [pshape] * 3,
        compiler_params=pltpu.CompilerParams(dimension_semantics=("arbitrary",)),
    )(y, r, kf, v, gw, rk, do)


@jax.custom_vjp
def gn_bonus(y, r, kf, v, gw, gb, rk):
    return _gn_fwd_call(y, r, kf, v, gw, gb, rk)


def _gn_bwd(res, do):
    y, r, kf, v, gw, gb, rk = res
    dy, dr, dkf, dv, dgw, dgb, drk = _gn_bwd_call(y, r, kf, v, gw, gb, rk, do)
    return dy, dr, dkf, dv, dgw, dgb, drk


gn_bonus.defvjp(lambda *a: (_gn_fwd_call(*a), a), _gn_bwd)


def _pair_masks(rows):
    lane = lax.broadcasted_iota(jnp.int32, (rows, PAIR), 1)
    return lane < HEAD_DIM, lane >= HEAD_DIM


def _bd(x):
    m0, m1 = _pair_masks(x.shape[0])
    return jnp.concatenate([jnp.where(m0, x, 0.0), jnp.where(m1, x, 0.0)], axis=0)


def _unbd(m, c):
    return jnp.where(_pair_masks(c)[0], m[:c], m[c:])


def _pair_a(l2, r2):
    return _mm(l2, _bd(r2), tb=True)


def _pair_mul(p2, x2):
    return _mm(p2, _bd(x2))


def _pair_mul_t(p2, x2):
    return _unbd(_mm(p2, x2, ta=True), p2.shape[0])


def _block_diag_mask():
    row = lax.broadcasted_iota(jnp.int32, (PAIR, PAIR), 0)
    lane = lax.broadcasted_iota(jnp.int32, (PAIR, PAIR), 1)
    return (row < HEAD_DIM) == (lane < HEAD_DIM), row == lane


def _wkv_pair_common(r, lw, k, a, b):
    c = r[0].shape[0]
    pairs = range(len(r))
    i = lax.broadcasted_iota(jnp.int32, (c, PAIR), 0)
    j = lax.broadcasted_iota(jnp.int32, (c, PAIR), 1) % c
    strict, incl = i > j, i >= j
    ti = lax.broadcasted_iota(jnp.int32, (c, c), 0)
    tj = lax.broadcasted_iota(jnp.int32, (c, c), 1)
    tri = jnp.where(ti >= tj, 1.0, 0.0).astype(BF16)
    lc = [sum(_dg(tri, part, False, False) for part in _split(lw[p], 3)) for p in pairs]
    lend = [lc[p][c - 1:c, :] for p in pairs]
    rt = [r[p] * jnp.exp(lc[p]) for p in pairs]
    at = [a[p] * jnp.exp(lc[p] - lw[p]) for p in pairs]
    pinv = [jnp.exp(-lc[p]) for p in pairs]
    kt = [k[p] * pinv[p] for p in pairs]
    bt = [b[p] * pinv[p] for p in pairs]
    e = [jnp.exp(lend[p] - lc[p]) for p in pairs]
    ktp = [k[p] * e[p] for p in pairs]
    btp = [b[p] * e[p] for p in pairs]
    a_ab = [jnp.where(strict, _pair_a(at[p], bt[p]), 0.0) for p in pairs]
    a_ak = [jnp.where(strict, _pair_a(at[p], kt[p]), 0.0) for p in pairs]
    a_rb = [jnp.where(incl, _pair_a(rt[p], bt[p]), 0.0) for p in pairs]
    a_rk = [jnp.where(incl, _pair_a(rt[p], kt[p]), 0.0) for p in pairs]
    t = [jnp.where(i == j, 1.0, 0.0) + a_ab[p] for p in pairs]
    xp = a_ab
    n = 2
    while n < c:
        xp = [_pair_mul(xp[p], xp[p]) for p in pairs]
        t = [t[p] + _pair_mul(t[p], xp[p]) for p in pairs]
        n *= 2
    bdm, eye = _block_diag_mask()
    pend_col = [jnp.sum(jnp.where(eye, jnp.exp(lend[p]), 0.0), axis=1, keepdims=True) for p in pairs]
    return dict(rt=rt, at=at, kt=kt, bt=bt, ktp=ktp, btp=btp, a_ak=a_ak, a_rb=a_rb, a_rk=a_rk, t=t,
                pend_col=pend_col, lend=lend, lc=lc, strict=strict, incl=incl, tri=tri, bdm=bdm)


def _wkv_group(width):
    npair = width // PAIR
    g = min(WKV_PAIRS_PER_STEP, npair)
    assert npair % g == 0
    return npair, g


def _wkv_fwd_call(r, lw, k, v, a, b):
    tokens, width = r.shape
    c = WKV_CHUNK
    nc = tokens // c
    npair, g = _wkv_group(width)

    def body(r_ref, lw_ref, k_ref, v_ref, a_ref, b_ref, y_ref, s_ref, st):
        @pl.when(pl.program_id(1) == 0)
        def _():
            st[...] = jnp.zeros_like(st)

        pairs = range(g)
        rv, lwv, kv, vv, av, bv = ([ref[:, p * PAIR:(p + 1) * PAIR] for p in pairs]
                                   for ref in (r_ref, lw_ref, k_ref, v_ref, a_ref, b_ref))
        s0 = [st[p] for p in pairs]
        q = _wkv_pair_common(rv, lwv, kv, av, bv)
        w1 = [_mm(q["at"][p], s0[p]) + _pair_mul(q["a_ak"][p], vv[p]) for p in pairs]
        u = [_pair_mul(q["t"][p], w1[p]) for p in pairs]
        y = [_mm(q["rt"][p], s0[p]) + _pair_mul(q["a_rb"][p], u[p]) + _pair_mul(q["a_rk"][p], vv[p]) for p in pairs]
        grow = [_mm(jnp.concatenate([q["btp"][p], q["ktp"][p]], axis=0), jnp.concatenate([u[p], vv[p]], axis=0), ta=True)
                for p in pairs]
        for p in pairs:
            y_ref[:, p * PAIR:(p + 1) * PAIR] = y[p]
            s_ref[0, p] = s0[p]
            st[p] = q["pend_col"][p] * s0[p] + jnp.where(q["bdm"], grow[p], 0.0)

    tok = pl.BlockSpec((c, g * PAIR), lambda gi, ci: (ci, gi))
    return pl.pallas_call(
        body, name="wkv_fwd", grid=(npair // g, nc),
        in_specs=[tok] * 6,
        out_specs=[tok, pl.BlockSpec((1, g, PAIR, PAIR), lambda gi, ci: (ci, gi, 0, 0))],
        out_shape=[jax.ShapeDtypeStruct((tokens, width), F32), jax.ShapeDtypeStruct((nc, npair, PAIR, PAIR), F32)],
        scratch_shapes=[pltpu.VMEM((g, PAIR, PAIR), F32)],
        compiler_params=pltpu.CompilerParams(dimension_semantics=("parallel", "arbitrary")),
    )(r, lw, k, v, a, b)


def _wkv_bwd_call(r, lw, k, v, a, b, s, dy):
    tokens, width = r.shape
    c = WKV_CHUNK
    nc = tokens // c
    npair, g = _wkv_group(width)

    def body(r_ref, lw_ref, k_ref, v_ref, a_ref, b_ref, s_ref, dy_ref,
             dr_ref, dlw_ref, dk_ref, dv_ref, da_ref, db_ref, dst):
        @pl.when(pl.program_id(1) == 0)
        def _():
            dst[...] = jnp.zeros_like(dst)

        pairs = range(g)
        rv, lwv, kv, vv, av, bv, dyv = ([ref[:, p * PAIR:(p + 1) * PAIR] for p in pairs]
                                        for ref in (r_ref, lw_ref, k_ref, v_ref, a_ref, b_ref, dy_ref))
        s0 = [s_ref[0, p] for p in pairs]
        dsc = [dst[p] for p in pairs]
        q = _wkv_pair_common(rv, lwv, kv, av, bv)
        rt, at, kt, bt, ktp, btp, t = (q[n] for n in ("rt", "at", "kt", "bt", "ktp", "btp", "t"))
        a_ak, a_rb, a_rk, strict, incl = (q[n] for n in ("a_ak", "a_rb", "a_rk", "strict", "incl"))
        w1 = [_mm(at[p], s0[p]) + _pair_mul(a_ak[p], vv[p]) for p in pairs]
        u = [_pair_mul(t[p], w1[p]) for p in pairs]
        du = [_pair_mul_t(a_rb[p], dyv[p]) + _mm(btp[p], dsc[p]) for p in pairs]
        dw1 = [_pair_mul_t(t[p], du[p]) for p in pairs]
        dv = [_pair_mul_t(a_rk[p], dyv[p]) + _mm(ktp[p], dsc[p]) + _pair_mul_t(a_ak[p], dw1[p]) for p in pairs]
        da_ab = [jnp.where(strict, _pair_a(dw1[p], u[p]), 0.0) for p in pairs]
        da_ak = [jnp.where(strict, _pair_a(dw1[p], vv[p]), 0.0) for p in pairs]
        da_rb = [jnp.where(incl, _pair_a(dyv[p], u[p]), 0.0) for p in pairs]
        da_rk = [jnp.where(incl, _pair_a(dyv[p], vv[p]), 0.0) for p in pairs]
        d_rt = [_mm(dyv[p], s0[p], tb=True) + _pair_mul(da_rb[p], bt[p]) + _pair_mul(da_rk[p], kt[p]) for p in pairs]
        d_at = [_mm(dw1[p], s0[p], tb=True) + _pair_mul(da_ab[p], bt[p]) + _pair_mul(da_ak[p], kt[p]) for p in pairs]
        d_bt = [_pair_mul_t(da_ab[p], at[p]) + _pair_mul_t(da_rb[p], rt[p]) for p in pairs]
        d_kt = [_pair_mul_t(da_ak[p], at[p]) + _pair_mul_t(da_rk[p], rt[p]) for p in pairs]
        d_btp = [_mm(u[p], dsc[p], tb=True) for p in pairs]
        d_ktp = [_mm(vv[p], dsc[p], tb=True) for p in pairs]
        ones = jnp.ones((8, PAIR), BF16)
        dpend = [sum(_dg(ones, part, False, True) for part in _split(dsc[p] * s0[p], 3))[0:1, :] * jnp.exp(q["lend"][p])
                 for p in pairs]
        grow = [_mm(jnp.concatenate([rt[p], at[p]], axis=0), jnp.concatenate([dyv[p], dw1[p]], axis=0), ta=True)
                for p in pairs]
        last = lax.broadcasted_iota(jnp.int32, (c, PAIR), 0) == c - 1
        for p in pairs:
            sl = slice(p * PAIR, (p + 1) * PAIR)
            dst[p] = q["pend_col"][p] * dsc[p] + jnp.where(q["bdm"], grow[p], 0.0)
            lc_e = d_ktp[p] * ktp[p] + d_btp[p] * btp[p]
            dlend = jnp.sum(lc_e, axis=0, keepdims=True) + dpend[p]
            dlc = d_rt[p] * rt[p] - d_kt[p] * kt[p] - d_bt[p] * bt[p] - lc_e + jnp.where(last, dlend, 0.0)
            dlp = d_at[p] * at[p]
            dlw_ref[:, sl] = sum(_dg(q["tri"], part, True, False) for part in _split(dlc + dlp, 3)) - dlp
            lc = q["lc"][p]
            pinv = jnp.exp(-lc)
            e = jnp.exp(q["lend"][p] - lc)
            dr_ref[:, sl] = d_rt[p] * jnp.exp(lc)
            da_ref[:, sl] = d_at[p] * jnp.exp(lc - lwv[p])
            dk_ref[:, sl] = d_kt[p] * pinv + d_ktp[p] * e
            db_ref[:, sl] = d_bt[p] * pinv + d_btp[p] * e
            dv_ref[:, sl] = dv[p]

    tok = pl.BlockSpec((c, g * PAIR), lambda gi, ci: (nc - 1 - ci, gi))
    tshape = jax.ShapeDtypeStruct((tokens, width), F32)
    return pl.pallas_call(
        body, name="wkv_bwd", grid=(npair // g, nc),
        in_specs=[tok] * 6 + [pl.BlockSpec((1, g, PAIR, PAIR), lambda gi, ci: (nc - 1 - ci, gi, 0, 0)), tok],
        out_specs=[tok] * 6, out_shape=[tshape] * 6,
        scratch_shapes=[pltpu.VMEM((g, PAIR, PAIR), F32)],
        compiler_params=pltpu.CompilerParams(dimension_semantics=("parallel", "arbitrary")),
    )(r, lw, k, v, a, b, s, dy)


@jax.custom_vjp
def wkv7(r, lw, k, v, a, b):
    return _wkv_fwd_call(r, lw, k, v, a, b)[0]


def _wkv7_fwd(r, lw, k, v, a, b):
    y, s = _wkv_fwd_call(r, lw, k, v, a, b)
    return y, (r, lw, k, v, a, b, s)


wkv7.defvjp(_wkv7_fwd, lambda res, dy: tuple(_wkv_bwd_call(*res, dy)))


def _attn_block(tokens):
    return ATTN_BLOCK_BIG if tokens % ATTN_BLOCK_BIG == 0 else ATTN_BLOCK


def _fox_layouts(cum):
    tokens, heads = cum.shape
    t = _attn_block(tokens)
    cq = cum.reshape(tokens, heads // 2, 2).transpose(1, 0, 2)
    ck = cum.T.reshape(heads // 2, 2, tokens // t, t).transpose(0, 2, 1, 3)
    return cq, ck


def _head_lane_masks(rows):
    lane = lax.broadcasted_iota(jnp.int32, (rows, 2 * HEAD_DIM), 1)
    return [lane < HEAD_DIM, lane >= HEAD_DIM]


def _fox_fwd_call(q, k, v, cq, ck):
    tokens, width = q.shape
    t = _attn_block(tokens)
    nb = tokens // t
    hd = HEAD_DIM
    npair = width // (2 * hd)

    def body(q_ref, k_ref, v_ref, cq_ref, ck_ref, o_ref, lse_ref):
        i = pl.program_id(1)
        masks = _head_lane_masks(t)
        q2 = q_ref[...]
        qs = [jnp.where(mk, q2, 0.0).astype(BF16) for mk in masks]
        cqs = [cq_ref[0, :, hh:hh + 1] for hh in range(2)]

        def block(j, carry, diagonal):
            off = pl.multiple_of(j * t, t)
            ckj = ck_ref[0, j]
            k2 = k_ref[pl.ds(off, t), :].astype(BF16)
            v2 = v_ref[pl.ds(off, t), :].astype(BF16)
            out = []
            for hh in range(2):
                m, l, acc = carry[hh]
                s = _dg(qs[hh], k2, False, True) + (cqs[hh] - ckj[hh:hh + 1, :])
                if diagonal:
                    keep = lax.broadcasted_iota(jnp.int32, (t, t), 0) >= lax.broadcasted_iota(jnp.int32, (t, t), 1)
                    s = jnp.where(keep, s, NEG_BIG)
                m_new = jnp.maximum(m, jnp.max(s, axis=1, keepdims=True))
                alpha = jnp.exp(m - m_new)
                p = jnp.exp(s - m_new)
                l = alpha * l + jnp.sum(p, axis=1, keepdims=True)
                acc = alpha * acc + _dg(p.astype(BF16), v2, False, False)
                out.append((m_new, l, acc))
            return tuple(out)

        init = tuple((jnp.full((t, 1), NEG_BIG, F32), jnp.zeros((t, 1), F32), jnp.zeros((t, 2 * hd), F32)) for _ in range(2))
        res = lax.fori_loop(0, i, lambda j, c: block(j, c, False), init)
        res = block(i, res, True)
        o_ref[...] = jnp.where(masks[0], res[0][2] / res[0][1], res[1][2] / res[1][1])
        for hh in range(2):
            lse_ref[0, :, hh:hh + 1] = res[hh][0] + jnp.log(res[hh][1])

    blk = pl.BlockSpec((t, 2 * hd), lambda hp, i: (i, hp))
    full = pl.BlockSpec((tokens, 2 * hd), lambda hp, i: (0, hp))
    cq_spec = pl.BlockSpec((1, t, 2), lambda hp, i: (hp, i, 0))
    ck_spec = pl.BlockSpec((1, nb, 2, t), lambda hp, i: (hp, 0, 0, 0))
    return pl.pallas_call(
        body, name="fox_fwd", grid=(npair, nb),
        in_specs=[blk, full, full, cq_spec, ck_spec],
        out_specs=[blk, cq_spec],
        out_shape=[jax.ShapeDtypeStruct((tokens, width), F32), jax.ShapeDtypeStruct((npair, tokens, 2), F32)],
        compiler_params=pltpu.CompilerParams(dimension_semantics=("parallel", "arbitrary")),
    )(q, k, v, cq, ck)


def _fox_bwd_call(q, k, v, cq, ck, o, lse, do):
    tokens, width = q.shape
    t = _attn_block(tokens)
    nb = tokens // t
    hd = HEAD_DIM
    npair = width // (2 * hd)

    def body(q_ref, k_ref, v_ref, cq_ref, ck_ref, o_ref, lse_ref, do_ref, dq_ref, dk_ref, dv_ref, dck_ref, dcq_ref):
        i = pl.program_id(1)

        @pl.when(i == 0)
        def _():
            dk_ref[...] = jnp.zeros_like(dk_ref)
            dv_ref[...] = jnp.zeros_like(dv_ref)
            dck_ref[...] = jnp.zeros_like(dck_ref)

        masks = _head_lane_masks(t)
        q2, do2, o2 = q_ref[...], do_ref[...], o_ref[...]
        qs = [jnp.where(mk, q2, 0.0).astype(BF16) for mk in masks]
        dos = [jnp.where(mk, do2, 0.0).astype(BF16) for mk in masks]
        deltas = [jnp.sum(dos[hh].astype(F32) * o2, axis=1, keepdims=True) for hh in range(2)]
        bias = [cq_ref[0, :, hh:hh + 1] - lse_ref[0, :, hh:hh + 1] for hh in range(2)]

        def block(j, carry, diagonal):
            off = pl.multiple_of(j * t, t)
            ckj = ck_ref[0, j]
            k2 = k_ref[pl.ds(off, t), :].astype(BF16)
            v2 = v_ref[pl.ds(off, t), :].astype(BF16)
            out = []
            dk2 = jnp.zeros((t, 2 * hd), F32)
            dv2 = jnp.zeros((t, 2 * hd), F32)
            for hh in range(2):
                s = _dg(qs[hh], k2, False, True) + (bias[hh] - ckj[hh:hh + 1, :])
                if diagonal:
                    keep = lax.broadcasted_iota(jnp.int32, (t, t), 0) >= lax.broadcasted_iota(jnp.int32, (t, t), 1)
                    s = jnp.where(keep, s, NEG_BIG)
                p = jnp.exp(s)
                dp = _dg(dos[hh], v2, False, True)
                ds = p * (dp - deltas[hh])
                dsb = ds.astype(BF16)
                dq, rowsum = carry[hh]
                out.append((dq + _dg(dsb, k2, False, False), rowsum + jnp.sum(ds, axis=1, keepdims=True)))
                dk2 = dk2 + _dg(dsb, qs[hh], True, False)
                dv2 = dv2 + _dg(p.astype(BF16), dos[hh], True, False)
                dck_ref[0, j, hh:hh + 1, :] -= jnp.sum(ds, axis=0, keepdims=True)
            dk_ref[pl.ds(off, t), :] += dk2
            dv_ref[pl.ds(off, t), :] += dv2
            return tuple(out)

        init = tuple((jnp.zeros((t, 2 * hd), F32), jnp.zeros((t, 1), F32)) for _ in range(2))
        res = lax.fori_loop(0, i, lambda j, c: block(j, c, False), init)
        res = block(i, res, True)
        dq_ref[...] = jnp.where(masks[0], res[0][0], res[1][0])
        for hh in range(2):
            dcq_ref[0, :, hh:hh + 1] = res[hh][1]

    blk = pl.BlockSpec((t, 2 * hd), lambda hp, i: (i, hp))
    full = pl.BlockSpec((tokens, 2 * hd), lambda hp, i: (0, hp))
    cq_spec = pl.BlockSpec((1, t, 2), lambda hp, i: (hp, i, 0))
    ck_spec = pl.BlockSpec((1, nb, 2, t), lambda hp, i: (hp, 0, 0, 0))
    tshape = jax.ShapeDtypeStruct((tokens, width), F32)
    return pl.pallas_call(
        body, name="fox_bwd", grid=(npair, nb),
        in_specs=[blk, full, full, cq_spec, ck_spec, blk, cq_spec, blk],
        out_specs=[blk, full, full, ck_spec, cq_spec],
        out_shape=[tshape, tshape, tshape, jax.ShapeDtypeStruct((npair, nb, 2, t), F32),
                   jax.ShapeDtypeStruct((npair, tokens, 2), F32)],
        compiler_params=pltpu.CompilerParams(dimension_semantics=("parallel", "arbitrary")),
    )(q, k, v, cq, ck, o, lse, do)


@jax.custom_vjp
def fox_attention(q, k, v, cum):
    return _fox_fwd_call(q, k, v, *_fox_layouts(cum))[0]


def _fox_fwd(q, k, v, cum):
    cq, ck = _fox_layouts(cum)
    o, lse = _fox_fwd_call(q, k, v, cq, ck)
    return o, (q, k, v, cq, ck, o, lse)


def _fox_bwd(res, do):
    q, k, v, cq, ck, o, lse = res
    dq, dk, dv, dck, dcq = _fox_bwd_call(q, k, v, cq, ck, o, lse, do)
    npair, nb, _, t = dck.shape
    dcum = dck.transpose(0, 2, 1, 3).reshape(2 * npair, nb * t).T + dcq.transpose(1, 0, 2).reshape(nb * t, 2 * npair)
    return dq, dk, dv, dcum


fox_attention.defvjp(_fox_fwd, _fox_bwd)


def _loss_call(y, target):
    rows, d = y.shape
    tr = _row_tile(rows, d)

    def body(y_ref, t_ref, loss_ref, dy_ref):
        @pl.when(pl.program_id(0) == 0)
        def _():
            loss_ref[...] = jnp.zeros_like(loss_ref)

        diff = y_ref[...] - t_ref[...]
        dy_ref[...] = diff * (1.0 / d)
        loss_ref[...] += (0.5 / d) * jnp.sum(jnp.sum(diff * diff, axis=1, keepdims=True), axis=0, keepdims=True)

    return pl.pallas_call(
        body, name="loss", grid=(rows // tr,),
        in_specs=[pl.BlockSpec((tr, d), lambda i: (i, 0))] * 2,
        out_specs=[pl.BlockSpec((1, 1), lambda i: (0, 0)), pl.BlockSpec((tr, d), lambda i: (i, 0))],
        out_shape=[jax.ShapeDtypeStruct((1, 1), F32), jax.ShapeDtypeStruct((rows, d), F32)],
        compiler_params=pltpu.CompilerParams(dimension_semantics=("arbitrary",)),
    )(y, target)


def _adamw_call(w, g, m, v):
    rows, cols = w.shape
    tr = _row_tile(rows, cols, budget=1024 * 1024)
    c1 = 1.0 / (1.0 - ADAM_B1 ** ADAM_STEP)
    c2 = 1.0 / (1.0 - ADAM_B2 ** ADAM_STEP)

    def body(w_ref, g_ref, m_ref, v_ref, d_ref, nm_ref, nv_ref):
        gv = g_ref[...]
        nm = ADAM_B1 * m_ref[...] + (1.0 - ADAM_B1) * gv
        nv = ADAM_B2 * v_ref[...] + (1.0 - ADAM_B2) * (gv * gv)
        nm_ref[...] = nm
        nv_ref[...] = nv
        d_ref[...] = -ADAM_LR * ((nm * c1) / (jnp.sqrt(nv * c2) + ADAM_EPS) + ADAM_WD * w_ref[...])

    spec = pl.BlockSpec((tr, cols), lambda i: (i, 0))
    shape = jax.ShapeDtypeStruct((rows, cols), F32)
    return pl.pallas_call(
        body, name="adamw", grid=(rows // tr,),
        in_specs=[spec] * 4, out_specs=[spec] * 3, out_shape=[shape] * 3,
        compiler_params=pltpu.CompilerParams(dimension_semantics=("parallel",)),
    )(w, g, m, v)


def _my_place():
    return lax.axis_index("x"), lax.axis_index("y"), lax.axis_index("c")


def _place_index(px, py, pc):
    return 4 * px + 2 * py + pc


HBM_SPEC = pl.BlockSpec(memory_space=pltpu.HBM)


def _all_gather_call(block):
    def body(x_ref, out_ref, send_sems, recv_sems, local_sem):
        x, y, c = _my_place()
        me, sibling = (x, y, c), (x, y, 1 - c)
        chips = [(1 - x, y), (x, 1 - y), (1 - x, 1 - y)]

        def slot(px, py, pc):
            return out_ref.at[_place_index(px, py, pc)]

        def copy(k, blk, to, src=None):
            return pltpu.make_async_remote_copy(
                src_ref=slot(*blk) if src is None else src, dst_ref=slot(*blk),
                send_sem=send_sems.at[k], recv_sem=recv_sems.at[k],
                device_id=to, device_id_type=pl.DeviceIdType.MESH)

        mine = pltpu.make_async_copy(x_ref, slot(*me), local_sem)
        mine.start()
        first = [copy(0, me, sibling, src=x_ref)]
        first += [copy(1 + j, me, (*chip, c), src=x_ref) for j, chip in enumerate(chips)]
        for cp in first:
            cp.start()
        passed = [copy(4 + j, (*chip, c), sibling) for j, chip in enumerate(chips)]
        for j, chip in enumerate(chips):
            copy(1 + j, (*chip, c), me).wait_recv()
            passed[j].start()
        copy(0, sibling, me).wait_recv()
        for j, chip in enumerate(chips):
            copy(4 + j, (*chip, 1 - c), me).wait_recv()
        for cp in first + passed:
            cp.wait_send()
        mine.wait()

    return pl.pallas_call(
        body, name="all_gather",
        out_shape=jax.ShapeDtypeStruct((N_DEV,) + block.shape, block.dtype),
        in_specs=[HBM_SPEC], out_specs=HBM_SPEC,
        scratch_shapes=[pltpu.SemaphoreType.DMA((7,)), pltpu.SemaphoreType.DMA((7,)), pltpu.SemaphoreType.DMA],
    )(block)


def _exchange_call(blocks):
    def body(g_ref, out_ref, send_sems, recv_sems, local_sem):
        x, y, c = _my_place()
        mine_idx = _place_index(x, y, c)
        mine = pltpu.make_async_copy(g_ref.at[mine_idx], out_ref.at[mine_idx], local_sem)
        mine.start()
        copies, arrivals = [], []
        for k in range(1, N_DEV):
            peer = (x ^ (k >> 2), y ^ ((k >> 1) & 1), c ^ (k & 1))
            peer_idx = _place_index(*peer)
            copies.append(pltpu.make_async_remote_copy(
                src_ref=g_ref.at[peer_idx], dst_ref=out_ref.at[mine_idx],
                send_sem=send_sems.at[k - 1], recv_sem=recv_sems.at[k - 1],
                device_id=peer, device_id_type=pl.DeviceIdType.MESH))
            arrivals.append(pltpu.make_async_remote_copy(
                src_ref=g_ref.at[mine_idx], dst_ref=out_ref.at[peer_idx],
                send_sem=send_sems.at[k - 1], recv_sem=recv_sems.at[k - 1],
                device_id=peer, device_id_type=pl.DeviceIdType.MESH))
        for cp in copies:
            cp.start()
        for cp in arrivals:
            cp.wait_recv()
        for cp in copies:
            cp.wait_send()
        mine.wait()

    return pl.pallas_call(
        body, name="grad_exchange",
        out_shape=jax.ShapeDtypeStruct(blocks.shape, blocks.dtype),
        in_specs=[HBM_SPEC], out_specs=HBM_SPEC,
        scratch_shapes=[pltpu.SemaphoreType.DMA((7,)), pltpu.SemaphoreType.DMA((7,)), pltpu.SemaphoreType.DMA],
    )(blocks)


def _sum_slots_call(slots):
    _, rows, cols = slots.shape
    tr = _row_tile(rows, cols, budget=512 * 1024)

    def body(s_ref, o_ref):
        acc = s_ref[0].astype(F32)
        for j in range(1, N_DEV):
            acc = acc + s_ref[j].astype(F32)
        o_ref[...] = acc

    return pl.pallas_call(
        body, name="sum_slots", grid=(rows // tr,),
        in_specs=[pl.BlockSpec((N_DEV, tr, cols), lambda i: (0, i, 0))],
        out_specs=pl.BlockSpec((tr, cols), lambda i: (i, 0)),
        out_shape=jax.ShapeDtypeStruct((rows, cols), F32),
        compiler_params=pltpu.CompilerParams(dimension_semantics=("parallel",)),
    )(slots)


def _make_gather(payload_dtype):
    @jax.custom_vjp
    def gather(block):
        return _all_gather_call(block.astype(payload_dtype))

    def bwd(_, dall):
        return (_sum_slots_call(_exchange_call(dall)),)

    gather.defvjp(lambda block: (_all_gather_call(block.astype(payload_dtype)), None), bwd)
    return gather


gather_bf16 = _make_gather(BF16)
gather_f32 = _make_gather(F32)


def _pack(vectors, width):
    flat = jnp.concatenate([v.reshape(-1) for v in vectors])
    return jnp.pad(flat, (0, width - flat.shape[0])).reshape(width // 128, 128)


def _unpack(packed, like):
    flat = packed.reshape(-1)
    out, at = [], 0
    for v in like:
        out.append(flat[at:at + v.size].reshape(v.shape))
        at += v.size
    return tuple(out)


@jax.custom_vjp
def replicated(params):
    return params


def _replicated_bwd(like, grads):
    n = sum(v.size for v in like)
    width = -(-n // 1024) * 1024
    total = _sum_slots_call(_all_gather_call(_pack(grads, width)))
    return (_unpack(total, like),)


replicated.defvjp(lambda params: (params, params), _replicated_bwd)


def _cols_from_slots(slots):
    n, rows, cols = slots.shape
    return slots.transpose(1, 0, 2).reshape(rows, n * cols)


def _forward(sharded, small, x, dims):
    n_meta, seq, lp = dims["n_meta"], dims["seq"], dims["lp"]
    (meta_s, w_in_s, w2_s, a2_s, g2_s, w_a_s, w_b_s, w_o_s, w_gu_s, w_dn_s) = sharded
    (n1, mu, w0, a0, k_k, k_a, r_k, gn_w, gn_b, q_g, k_g, f_bias, n2) = replicated(small)

    meta = _cols_from_slots(gather_f32(meta_s))
    w_in = _cols_from_slots(gather_bf16(w_in_s))
    w2 = _cols_from_slots(gather_bf16(w2_s))
    a2 = _cols_from_slots(gather_bf16(a2_s))
    g2 = _cols_from_slots(gather_bf16(g2_s))
    w_a = _cols_from_slots(gather_bf16(w_a_s))
    w_b = _cols_from_slots(gather_bf16(w_b_s))
    w_o = gather_bf16(w_o_s).reshape(-1, w_o_s.shape[1])
    w_gu = _cols_from_slots(gather_bf16(w_gu_s))
    w_dn = gather_bf16(w_dn_s).reshape(-1, w_dn_s.shape[1])

    d = x.shape[1]
    rw, fw = w_a.shape[0], w_b.shape[0]
    dl, al, gl = w2.shape[0], a2.shape[0], g2.shape[0]
    rcols = 3 * rw + dl + al + gl
    fheads = fw // HEAD_DIM
    fcols = 3 * fw + fheads
    pad128 = lambda n: -(-n // 128) * 128
    rpad, fpad = pad128(rcols), pad128(fcols)
    padc = lambda w, n: jnp.pad(w, ((0, 0), (0, n - w.shape[1])))
    w_cat = jnp.concatenate([padc(w_in[:, :rcols], rpad), padc(w_in[:, rcols:rcols + fcols], fpad),
                             w_in[:, rcols + fcols:]], axis=1)

    h0 = jnp.concatenate([meta, x, jnp.zeros((lp - n_meta - seq, d), F32)], axis=0)
    proj = dense(rmsnorm(h0, n1), w_cat)
    z_r, z_f, z_g = proj[:, :rcols], proj[:, rpad:rpad + fcols], proj[:, rpad + fpad:]

    z_prev = jnp.pad(z_r, ((1, 0), (0, 0)))[:-1]
    z = z_r + bmul(z_prev - z_r, mu)
    r, k, v = z[:, :rw], z[:, rw:2 * rw], z[:, 2 * rw:3 * rw]
    wd, ad, gd = z[:, 3 * rw:3 * rw + dl], z[:, 3 * rw + dl:3 * rw + dl + al], z[:, 3 * rw + dl + al:]
    w_log = -jax.nn.softplus(-badd(dense(jnp.tanh(wd), w2), w0)) - 0.5
    lw = -jnp.exp(w_log)
    a_sig = jax.nn.sigmoid(badd(dense(ad, a2), a0))
    g = dense(jax.nn.sigmoid(gd), g2)
    kk = head_l2norm(bmul(k, k_k))
    kf = k * (1.0 + bmul(a_sig - 1.0, k_a))
    y = wkv7(r, lw, kf, v, -kk, kk * a_sig)
    y_a = gn_bonus(y, r, kf, v, gn_w, gn_b, r_k.reshape(1, rw)) * g

    fq, fk, fv, fl = z_f[:, :fw], z_f[:, fw:2 * fw], z_f[:, 2 * fw:3 * fw], z_f[:, 3 * fw:]
    fq = head_rms(fq, q_g) * (HEAD_DIM ** -0.5)
    fk = head_rms(fk, k_g)
    cum = jnp.cumsum(jax.nn.log_sigmoid(badd(fl, f_bias)), axis=0)
    y_b = fox_attention(fq, fk, fv, cum)

    gates = jax.nn.sigmoid(z_g)
    merged = gates[:, :d] * dense(y_a, w_a) + gates[:, d:] * dense(y_b, w_b)
    h1 = h0 + dense(merged, w_o)
    gu = dense(rmsnorm(h1, n2), w_gu)
    dff = w_dn.shape[0]
    return h1 + dense(jax.nn.silu(gu[:, :dff]) * gu[:, dff:], w_dn)


SHARDED = ("meta_tokens", "w_in", "rwkv_w2", "rwkv_a2", "rwkv_g2", "w_branch_a", "w_branch_b", "w_o", "w_gate_up", "w_down")
SMALL = ("norm1_g", "rwkv_mu", "rwkv_w0", "rwkv_a0", "rwkv_k_k", "rwkv_k_a", "rwkv_r_k", "rwkv_gn_w", "rwkv_gn_b",
         "fox_q_norm_g", "fox_k_norm_g", "fox_f_bias", "norm2_g")
WEIGHTS = ("meta_tokens", "norm1_g", "w_in", "rwkv_mu", "rwkv_w0", "rwkv_w2", "rwkv_a0", "rwkv_a2", "rwkv_g2", "rwkv_k_k",
           "rwkv_k_a", "rwkv_r_k", "rwkv_gn_w", "rwkv_gn_b", "fox_q_norm_g", "fox_k_norm_g", "fox_f_bias", "w_branch_a",
           "w_branch_b", "w_o", "norm2_g", "w_gate_up", "w_down")


def _as2d(a):
    return a.reshape(-1, a.shape[-1])


def kernel(x, meta_tokens, norm1_g, w_in, rwkv_mu, rwkv_w0, rwkv_w2, rwkv_a0, rwkv_a2, rwkv_g2, rwkv_k_k, rwkv_k_a, rwkv_r_k, rwkv_gn_w, rwkv_gn_b, fox_q_norm_g, fox_k_norm_g, fox_f_bias, w_branch_a, w_branch_b, w_o, norm2_g, w_gate_up, w_down, loss_target, m_meta_tokens, m_norm1_g, m_w_in, m_rwkv_mu, m_rwkv_w0, m_rwkv_w2, m_rwkv_a0, m_rwkv_a2, m_rwkv_g2, m_rwkv_k_k, m_rwkv_k_a, m_rwkv_r_k, m_rwkv_gn_w, m_rwkv_gn_b, m_fox_q_norm_g, m_fox_k_norm_g, m_fox_f_bias, m_w_branch_a, m_w_branch_b, m_w_o, m_norm2_g, m_w_gate_up, m_w_down, v_meta_tokens, v_norm1_g, v_w_in, v_rwkv_mu, v_rwkv_w0, v_rwkv_w2, v_rwkv_a0, v_rwkv_a2, v_rwkv_g2, v_rwkv_k_k, v_rwkv_k_a, v_rwkv_r_k, v_rwkv_gn_w, v_rwkv_gn_b, v_fox_q_norm_g, v_fox_k_norm_g, v_fox_f_bias, v_w_branch_a, v_w_branch_b, v_w_o, v_norm2_g, v_w_gate_up, v_w_down):
    given = dict(locals())
    w = {n: given[n] for n in WEIGHTS}
    assert rwkv_r_k.shape[-1] == HEAD_DIM
    n_meta, seq = meta_tokens.shape[0], x.shape[1]
    tokens = n_meta + seq
    dims = dict(n_meta=n_meta, seq=seq, lp=-(-tokens // TOKEN_TILE) * TOKEN_TILE)

    sharded = tuple(_as2d(w[n]) for n in SHARDED)
    small = tuple(_as2d(w[n]) for n in SMALL)
    y, vjp = jax.vjp(lambda sh, sm, xs: _forward(sh, sm, xs, dims), sharded, small, x[0])
    loss_part, dy_real = _loss_call(y[n_meta:tokens], loss_target[0])
    dy = jnp.pad(dy_real, ((n_meta, dims["lp"] - tokens), (0, 0)))
    g_sharded, g_small, g_x = vjp(dy)
    loss = lax.psum(loss_part[0, 0], MESH_AXES)

    grads = {n: g.reshape(w[n].shape) for n, g in zip(SHARDED, g_sharded)}
    grads.update({n: g.reshape(w[n].shape) for n, g in zip(SMALL, g_small)})

    delta, new_m, new_v = {}, {}, {}
    for n in SHARDED:
        d_, m_, v_ = _adamw_call(_as2d(w[n]), _as2d(grads[n]), _as2d(given["m_" + n]), _as2d(given["v_" + n]))
        delta[n], new_m[n], new_v[n] = (t.reshape(w[n].shape) for t in (d_, m_, v_))
    n_small = sum(w[n].size for n in SMALL)
    width = -(-n_small // 1024) * 1024
    packs = [_pack([src[n] if p == "" else given[p + n] for n in SMALL], width)
             for p, src in (("", w), ("", grads), ("m_", None), ("v_", None))]
    like = [w[n] for n in SMALL]
    for out, packed in zip((delta, new_m, new_v), _adamw_call(*packs)):
        out.update(dict(zip(SMALL, _unpack(packed, like))))

    return (loss, g_x[None], *[grads[n] for n in WEIGHTS], *[delta[n] for n in WEIGHTS],
            *[new_m[n] for n in WEIGHTS], *[new_v[n] for n in WEIGHTS])
```

```python
import functools

import jax
import jax.numpy as jnp
from jax import lax
from jax.experimental import pallas as pl
from jax.experimental.pallas import tpu as pltpu

F32 = jnp.float32
BF16 = jnp.bfloat16

N_DEV = 8
MESH_AXES = ("x", "y", "c")
HEAD_DIM = 64
TOKEN_TILE = 128
WKV_CHUNK = 64
WKV_PAIRS_PER_STEP = 4
PAIR = 2 * HEAD_DIM
ATTN_BLOCK = 128
ATTN_BLOCK_BIG = 384
RMS_EPS = 1e-6
GN_EPS = 64e-5
L2_FLOOR = 1e-12
NEG_BIG = -1e30
ADAM_LR, ADAM_B1, ADAM_B2, ADAM_EPS, ADAM_WD, ADAM_STEP = 0.001, 0.9, 0.999, 1e-08, 0.01, 10
VMEM_BYTES_V7X = 64 * 1024 * 1024
VMEM_LIMIT_CAP = 56 * 1024 * 1024
VMEM_LIMIT_FLOOR = 32 * 1024 * 1024
MATMUL_VMEM_BUDGET = 36 * 1024 * 1024
GRID_STEP_BYTES = 1024 * 1024


def _vmem_limit(estimate_bytes):
    return int(min(max(estimate_bytes * 5 // 4, VMEM_LIMIT_FLOOR), VMEM_LIMIT_CAP))


def _pick(dim, cands):
    for c in cands:
        if dim % c == 0:
            return c
    return dim


def _row_tile(rows, width, itemsize=4, budget=2 * 1024 * 1024):
    for c in (1408, 1024, 704, 512, 384, 256, 128, 64, 32, 16, 8):
        if rows % c == 0 and c * width * itemsize <= budget:
            return c
    return rows


def _dg(a, b, ta, tb):
    dims = (((0 if ta else 1,), (1 if tb else 0,)), ((), ()))
    return lax.dot_general(a, b, dims, preferred_element_type=F32)


def _split(x, n):
    parts = []
    for _ in range(n):
        h = x.astype(BF16)
        parts.append(h)
        x = x - h.astype(F32)
    return parts


def _mm(a, b, ta=False, tb=False):
    return _dg(a.astype(BF16), b.astype(BF16), ta, tb)


def _matmul(a, b, ta=False, tb=False, out_dtype=F32, name="matmul", after=None):
    if ta:
        kdim, m = a.shape
    else:
        m, kdim = a.shape
    if tb:
        n, k2 = b.shape
    else:
        k2, n = b.shape
    assert kdim == k2, (a.shape, b.shape, ta, tb)
    sa, sb, so = a.dtype.itemsize, b.dtype.itemsize, jnp.dtype(out_dtype).itemsize
    tm, tn, tk = _matmul_tiles(m, n, kdim, ta, sa, sb, so)
    nk = kdim // tk

    order = () if after is None else (after,)

    def body(a_ref, b_ref, *rest):
        o_ref, acc = rest[len(order)], rest[len(order) + 1:]
        part = _dg(a_ref[...].astype(BF16), b_ref[...].astype(BF16), ta, tb)
        if nk == 1:
            o_ref[...] = part.astype(o_ref.dtype)
            return
        kk = pl.program_id(2)

        @pl.when(kk == 0)
        def _():
            acc[0][...] = part

        @pl.when(kk > 0)
        def _():
            acc[0][...] += part

        @pl.when(kk == nk - 1)
        def _():
            o_ref[...] = acc[0][...].astype(o_ref.dtype)

    a_spec = pl.BlockSpec((tk, tm), lambda i, j, k: (k, i)) if ta else pl.BlockSpec((tm, tk), lambda i, j, k: (i, k))
    b_spec = pl.BlockSpec((tn, tk), lambda i, j, k: (j, k)) if tb else pl.BlockSpec((tk, tn), lambda i, j, k: (k, j))
    return pl.pallas_call(
        body, name=name,
        grid=(m // tm, n // tn, nk),
        in_specs=[a_spec, b_spec] + [pl.BlockSpec(memory_space=pl.ANY)] * len(order),
        out_specs=pl.BlockSpec((tm, tn), lambda i, j, k: (i, j)),
        out_shape=jax.ShapeDtypeStruct((m, n), out_dtype),
        scratch_shapes=[pltpu.VMEM((tm, tn), F32)] if nk > 1 else [],
        compiler_params=pltpu.CompilerParams(dimension_semantics=("parallel", "parallel", "arbitrary"),
                                             vmem_limit_bytes=_vmem_limit(_matmul_vmem(tm, tn, tk, nk, sa, sb, so))),
    )(a, b, *order)


def _matmul_vmem(tm, tn, tk, nk, sa, sb, so):
    return 2 * (tm * tk * sa + tk * tn * sb + tm * tn * so) + tm * tn * 4 + (tm * tn * 4 if nk > 1 else 0)


def _matmul_tiles(m, n, kdim, ta, sa, sb, so):
    lane = (2048, 1024, 640, 512, 384, 256, 128)
    sublane = (2048, 1408, 1024, 704, 512, 384, 256, 128)
    divs = lambda dim, cands: [c for c in cands if dim % c == 0] or [dim]
    best = None
    for tm in divs(m, lane if ta else sublane):
        for tn in divs(n, lane):
            for tk in divs(kdim, sublane if ta else lane) + ([kdim] if kdim <= 2048 else []):
                nk, nm, nn = kdim // tk, m // tm, n // tn
                if _matmul_vmem(tm, tn, tk, nk, sa, sb, so) > MATMUL_VMEM_BUDGET:
                    continue
                a_bytes = m * kdim * sa * (nn if nk > 1 else 1)
                b_bytes = kdim * n * sb * (1 if (nk == 1 and nn == 1) else nm)
                cost = a_bytes + b_bytes + m * n * so + nm * nn * nk * GRID_STEP_BYTES
                if best is None or cost < best[0]:
                    best = (cost, tm, tn, tk)
    return best[1:]


@jax.custom_vjp
def dense(x, w):
    return _matmul(x.astype(BF16), w, name="dense_fwd")


def _dense_fwd(x, w):
    assert x.dtype == F32
    xb = x.astype(BF16)
    return _matmul(xb, w, name="dense_fwd"), (xb, w)


def _dense_bwd(res, dy):
    xb, w = res
    dyb = dy.astype(BF16)
    dx = _matmul(dyb, w, tb=True, out_dtype=F32, name="dense_dx")
    dw = _matmul(xb, dyb, ta=True, out_dtype=w.dtype, name="dense_dw")
    return dx, dw


dense.defvjp(_dense_fwd, _dense_bwd)


def _rms_fwd_call(x, g):
    rows, d = x.shape
    tr = _row_tile(rows, d)

    def body(x_ref, g_ref, y_ref):
        xv = x_ref[...]
        rstd = lax.rsqrt(jnp.mean(xv * xv, axis=1, keepdims=True) + RMS_EPS)
        y_ref[...] = (xv * rstd) * g_ref[...]

    return pl.pallas_call(
        body, name="rms_fwd", grid=(rows // tr,),
        in_specs=[pl.BlockSpec((tr, d), lambda i: (i, 0)), pl.BlockSpec((1, d), lambda i: (0, 0))],
        out_specs=pl.BlockSpec((tr, d), lambda i: (i, 0)),
        out_shape=jax.ShapeDtypeStruct((rows, d), F32),
        compiler_params=pltpu.CompilerParams(dimension_semantics=("parallel",)),
    )(x, g)


def _rms_bwd_call(x, g, dy):
    rows, d = x.shape
    tr = _row_tile(rows, d)

    def body(x_ref, g_ref, dy_ref, dx_ref, dg_ref):
        @pl.when(pl.program_id(0) == 0)
        def _():
            dg_ref[...] = jnp.zeros_like(dg_ref)

        xv = x_ref[...]
        dyv = dy_ref[...]
        rstd = lax.rsqrt(jnp.mean(xv * xv, axis=1, keepdims=True) + RMS_EPS)
        xhat = xv * rstd
        dxhat = dyv * g_ref[...]
        dx_ref[...] = rstd * (dxhat - xhat * jnp.mean(dxhat * xhat, axis=1, keepdims=True))
        dg_ref[...] += jnp.sum(dyv * xhat, axis=0, keepdims=True)

    return pl.pallas_call(
        body, name="rms_bwd", grid=(rows // tr,),
        in_specs=[pl.BlockSpec((tr, d), lambda i: (i, 0)), pl.BlockSpec((1, d), lambda i: (0, 0)),
                  pl.BlockSpec((tr, d), lambda i: (i, 0))],
        out_specs=[pl.BlockSpec((tr, d), lambda i: (i, 0)), pl.BlockSpec((1, d), lambda i: (0, 0))],
        out_shape=[jax.ShapeDtypeStruct((rows, d), F32), jax.ShapeDtypeStruct((1, d), F32)],
        compiler_params=pltpu.CompilerParams(dimension_semantics=("arbitrary",)),
    )(x, g, dy)


@jax.custom_vjp
def rmsnorm(x, g):
    return _rms_fwd_call(x, g)


rmsnorm.defvjp(lambda x, g: (_rms_fwd_call(x, g), (x, g)), lambda res, dy: tuple(_rms_bwd_call(res[0], res[1], dy)))


def _bcast_call(x, p, mul):
    rows, d = x.shape
    tr = _row_tile(rows, d)

    def body(x_ref, p_ref, y_ref):
        y_ref[...] = x_ref[...] * p_ref[...] if mul else x_ref[...] + p_ref[...]

    return pl.pallas_call(
        body, name="bcast_mul" if mul else "bcast_add", grid=(rows // tr,),
        in_specs=[pl.BlockSpec((tr, d), lambda i: (i, 0)), pl.BlockSpec((1, d), lambda i: (0, 0))],
        out_specs=pl.BlockSpec((tr, d), lambda i: (i, 0)),
        out_shape=jax.ShapeDtypeStruct((rows, d), F32),
        compiler_params=pltpu.CompilerParams(dimension_semantics=("parallel",)),
    )(x, p)


def _colsum_call(a, b=None):
    rows, d = a.shape
    tr = _row_tile(rows, d)
    ops = (a,) if b is None else (a, b)

    def body(*refs):
        o_ref = refs[-1]

        @pl.when(pl.program_id(0) == 0)
        def _():
            o_ref[...] = jnp.zeros_like(o_ref)

        v = refs[0][...] if b is None else refs[0][...] * refs[1][...]
        o_ref[...] += jnp.sum(v, axis=0, keepdims=True)

    return pl.pallas_call(
        body, name="colsum", grid=(rows // tr,),
        in_specs=[pl.BlockSpec((tr, d), lambda i: (i, 0))] * len(ops),
        out_specs=pl.BlockSpec((1, d), lambda i: (0, 0)),
        out_shape=jax.ShapeDtypeStruct((1, d), F32),
        compiler_params=pltpu.CompilerParams(dimension_semantics=("arbitrary",)),
    )(*ops)


@jax.custom_vjp
def bmul(x, p):
    return _bcast_call(x, p, True)


bmul.defvjp(lambda x, p: (_bcast_call(x, p, True), (x, p)),
            lambda res, dy: (_bcast_call(dy, res[1], True), _colsum_call(dy, res[0])))


@jax.custom_vjp
def badd(x, p):
    return _bcast_call(x, p, False)


badd.defvjp(lambda x, p: (_bcast_call(x, p, False), None), lambda res, dy: (dy, _colsum_call(dy)))


def _heads(width):
    return [slice(h * HEAD_DIM, (h + 1) * HEAD_DIM) for h in range(width // HEAD_DIM)]


def _head_rms_fwd_call(x, g):
    rows, w = x.shape
    tr = _row_tile(rows, w, budget=1024 * 1024)

    def body(x_ref, g_ref, y_ref):
        for sl in _heads(w):
            xv = x_ref[:, sl]
            rstd = lax.rsqrt(jnp.mean(xv * xv, axis=1, keepdims=True) + RMS_EPS)
            y_ref[:, sl] = (xv * rstd) * g_ref[...]

    return pl.pallas_call(
        body, name="head_rms_fwd", grid=(rows // tr,),
        in_specs=[pl.BlockSpec((tr, w), lambda i: (i, 0)), pl.BlockSpec((1, HEAD_DIM), lambda i: (0, 0))],
        out_specs=pl.BlockSpec((tr, w), lambda i: (i, 0)),
        out_shape=jax.ShapeDtypeStruct((rows, w), F32),
        compiler_params=pltpu.CompilerParams(dimension_semantics=("parallel",)),
    )(x, g)


def _head_rms_bwd_call(x, g, dy):
    rows, w = x.shape
    tr = _row_tile(rows, w, budget=1024 * 1024)

    def body(x_ref, g_ref, dy_ref, dx_ref, dg_ref):
        @pl.when(pl.program_id(0) == 0)
        def _():
            dg_ref[...] = jnp.zeros_like(dg_ref)

        dg = jnp.zeros((1, HEAD_DIM), F32)
        for sl in _heads(w):
            xv = x_ref[:, sl]
            dyv = dy_ref[:, sl]
            rstd = lax.rsqrt(jnp.mean(xv * xv, axis=1, keepdims=True) + RMS_EPS)
            xhat = xv * rstd
            dxhat = dyv * g_ref[...]
            dx_ref[:, sl] = rstd * (dxhat - xhat * jnp.mean(dxhat * xhat, axis=1, keepdims=True))
            dg = dg + jnp.sum(dyv * xhat, axis=0, keepdims=True)
        dg_ref[...] += dg

    return pl.pallas_call(
        body, name="head_rms_bwd", grid=(rows // tr,),
        in_specs=[pl.BlockSpec((tr, w), lambda i: (i, 0)), pl.BlockSpec((1, HEAD_DIM), lambda i: (0, 0)),
                  pl.BlockSpec((tr, w), lambda i: (i, 0))],
        out_specs=[pl.BlockSpec((tr, w), lambda i: (i, 0)), pl.BlockSpec((1, HEAD_DIM), lambda i: (0, 0))],
        out_shape=[jax.ShapeDtypeStruct((rows, w), F32), jax.ShapeDtypeStruct((1, HEAD_DIM), F32)],
        compiler_params=pltpu.CompilerParams(dimension_semantics=("arbitrary",)),
    )(x, g, dy)


@jax.custom_vjp
def head_rms(x, g):
    return _head_rms_fwd_call(x, g)


head_rms.defvjp(lambda x, g: (_head_rms_fwd_call(x, g), (x, g)),
                lambda res, dy: tuple(_head_rms_bwd_call(res[0], res[1], dy)))


def _head_l2_call(x, dy=None):
    rows, w = x.shape
    tr = _row_tile(rows, w, budget=1024 * 1024)
    ops = (x,) if dy is None else (x, dy)

    def body(*refs):
        o_ref = refs[-1]
        for sl in _heads(w):
            xv = refs[0][:, sl]
            nrm = jnp.sqrt(jnp.sum(xv * xv, axis=1, keepdims=True))
            live = nrm > L2_FLOOR
            inv = 1.0 / jnp.maximum(nrm, L2_FLOOR)
            y = xv * inv
            if dy is None:
                o_ref[:, sl] = y
            else:
                dyv = refs[1][:, sl]
                proj = jnp.where(live, jnp.sum(dyv * y, axis=1, keepdims=True), 0.0)
                o_ref[:, sl] = (dyv - y * proj) * inv

    return pl.pallas_call(
        body, name="head_l2_fwd" if dy is None else "head_l2_bwd", grid=(rows // tr,),
        in_specs=[pl.BlockSpec((tr, w), lambda i: (i, 0))] * len(ops),
        out_specs=pl.BlockSpec((tr, w), lambda i: (i, 0)),
        out_shape=jax.ShapeDtypeStruct((rows, w), F32),
        compiler_params=pltpu.CompilerParams(dimension_semantics=("parallel",)),
    )(*ops)


@jax.custom_vjp
def head_l2norm(x):
    return _head_l2_call(x)


head_l2norm.defvjp(lambda x: (_head_l2_call(x), x), lambda x, dy: (_head_l2_call(x, dy),))


def _gn_fwd_call(y, r, kf, v, gw, gb, rk):
    rows, w = y.shape
    tr = _row_tile(rows, w, budget=512 * 1024)

    def body(y_ref, r_ref, kf_ref, v_ref, gw_ref, gb_ref, rk_ref, o_ref):
        for sl in _heads(w):
            yv = y_ref[:, sl]
            yc = yv - jnp.mean(yv, axis=1, keepdims=True)
            rstd = lax.rsqrt(jnp.mean(yc * yc, axis=1, keepdims=True) + GN_EPS)
            s = jnp.sum(r_ref[:, sl] * kf_ref[:, sl] * rk_ref[:, sl], axis=1, keepdims=True)
            o_ref[:, sl] = (yc * rstd) * gw_ref[:, sl] + gb_ref[:, sl] + s * v_ref[:, sl]

    tok = pl.BlockSpec((tr, w), lambda i: (i, 0))
    par = pl.BlockSpec((1, w), lambda i: (0, 0))
    return pl.pallas_call(
        body, name="gn_bonus_fwd", grid=(rows // tr,),
        in_specs=[tok] * 4 + [par] * 3, out_specs=tok,
        out_shape=jax.ShapeDtypeStruct((rows, w), F32),
        compiler_params=pltpu.CompilerParams(dimension_semantics=("parallel",)),
    )(y, r, kf, v, gw, gb, rk)


def _gn_bwd_call(y, r, kf, v, gw, gb, rk, do):
    rows, w = y.shape
    tr = _row_tile(rows, w, budget=512 * 1024)

    def body(y_ref, r_ref, kf_ref, v_ref, gw_ref, rk_ref, do_ref,
             dy_ref, dr_ref, dkf_ref, dv_ref, dgw_ref, dgb_ref, drk_ref):
        @pl.when(pl.program_id(0) == 0)
        def _():
            dgw_ref[...] = jnp.zeros_like(dgw_ref)
            dgb_ref[...] = jnp.zeros_like(dgb_ref)
            drk_ref[...] = jnp.zeros_like(drk_ref)

        for sl in _heads(w):
            yv, rv, kv, vv, dov = y_ref[:, sl], r_ref[:, sl], kf_ref[:, sl], v_ref[:, sl], do_ref[:, sl]
            yc = yv - jnp.mean(yv, axis=1, keepdims=True)
            rstd = lax.rsqrt(jnp.mean(yc * yc, axis=1, keepdims=True) + GN_EPS)
            yhat = yc * rstd
            dyhat = dov * gw_ref[:, sl]
            dy_ref[:, sl] = rstd * (dyhat - jnp.mean(dyhat, axis=1, keepdims=True)
                                    - yhat * jnp.mean(dyhat * yhat, axis=1, keepdims=True))
            rkv = rk_ref[:, sl]
            s = jnp.sum(rv * kv * rkv, axis=1, keepdims=True)
            ds = jnp.sum(dov * vv, axis=1, keepdims=True)
            dv_ref[:, sl] = s * dov
            dr_ref[:, sl] = ds * kv * rkv
            dkf_ref[:, sl] = ds * rv * rkv
            dgw_ref[:, sl] += jnp.sum(dov * yhat, axis=0, keepdims=True)
            dgb_ref[:, sl] += jnp.sum(dov, axis=0, keepdims=True)
            drk_ref[:, sl] += jnp.sum(ds * rv * kv, axis=0, keepdims=True)

    tok = pl.BlockSpec((tr, w), lambda i: (i, 0))
    par = pl.BlockSpec((1, w), lambda i: (0, 0))
    tshape = jax.ShapeDtypeStruct((rows, w), F32)
    pshape = jax.ShapeDtypeStruct((1, w), F32)
    return pl.pallas_call(
        body, name="gn_bonus_bwd", grid=(rows // tr,),
        in_specs=[tok] * 4 + [par] * 2 + [tok], out_specs=[tok] * 4 + [par] * 3,
        out_shape=[tshape] * 4 + [pshape] * 3,
        compiler_params=pltpu.CompilerParams(dimension_semantics=("arbitrary",)),
    )(y, r, kf, v, gw, rk, do)


@jax.custom_vjp
def gn_bonus(y, r, kf, v, gw, gb, rk):
    return _gn_fwd_call(y, r, kf, v, gw, gb, rk)


def _gn_bwd(res, do):
    y, r, kf, v, gw, gb, rk = res
    dy, dr, dkf, dv, dgw, dgb, drk = _gn_bwd_call(y, r, kf, v, gw, gb, rk, do)
    return dy, dr, dkf, dv, dgw, dgb, drk


gn_bonus.defvjp(lambda *a: (_gn_fwd_call(*a), a), _gn_bwd)


def _pair_masks(rows):
    lane = lax.broadcasted_iota(jnp.int32, (rows, PAIR), 1)
    return lane < HEAD_DIM, lane >= HEAD_DIM


def _bd(x):
    m0, m1 = _pair_masks(x.shape[0])
    return jnp.concatenate([jnp.where(m0, x, 0.0), jnp.where(m1, x, 0.0)], axis=0)


def _unbd(m, c):
    return jnp.where(_pair_masks(c)[0], m[:c], m[c:])


def _pair_a(l2, r2):
    return _mm(l2, _bd(r2), tb=True)


def _pair_mul(p2, x2):
    return _mm(p2, _bd(x2))


def _pair_mul_t(p2, x2):
    return _unbd(_mm(p2, x2, ta=True), p2.shape[0])


def _block_diag_mask():
    row = lax.broadcasted_iota(jnp.int32, (PAIR, PAIR), 0)
    lane = lax.broadcasted_iota(jnp.int32, (PAIR, PAIR), 1)
    return (row < HEAD_DIM) == (lane < HEAD_DIM), row == lane


def _wkv_pair_common(r, lw, k, a, b):
    c = r[0].shape[0]
    pairs = range(len(r))
    i = lax.broadcasted_iota(jnp.int32, (c, PAIR), 0)
    j = lax.broadcasted_iota(jnp.int32, (c, PAIR), 1) % c
    strict, incl = i > j, i >= j
    ti = lax.broadcasted_iota(jnp.int32, (c, c), 0)
    tj = lax.broadcasted_iota(jnp.int32, (c, c), 1)
    tri = jnp.where(ti >= tj, 1.0, 0.0).astype(BF16)
    lc = [sum(_dg(tri, part, False, False) for part in _split(lw[p], 3)) for p in pairs]
    lend = [lc[p][c - 1:c, :] for p in pairs]
    rt = [r[p] * jnp.exp(lc[p]) for p in pairs]
    at = [a[p] * jnp.exp(lc[p] - lw[p]) for p in pairs]
    pinv = [jnp.exp(-lc[p]) for p in pairs]
    kt = [k[p] * pinv[p] for p in pairs]
    bt = [b[p] * pinv[p] for p in pairs]
    e = [jnp.exp(lend[p] - lc[p]) for p in pairs]
    ktp = [k[p] * e[p] for p in pairs]
    btp = [b[p] * e[p] for p in pairs]
    a_ab = [jnp.where(strict, _pair_a(at[p], bt[p]), 0.0) for p in pairs]
    a_ak = [jnp.where(strict, _pair_a(at[p], kt[p]), 0.0) for p in pairs]
    a_rb = [jnp.where(incl, _pair_a(rt[p], bt[p]), 0.0) for p in pairs]
    a_rk = [jnp.where(incl, _pair_a(rt[p], kt[p]), 0.0) for p in pairs]
    t = [jnp.where(i == j, 1.0, 0.0) + a_ab[p] for p in pairs]
    xp = a_ab
    n = 2
    while n < c:
        xp = [_pair_mul(xp[p], xp[p]) for p in pairs]
        t = [t[p] + _pair_mul(t[p], xp[p]) for p in pairs]
        n *= 2
    bdm, eye = _block_diag_mask()
    pend_col = [jnp.sum(jnp.where(eye, jnp.exp(lend[p]), 0.0), axis=1, keepdims=True) for p in pairs]
    return dict(rt=rt, at=at, kt=kt, bt=bt, ktp=ktp, btp=btp, a_ak=a_ak, a_rb=a_rb, a_rk=a_rk, t=t,
                pend_col=pend_col, lend=lend, lc=lc, strict=strict, incl=incl, tri=tri, bdm=bdm)


def _wkv_group(width):
    npair = width // PAIR
    g = min(WKV_PAIRS_PER_STEP, npair)
    assert npair % g == 0
    return npair, g


def _wkv_fwd_call(r, lw, k, v, a, b):
    tokens, width = r.shape
    c = WKV_CHUNK
    nc = tokens // c
    npair, g = _wkv_group(width)

    def body(r_ref, lw_ref, k_ref, v_ref, a_ref, b_ref, y_ref, s_ref, st):
        @pl.when(pl.program_id(1) == 0)
        def _():
            st[...] = jnp.zeros_like(st)

        pairs = range(g)
        rv, lwv, kv, vv, av, bv = ([ref[:, p * PAIR:(p + 1) * PAIR] for p in pairs]
                                   for ref in (r_ref, lw_ref, k_ref, v_ref, a_ref, b_ref))
        s0 = [st[p] for p in pairs]
        q = _wkv_pair_common(rv, lwv, kv, av, bv)
        w1 = [_mm(q["at"][p], s0[p]) + _pair_mul(q["a_ak"][p], vv[p]) for p in pairs]
        u = [_pair_mul(q["t"][p], w1[p]) for p in pairs]
        y = [_mm(q["rt"][p], s0[p]) + _pair_mul(q["a_rb"][p], u[p]) + _pair_mul(q["a_rk"][p], vv[p]) for p in pairs]
        grow = [_mm(jnp.concatenate([q["btp"][p], q["ktp"][p]], axis=0), jnp.concatenate([u[p], vv[p]], axis=0), ta=True)
                for p in pairs]
        for p in pairs:
            y_ref[:, p * PAIR:(p + 1) * PAIR] = y[p]
            s_ref[0, p] = s0[p]
            st[p] = q["pend_col"][p] * s0[p] + jnp.where(q["bdm"], grow[p], 0.0)

    tok = pl.BlockSpec((c, g * PAIR), lambda gi, ci: (ci, gi))
    return pl.pallas_call(
        body, name="wkv_fwd", grid=(npair // g, nc),
        in_specs=[tok] * 6,
        out_specs=[tok, pl.BlockSpec((1, g, PAIR, PAIR), lambda gi, ci: (ci, gi, 0, 0))],
        out_shape=[jax.ShapeDtypeStruct((tokens, width), F32), jax.ShapeDtypeStruct((nc, npair, PAIR, PAIR), F32)],
        scratch_shapes=[pltpu.VMEM((g, PAIR, PAIR), F32)],
        compiler_params=pltpu.CompilerParams(dimension_semantics=("parallel", "arbitrary")),
    )(r, lw, k, v, a, b)


def _wkv_bwd_call(r, lw, k, v, a, b, s, dy):
    tokens, width = r.shape
    c = WKV_CHUNK
    nc = tokens // c
    npair, g = _wkv_group(width)

    def body(r_ref, lw_ref, k_ref, v_ref, a_ref, b_ref, s_ref, dy_ref,
             dr_ref, dlw_ref, dk_ref, dv_ref, da_ref, db_ref, dst):
        @pl.when(pl.program_id(1) == 0)
        def _():
            dst[...] = jnp.zeros_like(dst)

        pairs = range(g)
        rv, lwv, kv, vv, av, bv, dyv = ([ref[:, p * PAIR:(p + 1) * PAIR] for p in pairs]
                                        for ref in (r_ref, lw_ref, k_ref, v_ref, a_ref, b_ref, dy_ref))
        s0 = [s_ref[0, p] for p in pairs]
        dsc = [dst[p] for p in pairs]
        q = _wkv_pair_common(rv, lwv, kv, av, bv)
        rt, at, kt, bt, ktp, btp, t = (q[n] for n in ("rt", "at", "kt", "bt", "ktp", "btp", "t"))
        a_ak, a_rb, a_rk, strict, incl = (q[n] for n in ("a_ak", "a_rb", "a_rk", "strict", "incl"))
        w1 = [_mm(at[p], s0[p]) + _pair_mul(a_ak[p], vv[p]) for p in pairs]
        u = [_pair_mul(t[p], w1[p]) for p in pairs]
        du = [_pair_mul_t(a_rb[p], dyv[p]) + _mm(btp[p], dsc[p]) for p in pairs]
        dw1 = [_pair_mul_t(t[p], du[p]) for p in pairs]
        dv = [_pair_mul_t(a_rk[p], dyv[p]) + _mm(ktp[p], dsc[p]) + _pair_mul_t(a_ak[p], dw1[p]) for p in pairs]
        da_ab = [jnp.where(strict, _pair_a(dw1[p], u[p]), 0.0) for p in pairs]
        da_ak = [jnp.where(strict, _pair_a(dw1[p], vv[p]), 0.0) for p in pairs]
        da_rb = [jnp.where(incl, _pair_a(dyv[p], u[p]), 0.0) for p in pairs]
        da_rk = [jnp.where(incl, _pair_a(dyv[p], vv[p]), 0.0) for p in pairs]
        d_rt = [_mm(dyv[p], s0[p], tb=True) + _pair_mul(da_rb[p], bt[p]) + _pair_mul(da_rk[p], kt[p]) for p in pairs]
        d_at = [_mm(dw1[p], s0[p], tb=True) + _pair_mul(da_ab[p], bt[p]) + _pair_mul(da_ak[p], kt[p]) for p in pairs]
        d_bt = [_pair_mul_t(da_ab[p], at[p]) + _pair_mul_t(da_rb[p], rt[p]) for p in pairs]
        d_kt = [_pair_mul_t(da_ak[p], at[p]) + _pair_mul_t(da_rk[p], rt[p]) for p in pairs]
        d_btp = [_mm(u[p], dsc[p], tb=True) for p in pairs]
        d_ktp = [_mm(vv[p], dsc[p], tb=True) for p in pairs]
        ones = jnp.ones((8, PAIR), BF16)
        dpend = [sum(_dg(ones, part, False, True) for part in _split(dsc[p] * s0[p], 3))[0:1, :] * jnp.exp(q["lend"][p])
                 for p in pairs]
        grow = [_mm(jnp.concatenate([rt[p], at[p]], axis=0), jnp.concatenate([dyv[p], dw1[p]], axis=0), ta=True)
                for p in pairs]
        last = lax.broadcasted_iota(jnp.int32, (c, PAIR), 0) == c - 1
        for p in pairs:
            sl = slice(p * PAIR, (p + 1) * PAIR)
            dst[p] = q["pend_col"][p] * dsc[p] + jnp.where(q["bdm"], grow[p], 0.0)
            lc_e = d_ktp[p] * ktp[p] + d_btp[p] * btp[p]
            dlend = jnp.sum(lc_e, axis=0, keepdims=True) + dpend[p]
            dlc = d_rt[p] * rt[p] - d_kt[p] * kt[p] - d_bt[p] * bt[p] - lc_e + jnp.where(last, dlend, 0.0)
            dlp = d_at[p] * at[p]
            dlw_ref[:, sl] = sum(_dg(q["tri"], part, True, False) for part in _split(dlc + dlp, 3)) - dlp
            lc = q["lc"][p]
            pinv = jnp.exp(-lc)
            e = jnp.exp(q["lend"][p] - lc)
            dr_ref[:, sl] = d_rt[p] * jnp.exp(lc)
            da_ref[:, sl] = d_at[p] * jnp.exp(lc - lwv[p])
            dk_ref[:, sl] = d_kt[p] * pinv + d_ktp[p] * e
            db_ref[:, sl] = d_bt[p] * pinv + d_btp[p] * e
            dv_ref[:, sl] = dv[p]

    tok = pl.BlockSpec((c, g * PAIR), lambda gi, ci: (nc - 1 - ci, gi))
    tshape = jax.ShapeDtypeStruct((tokens, width), F32)
    return pl.pallas_call(
        body, name="wkv_bwd", grid=(npair // g, nc),
        in_specs=[tok] * 6 + [pl.BlockSpec((1, g, PAIR, PAIR), lambda gi, ci: (nc - 1 - ci, gi, 0, 0)), tok],
        out_specs=[tok] * 6, out_shape=[tshape] * 6,
        scratch_shapes=[pltpu.VMEM((g, PAIR, PAIR), F32)],
        compiler_params=pltpu.CompilerParams(dimension_semantics=("parallel", "arbitrary")),
    )(r, lw, k, v, a, b, s, dy)


@jax.custom_vjp
def wkv7(r, lw, k, v, a, b):
    return _wkv_fwd_call(r, lw, k, v, a, b)[0]


def _wkv7_fwd(r, lw, k, v, a, b):
    y, s = _wkv_fwd_call(r, lw, k, v, a, b)
    return y, (r, lw, k, v, a, b, s)


wkv7.defvjp(_wkv7_fwd, lambda res, dy: tuple(_wkv_bwd_call(*res, dy)))


def _attn_block(tokens):
    return ATTN_BLOCK_BIG if tokens % ATTN_BLOCK_BIG == 0 else ATTN_BLOCK


def _fox_layouts(cum):
    tokens, heads = cum.shape
    t = _attn_block(tokens)
    cq = cum.reshape(tokens, heads // 2, 2).transpose(1, 0, 2)
    ck = cum.T.reshape(heads // 2, 2, tokens // t, t).transpose(0, 2, 1, 3)
    return cq, ck


def _head_lane_masks(rows):
    lane = lax.broadcasted_iota(jnp.int32, (rows, 2 * HEAD_DIM), 1)
    return [lane < HEAD_DIM, lane >= HEAD_DIM]


def _fox_fwd_call(q, k, v, cq, ck):
    tokens, width = q.shape
    t = _attn_block(tokens)
    nb = tokens // t
    hd = HEAD_DIM
    npair = width // (2 * hd)

    def body(q_ref, k_ref, v_ref, cq_ref, ck_ref, o_ref, lse_ref):
        i = pl.program_id(1)
        masks = _head_lane_masks(t)
        q2 = q_ref[...]
        qs = [jnp.where(mk, q2, 0.0).astype(BF16) for mk in masks]
        cqs = [cq_ref[0, :, hh:hh + 1] for hh in range(2)]

        def block(j, carry, diagonal):
            off = pl.multiple_of(j * t, t)
            ckj = ck_ref[0, j]
            k2 = k_ref[pl.ds(off, t), :].astype(BF16)
            v2 = v_ref[pl.ds(off, t), :].astype(BF16)
            out = []
            for hh in range(2):
                m, l, acc = carry[hh]
                s = _dg(qs[hh], k2, False, True) + (cqs[hh] - ckj[hh:hh + 1, :])
                if diagonal:
                    keep = lax.broadcasted_iota(jnp.int32, (t, t), 0) >= lax.broadcasted_iota(jnp.int32, (t, t), 1)
                    s = jnp.where(keep, s, NEG_BIG)
                m_new = jnp.maximum(m, jnp.max(s, axis=1, keepdims=True))
                alpha = jnp.exp(m - m_new)
                p = jnp.exp(s - m_new)
                l = alpha * l + jnp.sum(p, axis=1, keepdims=True)
                acc = alpha * acc + _dg(p.astype(BF16), v2, False, False)
                out.append((m_new, l, acc))
            return tuple(out)

        init = tuple((jnp.full((t, 1), NEG_BIG, F32), jnp.zeros((t, 1), F32), jnp.zeros((t, 2 * hd), F32)) for _ in range(2))
        res = lax.fori_loop(0, i, lambda j, c: block(j, c, False), init)
        res = block(i, res, True)
        o_ref[...] = jnp.where(masks[0], res[0][2] / res[0][1], res[1][2] / res[1][1])
        for hh in range(2):
            lse_ref[0, :, hh:hh + 1] = res[hh][0] + jnp.log(res[hh][1])

    blk = pl.BlockSpec((t, 2 * hd), lambda hp, i: (i, hp))
    full = pl.BlockSpec((tokens, 2 * hd), lambda hp, i: (0, hp))
    cq_spec = pl.BlockSpec((1, t, 2), lambda hp, i: (hp, i, 0))
    ck_spec = pl.BlockSpec((1, nb, 2, t), lambda hp, i: (hp, 0, 0, 0))
    return pl.pallas_call(
        body, name="fox_fwd", grid=(npair, nb),
        in_specs=[blk, full, full, cq_spec, ck_spec],
        out_specs=[blk, cq_spec],
        out_shape=[jax.ShapeDtypeStruct((tokens, width), F32), jax.ShapeDtypeStruct((npair, tokens, 2), F32)],
        compiler_params=pltpu.CompilerParams(dimension_semantics=("parallel", "arbitrary")),
    )(q, k, v, cq, ck)


def _fox_bwd_call(q, k, v, cq, ck, o, lse, do):
    tokens, width = q.shape
    t = _attn_block(tokens)
    nb = tokens // t
    hd = HEAD_DIM
    npair = width // (2 * hd)

    def body(q_ref, k_ref, v_ref, cq_ref, ck_ref, o_ref, lse_ref, do_ref, dq_ref, dk_ref, dv_ref, dck_ref, dcq_ref):
        i = pl.program_id(1)

        @pl.when(i == 0)
        def _():
            dk_ref[...] = jnp.zeros_like(dk_ref)
            dv_ref[...] = jnp.zeros_like(dv_ref)
            dck_ref[...] = jnp.zeros_like(dck_ref)

        masks = _head_lane_masks(t)
        q2, do2, o2 = q_ref[...], do_ref[...], o_ref[...]
        qs = [jnp.where(mk, q2, 0.0).astype(BF16) for mk in masks]
        dos = [jnp.where(mk, do2, 0.0).astype(BF16) for mk in masks]
        deltas = [jnp.sum(dos[hh].astype(F32) * o2, axis=1, keepdims=True) for hh in range(2)]
        bias = [cq_ref[0, :, hh:hh + 1] - lse_ref[0, :, hh:hh + 1] for hh in range(2)]

        def block(j, carry, diagonal):
            off = pl.multiple_of(j * t, t)
            ckj = ck_ref[0, j]
            k2 = k_ref[pl.ds(off, t), :].astype(BF16)
            v2 = v_ref[pl.ds(off, t), :].astype(BF16)
            out = []
            dk2 = jnp.zeros((t, 2 * hd), F32)
            dv2 = jnp.zeros((t, 2 * hd), F32)
            for hh in range(2):
                s = _dg(qs[hh], k2, False, True) + (bias[hh] - ckj[hh:hh + 1, :])
                if diagonal:
                    keep = lax.broadcasted_iota(jnp.int32, (t, t), 0) >= lax.broadcasted_iota(jnp.int32, (t, t), 1)
                    s = jnp.where(keep, s, NEG_BIG)
                p = jnp.exp(s)
                dp = _dg(dos[hh], v2, False, True)
                ds = p * (dp - deltas[hh])
                dsb = ds.astype(BF16)
                dq, rowsum = carry[hh]
                out.append((dq + _dg(dsb, k2, False, False), rowsum + jnp.sum(ds, axis=1, keepdims=True)))
                dk2 = dk2 + _dg(dsb, qs[hh], True, False)
                dv2 = dv2 + _dg(p.astype(BF16), dos[hh], True, False)
                dck_ref[0, j, hh:hh + 1, :] -= jnp.sum(ds, axis=0, keepdims=True)
            dk_ref[pl.ds(off, t), :] += dk2
            dv_ref[pl.ds(off, t), :] += dv2
            return tuple(out)

        init = tuple((jnp.zeros((t, 2 * hd), F32), jnp.zeros((t, 1), F32)) for _ in range(2))
        res = lax.fori_loop(0, i, lambda j, c: block(j, c, False), init)
        res = block(i, res, True)
        dq_ref[...] = jnp.where(masks[0], res[0][0], res[1][0])
        for hh in range(2):
            dcq_ref[0, :, hh:hh + 1] = res[hh][1]

    blk = pl.BlockSpec((t, 2 * hd), lambda hp, i: (i, hp))
    full = pl.BlockSpec((tokens, 2 * hd), lambda hp, i: (0, hp))
    cq_spec = pl.BlockSpec((1, t, 2), lambda hp, i: (hp, i, 0))
    ck_spec = pl.BlockSpec((1, nb, 2, t), lambda hp, i: (hp, 0, 0, 0))
    tshape = jax.ShapeDtypeStruct((tokens, width), F32)
    return pl.pallas_call(
        body, name="fox_bwd", grid=(npair, nb),
        in_specs=[blk, full, full, cq_spec, ck_spec, blk, cq_spec, blk],
        out_specs=[blk, full, full, ck_spec, cq_spec],
        out_shape=[tshape, tshape, tshape, jax.ShapeDtypeStruct((npair, nb, 2, t), F32),
                   jax.ShapeDtypeStruct((npair, tokens, 2), F32)],
        compiler_params=pltpu.CompilerParams(dimension_semantics=("parallel", "arbitrary")),
    )(q, k, v, cq, ck, o, lse, do)


@jax.custom_vjp
def fox_attention(q, k, v, cum):
    return _fox_fwd_call(q, k, v, *_fox_layouts(cum))[0]


def _fox_fwd(q, k, v, cum):
    cq, ck = _fox_layouts(cum)
    o, lse = _fox_fwd_call(q, k, v, cq, ck)
    return o, (q, k, v, cq, ck, o, lse)


def _fox_bwd(res, do):
    q, k, v, cq, ck, o, lse = res
    dq, dk, dv, dck, dcq = _fox_bwd_call(q, k, v, cq, ck, o, lse, do)
    npair, nb, _, t = dck.shape
    dcum = dck.transpose(0, 2, 1, 3).reshape(2 * npair, nb * t).T + dcq.transpose(1, 0, 2).reshape(nb * t, 2 * npair)
    return dq, dk, dv, dcum


fox_attention.defvjp(_fox_fwd, _fox_bwd)


def _loss_call(y, target):
    rows, d = y.shape
    tr = _row_tile(rows, d)

    def body(y_ref, t_ref, loss_ref, dy_ref):
        @pl.when(pl.program_id(0) == 0)
        def _():
            loss_ref[...] = jnp.zeros_like(loss_ref)

        diff = y_ref[...] - t_ref[...]
        dy_ref[...] = diff * (1.0 / d)
        loss_ref[...] += (0.5 / d) * jnp.sum(jnp.sum(diff * diff, axis=1, keepdims=True), axis=0, keepdims=True)

    return pl.pallas_call(
        body, name="loss", grid=(rows // tr,),
        in_specs=[pl.BlockSpec((tr, d), lambda i: (i, 0))] * 2,
        out_specs=[pl.BlockSpec((1, 1), lambda i: (0, 0)), pl.BlockSpec((tr, d), lambda i: (i, 0))],
        out_shape=[jax.ShapeDtypeStruct((1, 1), F32), jax.ShapeDtypeStruct((rows, d), F32)],
        compiler_params=pltpu.CompilerParams(dimension_semantics=("arbitrary",)),
    )(y, target)


def _adamw_call(w, g, m, v):
    rows, cols = w.shape
    tr = _row_tile(rows, cols, budget=1024 * 1024)
    c1 = 1.0 / (1.0 - ADAM_B1 ** ADAM_STEP)
    c2 = 1.0 / (1.0 - ADAM_B2 ** ADAM_STEP)

    def body(w_ref, g_ref, m_ref, v_ref, d_ref, nm_ref, nv_ref):
        gv = g_ref[...]
        nm = ADAM_B1 * m_ref[...] + (1.0 - ADAM_B1) * gv
        nv = ADAM_B2 * v_ref[...] + (1.0 - ADAM_B2) * (gv * gv)
        nm_ref[...] = nm
        nv_ref[...] = nv
        d_ref[...] = -ADAM_LR * ((nm * c1) / (jnp.sqrt(nv * c2) + ADAM_EPS) + ADAM_WD * w_ref[...])

    spec = pl.BlockSpec((tr, cols), lambda i: (i, 0))
    shape = jax.ShapeDtypeStruct((rows, cols), F32)
    return pl.pallas_call(
        body, name="adamw", grid=(rows // tr,),
        in_specs=[spec] * 4, out_specs=[spec] * 3, out_shape=[shape] * 3,
        compiler_params=pltpu.CompilerParams(dimension_semantics=("parallel",)),
    )(w, g, m, v)


def _my_place():
    return lax.axis_index("x"), lax.axis_index("y"), lax.axis_index("c")


def _place_index(px, py, pc):
    return 4 * px + 2 * py + pc


HBM_SPEC = pl.BlockSpec(memory_space=pltpu.HBM)


def _all_gather_call(block):
    def body(x_ref, out_ref, send_sems, recv_sems, local_sem):
        x, y, c = _my_place()
        me, sibling = (x, y, c), (x, y, 1 - c)
        chips = [(1 - x, y), (x, 1 - y), (1 - x, 1 - y)]

        def slot(px, py, pc):
            return out_ref.at[_place_index(px, py, pc)]

        def copy(k, blk, to, src=None):
            return pltpu.make_async_remote_copy(
                src_ref=slot(*blk) if src is None else src, dst_ref=slot(*blk),
                send_sem=send_sems.at[k], recv_sem=recv_sems.at[k],
                device_id=to, device_id_type=pl.DeviceIdType.MESH)

        mine = pltpu.make_async_copy(x_ref, slot(*me), local_sem)
        mine.start()
        first = [copy(0, me, sibling, src=x_ref)]
        first += [copy(1 + j, me, (*chip, c), src=x_ref) for j, chip in enumerate(chips)]
        for cp in first:
            cp.start()
        passed = [copy(4 + j, (*chip, c), sibling) for j, chip in enumerate(chips)]
        for j, chip in enumerate(chips):
            copy(1 + j, (*chip, c), me).wait_recv()
            passed[j].start()
        copy(0, sibling, me).wait_recv()
        for j, chip in enumerate(chips):
            copy(4 + j, (*chip, 1 - c), me).wait_recv()
        for cp in first + passed:
            cp.wait_send()
        mine.wait()

    return pl.pallas_call(
        body, name="all_gather",
        out_shape=jax.ShapeDtypeStruct((N_DEV,) + block.shape, block.dtype),
        in_specs=[HBM_SPEC], out_specs=HBM_SPEC,
        scratch_shapes=[pltpu.SemaphoreType.DMA((7,)), pltpu.SemaphoreType.DMA((7,)), pltpu.SemaphoreType.DMA],
    )(block)


SEM_SPEC = pl.BlockSpec(memory_space=pltpu.SEMAPHORE)
SIDE_EFFECT = pltpu.SideEffectType.DATAFLOW_SIDE_EFFECTING


def _peers():
    x, y, c = _my_place()
    out = []
    for k in range(1, N_DEV):
        peer = (x ^ (k >> 2), y ^ ((k >> 1) & 1), c ^ (k & 1))
        out.append((k - 1, peer, _place_index(*peer)))
    return _place_index(x, y, c), out


def _spread_start(src, per_peer, name):
    slot = src.shape[1:] if per_peer else src.shape

    def body(src_ref, land_ref, send_sems, recv_sems, src_thru, land_thru, token):
        mine, peers = _peers()
        for k, peer, peer_idx in peers:
            pltpu.make_async_remote_copy(
                src_ref=src_ref.at[peer_idx] if per_peer else src_ref, dst_ref=land_ref.at[mine],
                send_sem=send_sems.at[k], recv_sem=recv_sems.at[k],
                device_id=peer, device_id_type=pl.DeviceIdType.MESH).start()
        token[...] = jnp.zeros_like(token)

    return pl.pallas_call(
        body, name=name,
        out_shape=(pltpu.SemaphoreType.DMA((N_DEV - 1,)), pltpu.SemaphoreType.DMA((N_DEV - 1,)),
                   pltpu.HBM(src.shape, src.dtype), pltpu.HBM((N_DEV,) + slot, src.dtype),
                   jax.ShapeDtypeStruct((8, 128), F32)),
        in_specs=(HBM_SPEC, HBM_SPEC),
        out_specs=(SEM_SPEC, SEM_SPEC, HBM_SPEC, HBM_SPEC, pl.BlockSpec(memory_space=pltpu.VMEM)),
        input_output_aliases={0: 2, 1: 3},
        compiler_params=pltpu.CompilerParams(has_side_effects=SIDE_EFFECT),
    )(pltpu.with_memory_space_constraint(src, pltpu.HBM),
      pltpu.with_memory_space_constraint(lax.empty((N_DEV,) + slot, src.dtype), pltpu.HBM))


def _spread_wait(handles, after, per_peer, name):
    send_sems, recv_sems, src_thru, land_thru = handles

    def body(src_ref, land_ref, send_sems, recv_sems, after_ref, src_dead, got_ref):
        _, peers = _peers()
        for k, peer, peer_idx in peers:
            copy = pltpu.make_async_remote_copy(
                src_ref=src_ref.at[peer_idx] if per_peer else src_ref, dst_ref=land_ref.at[peer_idx],
                send_sem=send_sems.at[k], recv_sem=recv_sems.at[k],
                device_id=peer, device_id_type=pl.DeviceIdType.MESH)
            copy.wait_send()
            copy.wait_recv()

    return pl.pallas_call(
        body, name=name,
        out_shape=(pltpu.HBM(src_thru.shape, src_thru.dtype), pltpu.HBM(land_thru.shape, land_thru.dtype)),
        in_specs=(HBM_SPEC, HBM_SPEC, SEM_SPEC, SEM_SPEC, pl.BlockSpec(memory_space=pl.ANY)),
        out_specs=(HBM_SPEC, HBM_SPEC), input_output_aliases={0: 0, 1: 1},
        compiler_params=pltpu.CompilerParams(has_side_effects=SIDE_EFFECT),
    )(src_thru, land_thru, send_sems, recv_sems, after)


def _sum_slots_call(slots):
    _, rows, cols = slots.shape
    tr = _row_tile(rows, cols, budget=512 * 1024)

    def body(s_ref, o_ref):
        acc = s_ref[0].astype(F32)
        for j in range(1, N_DEV):
            acc = acc + s_ref[j].astype(F32)
        o_ref[...] = acc

    return pl.pallas_call(
        body, name="sum_slots", grid=(rows // tr,),
        in_specs=[pl.BlockSpec((N_DEV, tr, cols), lambda i: (0, i, 0))],
        out_specs=pl.BlockSpec((tr, cols), lambda i: (i, 0)),
        out_shape=jax.ShapeDtypeStruct((rows, cols), F32),
        compiler_params=pltpu.CompilerParams(dimension_semantics=("parallel",)),
    )(slots)


def _with_own_slot(got, own, mine):
    return lax.dynamic_update_index_in_dim(got, own, mine, 0)


def _pack(vectors, width):
    flat = jnp.concatenate([v.reshape(-1) for v in vectors])
    return jnp.pad(flat, (0, width - flat.shape[0])).reshape(width // 128, 128)


def _unpack(packed, like):
    flat = packed.reshape(-1)
    out, at = [], 0
    for v in like:
        out.append(flat[at:at + v.size].reshape(v.shape))
        at += v.size
    return tuple(out)


def _sum_over_devices(grads):
    n = sum(v.size for v in grads)
    width = -(-n // 1024) * 1024
    return _unpack(_sum_slots_call(_all_gather_call(_pack(grads, width))), grads)


def _cols_from_slots(slots):
    n, rows, cols = slots.shape
    return slots.transpose(1, 0, 2).reshape(rows, n * cols)


def _rows_from_slots(slots):
    return slots.reshape(-1, slots.shape[2])


def _pad128(n):
    return -(-n // 128) * 128


def _in_proj_layout(slots, rcols, fcols):
    w_in = _cols_from_slots(slots)
    padc = lambda w, n: jnp.pad(w, ((0, 0), (0, n - w.shape[1])))
    return jnp.concatenate([padc(w_in[:, :rcols], _pad128(rcols)), padc(w_in[:, rcols:rcols + fcols], _pad128(fcols)),
                            w_in[:, rcols + fcols:]], axis=1)


def _stage_embed(meta, x, n1, lp):
    h0 = jnp.concatenate([meta, x, jnp.zeros((lp - meta.shape[0] - x.shape[0], x.shape[1]), F32)], axis=0)
    return h0, rmsnorm(h0, n1)


def _stage_mix(proj, small, w2, a2, g2, rw, fw):
    (mu, w0, a0, k_k, k_a, r_k, gn_w, gn_b, q_g, k_g, f_bias) = small
    dl, al, gl = w2.shape[0], a2.shape[0], g2.shape[0]
    rcols = 3 * rw + dl + al + gl
    fcols = 3 * fw + fw // HEAD_DIM
    rpad, fpad = _pad128(rcols), _pad128(fcols)
    z_r, z_f, z_g = proj[:, :rcols], proj[:, rpad:rpad + fcols], proj[:, rpad + fpad:]

    z_prev = jnp.pad(z_r, ((1, 0), (0, 0)))[:-1]
    z = z_r + bmul(z_prev - z_r, mu)
    r, k, v = z[:, :rw], z[:, rw:2 * rw], z[:, 2 * rw:3 * rw]
    wd, ad, gd = z[:, 3 * rw:3 * rw + dl], z[:, 3 * rw + dl:3 * rw + dl + al], z[:, 3 * rw + dl + al:]
    w_log = -jax.nn.softplus(-badd(dense(jnp.tanh(wd), w2), w0)) - 0.5
    lw = -jnp.exp(w_log)
    a_sig = jax.nn.sigmoid(badd(dense(ad, a2), a0))
    g = dense(jax.nn.sigmoid(gd), g2)
    kk = head_l2norm(bmul(k, k_k))
    kf = k * (1.0 + bmul(a_sig - 1.0, k_a))
    y = wkv7(r, lw, kf, v, -kk, kk * a_sig)
    y_a = gn_bonus(y, r, kf, v, gn_w, gn_b, r_k.reshape(1, rw)) * g

    fq, fk, fv, fl = z_f[:, :fw], z_f[:, fw:2 * fw], z_f[:, 2 * fw:3 * fw], z_f[:, 3 * fw:]
    fq = head_rms(fq, q_g) * (HEAD_DIM ** -0.5)
    fk = head_rms(fk, k_g)
    cum = jnp.cumsum(jax.nn.log_sigmoid(badd(fl, f_bias)), axis=0)
    y_b = fox_attention(fq, fk, fv, cum)
    return y_a, y_b, jax.nn.sigmoid(z_g)


def _stage_merge(h0, y_a, y_b, gates, w_a, w_b, w_o):
    d = h0.shape[1]
    merged = gates[:, :d] * dense(y_a, w_a) + gates[:, d:] * dense(y_b, w_b)
    return h0 + dense(merged, w_o)


def _stage_ffn(h1, n2, w_gu, w_dn):
    gu = dense(rmsnorm(h1, n2), w_gu)
    dff = w_dn.shape[0]
    return h1 + dense(jax.nn.silu(gu[:, :dff]) * gu[:, dff:], w_dn)


SHARDED = ("meta_tokens", "w_in", "rwkv_w2", "rwkv_a2", "rwkv_g2", "w_branch_a", "w_branch_b", "w_o", "w_gate_up", "w_down")
SMALL = ("norm1_g", "rwkv_mu", "rwkv_w0", "rwkv_a0", "rwkv_k_k", "rwkv_k_a", "rwkv_r_k", "rwkv_gn_w", "rwkv_gn_b",
         "fox_q_norm_g", "fox_k_norm_g", "fox_f_bias", "norm2_g")
WEIGHTS = ("meta_tokens", "norm1_g", "w_in", "rwkv_mu", "rwkv_w0", "rwkv_w2", "rwkv_a0", "rwkv_a2", "rwkv_g2", "rwkv_k_k",
           "rwkv_k_a", "rwkv_r_k", "rwkv_gn_w", "rwkv_gn_b", "fox_q_norm_g", "fox_k_norm_g", "fox_f_bias", "w_branch_a",
           "w_branch_b", "w_o", "norm2_g", "w_gate_up", "w_down")


def _as2d(a):
    return a.reshape(-1, a.shape[-1])


def kernel(x, meta_tokens, norm1_g, w_in, rwkv_mu, rwkv_w0, rwkv_w2, rwkv_a0, rwkv_a2, rwkv_g2, rwkv_k_k, rwkv_k_a, rwkv_r_k, rwkv_gn_w, rwkv_gn_b, fox_q_norm_g, fox_k_norm_g, fox_f_bias, w_branch_a, w_branch_b, w_o, norm2_g, w_gate_up, w_down, loss_target, m_meta_tokens, m_norm1_g, m_w_in, m_rwkv_mu, m_rwkv_w0, m_rwkv_w2, m_rwkv_a0, m_rwkv_a2, m_rwkv_g2, m_rwkv_k_k, m_rwkv_k_a, m_rwkv_r_k, m_rwkv_gn_w, m_rwkv_gn_b, m_fox_q_norm_g, m_fox_k_norm_g, m_fox_f_bias, m_w_branch_a, m_w_branch_b, m_w_o, m_norm2_g, m_w_gate_up, m_w_down, v_meta_tokens, v_norm1_g, v_w_in, v_rwkv_mu, v_rwkv_w0, v_rwkv_w2, v_rwkv_a0, v_rwkv_a2, v_rwkv_g2, v_rwkv_k_k, v_rwkv_k_a, v_rwkv_r_k, v_rwkv_gn_w, v_rwkv_gn_b, v_fox_q_norm_g, v_fox_k_norm_g, v_fox_f_bias, v_w_branch_a, v_w_branch_b, v_w_o, v_norm2_g, v_w_gate_up, v_w_down):
    given = dict(locals())
    w = {n: given[n] for n in WEIGHTS}
    assert rwkv_r_k.shape[-1] == HEAD_DIM
    n_meta, seq = meta_tokens.shape[0], x.shape[1]
    tokens = n_meta + seq
    lp = -(-tokens // TOKEN_TILE) * TOKEN_TILE
    mine = _place_index(*(lax.axis_index(a) for a in MESH_AXES))
    x2 = x[0]

    blocks = {n: _as2d(w[n]).astype(F32 if n == "meta_tokens" else BF16) for n in SHARDED}
    started = {n: _spread_start(blocks[n], False, "gather_start_" + n) for n in SHARDED}
    zero = sum(h[4][0, 0] for h in started.values())

    def gathered(n, after):
        own, got = _spread_wait(started[n][:4], after, False, "gather_wait_" + n)
        return _with_own_slot(got, own, mine)

    sm = {n: _as2d(w[n]) for n in SMALL}
    small_mix = tuple(sm[n] for n in SMALL[1:-1])
    n1 = sm["norm1_g"] + zero
    rw, fw = w_branch_a.shape[-2], w_branch_b.shape[-2]
    rcols = 3 * rw + rwkv_w2.shape[-2] + rwkv_a2.shape[-2] + rwkv_g2.shape[-2]
    fcols = 3 * fw + fw // HEAD_DIM

    meta, un_meta = jax.vjp(_cols_from_slots, gathered("meta_tokens", x2))
    (h0, xn), vjp_embed = jax.vjp(lambda m, xs, g: _stage_embed(m, xs, g, lp), meta, x2, n1)
    w_cat, un_in = jax.vjp(lambda s: _in_proj_layout(s, rcols, fcols), gathered("w_in", xn))
    xn_b = xn.astype(BF16)
    proj = _matmul(xn_b, w_cat, name="in_proj")
    w2, un_w2 = jax.vjp(_cols_from_slots, gathered("rwkv_w2", xn))
    a2, un_a2 = jax.vjp(_cols_from_slots, gathered("rwkv_a2", xn))
    g2, un_g2 = jax.vjp(_cols_from_slots, gathered("rwkv_g2", xn))
    (y_a, y_b, gates), vjp_mix = jax.vjp(lambda p, s, a, b, c: _stage_mix(p, s, a, b, c, rw, fw), proj, small_mix, w2, a2, g2)
    w_a, un_wa = jax.vjp(_cols_from_slots, gathered("w_branch_a", y_a))
    w_b, un_wb = jax.vjp(_cols_from_slots, gathered("w_branch_b", y_a))
    w_o_full, un_wo = jax.vjp(_rows_from_slots, gathered("w_o", y_a))
    h1, vjp_merge = jax.vjp(_stage_merge, h0, y_a, y_b, gates, w_a, w_b, w_o_full)
    w_gu, un_gu = jax.vjp(_cols_from_slots, gathered("w_gate_up", h1))
    w_dn, un_dn = jax.vjp(_rows_from_slots, gathered("w_down", h1))
    y, vjp_ffn = jax.vjp(_stage_ffn, h1, sm["norm2_g"], w_gu, w_dn)

    loss_part, dy_real = _loss_call(y[n_meta:tokens], loss_target[0])
    dy = jnp.pad(dy_real, ((n_meta, lp - tokens), (0, 0)))
    loss = lax.psum(loss_part[0, 0], MESH_AXES)

    sent = {}

    def send_grad(n, dmat, unlayout):
        sent[n] = _spread_start(unlayout(dmat)[0], True, "grad_start_" + n)
        return sent[n][4][0, 0]

    d_h1, d_n2, d_wgu, d_wdn = vjp_ffn(dy)
    behind = send_grad("w_gate_up", d_wgu, un_gu) + send_grad("w_down", d_wdn, un_dn)
    d_h0, d_ya, d_yb, d_gates, d_wa, d_wb, d_wo = vjp_merge(d_h1 + behind)
    behind = send_grad("w_o", d_wo, un_wo) + send_grad("w_branch_a", d_wa, un_wa) + send_grad("w_branch_b", d_wb, un_wb)
    d_proj, d_small_mix, d_w2, d_a2, d_g2 = vjp_mix((d_ya + behind, d_yb, d_gates))
    dproj_b = d_proj.astype(BF16)
    d_wcat = _matmul(xn_b, dproj_b, ta=True, out_dtype=BF16, name="in_proj_dw")
    send_grad("w_in", d_wcat, un_in)
    d_xn = _matmul(dproj_b, w_cat, tb=True, out_dtype=F32, name="in_proj_dx", after=sent["w_in"][4])
    send_grad("rwkv_w2", d_w2, un_w2)
    send_grad("rwkv_a2", d_a2, un_a2)
    send_grad("rwkv_g2", d_g2, un_g2)
    d_meta, g_x, d_n1 = vjp_embed((d_h0, d_xn))
    send_grad("meta_tokens", d_meta, un_meta)

    grads = dict(zip(SMALL, _sum_over_devices((d_n1, *d_small_mix, d_n2))))
    grads = {n: g.reshape(w[n].shape) for n, g in grads.items()}

    delta, new_m, new_v = {}, {}, {}
    after = g_x
    for n in ("w_gate_up", "w_down", "w_o", "w_branch_a", "w_branch_b", "rwkv_g2", "rwkv_a2", "rwkv_w2", "meta_tokens", "w_in"):
        src, got = _spread_wait(sent[n][:4], after, True, "grad_wait_" + n)
        g = _sum_slots_call(_with_own_slot(got, lax.dynamic_index_in_dim(src, mine, 0, keepdims=False), mine))
        grads[n] = g.reshape(w[n].shape)
        d_, m_, v_ = _adamw_call(_as2d(w[n]), g, _as2d(given["m_" + n]), _as2d(given["v_" + n]))
        delta[n], new_m[n], new_v[n] = (t.reshape(w[n].shape) for t in (d_, m_, v_))
        after = m_
    n_small = sum(w[n].size for n in SMALL)
    width = -(-n_small // 1024) * 1024
    packs = [_pack([src[n] if p == "" else given[p + n] for n in SMALL], width)
             for p, src in (("", w), ("", grads), ("m_", None), ("v_", None))]
    like = [w[n] for n in SMALL]
    for out, packed in zip((delta, new_m, new_v), _adamw_call(*packs)):
        out.update(dict(zip(SMALL, _unpack(packed, like))))

    return (loss, g_x[None], *[grads[n] for n in WEIGHTS], *[delta[n] for n in WEIGHTS],
            *[new_m[n] for n in WEIGHTS], *[new_v[n] for n in WEIGHTS])
```

```python
import functools

import jax
import jax.numpy as jnp
from jax import lax
from jax.experimental import pallas as pl
from jax.experimental.pallas import tpu as pltpu

F32 = jnp.float32
BF16 = jnp.bfloat16

N_DEV = 8
MESH_AXES = ("x", "y", "c")
HEAD_DIM = 64
TOKEN_TILE = 128
WKV_CHUNK = 64
WKV_PAIRS_PER_STEP = 4
PAIR = 2 * HEAD_DIM
ATTN_BLOCK = 128
ATTN_BLOCK_BIG = 384
RMS_EPS = 1e-6
GN_EPS = 64e-5
L2_FLOOR = 1e-12
NEG_BIG = -1e30
ADAM_LR, ADAM_B1, ADAM_B2, ADAM_EPS, ADAM_WD, ADAM_STEP = 0.001, 0.9, 0.999, 1e-08, 0.01, 10
VMEM_BYTES_V7X = 64 * 1024 * 1024
VMEM_LIMIT_CAP = 56 * 1024 * 1024
VMEM_LIMIT_FLOOR = 32 * 1024 * 1024
MATMUL_VMEM_BUDGET = 36 * 1024 * 1024
GRID_STEP_BYTES = 1024 * 1024


def _vmem_limit(estimate_bytes):
    return int(min(max(estimate_bytes * 5 // 4, VMEM_LIMIT_FLOOR), VMEM_LIMIT_CAP))


def _pick(dim, cands):
    for c in cands:
        if dim % c == 0:
            return c
    return dim


def _row_tile(rows, width, itemsize=4, budget=2 * 1024 * 1024):
    for c in (1408, 1024, 704, 512, 384, 256, 128, 64, 32, 16, 8):
        if rows % c == 0 and c * width * itemsize <= budget:
            return c
    return rows


def _dg(a, b, ta, tb):
    dims = (((0 if ta else 1,), (1 if tb else 0,)), ((), ()))
    return lax.dot_general(a, b, dims, preferred_element_type=F32)


def _split(x, n):
    parts = []
    for _ in range(n):
        h = x.astype(BF16)
        parts.append(h)
        x = x - h.astype(F32)
    return parts


def _mm(a, b, ta=False, tb=False):
    return _dg(a.astype(BF16), b.astype(BF16), ta, tb)


def _matmul(a, b, ta=False, tb=False, out_dtype=F32, name="matmul", after=None, b_slots=False, out_slots=0):
    if ta:
        kdim, m = a.shape
    else:
        m, kdim = a.shape
    if b_slots:
        n_slots, brows, bcols = b.shape
        n, k2 = (brows, n_slots * bcols) if tb else (n_slots * bcols, brows)
    elif tb:
        n, k2 = b.shape
    else:
        k2, n = b.shape
    assert kdim == k2, (a.shape, b.shape, ta, tb)
    sa, sb, so = a.dtype.itemsize, b.dtype.itemsize, jnp.dtype(out_dtype).itemsize
    n_unit = bcols if (b_slots and not tb) else (n // out_slots if out_slots else n)
    k_unit = bcols if (b_slots and tb) else kdim
    tm, tn, tk = _matmul_tiles(m, n, kdim, ta, sa, sb, so, n_unit, k_unit)
    nk = kdim // tk

    order = () if after is None else (after,)

    def body(a_ref, b_ref, *rest):
        o_ref, acc = rest[len(order)], rest[len(order) + 1:]
        part = _dg(a_ref[...].astype(BF16), b_ref[...].astype(BF16), ta, tb)
        if nk == 1:
            o_ref[...] = part.astype(o_ref.dtype)
            return
        kk = pl.program_id(2)

        @pl.when(kk == 0)
        def _():
            acc[0][...] = part

        @pl.when(kk > 0)
        def _():
            acc[0][...] += part

        @pl.when(kk == nk - 1)
        def _():
            o_ref[...] = acc[0][...].astype(o_ref.dtype)

    a_spec = pl.BlockSpec((tk, tm), lambda i, j, k: (k, i)) if ta else pl.BlockSpec((tm, tk), lambda i, j, k: (i, k))
    if b_slots and tb:
        per = bcols // tk
        b_spec = pl.BlockSpec((None, tn, tk), lambda i, j, k: (k // per, j, k % per))
    elif b_slots:
        per = bcols // tn
        b_spec = pl.BlockSpec((None, tk, tn), lambda i, j, k: (j // per, k, j % per))
    else:
        b_spec = pl.BlockSpec((tn, tk), lambda i, j, k: (j, k)) if tb else pl.BlockSpec((tk, tn), lambda i, j, k: (k, j))
    if out_slots:
        per_out = n // out_slots // tn
        out_spec = pl.BlockSpec((None, tm, tn), lambda i, j, k: (j // per_out, i, j % per_out))
        out_shape = jax.ShapeDtypeStruct((out_slots, m, n // out_slots), out_dtype)
    else:
        out_spec = pl.BlockSpec((tm, tn), lambda i, j, k: (i, j))
        out_shape = jax.ShapeDtypeStruct((m, n), out_dtype)
    return pl.pallas_call(
        body, name=name,
        grid=(m // tm, n // tn, nk),
        in_specs=[a_spec, b_spec] + [pl.BlockSpec(memory_space=pl.ANY)] * len(order),
        out_specs=out_spec,
        out_shape=out_shape,
        scratch_shapes=[pltpu.VMEM((tm, tn), F32)] if nk > 1 else [],
        compiler_params=pltpu.CompilerParams(dimension_semantics=("parallel", "parallel", "arbitrary"),
                                             vmem_limit_bytes=_vmem_limit(_matmul_vmem(tm, tn, tk, nk, sa, sb, so))),
    )(a, b, *order)


def _matmul_vmem(tm, tn, tk, nk, sa, sb, so):
    return 2 * (tm * tk * sa + tk * tn * sb + tm * tn * so) + tm * tn * 4 + (tm * tn * 4 if nk > 1 else 0)


def _matmul_tiles(m, n, kdim, ta, sa, sb, so, n_unit, k_unit):
    lane = (2048, 1408, 1024, 640, 512, 384, 256, 128)
    sublane = (2048, 1408, 1024, 704, 512, 384, 256, 128)
    divs = lambda dim, cands: [c for c in cands if dim % c == 0] or [dim]
    best = None
    for tm in divs(m, lane if ta else sublane):
        for tn in divs(n_unit, lane):
            for tk in divs(k_unit, sublane if ta else lane) + ([kdim] if (kdim <= 2048 and k_unit == kdim) else []):
                nk, nm, nn = kdim // tk, m // tm, n // tn
                if _matmul_vmem(tm, tn, tk, nk, sa, sb, so) > MATMUL_VMEM_BUDGET:
                    continue
                a_bytes = m * kdim * sa * (nn if nk > 1 else 1)
                b_bytes = kdim * n * sb * (1 if (nk == 1 and nn == 1) else nm)
                cost = a_bytes + b_bytes + m * n * so + nm * nn * nk * GRID_STEP_BYTES
                if best is None or cost < best[0]:
                    best = (cost, tm, tn, tk)
    return best[1:]


@jax.custom_vjp
def dense(x, w):
    return _matmul(x.astype(BF16), w, name="dense_fwd")


def _dense_fwd(x, w):
    assert x.dtype == F32
    xb = x.astype(BF16)
    return _matmul(xb, w, name="dense_fwd"), (xb, w)


def _dense_bwd(res, dy):
    xb, w = res
    dyb = dy.astype(BF16)
    dx = _matmul(dyb, w, tb=True, out_dtype=F32, name="dense_dx")
    dw = _matmul(xb, dyb, ta=True, out_dtype=w.dtype, name="dense_dw")
    return dx, dw


dense.defvjp(_dense_fwd, _dense_bwd)


@jax.custom_vjp
def dense_cols(x, w_slots):
    return _matmul(x.astype(BF16), w_slots, b_slots=True, name="dense_cols_fwd")


def _dense_cols_fwd(x, w_slots):
    assert x.dtype == F32
    xb = x.astype(BF16)
    return _matmul(xb, w_slots, b_slots=True, name="dense_cols_fwd"), (xb, w_slots)


def _dense_cols_bwd(res, dy):
    xb, w_slots = res
    dyb = dy.astype(BF16)
    dx = _matmul(dyb, w_slots, tb=True, b_slots=True, out_dtype=F32, name="dense_cols_dx")
    dw = _matmul(xb, dyb, ta=True, out_slots=w_slots.shape[0], out_dtype=w_slots.dtype, name="dense_cols_dw")
    return dx, dw


dense_cols.defvjp(_dense_cols_fwd, _dense_cols_bwd)


def _rms_fwd_call(x, g):
    rows, d = x.shape
    tr = _row_tile(rows, d)

    def body(x_ref, g_ref, y_ref):
        xv = x_ref[...]
        rstd = lax.rsqrt(jnp.mean(xv * xv, axis=1, keepdims=True) + RMS_EPS)
        y_ref[...] = (xv * rstd) * g_ref[...]

    return pl.pallas_call(
        body, name="rms_fwd", grid=(rows // tr,),
        in_specs=[pl.BlockSpec((tr, d), lambda i: (i, 0)), pl.BlockSpec((1, d), lambda i: (0, 0))],
        out_specs=pl.BlockSpec((tr, d), lambda i: (i, 0)),
        out_shape=jax.ShapeDtypeStruct((rows, d), F32),
        compiler_params=pltpu.CompilerParams(dimension_semantics=("parallel",)),
    )(x, g)


def _rms_bwd_call(x, g, dy):
    rows, d = x.shape
    tr = _row_tile(rows, d)

    def body(x_ref, g_ref, dy_ref, dx_ref, dg_ref):
        @pl.when(pl.program_id(0) == 0)
        def _():
            dg_ref[...] = jnp.zeros_like(dg_ref)

        xv = x_ref[...]
        dyv = dy_ref[...]
        rstd = lax.rsqrt(jnp.mean(xv * xv, axis=1, keepdims=True) + RMS_EPS)
        xhat = xv * rstd
        dxhat = dyv * g_ref[...]
        dx_ref[...] = rstd * (dxhat - xhat * jnp.mean(dxhat * xhat, axis=1, keepdims=True))
        dg_ref[...] += jnp.sum(dyv * xhat, axis=0, keepdims=True)

    return pl.pallas_call(
        body, name="rms_bwd", grid=(rows // tr,),
        in_specs=[pl.BlockSpec((tr, d), lambda i: (i, 0)), pl.BlockSpec((1, d), lambda i: (0, 0)),
                  pl.BlockSpec((tr, d), lambda i: (i, 0))],
        out_specs=[pl.BlockSpec((tr, d), lambda i: (i, 0)), pl.BlockSpec((1, d), lambda i: (0, 0))],
        out_shape=[jax.ShapeDtypeStruct((rows, d), F32), jax.ShapeDtypeStruct((1, d), F32)],
        compiler_params=pltpu.CompilerParams(dimension_semantics=("arbitrary",)),
    )(x, g, dy)


@jax.custom_vjp
def rmsnorm(x, g):
    return _rms_fwd_call(x, g)


rmsnorm.defvjp(lambda x, g: (_rms_fwd_call(x, g), (x, g)), lambda res, dy: tuple(_rms_bwd_call(res[0], res[1], dy)))


def _bcast_call(x, p, mul):
    rows, d = x.shape
    tr = _row_tile(rows, d)

    def body(x_ref, p_ref, y_ref):
        y_ref[...] = x_ref[...] * p_ref[...] if mul else x_ref[...] + p_ref[...]

    return pl.pallas_call(
        body, name="bcast_mul" if mul else "bcast_add", grid=(rows // tr,),
        in_specs=[pl.BlockSpec((tr, d), lambda i: (i, 0)), pl.BlockSpec((1, d), lambda i: (0, 0))],
        out_specs=pl.BlockSpec((tr, d), lambda i: (i, 0)),
        out_shape=jax.ShapeDtypeStruct((rows, d), F32),
        compiler_params=pltpu.CompilerParams(dimension_semantics=("parallel",)),
    )(x, p)


def _colsum_call(a, b=None):
    rows, d = a.shape
    tr = _row_tile(rows, d)
    ops = (a,) if b is None else (a, b)

    def body(*refs):
        o_ref = refs[-1]

        @pl.when(pl.program_id(0) == 0)
        def _():
            o_ref[...] = jnp.zeros_like(o_ref)

        v = refs[0][...] if b is None else refs[0][...] * refs[1][...]
        o_ref[...] += jnp.sum(v, axis=0, keepdims=True)

    return pl.pallas_call(
        body, name="colsum", grid=(rows // tr,),
        in_specs=[pl.BlockSpec((tr, d), lambda i: (i, 0))] * len(ops),
        out_specs=pl.BlockSpec((1, d), lambda i: (0, 0)),
        out_shape=jax.ShapeDtypeStruct((1, d), F32),
        compiler_params=pltpu.CompilerParams(dimension_semantics=("arbitrary",)),
    )(*ops)


@jax.custom_vjp
def bmul(x, p):
    return _bcast_call(x, p, True)


bmul.defvjp(lambda x, p: (_bcast_call(x, p, True), (x, p)),
            lambda res, dy: (_bcast_call(dy, res[1], True), _colsum_call(dy, res[0])))


@jax.custom_vjp
def badd(x, p):
    return _bcast_call(x, p, False)


badd.defvjp(lambda x, p: (_bcast_call(x, p, False), None), lambda res, dy: (dy, _colsum_call(dy)))


def _heads(width):
    return [slice(h * HEAD_DIM, (h + 1) * HEAD_DIM) for h in range(width // HEAD_DIM)]


def _head_rms_fwd_call(x, g):
    rows, w = x.shape
    tr = _row_tile(rows, w, budget=1024 * 1024)

    def body(x_ref, g_ref, y_ref):
        for sl in _heads(w):
            xv = x_ref[:, sl]
            rstd = lax.rsqrt(jnp.mean(xv * xv, axis=1, keepdims=True) + RMS_EPS)
            y_ref[:, sl] = (xv * rstd) * g_ref[...]

    return pl.pallas_call(
        body, name="head_rms_fwd", grid=(rows // tr,),
        in_specs=[pl.BlockSpec((tr, w), lambda i: (i, 0)), pl.BlockSpec((1, HEAD_DIM), lambda i: (0, 0))],
        out_specs=pl.BlockSpec((tr, w), lambda i: (i, 0)),
        out_shape=jax.ShapeDtypeStruct((rows, w), F32),
        compiler_params=pltpu.CompilerParams(dimension_semantics=("parallel",)),
    )(x, g)


def _head_rms_bwd_call(x, g, dy):
    rows, w = x.shape
    tr = _row_tile(rows, w, budget=1024 * 1024)

    def body(x_ref, g_ref, dy_ref, dx_ref, dg_ref):
        @pl.when(pl.program_id(0) == 0)
        def _():
            dg_ref[...] = jnp.zeros_like(dg_ref)

        dg = jnp.zeros((1, HEAD_DIM), F32)
        for sl in _heads(w):
            xv = x_ref[:, sl]
            dyv = dy_ref[:, sl]
            rstd = lax.rsqrt(jnp.mean(xv * xv, axis=1, keepdims=True) + RMS_EPS)
            xhat = xv * rstd
            dxhat = dyv * g_ref[...]
            dx_ref[:, sl] = rstd * (dxhat - xhat * jnp.mean(dxhat * xhat, axis=1, keepdims=True))
            dg = dg + jnp.sum(dyv * xhat, axis=0, keepdims=True)
        dg_ref[...] += dg

    return pl.pallas_call(
        body, name="head_rms_bwd", grid=(rows // tr,),
        in_specs=[pl.BlockSpec((tr, w), lambda i: (i, 0)), pl.BlockSpec((1, HEAD_DIM), lambda i: (0, 0)),
                  pl.BlockSpec((tr, w), lambda i: (i, 0))],
        out_specs=[pl.BlockSpec((tr, w), lambda i: (i, 0)), pl.BlockSpec((1, HEAD_DIM), lambda i: (0, 0))],
        out_shape=[jax.ShapeDtypeStruct((rows, w), F32), jax.ShapeDtypeStruct((1, HEAD_DIM), F32)],
        compiler_params=pltpu.CompilerParams(dimension_semantics=("arbitrary",)),
    )(x, g, dy)


@jax.custom_vjp
def head_rms(x, g):
    return _head_rms_fwd_call(x, g)


head_rms.defvjp(lambda x, g: (_head_rms_fwd_call(x, g), (x, g)),
                lambda res, dy: tuple(_head_rms_bwd_call(res[0], res[1], dy)))


def _head_l2_call(x, dy=None):
    rows, w = x.shape
    tr = _row_tile(rows, w, budget=1024 * 1024)
    ops = (x,) if dy is None else (x, dy)

    def body(*refs):
        o_ref = refs[-1]
        for sl in _heads(w):
            xv = refs[0][:, sl]
            nrm = jnp.sqrt(jnp.sum(xv * xv, axis=1, keepdims=True))
            live = nrm > L2_FLOOR
            inv = 1.0 / jnp.maximum(nrm, L2_FLOOR)
            y = xv * inv
            if dy is None:
                o_ref[:, sl] = y
            else:
                dyv = refs[1][:, sl]
                proj = jnp.where(live, jnp.sum(dyv * y, axis=1, keepdims=True), 0.0)
                o_ref[:, sl] = (dyv - y * proj) * inv

    return pl.pallas_call(
        body, name="head_l2_fwd" if dy is None else "head_l2_bwd", grid=(rows // tr,),
        in_specs=[pl.BlockSpec((tr, w), lambda i: (i, 0))] * len(ops),
        out_specs=pl.BlockSpec((tr, w), lambda i: (i, 0)),
        out_shape=jax.ShapeDtypeStruct((rows, w), F32),
        compiler_params=pltpu.CompilerParams(dimension_semantics=("parallel",)),
    )(*ops)


@jax.custom_vjp
def head_l2norm(x):
    return _head_l2_call(x)


head_l2norm.defvjp(lambda x: (_head_l2_call(x), x), lambda x, dy: (_head_l2_call(x, dy),))


def _gn_fwd_call(y, r, kf, v, gw, gb, rk):
    rows, w = y.shape
    tr = _row_tile(rows, w, budget=512 * 1024)

    def body(y_ref, r_ref, kf_ref, v_ref, gw_ref, gb_ref, rk_ref, o_ref):
        for sl in _heads(w):
            yv = y_ref[:, sl]
            yc = yv - jnp.mean(yv, axis=1, keepdims=True)
            rstd = lax.rsqrt(jnp.mean(yc * yc, axis=1, keepdims=True) + GN_EPS)
            s = jnp.sum(r_ref[:, sl] * kf_ref[:, sl] * rk_ref[:, sl], axis=1, keepdims=True)
            o_ref[:, sl] = (yc * rstd) * gw_ref[:, sl] + gb_ref[:, sl] + s * v_ref[:, sl]

    tok = pl.BlockSpec((tr, w), lambda i: (i, 0))
    par = pl.BlockSpec((1, w), lambda i: (0, 0))
    return pl.pallas_call(
        body, name="gn_bonus_fwd", grid=(rows // tr,),
        in_specs=[tok] * 4 + [par] * 3, out_specs=tok,
        out_shape=jax.ShapeDtypeStruct((rows, w), F32),
        compiler_params=pltpu.CompilerParams(dimension_semantics=("parallel",)),
    )(y, r, kf, v, gw, gb, rk)


def _gn_bwd_call(y, r, kf, v, gw, gb, rk, do):
    rows, w = y.shape
    tr = _row_tile(rows, w, budget=512 * 1024)

    def body(y_ref, r_ref, kf_ref, v_ref, gw_ref, rk_ref, do_ref,
             dy_ref, dr_ref, dkf_ref, dv_ref, dgw_ref, dgb_ref, drk_ref):
        @pl.when(pl.program_id(0) == 0)
        def _():
            dgw_ref[...] = jnp.zeros_like(dgw_ref)
            dgb_ref[...] = jnp.zeros_like(dgb_ref)
            drk_ref[...] = jnp.zeros_like(drk_ref)

        for sl in _heads(w):
            yv, rv, kv, vv, dov = y_ref[:, sl], r_ref[:, sl], kf_ref[:, sl], v_ref[:, sl], do_ref[:, sl]
            yc = yv - jnp.mean(yv, axis=1, keepdims=True)
            rstd = lax.rsqrt(jnp.mean(yc * yc, axis=1, keepdims=True) + GN_EPS)
            yhat = yc * rstd
            dyhat = dov * gw_ref[:, sl]
            dy_ref[:, sl] = rstd * (dyhat - jnp.mean(dyhat, axis=1, keepdims=True)
                                    - yhat * jnp.mean(dyhat * yhat, axis=1, keepdims=True))
            rkv = rk_ref[:, sl]
            s = jnp.sum(rv * kv * rkv, axis=1, keepdims=True)
            ds = jnp.sum(dov * vv, axis=1, keepdims=True)
            dv_ref[:, sl] = s * dov
            dr_ref[:, sl] = ds * kv * rkv
            dkf_ref[:, sl] = ds * rv * rkv
            dgw_ref[:, sl] += jnp.sum(dov * yhat, axis=0, keepdims=True)
            dgb_ref[:, sl] += jnp.sum(dov, axis=0, keepdims=True)
            drk_ref[:, sl] += jnp.sum(ds * rv * kv, axis=0, keepdims=True)

    tok = pl.BlockSpec((tr, w), lambda i: (i, 0))
    par = pl.BlockSpec((1, w), lambda i: (0, 0))
    tshape = jax.ShapeDtypeStruct((rows, w), F32)
    pshape = jax.ShapeDtypeStruct((1, w), F32)
    return pl.pallas_call(
        body, name="gn_bonus_bwd", grid=(rows // tr,),
        in_specs=[tok] * 4 + [par] * 2 + [tok], out_specs=[tok] * 4 + [par] * 3,
        out_shape=[tshape] * 4 + [pshape] * 3,
        compiler_params=pltpu.CompilerParams(dimension_semantics=("arbitrary",)),
    )(y, r, kf, v, gw, rk, do)


@jax.custom_vjp
def gn_bonus(y, r, kf, v, gw, gb, rk):
    return _gn_fwd_call(y, r, kf, v, gw, gb, rk)


def _gn_bwd(res, do):
    y, r, kf, v, gw, gb, rk = res
    dy, dr, dkf, dv, dgw, dgb, drk = _gn_bwd_call(y, r, kf, v, gw, gb, rk, do)
    return dy, dr, dkf, dv, dgw, dgb, drk


gn_bonus.defvjp(lambda *a: (_gn_fwd_call(*a), a), _gn_bwd)


def _pair_masks(rows):
    lane = lax.broadcasted_iota(jnp.int32, (rows, PAIR), 1)
    return lane < HEAD_DIM, lane >= HEAD_DIM


def _bd(x):
    m0, m1 = _pair_masks(x.shape[0])
    return jnp.concatenate([jnp.where(m0, x, 0.0), jnp.where(m1, x, 0.0)], axis=0)


def _unbd(m, c):
    return jnp.where(_pair_masks(c)[0], m[:c], m[c:])


def _pair_a(l2, r2):
    return _mm(l2, _bd(r2), tb=True)


def _pair_mul(p2, x2):
    return _mm(p2, _bd(x2))


def _pair_mul_t(p2, x2):
    return _unbd(_mm(p2, x2, ta=True), p2.shape[0])


def _block_diag_mask():
    row = lax.broadcasted_iota(jnp.int32, (PAIR, PAIR), 0)
    lane = lax.broadcasted_iota(jnp.int32, (PAIR, PAIR), 1)
    return (row < HEAD_DIM) == (lane < HEAD_DIM), row == lane


def _wkv_pair_common(r, lw, k, a, b):
    c = r[0].shape[0]
    pairs = range(len(r))
    i = lax.broadcasted_iota(jnp.int32, (c, PAIR), 0)
    j = lax.broadcasted_iota(jnp.int32, (c, PAIR), 1) % c
    strict, incl = i > j, i >= j
    ti = lax.broadcasted_iota(jnp.int32, (c, c), 0)
    tj = lax.broadcasted_iota(jnp.int32, (c, c), 1)
    tri = jnp.where(ti >= tj, 1.0, 0.0).astype(BF16)
    lc = [sum(_dg(tri, part, False, False) for part in _split(lw[p], 3)) for p in pairs]
    lend = [lc[p][c - 1:c, :] for p in pairs]
    rt = [r[p] * jnp.exp(lc[p]) for p in pairs]
    at = [a[p] * jnp.exp(lc[p] - lw[p]) for p in pairs]
    pinv = [jnp.exp(-lc[p]) for p in pairs]
    kt = [k[p] * pinv[p] for p in pairs]
    bt = [b[p] * pinv[p] for p in pairs]
    e = [jnp.exp(lend[p] - lc[p]) for p in pairs]
    ktp = [k[p] * e[p] for p in pairs]
    btp = [b[p] * e[p] for p in pairs]
    a_ab = [jnp.where(strict, _pair_a(at[p], bt[p]), 0.0) for p in pairs]
    a_ak = [jnp.where(strict, _pair_a(at[p], kt[p]), 0.0) for p in pairs]
    a_rb = [jnp.where(incl, _pair_a(rt[p], bt[p]), 0.0) for p in pairs]
    a_rk = [jnp.where(incl, _pair_a(rt[p], kt[p]), 0.0) for p in pairs]
    t = [jnp.where(i == j, 1.0, 0.0) + a_ab[p] for p in pairs]
    xp = a_ab
    n = 2
    while n < c:
        xp = [_pair_mul(xp[p], xp[p]) for p in pairs]
        t = [t[p] + _pair_mul(t[p], xp[p]) for p in pairs]
        n *= 2
    bdm, eye = _block_diag_mask()
    pend_col = [jnp.sum(jnp.where(eye, jnp.exp(lend[p]), 0.0), axis=1, keepdims=True) for p in pairs]
    return dict(rt=rt, at=at, kt=kt, bt=bt, ktp=ktp, btp=btp, a_ak=a_ak, a_rb=a_rb, a_rk=a_rk, t=t,
                pend_col=pend_col, lend=lend, lc=lc, strict=strict, incl=incl, tri=tri, bdm=bdm)


def _wkv_group(width):
    npair = width // PAIR
    g = min(WKV_PAIRS_PER_STEP, npair)
    assert npair % g == 0
    return npair, g


def _wkv_fwd_call(r, lw, k, v, a, b):
    tokens, width = r.shape
    c = WKV_CHUNK
    nc = tokens // c
    npair, g = _wkv_group(width)

    def body(r_ref, lw_ref, k_ref, v_ref, a_ref, b_ref, y_ref, s_ref, st):
        @pl.when(pl.program_id(1) == 0)
        def _():
            st[...] = jnp.zeros_like(st)

        pairs = range(g)
        rv, lwv, kv, vv, av, bv = ([ref[:, p * PAIR:(p + 1) * PAIR] for p in pairs]
                                   for ref in (r_ref, lw_ref, k_ref, v_ref, a_ref, b_ref))
        s0 = [st[p] for p in pairs]
        q = _wkv_pair_common(rv, lwv, kv, av, bv)
        w1 = [_mm(q["at"][p], s0[p]) + _pair_mul(q["a_ak"][p], vv[p]) for p in pairs]
        u = [_pair_mul(q["t"][p], w1[p]) for p in pairs]
        y = [_mm(q["rt"][p], s0[p]) + _pair_mul(q["a_rb"][p], u[p]) + _pair_mul(q["a_rk"][p], vv[p]) for p in pairs]
        grow = [_mm(jnp.concatenate([q["btp"][p], q["ktp"][p]], axis=0), jnp.concatenate([u[p], vv[p]], axis=0), ta=True)
                for p in pairs]
        for p in pairs:
            y_ref[:, p * PAIR:(p + 1) * PAIR] = y[p]
            s_ref[0, p] = s0[p]
            st[p] = q["pend_col"][p] * s0[p] + jnp.where(q["bdm"], grow[p], 0.0)

    tok = pl.BlockSpec((c, g * PAIR), lambda gi, ci: (ci, gi))
    return pl.pallas_call(
        body, name="wkv_fwd", grid=(npair // g, nc),
        in_specs=[tok] * 6,
        out_specs=[tok, pl.BlockSpec((1, g, PAIR, PAIR), lambda gi, ci: (ci, gi, 0, 0))],
        out_shape=[jax.ShapeDtypeStruct((tokens, width), F32), jax.ShapeDtypeStruct((nc, npair, PAIR, PAIR), F32)],
        scratch_shapes=[pltpu.VMEM((g, PAIR, PAIR), F32)],
        compiler_params=pltpu.CompilerParams(dimension_semantics=("parallel", "arbitrary")),
    )(r, lw, k, v, a, b)


def _wkv_bwd_call(r, lw, k, v, a, b, s, dy):
    tokens, width = r.shape
    c = WKV_CHUNK
    nc = tokens // c
    npair, g = _wkv_group(width)

    def body(r_ref, lw_ref, k_ref, v_ref, a_ref, b_ref, s_ref, dy_ref,
             dr_ref, dlw_ref, dk_ref, dv_ref, da_ref, db_ref, dst):
        @pl.when(pl.program_id(1) == 0)
        def _():
            dst[...] = jnp.zeros_like(dst)

        pairs = range(g)
        rv, lwv, kv, vv, av, bv, dyv = ([ref[:, p * PAIR:(p + 1) * PAIR] for p in pairs]
                                        for ref in (r_ref, lw_ref, k_ref, v_ref, a_ref, b_ref, dy_ref))
        s0 = [s_ref[0, p] for p in pairs]
        dsc = [dst[p] for p in pairs]
        q = _wkv_pair_common(rv, lwv, kv, av, bv)
        rt, at, kt, bt, ktp, btp, t = (q[n] for n in ("rt", "at", "kt", "bt", "ktp", "btp", "t"))
        a_ak, a_rb, a_rk, strict, incl = (q[n] for n in ("a_ak", "a_rb", "a_rk", "strict", "incl"))
        w1 = [_mm(at[p], s0[p]) + _pair_mul(a_ak[p], vv[p]) for p in pairs]
        u = [_pair_mul(t[p], w1[p]) for p in pairs]
        du = [_pair_mul_t(a_rb[p], dyv[p]) + _mm(btp[p], dsc[p]) for p in pairs]
        dw1 = [_pair_mul_t(t[p], du[p]) for p in pairs]
        dv = [_pair_mul_t(a_rk[p], dyv[p]) + _mm(ktp[p], dsc[p]) + _pair_mul_t(a_ak[p], dw1[p]) for p in pairs]
        da_ab = [jnp.where(strict, _pair_a(dw1[p], u[p]), 0.0) for p in pairs]
        da_ak = [jnp.where(strict, _pair_a(dw1[p], vv[p]), 0.0) for p in pairs]
        da_rb = [jnp.where(incl, _pair_a(dyv[p], u[p]), 0.0) for p in pairs]
        da_rk = [jnp.where(incl, _pair_a(dyv[p], vv[p]), 0.0) for p in pairs]
        d_rt = [_mm(dyv[p], s0[p], tb=True) + _pair_mul(da_rb[p], bt[p]) + _pair_mul(da_rk[p], kt[p]) for p in pairs]
        d_at = [_mm(dw1[p], s0[p], tb=True) + _pair_mul(da_ab[p], bt[p]) + _pair_mul(da_ak[p], kt[p]) for p in pairs]
        d_bt = [_pair_mul_t(da_ab[p], at[p]) + _pair_mul_t(da_rb[p], rt[p]) for p in pairs]
        d_kt = [_pair_mul_t(da_ak[p], at[p]) + _pair_mul_t(da_rk[p], rt[p]) for p in pairs]
        d_btp = [_mm(u[p], dsc[p], tb=True) for p in pairs]
        d_ktp = [_mm(vv[p], dsc[p], tb=True) for p in pairs]
        ones = jnp.ones((8, PAIR), BF16)
        dpend = [sum(_dg(ones, part, False, True) for part in _split(dsc[p] * s0[p], 3))[0:1, :] * jnp.exp(q["lend"][p])
                 for p in pairs]
        grow = [_mm(jnp.concatenate([rt[p], at[p]], axis=0), jnp.concatenate([dyv[p], dw1[p]], axis=0), ta=True)
                for p in pairs]
        last = lax.broadcasted_iota(jnp.int32, (c, PAIR), 0) == c - 1
        for p in pairs:
            sl = slice(p * PAIR, (p + 1) * PAIR)
            dst[p] = q["pend_col"][p] * dsc[p] + jnp.where(q["bdm"], grow[p], 0.0)
            lc_e = d_ktp[p] * ktp[p] + d_btp[p] * btp[p]
            dlend = jnp.sum(lc_e, axis=0, keepdims=True) + dpend[p]
            dlc = d_rt[p] * rt[p] - d_kt[p] * kt[p] - d_bt[p] * bt[p] - lc_e + jnp.where(last, dlend, 0.0)
            dlp = d_at[p] * at[p]
            dlw_ref[:, sl] = sum(_dg(q["tri"], part, True, False) for part in _split(dlc + dlp, 3)) - dlp
            lc = q["lc"][p]
            pinv = jnp.exp(-lc)
            e = jnp.exp(q["lend"][p] - lc)
            dr_ref[:, sl] = d_rt[p] * jnp.exp(lc)
            da_ref[:, sl] = d_at[p] * jnp.exp(lc - lwv[p])
            dk_ref[:, sl] = d_kt[p] * pinv + d_ktp[p] * e
            db_ref[:, sl] = d_bt[p] * pinv + d_btp[p] * e
            dv_ref[:, sl] = dv[p]

    tok = pl.BlockSpec((c, g * PAIR), lambda gi, ci: (nc - 1 - ci, gi))
    tshape = jax.ShapeDtypeStruct((tokens, width), F32)
    return pl.pallas_call(
        body, name="wkv_bwd", grid=(npair // g, nc),
        in_specs=[tok] * 6 + [pl.BlockSpec((1, g, PAIR, PAIR), lambda gi, ci: (nc - 1 - ci, gi, 0, 0)), tok],
        out_specs=[tok] * 6, out_shape=[tshape] * 6,
        scratch_shapes=[pltpu.VMEM((g, PAIR, PAIR), F32)],
        compiler_params=pltpu.CompilerParams(dimension_semantics=("parallel", "arbitrary")),
    )(r, lw, k, v, a, b, s, dy)


@jax.custom_vjp
def wkv7(r, lw, k, v, a, b):
    return _wkv_fwd_call(r, lw, k, v, a, b)[0]


def _wkv7_fwd(r, lw, k, v, a, b):
    y, s = _wkv_fwd_call(r, lw, k, v, a, b)
    return y, (r, lw, k, v, a, b, s)


wkv7.defvjp(_wkv7_fwd, lambda res, dy: tuple(_wkv_bwd_call(*res, dy)))


def _attn_block(tokens):
    return ATTN_BLOCK_BIG if tokens % ATTN_BLOCK_BIG == 0 else ATTN_BLOCK


def _fox_layouts(cum):
    tokens, heads = cum.shape
    t = _attn_block(tokens)
    cq = cum.reshape(tokens, heads // 2, 2).transpose(1, 0, 2)
    ck = cum.T.reshape(heads // 2, 2, tokens // t, t).transpose(0, 2, 1, 3)
    return cq, ck


def _head_lane_masks(rows):
    lane = lax.broadcasted_iota(jnp.int32, (rows, 2 * HEAD_DIM), 1)
    return [lane < HEAD_DIM, lane >= HEAD_DIM]


def _fox_fwd_call(q, k, v, cq, ck):
    tokens, width = q.shape
    t = _attn_block(tokens)
    nb = tokens // t
    hd = HEAD_DIM
    npair = width // (2 * hd)

    def body(q_ref, k_ref, v_ref, cq_ref, ck_ref, o_ref, lse_ref):
        i = pl.program_id(1)
        masks = _head_lane_masks(t)
        q2 = q_ref[...]
        qs = [jnp.where(mk, q2, 0.0).astype(BF16) for mk in masks]
        cqs = [cq_ref[0, :, hh:hh + 1] for hh in range(2)]

        def block(j, carry, diagonal):
            off = pl.multiple_of(j * t, t)
            ckj = ck_ref[0, j]
            k2 = k_ref[pl.ds(off, t), :].astype(BF16)
            v2 = v_ref[pl.ds(off, t), :].astype(BF16)
            out = []
            for hh in range(2):
                m, l, acc = carry[hh]
                s = _dg(qs[hh], k2, False, True) + (cqs[hh] - ckj[hh:hh + 1, :])
                if diagonal:
                    keep = lax.broadcasted_iota(jnp.int32, (t, t), 0) >= lax.broadcasted_iota(jnp.int32, (t, t), 1)
                    s = jnp.where(keep, s, NEG_BIG)
                m_new = jnp.maximum(m, jnp.max(s, axis=1, keepdims=True))
                alpha = jnp.exp(m - m_new)
                p = jnp.exp(s - m_new)
                l = alpha * l + jnp.sum(p, axis=1, keepdims=True)
                acc = alpha * acc + _dg(p.astype(BF16), v2, False, False)
                out.append((m_new, l, acc))
            return tuple(out)

        init = tuple((jnp.full((t, 1), NEG_BIG, F32), jnp.zeros((t, 1), F32), jnp.zeros((t, 2 * hd), F32)) for _ in range(2))
        res = lax.fori_loop(0, i, lambda j, c: block(j, c, False), init)
        res = block(i, res, True)
        o_ref[...] = jnp.where(masks[0], res[0][2] / res[0][1], res[1][2] / res[1][1])
        for hh in range(2):
            lse_ref[0, :, hh:hh + 1] = res[hh][0] + jnp.log(res[hh][1])

    blk = pl.BlockSpec((t, 2 * hd), lambda hp, i: (i, hp))
    full = pl.BlockSpec((tokens, 2 * hd), lambda hp, i: (0, hp))
    cq_spec = pl.BlockSpec((1, t, 2), lambda hp, i: (hp, i, 0))
    ck_spec = pl.BlockSpec((1, nb, 2, t), lambda hp, i: (hp, 0, 0, 0))
    return pl.pallas_call(
        body, name="fox_fwd", grid=(npair, nb),
        in_specs=[blk, full, full, cq_spec, ck_spec],
        out_specs=[blk, cq_spec],
        out_shape=[jax.ShapeDtypeStruct((tokens, width), F32), jax.ShapeDtypeStruct((npair, tokens, 2), F32)],
        compiler_params=pltpu.CompilerParams(dimension_semantics=("parallel", "arbitrary")),
    )(q, k, v, cq, ck)


def _fox_bwd_call(q, k, v, cq, ck, o, lse, do):
    tokens, width = q.shape
    t = _attn_block(tokens)
    nb = tokens // t
    hd = HEAD_DIM
    npair = width // (2 * hd)

    def body(q_ref, k_ref, v_ref, cq_ref, ck_ref, o_ref, lse_ref, do_ref, dq_ref, dk_ref, dv_ref, dck_ref, dcq_ref):
        i = pl.program_id(1)

        @pl.when(i == 0)
        def _():
            dk_ref[...] = jnp.zeros_like(dk_ref)
            dv_ref[...] = jnp.zeros_like(dv_ref)
            dck_ref[...] = jnp.zeros_like(dck_ref)

        masks = _head_lane_masks(t)
        q2, do2, o2 = q_ref[...], do_ref[...], o_ref[...]
        qs = [jnp.where(mk, q2, 0.0).astype(BF16) for mk in masks]
        dos = [jnp.where(mk, do2, 0.0).astype(BF16) for mk in masks]
        deltas = [jnp.sum(dos[hh].astype(F32) * o2, axis=1, keepdims=True) for hh in range(2)]
        bias = [cq_ref[0, :, hh:hh + 1] - lse_ref[0, :, hh:hh + 1] for hh in range(2)]

        def block(j, carry, diagonal):
            off = pl.multiple_of(j * t, t)
            ckj = ck_ref[0, j]
            k2 = k_ref[pl.ds(off, t), :].astype(BF16)
            v2 = v_ref[pl.ds(off, t), :].astype(BF16)
            out = []
            dk2 = jnp.zeros((t, 2 * hd), F32)
            dv2 = jnp.zeros((t, 2 * hd), F32)
            for hh in range(2):
                s = _dg(qs[hh], k2, False, True) + (bias[hh] - ckj[hh:hh + 1, :])
                if diagonal:
                    keep = lax.broadcasted_iota(jnp.int32, (t, t), 0) >= lax.broadcasted_iota(jnp.int32, (t, t), 1)
                    s = jnp.where(keep, s, NEG_BIG)
                p = jnp.exp(s)
                dp = _dg(dos[hh], v2, False, True)
                ds = p * (dp - deltas[hh])
                dsb = ds.astype(BF16)
                dq, rowsum = carry[hh]
                out.append((dq + _dg(dsb, k2, False, False), rowsum + jnp.sum(ds, axis=1, keepdims=True)))
                dk2 = dk2 + _dg(dsb, qs[hh], True, False)
                dv2 = dv2 + _dg(p.astype(BF16), dos[hh], True, False)
                dck_ref[0, j, hh:hh + 1, :] -= jnp.sum(ds, axis=0, keepdims=True)
            dk_ref[pl.ds(off, t), :] += dk2
            dv_ref[pl.ds(off, t), :] += dv2
            return tuple(out)

        init = tuple((jnp.zeros((t, 2 * hd), F32), jnp.zeros((t, 1), F32)) for _ in range(2))
        res = lax.fori_loop(0, i, lambda j, c: block(j, c, False), init)
        res = block(i, res, True)
        dq_ref[...] = jnp.where(masks[0], res[0][0], res[1][0])
        for hh in range(2):
            dcq_ref[0, :, hh:hh + 1] = res[hh][1]

    blk = pl.BlockSpec((t, 2 * hd), lambda hp, i: (i, hp))
    full = pl.BlockSpec((tokens, 2 * hd), lambda hp, i: (0, hp))
    cq_spec = pl.BlockSpec((1, t, 2), lambda hp, i: (hp, i, 0))
    ck_spec = pl.BlockSpec((1, nb, 2, t), lambda hp, i: (hp, 0, 0, 0))
    tshape = jax.ShapeDtypeStruct((tokens, width), F32)
    return pl.pallas_call(
        body, name="fox_bwd", grid=(npair, nb),
        in_specs=[blk, full, full, cq_spec, ck_spec, blk, cq_spec, blk],
        out_specs=[blk, full, full, ck_spec, cq_spec],
        out_shape=[tshape, tshape, tshape, jax.ShapeDtypeStruct((npair, nb, 2, t), F32),
                   jax.ShapeDtypeStruct((npair, tokens, 2), F32)],
        compiler_params=pltpu.CompilerParams(dimension_semantics=("parallel", "arbitrary")),
    )(q, k, v, cq, ck, o, lse, do)


@jax.custom_vjp
def fox_attention(q, k, v, cum):
    return _fox_fwd_call(q, k, v, *_fox_layouts(cum))[0]


def _fox_fwd(q, k, v, cum):
    cq, ck = _fox_layouts(cum)
    o, lse = _fox_fwd_call(q, k, v, cq, ck)
    return o, (q, k, v, cq, ck, o, lse)


def _fox_bwd(res, do):
    q, k, v, cq, ck, o, lse = res
    dq, dk, dv, dck, dcq = _fox_bwd_call(q, k, v, cq, ck, o, lse, do)
    npair, nb, _, t = dck.shape
    dcum = dck.transpose(0, 2, 1, 3).reshape(2 * npair, nb * t).T + dcq.transpose(1, 0, 2).reshape(nb * t, 2 * npair)
    return dq, dk, dv, dcum


fox_attention.defvjp(_fox_fwd, _fox_bwd)


def _loss_call(y, target):
    rows, d = y.shape
    tr = _row_tile(rows, d)

    def body(y_ref, t_ref, loss_ref, dy_ref):
        @pl.when(pl.program_id(0) == 0)
        def _():
            loss_ref[...] = jnp.zeros_like(loss_ref)

        diff = y_ref[...] - t_ref[...]
        dy_ref[...] = diff * (1.0 / d)
        loss_ref[...] += (0.5 / d) * jnp.sum(jnp.sum(diff * diff, axis=1, keepdims=True), axis=0, keepdims=True)

    return pl.pallas_call(
        body, name="loss", grid=(rows // tr,),
        in_specs=[pl.BlockSpec((tr, d), lambda i: (i, 0))] * 2,
        out_specs=[pl.BlockSpec((1, 1), lambda i: (0, 0)), pl.BlockSpec((tr, d), lambda i: (i, 0))],
        out_shape=[jax.ShapeDtypeStruct((1, 1), F32), jax.ShapeDtypeStruct((rows, d), F32)],
        compiler_params=pltpu.CompilerParams(dimension_semantics=("arbitrary",)),
    )(y, target)


def _adamw_call(w, g, m, v):
    rows, cols = w.shape
    tr = _row_tile(rows, cols, budget=1024 * 1024)
    c1 = 1.0 / (1.0 - ADAM_B1 ** ADAM_STEP)
    c2 = 1.0 / (1.0 - ADAM_B2 ** ADAM_STEP)

    def body(w_ref, g_ref, m_ref, v_ref, d_ref, nm_ref, nv_ref):
        gv = g_ref[...]
        nm = ADAM_B1 * m_ref[...] + (1.0 - ADAM_B1) * gv
        nv = ADAM_B2 * v_ref[...] + (1.0 - ADAM_B2) * (gv * gv)
        nm_ref[...] = nm
        nv_ref[...] = nv
        d_ref[...] = -ADAM_LR * ((nm * c1) / (jnp.sqrt(nv * c2) + ADAM_EPS) + ADAM_WD * w_ref[...])

    spec = pl.BlockSpec((tr, cols), lambda i: (i, 0))
    shape = jax.ShapeDtypeStruct((rows, cols), F32)
    return pl.pallas_call(
        body, name="adamw", grid=(rows // tr,),
        in_specs=[spec] * 4, out_specs=[spec] * 3, out_shape=[shape] * 3,
        compiler_params=pltpu.CompilerParams(dimension_semantics=("parallel",)),
    )(w, g, m, v)


def _my_place():
    return lax.axis_index("x"), lax.axis_index("y"), lax.axis_index("c")


def _place_index(px, py, pc):
    return 4 * px + 2 * py + pc


HBM_SPEC = pl.BlockSpec(memory_space=pltpu.HBM)


def _all_gather_call(block):
    def body(x_ref, out_ref, send_sems, recv_sems, local_sem):
        x, y, c = _my_place()
        me, sibling = (x, y, c), (x, y, 1 - c)
        chips = [(1 - x, y), (x, 1 - y), (1 - x, 1 - y)]

        def slot(px, py, pc):
            return out_ref.at[_place_index(px, py, pc)]

        def copy(k, blk, to, src=None):
            return pltpu.make_async_remote_copy(
                src_ref=slot(*blk) if src is None else src, dst_ref=slot(*blk),
                send_sem=send_sems.at[k], recv_sem=recv_sems.at[k],
                device_id=to, device_id_type=pl.DeviceIdType.MESH)

        mine = pltpu.make_async_copy(x_ref, slot(*me), local_sem)
        mine.start()
        first = [copy(0, me, sibling, src=x_ref)]
        first += [copy(1 + j, me, (*chip, c), src=x_ref) for j, chip in enumerate(chips)]
        for cp in first:
            cp.start()
        passed = [copy(4 + j, (*chip, c), sibling) for j, chip in enumerate(chips)]
        for j, chip in enumerate(chips):
            copy(1 + j, (*chip, c), me).wait_recv()
            passed[j].start()
        copy(0, sibling, me).wait_recv()
        for j, chip in enumerate(chips):
            copy(4 + j, (*chip, 1 - c), me).wait_recv()
        for cp in first + passed:
            cp.wait_send()
        mine.wait()

    return pl.pallas_call(
        body, name="all_gather",
        out_shape=jax.ShapeDtypeStruct((N_DEV,) + block.shape, block.dtype),
        in_specs=[HBM_SPEC], out_specs=HBM_SPEC,
        scratch_shapes=[pltpu.SemaphoreType.DMA((7,)), pltpu.SemaphoreType.DMA((7,)), pltpu.SemaphoreType.DMA],
    )(block)


SEM_SPEC = pl.BlockSpec(memory_space=pltpu.SEMAPHORE)
SIDE_EFFECT = pltpu.SideEffectType.DATAFLOW_SIDE_EFFECTING


def _peers():
    x, y, c = _my_place()
    out = []
    for k in range(1, N_DEV):
        peer = (x ^ (k >> 2), y ^ ((k >> 1) & 1), c ^ (k & 1))
        out.append((k - 1, peer, _place_index(*peer)))
    return _place_index(x, y, c), out


def _spread_start(src, per_peer, name, after=None):
    slot = src.shape[1:] if per_peer else src.shape
    order = () if after is None else (after,)

    def body(src_ref, land_ref, *rest):
        send_sems, recv_sems, src_thru, land_thru, token = rest[len(order):]
        mine, peers = _peers()
        for k, peer, peer_idx in peers:
            pltpu.make_async_remote_copy(
                src_ref=src_ref.at[peer_idx] if per_peer else src_ref, dst_ref=land_ref.at[mine],
                send_sem=send_sems.at[k], recv_sem=recv_sems.at[k],
                device_id=peer, device_id_type=pl.DeviceIdType.MESH).start()
        token[...] = jnp.zeros_like(token)

    return pl.pallas_call(
        body, name=name,
        out_shape=(pltpu.SemaphoreType.DMA((N_DEV - 1,)), pltpu.SemaphoreType.DMA((N_DEV - 1,)),
                   pltpu.HBM(src.shape, src.dtype), pltpu.HBM((N_DEV,) + slot, src.dtype),
                   jax.ShapeDtypeStruct((8, 128), F32)),
        in_specs=(HBM_SPEC, HBM_SPEC) + (pl.BlockSpec(memory_space=pl.ANY),) * len(order),
        out_specs=(SEM_SPEC, SEM_SPEC, HBM_SPEC, HBM_SPEC, pl.BlockSpec(memory_space=pltpu.VMEM)),
        input_output_aliases={0: 2, 1: 3},
        compiler_params=pltpu.CompilerParams(has_side_effects=SIDE_EFFECT),
    )(pltpu.with_memory_space_constraint(src, pltpu.HBM),
      pltpu.with_memory_space_constraint(lax.empty((N_DEV,) + slot, src.dtype), pltpu.HBM), *order)


def _spread_wait(handles, after, per_peer, name):
    send_sems, recv_sems, src_thru, land_thru = handles

    def body(src_ref, land_ref, send_sems, recv_sems, after_ref, src_dead, got_ref):
        _, peers = _peers()
        for k, peer, peer_idx in peers:
            copy = pltpu.make_async_remote_copy(
                src_ref=src_ref.at[peer_idx] if per_peer else src_ref, dst_ref=land_ref.at[peer_idx],
                send_sem=send_sems.at[k], recv_sem=recv_sems.at[k],
                device_id=peer, device_id_type=pl.DeviceIdType.MESH)
            copy.wait_send()
            copy.wait_recv()

    return pl.pallas_call(
        body, name=name,
        out_shape=(pltpu.HBM(src_thru.shape, src_thru.dtype), pltpu.HBM(land_thru.shape, land_thru.dtype)),
        in_specs=(HBM_SPEC, HBM_SPEC, SEM_SPEC, SEM_SPEC, pl.BlockSpec(memory_space=pl.ANY)),
        out_specs=(HBM_SPEC, HBM_SPEC), input_output_aliases={0: 0, 1: 1},
        compiler_params=pltpu.CompilerParams(has_side_effects=SIDE_EFFECT),
    )(src_thru, land_thru, send_sems, recv_sems, after)


def _sum_slots_call(slots):
    _, rows, cols = slots.shape
    tr = _row_tile(rows, cols, budget=512 * 1024)

    def body(s_ref, o_ref):
        acc = s_ref[0].astype(F32)
        for j in range(1, N_DEV):
            acc = acc + s_ref[j].astype(F32)
        o_ref[...] = acc

    return pl.pallas_call(
        body, name="sum_slots", grid=(rows // tr,),
        in_specs=[pl.BlockSpec((N_DEV, tr, cols), lambda i: (0, i, 0))],
        out_specs=pl.BlockSpec((tr, cols), lambda i: (i, 0)),
        out_shape=jax.ShapeDtypeStruct((rows, cols), F32),
        compiler_params=pltpu.CompilerParams(dimension_semantics=("parallel",)),
    )(slots)


def _with_own_slot(got, own, mine):
    return lax.dynamic_update_index_in_dim(got, own, mine, 0)


def _pack(vectors, width):
    flat = jnp.concatenate([v.reshape(-1) for v in vectors])
    return jnp.pad(flat, (0, width - flat.shape[0])).reshape(width // 128, 128)


def _unpack(packed, like):
    flat = packed.reshape(-1)
    out, at = [], 0
    for v in like:
        out.append(flat[at:at + v.size].reshape(v.shape))
        at += v.size
    return tuple(out)


def _sum_over_devices(grads):
    n = sum(v.size for v in grads)
    width = -(-n // 1024) * 1024
    return _unpack(_sum_slots_call(_all_gather_call(_pack(grads, width))), grads)


def _cols_from_slots(slots):
    n, rows, cols = slots.shape
    return slots.transpose(1, 0, 2).reshape(rows, n * cols)


def _rows_from_slots(slots):
    return slots.reshape(-1, slots.shape[2])


def _pad128(n):
    return -(-n // 128) * 128


def _in_proj_layout(slots, rcols, fcols):
    w_in = _cols_from_slots(slots)
    padc = lambda w, n: jnp.pad(w, ((0, 0), (0, n - w.shape[1])))
    return jnp.concatenate([padc(w_in[:, :rcols], _pad128(rcols)), padc(w_in[:, rcols:rcols + fcols], _pad128(fcols)),
                            w_in[:, rcols + fcols:]], axis=1)


def _stage_embed(meta, x, n1, lp):
    h0 = jnp.concatenate([meta, x, jnp.zeros((lp - meta.shape[0] - x.shape[0], x.shape[1]), F32)], axis=0)
    return h0, rmsnorm(h0, n1)


def _stage_mix(proj, small, w2, a2, g2, rw, fw):
    (mu, w0, a0, k_k, k_a, r_k, gn_w, gn_b, q_g, k_g, f_bias) = small
    dl, al, gl = w2.shape[1], a2.shape[1], g2.shape[1]
    rcols = 3 * rw + dl + al + gl
    fcols = 3 * fw + fw // HEAD_DIM
    rpad, fpad = _pad128(rcols), _pad128(fcols)
    z_r, z_f, z_g = proj[:, :rcols], proj[:, rpad:rpad + fcols], proj[:, rpad + fpad:]

    z_prev = jnp.pad(z_r, ((1, 0), (0, 0)))[:-1]
    z = z_r + bmul(z_prev - z_r, mu)
    r, k, v = z[:, :rw], z[:, rw:2 * rw], z[:, 2 * rw:3 * rw]
    wd, ad, gd = z[:, 3 * rw:3 * rw + dl], z[:, 3 * rw + dl:3 * rw + dl + al], z[:, 3 * rw + dl + al:]
    w_log = -jax.nn.softplus(-badd(dense_cols(jnp.tanh(wd), w2), w0)) - 0.5
    lw = -jnp.exp(w_log)
    a_sig = jax.nn.sigmoid(badd(dense_cols(ad, a2), a0))
    g = dense_cols(jax.nn.sigmoid(gd), g2)
    kk = head_l2norm(bmul(k, k_k))
    kf = k * (1.0 + bmul(a_sig - 1.0, k_a))
    y = wkv7(r, lw, kf, v, -kk, kk * a_sig)
    y_a = gn_bonus(y, r, kf, v, gn_w, gn_b, r_k.reshape(1, rw)) * g

    fq, fk, fv, fl = z_f[:, :fw], z_f[:, fw:2 * fw], z_f[:, 2 * fw:3 * fw], z_f[:, 3 * fw:]
    fq = head_rms(fq, q_g) * (HEAD_DIM ** -0.5)
    fk = head_rms(fk, k_g)
    cum = jnp.cumsum(jax.nn.log_sigmoid(badd(fl, f_bias)), axis=0)
    y_b = fox_attention(fq, fk, fv, cum)
    return y_a, y_b, jax.nn.sigmoid(z_g)


def _stage_merge(h0, y_a, y_b, gates, w_a, w_b, w_o):
    d = h0.shape[1]
    merged = gates[:, :d] * dense_cols(y_a, w_a) + gates[:, d:] * dense_cols(y_b, w_b)
    return h0 + dense(merged, w_o)


def _stage_ffn(h1, n2, w_gu, w_dn):
    gu = dense_cols(rmsnorm(h1, n2), w_gu)
    dff = w_dn.shape[0]
    return h1 + dense(jax.nn.silu(gu[:, :dff]) * gu[:, dff:], w_dn)


SHARDED = ("meta_tokens", "w_in", "rwkv_w2", "rwkv_a2", "rwkv_g2", "w_branch_a", "w_branch_b", "w_o", "w_gate_up", "w_down")
SMALL = ("norm1_g", "rwkv_mu", "rwkv_w0", "rwkv_a0", "rwkv_k_k", "rwkv_k_a", "rwkv_r_k", "rwkv_gn_w", "rwkv_gn_b",
         "fox_q_norm_g", "fox_k_norm_g", "fox_f_bias", "norm2_g")
WEIGHTS = ("meta_tokens", "norm1_g", "w_in", "rwkv_mu", "rwkv_w0", "rwkv_w2", "rwkv_a0", "rwkv_a2", "rwkv_g2", "rwkv_k_k",
           "rwkv_k_a", "rwkv_r_k", "rwkv_gn_w", "rwkv_gn_b", "fox_q_norm_g", "fox_k_norm_g", "fox_f_bias", "w_branch_a",
           "w_branch_b", "w_o", "norm2_g", "w_gate_up", "w_down")


def _as2d(a):
    return a.reshape(-1, a.shape[-1])


def kernel(x, meta_tokens, norm1_g, w_in, rwkv_mu, rwkv_w0, rwkv_w2, rwkv_a0, rwkv_a2, rwkv_g2, rwkv_k_k, rwkv_k_a, rwkv_r_k, rwkv_gn_w, rwkv_gn_b, fox_q_norm_g, fox_k_norm_g, fox_f_bias, w_branch_a, w_branch_b, w_o, norm2_g, w_gate_up, w_down, loss_target, m_meta_tokens, m_norm1_g, m_w_in, m_rwkv_mu, m_rwkv_w0, m_rwkv_w2, m_rwkv_a0, m_rwkv_a2, m_rwkv_g2, m_rwkv_k_k, m_rwkv_k_a, m_rwkv_r_k, m_rwkv_gn_w, m_rwkv_gn_b, m_fox_q_norm_g, m_fox_k_norm_g, m_fox_f_bias, m_w_branch_a, m_w_branch_b, m_w_o, m_norm2_g, m_w_gate_up, m_w_down, v_meta_tokens, v_norm1_g, v_w_in, v_rwkv_mu, v_rwkv_w0, v_rwkv_w2, v_rwkv_a0, v_rwkv_a2, v_rwkv_g2, v_rwkv_k_k, v_rwkv_k_a, v_rwkv_r_k, v_rwkv_gn_w, v_rwkv_gn_b, v_fox_q_norm_g, v_fox_k_norm_g, v_fox_f_bias, v_w_branch_a, v_w_branch_b, v_w_o, v_norm2_g, v_w_gate_up, v_w_down):
    given = dict(locals())
    w = {n: given[n] for n in WEIGHTS}
    assert rwkv_r_k.shape[-1] == HEAD_DIM
    n_meta, seq = meta_tokens.shape[0], x.shape[1]
    tokens = n_meta + seq
    lp = -(-tokens // TOKEN_TILE) * TOKEN_TILE
    mine = _place_index(*(lax.axis_index(a) for a in MESH_AXES))
    x2 = x[0]

    blocks = {n: _as2d(w[n]).astype(F32 if n == "meta_tokens" else BF16) for n in SHARDED}
    first = ("meta_tokens", "w_in", "rwkv_w2", "rwkv_a2", "rwkv_g2")
    started = {n: _spread_start(blocks[n], False, "gather_start_" + n) for n in first}
    zero = sum(started[n][4][0, 0] for n in first)

    def gathered(n, after):
        own, got = _spread_wait(started[n][:4], after, False, "gather_wait_" + n)
        return _with_own_slot(got, own, mine)

    sm = {n: _as2d(w[n]) for n in SMALL}
    small_mix = tuple(sm[n] for n in SMALL[1:-1])
    n1 = sm["norm1_g"] + zero
    rw, fw = w_branch_a.shape[-2], w_branch_b.shape[-2]
    rcols = 3 * rw + rwkv_w2.shape[-2] + rwkv_a2.shape[-2] + rwkv_g2.shape[-2]
    fcols = 3 * fw + fw // HEAD_DIM
    same = lambda s: (s,)

    meta, un_meta = jax.vjp(_cols_from_slots, gathered("meta_tokens", x2))
    (h0, xn), vjp_embed = jax.vjp(lambda m, xs, g: _stage_embed(m, xs, g, lp), meta, x2, n1)
    in_slots = gathered("w_in", xn)
    later = [n for n in SHARDED if n not in first]
    started.update({n: _spread_start(blocks[n], False, "gather_start_" + n, after=in_slots) for n in later})
    w_cat, un_in = jax.vjp(lambda s: _in_proj_layout(s, rcols, fcols), in_slots)
    xn_b = xn.astype(BF16)
    proj = _matmul(xn_b, w_cat, name="in_proj", after=sum(started[n][4] for n in later))
    w2, a2, g2 = (gathered(n, xn) for n in ("rwkv_w2", "rwkv_a2", "rwkv_g2"))
    (y_a, y_b, gates), vjp_mix = jax.vjp(lambda p, s, a, b, c: _stage_mix(p, s, a, b, c, rw, fw), proj, small_mix, w2, a2, g2)
    w_a, w_b = gathered("w_branch_a", y_a), gathered("w_branch_b", y_a)
    w_o_full, un_wo = jax.vjp(_rows_from_slots, gathered("w_o", y_a))
    h1, vjp_merge = jax.vjp(_stage_merge, h0, y_a, y_b, gates, w_a, w_b, w_o_full)
    w_gu = gathered("w_gate_up", h1)
    w_dn, un_dn = jax.vjp(_rows_from_slots, gathered("w_down", h1))
    y, vjp_ffn = jax.vjp(_stage_ffn, h1, sm["norm2_g"], w_gu, w_dn)

    loss_part, dy_real = _loss_call(y[n_meta:tokens], loss_target[0])
    dy = jnp.pad(dy_real, ((n_meta, lp - tokens), (0, 0)))
    loss = lax.psum(loss_part[0, 0], MESH_AXES)

    sent = {}

    def send_grad(n, dmat, unlayout):
        sent[n] = _spread_start(unlayout(dmat)[0], True, "grad_start_" + n)
        return sent[n][4][0, 0]

    d_h1, d_n2, d_wgu, d_wdn = vjp_ffn(dy)
    behind = send_grad("w_gate_up", d_wgu, same) + send_grad("w_down", d_wdn, un_dn)
    d_h0, d_ya, d_yb, d_gates, d_wa, d_wb, d_wo = vjp_merge(d_h1 + behind)
    behind = send_grad("w_o", d_wo, un_wo) + send_grad("w_branch_a", d_wa, same) + send_grad("w_branch_b", d_wb, same)
    d_proj, d_small_mix, d_w2, d_a2, d_g2 = vjp_mix((d_ya + behind, d_yb, d_gates))
    dproj_b = d_proj.astype(BF16)
    d_wcat = _matmul(xn_b, dproj_b, ta=True, out_dtype=BF16, name="in_proj_dw")
    send_grad("w_in", d_wcat, un_in)
    d_xn = _matmul(dproj_b, w_cat, tb=True, out_dtype=F32, name="in_proj_dx", after=sent["w_in"][4])
    send_grad("rwkv_w2", d_w2, same)
    send_grad("rwkv_a2", d_a2, same)
    send_grad("rwkv_g2", d_g2, same)
    d_meta, g_x, d_n1 = vjp_embed((d_h0, d_xn))
    send_grad("meta_tokens", d_meta, un_meta)

    grads = dict(zip(SMALL, _sum_over_devices((d_n1, *d_small_mix, d_n2))))
    grads = {n: g.reshape(w[n].shape) for n, g in grads.items()}

    delta, new_m, new_v = {}, {}, {}
    after = g_x
    for n in ("w_gate_up", "w_down", "w_o", "w_branch_a", "w_branch_b", "rwkv_g2", "rwkv_a2", "rwkv_w2", "meta_tokens", "w_in"):
        src, got = _spread_wait(sent[n][:4], after, True, "grad_wait_" + n)
        g = _sum_slots_call(_with_own_slot(got, lax.dynamic_index_in_dim(src, mine, 0, keepdims=False), mine))
        grads[n] = g.reshape(w[n].shape)
        d_, m_, v_ = _adamw_call(_as2d(w[n]), g, _as2d(given["m_" + n]), _as2d(given["v_" + n]))
        delta[n], new_m[n], new_v[n] = (t.reshape(w[n].shape) for t in (d_, m_, v_))
        after = m_
    n_small = sum(w[n].size for n in SMALL)
    width = -(-n_small // 1024) * 1024
    packs = [_pack([src[n] if p == "" else given[p + n] for n in SMALL], width)
             for p, src in (("", w), ("", grads), ("m_", None), ("v_", None))]
    like = [w[n] for n in SMALL]
    for out, packed in zip((delta, new_m, new_v), _adamw_call(*packs)):
        out.update(dict(zip(SMALL, _unpack(packed, like))))

    return (loss, g_x[None], *[grads[n] for n in WEIGHTS], *[delta[n] for n in WEIGHTS],
            *[new_m[n] for n in WEIGHTS], *[new_v[n] for n in WEIGHTS])
```

```python
import functools

import jax
import jax.numpy as jnp
from jax import lax
from jax.experimental import pallas as pl
from jax.experimental.pallas import tpu as pltpu

F32 = jnp.float32
BF16 = jnp.bfloat16

N_DEV = 8
MESH_AXES = ("x", "y", "c")
HEAD_DIM = 64
TOKEN_TILE = 128
WKV_CHUNK = 64
WKV_PAIRS_PER_STEP = 4
PAIR = 2 * HEAD_DIM
ATTN_BLOCK = 128
ATTN_BLOCK_BIG = 384
RMS_EPS = 1e-6
GN_EPS = 64e-5
L2_FLOOR = 1e-12
NEG_BIG = -1e30
ADAM_LR, ADAM_B1, ADAM_B2, ADAM_EPS, ADAM_WD, ADAM_STEP = 0.001, 0.9, 0.999, 1e-08, 0.01, 10
VMEM_BYTES_V7X = 64 * 1024 * 1024
VMEM_LIMIT_CAP = 56 * 1024 * 1024
VMEM_LIMIT_FLOOR = 32 * 1024 * 1024
MATMUL_VMEM_BUDGET = 36 * 1024 * 1024
GRID_STEP_BYTES = 1024 * 1024


def _vmem_limit(estimate_bytes):
    return int(min(max(estimate_bytes * 5 // 4, VMEM_LIMIT_FLOOR), VMEM_LIMIT_CAP))


def _pick(dim, cands):
    for c in cands:
        if dim % c == 0:
            return c
    return dim


def _row_tile(rows, width, itemsize=4, budget=2 * 1024 * 1024):
    for c in (1408, 1024, 704, 512, 384, 256, 128, 64, 32, 16, 8):
        if rows % c == 0 and c * width * itemsize <= budget:
            return c
    return rows


def _row_tile_ragged(rows, width, itemsize=4, budget=2 * 1024 * 1024):
    tile = _row_tile(rows, width, itemsize, budget)
    if tile * width * itemsize <= budget or rows < 16:
        return tile
    padded = -(-rows // 16) * 16
    for c in (1408, 1024, 704, 512, 384, 336, 256, 192, 128, 96, 64, 48, 32, 16):
        if padded % c == 0 and c * width * itemsize <= budget:
            return c
    return tile


def _dg(a, b, ta, tb):
    dims = (((0 if ta else 1,), (1 if tb else 0,)), ((), ()))
    return lax.dot_general(a, b, dims, preferred_element_type=F32)


def _split(x, n):
    parts = []
    for _ in range(n):
        h = x.astype(BF16)
        parts.append(h)
        x = x - h.astype(F32)
    return parts


def _mm(a, b, ta=False, tb=False):
    return _dg(a.astype(BF16), b.astype(BF16), ta, tb)


def _matmul(a, b, ta=False, tb=False, out_dtype=F32, name="matmul", after=None, b_slots=False, out_slots=0):
    if ta:
        kdim, m = a.shape
    else:
        m, kdim = a.shape
    if b_slots:
        n_slots, brows, bcols = b.shape
        n, k2 = (brows, n_slots * bcols) if tb else (n_slots * bcols, brows)
    elif tb:
        n, k2 = b.shape
    else:
        k2, n = b.shape
    assert kdim == k2, (a.shape, b.shape, ta, tb)
    sa, sb, so = a.dtype.itemsize, b.dtype.itemsize, jnp.dtype(out_dtype).itemsize
    n_unit = bcols if (b_slots and not tb) else (n // out_slots if out_slots else n)
    k_unit = bcols if (b_slots and tb) else kdim
    tm, tn, tk = _matmul_tiles(m, n, kdim, ta, sa, sb, so, n_unit, k_unit)
    nk = kdim // tk

    order = () if after is None else (after,)

    def body(a_ref, b_ref, *rest):
        o_ref, acc = rest[len(order)], rest[len(order) + 1:]
        part = _dg(a_ref[...].astype(BF16), b_ref[...].astype(BF16), ta, tb)
        if nk == 1:
            o_ref[...] = part.astype(o_ref.dtype)
            return
        kk = pl.program_id(2)

        @pl.when(kk == 0)
        def _():
            acc[0][...] = part

        @pl.when(kk > 0)
        def _():
            acc[0][...] += part

        @pl.when(kk == nk - 1)
        def _():
            o_ref[...] = acc[0][...].astype(o_ref.dtype)

    a_spec = pl.BlockSpec((tk, tm), lambda i, j, k: (k, i)) if ta else pl.BlockSpec((tm, tk), lambda i, j, k: (i, k))
    if b_slots and tb:
        per = bcols // tk
        b_spec = pl.BlockSpec((None, tn, tk), lambda i, j, k: (k // per, j, k % per))
    elif b_slots:
        per = bcols // tn
        b_spec = pl.BlockSpec((None, tk, tn), lambda i, j, k: (j // per, k, j % per))
    else:
        b_spec = pl.BlockSpec((tn, tk), lambda i, j, k: (j, k)) if tb else pl.BlockSpec((tk, tn), lambda i, j, k: (k, j))
    if out_slots:
        per_out = n // out_slots // tn
        out_spec = pl.BlockSpec((None, tm, tn), lambda i, j, k: (j // per_out, i, j % per_out))
        out_shape = jax.ShapeDtypeStruct((out_slots, m, n // out_slots), out_dtype)
    else:
        out_spec = pl.BlockSpec((tm, tn), lambda i, j, k: (i, j))
        out_shape = jax.ShapeDtypeStruct((m, n), out_dtype)
    return pl.pallas_call(
        body, name=name,
        grid=(m // tm, n // tn, nk),
        in_specs=[a_spec, b_spec] + [pl.BlockSpec(memory_space=pl.ANY)] * len(order),
        out_specs=out_spec,
        out_shape=out_shape,
        scratch_shapes=[pltpu.VMEM((tm, tn), F32)] if nk > 1 else [],
        compiler_params=pltpu.CompilerParams(dimension_semantics=("parallel", "parallel", "arbitrary"),
                                             vmem_limit_bytes=_vmem_limit(_matmul_vmem(tm, tn, tk, nk, sa, sb, so))),
    )(a, b, *order)


def _matmul_vmem(tm, tn, tk, nk, sa, sb, so):
    return 2 * (tm * tk * sa + tk * tn * sb + tm * tn * so) + tm * tn * 4 + (tm * tn * 4 if nk > 1 else 0)


def _matmul_tiles(m, n, kdim, ta, sa, sb, so, n_unit, k_unit):
    lane = (2048, 1408, 1024, 640, 512, 384, 256, 128)
    sublane = (2048, 1408, 1024, 704, 512, 384, 256, 128)
    divs = lambda dim, cands: [c for c in cands if dim % c == 0] or [dim]
    best = None
    for tm in divs(m, lane if ta else sublane):
        for tn in divs(n_unit, lane):
            for tk in divs(k_unit, sublane if ta else lane) + ([kdim] if (kdim <= 2048 and k_unit == kdim) else []):
                nk, nm, nn = kdim // tk, m // tm, n // tn
                if _matmul_vmem(tm, tn, tk, nk, sa, sb, so) > MATMUL_VMEM_BUDGET:
                    continue
                a_bytes = m * kdim * sa * (nn if nk > 1 else 1)
                b_bytes = kdim * n * sb * (1 if (nk == 1 and nn == 1) else nm)
                cost = a_bytes + b_bytes + m * n * so + nm * nn * nk * GRID_STEP_BYTES
                if best is None or cost < best[0]:
                    best = (cost, tm, tn, tk)
    return best[1:]


@jax.custom_vjp
def dense(x, w):
    return _matmul(x.astype(BF16), w, name="dense_fwd")


def _dense_fwd(x, w):
    assert x.dtype == F32
    xb = x.astype(BF16)
    return _matmul(xb, w, name="dense_fwd"), (xb, w)


def _dense_bwd(res, dy):
    xb, w = res
    dyb = dy.astype(BF16)
    dx = _matmul(dyb, w, tb=True, out_dtype=F32, name="dense_dx")
    dw = _matmul(xb, dyb, ta=True, out_dtype=w.dtype, name="dense_dw")
    return dx, dw


dense.defvjp(_dense_fwd, _dense_bwd)


@jax.custom_vjp
def dense_cols(x, w_slots):
    return _matmul(x.astype(BF16), w_slots, b_slots=True, name="dense_cols_fwd")


def _dense_cols_fwd(x, w_slots):
    assert x.dtype == F32
    xb = x.astype(BF16)
    return _matmul(xb, w_slots, b_slots=True, name="dense_cols_fwd"), (xb, w_slots)


def _dense_cols_bwd(res, dy):
    xb, w_slots = res
    dyb = dy.astype(BF16)
    dx = _matmul(dyb, w_slots, tb=True, b_slots=True, out_dtype=F32, name="dense_cols_dx")
    dw = _matmul(xb, dyb, ta=True, out_slots=w_slots.shape[0], out_dtype=w_slots.dtype, name="dense_cols_dw")
    return dx, dw


dense_cols.defvjp(_dense_cols_fwd, _dense_cols_bwd)


def _rms_fwd_call(x, g):
    rows, d = x.shape
    tr = _row_tile(rows, d)

    def body(x_ref, g_ref, y_ref):
        xv = x_ref[...]
        rstd = lax.rsqrt(jnp.mean(xv * xv, axis=1, keepdims=True) + RMS_EPS)
        y_ref[...] = (xv * rstd) * g_ref[...]

    return pl.pallas_call(
        body, name="rms_fwd", grid=(rows // tr,),
        in_specs=[pl.BlockSpec((tr, d), lambda i: (i, 0)), pl.BlockSpec((1, d), lambda i: (0, 0))],
        out_specs=pl.BlockSpec((tr, d), lambda i: (i, 0)),
        out_shape=jax.ShapeDtypeStruct((rows, d), F32),
        compiler_params=pltpu.CompilerParams(dimension_semantics=("parallel",)),
    )(x, g)


def _rms_bwd_call(x, g, dy):
    rows, d = x.shape
    tr = _row_tile(rows, d)

    def body(x_ref, g_ref, dy_ref, dx_ref, dg_ref):
        @pl.when(pl.program_id(0) == 0)
        def _():
            dg_ref[...] = jnp.zeros_like(dg_ref)

        xv = x_ref[...]
        dyv = dy_ref[...]
        rstd = lax.rsqrt(jnp.mean(xv * xv, axis=1, keepdims=True) + RMS_EPS)
        xhat = xv * rstd
        dxhat = dyv * g_ref[...]
        dx_ref[...] = rstd * (dxhat - xhat * jnp.mean(dxhat * xhat, axis=1, keepdims=True))
        dg_ref[...] += jnp.sum(dyv * xhat, axis=0, keepdims=True)

    return pl.pallas_call(
        body, name="rms_bwd", grid=(rows // tr,),
        in_specs=[pl.BlockSpec((tr, d), lambda i: (i, 0)), pl.BlockSpec((1, d), lambda i: (0, 0)),
                  pl.BlockSpec((tr, d), lambda i: (i, 0))],
        out_specs=[pl.BlockSpec((tr, d), lambda i: (i, 0)), pl.BlockSpec((1, d), lambda i: (0, 0))],
        out_shape=[jax.ShapeDtypeStruct((rows, d), F32), jax.ShapeDtypeStruct((1, d), F32)],
        compiler_params=pltpu.CompilerParams(dimension_semantics=("arbitrary",)),
    )(x, g, dy)


@jax.custom_vjp
def rmsnorm(x, g):
    return _rms_fwd_call(x, g)


rmsnorm.defvjp(lambda x, g: (_rms_fwd_call(x, g), (x, g)), lambda res, dy: tuple(_rms_bwd_call(res[0], res[1], dy)))


def _bcast_call(x, p, mul):
    rows, d = x.shape
    tr = _row_tile(rows, d)

    def body(x_ref, p_ref, y_ref):
        y_ref[...] = x_ref[...] * p_ref[...] if mul else x_ref[...] + p_ref[...]

    return pl.pallas_call(
        body, name="bcast_mul" if mul else "bcast_add", grid=(rows // tr,),
        in_specs=[pl.BlockSpec((tr, d), lambda i: (i, 0)), pl.BlockSpec((1, d), lambda i: (0, 0))],
        out_specs=pl.BlockSpec((tr, d), lambda i: (i, 0)),
        out_shape=jax.ShapeDtypeStruct((rows, d), F32),
        compiler_params=pltpu.CompilerParams(dimension_semantics=("parallel",)),
    )(x, p)


def _colsum_call(a, b=None):
    rows, d = a.shape
    tr = _row_tile(rows, d)
    ops = (a,) if b is None else (a, b)

    def body(*refs):
        o_ref = refs[-1]

        @pl.when(pl.program_id(0) == 0)
        def _():
            o_ref[...] = jnp.zeros_like(o_ref)

        v = refs[0][...] if b is None else refs[0][...] * refs[1][...]
        o_ref[...] += jnp.sum(v, axis=0, keepdims=True)

    return pl.pallas_call(
        body, name="colsum", grid=(rows // tr,),
        in_specs=[pl.BlockSpec((tr, d), lambda i: (i, 0))] * len(ops),
        out_specs=pl.BlockSpec((1, d), lambda i: (0, 0)),
        out_shape=jax.ShapeDtypeStruct((1, d), F32),
        compiler_params=pltpu.CompilerParams(dimension_semantics=("arbitrary",)),
    )(*ops)


@jax.custom_vjp
def bmul(x, p):
    return _bcast_call(x, p, True)


bmul.defvjp(lambda x, p: (_bcast_call(x, p, True), (x, p)),
            lambda res, dy: (_bcast_call(dy, res[1], True), _colsum_call(dy, res[0])))


@jax.custom_vjp
def badd(x, p):
    return _bcast_call(x, p, False)


badd.defvjp(lambda x, p: (_bcast_call(x, p, False), None), lambda res, dy: (dy, _colsum_call(dy)))


def _heads(width):
    return [slice(h * HEAD_DIM, (h + 1) * HEAD_DIM) for h in range(width // HEAD_DIM)]


def _head_rms_fwd_call(x, g):
    rows, w = x.shape
    tr = _row_tile(rows, w, budget=1024 * 1024)

    def body(x_ref, g_ref, y_ref):
        for sl in _heads(w):
            xv = x_ref[:, sl]
            rstd = lax.rsqrt(jnp.mean(xv * xv, axis=1, keepdims=True) + RMS_EPS)
            y_ref[:, sl] = (xv * rstd) * g_ref[...]

    return pl.pallas_call(
        body, name="head_rms_fwd", grid=(rows // tr,),
        in_specs=[pl.BlockSpec((tr, w), lambda i: (i, 0)), pl.BlockSpec((1, HEAD_DIM), lambda i: (0, 0))],
        out_specs=pl.BlockSpec((tr, w), lambda i: (i, 0)),
        out_shape=jax.ShapeDtypeStruct((rows, w), F32),
        compiler_params=pltpu.CompilerParams(dimension_semantics=("parallel",)),
    )(x, g)


def _head_rms_bwd_call(x, g, dy):
    rows, w = x.shape
    tr = _row_tile(rows, w, budget=1024 * 1024)

    def body(x_ref, g_ref, dy_ref, dx_ref, dg_ref):
        @pl.when(pl.program_id(0) == 0)
        def _():
            dg_ref[...] = jnp.zeros_like(dg_ref)

        dg = jnp.zeros((1, HEAD_DIM), F32)
        for sl in _heads(w):
            xv = x_ref[:, sl]
            dyv = dy_ref[:, sl]
            rstd = lax.rsqrt(jnp.mean(xv * xv, axis=1, keepdims=True) + RMS_EPS)
            xhat = xv * rstd
            dxhat = dyv * g_ref[...]
            dx_ref[:, sl] = rstd * (dxhat - xhat * jnp.mean(dxhat * xhat, axis=1, keepdims=True))
            dg = dg + jnp.sum(dyv * xhat, axis=0, keepdims=True)
        dg_ref[...] += dg

    return pl.pallas_call(
        body, name="head_rms_bwd", grid=(rows // tr,),
        in_specs=[pl.BlockSpec((tr, w), lambda i: (i, 0)), pl.BlockSpec((1, HEAD_DIM), lambda i: (0, 0)),
                  pl.BlockSpec((tr, w), lambda i: (i, 0))],
        out_specs=[pl.BlockSpec((tr, w), lambda i: (i, 0)), pl.BlockSpec((1, HEAD_DIM), lambda i: (0, 0))],
        out_shape=[jax.ShapeDtypeStruct((rows, w), F32), jax.ShapeDtypeStruct((1, HEAD_DIM), F32)],
        compiler_params=pltpu.CompilerParams(dimension_semantics=("arbitrary",)),
    )(x, g, dy)


@jax.custom_vjp
def head_rms(x, g):
    return _head_rms_fwd_call(x, g)


head_rms.defvjp(lambda x, g: (_head_rms_fwd_call(x, g), (x, g)),
                lambda res, dy: tuple(_head_rms_bwd_call(res[0], res[1], dy)))


def _head_l2_call(x, dy=None):
    rows, w = x.shape
    tr = _row_tile(rows, w, budget=1024 * 1024)
    ops = (x,) if dy is None else (x, dy)

    def body(*refs):
        o_ref = refs[-1]
        for sl in _heads(w):
            xv = refs[0][:, sl]
            nrm = jnp.sqrt(jnp.sum(xv * xv, axis=1, keepdims=True))
            live = nrm > L2_FLOOR
            inv = 1.0 / jnp.maximum(nrm, L2_FLOOR)
            y = xv * inv
            if dy is None:
                o_ref[:, sl] = y
            else:
                dyv = refs[1][:, sl]
                proj = jnp.where(live, jnp.sum(dyv * y, axis=1, keepdims=True), 0.0)
                o_ref[:, sl] = (dyv - y * proj) * inv

    return pl.pallas_call(
        body, name="head_l2_fwd" if dy is None else "head_l2_bwd", grid=(rows // tr,),
        in_specs=[pl.BlockSpec((tr, w), lambda i: (i, 0))] * len(ops),
        out_specs=pl.BlockSpec((tr, w), lambda i: (i, 0)),
        out_shape=jax.ShapeDtypeStruct((rows, w), F32),
        compiler_params=pltpu.CompilerParams(dimension_semantics=("parallel",)),
    )(*ops)


@jax.custom_vjp
def head_l2norm(x):
    return _head_l2_call(x)


head_l2norm.defvjp(lambda x: (_head_l2_call(x), x), lambda x, dy: (_head_l2_call(x, dy),))


def _gn_fwd_call(y, r, kf, v, gw, gb, rk):
    rows, w = y.shape
    tr = _row_tile(rows, w, budget=512 * 1024)

    def body(y_ref, r_ref, kf_ref, v_ref, gw_ref, gb_ref, rk_ref, o_ref):
        for sl in _heads(w):
            yv = y_ref[:, sl]
            yc = yv - jnp.mean(yv, axis=1, keepdims=True)
            rstd = lax.rsqrt(jnp.mean(yc * yc, axis=1, keepdims=True) + GN_EPS)
            s = jnp.sum(r_ref[:, sl] * kf_ref[:, sl] * rk_ref[:, sl], axis=1, keepdims=True)
            o_ref[:, sl] = (yc * rstd) * gw_ref[:, sl] + gb_ref[:, sl] + s * v_ref[:, sl]

    tok = pl.BlockSpec((tr, w), lambda i: (i, 0))
    par = pl.BlockSpec((1, w), lambda i: (0, 0))
    return pl.pallas_call(
        body, name="gn_bonus_fwd", grid=(rows // tr,),
        in_specs=[tok] * 4 + [par] * 3, out_specs=tok,
        out_shape=jax.ShapeDtypeStruct((rows, w), F32),
        compiler_params=pltpu.CompilerParams(dimension_semantics=("parallel",)),
    )(y, r, kf, v, gw, gb, rk)


def _gn_bwd_call(y, r, kf, v, gw, gb, rk, do):
    rows, w = y.shape
    tr = _row_tile(rows, w, budget=512 * 1024)

    def body(y_ref, r_ref, kf_ref, v_ref, gw_ref, rk_ref, do_ref,
             dy_ref, dr_ref, dkf_ref, dv_ref, dgw_ref, dgb_ref, drk_ref):
        @pl.when(pl.program_id(0) == 0)
        def _():
            dgw_ref[...] = jnp.zeros_like(dgw_ref)
            dgb_ref[...] = jnp.zeros_like(dgb_ref)
            drk_ref[...] = jnp.zeros_like(drk_ref)

        for sl in _heads(w):
            yv, rv, kv, vv, dov = y_ref[:, sl], r_ref[:, sl], kf_ref[:, sl], v_ref[:, sl], do_ref[:, sl]
            yc = yv - jnp.mean(yv, axis=1, keepdims=True)
            rstd = lax.rsqrt(jnp.mean(yc * yc, axis=1, keepdims=True) + GN_EPS)
            yhat = yc * rstd
            dyhat = dov * gw_ref[:, sl]
            dy_ref[:, sl] = rstd * (dyhat - jnp.mean(dyhat, axis=1, keepdims=True)
                                    - yhat * jnp.mean(dyhat * yhat, axis=1, keepdims=True))
            rkv = rk_ref[:, sl]
            s = jnp.sum(rv * kv * rkv, axis=1, keepdims=True)
            ds = jnp.sum(dov * vv, axis=1, keepdims=True)
            dv_ref[:, sl] = s * dov
            dr_ref[:, sl] = ds * kv * rkv
            dkf_ref[:, sl] = ds * rv * rkv
            dgw_ref[:, sl] += jnp.sum(dov * yhat, axis=0, keepdims=True)
            dgb_ref[:, sl] += jnp.sum(dov, axis=0, keepdims=True)
            drk_ref[:, sl] += jnp.sum(ds * rv * kv, axis=0, keepdims=True)

    tok = pl.BlockSpec((tr, w), lambda i: (i, 0))
    par = pl.BlockSpec((1, w), lambda i: (0, 0))
    tshape = jax.ShapeDtypeStruct((rows, w), F32)
    pshape = jax.ShapeDtypeStruct((1, w), F32)
    return pl.pallas_call(
        body, name="gn_bonus_bwd", grid=(rows // tr,),
        in_specs=[tok] * 4 + [par] * 2 + [tok], out_specs=[tok] * 4 + [par] * 3,
        out_shape=[tshape] * 4 + [pshape] * 3,
        compiler_params=pltpu.CompilerParams(dimension_semantics=("arbitrary",)),
    )(y, r, kf, v, gw, rk, do)


@jax.custom_vjp
def gn_bonus(y, r, kf, v, gw, gb, rk):
    return _gn_fwd_call(y, r, kf, v, gw, gb, rk)


def _gn_bwd(res, do):
    y, r, kf, v, gw, gb, rk = res
    dy, dr, dkf, dv, dgw, dgb, drk = _gn_bwd_call(y, r, kf, v, gw, gb, rk, do)
    return dy, dr, dkf, dv, dgw, dgb, drk


gn_bonus.defvjp(lambda *a: (_gn_fwd_call(*a), a), _gn_bwd)


def _pair_masks(rows):
    lane = lax.broadcasted_iota(jnp.int32, (rows, PAIR), 1)
    return lane < HEAD_DIM, lane >= HEAD_DIM


def _bd(x):
    m0, m1 = _pair_masks(x.shape[0])
    return jnp.concatenate([jnp.where(m0, x, 0.0), jnp.where(m1, x, 0.0)], axis=0)


def _unbd(m, c):
    return jnp.where(_pair_masks(c)[0], m[:c], m[c:])


def _pair_a(l2, r2):
    return _mm(l2, _bd(r2), tb=True)


def _pair_mul(p2, x2):
    return _mm(p2, _bd(x2))


def _pair_mul_t(p2, x2):
    return _unbd(_mm(p2, x2, ta=True), p2.shape[0])


def _block_diag_mask():
    row = lax.broadcasted_iota(jnp.int32, (PAIR, PAIR), 0)
    lane = lax.broadcasted_iota(jnp.int32, (PAIR, PAIR), 1)
    return (row < HEAD_DIM) == (lane < HEAD_DIM), row == lane


def _wkv_pair_common(r, lw, k, a, b):
    c = r[0].shape[0]
    pairs = range(len(r))
    i = lax.broadcasted_iota(jnp.int32, (c, PAIR), 0)
    j = lax.broadcasted_iota(jnp.int32, (c, PAIR), 1) % c
    strict, incl = i > j, i >= j
    ti = lax.broadcasted_iota(jnp.int32, (c, c), 0)
    tj = lax.broadcasted_iota(jnp.int32, (c, c), 1)
    tri = jnp.where(ti >= tj, 1.0, 0.0).astype(BF16)
    lc = [sum(_dg(tri, part, False, False) for part in _split(lw[p], 3)) for p in pairs]
    lend = [lc[p][c - 1:c, :] for p in pairs]
    rt = [r[p] * jnp.exp(lc[p]) for p in pairs]
    at = [a[p] * jnp.exp(lc[p] - lw[p]) for p in pairs]
    pinv = [jnp.exp(-lc[p]) for p in pairs]
    kt = [k[p] * pinv[p] for p in pairs]
    bt = [b[p] * pinv[p] for p in pairs]
    e = [jnp.exp(lend[p] - lc[p]) for p in pairs]
    ktp = [k[p] * e[p] for p in pairs]
    btp = [b[p] * e[p] for p in pairs]
    a_ab = [jnp.where(strict, _pair_a(at[p], bt[p]), 0.0) for p in pairs]
    a_ak = [jnp.where(strict, _pair_a(at[p], kt[p]), 0.0) for p in pairs]
    a_rb = [jnp.where(incl, _pair_a(rt[p], bt[p]), 0.0) for p in pairs]
    a_rk = [jnp.where(incl, _pair_a(rt[p], kt[p]), 0.0) for p in pairs]
    t = [jnp.where(i == j, 1.0, 0.0) + a_ab[p] for p in pairs]
    xp = a_ab
    n = 2
    while n < c:
        xp = [_pair_mul(xp[p], xp[p]) for p in pairs]
        t = [t[p] + _pair_mul(t[p], xp[p]) for p in pairs]
        n *= 2
    bdm, eye = _block_diag_mask()
    pend_col = [jnp.sum(jnp.where(eye, jnp.exp(lend[p]), 0.0), axis=1, keepdims=True) for p in pairs]
    return dict(rt=rt, at=at, kt=kt, bt=bt, ktp=ktp, btp=btp, a_ak=a_ak, a_rb=a_rb, a_rk=a_rk, t=t,
                pend_col=pend_col, lend=lend, lc=lc, strict=strict, incl=incl, tri=tri, bdm=bdm)


def _wkv_group(width):
    npair = width // PAIR
    g = min(WKV_PAIRS_PER_STEP, npair)
    assert npair % g == 0
    return npair, g


def _wkv_fwd_call(r, lw, k, v, a, b):
    tokens, width = r.shape
    c = WKV_CHUNK
    nc = tokens // c
    npair, g = _wkv_group(width)

    def body(r_ref, lw_ref, k_ref, v_ref, a_ref, b_ref, y_ref, s_ref, st):
        @pl.when(pl.program_id(1) == 0)
        def _():
            st[...] = jnp.zeros_like(st)

        pairs = range(g)
        rv, lwv, kv, vv, av, bv = ([ref[:, p * PAIR:(p + 1) * PAIR] for p in pairs]
                                   for ref in (r_ref, lw_ref, k_ref, v_ref, a_ref, b_ref))
        s0 = [st[p] for p in pairs]
        q = _wkv_pair_common(rv, lwv, kv, av, bv)
        w1 = [_mm(q["at"][p], s0[p]) + _pair_mul(q["a_ak"][p], vv[p]) for p in pairs]
        u = [_pair_mul(q["t"][p], w1[p]) for p in pairs]
        y = [_mm(q["rt"][p], s0[p]) + _pair_mul(q["a_rb"][p], u[p]) + _pair_mul(q["a_rk"][p], vv[p]) for p in pairs]
        grow = [_mm(jnp.concatenate([q["btp"][p], q["ktp"][p]], axis=0), jnp.concatenate([u[p], vv[p]], axis=0), ta=True)
                for p in pairs]
        for p in pairs:
            y_ref[:, p * PAIR:(p + 1) * PAIR] = y[p]
            s_ref[0, p] = s0[p]
            st[p] = q["pend_col"][p] * s0[p] + jnp.where(q["bdm"], grow[p], 0.0)

    tok = pl.BlockSpec((c, g * PAIR), lambda gi, ci: (ci, gi))
    return pl.pallas_call(
        body, name="wkv_fwd", grid=(npair // g, nc),
        in_specs=[tok] * 6,
        out_specs=[tok, pl.BlockSpec((1, g, PAIR, PAIR), lambda gi, ci: (ci, gi, 0, 0))],
        out_shape=[jax.ShapeDtypeStruct((tokens, width), F32), jax.ShapeDtypeStruct((nc, npair, PAIR, PAIR), F32)],
        scratch_shapes=[pltpu.VMEM((g, PAIR, PAIR), F32)],
        compiler_params=pltpu.CompilerParams(dimension_semantics=("parallel", "arbitrary")),
    )(r, lw, k, v, a, b)


def _wkv_bwd_call(r, lw, k, v, a, b, s, dy):
    tokens, width = r.shape
    c = WKV_CHUNK
    nc = tokens // c
    npair, g = _wkv_group(width)

    def body(r_ref, lw_ref, k_ref, v_ref, a_ref, b_ref, s_ref, dy_ref,
             dr_ref, dlw_ref, dk_ref, dv_ref, da_ref, db_ref, dst):
        @pl.when(pl.program_id(1) == 0)
        def _():
            dst[...] = jnp.zeros_like(dst)

        pairs = range(g)
        rv, lwv, kv, vv, av, bv, dyv = ([ref[:, p * PAIR:(p + 1) * PAIR] for p in pairs]
                                        for ref in (r_ref, lw_ref, k_ref, v_ref, a_ref, b_ref, dy_ref))
        s0 = [s_ref[0, p] for p in pairs]
        dsc = [dst[p] for p in pairs]
        q = _wkv_pair_common(rv, lwv, kv, av, bv)
        rt, at, kt, bt, ktp, btp, t = (q[n] for n in ("rt", "at", "kt", "bt", "ktp", "btp", "t"))
        a_ak, a_rb, a_rk, strict, incl = (q[n] for n in ("a_ak", "a_rb", "a_rk", "strict", "incl"))
        w1 = [_mm(at[p], s0[p]) + _pair_mul(a_ak[p], vv[p]) for p in pairs]
        u = [_pair_mul(t[p], w1[p]) for p in pairs]
        du = [_pair_mul_t(a_rb[p], dyv[p]) + _mm(btp[p], dsc[p]) for p in pairs]
        dw1 = [_pair_mul_t(t[p], du[p]) for p in pairs]
        dv = [_pair_mul_t(a_rk[p], dyv[p]) + _mm(ktp[p], dsc[p]) + _pair_mul_t(a_ak[p], dw1[p]) for p in pairs]
        da_ab = [jnp.where(strict, _pair_a(dw1[p], u[p]), 0.0) for p in pairs]
        da_ak = [jnp.where(strict, _pair_a(dw1[p], vv[p]), 0.0) for p in pairs]
        da_rb = [jnp.where(incl, _pair_a(dyv[p], u[p]), 0.0) for p in pairs]
        da_rk = [jnp.where(incl, _pair_a(dyv[p], vv[p]), 0.0) for p in pairs]
        d_rt = [_mm(dyv[p], s0[p], tb=True) + _pair_mul(da_rb[p], bt[p]) + _pair_mul(da_rk[p], kt[p]) for p in pairs]
        d_at = [_mm(dw1[p], s0[p], tb=True) + _pair_mul(da_ab[p], bt[p]) + _pair_mul(da_ak[p], kt[p]) for p in pairs]
        d_bt = [_pair_mul_t(da_ab[p], at[p]) + _pair_mul_t(da_rb[p], rt[p]) for p in pairs]
        d_kt = [_pair_mul_t(da_ak[p], at[p]) + _pair_mul_t(da_rk[p], rt[p]) for p in pairs]
        d_btp = [_mm(u[p], dsc[p], tb=True) for p in pairs]
        d_ktp = [_mm(vv[p], dsc[p], tb=True) for p in pairs]
        ones = jnp.ones((8, PAIR), BF16)
        dpend = [sum(_dg(ones, part, False, True) for part in _split(dsc[p] * s0[p], 3))[0:1, :] * jnp.exp(q["lend"][p])
                 for p in pairs]
        grow = [_mm(jnp.concatenate([rt[p], at[p]], axis=0), jnp.concatenate([dyv[p], dw1[p]], axis=0), ta=True)
                for p in pairs]
        last = lax.broadcasted_iota(jnp.int32, (c, PAIR), 0) == c - 1
        for p in pairs:
            sl = slice(p * PAIR, (p + 1) * PAIR)
            dst[p] = q["pend_col"][p] * dsc[p] + jnp.where(q["bdm"], grow[p], 0.0)
            lc_e = d_ktp[p] * ktp[p] + d_btp[p] * btp[p]
            dlend = jnp.sum(lc_e, axis=0, keepdims=True) + dpend[p]
            dlc = d_rt[p] * rt[p] - d_kt[p] * kt[p] - d_bt[p] * bt[p] - lc_e + jnp.where(last, dlend, 0.0)
            dlp = d_at[p] * at[p]
            dlw_ref[:, sl] = sum(_dg(q["tri"], part, True, False) for part in _split(dlc + dlp, 3)) - dlp
            lc = q["lc"][p]
            pinv = jnp.exp(-lc)
            e = jnp.exp(q["lend"][p] - lc)
            dr_ref[:, sl] = d_rt[p] * jnp.exp(lc)
            da_ref[:, sl] = d_at[p] * jnp.exp(lc - lwv[p])
            dk_ref[:, sl] = d_kt[p] * pinv + d_ktp[p] * e
            db_ref[:, sl] = d_bt[p] * pinv + d_btp[p] * e
            dv_ref[:, sl] = dv[p]

    tok = pl.BlockSpec((c, g * PAIR), lambda gi, ci: (nc - 1 - ci, gi))
    tshape = jax.ShapeDtypeStruct((tokens, width), F32)
    return pl.pallas_call(
        body, name="wkv_bwd", grid=(npair // g, nc),
        in_specs=[tok] * 6 + [pl.BlockSpec((1, g, PAIR, PAIR), lambda gi, ci: (nc - 1 - ci, gi, 0, 0)), tok],
        out_specs=[tok] * 6, out_shape=[tshape] * 6,
        scratch_shapes=[pltpu.VMEM((g, PAIR, PAIR), F32)],
        compiler_params=pltpu.CompilerParams(dimension_semantics=("parallel", "arbitrary")),
    )(r, lw, k, v, a, b, s, dy)


@jax.custom_vjp
def wkv7(r, lw, k, v, a, b):
    return _wkv_fwd_call(r, lw, k, v, a, b)[0]


def _wkv7_fwd(r, lw, k, v, a, b):
    y, s = _wkv_fwd_call(r, lw, k, v, a, b)
    return y, (r, lw, k, v, a, b, s)


wkv7.defvjp(_wkv7_fwd, lambda res, dy: tuple(_wkv_bwd_call(*res, dy)))


def _attn_block(tokens):
    return ATTN_BLOCK_BIG if tokens % ATTN_BLOCK_BIG == 0 else ATTN_BLOCK


def _fox_layouts(cum):
    tokens, heads = cum.shape
    t = _attn_block(tokens)
    cq = cum.reshape(tokens, heads // 2, 2).transpose(1, 0, 2)
    ck = cum.T.reshape(heads // 2, 2, tokens // t, t).transpose(0, 2, 1, 3)
    return cq, ck


def _head_lane_masks(rows):
    lane = lax.broadcasted_iota(jnp.int32, (rows, 2 * HEAD_DIM), 1)
    return [lane < HEAD_DIM, lane >= HEAD_DIM]


def _fox_fwd_call(q, k, v, cq, ck):
    tokens, width = q.shape
    t = _attn_block(tokens)
    nb = tokens // t
    hd = HEAD_DIM
    npair = width // (2 * hd)

    def body(q_ref, k_ref, v_ref, cq_ref, ck_ref, o_ref, lse_ref):
        i = pl.program_id(1)
        masks = _head_lane_masks(t)
        q2 = q_ref[...]
        qs = [jnp.where(mk, q2, 0.0).astype(BF16) for mk in masks]
        cqs = [cq_ref[0, :, hh:hh + 1] for hh in range(2)]

        def block(j, carry, diagonal):
            off = pl.multiple_of(j * t, t)
            ckj = ck_ref[0, j]
            k2 = k_ref[pl.ds(off, t), :].astype(BF16)
            v2 = v_ref[pl.ds(off, t), :].astype(BF16)
            out = []
            for hh in range(2):
                m, l, acc = carry[hh]
                s = _dg(qs[hh], k2, False, True) + (cqs[hh] - ckj[hh:hh + 1, :])
                if diagonal:
                    keep = lax.broadcasted_iota(jnp.int32, (t, t), 0) >= lax.broadcasted_iota(jnp.int32, (t, t), 1)
                    s = jnp.where(keep, s, NEG_BIG)
                m_new = jnp.maximum(m, jnp.max(s, axis=1, keepdims=True))
                alpha = jnp.exp(m - m_new)
                p = jnp.exp(s - m_new)
                l = alpha * l + jnp.sum(p, axis=1, keepdims=True)
                acc = alpha * acc + _dg(p.astype(BF16), v2, False, False)
                out.append((m_new, l, acc))
            return tuple(out)

        init = tuple((jnp.full((t, 1), NEG_BIG, F32), jnp.zeros((t, 1), F32), jnp.zeros((t, 2 * hd), F32)) for _ in range(2))
        res = lax.fori_loop(0, i, lambda j, c: block(j, c, False), init)
        res = block(i, res, True)
        o_ref[...] = jnp.where(masks[0], res[0][2] / res[0][1], res[1][2] / res[1][1])
        for hh in range(2):
            lse_ref[0, :, hh:hh + 1] = res[hh][0] + jnp.log(res[hh][1])

    blk = pl.BlockSpec((t, 2 * hd), lambda hp, i: (i, hp))
    full = pl.BlockSpec((tokens, 2 * hd), lambda hp, i: (0, hp))
    cq_spec = pl.BlockSpec((1, t, 2), lambda hp, i: (hp, i, 0))
    ck_spec = pl.BlockSpec((1, nb, 2, t), lambda hp, i: (hp, 0, 0, 0))
    return pl.pallas_call(
        body, name="fox_fwd", grid=(npair, nb),
        in_specs=[blk, full, full, cq_spec, ck_spec],
        out_specs=[blk, cq_spec],
        out_shape=[jax.ShapeDtypeStruct((tokens, width), F32), jax.ShapeDtypeStruct((npair, tokens, 2), F32)],
        compiler_params=pltpu.CompilerParams(dimension_semantics=("parallel", "arbitrary")),
    )(q, k, v, cq, ck)


def _fox_bwd_call(q, k, v, cq, ck, o, lse, do):
    tokens, width = q.shape
    t = _attn_block(tokens)
    nb = tokens // t
    hd = HEAD_DIM
    npair = width // (2 * hd)

    def body(q_ref, k_ref, v_ref, cq_ref, ck_ref, o_ref, lse_ref, do_ref, dq_ref, dk_ref, dv_ref, dck_ref, dcq_ref):
        i = pl.program_id(1)

        @pl.when(i == 0)
        def _():
            dk_ref[...] = jnp.zeros_like(dk_ref)
            dv_ref[...] = jnp.zeros_like(dv_ref)
            dck_ref[...] = jnp.zeros_like(dck_ref)

        masks = _head_lane_masks(t)
        q2, do2, o2 = q_ref[...], do_ref[...], o_ref[...]
        qs = [jnp.where(mk, q2, 0.0).astype(BF16) for mk in masks]
        dos = [jnp.where(mk, do2, 0.0).astype(BF16) for mk in masks]
        deltas = [jnp.sum(dos[hh].astype(F32) * o2, axis=1, keepdims=True) for hh in range(2)]
        bias = [cq_ref[0, :, hh:hh + 1] - lse_ref[0, :, hh:hh + 1] for hh in range(2)]

        def block(j, carry, diagonal):
            off = pl.multiple_of(j * t, t)
            ckj = ck_ref[0, j]
            k2 = k_ref[pl.ds(off, t), :].astype(BF16)
            v2 = v_ref[pl.ds(off, t), :].astype(BF16)
            out = []
            dk2 = jnp.zeros((t, 2 * hd), F32)
            dv2 = jnp.zeros((t, 2 * hd), F32)
            for hh in range(2):
                s = _dg(qs[hh], k2, False, True) + (bias[hh] - ckj[hh:hh + 1, :])
                if diagonal:
                    keep = lax.broadcasted_iota(jnp.int32, (t, t), 0) >= lax.broadcasted_iota(jnp.int32, (t, t), 1)
                    s = jnp.where(keep, s, NEG_BIG)
                p = jnp.exp(s)
                dp = _dg(dos[hh], v2, False, True)
                ds = p * (dp - deltas[hh])
                dsb = ds.astype(BF16)
                dq, rowsum = carry[hh]
                out.append((dq + _dg(dsb, k2, False, False), rowsum + jnp.sum(ds, axis=1, keepdims=True)))
                dk2 = dk2 + _dg(dsb, qs[hh], True, False)
                dv2 = dv2 + _dg(p.astype(BF16), dos[hh], True, False)
                dck_ref[0, j, hh:hh + 1, :] -= jnp.sum(ds, axis=0, keepdims=True)
            dk_ref[pl.ds(off, t), :] += dk2
            dv_ref[pl.ds(off, t), :] += dv2
            return tuple(out)

        init = tuple((jnp.zeros((t, 2 * hd), F32), jnp.zeros((t, 1), F32)) for _ in range(2))
        res = lax.fori_loop(0, i, lambda j, c: block(j, c, False), init)
        res = block(i, res, True)
        dq_ref[...] = jnp.where(masks[0], res[0][0], res[1][0])
        for hh in range(2):
            dcq_ref[0, :, hh:hh + 1] = res[hh][1]

    blk = pl.BlockSpec((t, 2 * hd), lambda hp, i: (i, hp))
    full = pl.BlockSpec((tokens, 2 * hd), lambda hp, i: (0, hp))
    cq_spec = pl.BlockSpec((1, t, 2), lambda hp, i: (hp, i, 0))
    ck_spec = pl.BlockSpec((1, nb, 2, t), lambda hp, i: (hp, 0, 0, 0))
    tshape = jax.ShapeDtypeStruct((tokens, width), F32)
    return pl.pallas_call(
        body, name="fox_bwd", grid=(npair, nb),
        in_specs=[blk, full, full, cq_spec, ck_spec, blk, cq_spec, blk],
        out_specs=[blk, full, full, ck_spec, cq_spec],
        out_shape=[tshape, tshape, tshape, jax.ShapeDtypeStruct((npair, nb, 2, t), F32),
                   jax.ShapeDtypeStruct((npair, tokens, 2), F32)],
        compiler_params=pltpu.CompilerParams(dimension_semantics=("parallel", "arbitrary")),
    )(q, k, v, cq, ck, o, lse, do)


@jax.custom_vjp
def fox_attention(q, k, v, cum):
    return _fox_fwd_call(q, k, v, *_fox_layouts(cum))[0]


def _fox_fwd(q, k, v, cum):
    cq, ck = _fox_layouts(cum)
    o, lse = _fox_fwd_call(q, k, v, cq, ck)
    return o, (q, k, v, cq, ck, o, lse)


def _fox_bwd(res, do):
    q, k, v, cq, ck, o, lse = res
    dq, dk, dv, dck, dcq = _fox_bwd_call(q, k, v, cq, ck, o, lse, do)
    npair, nb, _, t = dck.shape
    dcum = dck.transpose(0, 2, 1, 3).reshape(2 * npair, nb * t).T + dcq.transpose(1, 0, 2).reshape(nb * t, 2 * npair)
    return dq, dk, dv, dcum


fox_attention.defvjp(_fox_fwd, _fox_bwd)


def _loss_call(y, target):
    rows, d = y.shape
    tr = _row_tile(rows, d)

    def body(y_ref, t_ref, loss_ref, dy_ref):
        @pl.when(pl.program_id(0) == 0)
        def _():
            loss_ref[...] = jnp.zeros_like(loss_ref)

        diff = y_ref[...] - t_ref[...]
        dy_ref[...] = diff * (1.0 / d)
        loss_ref[...] += (0.5 / d) * jnp.sum(jnp.sum(diff * diff, axis=1, keepdims=True), axis=0, keepdims=True)

    return pl.pallas_call(
        body, name="loss", grid=(rows // tr,),
        in_specs=[pl.BlockSpec((tr, d), lambda i: (i, 0))] * 2,
        out_specs=[pl.BlockSpec((1, 1), lambda i: (0, 0)), pl.BlockSpec((tr, d), lambda i: (i, 0))],
        out_shape=[jax.ShapeDtypeStruct((1, 1), F32), jax.ShapeDtypeStruct((rows, d), F32)],
        compiler_params=pltpu.CompilerParams(dimension_semantics=("arbitrary",)),
    )(y, target)


def _adamw_call(w, g, m, v):
    rows, cols = w.shape
    tr = _row_tile_ragged(rows, cols, budget=1024 * 1024)
    c1 = 1.0 / (1.0 - ADAM_B1 ** ADAM_STEP)
    c2 = 1.0 / (1.0 - ADAM_B2 ** ADAM_STEP)

    def body(w_ref, g_ref, m_ref, v_ref, d_ref, nm_ref, nv_ref):
        gv = g_ref[...]
        nm = ADAM_B1 * m_ref[...] + (1.0 - ADAM_B1) * gv
        nv = ADAM_B2 * v_ref[...] + (1.0 - ADAM_B2) * (gv * gv)
        nm_ref[...] = nm
        nv_ref[...] = nv
        d_ref[...] = -ADAM_LR * ((nm * c1) / (jnp.sqrt(nv * c2) + ADAM_EPS) + ADAM_WD * w_ref[...])

    spec = pl.BlockSpec((tr, cols), lambda i: (i, 0))
    shape = jax.ShapeDtypeStruct((rows, cols), F32)
    return pl.pallas_call(
        body, name="adamw", grid=(pl.cdiv(rows, tr),),
        in_specs=[spec] * 4, out_specs=[spec] * 3, out_shape=[shape] * 3,
        compiler_params=pltpu.CompilerParams(dimension_semantics=("parallel",)),
    )(w, g, m, v)


def _my_place():
    return lax.axis_index("x"), lax.axis_index("y"), lax.axis_index("c")


def _place_index(px, py, pc):
    return 4 * px + 2 * py + pc


HBM_SPEC = pl.BlockSpec(memory_space=pltpu.HBM)


def _all_gather_call(block):
    def body(x_ref, out_ref, send_sems, recv_sems, local_sem):
        x, y, c = _my_place()
        me, sibling = (x, y, c), (x, y, 1 - c)
        chips = [(1 - x, y), (x, 1 - y), (1 - x, 1 - y)]

        def slot(px, py, pc):
            return out_ref.at[_place_index(px, py, pc)]

        def copy(k, blk, to, src=None):
            return pltpu.make_async_remote_copy(
                src_ref=slot(*blk) if src is None else src, dst_ref=slot(*blk),
                send_sem=send_sems.at[k], recv_sem=recv_sems.at[k],
                device_id=to, device_id_type=pl.DeviceIdType.MESH)

        mine = pltpu.make_async_copy(x_ref, slot(*me), local_sem)
        mine.start()
        first = [copy(0, me, sibling, src=x_ref)]
        first += [copy(1 + j, me, (*chip, c), src=x_ref) for j, chip in enumerate(chips)]
        for cp in first:
            cp.start()
        passed = [copy(4 + j, (*chip, c), sibling) for j, chip in enumerate(chips)]
        for j, chip in enumerate(chips):
            copy(1 + j, (*chip, c), me).wait_recv()
            passed[j].start()
        copy(0, sibling, me).wait_recv()
        for j, chip in enumerate(chips):
            copy(4 + j, (*chip, 1 - c), me).wait_recv()
        for cp in first + passed:
            cp.wait_send()
        mine.wait()

    return pl.pallas_call(
        body, name="all_gather",
        out_shape=jax.ShapeDtypeStruct((N_DEV,) + block.shape, block.dtype),
        in_specs=[HBM_SPEC], out_specs=HBM_SPEC,
        scratch_shapes=[pltpu.SemaphoreType.DMA((7,)), pltpu.SemaphoreType.DMA((7,)), pltpu.SemaphoreType.DMA],
    )(block)


SEM_SPEC = pl.BlockSpec(memory_space=pltpu.SEMAPHORE)
SIDE_EFFECT = pltpu.SideEffectType.DATAFLOW_SIDE_EFFECTING


def _peers():
    x, y, c = _my_place()
    out = []
    for k in range(1, N_DEV):
        peer = (x ^ (k >> 2), y ^ ((k >> 1) & 1), c ^ (k & 1))
        out.append((k - 1, peer, _place_index(*peer)))
    return _place_index(x, y, c), out


def _spread_start(src, per_peer, name, after=None):
    slot = src.shape[1:] if per_peer else src.shape
    order = () if after is None else (after,)

    def body(src_ref, land_ref, *rest):
        send_sems, recv_sems, src_thru, land_thru, token = rest[len(order):]
        mine, peers = _peers()
        for k, peer, peer_idx in peers:
            pltpu.make_async_remote_copy(
                src_ref=src_ref.at[peer_idx] if per_peer else src_ref, dst_ref=land_ref.at[mine],
                send_sem=send_sems.at[k], recv_sem=recv_sems.at[k],
                device_id=peer, device_id_type=pl.DeviceIdType.MESH).start()
        token[...] = jnp.zeros_like(token)

    return pl.pallas_call(
        body, name=name,
        out_shape=(pltpu.SemaphoreType.DMA((N_DEV - 1,)), pltpu.SemaphoreType.DMA((N_DEV - 1,)),
                   pltpu.HBM(src.shape, src.dtype), pltpu.HBM((N_DEV,) + slot, src.dtype),
                   jax.ShapeDtypeStruct((8, 128), F32)),
        in_specs=(HBM_SPEC, HBM_SPEC) + (pl.BlockSpec(memory_space=pl.ANY),) * len(order),
        out_specs=(SEM_SPEC, SEM_SPEC, HBM_SPEC, HBM_SPEC, pl.BlockSpec(memory_space=pltpu.VMEM)),
        input_output_aliases={0: 2, 1: 3},
        compiler_params=pltpu.CompilerParams(has_side_effects=SIDE_EFFECT),
    )(pltpu.with_memory_space_constraint(src, pltpu.HBM),
      pltpu.with_memory_space_constraint(lax.empty((N_DEV,) + slot, src.dtype), pltpu.HBM), *order)


def _spread_wait(handles, after, per_peer, name):
    send_sems, recv_sems, src_thru, land_thru = handles

    def body(src_ref, land_ref, send_sems, recv_sems, after_ref, src_dead, got_ref):
        _, peers = _peers()
        for k, peer, peer_idx in peers:
            copy = pltpu.make_async_remote_copy(
                src_ref=src_ref.at[peer_idx] if per_peer else src_ref, dst_ref=land_ref.at[peer_idx],
                send_sem=send_sems.at[k], recv_sem=recv_sems.at[k],
                device_id=peer, device_id_type=pl.DeviceIdType.MESH)
            copy.wait_send()
            copy.wait_recv()

    return pl.pallas_call(
        body, name=name,
        out_shape=(pltpu.HBM(src_thru.shape, src_thru.dtype), pltpu.HBM(land_thru.shape, land_thru.dtype)),
        in_specs=(HBM_SPEC, HBM_SPEC, SEM_SPEC, SEM_SPEC, pl.BlockSpec(memory_space=pl.ANY)),
        out_specs=(HBM_SPEC, HBM_SPEC), input_output_aliases={0: 0, 1: 1},
        compiler_params=pltpu.CompilerParams(has_side_effects=SIDE_EFFECT),
    )(src_thru, land_thru, send_sems, recv_sems, after)


def _sum_slots_call(slots):
    _, rows, cols = slots.shape
    tr = _row_tile_ragged(rows, cols, budget=512 * 1024)

    def body(s_ref, o_ref):
        acc = s_ref[0].astype(F32)
        for j in range(1, N_DEV):
            acc = acc + s_ref[j].astype(F32)
        o_ref[...] = acc

    return pl.pallas_call(
        body, name="sum_slots", grid=(pl.cdiv(rows, tr),),
        in_specs=[pl.BlockSpec((N_DEV, tr, cols), lambda i: (0, i, 0))],
        out_specs=pl.BlockSpec((tr, cols), lambda i: (i, 0)),
        out_shape=jax.ShapeDtypeStruct((rows, cols), F32),
        compiler_params=pltpu.CompilerParams(dimension_semantics=("parallel",)),
    )(slots)


def _with_own_slot(got, own, mine):
    return lax.dynamic_update_index_in_dim(got, own, mine, 0)


def _pack(vectors, width):
    flat = jnp.concatenate([v.reshape(-1) for v in vectors])
    return jnp.pad(flat, (0, width - flat.shape[0])).reshape(width // 128, 128)


def _unpack(packed, like):
    flat = packed.reshape(-1)
    out, at = [], 0
    for v in like:
        out.append(flat[at:at + v.size].reshape(v.shape))
        at += v.size
    return tuple(out)


def _sum_over_devices(grads):
    n = sum(v.size for v in grads)
    width = -(-n // 1024) * 1024
    return _unpack(_sum_slots_call(_all_gather_call(_pack(grads, width))), grads)


def _cols_from_slots(slots):
    n, rows, cols = slots.shape
    return slots.transpose(1, 0, 2).reshape(rows, n * cols)


def _rows_from_slots(slots):
    return slots.reshape(-1, slots.shape[2])


def _pad128(n):
    return -(-n // 128) * 128


def _in_proj_layout(slots, rcols, fcols):
    wt = _rows_from_slots(slots)
    padr = lambda w, n: jnp.pad(w, ((0, n - w.shape[0]), (0, 0)))
    return jnp.concatenate([padr(wt[:rcols], _pad128(rcols)), padr(wt[rcols:rcols + fcols], _pad128(fcols)),
                            wt[rcols + fcols:]], axis=0)


def _stage_embed(meta, x, n1, lp):
    h0 = jnp.concatenate([meta, x, jnp.zeros((lp - meta.shape[0] - x.shape[0], x.shape[1]), F32)], axis=0)
    return h0, rmsnorm(h0, n1)


def _stage_mix(proj, small, w2, a2, g2, rw, fw):
    (mu, w0, a0, k_k, k_a, r_k, gn_w, gn_b, q_g, k_g, f_bias) = small
    dl, al, gl = w2.shape[1], a2.shape[1], g2.shape[1]
    rcols = 3 * rw + dl + al + gl
    fcols = 3 * fw + fw // HEAD_DIM
    rpad, fpad = _pad128(rcols), _pad128(fcols)
    z_r, z_f, z_g = proj[:, :rcols], proj[:, rpad:rpad + fcols], proj[:, rpad + fpad:]

    z_prev = jnp.pad(z_r, ((1, 0), (0, 0)))[:-1]
    z = z_r + bmul(z_prev - z_r, mu)
    r, k, v = z[:, :rw], z[:, rw:2 * rw], z[:, 2 * rw:3 * rw]
    wd, ad, gd = z[:, 3 * rw:3 * rw + dl], z[:, 3 * rw + dl:3 * rw + dl + al], z[:, 3 * rw + dl + al:]
    w_log = -jax.nn.softplus(-badd(dense_cols(jnp.tanh(wd), w2), w0)) - 0.5
    lw = -jnp.exp(w_log)
    a_sig = jax.nn.sigmoid(badd(dense_cols(ad, a2), a0))
    g = dense_cols(jax.nn.sigmoid(gd), g2)
    kk = head_l2norm(bmul(k, k_k))
    kf = k * (1.0 + bmul(a_sig - 1.0, k_a))
    y = wkv7(r, lw, kf, v, -kk, kk * a_sig)
    y_a = gn_bonus(y, r, kf, v, gn_w, gn_b, r_k.reshape(1, rw)) * g

    fq, fk, fv, fl = z_f[:, :fw], z_f[:, fw:2 * fw], z_f[:, 2 * fw:3 * fw], z_f[:, 3 * fw:]
    fq = head_rms(fq, q_g) * (HEAD_DIM ** -0.5)
    fk = head_rms(fk, k_g)
    cum = jnp.cumsum(jax.nn.log_sigmoid(badd(fl, f_bias)), axis=0)
    y_b = fox_attention(fq, fk, fv, cum)
    return y_a, y_b, jax.nn.sigmoid(z_g)


def _stage_merge(h0, y_a, y_b, gates, w_a, w_b, w_o):
    d = h0.shape[1]
    merged = gates[:, :d] * dense_cols(y_a, w_a) + gates[:, d:] * dense_cols(y_b, w_b)
    return h0 + dense(merged, w_o)


def _stage_ffn(h1, n2, w_gu, w_dn):
    gu = dense_cols(rmsnorm(h1, n2), w_gu)
    dff = w_dn.shape[0]
    return h1 + dense(jax.nn.silu(gu[:, :dff]) * gu[:, dff:], w_dn)


SHARDED = ("meta_tokens", "w_in", "rwkv_w2", "rwkv_a2", "rwkv_g2", "w_branch_a", "w_branch_b", "w_o", "w_gate_up", "w_down")
SMALL = ("norm1_g", "rwkv_mu", "rwkv_w0", "rwkv_a0", "rwkv_k_k", "rwkv_k_a", "rwkv_r_k", "rwkv_gn_w", "rwkv_gn_b",
         "fox_q_norm_g", "fox_k_norm_g", "fox_f_bias", "norm2_g")
WEIGHTS = ("meta_tokens", "norm1_g", "w_in", "rwkv_mu", "rwkv_w0", "rwkv_w2", "rwkv_a0", "rwkv_a2", "rwkv_g2", "rwkv_k_k",
           "rwkv_k_a", "rwkv_r_k", "rwkv_gn_w", "rwkv_gn_b", "fox_q_norm_g", "fox_k_norm_g", "fox_f_bias", "w_branch_a",
           "w_branch_b", "w_o", "norm2_g", "w_gate_up", "w_down")


def _as2d(a):
    return a.reshape(-1, a.shape[-1])


def kernel(x, meta_tokens, norm1_g, w_in, rwkv_mu, rwkv_w0, rwkv_w2, rwkv_a0, rwkv_a2, rwkv_g2, rwkv_k_k, rwkv_k_a, rwkv_r_k, rwkv_gn_w, rwkv_gn_b, fox_q_norm_g, fox_k_norm_g, fox_f_bias, w_branch_a, w_branch_b, w_o, norm2_g, w_gate_up, w_down, loss_target, m_meta_tokens, m_norm1_g, m_w_in, m_rwkv_mu, m_rwkv_w0, m_rwkv_w2, m_rwkv_a0, m_rwkv_a2, m_rwkv_g2, m_rwkv_k_k, m_rwkv_k_a, m_rwkv_r_k, m_rwkv_gn_w, m_rwkv_gn_b, m_fox_q_norm_g, m_fox_k_norm_g, m_fox_f_bias, m_w_branch_a, m_w_branch_b, m_w_o, m_norm2_g, m_w_gate_up, m_w_down, v_meta_tokens, v_norm1_g, v_w_in, v_rwkv_mu, v_rwkv_w0, v_rwkv_w2, v_rwkv_a0, v_rwkv_a2, v_rwkv_g2, v_rwkv_k_k, v_rwkv_k_a, v_rwkv_r_k, v_rwkv_gn_w, v_rwkv_gn_b, v_fox_q_norm_g, v_fox_k_norm_g, v_fox_f_bias, v_w_branch_a, v_w_branch_b, v_w_o, v_norm2_g, v_w_gate_up, v_w_down):
    given = dict(locals())
    w = {n: given[n] for n in WEIGHTS}
    assert rwkv_r_k.shape[-1] == HEAD_DIM
    n_meta, seq = meta_tokens.shape[0], x.shape[1]
    tokens = n_meta + seq
    lp = -(-tokens // TOKEN_TILE) * TOKEN_TILE
    mine = _place_index(*(lax.axis_index(a) for a in MESH_AXES))
    x2 = x[0]

    local = {n: _as2d(given[n]) for n in given if n != "x" and n != "loss_target"}
    for n in ("w_in", "m_w_in", "v_w_in"):
        local[n] = jnp.transpose(given[n][0])
    blocks = {n: local[n].astype(F32 if n == "meta_tokens" else BF16) for n in SHARDED}
    first = ("meta_tokens", "rwkv_w2", "rwkv_a2", "rwkv_g2")
    started = {n: _spread_start(blocks[n], False, "gather_start_" + n) for n in first}
    zero = sum(started[n][4][0, 0] for n in first)

    def gathered(n, after):
        own, got = _spread_wait(started[n][:4], after, False, "gather_wait_" + n)
        return _with_own_slot(got, own, mine)

    sm = {n: _as2d(w[n]) for n in SMALL}
    small_mix = tuple(sm[n] for n in SMALL[1:-1])
    n1 = sm["norm1_g"] + zero
    rw, fw = w_branch_a.shape[-2], w_branch_b.shape[-2]
    rcols = 3 * rw + rwkv_w2.shape[-2] + rwkv_a2.shape[-2] + rwkv_g2.shape[-2]
    fcols = 3 * fw + fw // HEAD_DIM
    same = lambda s: (s,)

    meta, un_meta = jax.vjp(_cols_from_slots, gathered("meta_tokens", x2))
    (h0, xn), vjp_embed = jax.vjp(lambda m, xs, g: _stage_embed(m, xs, g, lp), meta, x2, n1)
    in_slots = _all_gather_call(blocks["w_in"])
    later = [n for n in SHARDED if n not in first and n != "w_in"]
    started.update({n: _spread_start(blocks[n], False, "gather_start_" + n, after=in_slots) for n in later})
    w_cat, un_in = jax.vjp(lambda s: _in_proj_layout(s, rcols, fcols), in_slots)
    xn_b = xn.astype(BF16)
    proj = _matmul(xn_b, w_cat, tb=True, name="in_proj", after=sum(started[n][4] for n in later))
    w2, a2, g2 = (gathered(n, xn) for n in ("rwkv_w2", "rwkv_a2", "rwkv_g2"))
    (y_a, y_b, gates), vjp_mix = jax.vjp(lambda p, s, a, b, c: _stage_mix(p, s, a, b, c, rw, fw), proj, small_mix, w2, a2, g2)
    w_a, w_b = gathered("w_branch_a", y_a), gathered("w_branch_b", y_a)
    w_o_full, un_wo = jax.vjp(_rows_from_slots, gathered("w_o", y_a))
    h1, vjp_merge = jax.vjp(_stage_merge, h0, y_a, y_b, gates, w_a, w_b, w_o_full)
    w_gu = gathered("w_gate_up", h1)
    w_dn, un_dn = jax.vjp(_rows_from_slots, gathered("w_down", h1))
    y, vjp_ffn = jax.vjp(_stage_ffn, h1, sm["norm2_g"], w_gu, w_dn)

    loss_part, dy_real = _loss_call(y[n_meta:tokens], loss_target[0])
    dy = jnp.pad(dy_real, ((n_meta, lp - tokens), (0, 0)))
    loss = lax.psum(loss_part[0, 0], MESH_AXES)

    sent = {}

    def send_grad(n, dmat, unlayout):
        sent[n] = _spread_start(unlayout(dmat)[0], True, "grad_start_" + n)
        return sent[n][4][0, 0]

    d_h1, d_n2, d_wgu, d_wdn = vjp_ffn(dy)
    behind = send_grad("w_gate_up", d_wgu, same) + send_grad("w_down", d_wdn, un_dn)
    d_h0, d_ya, d_yb, d_gates, d_wa, d_wb, d_wo = vjp_merge(d_h1 + behind)
    behind = send_grad("w_o", d_wo, un_wo) + send_grad("w_branch_a", d_wa, same) + send_grad("w_branch_b", d_wb, same)
    d_proj, d_small_mix, d_w2, d_a2, d_g2 = vjp_mix((d_ya + behind, d_yb, d_gates))
    dproj_b = d_proj.astype(BF16)
    d_wcat = _matmul(dproj_b, xn_b, ta=True, out_dtype=BF16, name="in_proj_dw")
    send_grad("w_in", d_wcat, un_in)
    d_xn = _matmul(dproj_b, w_cat, out_dtype=F32, name="in_proj_dx", after=sent["w_in"][4])
    send_grad("rwkv_w2", d_w2, same)
    send_grad("rwkv_a2", d_a2, same)
    send_grad("rwkv_g2", d_g2, same)
    d_meta, g_x, d_n1 = vjp_embed((d_h0, d_xn))
    send_grad("meta_tokens", d_meta, un_meta)

    grads = dict(zip(SMALL, _sum_over_devices((d_n1, *d_small_mix, d_n2))))
    grads = {n: g.reshape(w[n].shape) for n, g in grads.items()}

    delta, new_m, new_v = {}, {}, {}
    after = g_x
    for n in ("w_gate_up", "w_down", "w_o", "w_branch_a", "w_branch_b", "rwkv_g2", "rwkv_a2", "rwkv_w2", "meta_tokens", "w_in"):
        src, got = _spread_wait(sent[n][:4], after, True, "grad_wait_" + n)
        g = _sum_slots_call(_with_own_slot(got, lax.dynamic_index_in_dim(src, mine, 0, keepdims=False), mine))
        d_, m_, v_ = _adamw_call(local[n], g, local["m_" + n], local["v_" + n])
        back = (lambda t: jnp.transpose(t)[None]) if n == "w_in" else (lambda t: t.reshape(w[n].shape))
        grads[n], delta[n], new_m[n], new_v[n] = (back(t) for t in (g, d_, m_, v_))
        after = m_
    n_small = sum(w[n].size for n in SMALL)
    width = -(-n_small // 1024) * 1024
    packs = [_pack([src[n] if p == "" else given[p + n] for n in SMALL], width)
             for p, src in (("", w), ("", grads), ("m_", None), ("v_", None))]
    like = [w[n] for n in SMALL]
    for out, packed in zip((delta, new_m, new_v), _adamw_call(*packs)):
        out.update(dict(zip(SMALL, _unpack(packed, like))))

    return (loss, g_x[None], *[grads[n] for n in WEIGHTS], *[delta[n] for n in WEIGHTS],
            *[new_m[n] for n in WEIGHTS], *[new_v[n] for n in WEIGHTS])
```

```python
import functools

import jax
import jax.numpy as jnp
from jax import lax
from jax.experimental import pallas as pl
from jax.experimental.pallas import tpu as pltpu

F32 = jnp.float32
BF16 = jnp.bfloat16

N_DEV = 8
MESH_AXES = ("x", "y", "c")
HEAD_DIM = 64
TOKEN_TILE = 128
WKV_CHUNK = 64
WKV_PAIRS_PER_STEP = 4
PAIR = 2 * HEAD_DIM
ATTN_BLOCK = 128
ATTN_BLOCK_BIG = 384
RMS_EPS = 1e-6
GN_EPS = 64e-5
L2_FLOOR = 1e-12
NEG_BIG = -1e30
ADAM_LR, ADAM_B1, ADAM_B2, ADAM_EPS, ADAM_WD, ADAM_STEP = 0.001, 0.9, 0.999, 1e-08, 0.01, 10
VMEM_BYTES_V7X = 64 * 1024 * 1024
VMEM_LIMIT_CAP = 56 * 1024 * 1024
VMEM_LIMIT_FLOOR = 32 * 1024 * 1024
MATMUL_VMEM_BUDGET = 36 * 1024 * 1024
GRID_STEP_BYTES = 1024 * 1024
ACC_BYTES_PER_HBM_BYTE = 6


def _vmem_limit(estimate_bytes):
    return int(min(max(estimate_bytes * 5 // 4, VMEM_LIMIT_FLOOR), VMEM_LIMIT_CAP))


def _pick(dim, cands):
    for c in cands:
        if dim % c == 0:
            return c
    return dim


def _row_tile(rows, width, itemsize=4, budget=2 * 1024 * 1024):
    for c in (1408, 1024, 704, 512, 384, 256, 128, 64, 32, 16, 8):
        if rows % c == 0 and c * width * itemsize <= budget:
            return c
    return rows


def _row_tile_ragged(rows, width, itemsize=4, budget=2 * 1024 * 1024):
    tile = _row_tile(rows, width, itemsize, budget)
    if tile * width * itemsize <= budget or rows < 16:
        return tile
    padded = -(-rows // 16) * 16
    for c in (1408, 1024, 704, 512, 384, 336, 256, 192, 128, 96, 64, 48, 32, 16):
        if padded % c == 0 and c * width * itemsize <= budget:
            return c
    return tile


def _dg(a, b, ta, tb):
    dims = (((0 if ta else 1,), (1 if tb else 0,)), ((), ()))
    return lax.dot_general(a, b, dims, preferred_element_type=F32)


def _split(x, n):
    parts = []
    for _ in range(n):
        h = x.astype(BF16)
        parts.append(h)
        x = x - h.astype(F32)
    return parts


def _mm(a, b, ta=False, tb=False):
    return _dg(a.astype(BF16), b.astype(BF16), ta, tb)


def _matmul(a, b, ta=False, tb=False, out_dtype=F32, name="matmul", after=None, b_slots=False, out_slots=0):
    if ta:
        kdim, m = a.shape
    else:
        m, kdim = a.shape
    if b_slots:
        n_slots, brows, bcols = b.shape
        n, k2 = (brows, n_slots * bcols) if tb else (n_slots * bcols, brows)
    elif tb:
        n, k2 = b.shape
    else:
        k2, n = b.shape
    assert kdim == k2, (a.shape, b.shape, ta, tb)
    sa, sb, so = a.dtype.itemsize, b.dtype.itemsize, jnp.dtype(out_dtype).itemsize
    n_unit = bcols if (b_slots and not tb) else (n // out_slots if out_slots else n)
    k_unit = bcols if (b_slots and tb) else kdim
    tm, tn, tk = _matmul_tiles(m, n, kdim, ta, sa, sb, so, n_unit, k_unit)
    nk = kdim // tk

    order = () if after is None else (after,)

    def body(a_ref, b_ref, *rest):
        o_ref, acc = rest[len(order)], rest[len(order) + 1:]
        part = _dg(a_ref[...].astype(BF16), b_ref[...].astype(BF16), ta, tb)
        if nk == 1:
            o_ref[...] = part.astype(o_ref.dtype)
            return
        kk = pl.program_id(2)

        @pl.when(kk == 0)
        def _():
            acc[0][...] = part

        @pl.when(kk > 0)
        def _():
            acc[0][...] += part

        @pl.when(kk == nk - 1)
        def _():
            o_ref[...] = acc[0][...].astype(o_ref.dtype)

    a_spec = pl.BlockSpec((tk, tm), lambda i, j, k: (k, i)) if ta else pl.BlockSpec((tm, tk), lambda i, j, k: (i, k))
    if b_slots and tb:
        per = bcols // tk
        b_spec = pl.BlockSpec((None, tn, tk), lambda i, j, k: (k // per, j, k % per))
    elif b_slots:
        per = bcols // tn
        b_spec = pl.BlockSpec((None, tk, tn), lambda i, j, k: (j // per, k, j % per))
    else:
        b_spec = pl.BlockSpec((tn, tk), lambda i, j, k: (j, k)) if tb else pl.BlockSpec((tk, tn), lambda i, j, k: (k, j))
    if out_slots:
        per_out = n // out_slots // tn
        out_spec = pl.BlockSpec((None, tm, tn), lambda i, j, k: (j // per_out, i, j % per_out))
        out_shape = jax.ShapeDtypeStruct((out_slots, m, n // out_slots), out_dtype)
    else:
        out_spec = pl.BlockSpec((tm, tn), lambda i, j, k: (i, j))
        out_shape = jax.ShapeDtypeStruct((m, n), out_dtype)
    return pl.pallas_call(
        body, name=name,
        grid=(m // tm, n // tn, nk),
        in_specs=[a_spec, b_spec] + [pl.BlockSpec(memory_space=pl.ANY)] * len(order),
        out_specs=out_spec,
        out_shape=out_shape,
        scratch_shapes=[pltpu.VMEM((tm, tn), F32)] if nk > 1 else [],
        compiler_params=pltpu.CompilerParams(dimension_semantics=("parallel", "parallel", "arbitrary"),
                                             vmem_limit_bytes=_vmem_limit(_matmul_vmem(tm, tn, tk, nk, sa, sb, so))),
    )(a, b, *order)


def _matmul_vmem(tm, tn, tk, nk, sa, sb, so):
    return 2 * (tm * tk * sa + tk * tn * sb + tm * tn * so) + tm * tn * 4 + (tm * tn * 4 if nk > 1 else 0)


def _matmul_tiles(m, n, kdim, ta, sa, sb, so, n_unit, k_unit):
    lane = (2816, 2176, 2048, 1408, 1024, 640, 512, 384, 256, 128)
    sublane = (2816, 2176, 2048, 1408, 1024, 704, 512, 384, 256, 128)
    divs = lambda dim, cands: [c for c in cands if dim % c == 0] or [dim]
    best = None
    for tm in divs(m, lane if ta else sublane):
        for tn in divs(n_unit, lane):
            for tk in divs(k_unit, sublane if ta else lane) + ([kdim] if (kdim <= 2048 and k_unit == kdim) else []):
                nk, nm, nn = kdim // tk, m // tm, n // tn
                if _matmul_vmem(tm, tn, tk, nk, sa, sb, so) > MATMUL_VMEM_BUDGET:
                    continue
                a_bytes = m * kdim * sa * (nn if nk > 1 else 1)
                b_bytes = kdim * n * sb * (1 if (nk == 1 and nn == 1) else nm)
                acc_bytes = m * n * 4 * 3 * nk // ACC_BYTES_PER_HBM_BYTE if nk > 1 else 0
                cost = a_bytes + b_bytes + m * n * so + acc_bytes + nm * nn * nk * GRID_STEP_BYTES
                if best is None or cost < best[0]:
                    best = (cost, tm, tn, tk)
    return best[1:]


@jax.custom_vjp
def dense(x, w):
    return _matmul(x.astype(BF16), w, name="dense_fwd")


def _dense_fwd(x, w):
    assert x.dtype == F32
    xb = x.astype(BF16)
    return _matmul(xb, w, name="dense_fwd"), (xb, w)


def _dense_bwd(res, dy):
    xb, w = res
    dyb = dy.astype(BF16)
    dx = _matmul(dyb, w, tb=True, out_dtype=F32, name="dense_dx")
    dw = _matmul(xb, dyb, ta=True, out_dtype=w.dtype, name="dense_dw")
    return dx, dw


dense.defvjp(_dense_fwd, _dense_bwd)


@jax.custom_vjp
def dense_cols(x, w_slots):
    return _matmul(x.astype(BF16), w_slots, b_slots=True, name="dense_cols_fwd")


def _dense_cols_fwd(x, w_slots):
    assert x.dtype == F32
    xb = x.astype(BF16)
    return _matmul(xb, w_slots, b_slots=True, name="dense_cols_fwd"), (xb, w_slots)


def _dense_cols_bwd(res, dy):
    xb, w_slots = res
    dyb = dy.astype(BF16)
    dx = _matmul(dyb, w_slots, tb=True, b_slots=True, out_dtype=F32, name="dense_cols_dx")
    dw = _matmul(xb, dyb, ta=True, out_slots=w_slots.shape[0], out_dtype=w_slots.dtype, name="dense_cols_dw")
    return dx, dw


dense_cols.defvjp(_dense_cols_fwd, _dense_cols_bwd)


def _rms_fwd_call(x, g):
    rows, d = x.shape
    tr = _row_tile(rows, d)

    def body(x_ref, g_ref, y_ref):
        xv = x_ref[...]
        rstd = lax.rsqrt(jnp.mean(xv * xv, axis=1, keepdims=True) + RMS_EPS)
        y_ref[...] = (xv * rstd) * g_ref[...]

    return pl.pallas_call(
        body, name="rms_fwd", grid=(rows // tr,),
        in_specs=[pl.BlockSpec((tr, d), lambda i: (i, 0)), pl.BlockSpec((1, d), lambda i: (0, 0))],
        out_specs=pl.BlockSpec((tr, d), lambda i: (i, 0)),
        out_shape=jax.ShapeDtypeStruct((rows, d), F32),
        compiler_params=pltpu.CompilerParams(dimension_semantics=("parallel",)),
    )(x, g)


def _rms_bwd_call(x, g, dy):
    rows, d = x.shape
    tr = _row_tile(rows, d)

    def body(x_ref, g_ref, dy_ref, dx_ref, dg_ref):
        @pl.when(pl.program_id(0) == 0)
        def _():
            dg_ref[...] = jnp.zeros_like(dg_ref)

        xv = x_ref[...]
        dyv = dy_ref[...]
        rstd = lax.rsqrt(jnp.mean(xv * xv, axis=1, keepdims=True) + RMS_EPS)
        xhat = xv * rstd
        dxhat = dyv * g_ref[...]
        dx_ref[...] = rstd * (dxhat - xhat * jnp.mean(dxhat * xhat, axis=1, keepdims=True))
        dg_ref[...] += jnp.sum(dyv * xhat, axis=0, keepdims=True)

    return pl.pallas_call(
        body, name="rms_bwd", grid=(rows // tr,),
        in_specs=[pl.BlockSpec((tr, d), lambda i: (i, 0)), pl.BlockSpec((1, d), lambda i: (0, 0)),
                  pl.BlockSpec((tr, d), lambda i: (i, 0))],
        out_specs=[pl.BlockSpec((tr, d), lambda i: (i, 0)), pl.BlockSpec((1, d), lambda i: (0, 0))],
        out_shape=[jax.ShapeDtypeStruct((rows, d), F32), jax.ShapeDtypeStruct((1, d), F32)],
        compiler_params=pltpu.CompilerParams(dimension_semantics=("arbitrary",)),
    )(x, g, dy)


@jax.custom_vjp
def rmsnorm(x, g):
    return _rms_fwd_call(x, g)


rmsnorm.defvjp(lambda x, g: (_rms_fwd_call(x, g), (x, g)), lambda res, dy: tuple(_rms_bwd_call(res[0], res[1], dy)))


def _bcast_call(x, p, mul):
    rows, d = x.shape
    tr = _row_tile(rows, d)

    def body(x_ref, p_ref, y_ref):
        y_ref[...] = x_ref[...] * p_ref[...] if mul else x_ref[...] + p_ref[...]

    return pl.pallas_call(
        body, name="bcast_mul" if mul else "bcast_add", grid=(rows // tr,),
        in_specs=[pl.BlockSpec((tr, d), lambda i: (i, 0)), pl.BlockSpec((1, d), lambda i: (0, 0))],
        out_specs=pl.BlockSpec((tr, d), lambda i: (i, 0)),
        out_shape=jax.ShapeDtypeStruct((rows, d), F32),
        compiler_params=pltpu.CompilerParams(dimension_semantics=("parallel",)),
    )(x, p)


def _colsum_call(a, b=None):
    rows, d = a.shape
    tr = _row_tile(rows, d)
    ops = (a,) if b is None else (a, b)

    def body(*refs):
        o_ref = refs[-1]

        @pl.when(pl.program_id(0) == 0)
        def _():
            o_ref[...] = jnp.zeros_like(o_ref)

        v = refs[0][...] if b is None else refs[0][...] * refs[1][...]
        o_ref[...] += jnp.sum(v, axis=0, keepdims=True)

    return pl.pallas_call(
        body, name="colsum", grid=(rows // tr,),
        in_specs=[pl.BlockSpec((tr, d), lambda i: (i, 0))] * len(ops),
        out_specs=pl.BlockSpec((1, d), lambda i: (0, 0)),
        out_shape=jax.ShapeDtypeStruct((1, d), F32),
        compiler_params=pltpu.CompilerParams(dimension_semantics=("arbitrary",)),
    )(*ops)


@jax.custom_vjp
def bmul(x, p):
    return _bcast_call(x, p, True)


bmul.defvjp(lambda x, p: (_bcast_call(x, p, True), (x, p)),
            lambda res, dy: (_bcast_call(dy, res[1], True), _colsum_call(dy, res[0])))


@jax.custom_vjp
def badd(x, p):
    return _bcast_call(x, p, False)


badd.defvjp(lambda x, p: (_bcast_call(x, p, False), None), lambda res, dy: (dy, _colsum_call(dy)))


def _heads(width):
    return [slice(h * HEAD_DIM, (h + 1) * HEAD_DIM) for h in range(width // HEAD_DIM)]


def _head_rms_fwd_call(x, g):
    rows, w = x.shape
    tr = _row_tile(rows, w, budget=1024 * 1024)

    def body(x_ref, g_ref, y_ref):
        for sl in _heads(w):
            xv = x_ref[:, sl]
            rstd = lax.rsqrt(jnp.mean(xv * xv, axis=1, keepdims=True) + RMS_EPS)
            y_ref[:, sl] = (xv * rstd) * g_ref[...]

    return pl.pallas_call(
        body, name="head_rms_fwd", grid=(rows // tr,),
        in_specs=[pl.BlockSpec((tr, w), lambda i: (i, 0)), pl.BlockSpec((1, HEAD_DIM), lambda i: (0, 0))],
        out_specs=pl.BlockSpec((tr, w), lambda i: (i, 0)),
        out_shape=jax.ShapeDtypeStruct((rows, w), F32),
        compiler_params=pltpu.CompilerParams(dimension_semantics=("parallel",)),
    )(x, g)


def _head_rms_bwd_call(x, g, dy):
    rows, w = x.shape
    tr = _row_tile(rows, w, budget=1024 * 1024)

    def body(x_ref, g_ref, dy_ref, dx_ref, dg_ref):
        @pl.when(pl.program_id(0) == 0)
        def _():
            dg_ref[...] = jnp.zeros_like(dg_ref)

        dg = jnp.zeros((1, HEAD_DIM), F32)
        for sl in _heads(w):
            xv = x_ref[:, sl]
            dyv = dy_ref[:, sl]
            rstd = lax.rsqrt(jnp.mean(xv * xv, axis=1, keepdims=True) + RMS_EPS)
            xhat = xv * rstd
            dxhat = dyv * g_ref[...]
            dx_ref[:, sl] = rstd * (dxhat - xhat * jnp.mean(dxhat * xhat, axis=1, keepdims=True))
            dg = dg + jnp.sum(dyv * xhat, axis=0, keepdims=True)
        dg_ref[...] += dg

    return pl.pallas_call(
        body, name="head_rms_bwd", grid=(rows // tr,),
        in_specs=[pl.BlockSpec((tr, w), lambda i: (i, 0)), pl.BlockSpec((1, HEAD_DIM), lambda i: (0, 0)),
                  pl.BlockSpec((tr, w), lambda i: (i, 0))],
        out_specs=[pl.BlockSpec((tr, w), lambda i: (i, 0)), pl.BlockSpec((1, HEAD_DIM), lambda i: (0, 0))],
        out_shape=[jax.ShapeDtypeStruct((rows, w), F32), jax.ShapeDtypeStruct((1, HEAD_DIM), F32)],
        compiler_params=pltpu.CompilerParams(dimension_semantics=("arbitrary",)),
    )(x, g, dy)


@jax.custom_vjp
def head_rms(x, g):
    return _head_rms_fwd_call(x, g)


head_rms.defvjp(lambda x, g: (_head_rms_fwd_call(x, g), (x, g)),
                lambda res, dy: tuple(_head_rms_bwd_call(res[0], res[1], dy)))


def _head_l2_call(x, dy=None):
    rows, w = x.shape
    tr = _row_tile(rows, w, budget=1024 * 1024)
    ops = (x,) if dy is None else (x, dy)

    def body(*refs):
        o_ref = refs[-1]
        for sl in _heads(w):
            xv = refs[0][:, sl]
            nrm = jnp.sqrt(jnp.sum(xv * xv, axis=1, keepdims=True))
            live = nrm > L2_FLOOR
            inv = 1.0 / jnp.maximum(nrm, L2_FLOOR)
            y = xv * inv
            if dy is None:
                o_ref[:, sl] = y
            else:
                dyv = refs[1][:, sl]
                proj = jnp.where(live, jnp.sum(dyv * y, axis=1, keepdims=True), 0.0)
                o_ref[:, sl] = (dyv - y * proj) * inv

    return pl.pallas_call(
        body, name="head_l2_fwd" if dy is None else "head_l2_bwd", grid=(rows // tr,),
        in_specs=[pl.BlockSpec((tr, w), lambda i: (i, 0))] * len(ops),
        out_specs=pl.BlockSpec((tr, w), lambda i: (i, 0)),
        out_shape=jax.ShapeDtypeStruct((rows, w), F32),
        compiler_params=pltpu.CompilerParams(dimension_semantics=("parallel",)),
    )(*ops)


@jax.custom_vjp
def head_l2norm(x):
    return _head_l2_call(x)


head_l2norm.defvjp(lambda x: (_head_l2_call(x), x), lambda x, dy: (_head_l2_call(x, dy),))


def _gn_fwd_call(y, r, kf, v, gw, gb, rk):
    rows, w = y.shape
    tr = _row_tile(rows, w, budget=512 * 1024)

    def body(y_ref, r_ref, kf_ref, v_ref, gw_ref, gb_ref, rk_ref, o_ref):
        for sl in _heads(w):
            yv = y_ref[:, sl]
            yc = yv - jnp.mean(yv, axis=1, keepdims=True)
            rstd = lax.rsqrt(jnp.mean(yc * yc, axis=1, keepdims=True) + GN_EPS)
            s = jnp.sum(r_ref[:, sl] * kf_ref[:, sl] * rk_ref[:, sl], axis=1, keepdims=True)
            o_ref[:, sl] = (yc * rstd) * gw_ref[:, sl] + gb_ref[:, sl] + s * v_ref[:, sl]

    tok = pl.BlockSpec((tr, w), lambda i: (i, 0))
    par = pl.BlockSpec((1, w), lambda i: (0, 0))
    return pl.pallas_call(
        body, name="gn_bonus_fwd", grid=(rows // tr,),
        in_specs=[tok] * 4 + [par] * 3, out_specs=tok,
        out_shape=jax.ShapeDtypeStruct((rows, w), F32),
        compiler_params=pltpu.CompilerParams(dimension_semantics=("parallel",)),
    )(y, r, kf, v, gw, gb, rk)


def _gn_bwd_call(y, r, kf, v, gw, gb, rk, do):
    rows, w = y.shape
    tr = _row_tile(rows, w, budget=512 * 1024)

    def body(y_ref, r_ref, kf_ref, v_ref, gw_ref, rk_ref, do_ref,
             dy_ref, dr_ref, dkf_ref, dv_ref, dgw_ref, dgb_ref, drk_ref):
        @pl.when(pl.program_id(0) == 0)
        def _():
            dgw_ref[...] = jnp.zeros_like(dgw_ref)
            dgb_ref[...] = jnp.zeros_like(dgb_ref)
            drk_ref[...] = jnp.zeros_like(drk_ref)

        for sl in _heads(w):
            yv, rv, kv, vv, dov = y_ref[:, sl], r_ref[:, sl], kf_ref[:, sl], v_ref[:, sl], do_ref[:, sl]
            yc = yv - jnp.mean(yv, axis=1, keepdims=True)
            rstd = lax.rsqrt(jnp.mean(yc * yc, axis=1, keepdims=True) + GN_EPS)
            yhat = yc * rstd
            dyhat = dov * gw_ref[:, sl]
            dy_ref[:, sl] = rstd * (dyhat - jnp.mean(dyhat, axis=1, keepdims=True)
                                    - yhat * jnp.mean(dyhat * yhat, axis=1, keepdims=True))
            rkv = rk_ref[:, sl]
            s = jnp.sum(rv * kv * rkv, axis=1, keepdims=True)
            ds = jnp.sum(dov * vv, axis=1, keepdims=True)
            dv_ref[:, sl] = s * dov
            dr_ref[:, sl] = ds * kv * rkv
            dkf_ref[:, sl] = ds * rv * rkv
            dgw_ref[:, sl] += jnp.sum(dov * yhat, axis=0, keepdims=True)
            dgb_ref[:, sl] += jnp.sum(dov, axis=0, keepdims=True)
            drk_ref[:, sl] += jnp.sum(ds * rv * kv, axis=0, keepdims=True)

    tok = pl.BlockSpec((tr, w), lambda i: (i, 0))
    par = pl.BlockSpec((1, w), lambda i: (0, 0))
    tshape = jax.ShapeDtypeStruct((rows, w), F32)
    pshape = jax.ShapeDtypeStruct((1, w), F32)
    return pl.pallas_call(
        body, name="gn_bonus_bwd", grid=(rows // tr,),
        in_specs=[tok] * 4 + [par] * 2 + [tok], out_specs=[tok] * 4 + [par] * 3,
        out_shape=[tshape] * 4 + [pshape] * 3,
        compiler_params=pltpu.CompilerParams(dimension_semantics=("arbitrary",)),
    )(y, r, kf, v, gw, rk, do)


@jax.custom_vjp
def gn_bonus(y, r, kf, v, gw, gb, rk):
    return _gn_fwd_call(y, r, kf, v, gw, gb, rk)


def _gn_bwd(res, do):
    y, r, kf, v, gw, gb, rk = res
    dy, dr, dkf, dv, dgw, dgb, drk = _gn_bwd_call(y, r, kf, v, gw, gb, rk, do)
    return dy, dr, dkf, dv, dgw, dgb, drk


gn_bonus.defvjp(lambda *a: (_gn_fwd_call(*a), a), _gn_bwd)


PREP_ROWS = 128


def _head_sum_matrix(width):
    head = jnp.arange(width) // HEAD_DIM
    return (head[:, None] == head[None, :]).astype(BF16)


def _head_sums(x, ones_bd):
    hi, lo = _split(x, 2)
    return _dg(hi, ones_bd, False, False) + _dg(lo, ones_bd, False, False)


def _prep_segments(rw, lora_w, lora_a, lora_g):
    at = 3 * rw
    seg = {"r": (0, rw), "k": (rw, 2 * rw), "v": (2 * rw, 3 * rw)}
    for name, n in (("wd", lora_w), ("ad", lora_a), ("gd", lora_g)):
        seg[name] = (at, at + _pad128(n))
        at += _pad128(n)
    return seg, at


def _prep_shifted(z_ref, zlast_ref, mu_ref, seg, first_tile):
    lo, hi = seg
    zr = z_ref[:, lo:hi]
    rows = zr.shape[0]
    before = jnp.where(first_tile, 0.0, zlast_ref[7:8, lo:hi])
    row0 = lax.broadcasted_iota(jnp.int32, zr.shape, 0) == 0
    diff = jnp.where(row0, before, pltpu.roll(zr, 1, axis=0)) - zr
    return zr + diff * mu_ref[:, lo:hi], diff


def _prep_forward_values(z_ref, zlast_ref, mu_ref, w0_ref, a0_ref, kk_ref, ka_ref, w2_ref, a2_ref, g2_ref, bd_ref, segs, first_tile):
    z = {n: _prep_shifted(z_ref, zlast_ref, mu_ref, segs[n], first_tile) for n in segs}
    r, k, v, wd, ad, gd = (z[n][0] for n in ("r", "k", "v", "wd", "ad", "gd"))
    twd = jnp.tanh(wd)
    pw = _mm(twd, w2_ref[...]) + w0_ref[...]
    lw = -jnp.exp(-(jnp.maximum(-pw, 0.0) + jnp.log(1.0 + jnp.exp(-jnp.abs(pw)))) - 0.5)
    a_sig = 1.0 / (1.0 + jnp.exp(-(_mm(ad, a2_ref[...]) + a0_ref[...])))
    sg = 1.0 / (1.0 + jnp.exp(-gd))
    kx = k * kk_ref[...]
    nrm = jnp.sqrt(_head_sums(kx * kx, bd_ref[...]))
    inv = 1.0 / jnp.maximum(nrm, L2_FLOOR)
    return dict(z=z, r=r, k=k, v=v, twd=twd, pw=pw, lw=lw, a_sig=a_sig, sg=sg, ad=ad, kk=kx * inv, inv=inv, live=nrm > L2_FLOOR)


def _prep_specs(tokens, rpad, rw, w2, a2, g2):
    tr = PREP_ROWS
    tile = lambda w: pl.BlockSpec((tr, w), lambda i: (i, 0))
    before = pl.BlockSpec((8, rpad), lambda i: (jnp.maximum(i * (tr // 8) - 1, 0), 0))
    whole = lambda a: pl.BlockSpec(a.shape, lambda i: (0, 0))
    par = pl.BlockSpec((1, rw), lambda i: (0, 0))
    return tile, before, whole, par, pl.BlockSpec((1, rpad), lambda i: (0, 0))


def _prep_fwd_call(zr, mu, w0, a0, k_k, k_a, w2, a2, g2):
    tokens, rpad = zr.shape
    rw = w0.shape[1]
    segs, _ = _prep_segments(rw, w2.shape[0], a2.shape[0], g2.shape[0])
    tile, before, whole, par, mu_spec = _prep_specs(tokens, rpad, rw, w2, a2, g2)
    bd = _head_sum_matrix(rw)

    def body(z_ref, zlast_ref, mu_ref, w0_ref, a0_ref, kk_ref, ka_ref, w2_ref, a2_ref, g2_ref, bd_ref,
             r_ref, lw_ref, kf_ref, v_ref, na_ref, b_ref, g_ref):
        f = _prep_forward_values(z_ref, zlast_ref, mu_ref, w0_ref, a0_ref, kk_ref, ka_ref, w2_ref, a2_ref, g2_ref, bd_ref,
                                 segs, pl.program_id(0) == 0)
        r_ref[...] = f["r"]
        v_ref[...] = f["v"]
        lw_ref[...] = f["lw"]
        kf_ref[...] = f["k"] * (1.0 + (f["a_sig"] - 1.0) * ka_ref[...])
        na_ref[...] = -f["kk"]
        b_ref[...] = f["kk"] * f["a_sig"]
        g_ref[...] = _mm(f["sg"], g2_ref[...])

    shape = jax.ShapeDtypeStruct((tokens, rw), F32)
    return pl.pallas_call(
        body, name="rwkv_prep_fwd", grid=(tokens // PREP_ROWS,),
        in_specs=[tile(rpad), before, mu_spec, par, par, par, par, whole(w2), whole(a2), whole(g2), whole(bd)],
        out_specs=[tile(rw)] * 7, out_shape=[shape] * 7,
        compiler_params=pltpu.CompilerParams(dimension_semantics=("parallel",), vmem_limit_bytes=VMEM_LIMIT_CAP),
    )(zr, zr, mu, w0, a0, k_k, k_a, w2, a2, g2, bd)


def _prep_bwd_call(zr, mu, w0, a0, k_k, k_a, w2, a2, g2, cts):
    tokens, rpad = zr.shape
    rw = w0.shape[1]
    segs, _ = _prep_segments(rw, w2.shape[0], a2.shape[0], g2.shape[0])
    tile, before, whole, par, mu_spec = _prep_specs(tokens, rpad, rw, w2, a2, g2)
    bd = _head_sum_matrix(rw)
    nt = tokens // PREP_ROWS
    rev = lambda spec: pl.BlockSpec(spec.block_shape, lambda i, f=spec.index_map: f(nt - 1 - i))

    def body(z_ref, zlast_ref, mu_ref, w0_ref, a0_ref, kk_ref, ka_ref, w2_ref, a2_ref, g2_ref, bd_ref,
             dr_ref, dlw_ref, dkf_ref, dv_ref, dna_ref, db_ref, dg_ref,
             dz_ref, dmu_ref, dw0_ref, da0_ref, dkk_ref, dka_ref, dw2_ref, da2_ref, dg2_ref, carry):
        step = pl.program_id(0)

        @pl.when(step == 0)
        def _():
            for ref in (dmu_ref, dw0_ref, da0_ref, dkk_ref, dka_ref, dw2_ref, da2_ref, dg2_ref, carry):
                ref[...] = jnp.zeros_like(ref)

        f = _prep_forward_values(z_ref, zlast_ref, mu_ref, w0_ref, a0_ref, kk_ref, ka_ref, w2_ref, a2_ref, g2_ref, bd_ref,
                                 segs, step == nt - 1)
        k, kk, a_sig, sg, twd = f["k"], f["kk"], f["a_sig"], f["sg"], f["twd"]
        colsum = lambda t: jnp.sum(t, axis=0, keepdims=True)
        dkf, db, dg = dkf_ref[...], db_ref[...], dg_ref[...]
        ka = ka_ref[...]
        dgd = _mm(dg, g2_ref[...], tb=True) * sg * (1.0 - sg)
        dg2_ref[...] += _mm(sg, dg, ta=True)
        dkk = db * a_sig - dna_ref[...]
        da_sig = db * kk + dkf * k * ka
        dk = dkf * (1.0 + (a_sig - 1.0) * ka)
        dka_ref[...] += colsum(dkf * k * (a_sig - 1.0))
        along = jnp.where(f["live"], _head_sums(dkk * kk, bd_ref[...]), 0.0)
        dkx = (dkk - kk * along) * f["inv"]
        dk = dk + dkx * kk_ref[...]
        dkk_ref[...] += colsum(dkx * k)
        dpa = da_sig * a_sig * (1.0 - a_sig)
        da0_ref[...] += colsum(dpa)
        dad = _mm(dpa, a2_ref[...], tb=True)
        da2_ref[...] += _mm(f["ad"], dpa, ta=True)
        dpw = dlw_ref[...] * f["lw"] / (1.0 + jnp.exp(f["pw"]))
        dw0_ref[...] += colsum(dpw)
        dwd = _mm(dpw, w2_ref[...], tb=True) * (1.0 - twd * twd)
        dw2_ref[...] += _mm(twd, dpw, ta=True)
        rows = PREP_ROWS
        last = lax.broadcasted_iota(jnp.int32, (rows, 1), 0) == rows - 1
        for name, dz in (("r", dr_ref[...]), ("k", dk), ("v", dv_ref[...]), ("wd", dwd), ("ad", dad), ("gd", dgd)):
            lo, hi = segs[name]
            mu_s = mu_ref[:, lo:hi]
            dmu_ref[:, lo:hi] += colsum(dz * f["z"][name][1])
            later = dz * mu_s
            dz_ref[:, lo:hi] = dz * (1.0 - mu_s) + jnp.where(last, carry[:, lo:hi], pltpu.roll(later, rows - 1, axis=0))
            carry[:, lo:hi] = later[0:1, :]

    tok = jax.ShapeDtypeStruct((tokens, rw), F32)
    acc = lambda a: jax.ShapeDtypeStruct(a.shape, F32)
    return pl.pallas_call(
        body, name="rwkv_prep_bwd", grid=(nt,),
        in_specs=[rev(tile(rpad)), rev(before), mu_spec, par, par, par, par, whole(w2), whole(a2), whole(g2), whole(bd)]
                 + [rev(tile(rw))] * 7,
        out_specs=[rev(tile(rpad)), mu_spec, par, par, par, par, whole(w2), whole(a2), whole(g2)],
        out_shape=[jax.ShapeDtypeStruct((tokens, rpad), F32), acc(mu), acc(w0), acc(a0), acc(k_k), acc(k_a), acc(w2), acc(a2), acc(g2)],
        scratch_shapes=[pltpu.VMEM((1, rpad), F32)],
        compiler_params=pltpu.CompilerParams(dimension_semantics=("arbitrary",), vmem_limit_bytes=VMEM_LIMIT_CAP),
    )(zr, zr, mu, w0, a0, k_k, k_a, w2, a2, g2, bd, *cts)


@jax.custom_vjp
def rwkv_prep(zr, mu, w0, a0, k_k, k_a, w2, a2, g2):
    return tuple(_prep_fwd_call(zr, mu, w0, a0, k_k, k_a, w2, a2, g2))


def _rwkv_prep_bwd(res, cts):
    zr, mu, w0, a0, k_k, k_a, w2, a2, g2 = res
    dz, dmu, dw0, da0, dkk, dka, dw2, da2, dg2 = _prep_bwd_call(*res, cts)
    return dz, dmu, dw0, da0, dkk, dka, dw2.astype(w2.dtype), da2.astype(a2.dtype), dg2.astype(g2.dtype)


rwkv_prep.defvjp(lambda *a: (tuple(_prep_fwd_call(*a)), a), _rwkv_prep_bwd)


def _pair_masks(rows):
    lane = lax.broadcasted_iota(jnp.int32, (rows, PAIR), 1)
    return lane < HEAD_DIM, lane >= HEAD_DIM


def _bd(x):
    m0, m1 = _pair_masks(x.shape[0])
    return jnp.concatenate([jnp.where(m0, x, 0.0), jnp.where(m1, x, 0.0)], axis=0)


def _unbd(m, c):
    return jnp.where(_pair_masks(c)[0], m[:c], m[c:])


def _pair_a(l2, r2):
    return _mm(l2, _bd(r2), tb=True)


def _pair_mul(p2, x2):
    return _mm(p2, _bd(x2))


def _pair_mul_t(p2, x2):
    return _unbd(_mm(p2, x2, ta=True), p2.shape[0])


def _block_diag_mask():
    row = lax.broadcasted_iota(jnp.int32, (PAIR, PAIR), 0)
    lane = lax.broadcasted_iota(jnp.int32, (PAIR, PAIR), 1)
    return (row < HEAD_DIM) == (lane < HEAD_DIM), row == lane


def _wkv_pair_common(r, lw, k, a, b):
    c = r[0].shape[0]
    pairs = range(len(r))
    i = lax.broadcasted_iota(jnp.int32, (c, PAIR), 0)
    j = lax.broadcasted_iota(jnp.int32, (c, PAIR), 1) % c
    strict, incl = i > j, i >= j
    ti = lax.broadcasted_iota(jnp.int32, (c, c), 0)
    tj = lax.broadcasted_iota(jnp.int32, (c, c), 1)
    tri = jnp.where(ti >= tj, 1.0, 0.0).astype(BF16)
    lc = [sum(_dg(tri, part, False, False) for part in _split(lw[p], 3)) for p in pairs]
    lend = [lc[p][c - 1:c, :] for p in pairs]
    rt = [r[p] * jnp.exp(lc[p]) for p in pairs]
    at = [a[p] * jnp.exp(lc[p] - lw[p]) for p in pairs]
    pinv = [jnp.exp(-lc[p]) for p in pairs]
    kt = [k[p] * pinv[p] for p in pairs]
    bt = [b[p] * pinv[p] for p in pairs]
    e = [jnp.exp(lend[p] - lc[p]) for p in pairs]
    ktp = [k[p] * e[p] for p in pairs]
    btp = [b[p] * e[p] for p in pairs]
    a_ab = [jnp.where(strict, _pair_a(at[p], bt[p]), 0.0) for p in pairs]
    a_ak = [jnp.where(strict, _pair_a(at[p], kt[p]), 0.0) for p in pairs]
    a_rb = [jnp.where(incl, _pair_a(rt[p], bt[p]), 0.0) for p in pairs]
    a_rk = [jnp.where(incl, _pair_a(rt[p], kt[p]), 0.0) for p in pairs]
    t = [jnp.where(i == j, 1.0, 0.0) + a_ab[p] for p in pairs]
    xp = a_ab
    n = 2
    while n < c:
        xp = [_pair_mul(xp[p], xp[p]) for p in pairs]
        t = [t[p] + _pair_mul(t[p], xp[p]) for p in pairs]
        n *= 2
    bdm, eye = _block_diag_mask()
    pend_col = [jnp.sum(jnp.where(eye, jnp.exp(lend[p]), 0.0), axis=1, keepdims=True) for p in pairs]
    return dict(rt=rt, at=at, kt=kt, bt=bt, ktp=ktp, btp=btp, a_ak=a_ak, a_rb=a_rb, a_rk=a_rk, t=t,
                pend_col=pend_col, lend=lend, lc=lc, strict=strict, incl=incl, tri=tri, bdm=bdm)


def _wkv_group(width):
    npair = width // PAIR
    g = min(WKV_PAIRS_PER_STEP, npair)
    assert npair % g == 0
    return npair, g


def _wkv_fwd_call(r, lw, k, v, a, b):
    tokens, width = r.shape
    c = WKV_CHUNK
    nc = tokens // c
    npair, g = _wkv_group(width)

    def body(r_ref, lw_ref, k_ref, v_ref, a_ref, b_ref, y_ref, s_ref, st):
        @pl.when(pl.program_id(1) == 0)
        def _():
            st[...] = jnp.zeros_like(st)

        pairs = range(g)
        rv, lwv, kv, vv, av, bv = ([ref[:, p * PAIR:(p + 1) * PAIR] for p in pairs]
                                   for ref in (r_ref, lw_ref, k_ref, v_ref, a_ref, b_ref))
        s0 = [st[p] for p in pairs]
        q = _wkv_pair_common(rv, lwv, kv, av, bv)
        w1 = [_mm(q["at"][p], s0[p]) + _pair_mul(q["a_ak"][p], vv[p]) for p in pairs]
        u = [_pair_mul(q["t"][p], w1[p]) for p in pairs]
        y = [_mm(q["rt"][p], s0[p]) + _pair_mul(q["a_rb"][p], u[p]) + _pair_mul(q["a_rk"][p], vv[p]) for p in pairs]
        grow = [_mm(jnp.concatenate([q["btp"][p], q["ktp"][p]], axis=0), jnp.concatenate([u[p], vv[p]], axis=0), ta=True)
                for p in pairs]
        for p in pairs:
            y_ref[:, p * PAIR:(p + 1) * PAIR] = y[p]
            s_ref[0, p] = s0[p]
            st[p] = q["pend_col"][p] * s0[p] + jnp.where(q["bdm"], grow[p], 0.0)

    tok = pl.BlockSpec((c, g * PAIR), lambda gi, ci: (ci, gi))
    return pl.pallas_call(
        body, name="wkv_fwd", grid=(npair // g, nc),
        in_specs=[tok] * 6,
        out_specs=[tok, pl.BlockSpec((1, g, PAIR, PAIR), lambda gi, ci: (ci, gi, 0, 0))],
        out_shape=[jax.ShapeDtypeStruct((tokens, width), F32), jax.ShapeDtypeStruct((nc, npair, PAIR, PAIR), F32)],
        scratch_shapes=[pltpu.VMEM((g, PAIR, PAIR), F32)],
        compiler_params=pltpu.CompilerParams(dimension_semantics=("parallel", "arbitrary")),
    )(r, lw, k, v, a, b)


def _wkv_bwd_call(r, lw, k, v, a, b, s, dy):
    tokens, width = r.shape
    c = WKV_CHUNK
    nc = tokens // c
    npair, g = _wkv_group(width)

    def body(r_ref, lw_ref, k_ref, v_ref, a_ref, b_ref, s_ref, dy_ref,
             dr_ref, dlw_ref, dk_ref, dv_ref, da_ref, db_ref, dst):
        @pl.when(pl.program_id(1) == 0)
        def _():
            dst[...] = jnp.zeros_like(dst)

        pairs = range(g)
        rv, lwv, kv, vv, av, bv, dyv = ([ref[:, p * PAIR:(p + 1) * PAIR] for p in pairs]
                                        for ref in (r_ref, lw_ref, k_ref, v_ref, a_ref, b_ref, dy_ref))
        s0 = [s_ref[0, p] for p in pairs]
        dsc = [dst[p] for p in pairs]
        q = _wkv_pair_common(rv, lwv, kv, av, bv)
        rt, at, kt, bt, ktp, btp, t = (q[n] for n in ("rt", "at", "kt", "bt", "ktp", "btp", "t"))
        a_ak, a_rb, a_rk, strict, incl = (q[n] for n in ("a_ak", "a_rb", "a_rk", "strict", "incl"))
        w1 = [_mm(at[p], s0[p]) + _pair_mul(a_ak[p], vv[p]) for p in pairs]
        u = [_pair_mul(t[p], w1[p]) for p in pairs]
        du = [_pair_mul_t(a_rb[p], dyv[p]) + _mm(btp[p], dsc[p]) for p in pairs]
        dw1 = [_pair_mul_t(t[p], du[p]) for p in pairs]
        dv = [_pair_mul_t(a_rk[p], dyv[p]) + _mm(ktp[p], dsc[p]) + _pair_mul_t(a_ak[p], dw1[p]) for p in pairs]
        da_ab = [jnp.where(strict, _pair_a(dw1[p], u[p]), 0.0) for p in pairs]
        da_ak = [jnp.where(strict, _pair_a(dw1[p], vv[p]), 0.0) for p in pairs]
        da_rb = [jnp.where(incl, _pair_a(dyv[p], u[p]), 0.0) for p in pairs]
        da_rk = [jnp.where(incl, _pair_a(dyv[p], vv[p]), 0.0) for p in pairs]
        d_rt = [_mm(dyv[p], s0[p], tb=True) + _pair_mul(da_rb[p], bt[p]) + _pair_mul(da_rk[p], kt[p]) for p in pairs]
        d_at = [_mm(dw1[p], s0[p], tb=True) + _pair_mul(da_ab[p], bt[p]) + _pair_mul(da_ak[p], kt[p]) for p in pairs]
        d_bt = [_pair_mul_t(da_ab[p], at[p]) + _pair_mul_t(da_rb[p], rt[p]) for p in pairs]
        d_kt = [_pair_mul_t(da_ak[p], at[p]) + _pair_mul_t(da_rk[p], rt[p]) for p in pairs]
        d_btp = [_mm(u[p], dsc[p], tb=True) for p in pairs]
        d_ktp = [_mm(vv[p], dsc[p], tb=True) for p in pairs]
        ones = jnp.ones((8, PAIR), BF16)
        dpend = [sum(_dg(ones, part, False, True) for part in _split(dsc[p] * s0[p], 3))[0:1, :] * jnp.exp(q["lend"][p])
                 for p in pairs]
        grow = [_mm(jnp.concatenate([rt[p], at[p]], axis=0), jnp.concatenate([dyv[p], dw1[p]], axis=0), ta=True)
                for p in pairs]
        last = lax.broadcasted_iota(jnp.int32, (c, PAIR), 0) == c - 1
        for p in pairs:
            sl = slice(p * PAIR, (p + 1) * PAIR)
            dst[p] = q["pend_col"][p] * dsc[p] + jnp.where(q["bdm"], grow[p], 0.0)
            lc_e = d_ktp[p] * ktp[p] + d_btp[p] * btp[p]
            dlend = jnp.sum(lc_e, axis=0, keepdims=True) + dpend[p]
            dlc = d_rt[p] * rt[p] - d_kt[p] * kt[p] - d_bt[p] * bt[p] - lc_e + jnp.where(last, dlend, 0.0)
            dlp = d_at[p] * at[p]
            dlw_ref[:, sl] = sum(_dg(q["tri"], part, True, False) for part in _split(dlc + dlp, 3)) - dlp
            lc = q["lc"][p]
            pinv = jnp.exp(-lc)
            e = jnp.exp(q["lend"][p] - lc)
            dr_ref[:, sl] = d_rt[p] * jnp.exp(lc)
            da_ref[:, sl] = d_at[p] * jnp.exp(lc - lwv[p])
            dk_ref[:, sl] = d_kt[p] * pinv + d_ktp[p] * e
            db_ref[:, sl] = d_bt[p] * pinv + d_btp[p] * e
            dv_ref[:, sl] = dv[p]

    tok = pl.BlockSpec((c, g * PAIR), lambda gi, ci: (nc - 1 - ci, gi))
    tshape = jax.ShapeDtypeStruct((tokens, width), F32)
    return pl.pallas_call(
        body, name="wkv_bwd", grid=(npair // g, nc),
        in_specs=[tok] * 6 + [pl.BlockSpec((1, g, PAIR, PAIR), lambda gi, ci: (nc - 1 - ci, gi, 0, 0)), tok],
        out_specs=[tok] * 6, out_shape=[tshape] * 6,
        scratch_shapes=[pltpu.VMEM((g, PAIR, PAIR), F32)],
        compiler_params=pltpu.CompilerParams(dimension_semantics=("parallel", "arbitrary")),
    )(r, lw, k, v, a, b, s, dy)


@jax.custom_vjp
def wkv7(r, lw, k, v, a, b):
    return _wkv_fwd_call(r, lw, k, v, a, b)[0]


def _wkv7_fwd(r, lw, k, v, a, b):
    y, s = _wkv_fwd_call(r, lw, k, v, a, b)
    return y, (r, lw, k, v, a, b, s)


wkv7.defvjp(_wkv7_fwd, lambda res, dy: tuple(_wkv_bwd_call(*res, dy)))


def _attn_block(tokens):
    return ATTN_BLOCK_BIG if tokens % ATTN_BLOCK_BIG == 0 else ATTN_BLOCK


def _fox_layouts(cum):
    tokens, heads = cum.shape
    t = _attn_block(tokens)
    cq = cum.reshape(tokens, heads // 2, 2).transpose(1, 0, 2)
    ck = cum.T.reshape(heads // 2, 2, tokens // t, t).transpose(0, 2, 1, 3)
    return cq, ck


def _head_lane_masks(rows):
    lane = lax.broadcasted_iota(jnp.int32, (rows, 2 * HEAD_DIM), 1)
    return [lane < HEAD_DIM, lane >= HEAD_DIM]


def _fox_fwd_call(q, k, v, cq, ck):
    tokens, width = q.shape
    t = _attn_block(tokens)
    nb = tokens // t
    hd = HEAD_DIM
    npair = width // (2 * hd)

    def body(q_ref, k_ref, v_ref, cq_ref, ck_ref, o_ref, lse_ref):
        i = pl.program_id(1)
        masks = _head_lane_masks(t)
        q2 = q_ref[...]
        qs = [jnp.where(mk, q2, 0.0).astype(BF16) for mk in masks]
        cqs = [cq_ref[0, :, hh:hh + 1] for hh in range(2)]

        def block(j, carry, diagonal):
            off = pl.multiple_of(j * t, t)
            ckj = ck_ref[0, j]
            k2 = k_ref[pl.ds(off, t), :].astype(BF16)
            v2 = v_ref[pl.ds(off, t), :].astype(BF16)
            out = []
            for hh in range(2):
                m, l, acc = carry[hh]
                s = _dg(qs[hh], k2, False, True) + (cqs[hh] - ckj[hh:hh + 1, :])
                if diagonal:
                    keep = lax.broadcasted_iota(jnp.int32, (t, t), 0) >= lax.broadcasted_iota(jnp.int32, (t, t), 1)
                    s = jnp.where(keep, s, NEG_BIG)
                m_new = jnp.maximum(m, jnp.max(s, axis=1, keepdims=True))
                alpha = jnp.exp(m - m_new)
                p = jnp.exp(s - m_new)
                l = alpha * l + jnp.sum(p, axis=1, keepdims=True)
                acc = alpha * acc + _dg(p.astype(BF16), v2, False, False)
                out.append((m_new, l, acc))
            return tuple(out)

        init = tuple((jnp.full((t, 1), NEG_BIG, F32), jnp.zeros((t, 1), F32), jnp.zeros((t, 2 * hd), F32)) for _ in range(2))
        res = lax.fori_loop(0, i, lambda j, c: block(j, c, False), init)
        res = block(i, res, True)
        o_ref[...] = jnp.where(masks[0], res[0][2] / res[0][1], res[1][2] / res[1][1])
        for hh in range(2):
            lse_ref[0, :, hh:hh + 1] = res[hh][0] + jnp.log(res[hh][1])

    blk = pl.BlockSpec((t, 2 * hd), lambda hp, i: (i, hp))
    full = pl.BlockSpec((tokens, 2 * hd), lambda hp, i: (0, hp))
    cq_spec = pl.BlockSpec((1, t, 2), lambda hp, i: (hp, i, 0))
    ck_spec = pl.BlockSpec((1, nb, 2, t), lambda hp, i: (hp, 0, 0, 0))
    return pl.pallas_call(
        body, name="fox_fwd", grid=(npair, nb),
        in_specs=[blk, full, full, cq_spec, ck_spec],
        out_specs=[blk, cq_spec],
        out_shape=[jax.ShapeDtypeStruct((tokens, width), F32), jax.ShapeDtypeStruct((npair, tokens, 2), F32)],
        compiler_params=pltpu.CompilerParams(dimension_semantics=("parallel", "arbitrary")),
    )(q, k, v, cq, ck)


def _fox_bwd_call(q, k, v, cq, ck, o, lse, do):
    tokens, width = q.shape
    t = _attn_block(tokens)
    nb = tokens // t
    hd = HEAD_DIM
    npair = width // (2 * hd)

    def body(q_ref, k_ref, v_ref, cq_ref, ck_ref, o_ref, lse_ref, do_ref, dq_ref, dk_ref, dv_ref, dck_ref, dcq_ref):
        i = pl.program_id(1)

        @pl.when(i == 0)
        def _():
            dk_ref[...] = jnp.zeros_like(dk_ref)
            dv_ref[...] = jnp.zeros_like(dv_ref)
            dck_ref[...] = jnp.zeros_like(dck_ref)

        masks = _head_lane_masks(t)
        q2, do2, o2 = q_ref[...], do_ref[...], o_ref[...]
        qs = [jnp.where(mk, q2, 0.0).astype(BF16) for mk in masks]
        dos = [jnp.where(mk, do2, 0.0).astype(BF16) for mk in masks]
        deltas = [jnp.sum(dos[hh].astype(F32) * o2, axis=1, keepdims=True) for hh in range(2)]
        bias = [cq_ref[0, :, hh:hh + 1] - lse_ref[0, :, hh:hh + 1] for hh in range(2)]

        def block(j, carry, diagonal):
            off = pl.multiple_of(j * t, t)
            ckj = ck_ref[0, j]
            k2 = k_ref[pl.ds(off, t), :].astype(BF16)
            v2 = v_ref[pl.ds(off, t), :].astype(BF16)
            out = []
            dk2 = jnp.zeros((t, 2 * hd), F32)
            dv2 = jnp.zeros((t, 2 * hd), F32)
            for hh in range(2):
                s = _dg(qs[hh], k2, False, True) + (bias[hh] - ckj[hh:hh + 1, :])
                if diagonal:
                    keep = lax.broadcasted_iota(jnp.int32, (t, t), 0) >= lax.broadcasted_iota(jnp.int32, (t, t), 1)
                    s = jnp.where(keep, s, NEG_BIG)
                p = jnp.exp(s)
                dp = _dg(dos[hh], v2, False, True)
                ds = p * (dp - deltas[hh])
                dsb = ds.astype(BF16)
                dq, rowsum = carry[hh]
                out.append((dq + _dg(dsb, k2, False, False), rowsum + jnp.sum(ds, axis=1, keepdims=True)))
                dk2 = dk2 + _dg(dsb, qs[hh], True, False)
                dv2 = dv2 + _dg(p.astype(BF16), dos[hh], True, False)
                dck_ref[0, j, hh:hh + 1, :] -= jnp.sum(ds, axis=0, keepdims=True)
            dk_ref[pl.ds(off, t), :] += dk2
            dv_ref[pl.ds(off, t), :] += dv2
            return tuple(out)

        init = tuple((jnp.zeros((t, 2 * hd), F32), jnp.zeros((t, 1), F32)) for _ in range(2))
        res = lax.fori_loop(0, i, lambda j, c: block(j, c, False), init)
        res = block(i, res, True)
        dq_ref[...] = jnp.where(masks[0], res[0][0], res[1][0])
        for hh in range(2):
            dcq_ref[0, :, hh:hh + 1] = res[hh][1]

    blk = pl.BlockSpec((t, 2 * hd), lambda hp, i: (i, hp))
    full = pl.BlockSpec((tokens, 2 * hd), lambda hp, i: (0, hp))
    cq_spec = pl.BlockSpec((1, t, 2), lambda hp, i: (hp, i, 0))
    ck_spec = pl.BlockSpec((1, nb, 2, t), lambda hp, i: (hp, 0, 0, 0))
    tshape = jax.ShapeDtypeStruct((tokens, width), F32)
    return pl.pallas_call(
        body, name="fox_bwd", grid=(npair, nb),
        in_specs=[blk, full, full, cq_spec, ck_spec, blk, cq_spec, blk],
        out_specs=[blk, full, full, ck_spec, cq_spec],
        out_shape=[tshape, tshape, tshape, jax.ShapeDtypeStruct((npair, nb, 2, t), F32),
                   jax.ShapeDtypeStruct((npair, tokens, 2), F32)],
        compiler_params=pltpu.CompilerParams(dimension_semantics=("parallel", "arbitrary")),
    )(q, k, v, cq, ck, o, lse, do)


@jax.custom_vjp
def fox_attention(q, k, v, cum):
    return _fox_fwd_call(q, k, v, *_fox_layouts(cum))[0]


def _fox_fwd(q, k, v, cum):
    cq, ck = _fox_layouts(cum)
    o, lse = _fox_fwd_call(q, k, v, cq, ck)
    return o, (q, k, v, cq, ck, o, lse)


def _fox_bwd(res, do):
    q, k, v, cq, ck, o, lse = res
    dq, dk, dv, dck, dcq = _fox_bwd_call(q, k, v, cq, ck, o, lse, do)
    npair, nb, _, t = dck.shape
    dcum = dck.transpose(0, 2, 1, 3).reshape(2 * npair, nb * t).T + dcq.transpose(1, 0, 2).reshape(nb * t, 2 * npair)
    return dq, dk, dv, dcum


fox_attention.defvjp(_fox_fwd, _fox_bwd)


def _loss_call(y, target):
    rows, d = y.shape
    tr = _row_tile(rows, d)

    def body(y_ref, t_ref, loss_ref, dy_ref):
        @pl.when(pl.program_id(0) == 0)
        def _():
            loss_ref[...] = jnp.zeros_like(loss_ref)

        diff = y_ref[...] - t_ref[...]
        dy_ref[...] = diff * (1.0 / d)
        loss_ref[...] += (0.5 / d) * jnp.sum(jnp.sum(diff * diff, axis=1, keepdims=True), axis=0, keepdims=True)

    return pl.pallas_call(
        body, name="loss", grid=(rows // tr,),
        in_specs=[pl.BlockSpec((tr, d), lambda i: (i, 0))] * 2,
        out_specs=[pl.BlockSpec((1, 1), lambda i: (0, 0)), pl.BlockSpec((tr, d), lambda i: (i, 0))],
        out_shape=[jax.ShapeDtypeStruct((1, 1), F32), jax.ShapeDtypeStruct((rows, d), F32)],
        compiler_params=pltpu.CompilerParams(dimension_semantics=("arbitrary",)),
    )(y, target)


def _adamw_call(w, g, m, v):
    rows, cols = w.shape
    tr = _row_tile_ragged(rows, cols, budget=1024 * 1024)
    c1 = 1.0 / (1.0 - ADAM_B1 ** ADAM_STEP)
    c2 = 1.0 / (1.0 - ADAM_B2 ** ADAM_STEP)

    def body(w_ref, g_ref, m_ref, v_ref, d_ref, nm_ref, nv_ref):
        gv = g_ref[...]
        nm = ADAM_B1 * m_ref[...] + (1.0 - ADAM_B1) * gv
        nv = ADAM_B2 * v_ref[...] + (1.0 - ADAM_B2) * (gv * gv)
        nm_ref[...] = nm
        nv_ref[...] = nv
        d_ref[...] = -ADAM_LR * ((nm * c1) / (jnp.sqrt(nv * c2) + ADAM_EPS) + ADAM_WD * w_ref[...])

    spec = pl.BlockSpec((tr, cols), lambda i: (i, 0))
    shape = jax.ShapeDtypeStruct((rows, cols), F32)
    return pl.pallas_call(
        body, name="adamw", grid=(pl.cdiv(rows, tr),),
        in_specs=[spec] * 4, out_specs=[spec] * 3, out_shape=[shape] * 3,
        compiler_params=pltpu.CompilerParams(dimension_semantics=("parallel",)),
    )(w, g, m, v)


def _my_place():
    return lax.axis_index("x"), lax.axis_index("y"), lax.axis_index("c")


def _place_index(px, py, pc):
    return 4 * px + 2 * py + pc


HBM_SPEC = pl.BlockSpec(memory_space=pltpu.HBM)


def _all_gather_call(block):
    def body(x_ref, out_ref, send_sems, recv_sems, local_sem):
        x, y, c = _my_place()
        me, sibling = (x, y, c), (x, y, 1 - c)
        chips = [(1 - x, y), (x, 1 - y), (1 - x, 1 - y)]

        def slot(px, py, pc):
            return out_ref.at[_place_index(px, py, pc)]

        def copy(k, blk, to, src=None):
            return pltpu.make_async_remote_copy(
                src_ref=slot(*blk) if src is None else src, dst_ref=slot(*blk),
                send_sem=send_sems.at[k], recv_sem=recv_sems.at[k],
                device_id=to, device_id_type=pl.DeviceIdType.MESH)

        mine = pltpu.make_async_copy(x_ref, slot(*me), local_sem)
        mine.start()
        first = [copy(0, me, sibling, src=x_ref)]
        first += [copy(1 + j, me, (*chip, c), src=x_ref) for j, chip in enumerate(chips)]
        for cp in first:
            cp.start()
        passed = [copy(4 + j, (*chip, c), sibling) for j, chip in enumerate(chips)]
        for j, chip in enumerate(chips):
            copy(1 + j, (*chip, c), me).wait_recv()
            passed[j].start()
        copy(0, sibling, me).wait_recv()
        for j, chip in enumerate(chips):
            copy(4 + j, (*chip, 1 - c), me).wait_recv()
        for cp in first + passed:
            cp.wait_send()
        mine.wait()

    return pl.pallas_call(
        body, name="all_gather",
        out_shape=jax.ShapeDtypeStruct((N_DEV,) + block.shape, block.dtype),
        in_specs=[HBM_SPEC], out_specs=HBM_SPEC,
        scratch_shapes=[pltpu.SemaphoreType.DMA((7,)), pltpu.SemaphoreType.DMA((7,)), pltpu.SemaphoreType.DMA],
    )(block)


SEM_SPEC = pl.BlockSpec(memory_space=pltpu.SEMAPHORE)
SIDE_EFFECT = pltpu.SideEffectType.DATAFLOW_SIDE_EFFECTING


def _peers():
    x, y, c = _my_place()
    out = []
    for k in range(1, N_DEV):
        peer = (x ^ (k >> 2), y ^ ((k >> 1) & 1), c ^ (k & 1))
        out.append((k - 1, peer, _place_index(*peer)))
    return _place_index(x, y, c), out


def _spread_start(src, per_peer, name, after=None):
    slot = src.shape[1:] if per_peer else src.shape
    order = () if after is None else (after,)

    def body(src_ref, land_ref, *rest):
        send_sems, recv_sems, src_thru, land_thru, token = rest[len(order):]
        mine, peers = _peers()
        for k, peer, peer_idx in peers:
            pltpu.make_async_remote_copy(
                src_ref=src_ref.at[peer_idx] if per_peer else src_ref, dst_ref=land_ref.at[mine],
                send_sem=send_sems.at[k], recv_sem=recv_sems.at[k],
                device_id=peer, device_id_type=pl.DeviceIdType.MESH).start()
        token[...] = jnp.zeros_like(token)

    return pl.pallas_call(
        body, name=name,
        out_shape=(pltpu.SemaphoreType.DMA((N_DEV - 1,)), pltpu.SemaphoreType.DMA((N_DEV - 1,)),
                   pltpu.HBM(src.shape, src.dtype), pltpu.HBM((N_DEV,) + slot, src.dtype),
                   jax.ShapeDtypeStruct((8, 128), F32)),
        in_specs=(HBM_SPEC, HBM_SPEC) + (pl.BlockSpec(memory_space=pl.ANY),) * len(order),
        out_specs=(SEM_SPEC, SEM_SPEC, HBM_SPEC, HBM_SPEC, pl.BlockSpec(memory_space=pltpu.VMEM)),
        input_output_aliases={0: 2, 1: 3},
        compiler_params=pltpu.CompilerParams(has_side_effects=SIDE_EFFECT),
    )(pltpu.with_memory_space_constraint(src, pltpu.HBM),
      pltpu.with_memory_space_constraint(lax.empty((N_DEV,) + slot, src.dtype), pltpu.HBM), *order)


def _spread_wait(handles, after, per_peer, name):
    send_sems, recv_sems, src_thru, land_thru = handles

    def body(src_ref, land_ref, send_sems, recv_sems, after_ref, src_dead, got_ref):
        _, peers = _peers()
        for k, peer, peer_idx in peers:
            copy = pltpu.make_async_remote_copy(
                src_ref=src_ref.at[peer_idx] if per_peer else src_ref, dst_ref=land_ref.at[peer_idx],
                send_sem=send_sems.at[k], recv_sem=recv_sems.at[k],
                device_id=peer, device_id_type=pl.DeviceIdType.MESH)
            copy.wait_send()
            copy.wait_recv()

    return pl.pallas_call(
        body, name=name,
        out_shape=(pltpu.HBM(src_thru.shape, src_thru.dtype), pltpu.HBM(land_thru.shape, land_thru.dtype)),
        in_specs=(HBM_SPEC, HBM_SPEC, SEM_SPEC, SEM_SPEC, pl.BlockSpec(memory_space=pl.ANY)),
        out_specs=(HBM_SPEC, HBM_SPEC), input_output_aliases={0: 0, 1: 1},
        compiler_params=pltpu.CompilerParams(has_side_effects=SIDE_EFFECT),
    )(src_thru, land_thru, send_sems, recv_sems, after)


def _sum_slots_call(slots):
    _, rows, cols = slots.shape
    tr = _row_tile_ragged(rows, cols, budget=512 * 1024)

    def body(s_ref, o_ref):
        acc = s_ref[0].astype(F32)
        for j in range(1, N_DEV):
            acc = acc + s_ref[j].astype(F32)
        o_ref[...] = acc

    return pl.pallas_call(
        body, name="sum_slots", grid=(pl.cdiv(rows, tr),),
        in_specs=[pl.BlockSpec((N_DEV, tr, cols), lambda i: (0, i, 0))],
        out_specs=pl.BlockSpec((tr, cols), lambda i: (i, 0)),
        out_shape=jax.ShapeDtypeStruct((rows, cols), F32),
        compiler_params=pltpu.CompilerParams(dimension_semantics=("parallel",)),
    )(slots)


def _with_own_slot(got, own, mine):
    return lax.dynamic_update_index_in_dim(got, own, mine, 0)


def _pack(vectors, width):
    flat = jnp.concatenate([v.reshape(-1) for v in vectors])
    return jnp.pad(flat, (0, width - flat.shape[0])).reshape(width // 128, 128)


def _unpack(packed, like):
    flat = packed.reshape(-1)
    out, at = [], 0
    for v in like:
        out.append(flat[at:at + v.size].reshape(v.shape))
        at += v.size
    return tuple(out)


def _sum_over_devices(grads):
    n = sum(v.size for v in grads)
    width = -(-n // 1024) * 1024
    return _unpack(_sum_slots_call(_all_gather_call(_pack(grads, width))), grads)


def _cols_from_slots(slots):
    n, rows, cols = slots.shape
    return slots.transpose(1, 0, 2).reshape(rows, n * cols)


def _rows_from_slots(slots):
    return slots.reshape(-1, slots.shape[2])


def _pad128(n):
    return -(-n // 128) * 128


def _pad_to_tiles(a, axis):
    n = a.shape[axis]
    pads = [(0, 0)] * a.ndim
    pads[axis] = (0, _pad128(n) - n)
    return jnp.pad(a, pads)


def _rwkv_group(a, rw, dl, al, gl, axis):
    take = lambda lo, hi: lax.slice_in_dim(a, lo, hi, axis=axis)
    at = 3 * rw
    parts = [take(0, at)]
    for n in (dl, al, gl):
        parts.append(_pad_to_tiles(take(at, at + n), axis))
        at += n
    return jnp.concatenate(parts, axis=axis)


def _in_proj_layout(slots, rw, fw, dl, al, gl):
    wt = _rows_from_slots(slots)
    rcols = 3 * rw + dl + al + gl
    fcols = 3 * fw + fw // HEAD_DIM
    return _rwkv_group(wt[:rcols], rw, dl, al, gl, 0), _pad_to_tiles(wt[rcols:rcols + fcols], 0), wt[rcols + fcols:]


def _low_rank_layout(slots):
    return _pad_to_tiles(_cols_from_slots(slots), 0)


def _stage_embed(meta, x, n1, lp):
    h0 = jnp.concatenate([meta, x, jnp.zeros((lp - meta.shape[0] - x.shape[0], x.shape[1]), F32)], axis=0)
    return h0, rmsnorm(h0, n1)


def _stage_mix(z_r, z_f, z_g, small, w2, a2, g2, dims):
    (mu, w0, a0, k_k, k_a, r_k, gn_w, gn_b, q_g, k_g, f_bias) = small
    rw, fw, dl, al, gl = dims
    fcols = 3 * fw + fw // HEAD_DIM

    r, lw, kf, v, na, b, g = rwkv_prep(z_r, _rwkv_group(mu, rw, dl, al, gl, 1), w0, a0, k_k, k_a, w2, a2, g2)
    y = wkv7(r, lw, kf, v, na, b)
    y_a = gn_bonus(y, r, kf, v, gn_w, gn_b, r_k.reshape(1, rw)) * g

    fq, fk, fv, fl = z_f[:, :fw], z_f[:, fw:2 * fw], z_f[:, 2 * fw:3 * fw], z_f[:, 3 * fw:fcols]
    fq = head_rms(fq, q_g) * (HEAD_DIM ** -0.5)
    fk = head_rms(fk, k_g)
    cum = jnp.cumsum(jax.nn.log_sigmoid(badd(fl, f_bias)), axis=0)
    y_b = fox_attention(fq, fk, fv, cum)
    return y_a, y_b, jax.nn.sigmoid(z_g)


def _stage_merge(h0, y_a, y_b, gates, w_a, w_b, w_o):
    d = h0.shape[1]
    merged = gates[:, :d] * dense_cols(y_a, w_a) + gates[:, d:] * dense_cols(y_b, w_b)
    return h0 + dense(merged, w_o)


def _stage_ffn(h1, n2, w_gu, w_dn):
    gu = dense_cols(rmsnorm(h1, n2), w_gu)
    dff = w_dn.shape[0]
    return h1 + dense(jax.nn.silu(gu[:, :dff]) * gu[:, dff:], w_dn)


SHARDED = ("meta_tokens", "w_in", "rwkv_w2", "rwkv_a2", "rwkv_g2", "w_branch_a", "w_branch_b", "w_o", "w_gate_up", "w_down")
SMALL = ("norm1_g", "rwkv_mu", "rwkv_w0", "rwkv_a0", "rwkv_k_k", "rwkv_k_a", "rwkv_r_k", "rwkv_gn_w", "rwkv_gn_b",
         "fox_q_norm_g", "fox_k_norm_g", "fox_f_bias", "norm2_g")
WEIGHTS = ("meta_tokens", "norm1_g", "w_in", "rwkv_mu", "rwkv_w0", "rwkv_w2", "rwkv_a0", "rwkv_a2", "rwkv_g2", "rwkv_k_k",
           "rwkv_k_a", "rwkv_r_k", "rwkv_gn_w", "rwkv_gn_b", "fox_q_norm_g", "fox_k_norm_g", "fox_f_bias", "w_branch_a",
           "w_branch_b", "w_o", "norm2_g", "w_gate_up", "w_down")


def _as2d(a):
    return a.reshape(-1, a.shape[-1])


def kernel(x, meta_tokens, norm1_g, w_in, rwkv_mu, rwkv_w0, rwkv_w2, rwkv_a0, rwkv_a2, rwkv_g2, rwkv_k_k, rwkv_k_a, rwkv_r_k, rwkv_gn_w, rwkv_gn_b, fox_q_norm_g, fox_k_norm_g, fox_f_bias, w_branch_a, w_branch_b, w_o, norm2_g, w_gate_up, w_down, loss_target, m_meta_tokens, m_norm1_g, m_w_in, m_rwkv_mu, m_rwkv_w0, m_rwkv_w2, m_rwkv_a0, m_rwkv_a2, m_rwkv_g2, m_rwkv_k_k, m_rwkv_k_a, m_rwkv_r_k, m_rwkv_gn_w, m_rwkv_gn_b, m_fox_q_norm_g, m_fox_k_norm_g, m_fox_f_bias, m_w_branch_a, m_w_branch_b, m_w_o, m_norm2_g, m_w_gate_up, m_w_down, v_meta_tokens, v_norm1_g, v_w_in, v_rwkv_mu, v_rwkv_w0, v_rwkv_w2, v_rwkv_a0, v_rwkv_a2, v_rwkv_g2, v_rwkv_k_k, v_rwkv_k_a, v_rwkv_r_k, v_rwkv_gn_w, v_rwkv_gn_b, v_fox_q_norm_g, v_fox_k_norm_g, v_fox_f_bias, v_w_branch_a, v_w_branch_b, v_w_o, v_norm2_g, v_w_gate_up, v_w_down):
    given = dict(locals())
    w = {n: given[n] for n in WEIGHTS}
    assert rwkv_r_k.shape[-1] == HEAD_DIM
    n_meta, seq = meta_tokens.shape[0], x.shape[1]
    tokens = n_meta + seq
    lp = -(-tokens // TOKEN_TILE) * TOKEN_TILE
    mine = _place_index(*(lax.axis_index(a) for a in MESH_AXES))
    x2 = x[0]

    local = {n: _as2d(given[n]) for n in given if n != "x" and n != "loss_target"}
    for n in ("w_in", "m_w_in", "v_w_in"):
        local[n] = jnp.transpose(given[n][0])
    blocks = {n: local[n].astype(F32 if n == "meta_tokens" else BF16) for n in SHARDED}
    first = ("meta_tokens", "rwkv_w2", "rwkv_a2", "rwkv_g2")
    started = {n: _spread_start(blocks[n], False, "gather_start_" + n) for n in first}
    zero = sum(started[n][4][0, 0] for n in first)

    def gathered(n, after):
        own, got = _spread_wait(started[n][:4], after, False, "gather_wait_" + n)
        return _with_own_slot(got, own, mine)

    sm = {n: _as2d(w[n]) for n in SMALL}
    small_mix = tuple(sm[n] for n in SMALL[1:-1])
    n1 = sm["norm1_g"] + zero
    rw, fw = w_branch_a.shape[-2], w_branch_b.shape[-2]
    dims = (rw, fw, rwkv_w2.shape[-2], rwkv_a2.shape[-2], rwkv_g2.shape[-2])
    same = lambda s: (s,)

    meta, un_meta = jax.vjp(_cols_from_slots, gathered("meta_tokens", x2))
    (h0, xn), vjp_embed = jax.vjp(lambda m, xs, g: _stage_embed(m, xs, g, lp), meta, x2, n1)
    in_slots = _all_gather_call(blocks["w_in"])
    later = [n for n in SHARDED if n not in first and n != "w_in"]
    started.update({n: _spread_start(blocks[n], False, "gather_start_" + n, after=in_slots) for n in later})
    w_groups, un_in = jax.vjp(lambda s: _in_proj_layout(s, *dims), in_slots)
    xn_b = xn.astype(BF16)
    behind = sum(started[n][4] for n in later)
    z_r, z_f, z_g = (_matmul(xn_b, wg, tb=True, name="in_proj_" + tag, after=behind) for wg, tag in zip(w_groups, "rfg"))
    (w2, un_w2), (a2, un_a2), (g2, un_g2) = (jax.vjp(_low_rank_layout, gathered(n, xn)) for n in ("rwkv_w2", "rwkv_a2", "rwkv_g2"))
    (y_a, y_b, gates), vjp_mix = jax.vjp(lambda zr, zf, zg, s, a, b, c: _stage_mix(zr, zf, zg, s, a, b, c, dims),
                                         z_r, z_f, z_g, small_mix, w2, a2, g2)
    w_a, w_b = gathered("w_branch_a", y_a), gathered("w_branch_b", y_a)
    w_o_full, un_wo = jax.vjp(_rows_from_slots, gathered("w_o", y_a))
    h1, vjp_merge = jax.vjp(_stage_merge, h0, y_a, y_b, gates, w_a, w_b, w_o_full)
    w_gu = gathered("w_gate_up", h1)
    w_dn, un_dn = jax.vjp(_rows_from_slots, gathered("w_down", h1))
    y, vjp_ffn = jax.vjp(_stage_ffn, h1, sm["norm2_g"], w_gu, w_dn)

    loss_part, dy_real = _loss_call(y[n_meta:tokens], loss_target[0])
    dy = jnp.pad(dy_real, ((n_meta, lp - tokens), (0, 0)))
    loss = lax.psum(loss_part[0, 0], MESH_AXES)

    sent = {}

    def send_grad(n, dmat, unlayout):
        sent[n] = _spread_start(unlayout(dmat)[0], True, "grad_start_" + n)
        return sent[n][4][0, 0]

    d_h1, d_n2, d_wgu, d_wdn = vjp_ffn(dy)
    behind = send_grad("w_gate_up", d_wgu, same) + send_grad("w_down", d_wdn, un_dn)
    d_h0, d_ya, d_yb, d_gates, d_wa, d_wb, d_wo = vjp_merge(d_h1 + behind)
    behind = send_grad("w_o", d_wo, un_wo) + send_grad("w_branch_a", d_wa, same) + send_grad("w_branch_b", d_wb, same)
    d_zr, d_zf, d_zg, d_small_mix, d_w2, d_a2, d_g2 = vjp_mix((d_ya + behind, d_yb, d_gates))
    dproj_b = jnp.concatenate([d_zr.astype(BF16), d_zf.astype(BF16), d_zg.astype(BF16)], axis=1)
    d_wcat = _matmul(dproj_b, xn_b, ta=True, out_dtype=BF16, name="in_proj_dw")
    ends = (w_groups[0].shape[0], w_groups[0].shape[0] + w_groups[1].shape[0])
    send_grad("w_in", (d_wcat[:ends[0]], d_wcat[ends[0]:ends[1]], d_wcat[ends[1]:]), un_in)
    d_xn = _matmul(dproj_b, jnp.concatenate(w_groups, axis=0), out_dtype=F32, name="in_proj_dx", after=sent["w_in"][4])
    send_grad("rwkv_w2", d_w2, un_w2)
    send_grad("rwkv_a2", d_a2, un_a2)
    send_grad("rwkv_g2", d_g2, un_g2)
    d_meta, g_x, d_n1 = vjp_embed((d_h0, d_xn))
    send_grad("meta_tokens", d_meta, un_meta)

    grads = dict(zip(SMALL, _sum_over_devices((d_n1, *d_small_mix, d_n2))))
    grads = {n: g.reshape(w[n].shape) for n, g in grads.items()}

    delta, new_m, new_v = {}, {}, {}
    after = g_x
    for n in ("w_gate_up", "w_down", "w_o", "w_branch_a", "w_branch_b", "rwkv_g2", "rwkv_a2", "rwkv_w2", "meta_tokens", "w_in"):
        src, got = _spread_wait(sent[n][:4], after, True, "grad_wait_" + n)
        g = _sum_slots_call(_with_own_slot(got, lax.dynamic_index_in_dim(src, mine, 0, keepdims=False), mine))
        d_, m_, v_ = _adamw_call(local[n], g, local["m_" + n], local["v_" + n])
        back = (lambda t: jnp.transpose(t)[None]) if n == "w_in" else (lambda t: t.reshape(w[n].shape))
        grads[n], delta[n], new_m[n], new_v[n] = (back(t) for t in (g, d_, m_, v_))
        after = m_
    n_small = sum(w[n].size for n in SMALL)
    width = -(-n_small // 1024) * 1024
    packs = [_pack([src[n] if p == "" else given[p + n] for n in SMALL], width)
             for p, src in (("", w), ("", grads), ("m_", None), ("v_", None))]
    like = [w[n] for n in SMALL]
    for out, packed in zip((delta, new_m, new_v), _adamw_call(*packs)):
        out.update(dict(zip(SMALL, _unpack(packed, like))))

    return (loss, g_x[None], *[grads[n] for n in WEIGHTS], *[delta[n] for n in WEIGHTS],
            *[new_m[n] for n in WEIGHTS], *[new_v[n] for n in WEIGHTS])
```

```python
import functools

import jax
import jax.numpy as jnp
from jax import lax
from jax.experimental import pallas as pl
from jax.experimental.pallas import tpu as pltpu

F32 = jnp.float32
BF16 = jnp.bfloat16

N_DEV = 8
MESH_AXES = ("x", "y", "c")
HEAD_DIM = 64
TOKEN_TILE = 128
WKV_CHUNK = 64
WKV_PAIRS_PER_STEP = 4
PAIR = 2 * HEAD_DIM
ATTN_BLOCK = 128
ATTN_BLOCK_BIG = 384
RMS_EPS = 1e-6
GN_EPS = 64e-5
L2_FLOOR = 1e-12
NEG_BIG = -1e30
ADAM_LR, ADAM_B1, ADAM_B2, ADAM_EPS, ADAM_WD, ADAM_STEP = 0.001, 0.9, 0.999, 1e-08, 0.01, 10
VMEM_BYTES_V7X = 64 * 1024 * 1024
VMEM_LIMIT_CAP = 56 * 1024 * 1024
VMEM_LIMIT_FLOOR = 32 * 1024 * 1024
MATMUL_VMEM_BUDGET = 36 * 1024 * 1024
GRID_STEP_BYTES = 1024 * 1024
ACC_BYTES_PER_HBM_BYTE = 6


def _vmem_limit(estimate_bytes):
    return int(min(max(estimate_bytes * 5 // 4, VMEM_LIMIT_FLOOR), VMEM_LIMIT_CAP))


def _pick(dim, cands):
    for c in cands:
        if dim % c == 0:
            return c
    return dim


def _row_tile(rows, width, itemsize=4, budget=2 * 1024 * 1024):
    for c in (1408, 1024, 704, 512, 384, 256, 128, 64, 32, 16, 8):
        if rows % c == 0 and c * width * itemsize <= budget:
            return c
    return rows


def _row_tile_ragged(rows, width, itemsize=4, budget=2 * 1024 * 1024):
    tile = _row_tile(rows, width, itemsize, budget)
    if tile * width * itemsize <= budget or rows < 16:
        return tile
    padded = -(-rows // 16) * 16
    for c in (1408, 1024, 704, 512, 384, 336, 256, 192, 128, 96, 64, 48, 32, 16):
        if padded % c == 0 and c * width * itemsize <= budget:
            return c
    return tile


def _dg(a, b, ta, tb):
    dims = (((0 if ta else 1,), (1 if tb else 0,)), ((), ()))
    return lax.dot_general(a, b, dims, preferred_element_type=F32)


def _split(x, n):
    parts = []
    for _ in range(n):
        h = x.astype(BF16)
        parts.append(h)
        x = x - h.astype(F32)
    return parts


def _mm(a, b, ta=False, tb=False):
    return _dg(a.astype(BF16), b.astype(BF16), ta, tb)


def _matmul(a, b, ta=False, tb=False, out_dtype=F32, name="matmul", after=None, b_slots=False, out_slots=0):
    if ta:
        kdim, m = a.shape
    else:
        m, kdim = a.shape
    if b_slots:
        n_slots, brows, bcols = b.shape
        n, k2 = (brows, n_slots * bcols) if tb else (n_slots * bcols, brows)
    elif tb:
        n, k2 = b.shape
    else:
        k2, n = b.shape
    assert kdim == k2, (a.shape, b.shape, ta, tb)
    sa, sb, so = a.dtype.itemsize, b.dtype.itemsize, jnp.dtype(out_dtype).itemsize
    n_unit = bcols if (b_slots and not tb) else (n // out_slots if out_slots else n)
    k_unit = bcols if (b_slots and tb) else kdim
    tm, tn, tk = _matmul_tiles(m, n, kdim, ta, sa, sb, so, n_unit, k_unit)
    nk = kdim // tk

    order = () if after is None else (after,)

    def body(a_ref, b_ref, *rest):
        o_ref, acc = rest[len(order)], rest[len(order) + 1:]
        part = _dg(a_ref[...].astype(BF16), b_ref[...].astype(BF16), ta, tb)
        if nk == 1:
            o_ref[...] = part.astype(o_ref.dtype)
            return
        kk = pl.program_id(2)

        @pl.when(kk == 0)
        def _():
            acc[0][...] = part

        @pl.when(kk > 0)
        def _():
            acc[0][...] += part

        @pl.when(kk == nk - 1)
        def _():
            o_ref[...] = acc[0][...].astype(o_ref.dtype)

    a_spec = pl.BlockSpec((tk, tm), lambda i, j, k: (k, i)) if ta else pl.BlockSpec((tm, tk), lambda i, j, k: (i, k))
    if b_slots and tb:
        per = bcols // tk
        b_spec = pl.BlockSpec((None, tn, tk), lambda i, j, k: (k // per, j, k % per))
    elif b_slots:
        per = bcols // tn
        b_spec = pl.BlockSpec((None, tk, tn), lambda i, j, k: (j // per, k, j % per))
    else:
        b_spec = pl.BlockSpec((tn, tk), lambda i, j, k: (j, k)) if tb else pl.BlockSpec((tk, tn), lambda i, j, k: (k, j))
    if out_slots:
        per_out = n // out_slots // tn
        out_spec = pl.BlockSpec((None, tm, tn), lambda i, j, k: (j // per_out, i, j % per_out))
        out_shape = jax.ShapeDtypeStruct((out_slots, m, n // out_slots), out_dtype)
    else:
        out_spec = pl.BlockSpec((tm, tn), lambda i, j, k: (i, j))
        out_shape = jax.ShapeDtypeStruct((m, n), out_dtype)
    return pl.pallas_call(
        body, name=name,
        grid=(m // tm, n // tn, nk),
        in_specs=[a_spec, b_spec] + [pl.BlockSpec(memory_space=pl.ANY)] * len(order),
        out_specs=out_spec,
        out_shape=out_shape,
        scratch_shapes=[pltpu.VMEM((tm, tn), F32)] if nk > 1 else [],
        compiler_params=pltpu.CompilerParams(dimension_semantics=("parallel", "parallel", "arbitrary"),
                                             vmem_limit_bytes=_vmem_limit(_matmul_vmem(tm, tn, tk, nk, sa, sb, so))),
    )(a, b, *order)


def _matmul_vmem(tm, tn, tk, nk, sa, sb, so):
    return 2 * (tm * tk * sa + tk * tn * sb + tm * tn * so) + tm * tn * 4 + (tm * tn * 4 if nk > 1 else 0)


def _matmul_tiles(m, n, kdim, ta, sa, sb, so, n_unit, k_unit):
    lane = (2816, 2176, 2048, 1408, 1024, 640, 512, 384, 256, 128)
    sublane = (2816, 2176, 2048, 1408, 1024, 704, 512, 384, 256, 128)
    divs = lambda dim, cands: [c for c in cands if dim % c == 0] or [dim]
    best = None
    for tm in divs(m, lane if ta else sublane):
        for tn in divs(n_unit, lane):
            for tk in divs(k_unit, sublane if ta else lane) + ([kdim] if (kdim <= 2048 and k_unit == kdim) else []):
                nk, nm, nn = kdim // tk, m // tm, n // tn
                if _matmul_vmem(tm, tn, tk, nk, sa, sb, so) > MATMUL_VMEM_BUDGET:
                    continue
                a_bytes = m * kdim * sa * (nn if nk > 1 else 1)
                b_bytes = kdim * n * sb * (1 if (nk == 1 and nn == 1) else nm)
                acc_bytes = m * n * 4 * 3 * nk // ACC_BYTES_PER_HBM_BYTE if nk > 1 else 0
                cost = a_bytes + b_bytes + m * n * so + acc_bytes + nm * nn * nk * GRID_STEP_BYTES
                if best is None or cost < best[0]:
                    best = (cost, tm, tn, tk)
    return best[1:]


@jax.custom_vjp
def dense(x, w):
    return _matmul(x.astype(BF16), w, name="dense_fwd")


def _dense_fwd(x, w):
    assert x.dtype == F32
    xb = x.astype(BF16)
    return _matmul(xb, w, name="dense_fwd"), (xb, w)


def _dense_bwd(res, dy):
    xb, w = res
    dyb = dy.astype(BF16)
    dx = _matmul(dyb, w, tb=True, out_dtype=F32, name="dense_dx")
    dw = _matmul(xb, dyb, ta=True, out_dtype=w.dtype, name="dense_dw")
    return dx, dw


dense.defvjp(_dense_fwd, _dense_bwd)


@jax.custom_vjp
def dense_cols(x, w_slots):
    return _matmul(x.astype(BF16), w_slots, b_slots=True, name="dense_cols_fwd")


def _dense_cols_fwd(x, w_slots):
    assert x.dtype == F32
    xb = x.astype(BF16)
    return _matmul(xb, w_slots, b_slots=True, name="dense_cols_fwd"), (xb, w_slots)


def _dense_cols_bwd(res, dy):
    xb, w_slots = res
    dyb = dy.astype(BF16)
    dx = _matmul(dyb, w_slots, tb=True, b_slots=True, out_dtype=F32, name="dense_cols_dx")
    dw = _matmul(xb, dyb, ta=True, out_slots=w_slots.shape[0], out_dtype=w_slots.dtype, name="dense_cols_dw")
    return dx, dw


dense_cols.defvjp(_dense_cols_fwd, _dense_cols_bwd)


def _rms_fwd_call(x, g):
    rows, d = x.shape
    tr = _row_tile(rows, d)

    def body(x_ref, g_ref, y_ref):
        xv = x_ref[...]
        rstd = lax.rsqrt(jnp.mean(xv * xv, axis=1, keepdims=True) + RMS_EPS)
        y_ref[...] = (xv * rstd) * g_ref[...]

    return pl.pallas_call(
        body, name="rms_fwd", grid=(rows // tr,),
        in_specs=[pl.BlockSpec((tr, d), lambda i: (i, 0)), pl.BlockSpec((1, d), lambda i: (0, 0))],
        out_specs=pl.BlockSpec((tr, d), lambda i: (i, 0)),
        out_shape=jax.ShapeDtypeStruct((rows, d), F32),
        compiler_params=pltpu.CompilerParams(dimension_semantics=("parallel",)),
    )(x, g)


def _rms_bwd_call(x, g, dy):
    rows, d = x.shape
    tr = _row_tile(rows, d)

    def body(x_ref, g_ref, dy_ref, dx_ref, dg_ref):
        @pl.when(pl.program_id(0) == 0)
        def _():
            dg_ref[...] = jnp.zeros_like(dg_ref)

        xv = x_ref[...]
        dyv = dy_ref[...]
        rstd = lax.rsqrt(jnp.mean(xv * xv, axis=1, keepdims=True) + RMS_EPS)
        xhat = xv * rstd
        dxhat = dyv * g_ref[...]
        dx_ref[...] = rstd * (dxhat - xhat * jnp.mean(dxhat * xhat, axis=1, keepdims=True))
        dg_ref[...] += jnp.sum(dyv * xhat, axis=0, keepdims=True)

    return pl.pallas_call(
        body, name="rms_bwd", grid=(rows // tr,),
        in_specs=[pl.BlockSpec((tr, d), lambda i: (i, 0)), pl.BlockSpec((1, d), lambda i: (0, 0)),
                  pl.BlockSpec((tr, d), lambda i: (i, 0))],
        out_specs=[pl.BlockSpec((tr, d), lambda i: (i, 0)), pl.BlockSpec((1, d), lambda i: (0, 0))],
        out_shape=[jax.ShapeDtypeStruct((rows, d), F32), jax.ShapeDtypeStruct((1, d), F32)],
        compiler_params=pltpu.CompilerParams(dimension_semantics=("arbitrary",)),
    )(x, g, dy)


@jax.custom_vjp
def rmsnorm(x, g):
    return _rms_fwd_call(x, g)


rmsnorm.defvjp(lambda x, g: (_rms_fwd_call(x, g), (x, g)), lambda res, dy: tuple(_rms_bwd_call(res[0], res[1], dy)))


def _bcast_call(x, p, mul):
    rows, d = x.shape
    tr = _row_tile(rows, d)

    def body(x_ref, p_ref, y_ref):
        y_ref[...] = x_ref[...] * p_ref[...] if mul else x_ref[...] + p_ref[...]

    return pl.pallas_call(
        body, name="bcast_mul" if mul else "bcast_add", grid=(rows // tr,),
        in_specs=[pl.BlockSpec((tr, d), lambda i: (i, 0)), pl.BlockSpec((1, d), lambda i: (0, 0))],
        out_specs=pl.BlockSpec((tr, d), lambda i: (i, 0)),
        out_shape=jax.ShapeDtypeStruct((rows, d), F32),
        compiler_params=pltpu.CompilerParams(dimension_semantics=("parallel",)),
    )(x, p)


def _colsum_call(a, b=None):
    rows, d = a.shape
    tr = _row_tile(rows, d)
    ops = (a,) if b is None else (a, b)

    def body(*refs):
        o_ref = refs[-1]

        @pl.when(pl.program_id(0) == 0)
        def _():
            o_ref[...] = jnp.zeros_like(o_ref)

        v = refs[0][...] if b is None else refs[0][...] * refs[1][...]
        o_ref[...] += jnp.sum(v, axis=0, keepdims=True)

    return pl.pallas_call(
        body, name="colsum", grid=(rows // tr,),
        in_specs=[pl.BlockSpec((tr, d), lambda i: (i, 0))] * len(ops),
        out_specs=pl.BlockSpec((1, d), lambda i: (0, 0)),
        out_shape=jax.ShapeDtypeStruct((1, d), F32),
        compiler_params=pltpu.CompilerParams(dimension_semantics=("arbitrary",)),
    )(*ops)


@jax.custom_vjp
def badd(x, p):
    return _bcast_call(x, p, False)


badd.defvjp(lambda x, p: (_bcast_call(x, p, False), None), lambda res, dy: (dy, _colsum_call(dy)))


def _head_sums(x):
    i = lax.broadcasted_iota(jnp.int32, (PAIR, PAIR), 0) // HEAD_DIM
    j = lax.broadcasted_iota(jnp.int32, (PAIR, PAIR), 1) // HEAD_DIM
    ones = jnp.where(i == j, 1.0, 0.0).astype(BF16)
    hi, lo = _split(x, 2)
    cols = [slice(p * PAIR, (p + 1) * PAIR) for p in range(x.shape[1] // PAIR)]
    return jnp.concatenate([_dg(hi[:, c], ones, False, False) + _dg(lo[:, c], ones, False, False) for c in cols], axis=1)


def _head_rms_fwd_call(x, g):
    rows, w = x.shape
    tr = _row_tile(rows, w, budget=1024 * 1024)

    def body(x_ref, g_ref, y_ref):
        xv = x_ref[...]
        rstd = lax.rsqrt(_head_sums(xv * xv) * (1.0 / HEAD_DIM) + RMS_EPS)
        y_ref[...] = (xv * rstd) * g_ref[...]

    return pl.pallas_call(
        body, name="head_rms_fwd", grid=(rows // tr,),
        in_specs=[pl.BlockSpec((tr, w), lambda i: (i, 0)), pl.BlockSpec((1, w), lambda i: (0, 0))],
        out_specs=pl.BlockSpec((tr, w), lambda i: (i, 0)),
        out_shape=jax.ShapeDtypeStruct((rows, w), F32),
        compiler_params=pltpu.CompilerParams(dimension_semantics=("parallel",)),
    )(x, g)


def _head_rms_bwd_call(x, g, dy):
    rows, w = x.shape
    tr = _row_tile(rows, w, budget=1024 * 1024)

    def body(x_ref, g_ref, dy_ref, dx_ref, dg_ref):
        @pl.when(pl.program_id(0) == 0)
        def _():
            dg_ref[...] = jnp.zeros_like(dg_ref)

        xv, dyv = x_ref[...], dy_ref[...]
        rstd = lax.rsqrt(_head_sums(xv * xv) * (1.0 / HEAD_DIM) + RMS_EPS)
        xhat = xv * rstd
        dxhat = dyv * g_ref[...]
        dx_ref[...] = rstd * (dxhat - xhat * (_head_sums(dxhat * xhat) * (1.0 / HEAD_DIM)))
        dg_ref[...] += jnp.sum(dyv * xhat, axis=0, keepdims=True)

    return pl.pallas_call(
        body, name="head_rms_bwd", grid=(rows // tr,),
        in_specs=[pl.BlockSpec((tr, w), lambda i: (i, 0)), pl.BlockSpec((1, w), lambda i: (0, 0)),
                  pl.BlockSpec((tr, w), lambda i: (i, 0))],
        out_specs=[pl.BlockSpec((tr, w), lambda i: (i, 0)), pl.BlockSpec((1, w), lambda i: (0, 0))],
        out_shape=[jax.ShapeDtypeStruct((rows, w), F32), jax.ShapeDtypeStruct((1, w), F32)],
        compiler_params=pltpu.CompilerParams(dimension_semantics=("arbitrary",)),
    )(x, g, dy)


@jax.custom_vjp
def head_rms(x, g):
    return _head_rms_fwd_call(x, g)


head_rms.defvjp(lambda x, g: (_head_rms_fwd_call(x, g), (x, g)),
                lambda res, dy: tuple(_head_rms_bwd_call(res[0], res[1], dy)))


def _gn_fwd_call(y, r, kf, v, gw, gb, rk):
    rows, w = y.shape
    tr = _row_tile(rows, w, budget=512 * 1024)

    def body(y_ref, r_ref, kf_ref, v_ref, gw_ref, gb_ref, rk_ref, o_ref):
        yv = y_ref[...]
        yc = yv - _head_sums(yv) * (1.0 / HEAD_DIM)
        rstd = lax.rsqrt(_head_sums(yc * yc) * (1.0 / HEAD_DIM) + GN_EPS)
        s = _head_sums(r_ref[...] * kf_ref[...] * rk_ref[...])
        o_ref[...] = (yc * rstd) * gw_ref[...] + gb_ref[...] + s * v_ref[...]

    tok = pl.BlockSpec((tr, w), lambda i: (i, 0))
    par = pl.BlockSpec((1, w), lambda i: (0, 0))
    return pl.pallas_call(
        body, name="gn_bonus_fwd", grid=(rows // tr,),
        in_specs=[tok] * 4 + [par] * 3, out_specs=tok,
        out_shape=jax.ShapeDtypeStruct((rows, w), F32),
        compiler_params=pltpu.CompilerParams(dimension_semantics=("parallel",)),
    )(y, r, kf, v, gw, gb, rk)


def _gn_bwd_call(y, r, kf, v, gw, gb, rk, do):
    rows, w = y.shape
    tr = _row_tile(rows, w, budget=512 * 1024)

    def body(y_ref, r_ref, kf_ref, v_ref, gw_ref, rk_ref, do_ref,
             dy_ref, dr_ref, dkf_ref, dv_ref, dgw_ref, dgb_ref, drk_ref):
        @pl.when(pl.program_id(0) == 0)
        def _():
            dgw_ref[...] = jnp.zeros_like(dgw_ref)
            dgb_ref[...] = jnp.zeros_like(dgb_ref)
            drk_ref[...] = jnp.zeros_like(drk_ref)

        yv, rv, kv, vv, dov, rkv = y_ref[...], r_ref[...], kf_ref[...], v_ref[...], do_ref[...], rk_ref[...]
        mean = lambda t: _head_sums(t) * (1.0 / HEAD_DIM)
        yc = yv - mean(yv)
        rstd = lax.rsqrt(mean(yc * yc) + GN_EPS)
        yhat = yc * rstd
        dyhat = dov * gw_ref[...]
        dy_ref[...] = rstd * (dyhat - mean(dyhat) - yhat * mean(dyhat * yhat))
        s = _head_sums(rv * kv * rkv)
        ds = _head_sums(dov * vv)
        dv_ref[...] = s * dov
        dr_ref[...] = ds * kv * rkv
        dkf_ref[...] = ds * rv * rkv
        dgw_ref[...] += jnp.sum(dov * yhat, axis=0, keepdims=True)
        dgb_ref[...] += jnp.sum(dov, axis=0, keepdims=True)
        drk_ref[...] += jnp.sum(ds * rv * kv, axis=0, keepdims=True)

    tok = pl.BlockSpec((tr, w), lambda i: (i, 0))
    par = pl.BlockSpec((1, w), lambda i: (0, 0))
    tshape = jax.ShapeDtypeStruct((rows, w), F32)
    pshape = jax.ShapeDtypeStruct((1, w), F32)
    return pl.pallas_call(
        body, name="gn_bonus_bwd", grid=(rows // tr,),
        in_specs=[tok] * 4 + [par] * 2 + [tok], out_specs=[tok] * 4 + [par] * 3,
        out_shape=[tshape] * 4 + [pshape] * 3,
        compiler_params=pltpu.CompilerParams(dimension_semantics=("arbitrary",)),
    )(y, r, kf, v, gw, rk, do)


@jax.custom_vjp
def gn_bonus(y, r, kf, v, gw, gb, rk):
    return _gn_fwd_call(y, r, kf, v, gw, gb, rk)


def _gn_bwd(res, do):
    y, r, kf, v, gw, gb, rk = res
    dy, dr, dkf, dv, dgw, dgb, drk = _gn_bwd_call(y, r, kf, v, gw, gb, rk, do)
    return dy, dr, dkf, dv, dgw, dgb, drk


gn_bonus.defvjp(lambda *a: (_gn_fwd_call(*a), a), _gn_bwd)


PREP_ROWS = 128


def _prep_segments(rw, lora_w, lora_a, lora_g):
    at = 3 * rw
    seg = {"r": (0, rw), "k": (rw, 2 * rw), "v": (2 * rw, 3 * rw)}
    for name, n in (("wd", lora_w), ("ad", lora_a), ("gd", lora_g)):
        seg[name] = (at, at + _pad128(n))
        at += _pad128(n)
    return seg, at


def _prep_shifted(z_ref, zlast_ref, mu_ref, seg, first_tile):
    lo, hi = seg
    zr = z_ref[:, lo:hi]
    rows = zr.shape[0]
    before = jnp.where(first_tile, 0.0, zlast_ref[7:8, lo:hi])
    row0 = lax.broadcasted_iota(jnp.int32, zr.shape, 0) == 0
    diff = jnp.where(row0, before, pltpu.roll(zr, 1, axis=0)) - zr
    return zr + diff * mu_ref[:, lo:hi], diff


def _prep_forward_values(z_ref, zlast_ref, mu_ref, w0_ref, a0_ref, kk_ref, ka_ref, w2_ref, a2_ref, g2_ref, segs, first_tile):
    z = {n: _prep_shifted(z_ref, zlast_ref, mu_ref, segs[n], first_tile) for n in segs}
    r, k, v, wd, ad, gd = (z[n][0] for n in ("r", "k", "v", "wd", "ad", "gd"))
    twd = jnp.tanh(wd)
    pw = _mm(twd, w2_ref[...]) + w0_ref[...]
    lw = -jnp.exp(-(jnp.maximum(-pw, 0.0) + jnp.log(1.0 + jnp.exp(-jnp.abs(pw)))) - 0.5)
    a_sig = 1.0 / (1.0 + jnp.exp(-(_mm(ad, a2_ref[...]) + a0_ref[...])))
    sg = 1.0 / (1.0 + jnp.exp(-gd))
    kx = k * kk_ref[...]
    nrm = jnp.sqrt(_head_sums(kx * kx))
    inv = 1.0 / jnp.maximum(nrm, L2_FLOOR)
    return dict(z=z, r=r, k=k, v=v, twd=twd, pw=pw, lw=lw, a_sig=a_sig, sg=sg, ad=ad, kk=kx * inv, inv=inv, live=nrm > L2_FLOOR)


def _prep_specs(tokens, rpad, rw, w2, a2, g2):
    tr = PREP_ROWS
    tile = lambda w: pl.BlockSpec((tr, w), lambda i: (i, 0))
    before = pl.BlockSpec((8, rpad), lambda i: (jnp.maximum(i * (tr // 8) - 1, 0), 0))
    whole = lambda a: pl.BlockSpec(a.shape, lambda i: (0, 0))
    par = pl.BlockSpec((1, rw), lambda i: (0, 0))
    return tile, before, whole, par, pl.BlockSpec((1, rpad), lambda i: (0, 0))


def _prep_fwd_call(zr, mu, w0, a0, k_k, k_a, w2, a2, g2):
    tokens, rpad = zr.shape
    rw = w0.shape[1]
    segs, _ = _prep_segments(rw, w2.shape[0], a2.shape[0], g2.shape[0])
    tile, before, whole, par, mu_spec = _prep_specs(tokens, rpad, rw, w2, a2, g2)

    def body(z_ref, zlast_ref, mu_ref, w0_ref, a0_ref, kk_ref, ka_ref, w2_ref, a2_ref, g2_ref,
             r_ref, lw_ref, kf_ref, v_ref, na_ref, b_ref, g_ref):
        f = _prep_forward_values(z_ref, zlast_ref, mu_ref, w0_ref, a0_ref, kk_ref, ka_ref, w2_ref, a2_ref, g2_ref,
                                 segs, pl.program_id(0) == 0)
        r_ref[...] = f["r"]
        v_ref[...] = f["v"]
        lw_ref[...] = f["lw"]
        kf_ref[...] = f["k"] * (1.0 + (f["a_sig"] - 1.0) * ka_ref[...])
        na_ref[...] = -f["kk"]
        b_ref[...] = f["kk"] * f["a_sig"]
        g_ref[...] = _mm(f["sg"], g2_ref[...])

    shape = jax.ShapeDtypeStruct((tokens, rw), F32)
    return pl.pallas_call(
        body, name="rwkv_prep_fwd", grid=(tokens // PREP_ROWS,),
        in_specs=[tile(rpad), before, mu_spec, par, par, par, par, whole(w2), whole(a2), whole(g2)],
        out_specs=[tile(rw)] * 7, out_shape=[shape] * 7,
        compiler_params=pltpu.CompilerParams(dimension_semantics=("parallel",), vmem_limit_bytes=VMEM_LIMIT_CAP),
    )(zr, zr, mu, w0, a0, k_k, k_a, w2, a2, g2)


def _prep_bwd_call(zr, mu, w0, a0, k_k, k_a, w2, a2, g2, cts):
    tokens, rpad = zr.shape
    rw = w0.shape[1]
    segs, _ = _prep_segments(rw, w2.shape[0], a2.shape[0], g2.shape[0])
    tile, before, whole, par, mu_spec = _prep_specs(tokens, rpad, rw, w2, a2, g2)
    nt = tokens // PREP_ROWS
    rev = lambda spec: pl.BlockSpec(spec.block_shape, lambda i, f=spec.index_map: f(nt - 1 - i))

    def body(z_ref, zlast_ref, mu_ref, w0_ref, a0_ref, kk_ref, ka_ref, w2_ref, a2_ref, g2_ref,
             dr_ref, dlw_ref, dkf_ref, dv_ref, dna_ref, db_ref, dg_ref,
             dz_ref, dmu_ref, dw0_ref, da0_ref, dkk_ref, dka_ref, dw2_ref, da2_ref, dg2_ref, carry):
        step = pl.program_id(0)

        @pl.when(step == 0)
        def _():
            for ref in (dmu_ref, dw0_ref, da0_ref, dkk_ref, dka_ref, dw2_ref, da2_ref, dg2_ref, carry):
                ref[...] = jnp.zeros_like(ref)

        f = _prep_forward_values(z_ref, zlast_ref, mu_ref, w0_ref, a0_ref, kk_ref, ka_ref, w2_ref, a2_ref, g2_ref,
                                 segs, step == nt - 1)
        k, kk, a_sig, sg, twd = f["k"], f["kk"], f["a_sig"], f["sg"], f["twd"]
        colsum = lambda t: jnp.sum(t, axis=0, keepdims=True)
        dkf, db, dg = dkf_ref[...], db_ref[...], dg_ref[...]
        ka = ka_ref[...]
        dgd = _mm(dg, g2_ref[...], tb=True) * sg * (1.0 - sg)
        dg2_ref[...] += _mm(sg, dg, ta=True)
        dkk = db * a_sig - dna_ref[...]
        da_sig = db * kk + dkf * k * ka
        dk = dkf * (1.0 + (a_sig - 1.0) * ka)
        dka_ref[...] += colsum(dkf * k * (a_sig - 1.0))
        along = jnp.where(f["live"], _head_sums(dkk * kk), 0.0)
        dkx = (dkk - kk * along) * f["inv"]
        dk = dk + dkx * kk_ref[...]
        dkk_ref[...] += colsum(dkx * k)
        dpa = da_sig * a_sig * (1.0 - a_sig)
        da0_ref[...] += colsum(dpa)
        dad = _mm(dpa, a2_ref[...], tb=True)
        da2_ref[...] += _mm(f["ad"], dpa, ta=True)
        dpw = dlw_ref[...] * f["lw"] / (1.0 + jnp.exp(f["pw"]))
        dw0_ref[...] += colsum(dpw)
        dwd = _mm(dpw, w2_ref[...], tb=True) * (1.0 - twd * twd)
        dw2_ref[...] += _mm(twd, dpw, ta=True)
        rows = PREP_ROWS
        last = lax.broadcasted_iota(jnp.int32, (rows, 1), 0) == rows - 1
        for name, dz in (("r", dr_ref[...]), ("k", dk), ("v", dv_ref[...]), ("wd", dwd), ("ad", dad), ("gd", dgd)):
            lo, hi = segs[name]
            mu_s = mu_ref[:, lo:hi]
            dmu_ref[:, lo:hi] += colsum(dz * f["z"][name][1])
            later = dz * mu_s
            dz_ref[:, lo:hi] = dz * (1.0 - mu_s) + jnp.where(last, carry[:, lo:hi], pltpu.roll(later, rows - 1, axis=0))
            carry[:, lo:hi] = later[0:1, :]

    tok = jax.ShapeDtypeStruct((tokens, rw), F32)
    acc = lambda a: jax.ShapeDtypeStruct(a.shape, F32)
    return pl.pallas_call(
        body, name="rwkv_prep_bwd", grid=(nt,),
        in_specs=[rev(tile(rpad)), rev(before), mu_spec, par, par, par, par, whole(w2), whole(a2), whole(g2)]
                 + [rev(tile(rw))] * 7,
        out_specs=[rev(tile(rpad)), mu_spec, par, par, par, par, whole(w2), whole(a2), whole(g2)],
        out_shape=[jax.ShapeDtypeStruct((tokens, rpad), F32), acc(mu), acc(w0), acc(a0), acc(k_k), acc(k_a), acc(w2), acc(a2), acc(g2)],
        scratch_shapes=[pltpu.VMEM((1, rpad), F32)],
        compiler_params=pltpu.CompilerParams(dimension_semantics=("arbitrary",), vmem_limit_bytes=VMEM_LIMIT_CAP),
    )(zr, zr, mu, w0, a0, k_k, k_a, w2, a2, g2, *cts)


@jax.custom_vjp
def rwkv_prep(zr, mu, w0, a0, k_k, k_a, w2, a2, g2):
    return tuple(_prep_fwd_call(zr, mu, w0, a0, k_k, k_a, w2, a2, g2))


def _rwkv_prep_bwd(res, cts):
    zr, mu, w0, a0, k_k, k_a, w2, a2, g2 = res
    dz, dmu, dw0, da0, dkk, dka, dw2, da2, dg2 = _prep_bwd_call(*res, cts)
    return dz, dmu, dw0, da0, dkk, dka, dw2.astype(w2.dtype), da2.astype(a2.dtype), dg2.astype(g2.dtype)


rwkv_prep.defvjp(lambda *a: (tuple(_prep_fwd_call(*a)), a), _rwkv_prep_bwd)


def _pair_masks(rows):
    lane = lax.broadcasted_iota(jnp.int32, (rows, PAIR), 1)
    return lane < HEAD_DIM, lane >= HEAD_DIM


def _bd(x):
    m0, m1 = _pair_masks(x.shape[0])
    return jnp.concatenate([jnp.where(m0, x, 0.0), jnp.where(m1, x, 0.0)], axis=0)


def _unbd(m, c):
    return jnp.where(_pair_masks(c)[0], m[:c], m[c:])


def _pair_a(l2, r2):
    return _mm(l2, _bd(r2), tb=True)


def _pair_mul(p2, x2):
    return _mm(p2, _bd(x2))


def _pair_mul_t(p2, x2):
    return _unbd(_mm(p2, x2, ta=True), p2.shape[0])


def _block_diag_mask():
    row = lax.broadcasted_iota(jnp.int32, (PAIR, PAIR), 0)
    lane = lax.broadcasted_iota(jnp.int32, (PAIR, PAIR), 1)
    return (row < HEAD_DIM) == (lane < HEAD_DIM), row == lane


def _wkv_pair_common(r, lw, k, a, b):
    c = r[0].shape[0]
    pairs = range(len(r))
    i = lax.broadcasted_iota(jnp.int32, (c, PAIR), 0)
    j = lax.broadcasted_iota(jnp.int32, (c, PAIR), 1) % c
    strict, incl = i > j, i >= j
    ti = lax.broadcasted_iota(jnp.int32, (c, c), 0)
    tj = lax.broadcasted_iota(jnp.int32, (c, c), 1)
    tri = jnp.where(ti >= tj, 1.0, 0.0).astype(BF16)
    lc = [sum(_dg(tri, part, False, False) for part in _split(lw[p], 3)) for p in pairs]
    lend = [lc[p][c - 1:c, :] for p in pairs]
    rt = [r[p] * jnp.exp(lc[p]) for p in pairs]
    at = [a[p] * jnp.exp(lc[p] - lw[p]) for p in pairs]
    pinv = [jnp.exp(-lc[p]) for p in pairs]
    kt = [k[p] * pinv[p] for p in pairs]
    bt = [b[p] * pinv[p] for p in pairs]
    e = [jnp.exp(lend[p] - lc[p]) for p in pairs]
    ktp = [k[p] * e[p] for p in pairs]
    btp = [b[p] * e[p] for p in pairs]
    a_ab = [jnp.where(strict, _pair_a(at[p], bt[p]), 0.0) for p in pairs]
    a_ak = [jnp.where(strict, _pair_a(at[p], kt[p]), 0.0) for p in pairs]
    a_rb = [jnp.where(incl, _pair_a(rt[p], bt[p]), 0.0) for p in pairs]
    a_rk = [jnp.where(incl, _pair_a(rt[p], kt[p]), 0.0) for p in pairs]
    t = [jnp.where(i == j, 1.0, 0.0) + a_ab[p] for p in pairs]
    xp = a_ab
    n = 2
    while n < c:
        xp = [_pair_mul(xp[p], xp[p]) for p in pairs]
        t = [t[p] + _pair_mul(t[p], xp[p]) for p in pairs]
        n *= 2
    bdm, eye = _block_diag_mask()
    pend_col = [jnp.sum(jnp.where(eye, jnp.exp(lend[p]), 0.0), axis=1, keepdims=True) for p in pairs]
    return dict(rt=rt, at=at, kt=kt, bt=bt, ktp=ktp, btp=btp, a_ak=a_ak, a_rb=a_rb, a_rk=a_rk, t=t,
                pend_col=pend_col, lend=lend, lc=lc, strict=strict, incl=incl, tri=tri, bdm=bdm)


def _wkv_group(width):
    npair = width // PAIR
    g = min(WKV_PAIRS_PER_STEP, npair)
    assert npair % g == 0
    return npair, g


def _wkv_fwd_call(r, lw, k, v, a, b):
    tokens, width = r.shape
    c = WKV_CHUNK
    nc = tokens // c
    npair, g = _wkv_group(width)

    def body(r_ref, lw_ref, k_ref, v_ref, a_ref, b_ref, y_ref, s_ref, st):
        @pl.when(pl.program_id(1) == 0)
        def _():
            st[...] = jnp.zeros_like(st)

        pairs = range(g)
        rv, lwv, kv, vv, av, bv = ([ref[:, p * PAIR:(p + 1) * PAIR] for p in pairs]
                                   for ref in (r_ref, lw_ref, k_ref, v_ref, a_ref, b_ref))
        s0 = [st[p] for p in pairs]
        q = _wkv_pair_common(rv, lwv, kv, av, bv)
        w1 = [_mm(q["at"][p], s0[p]) + _pair_mul(q["a_ak"][p], vv[p]) for p in pairs]
        u = [_pair_mul(q["t"][p], w1[p]) for p in pairs]
        y = [_mm(q["rt"][p], s0[p]) + _pair_mul(q["a_rb"][p], u[p]) + _pair_mul(q["a_rk"][p], vv[p]) for p in pairs]
        grow = [_mm(jnp.concatenate([q["btp"][p], q["ktp"][p]], axis=0), jnp.concatenate([u[p], vv[p]], axis=0), ta=True)
                for p in pairs]
        for p in pairs:
            y_ref[:, p * PAIR:(p + 1) * PAIR] = y[p]
            s_ref[0, p] = s0[p]
            st[p] = q["pend_col"][p] * s0[p] + jnp.where(q["bdm"], grow[p], 0.0)

    tok = pl.BlockSpec((c, g * PAIR), lambda gi, ci: (ci, gi))
    return pl.pallas_call(
        body, name="wkv_fwd", grid=(npair // g, nc),
        in_specs=[tok] * 6,
        out_specs=[tok, pl.BlockSpec((1, g, PAIR, PAIR), lambda gi, ci: (ci, gi, 0, 0))],
        out_shape=[jax.ShapeDtypeStruct((tokens, width), F32), jax.ShapeDtypeStruct((nc, npair, PAIR, PAIR), F32)],
        scratch_shapes=[pltpu.VMEM((g, PAIR, PAIR), F32)],
        compiler_params=pltpu.CompilerParams(dimension_semantics=("parallel", "arbitrary")),
    )(r, lw, k, v, a, b)


def _wkv_bwd_call(r, lw, k, v, a, b, s, dy):
    tokens, width = r.shape
    c = WKV_CHUNK
    nc = tokens // c
    npair, g = _wkv_group(width)

    def body(r_ref, lw_ref, k_ref, v_ref, a_ref, b_ref, s_ref, dy_ref,
             dr_ref, dlw_ref, dk_ref, dv_ref, da_ref, db_ref, dst):
        @pl.when(pl.program_id(1) == 0)
        def _():
            dst[...] = jnp.zeros_like(dst)

        pairs = range(g)
        rv, lwv, kv, vv, av, bv, dyv = ([ref[:, p * PAIR:(p + 1) * PAIR] for p in pairs]
                                        for ref in (r_ref, lw_ref, k_ref, v_ref, a_ref, b_ref, dy_ref))
        s0 = [s_ref[0, p] for p in pairs]
        dsc = [dst[p] for p in pairs]
        q = _wkv_pair_common(rv, lwv, kv, av, bv)
        rt, at, kt, bt, ktp, btp, t = (q[n] for n in ("rt", "at", "kt", "bt", "ktp", "btp", "t"))
        a_ak, a_rb, a_rk, strict, incl = (q[n] for n in ("a_ak", "a_rb", "a_rk", "strict", "incl"))
        w1 = [_mm(at[p], s0[p]) + _pair_mul(a_ak[p], vv[p]) for p in pairs]
        u = [_pair_mul(t[p], w1[p]) for p in pairs]
        du = [_pair_mul_t(a_rb[p], dyv[p]) + _mm(btp[p], dsc[p]) for p in pairs]
        dw1 = [_pair_mul_t(t[p], du[p]) for p in pairs]
        dv = [_pair_mul_t(a_rk[p], dyv[p]) + _mm(ktp[p], dsc[p]) + _pair_mul_t(a_ak[p], dw1[p]) for p in pairs]
        da_ab = [jnp.where(strict, _pair_a(dw1[p], u[p]), 0.0) for p in pairs]
        da_ak = [jnp.where(strict, _pair_a(dw1[p], vv[p]), 0.0) for p in pairs]
        da_rb = [jnp.where(incl, _pair_a(dyv[p], u[p]), 0.0) for p in pairs]
        da_rk = [jnp.where(incl, _pair_a(dyv[p], vv[p]), 0.0) for p in pairs]
        d_rt = [_mm(dyv[p], s0[p], tb=True) + _pair_mul(da_rb[p], bt[p]) + _pair_mul(da_rk[p], kt[p]) for p in pairs]
        d_at = [_mm(dw1[p], s0[p], tb=True) + _pair_mul(da_ab[p], bt[p]) + _pair_mul(da_ak[p], kt[p]) for p in pairs]
        d_bt = [_pair_mul_t(da_ab[p], at[p]) + _pair_mul_t(da_rb[p], rt[p]) for p in pairs]
        d_kt = [_pair_mul_t(da_ak[p], at[p]) + _pair_mul_t(da_rk[p], rt[p]) for p in pairs]
        d_btp = [_mm(u[p], dsc[p], tb=True) for p in pairs]
        d_ktp = [_mm(vv[p], dsc[p], tb=True) for p in pairs]
        ones = jnp.ones((8, PAIR), BF16)
        dpend = [sum(_dg(ones, part, False, True) for part in _split(dsc[p] * s0[p], 3))[0:1, :] * jnp.exp(q["lend"][p])
                 for p in pairs]
        grow = [_mm(jnp.concatenate([rt[p], at[p]], axis=0), jnp.concatenate([dyv[p], dw1[p]], axis=0), ta=True)
                for p in pairs]
        last = lax.broadcasted_iota(jnp.int32, (c, PAIR), 0) == c - 1
        for p in pairs:
            sl = slice(p * PAIR, (p + 1) * PAIR)
            dst[p] = q["pend_col"][p] * dsc[p] + jnp.where(q["bdm"], grow[p], 0.0)
            lc_e = d_ktp[p] * ktp[p] + d_btp[p] * btp[p]
            dlend = jnp.sum(lc_e, axis=0, keepdims=True) + dpend[p]
            dlc = d_rt[p] * rt[p] - d_kt[p] * kt[p] - d_bt[p] * bt[p] - lc_e + jnp.where(last, dlend, 0.0)
            dlp = d_at[p] * at[p]
            dlw_ref[:, sl] = sum(_dg(q["tri"], part, True, False) for part in _split(dlc + dlp, 3)) - dlp
            lc = q["lc"][p]
            pinv = jnp.exp(-lc)
            e = jnp.exp(q["lend"][p] - lc)
            dr_ref[:, sl] = d_rt[p] * jnp.exp(lc)
            da_ref[:, sl] = d_at[p] * jnp.exp(lc - lwv[p])
            dk_ref[:, sl] = d_kt[p] * pinv + d_ktp[p] * e
            db_ref[:, sl] = d_bt[p] * pinv + d_btp[p] * e
            dv_ref[:, sl] = dv[p]

    tok = pl.BlockSpec((c, g * PAIR), lambda gi, ci: (nc - 1 - ci, gi))
    tshape = jax.ShapeDtypeStruct((tokens, width), F32)
    return pl.pallas_call(
        body, name="wkv_bwd", grid=(npair // g, nc),
        in_specs=[tok] * 6 + [pl.BlockSpec((1, g, PAIR, PAIR), lambda gi, ci: (nc - 1 - ci, gi, 0, 0)), tok],
        out_specs=[tok] * 6, out_shape=[tshape] * 6,
        scratch_shapes=[pltpu.VMEM((g, PAIR, PAIR), F32)],
        compiler_params=pltpu.CompilerParams(dimension_semantics=("parallel", "arbitrary")),
    )(r, lw, k, v, a, b, s, dy)


@jax.custom_vjp
def wkv7(r, lw, k, v, a, b):
    return _wkv_fwd_call(r, lw, k, v, a, b)[0]


def _wkv7_fwd(r, lw, k, v, a, b):
    y, s = _wkv_fwd_call(r, lw, k, v, a, b)
    return y, (r, lw, k, v, a, b, s)


wkv7.defvjp(_wkv7_fwd, lambda res, dy: tuple(_wkv_bwd_call(*res, dy)))


def _attn_block(tokens):
    return ATTN_BLOCK_BIG if tokens % ATTN_BLOCK_BIG == 0 else ATTN_BLOCK


def _fox_layouts(cum):
    tokens, heads = cum.shape
    t = _attn_block(tokens)
    cq = cum.reshape(tokens, heads // 2, 2).transpose(1, 0, 2)
    ck = cum.T.reshape(heads // 2, 2, tokens // t, t).transpose(0, 2, 1, 3)
    return cq, ck


def _head_lane_masks(rows):
    lane = lax.broadcasted_iota(jnp.int32, (rows, 2 * HEAD_DIM), 1)
    return [lane < HEAD_DIM, lane >= HEAD_DIM]


def _fox_fwd_call(q, k, v, cq, ck):
    tokens, width = q.shape
    t = _attn_block(tokens)
    nb = tokens // t
    hd = HEAD_DIM
    npair = width // (2 * hd)

    def body(q_ref, k_ref, v_ref, cq_ref, ck_ref, o_ref, lse_ref):
        i = pl.program_id(1)
        masks = _head_lane_masks(t)
        q2 = q_ref[...]
        qs = [jnp.where(mk, q2, 0.0).astype(BF16) for mk in masks]
        cqs = [cq_ref[0, :, hh:hh + 1] for hh in range(2)]

        def block(j, carry, diagonal):
            off = pl.multiple_of(j * t, t)
            ckj = ck_ref[0, j]
            k2 = k_ref[pl.ds(off, t), :].astype(BF16)
            v2 = v_ref[pl.ds(off, t), :].astype(BF16)
            out = []
            for hh in range(2):
                m, l, acc = carry[hh]
                s = _dg(qs[hh], k2, False, True) + (cqs[hh] - ckj[hh:hh + 1, :])
                if diagonal:
                    keep = lax.broadcasted_iota(jnp.int32, (t, t), 0) >= lax.broadcasted_iota(jnp.int32, (t, t), 1)
                    s = jnp.where(keep, s, NEG_BIG)
                m_new = jnp.maximum(m, jnp.max(s, axis=1, keepdims=True))
                alpha = jnp.exp(m - m_new)
                p = jnp.exp(s - m_new)
                l = alpha * l + jnp.sum(p, axis=1, keepdims=True)
                acc = alpha * acc + _dg(p.astype(BF16), v2, False, False)
                out.append((m_new, l, acc))
            return tuple(out)

        init = tuple((jnp.full((t, 1), NEG_BIG, F32), jnp.zeros((t, 1), F32), jnp.zeros((t, 2 * hd), F32)) for _ in range(2))
        res = lax.fori_loop(0, i, lambda j, c: block(j, c, False), init)
        res = block(i, res, True)
        o_ref[...] = jnp.where(masks[0], res[0][2] / res[0][1], res[1][2] / res[1][1])
        for hh in range(2):
            lse_ref[0, :, hh:hh + 1] = res[hh][0] + jnp.log(res[hh][1])

    blk = pl.BlockSpec((t, 2 * hd), lambda hp, i: (i, hp))
    full = pl.BlockSpec((tokens, 2 * hd), lambda hp, i: (0, hp))
    cq_spec = pl.BlockSpec((1, t, 2), lambda hp, i: (hp, i, 0))
    ck_spec = pl.BlockSpec((1, nb, 2, t), lambda hp, i: (hp, 0, 0, 0))
    return pl.pallas_call(
        body, name="fox_fwd", grid=(npair, nb),
        in_specs=[blk, full, full, cq_spec, ck_spec],
        out_specs=[blk, cq_spec],
        out_shape=[jax.ShapeDtypeStruct((tokens, width), F32), jax.ShapeDtypeStruct((npair, tokens, 2), F32)],
        compiler_params=pltpu.CompilerParams(dimension_semantics=("parallel", "arbitrary")),
    )(q, k, v, cq, ck)


def _fox_bwd_call(q, k, v, cq, ck, o, lse, do):
    tokens, width = q.shape
    t = _attn_block(tokens)
    nb = tokens // t
    hd = HEAD_DIM
    npair = width // (2 * hd)

    def body(q_ref, k_ref, v_ref, cq_ref, ck_ref, o_ref, lse_ref, do_ref, dq_ref, dk_ref, dv_ref, dck_ref, dcq_ref):
        i = pl.program_id(1)

        @pl.when(i == 0)
        def _():
            dk_ref[...] = jnp.zeros_like(dk_ref)
            dv_ref[...] = jnp.zeros_like(dv_ref)
            dck_ref[...] = jnp.zeros_like(dck_ref)

        masks = _head_lane_masks(t)
        q2, do2, o2 = q_ref[...], do_ref[...], o_ref[...]
        qs = [jnp.where(mk, q2, 0.0).astype(BF16) for mk in masks]
        dos = [jnp.where(mk, do2, 0.0).astype(BF16) for mk in masks]
        deltas = [jnp.sum(dos[hh].astype(F32) * o2, axis=1, keepdims=True) for hh in range(2)]
        bias = [cq_ref[0, :, hh:hh + 1] - lse_ref[0, :, hh:hh + 1] for hh in range(2)]

        def block(j, carry, diagonal):
            off = pl.multiple_of(j * t, t)
            ckj = ck_ref[0, j]
            k2 = k_ref[pl.ds(off, t), :].astype(BF16)
            v2 = v_ref[pl.ds(off, t), :].astype(BF16)
            out = []
            dk2 = jnp.zeros((t, 2 * hd), F32)
            dv2 = jnp.zeros((t, 2 * hd), F32)
            for hh in range(2):
                s = _dg(qs[hh], k2, False, True) + (bias[hh] - ckj[hh:hh + 1, :])
                if diagonal:
                    keep = lax.broadcasted_iota(jnp.int32, (t, t), 0) >= lax.broadcasted_iota(jnp.int32, (t, t), 1)
                    s = jnp.where(keep, s, NEG_BIG)
                p = jnp.exp(s)
                dp = _dg(dos[hh], v2, False, True)
                ds = p * (dp - deltas[hh])
                dsb = ds.astype(BF16)
                dq, rowsum = carry[hh]
                out.append((dq + _dg(dsb, k2, False, False), rowsum + jnp.sum(ds, axis=1, keepdims=True)))
                dk2 = dk2 + _dg(dsb, qs[hh], True, False)
                dv2 = dv2 + _dg(p.astype(BF16), dos[hh], True, False)
                dck_ref[0, j, hh:hh + 1, :] -= jnp.sum(ds, axis=0, keepdims=True)
            dk_ref[pl.ds(off, t), :] += dk2
            dv_ref[pl.ds(off, t), :] += dv2
            return tuple(out)

        init = tuple((jnp.zeros((t, 2 * hd), F32), jnp.zeros((t, 1), F32)) for _ in range(2))
        res = lax.fori_loop(0, i, lambda j, c: block(j, c, False), init)
        res = block(i, res, True)
        dq_ref[...] = jnp.where(masks[0], res[0][0], res[1][0])
        for hh in range(2):
            dcq_ref[0, :, hh:hh + 1] = res[hh][1]

    blk = pl.BlockSpec((t, 2 * hd), lambda hp, i: (i, hp))
    full = pl.BlockSpec((tokens, 2 * hd), lambda hp, i: (0, hp))
    cq_spec = pl.BlockSpec((1, t, 2), lambda hp, i: (hp, i, 0))
    ck_spec = pl.BlockSpec((1, nb, 2, t), lambda hp, i: (hp, 0, 0, 0))
    tshape = jax.ShapeDtypeStruct((tokens, width), F32)
    return pl.pallas_call(
        body, name="fox_bwd", grid=(npair, nb),
        in_specs=[blk, full, full, cq_spec, ck_spec, blk, cq_spec, blk],
        out_specs=[blk, full, full, ck_spec, cq_spec],
        out_shape=[tshape, tshape, tshape, jax.ShapeDtypeStruct((npair, nb, 2, t), F32),
                   jax.ShapeDtypeStruct((npair, tokens, 2), F32)],
        compiler_params=pltpu.CompilerParams(dimension_semantics=("parallel", "arbitrary")),
    )(q, k, v, cq, ck, o, lse, do)


@jax.custom_vjp
def fox_attention(q, k, v, cum):
    return _fox_fwd_call(q, k, v, *_fox_layouts(cum))[0]


def _fox_fwd(q, k, v, cum):
    cq, ck = _fox_layouts(cum)
    o, lse = _fox_fwd_call(q, k, v, cq, ck)
    return o, (q, k, v, cq, ck, o, lse)


def _fox_bwd(res, do):
    q, k, v, cq, ck, o, lse = res
    dq, dk, dv, dck, dcq = _fox_bwd_call(q, k, v, cq, ck, o, lse, do)
    npair, nb, _, t = dck.shape
    dcum = dck.transpose(0, 2, 1, 3).reshape(2 * npair, nb * t).T + dcq.transpose(1, 0, 2).reshape(nb * t, 2 * npair)
    return dq, dk, dv, dcum


fox_attention.defvjp(_fox_fwd, _fox_bwd)


def _loss_call(y, target):
    rows, d = y.shape
    tr = _row_tile(rows, d)

    def body(y_ref, t_ref, loss_ref, dy_ref):
        @pl.when(pl.program_id(0) == 0)
        def _():
            loss_ref[...] = jnp.zeros_like(loss_ref)

        diff = y_ref[...] - t_ref[...]
        dy_ref[...] = diff * (1.0 / d)
        loss_ref[...] += (0.5 / d) * jnp.sum(jnp.sum(diff * diff, axis=1, keepdims=True), axis=0, keepdims=True)

    return pl.pallas_call(
        body, name="loss", grid=(rows // tr,),
        in_specs=[pl.BlockSpec((tr, d), lambda i: (i, 0))] * 2,
        out_specs=[pl.BlockSpec((1, 1), lambda i: (0, 0)), pl.BlockSpec((tr, d), lambda i: (i, 0))],
        out_shape=[jax.ShapeDtypeStruct((1, 1), F32), jax.ShapeDtypeStruct((rows, d), F32)],
        compiler_params=pltpu.CompilerParams(dimension_semantics=("arbitrary",)),
    )(y, target)


def _adamw_call(w, g, m, v):
    rows, cols = w.shape
    tr = _row_tile_ragged(rows, cols, budget=1024 * 1024)
    c1 = 1.0 / (1.0 - ADAM_B1 ** ADAM_STEP)
    c2 = 1.0 / (1.0 - ADAM_B2 ** ADAM_STEP)

    def body(w_ref, g_ref, m_ref, v_ref, d_ref, nm_ref, nv_ref):
        gv = g_ref[...]
        nm = ADAM_B1 * m_ref[...] + (1.0 - ADAM_B1) * gv
        nv = ADAM_B2 * v_ref[...] + (1.0 - ADAM_B2) * (gv * gv)
        nm_ref[...] = nm
        nv_ref[...] = nv
        d_ref[...] = -ADAM_LR * ((nm * c1) / (jnp.sqrt(nv * c2) + ADAM_EPS) + ADAM_WD * w_ref[...])

    spec = pl.BlockSpec((tr, cols), lambda i: (i, 0))
    shape = jax.ShapeDtypeStruct((rows, cols), F32)
    return pl.pallas_call(
        body, name="adamw", grid=(pl.cdiv(rows, tr),),
        in_specs=[spec] * 4, out_specs=[spec] * 3, out_shape=[shape] * 3,
        compiler_params=pltpu.CompilerParams(dimension_semantics=("parallel",)),
    )(w, g, m, v)


def _my_place():
    return lax.axis_index("x"), lax.axis_index("y"), lax.axis_index("c")


def _place_index(px, py, pc):
    return 4 * px + 2 * py + pc


HBM_SPEC = pl.BlockSpec(memory_space=pltpu.HBM)


def _all_gather_call(block):
    def body(x_ref, out_ref, send_sems, recv_sems, local_sem):
        x, y, c = _my_place()
        me, sibling = (x, y, c), (x, y, 1 - c)
        chips = [(1 - x, y), (x, 1 - y), (1 - x, 1 - y)]

        def slot(px, py, pc):
            return out_ref.at[_place_index(px, py, pc)]

        def copy(k, blk, to, src=None):
            return pltpu.make_async_remote_copy(
                src_ref=slot(*blk) if src is None else src, dst_ref=slot(*blk),
                send_sem=send_sems.at[k], recv_sem=recv_sems.at[k],
                device_id=to, device_id_type=pl.DeviceIdType.MESH)

        mine = pltpu.make_async_copy(x_ref, slot(*me), local_sem)
        mine.start()
        first = [copy(0, me, sibling, src=x_ref)]
        first += [copy(1 + j, me, (*chip, c), src=x_ref) for j, chip in enumerate(chips)]
        for cp in first:
            cp.start()
        passed = [copy(4 + j, (*chip, c), sibling) for j, chip in enumerate(chips)]
        for j, chip in enumerate(chips):
            copy(1 + j, (*chip, c), me).wait_recv()
            passed[j].start()
        copy(0, sibling, me).wait_recv()
        for j, chip in enumerate(chips):
            copy(4 + j, (*chip, 1 - c), me).wait_recv()
        for cp in first + passed:
            cp.wait_send()
        mine.wait()

    return pl.pallas_call(
        body, name="all_gather",
        out_shape=jax.ShapeDtypeStruct((N_DEV,) + block.shape, block.dtype),
        in_specs=[HBM_SPEC], out_specs=HBM_SPEC,
        scratch_shapes=[pltpu.SemaphoreType.DMA((7,)), pltpu.SemaphoreType.DMA((7,)), pltpu.SemaphoreType.DMA],
    )(block)


SEM_SPEC = pl.BlockSpec(memory_space=pltpu.SEMAPHORE)
SIDE_EFFECT = pltpu.SideEffectType.DATAFLOW_SIDE_EFFECTING


def _peers():
    x, y, c = _my_place()
    out = []
    for k in range(1, N_DEV):
        peer = (x ^ (k >> 2), y ^ ((k >> 1) & 1), c ^ (k & 1))
        out.append((k - 1, peer, _place_index(*peer)))
    return _place_index(x, y, c), out


def _spread_start(src, per_peer, name, after=None):
    slot = src.shape[1:] if per_peer else src.shape
    order = () if after is None else (after,)

    def body(src_ref, land_ref, *rest):
        send_sems, recv_sems, src_thru, land_thru, token = rest[len(order):]
        mine, peers = _peers()
        for k, peer, peer_idx in peers:
            pltpu.make_async_remote_copy(
                src_ref=src_ref.at[peer_idx] if per_peer else src_ref, dst_ref=land_ref.at[mine],
                send_sem=send_sems.at[k], recv_sem=recv_sems.at[k],
                device_id=peer, device_id_type=pl.DeviceIdType.MESH).start()
        token[...] = jnp.zeros_like(token)

    return pl.pallas_call(
        body, name=name,
        out_shape=(pltpu.SemaphoreType.DMA((N_DEV - 1,)), pltpu.SemaphoreType.DMA((N_DEV - 1,)),
                   pltpu.HBM(src.shape, src.dtype), pltpu.HBM((N_DEV,) + slot, src.dtype),
                   jax.ShapeDtypeStruct((8, 128), F32)),
        in_specs=(HBM_SPEC, HBM_SPEC) + (pl.BlockSpec(memory_space=pl.ANY),) * len(order),
        out_specs=(SEM_SPEC, SEM_SPEC, HBM_SPEC, HBM_SPEC, pl.BlockSpec(memory_space=pltpu.VMEM)),
        input_output_aliases={0: 2, 1: 3},
        compiler_params=pltpu.CompilerParams(has_side_effects=SIDE_EFFECT),
    )(pltpu.with_memory_space_constraint(src, pltpu.HBM),
      pltpu.with_memory_space_constraint(lax.empty((N_DEV,) + slot, src.dtype), pltpu.HBM), *order)


def _spread_wait(handles, after, per_peer, name):
    send_sems, recv_sems, src_thru, land_thru = handles

    def body(src_ref, land_ref, send_sems, recv_sems, after_ref, src_dead, got_ref):
        _, peers = _peers()
        for k, peer, peer_idx in peers:
            copy = pltpu.make_async_remote_copy(
                src_ref=src_ref.at[peer_idx] if per_peer else src_ref, dst_ref=land_ref.at[peer_idx],
                send_sem=send_sems.at[k], recv_sem=recv_sems.at[k],
                device_id=peer, device_id_type=pl.DeviceIdType.MESH)
            copy.wait_send()
            copy.wait_recv()

    return pl.pallas_call(
        body, name=name,
        out_shape=(pltpu.HBM(src_thru.shape, src_thru.dtype), pltpu.HBM(land_thru.shape, land_thru.dtype)),
        in_specs=(HBM_SPEC, HBM_SPEC, SEM_SPEC, SEM_SPEC, pl.BlockSpec(memory_space=pl.ANY)),
        out_specs=(HBM_SPEC, HBM_SPEC), input_output_aliases={0: 0, 1: 1},
        compiler_params=pltpu.CompilerParams(has_side_effects=SIDE_EFFECT),
    )(src_thru, land_thru, send_sems, recv_sems, after)


def _sum_slots_call(slots):
    _, rows, cols = slots.shape
    tr = _row_tile_ragged(rows, cols, budget=512 * 1024)

    def body(s_ref, o_ref):
        acc = s_ref[0].astype(F32)
        for j in range(1, N_DEV):
            acc = acc + s_ref[j].astype(F32)
        o_ref[...] = acc

    return pl.pallas_call(
        body, name="sum_slots", grid=(pl.cdiv(rows, tr),),
        in_specs=[pl.BlockSpec((N_DEV, tr, cols), lambda i: (0, i, 0))],
        out_specs=pl.BlockSpec((tr, cols), lambda i: (i, 0)),
        out_shape=jax.ShapeDtypeStruct((rows, cols), F32),
        compiler_params=pltpu.CompilerParams(dimension_semantics=("parallel",)),
    )(slots)


def _with_own_slot(got, own, mine):
    return lax.dynamic_update_index_in_dim(got, own, mine, 0)


def _pack(vectors, width):
    flat = jnp.concatenate([v.reshape(-1) for v in vectors])
    return jnp.pad(flat, (0, width - flat.shape[0])).reshape(width // 128, 128)


def _unpack(packed, like):
    flat = packed.reshape(-1)
    out, at = [], 0
    for v in like:
        out.append(flat[at:at + v.size].reshape(v.shape))
        at += v.size
    return tuple(out)


def _sum_over_devices(grads):
    n = sum(v.size for v in grads)
    width = -(-n // 1024) * 1024
    return _unpack(_sum_slots_call(_all_gather_call(_pack(grads, width))), grads)


def _cols_from_slots(slots):
    n, rows, cols = slots.shape
    return slots.transpose(1, 0, 2).reshape(rows, n * cols)


def _rows_from_slots(slots):
    return slots.reshape(-1, slots.shape[2])


def _pad128(n):
    return -(-n // 128) * 128


def _pad_to_tiles(a, axis):
    n = a.shape[axis]
    pads = [(0, 0)] * a.ndim
    pads[axis] = (0, _pad128(n) - n)
    return jnp.pad(a, pads)


def _rwkv_group(a, rw, dl, al, gl, axis):
    take = lambda lo, hi: lax.slice_in_dim(a, lo, hi, axis=axis)
    at = 3 * rw
    parts = [take(0, at)]
    for n in (dl, al, gl):
        parts.append(_pad_to_tiles(take(at, at + n), axis))
        at += n
    return jnp.concatenate(parts, axis=axis)


def _in_proj_layout(slots, rw, fw, dl, al, gl):
    wt = _rows_from_slots(slots)
    rcols = 3 * rw + dl + al + gl
    fcols = 3 * fw + fw // HEAD_DIM
    return _rwkv_group(wt[:rcols], rw, dl, al, gl, 0), _pad_to_tiles(wt[rcols:rcols + fcols], 0), wt[rcols + fcols:]


def _low_rank_layout(slots):
    return _pad_to_tiles(_cols_from_slots(slots), 0)


def _stage_embed(meta, x, n1, lp):
    h0 = jnp.concatenate([meta, x, jnp.zeros((lp - meta.shape[0] - x.shape[0], x.shape[1]), F32)], axis=0)
    return h0, rmsnorm(h0, n1)


def _stage_mix(z_r, z_f, z_g, small, w2, a2, g2, dims):
    (mu, w0, a0, k_k, k_a, r_k, gn_w, gn_b, q_g, k_g, f_bias) = small
    rw, fw, dl, al, gl = dims
    fcols = 3 * fw + fw // HEAD_DIM

    r, lw, kf, v, na, b, g = rwkv_prep(z_r, _rwkv_group(mu, rw, dl, al, gl, 1), w0, a0, k_k, k_a, w2, a2, g2)
    y = wkv7(r, lw, kf, v, na, b)
    y_a = gn_bonus(y, r, kf, v, gn_w, gn_b, r_k.reshape(1, rw)) * g

    fq, fk, fv, fl = z_f[:, :fw], z_f[:, fw:2 * fw], z_f[:, 2 * fw:3 * fw], z_f[:, 3 * fw:fcols]
    fq = head_rms(fq, jnp.tile(q_g, (1, fw // HEAD_DIM))) * (HEAD_DIM ** -0.5)
    fk = head_rms(fk, jnp.tile(k_g, (1, fw // HEAD_DIM)))
    cum = jnp.cumsum(jax.nn.log_sigmoid(badd(fl, f_bias)), axis=0)
    y_b = fox_attention(fq, fk, fv, cum)
    return y_a, y_b, jax.nn.sigmoid(z_g)


def _stage_merge(h0, y_a, y_b, gates, w_a, w_b, w_o):
    d = h0.shape[1]
    merged = gates[:, :d] * dense_cols(y_a, w_a) + gates[:, d:] * dense_cols(y_b, w_b)
    return h0 + dense(merged, w_o)


def _stage_ffn(h1, n2, w_gu, w_dn):
    gu = dense_cols(rmsnorm(h1, n2), w_gu)
    dff = w_dn.shape[0]
    return h1 + dense(jax.nn.silu(gu[:, :dff]) * gu[:, dff:], w_dn)


SHARDED = ("meta_tokens", "w_in", "rwkv_w2", "rwkv_a2", "rwkv_g2", "w_branch_a", "w_branch_b", "w_o", "w_gate_up", "w_down")
SMALL = ("norm1_g", "rwkv_mu", "rwkv_w0", "rwkv_a0", "rwkv_k_k", "rwkv_k_a", "rwkv_r_k", "rwkv_gn_w", "rwkv_gn_b",
         "fox_q_norm_g", "fox_k_norm_g", "fox_f_bias", "norm2_g")
WEIGHTS = ("meta_tokens", "norm1_g", "w_in", "rwkv_mu", "rwkv_w0", "rwkv_w2", "rwkv_a0", "rwkv_a2", "rwkv_g2", "rwkv_k_k",
           "rwkv_k_a", "rwkv_r_k", "rwkv_gn_w", "rwkv_gn_b", "fox_q_norm_g", "fox_k_norm_g", "fox_f_bias", "w_branch_a",
           "w_branch_b", "w_o", "norm2_g", "w_gate_up", "w_down")


def _as2d(a):
    return a.reshape(-1, a.shape[-1])


def kernel(x, meta_tokens, norm1_g, w_in, rwkv_mu, rwkv_w0, rwkv_w2, rwkv_a0, rwkv_a2, rwkv_g2, rwkv_k_k, rwkv_k_a, rwkv_r_k, rwkv_gn_w, rwkv_gn_b, fox_q_norm_g, fox_k_norm_g, fox_f_bias, w_branch_a, w_branch_b, w_o, norm2_g, w_gate_up, w_down, loss_target, m_meta_tokens, m_norm1_g, m_w_in, m_rwkv_mu, m_rwkv_w0, m_rwkv_w2, m_rwkv_a0, m_rwkv_a2, m_rwkv_g2, m_rwkv_k_k, m_rwkv_k_a, m_rwkv_r_k, m_rwkv_gn_w, m_rwkv_gn_b, m_fox_q_norm_g, m_fox_k_norm_g, m_fox_f_bias, m_w_branch_a, m_w_branch_b, m_w_o, m_norm2_g, m_w_gate_up, m_w_down, v_meta_tokens, v_norm1_g, v_w_in, v_rwkv_mu, v_rwkv_w0, v_rwkv_w2, v_rwkv_a0, v_rwkv_a2, v_rwkv_g2, v_rwkv_k_k, v_rwkv_k_a, v_rwkv_r_k, v_rwkv_gn_w, v_rwkv_gn_b, v_fox_q_norm_g, v_fox_k_norm_g, v_fox_f_bias, v_w_branch_a, v_w_branch_b, v_w_o, v_norm2_g, v_w_gate_up, v_w_down):
    given = dict(locals())
    w = {n: given[n] for n in WEIGHTS}
    assert rwkv_r_k.shape[-1] == HEAD_DIM
    n_meta, seq = meta_tokens.shape[0], x.shape[1]
    tokens = n_meta + seq
    lp = -(-tokens // TOKEN_TILE) * TOKEN_TILE
    mine = _place_index(*(lax.axis_index(a) for a in MESH_AXES))
    x2 = x[0]

    local = {n: _as2d(given[n]) for n in given if n != "x" and n != "loss_target"}
    for n in ("w_in", "m_w_in", "v_w_in"):
        local[n] = jnp.transpose(given[n][0])
    blocks = {n: local[n].astype(F32 if n == "meta_tokens" else BF16) for n in SHARDED}
    first = ("meta_tokens", "rwkv_w2", "rwkv_a2", "rwkv_g2")
    started = {n: _spread_start(blocks[n], False, "gather_start_" + n) for n in first}
    zero = sum(started[n][4][0, 0] for n in first)

    def gathered(n, after):
        own, got = _spread_wait(started[n][:4], after, False, "gather_wait_" + n)
        return _with_own_slot(got, own, mine)

    sm = {n: _as2d(w[n]) for n in SMALL}
    small_mix = tuple(sm[n] for n in SMALL[1:-1])
    n1 = sm["norm1_g"] + zero
    rw, fw = w_branch_a.shape[-2], w_branch_b.shape[-2]
    dims = (rw, fw, rwkv_w2.shape[-2], rwkv_a2.shape[-2], rwkv_g2.shape[-2])
    same = lambda s: (s,)

    meta, un_meta = jax.vjp(_cols_from_slots, gathered("meta_tokens", x2))
    (h0, xn), vjp_embed = jax.vjp(lambda m, xs, g: _stage_embed(m, xs, g, lp), meta, x2, n1)
    in_slots = _all_gather_call(blocks["w_in"])
    later = [n for n in SHARDED if n not in first and n != "w_in"]
    started.update({n: _spread_start(blocks[n], False, "gather_start_" + n, after=in_slots) for n in later})
    w_groups, un_in = jax.vjp(lambda s: _in_proj_layout(s, *dims), in_slots)
    xn_b = xn.astype(BF16)
    behind = sum(started[n][4] for n in later)
    z_r, z_f, z_g = (_matmul(xn_b, wg, tb=True, name="in_proj_" + tag, after=behind) for wg, tag in zip(w_groups, "rfg"))
    (w2, un_w2), (a2, un_a2), (g2, un_g2) = (jax.vjp(_low_rank_layout, gathered(n, xn)) for n in ("rwkv_w2", "rwkv_a2", "rwkv_g2"))
    (y_a, y_b, gates), vjp_mix = jax.vjp(lambda zr, zf, zg, s, a, b, c: _stage_mix(zr, zf, zg, s, a, b, c, dims),
                                         z_r, z_f, z_g, small_mix, w2, a2, g2)
    w_a, w_b = gathered("w_branch_a", y_a), gathered("w_branch_b", y_a)
    w_o_full, un_wo = jax.vjp(_rows_from_slots, gathered("w_o", y_a))
    h1, vjp_merge = jax.vjp(_stage_merge, h0, y_a, y_b, gates, w_a, w_b, w_o_full)
    w_gu = gathered("w_gate_up", h1)
    w_dn, un_dn = jax.vjp(_rows_from_slots, gathered("w_down", h1))
    y, vjp_ffn = jax.vjp(_stage_ffn, h1, sm["norm2_g"], w_gu, w_dn)

    loss_part, dy_real = _loss_call(y[n_meta:tokens], loss_target[0])
    dy = jnp.pad(dy_real, ((n_meta, lp - tokens), (0, 0)))
    loss = lax.psum(loss_part[0, 0], MESH_AXES)

    sent = {}

    def send_grad(n, dmat, unlayout):
        sent[n] = _spread_start(unlayout(dmat)[0], True, "grad_start_" + n)
        return sent[n][4][0, 0]

    d_h1, d_n2, d_wgu, d_wdn = vjp_ffn(dy)
    behind = send_grad("w_gate_up", d_wgu, same) + send_grad("w_down", d_wdn, un_dn)
    d_h0, d_ya, d_yb, d_gates, d_wa, d_wb, d_wo = vjp_merge(d_h1 + behind)
    behind = send_grad("w_o", d_wo, un_wo) + send_grad("w_branch_a", d_wa, same) + send_grad("w_branch_b", d_wb, same)
    d_zr, d_zf, d_zg, d_small_mix, d_w2, d_a2, d_g2 = vjp_mix((d_ya + behind, d_yb, d_gates))
    dproj_b = jnp.concatenate([d_zr.astype(BF16), d_zf.astype(BF16), d_zg.astype(BF16)], axis=1)
    d_wcat = _matmul(dproj_b, xn_b, ta=True, out_dtype=BF16, name="in_proj_dw")
    ends = (w_groups[0].shape[0], w_groups[0].shape[0] + w_groups[1].shape[0])
    send_grad("w_in", (d_wcat[:ends[0]], d_wcat[ends[0]:ends[1]], d_wcat[ends[1]:]), un_in)
    d_xn = _matmul(dproj_b, jnp.concatenate(w_groups, axis=0), out_dtype=F32, name="in_proj_dx", after=sent["w_in"][4])
    send_grad("rwkv_w2", d_w2, un_w2)
    send_grad("rwkv_a2", d_a2, un_a2)
    send_grad("rwkv_g2", d_g2, un_g2)
    d_meta, g_x, d_n1 = vjp_embed((d_h0, d_xn))
    send_grad("meta_tokens", d_meta, un_meta)

    grads = dict(zip(SMALL, _sum_over_devices((d_n1, *d_small_mix, d_n2))))
    grads = {n: g.reshape(w[n].shape) for n, g in grads.items()}

    delta, new_m, new_v = {}, {}, {}
    after = g_x
    for n in ("w_gate_up", "w_down", "w_o", "w_branch_a", "w_branch_b", "rwkv_g2", "rwkv_a2", "rwkv_w2", "meta_tokens", "w_in"):
        src, got = _spread_wait(sent[n][:4], after, True, "grad_wait_" + n)
        g = _sum_slots_call(_with_own_slot(got, lax.dynamic_index_in_dim(src, mine, 0, keepdims=False), mine))
        d_, m_, v_ = _adamw_call(local[n], g, local["m_" + n], local["v_" + n])
        back = (lambda t: jnp.transpose(t)[None]) if n == "w_in" else (lambda t: t.reshape(w[n].shape))
        grads[n], delta[n], new_m[n], new_v[n] = (back(t) for t in (g, d_, m_, v_))
        after = m_
    n_small = sum(w[n].size for n in SMALL)
    width = -(-n_small // 1024) * 1024
    packs = [_pack([src[n] if p == "" else given[p + n] for n in SMALL], width)
             for p, src in (("", w), ("", grads), ("m_", None), ("v_", None))]
    like = [w[n] for n in SMALL]
    for out, packed in zip((delta, new_m, new_v), _adamw_call(*packs)):
        out.update(dict(zip(SMALL, _unpack(packed, like))))

    return (loss, g_x[None], *[grads[n] for n in WEIGHTS], *[delta[n] for n in WEIGHTS],
            *[new_m[n] for n in WEIGHTS], *[new_v[n] for n in WEIGHTS])
```

```python
import functools

import jax
import jax.numpy as jnp
from jax import lax
from jax.experimental import pallas as pl
from jax.experimental.pallas import tpu as pltpu

F32 = jnp.float32
BF16 = jnp.bfloat16

N_DEV = 8
MESH_AXES = ("x", "y", "c")
HEAD_DIM = 64
TOKEN_TILE = 128
WKV_CHUNK = 64
WKV_PAIRS_PER_STEP = 8
PAIR = 2 * HEAD_DIM
ATTN_BLOCK = 128
ATTN_BLOCK_BIG = 384
RMS_EPS = 1e-6
GN_EPS = 64e-5
L2_FLOOR = 1e-12
NEG_BIG = -1e30
ADAM_LR, ADAM_B1, ADAM_B2, ADAM_EPS, ADAM_WD, ADAM_STEP = 0.001, 0.9, 0.999, 1e-08, 0.01, 10
VMEM_BYTES_V7X = 64 * 1024 * 1024
VMEM_LIMIT_CAP = 56 * 1024 * 1024
VMEM_LIMIT_FLOOR = 32 * 1024 * 1024
MATMUL_VMEM_BUDGET = 36 * 1024 * 1024
GRID_STEP_BYTES = 1024 * 1024
ACC_BYTES_PER_HBM_BYTE = 6


def _vmem_limit(estimate_bytes):
    return int(min(max(estimate_bytes * 5 // 4, VMEM_LIMIT_FLOOR), VMEM_LIMIT_CAP))


def _pick(dim, cands):
    for c in cands:
        if dim % c == 0:
            return c
    return dim


def _row_tile(rows, width, itemsize=4, budget=2 * 1024 * 1024):
    for c in (1408, 1024, 704, 512, 384, 256, 128, 64, 32, 16, 8):
        if rows % c == 0 and c * width * itemsize <= budget:
            return c
    return rows


def _row_tile_ragged(rows, width, itemsize=4, budget=2 * 1024 * 1024):
    tile = _row_tile(rows, width, itemsize, budget)
    if tile * width * itemsize <= budget or rows < 16:
        return tile
    padded = -(-rows // 16) * 16
    for c in (1408, 1024, 704, 512, 384, 336, 256, 192, 128, 96, 64, 48, 32, 16):
        if padded % c == 0 and c * width * itemsize <= budget:
            return c
    return tile


def _dg(a, b, ta, tb):
    dims = (((0 if ta else 1,), (1 if tb else 0,)), ((), ()))
    return lax.dot_general(a, b, dims, preferred_element_type=F32)


def _split(x, n):
    parts = []
    for _ in range(n):
        h = x.astype(BF16)
        parts.append(h)
        x = x - h.astype(F32)
    return parts


def _mm(a, b, ta=False, tb=False):
    return _dg(a.astype(BF16), b.astype(BF16), ta, tb)


def _matmul(a, b, ta=False, tb=False, out_dtype=F32, name="matmul", after=None, b_slots=False, out_slots=0):
    if ta:
        kdim, m = a.shape
    else:
        m, kdim = a.shape
    if b_slots:
        n_slots, brows, bcols = b.shape
        n, k2 = (brows, n_slots * bcols) if tb else (n_slots * bcols, brows)
    elif tb:
        n, k2 = b.shape
    else:
        k2, n = b.shape
    assert kdim == k2, (a.shape, b.shape, ta, tb)
    sa, sb, so = a.dtype.itemsize, b.dtype.itemsize, jnp.dtype(out_dtype).itemsize
    n_unit = bcols if (b_slots and not tb) else (n // out_slots if out_slots else n)
    k_unit = bcols if (b_slots and tb) else kdim
    tm, tn, tk = _matmul_tiles(m, n, kdim, ta, sa, sb, so, n_unit, k_unit)
    nk = kdim // tk

    order = () if after is None else (after,)

    def body(a_ref, b_ref, *rest):
        o_ref, acc = rest[len(order)], rest[len(order) + 1:]
        part = _dg(a_ref[...].astype(BF16), b_ref[...].astype(BF16), ta, tb)
        if nk == 1:
            o_ref[...] = part.astype(o_ref.dtype)
            return
        kk = pl.program_id(2)

        @pl.when(kk == 0)
        def _():
            acc[0][...] = part

        @pl.when(kk > 0)
        def _():
            acc[0][...] += part

        @pl.when(kk == nk - 1)
        def _():
            o_ref[...] = acc[0][...].astype(o_ref.dtype)

    a_spec = pl.BlockSpec((tk, tm), lambda i, j, k: (k, i)) if ta else pl.BlockSpec((tm, tk), lambda i, j, k: (i, k))
    if b_slots and tb:
        per = bcols // tk
        b_spec = pl.BlockSpec((None, tn, tk), lambda i, j, k: (k // per, j, k % per))
    elif b_slots:
        per = bcols // tn
        b_spec = pl.BlockSpec((None, tk, tn), lambda i, j, k: (j // per, k, j % per))
    else:
        b_spec = pl.BlockSpec((tn, tk), lambda i, j, k: (j, k)) if tb else pl.BlockSpec((tk, tn), lambda i, j, k: (k, j))
    if out_slots:
        per_out = n // out_slots // tn
        out_spec = pl.BlockSpec((None, tm, tn), lambda i, j, k: (j // per_out, i, j % per_out))
        out_shape = jax.ShapeDtypeStruct((out_slots, m, n // out_slots), out_dtype)
    else:
        out_spec = pl.BlockSpec((tm, tn), lambda i, j, k: (i, j))
        out_shape = jax.ShapeDtypeStruct((m, n), out_dtype)
    return pl.pallas_call(
        body, name=name,
        grid=(m // tm, n // tn, nk),
        in_specs=[a_spec, b_spec] + [pl.BlockSpec(memory_space=pl.ANY)] * len(order),
        out_specs=out_spec,
        out_shape=out_shape,
        scratch_shapes=[pltpu.VMEM((tm, tn), F32)] if nk > 1 else [],
        compiler_params=pltpu.CompilerParams(dimension_semantics=("parallel", "parallel", "arbitrary"),
                                             vmem_limit_bytes=_vmem_limit(_matmul_vmem(tm, tn, tk, nk, sa, sb, so))),
    )(a, b, *order)


def _matmul_vmem(tm, tn, tk, nk, sa, sb, so):
    return 2 * (tm * tk * sa + tk * tn * sb + tm * tn * so) + tm * tn * 4 + (tm * tn * 4 if nk > 1 else 0)


def _matmul_tiles(m, n, kdim, ta, sa, sb, so, n_unit, k_unit):
    lane = (2816, 2176, 2048, 1408, 1024, 640, 512, 384, 256, 128)
    sublane = (2816, 2176, 2048, 1408, 1024, 704, 512, 384, 256, 128)
    divs = lambda dim, cands: [c for c in cands if dim % c == 0] or [dim]
    best = None
    for tm in divs(m, lane if ta else sublane):
        for tn in divs(n_unit, lane):
            for tk in divs(k_unit, sublane if ta else lane) + ([kdim] if (kdim <= 2048 and k_unit == kdim) else []):
                nk, nm, nn = kdim // tk, m // tm, n // tn
                if _matmul_vmem(tm, tn, tk, nk, sa, sb, so) > MATMUL_VMEM_BUDGET:
                    continue
                a_bytes = m * kdim * sa * (nn if nk > 1 else 1)
                b_bytes = kdim * n * sb * (1 if (nk == 1 and nn == 1) else nm)
                acc_bytes = m * n * 4 * 3 * nk // ACC_BYTES_PER_HBM_BYTE if nk > 1 else 0
                cost = a_bytes + b_bytes + m * n * so + acc_bytes + nm * nn * nk * GRID_STEP_BYTES
                if best is None or cost < best[0]:
                    best = (cost, tm, tn, tk)
    return best[1:]


@jax.custom_vjp
def dense(x, w):
    return _matmul(x.astype(BF16), w, name="dense_fwd")


def _dense_fwd(x, w):
    return _matmul(x.astype(BF16), w, name="dense_fwd"), (x.astype(BF16), w, jnp.zeros((), x.dtype))


def _dense_bwd(res, dy):
    xb, w, like = res
    dyb = dy.astype(BF16)
    dx = _matmul(dyb, w, tb=True, out_dtype=like.dtype, name="dense_dx")
    dw = _matmul(xb, dyb, ta=True, out_dtype=w.dtype, name="dense_dw")
    return dx, dw


dense.defvjp(_dense_fwd, _dense_bwd)


def _make_dense_cols(out_dtype):
    @jax.custom_vjp
    def op(x, w_slots):
        return _matmul(x.astype(BF16), w_slots, b_slots=True, out_dtype=out_dtype, name="dense_cols_fwd")

    def fwd(x, w_slots):
        assert x.dtype == F32
        xb = x.astype(BF16)
        return _matmul(xb, w_slots, b_slots=True, out_dtype=out_dtype, name="dense_cols_fwd"), (xb, w_slots)

    def bwd(res, dy):
        xb, w_slots = res
        dyb = dy.astype(BF16)
        dx = _matmul(dyb, w_slots, tb=True, b_slots=True, out_dtype=F32, name="dense_cols_dx")
        dw = _matmul(xb, dyb, ta=True, out_slots=w_slots.shape[0], out_dtype=w_slots.dtype, name="dense_cols_dw")
        return dx, dw

    op.defvjp(fwd, bwd)
    return op


dense_cols = _make_dense_cols(F32)
dense_cols_bf16 = _make_dense_cols(BF16)


def _swiglu_call(gu, d_act=None):
    rows, two_f = gu.shape
    f = two_f // 2
    tr = _row_tile(rows, two_f, itemsize=2, budget=3 * 1024 * 1024)
    half = lambda j: pl.BlockSpec((tr, f), lambda i, j=j: (i, j))
    ops = (gu, gu) if d_act is None else (gu, gu, d_act)

    def body(*refs):
        g, u = refs[0][...].astype(F32), refs[1][...].astype(F32)
        s = 1.0 / (1.0 + jnp.exp(-g))
        if d_act is None:
            refs[2][...] = (g * s * u).astype(BF16)
        else:
            d = refs[2][...].astype(F32)
            refs[3][:, :f] = (d * u * s * (1.0 + g * (1.0 - s))).astype(BF16)
            refs[3][:, f:] = (d * g * s).astype(BF16)

    width = f if d_act is None else two_f
    return pl.pallas_call(
        body, name="swiglu_fwd" if d_act is None else "swiglu_bwd", grid=(rows // tr,),
        in_specs=[half(0), half(1)] + ([half(0)] if d_act is not None else []),
        out_specs=pl.BlockSpec((tr, width), lambda i: (i, 0)),
        out_shape=jax.ShapeDtypeStruct((rows, width), BF16),
        compiler_params=pltpu.CompilerParams(dimension_semantics=("parallel",)),
    )(*ops)


@jax.custom_vjp
def swiglu(gu):
    return _swiglu_call(gu)


swiglu.defvjp(lambda gu: (_swiglu_call(gu), gu), lambda gu, d_act: (_swiglu_call(gu, d_act),))


def _merge_call(zg, a, b, dm=None):
    rows, d = a.shape
    tr = _row_tile(rows, d, budget=1024 * 1024)
    half = lambda j: pl.BlockSpec((tr, d), lambda i, j=j: (i, j))
    tile = half(0)

    def body(*refs):
        ga = 1.0 / (1.0 + jnp.exp(-refs[0][...]))
        gb = 1.0 / (1.0 + jnp.exp(-refs[1][...]))
        av, bv = refs[2][...].astype(F32), refs[3][...].astype(F32)
        if dm is None:
            refs[4][...] = (ga * av + gb * bv).astype(BF16)
        else:
            dv = refs[4][...].astype(F32)
            dzg_ref, da_ref, db_ref = refs[5:]
            dzg_ref[:, :d] = dv * av * ga * (1.0 - ga)
            dzg_ref[:, d:] = dv * bv * gb * (1.0 - gb)
            da_ref[...] = (dv * ga).astype(BF16)
            db_ref[...] = (dv * gb).astype(BF16)

    shape_b = jax.ShapeDtypeStruct((rows, d), BF16)
    if dm is None:
        out_specs, out_shape, ops = tile, shape_b, (zg, zg, a, b)
    else:
        out_specs = [pl.BlockSpec((tr, 2 * d), lambda i: (i, 0)), tile, tile]
        out_shape = [jax.ShapeDtypeStruct((rows, 2 * d), F32), shape_b, shape_b]
        ops = (zg, zg, a, b, dm)
    return pl.pallas_call(
        body, name="merge_fwd" if dm is None else "merge_bwd", grid=(rows // tr,),
        in_specs=[half(0), half(1)] + [tile] * (len(ops) - 2),
        out_specs=out_specs, out_shape=out_shape,
        compiler_params=pltpu.CompilerParams(dimension_semantics=("parallel",)),
    )(*ops)


@jax.custom_vjp
def gated_merge(zg, a, b):
    return _merge_call(zg, a, b)


gated_merge.defvjp(lambda zg, a, b: (_merge_call(zg, a, b), (zg, a, b)),
                   lambda res, dm: tuple(_merge_call(*res, dm)))


def _rms_fwd_call(x, g):
    rows, d = x.shape
    tr = _row_tile(rows, d)

    def body(x_ref, g_ref, y_ref):
        xv = x_ref[...]
        rstd = lax.rsqrt(jnp.mean(xv * xv, axis=1, keepdims=True) + RMS_EPS)
        y_ref[...] = (xv * rstd) * g_ref[...]

    return pl.pallas_call(
        body, name="rms_fwd", grid=(rows // tr,),
        in_specs=[pl.BlockSpec((tr, d), lambda i: (i, 0)), pl.BlockSpec((1, d), lambda i: (0, 0))],
        out_specs=pl.BlockSpec((tr, d), lambda i: (i, 0)),
        out_shape=jax.ShapeDtypeStruct((rows, d), F32),
        compiler_params=pltpu.CompilerParams(dimension_semantics=("parallel",)),
    )(x, g)


def _rms_bwd_call(x, g, dy):
    rows, d = x.shape
    tr = _row_tile(rows, d)

    def body(x_ref, g_ref, dy_ref, dx_ref, dg_ref):
        @pl.when(pl.program_id(0) == 0)
        def _():
            dg_ref[...] = jnp.zeros_like(dg_ref)

        xv = x_ref[...]
        dyv = dy_ref[...]
        rstd = lax.rsqrt(jnp.mean(xv * xv, axis=1, keepdims=True) + RMS_EPS)
        xhat = xv * rstd
        dxhat = dyv * g_ref[...]
        dx_ref[...] = rstd * (dxhat - xhat * jnp.mean(dxhat * xhat, axis=1, keepdims=True))
        dg_ref[...] += jnp.sum(dyv * xhat, axis=0, keepdims=True)

    return pl.pallas_call(
        body, name="rms_bwd", grid=(rows // tr,),
        in_specs=[pl.BlockSpec((tr, d), lambda i: (i, 0)), pl.BlockSpec((1, d), lambda i: (0, 0)),
                  pl.BlockSpec((tr, d), lambda i: (i, 0))],
        out_specs=[pl.BlockSpec((tr, d), lambda i: (i, 0)), pl.BlockSpec((1, d), lambda i: (0, 0))],
        out_shape=[jax.ShapeDtypeStruct((rows, d), F32), jax.ShapeDtypeStruct((1, d), F32)],
        compiler_params=pltpu.CompilerParams(dimension_semantics=("arbitrary",)),
    )(x, g, dy)


@jax.custom_vjp
def rmsnorm(x, g):
    return _rms_fwd_call(x, g)


rmsnorm.defvjp(lambda x, g: (_rms_fwd_call(x, g), (x, g)), lambda res, dy: tuple(_rms_bwd_call(res[0], res[1], dy)))


def _bcast_call(x, p, mul):
    rows, d = x.shape
    tr = _row_tile(rows, d)

    def body(x_ref, p_ref, y_ref):
        y_ref[...] = x_ref[...] * p_ref[...] if mul else x_ref[...] + p_ref[...]

    return pl.pallas_call(
        body, name="bcast_mul" if mul else "bcast_add", grid=(rows // tr,),
        in_specs=[pl.BlockSpec((tr, d), lambda i: (i, 0)), pl.BlockSpec((1, d), lambda i: (0, 0))],
        out_specs=pl.BlockSpec((tr, d), lambda i: (i, 0)),
        out_shape=jax.ShapeDtypeStruct((rows, d), F32),
        compiler_params=pltpu.CompilerParams(dimension_semantics=("parallel",)),
    )(x, p)


def _colsum_call(a, b=None):
    rows, d = a.shape
    tr = _row_tile(rows, d)
    ops = (a,) if b is None else (a, b)

    def body(*refs):
        o_ref = refs[-1]

        @pl.when(pl.program_id(0) == 0)
        def _():
            o_ref[...] = jnp.zeros_like(o_ref)

        v = refs[0][...] if b is None else refs[0][...] * refs[1][...]
        o_ref[...] += jnp.sum(v, axis=0, keepdims=True)

    return pl.pallas_call(
        body, name="colsum", grid=(rows // tr,),
        in_specs=[pl.BlockSpec((tr, d), lambda i: (i, 0))] * len(ops),
        out_specs=pl.BlockSpec((1, d), lambda i: (0, 0)),
        out_shape=jax.ShapeDtypeStruct((1, d), F32),
        compiler_params=pltpu.CompilerParams(dimension_semantics=("arbitrary",)),
    )(*ops)


@jax.custom_vjp
def badd(x, p):
    return _bcast_call(x, p, False)


badd.defvjp(lambda x, p: (_bcast_call(x, p, False), None), lambda res, dy: (dy, _colsum_call(dy)))


def _head_sums(x):
    i = lax.broadcasted_iota(jnp.int32, (PAIR, PAIR), 0) // HEAD_DIM
    j = lax.broadcasted_iota(jnp.int32, (PAIR, PAIR), 1) // HEAD_DIM
    ones = jnp.where(i == j, 1.0, 0.0).astype(BF16)
    hi, lo = _split(x, 2)
    cols = [slice(p * PAIR, (p + 1) * PAIR) for p in range(x.shape[1] // PAIR)]
    return jnp.concatenate([_dg(hi[:, c], ones, False, False) + _dg(lo[:, c], ones, False, False) for c in cols], axis=1)


def _head_rms_fwd_call(x, g):
    rows, w = x.shape
    tr = _row_tile(rows, w, budget=1024 * 1024)

    def body(x_ref, g_ref, y_ref):
        xv = x_ref[...]
        rstd = lax.rsqrt(_head_sums(xv * xv) * (1.0 / HEAD_DIM) + RMS_EPS)
        y_ref[...] = (xv * rstd) * g_ref[...]

    return pl.pallas_call(
        body, name="head_rms_fwd", grid=(rows // tr,),
        in_specs=[pl.BlockSpec((tr, w), lambda i: (i, 0)), pl.BlockSpec((1, w), lambda i: (0, 0))],
        out_specs=pl.BlockSpec((tr, w), lambda i: (i, 0)),
        out_shape=jax.ShapeDtypeStruct((rows, w), F32),
        compiler_params=pltpu.CompilerParams(dimension_semantics=("parallel",)),
    )(x, g)


def _head_rms_bwd_call(x, g, dy):
    rows, w = x.shape
    tr = _row_tile(rows, w, budget=1024 * 1024)

    def body(x_ref, g_ref, dy_ref, dx_ref, dg_ref):
        @pl.when(pl.program_id(0) == 0)
        def _():
            dg_ref[...] = jnp.zeros_like(dg_ref)

        xv, dyv = x_ref[...], dy_ref[...]
        rstd = lax.rsqrt(_head_sums(xv * xv) * (1.0 / HEAD_DIM) + RMS_EPS)
        xhat = xv * rstd
        dxhat = dyv * g_ref[...]
        dx_ref[...] = rstd * (dxhat - xhat * (_head_sums(dxhat * xhat) * (1.0 / HEAD_DIM)))
        dg_ref[...] += jnp.sum(dyv * xhat, axis=0, keepdims=True)

    return pl.pallas_call(
        body, name="head_rms_bwd", grid=(rows // tr,),
        in_specs=[pl.BlockSpec((tr, w), lambda i: (i, 0)), pl.BlockSpec((1, w), lambda i: (0, 0)),
                  pl.BlockSpec((tr, w), lambda i: (i, 0))],
        out_specs=[pl.BlockSpec((tr, w), lambda i: (i, 0)), pl.BlockSpec((1, w), lambda i: (0, 0))],
        out_shape=[jax.ShapeDtypeStruct((rows, w), F32), jax.ShapeDtypeStruct((1, w), F32)],
        compiler_params=pltpu.CompilerParams(dimension_semantics=("arbitrary",)),
    )(x, g, dy)


@jax.custom_vjp
def head_rms(x, g):
    return _head_rms_fwd_call(x, g)


head_rms.defvjp(lambda x, g: (_head_rms_fwd_call(x, g), (x, g)),
                lambda res, dy: tuple(_head_rms_bwd_call(res[0], res[1], dy)))


def _gn_fwd_call(y, r, kf, v, gw, gb, rk):
    rows, w = y.shape
    tr = _row_tile(rows, w, budget=512 * 1024)

    def body(y_ref, r_ref, kf_ref, v_ref, gw_ref, gb_ref, rk_ref, o_ref):
        yv = y_ref[...]
        yc = yv - _head_sums(yv) * (1.0 / HEAD_DIM)
        rstd = lax.rsqrt(_head_sums(yc * yc) * (1.0 / HEAD_DIM) + GN_EPS)
        s = _head_sums(r_ref[...] * kf_ref[...] * rk_ref[...])
        o_ref[...] = (yc * rstd) * gw_ref[...] + gb_ref[...] + s * v_ref[...]

    tok = pl.BlockSpec((tr, w), lambda i: (i, 0))
    par = pl.BlockSpec((1, w), lambda i: (0, 0))
    return pl.pallas_call(
        body, name="gn_bonus_fwd", grid=(rows // tr,),
        in_specs=[tok] * 4 + [par] * 3, out_specs=tok,
        out_shape=jax.ShapeDtypeStruct((rows, w), F32),
        compiler_params=pltpu.CompilerParams(dimension_semantics=("parallel",)),
    )(y, r, kf, v, gw, gb, rk)


def _gn_bwd_call(y, r, kf, v, gw, gb, rk, do):
    rows, w = y.shape
    tr = _row_tile(rows, w, budget=512 * 1024)

    def body(y_ref, r_ref, kf_ref, v_ref, gw_ref, rk_ref, do_ref,
             dy_ref, dr_ref, dkf_ref, dv_ref, dgw_ref, dgb_ref, drk_ref):
        @pl.when(pl.program_id(0) == 0)
        def _():
            dgw_ref[...] = jnp.zeros_like(dgw_ref)
            dgb_ref[...] = jnp.zeros_like(dgb_ref)
            drk_ref[...] = jnp.zeros_like(drk_ref)

        yv, rv, kv, vv, dov, rkv = y_ref[...], r_ref[...], kf_ref[...], v_ref[...], do_ref[...], rk_ref[...]
        mean = lambda t: _head_sums(t) * (1.0 / HEAD_DIM)
        yc = yv - mean(yv)
        rstd = lax.rsqrt(mean(yc * yc) + GN_EPS)
        yhat = yc * rstd
        dyhat = dov * gw_ref[...]
        dy_ref[...] = rstd * (dyhat - mean(dyhat) - yhat * mean(dyhat * yhat))
        s = _head_sums(rv * kv * rkv)
        ds = _head_sums(dov * vv)
        dv_ref[...] = s * dov
        dr_ref[...] = ds * kv * rkv
        dkf_ref[...] = ds * rv * rkv
        dgw_ref[...] += jnp.sum(dov * yhat, axis=0, keepdims=True)
        dgb_ref[...] += jnp.sum(dov, axis=0, keepdims=True)
        drk_ref[...] += jnp.sum(ds * rv * kv, axis=0, keepdims=True)

    tok = pl.BlockSpec((tr, w), lambda i: (i, 0))
    par = pl.BlockSpec((1, w), lambda i: (0, 0))
    tshape = jax.ShapeDtypeStruct((rows, w), F32)
    pshape = jax.ShapeDtypeStruct((1, w), F32)
    return pl.pallas_call(
        body, name="gn_bonus_bwd", grid=(rows // tr,),
        in_specs=[tok] * 4 + [par] * 2 + [tok], out_specs=[tok] * 4 + [par] * 3,
        out_shape=[tshape] * 4 + [pshape] * 3,
        compiler_params=pltpu.CompilerParams(dimension_semantics=("arbitrary",)),
    )(y, r, kf, v, gw, rk, do)


@jax.custom_vjp
def gn_bonus(y, r, kf, v, gw, gb, rk):
    return _gn_fwd_call(y, r, kf, v, gw, gb, rk)


def _gn_bwd(res, do):
    y, r, kf, v, gw, gb, rk = res
    dy, dr, dkf, dv, dgw, dgb, drk = _gn_bwd_call(y, r, kf, v, gw, gb, rk, do)
    return dy, dr, dkf, dv, dgw, dgb, drk


gn_bonus.defvjp(lambda *a: (_gn_fwd_call(*a), a), _gn_bwd)


PREP_ROWS = 128


def _prep_segments(rw, lora_w, lora_a, lora_g):
    at = 3 * rw
    seg = {"r": (0, rw), "k": (rw, 2 * rw), "v": (2 * rw, 3 * rw)}
    for name, n in (("wd", lora_w), ("ad", lora_a), ("gd", lora_g)):
        seg[name] = (at, at + _pad128(n))
        at += _pad128(n)
    return seg, at


def _prep_shifted(z_ref, zlast_ref, mu_ref, seg, first_tile):
    lo, hi = seg
    zr = z_ref[:, lo:hi]
    rows = zr.shape[0]
    before = jnp.where(first_tile, 0.0, zlast_ref[7:8, lo:hi])
    row0 = lax.broadcasted_iota(jnp.int32, zr.shape, 0) == 0
    diff = jnp.where(row0, before, pltpu.roll(zr, 1, axis=0)) - zr
    return zr + diff * mu_ref[:, lo:hi], diff


def _prep_forward_values(z_ref, zlast_ref, mu_ref, w0_ref, a0_ref, kk_ref, ka_ref, w2_ref, a2_ref, g2_ref, segs, first_tile):
    z = {n: _prep_shifted(z_ref, zlast_ref, mu_ref, segs[n], first_tile) for n in segs}
    r, k, v, wd, ad, gd = (z[n][0] for n in ("r", "k", "v", "wd", "ad", "gd"))
    twd = jnp.tanh(wd)
    pw = _mm(twd, w2_ref[...]) + w0_ref[...]
    lw = -jnp.exp(-(jnp.maximum(-pw, 0.0) + jnp.log(1.0 + jnp.exp(-jnp.abs(pw)))) - 0.5)
    a_sig = 1.0 / (1.0 + jnp.exp(-(_mm(ad, a2_ref[...]) + a0_ref[...])))
    sg = 1.0 / (1.0 + jnp.exp(-gd))
    kx = k * kk_ref[...]
    nrm = jnp.sqrt(_head_sums(kx * kx))
    inv = 1.0 / jnp.maximum(nrm, L2_FLOOR)
    return dict(z=z, r=r, k=k, v=v, twd=twd, pw=pw, lw=lw, a_sig=a_sig, sg=sg, ad=ad, kk=kx * inv, inv=inv, live=nrm > L2_FLOOR)


def _prep_specs(tokens, rpad, rw, w2, a2, g2):
    tr = PREP_ROWS
    tile = lambda w: pl.BlockSpec((tr, w), lambda i: (i, 0))
    before = pl.BlockSpec((8, rpad), lambda i: (jnp.maximum(i * (tr // 8) - 1, 0), 0))
    whole = lambda a: pl.BlockSpec(a.shape, lambda i: (0, 0))
    par = pl.BlockSpec((1, rw), lambda i: (0, 0))
    return tile, before, whole, par, pl.BlockSpec((1, rpad), lambda i: (0, 0))


def _prep_fwd_call(zr, mu, w0, a0, k_k, k_a, w2, a2, g2):
    tokens, rpad = zr.shape
    rw = w0.shape[1]
    segs, _ = _prep_segments(rw, w2.shape[0], a2.shape[0], g2.shape[0])
    tile, before, whole, par, mu_spec = _prep_specs(tokens, rpad, rw, w2, a2, g2)

    def body(z_ref, zlast_ref, mu_ref, w0_ref, a0_ref, kk_ref, ka_ref, w2_ref, a2_ref, g2_ref,
             r_ref, lw_ref, kf_ref, v_ref, na_ref, b_ref, g_ref):
        f = _prep_forward_values(z_ref, zlast_ref, mu_ref, w0_ref, a0_ref, kk_ref, ka_ref, w2_ref, a2_ref, g2_ref,
                                 segs, pl.program_id(0) == 0)
        r_ref[...] = f["r"]
        v_ref[...] = f["v"]
        lw_ref[...] = f["lw"]
        kf_ref[...] = f["k"] * (1.0 + (f["a_sig"] - 1.0) * ka_ref[...])
        na_ref[...] = -f["kk"]
        b_ref[...] = f["kk"] * f["a_sig"]
        g_ref[...] = _mm(f["sg"], g2_ref[...])

    shape = jax.ShapeDtypeStruct((tokens, rw), F32)
    return pl.pallas_call(
        body, name="rwkv_prep_fwd", grid=(tokens // PREP_ROWS,),
        in_specs=[tile(rpad), before, mu_spec, par, par, par, par, whole(w2), whole(a2), whole(g2)],
        out_specs=[tile(rw)] * 7, out_shape=[shape] * 7,
        compiler_params=pltpu.CompilerParams(dimension_semantics=("parallel",), vmem_limit_bytes=VMEM_LIMIT_CAP),
    )(zr, zr, mu, w0, a0, k_k, k_a, w2, a2, g2)


def _prep_bwd_call(zr, mu, w0, a0, k_k, k_a, w2, a2, g2, cts):
    tokens, rpad = zr.shape
    rw = w0.shape[1]
    segs, _ = _prep_segments(rw, w2.shape[0], a2.shape[0], g2.shape[0])
    tile, before, whole, par, mu_spec = _prep_specs(tokens, rpad, rw, w2, a2, g2)
    nt = tokens // PREP_ROWS
    rev = lambda spec: pl.BlockSpec(spec.block_shape, lambda i, f=spec.index_map: f(nt - 1 - i))

    def body(z_ref, zlast_ref, mu_ref, w0_ref, a0_ref, kk_ref, ka_ref, w2_ref, a2_ref, g2_ref,
             dr_ref, dlw_ref, dkf_ref, dv_ref, dna_ref, db_ref, dg_ref,
             dz_ref, dmu_ref, dw0_ref, da0_ref, dkk_ref, dka_ref, dw2_ref, da2_ref, dg2_ref, carry):
        step = pl.program_id(0)

        @pl.when(step == 0)
        def _():
            for ref in (dmu_ref, dw0_ref, da0_ref, dkk_ref, dka_ref, dw2_ref, da2_ref, dg2_ref, carry):
                ref[...] = jnp.zeros_like(ref)

        f = _prep_forward_values(z_ref, zlast_ref, mu_ref, w0_ref, a0_ref, kk_ref, ka_ref, w2_ref, a2_ref, g2_ref,
                                 segs, step == nt - 1)
        k, kk, a_sig, sg, twd = f["k"], f["kk"], f["a_sig"], f["sg"], f["twd"]
        colsum = lambda t: jnp.sum(t, axis=0, keepdims=True)
        dkf, db, dg = dkf_ref[...], db_ref[...], dg_ref[...]
        ka = ka_ref[...]
        dgd = _mm(dg, g2_ref[...], tb=True) * sg * (1.0 - sg)
        dg2_ref[...] += _mm(sg, dg, ta=True)
        dkk = db * a_sig - dna_ref[...]
        da_sig = db * kk + dkf * k * ka
        dk = dkf * (1.0 + (a_sig - 1.0) * ka)
        dka_ref[...] += colsum(dkf * k * (a_sig - 1.0))
        along = jnp.where(f["live"], _head_sums(dkk * kk), 0.0)
        dkx = (dkk - kk * along) * f["inv"]
        dk = dk + dkx * kk_ref[...]
        dkk_ref[...] += colsum(dkx * k)
        dpa = da_sig * a_sig * (1.0 - a_sig)
        da0_ref[...] += colsum(dpa)
        dad = _mm(dpa, a2_ref[...], tb=True)
        da2_ref[...] += _mm(f["ad"], dpa, ta=True)
        dpw = dlw_ref[...] * f["lw"] / (1.0 + jnp.exp(f["pw"]))
        dw0_ref[...] += colsum(dpw)
        dwd = _mm(dpw, w2_ref[...], tb=True) * (1.0 - twd * twd)
        dw2_ref[...] += _mm(twd, dpw, ta=True)
        rows = PREP_ROWS
        last = lax.broadcasted_iota(jnp.int32, (rows, 1), 0) == rows - 1
        for name, dz in (("r", dr_ref[...]), ("k", dk), ("v", dv_ref[...]), ("wd", dwd), ("ad", dad), ("gd", dgd)):
            lo, hi = segs[name]
            mu_s = mu_ref[:, lo:hi]
            dmu_ref[:, lo:hi] += colsum(dz * f["z"][name][1])
            later = dz * mu_s
            dz_ref[:, lo:hi] = dz * (1.0 - mu_s) + jnp.where(last, carry[:, lo:hi], pltpu.roll(later, rows - 1, axis=0))
            carry[:, lo:hi] = later[0:1, :]

    tok = jax.ShapeDtypeStruct((tokens, rw), F32)
    acc = lambda a: jax.ShapeDtypeStruct(a.shape, F32)
    return pl.pallas_call(
        body, name="rwkv_prep_bwd", grid=(nt,),
        in_specs=[rev(tile(rpad)), rev(before), mu_spec, par, par, par, par, whole(w2), whole(a2), whole(g2)]
                 + [rev(tile(rw))] * 7,
        out_specs=[rev(tile(rpad)), mu_spec, par, par, par, par, whole(w2), whole(a2), whole(g2)],
        out_shape=[jax.ShapeDtypeStruct((tokens, rpad), F32), acc(mu), acc(w0), acc(a0), acc(k_k), acc(k_a), acc(w2), acc(a2), acc(g2)],
        scratch_shapes=[pltpu.VMEM((1, rpad), F32)],
        compiler_params=pltpu.CompilerParams(dimension_semantics=("arbitrary",), vmem_limit_bytes=VMEM_LIMIT_CAP),
    )(zr, zr, mu, w0, a0, k_k, k_a, w2, a2, g2, *cts)


@jax.custom_vjp
def rwkv_prep(zr, mu, w0, a0, k_k, k_a, w2, a2, g2):
    return tuple(_prep_fwd_call(zr, mu, w0, a0, k_k, k_a, w2, a2, g2))


def _rwkv_prep_bwd(res, cts):
    zr, mu, w0, a0, k_k, k_a, w2, a2, g2 = res
    dz, dmu, dw0, da0, dkk, dka, dw2, da2, dg2 = _prep_bwd_call(*res, cts)
    return dz, dmu, dw0, da0, dkk, dka, dw2.astype(w2.dtype), da2.astype(a2.dtype), dg2.astype(g2.dtype)


rwkv_prep.defvjp(lambda *a: (tuple(_prep_fwd_call(*a)), a), _rwkv_prep_bwd)


def _pair_masks(rows):
    lane = lax.broadcasted_iota(jnp.int32, (rows, PAIR), 1)
    return lane < HEAD_DIM, lane >= HEAD_DIM


def _bd(x):
    m0, m1 = _pair_masks(x.shape[0])
    return jnp.concatenate([jnp.where(m0, x, 0.0), jnp.where(m1, x, 0.0)], axis=0)


def _unbd(m, c):
    return jnp.where(_pair_masks(c)[0], m[:c], m[c:])


def _pair_a(l2, r2):
    return _mm(l2, _bd(r2), tb=True)


def _pair_mul(p2, x2):
    return _mm(p2, _bd(x2))


def _pair_mul_t(p2, x2):
    return _unbd(_mm(p2, x2, ta=True), p2.shape[0])


def _block_diag_mask():
    row = lax.broadcasted_iota(jnp.int32, (PAIR, PAIR), 0)
    lane = lax.broadcasted_iota(jnp.int32, (PAIR, PAIR), 1)
    return (row < HEAD_DIM) == (lane < HEAD_DIM), row == lane


def _wkv_pair_common(r, lw, k, a, b):
    c = r[0].shape[0]
    pairs = range(len(r))
    i = lax.broadcasted_iota(jnp.int32, (c, PAIR), 0)
    j = lax.broadcasted_iota(jnp.int32, (c, PAIR), 1) % c
    strict, incl = i > j, i >= j
    ti = lax.broadcasted_iota(jnp.int32, (c, c), 0)
    tj = lax.broadcasted_iota(jnp.int32, (c, c), 1)
    tri = jnp.where(ti >= tj, 1.0, 0.0).astype(BF16)
    lc = [sum(_dg(tri, part, False, False) for part in _split(lw[p], 3)) for p in pairs]
    lend = [lc[p][c - 1:c, :] for p in pairs]
    rt = [r[p] * jnp.exp(lc[p]) for p in pairs]
    at = [a[p] * jnp.exp(lc[p] - lw[p]) for p in pairs]
    pinv = [jnp.exp(-lc[p]) for p in pairs]
    kt = [k[p] * pinv[p] for p in pairs]
    bt = [b[p] * pinv[p] for p in pairs]
    e = [jnp.exp(lend[p] - lc[p]) for p in pairs]
    ktp = [k[p] * e[p] for p in pairs]
    btp = [b[p] * e[p] for p in pairs]
    a_ab = [jnp.where(strict, _pair_a(at[p], bt[p]), 0.0) for p in pairs]
    a_ak = [jnp.where(strict, _pair_a(at[p], kt[p]), 0.0) for p in pairs]
    a_rb = [jnp.where(incl, _pair_a(rt[p], bt[p]), 0.0) for p in pairs]
    a_rk = [jnp.where(incl, _pair_a(rt[p], kt[p]), 0.0) for p in pairs]
    t = [jnp.where(i == j, 1.0, 0.0) + a_ab[p] for p in pairs]
    xp = a_ab
    n = 2
    while n < c:
        xp = [_pair_mul(xp[p], xp[p]) for p in pairs]
        t = [t[p] + _pair_mul(t[p], xp[p]) for p in pairs]
        n *= 2
    bdm, eye = _block_diag_mask()
    pend_col = [jnp.sum(jnp.where(eye, jnp.exp(lend[p]), 0.0), axis=1, keepdims=True) for p in pairs]
    return dict(rt=rt, at=at, kt=kt, bt=bt, ktp=ktp, btp=btp, a_ak=a_ak, a_rb=a_rb, a_rk=a_rk, t=t,
                pend_col=pend_col, lend=lend, lc=lc, strict=strict, incl=incl, tri=tri, bdm=bdm)


def _wkv_group(width):
    npair = width // PAIR
    g = min(WKV_PAIRS_PER_STEP, npair)
    assert npair % g == 0
    return npair, g


def _wkv_fwd_call(r, lw, k, v, a, b):
    tokens, width = r.shape
    c = WKV_CHUNK
    nc = tokens // c
    npair, g = _wkv_group(width)

    def body(r_ref, lw_ref, k_ref, v_ref, a_ref, b_ref, y_ref, s_ref, st):
        @pl.when(pl.program_id(1) == 0)
        def _():
            st[...] = jnp.zeros_like(st)

        pairs = range(g)
        rv, lwv, kv, vv, av, bv = ([ref[:, p * PAIR:(p + 1) * PAIR] for p in pairs]
                                   for ref in (r_ref, lw_ref, k_ref, v_ref, a_ref, b_ref))
        s0 = [st[p] for p in pairs]
        q = _wkv_pair_common(rv, lwv, kv, av, bv)
        w1 = [_mm(q["at"][p], s0[p]) + _pair_mul(q["a_ak"][p], vv[p]) for p in pairs]
        u = [_pair_mul(q["t"][p], w1[p]) for p in pairs]
        y = [_mm(q["rt"][p], s0[p]) + _pair_mul(q["a_rb"][p], u[p]) + _pair_mul(q["a_rk"][p], vv[p]) for p in pairs]
        grow = [_mm(jnp.concatenate([q["btp"][p], q["ktp"][p]], axis=0), jnp.concatenate([u[p], vv[p]], axis=0), ta=True)
                for p in pairs]
        for p in pairs:
            y_ref[:, p * PAIR:(p + 1) * PAIR] = y[p]
            s_ref[0, p] = s0[p]
            st[p] = q["pend_col"][p] * s0[p] + jnp.where(q["bdm"], grow[p], 0.0)

    tok = pl.BlockSpec((c, g * PAIR), lambda gi, ci: (ci, gi))
    return pl.pallas_call(
        body, name="wkv_fwd", grid=(npair // g, nc),
        in_specs=[tok] * 6,
        out_specs=[tok, pl.BlockSpec((1, g, PAIR, PAIR), lambda gi, ci: (ci, gi, 0, 0))],
        out_shape=[jax.ShapeDtypeStruct((tokens, width), F32), jax.ShapeDtypeStruct((nc, npair, PAIR, PAIR), F32)],
        scratch_shapes=[pltpu.VMEM((g, PAIR, PAIR), F32)],
        compiler_params=pltpu.CompilerParams(dimension_semantics=("parallel", "arbitrary")),
    )(r, lw, k, v, a, b)


def _wkv_bwd_call(r, lw, k, v, a, b, s, dy):
    tokens, width = r.shape
    c = WKV_CHUNK
    nc = tokens // c
    npair, g = _wkv_group(width)

    def body(r_ref, lw_ref, k_ref, v_ref, a_ref, b_ref, s_ref, dy_ref,
             dr_ref, dlw_ref, dk_ref, dv_ref, da_ref, db_ref, dst):
        @pl.when(pl.program_id(1) == 0)
        def _():
            dst[...] = jnp.zeros_like(dst)

        pairs = range(g)
        rv, lwv, kv, vv, av, bv, dyv = ([ref[:, p * PAIR:(p + 1) * PAIR] for p in pairs]
                                        for ref in (r_ref, lw_ref, k_ref, v_ref, a_ref, b_ref, dy_ref))
        s0 = [s_ref[0, p] for p in pairs]
        dsc = [dst[p] for p in pairs]
        q = _wkv_pair_common(rv, lwv, kv, av, bv)
        rt, at, kt, bt, ktp, btp, t = (q[n] for n in ("rt", "at", "kt", "bt", "ktp", "btp", "t"))
        a_ak, a_rb, a_rk, strict, incl = (q[n] for n in ("a_ak", "a_rb", "a_rk", "strict", "incl"))
        w1 = [_mm(at[p], s0[p]) + _pair_mul(a_ak[p], vv[p]) for p in pairs]
        u = [_pair_mul(t[p], w1[p]) for p in pairs]
        du = [_pair_mul_t(a_rb[p], dyv[p]) + _mm(btp[p], dsc[p]) for p in pairs]
        dw1 = [_pair_mul_t(t[p], du[p]) for p in pairs]
        dv = [_pair_mul_t(a_rk[p], dyv[p]) + _mm(ktp[p], dsc[p]) + _pair_mul_t(a_ak[p], dw1[p]) for p in pairs]
        da_ab = [jnp.where(strict, _pair_a(dw1[p], u[p]), 0.0) for p in pairs]
        da_ak = [jnp.where(strict, _pair_a(dw1[p], vv[p]), 0.0) for p in pairs]
        da_rb = [jnp.where(incl, _pair_a(dyv[p], u[p]), 0.0) for p in pairs]
        da_rk = [jnp.where(incl, _pair_a(dyv[p], vv[p]), 0.0) for p in pairs]
        d_rt = [_mm(dyv[p], s0[p], tb=True) + _pair_mul(da_rb[p], bt[p]) + _pair_mul(da_rk[p], kt[p]) for p in pairs]
        d_at = [_mm(dw1[p], s0[p], tb=True) + _pair_mul(da_ab[p], bt[p]) + _pair_mul(da_ak[p], kt[p]) for p in pairs]
        d_bt = [_pair_mul_t(da_ab[p], at[p]) + _pair_mul_t(da_rb[p], rt[p]) for p in pairs]
        d_kt = [_pair_mul_t(da_ak[p], at[p]) + _pair_mul_t(da_rk[p], rt[p]) for p in pairs]
        d_btp = [_mm(u[p], dsc[p], tb=True) for p in pairs]
        d_ktp = [_mm(vv[p], dsc[p], tb=True) for p in pairs]
        ones = jnp.ones((8, PAIR), BF16)
        dpend = [sum(_dg(ones, part, False, True) for part in _split(dsc[p] * s0[p], 3))[0:1, :] * jnp.exp(q["lend"][p])
                 for p in pairs]
        grow = [_mm(jnp.concatenate([rt[p], at[p]], axis=0), jnp.concatenate([dyv[p], dw1[p]], axis=0), ta=True)
                for p in pairs]
        last = lax.broadcasted_iota(jnp.int32, (c, PAIR), 0) == c - 1
        for p in pairs:
            sl = slice(p * PAIR, (p + 1) * PAIR)
            dst[p] = q["pend_col"][p] * dsc[p] + jnp.where(q["bdm"], grow[p], 0.0)
            lc_e = d_ktp[p] * ktp[p] + d_btp[p] * btp[p]
            dlend = jnp.sum(lc_e, axis=0, keepdims=True) + dpend[p]
            dlc = d_rt[p] * rt[p] - d_kt[p] * kt[p] - d_bt[p] * bt[p] - lc_e + jnp.where(last, dlend, 0.0)
            dlp = d_at[p] * at[p]
            dlw_ref[:, sl] = sum(_dg(q["tri"], part, True, False) for part in _split(dlc + dlp, 3)) - dlp
            lc = q["lc"][p]
            pinv = jnp.exp(-lc)
            e = jnp.exp(q["lend"][p] - lc)
            dr_ref[:, sl] = d_rt[p] * jnp.exp(lc)
            da_ref[:, sl] = d_at[p] * jnp.exp(lc - lwv[p])
            dk_ref[:, sl] = d_kt[p] * pinv + d_ktp[p] * e
            db_ref[:, sl] = d_bt[p] * pinv + d_btp[p] * e
            dv_ref[:, sl] = dv[p]

    tok = pl.BlockSpec((c, g * PAIR), lambda gi, ci: (nc - 1 - ci, gi))
    tshape = jax.ShapeDtypeStruct((tokens, width), F32)
    return pl.pallas_call(
        body, name="wkv_bwd", grid=(npair // g, nc),
        in_specs=[tok] * 6 + [pl.BlockSpec((1, g, PAIR, PAIR), lambda gi, ci: (nc - 1 - ci, gi, 0, 0)), tok],
        out_specs=[tok] * 6, out_shape=[tshape] * 6,
        scratch_shapes=[pltpu.VMEM((g, PAIR, PAIR), F32)],
        compiler_params=pltpu.CompilerParams(dimension_semantics=("parallel", "arbitrary")),
    )(r, lw, k, v, a, b, s, dy)


@jax.custom_vjp
def wkv7(r, lw, k, v, a, b):
    return _wkv_fwd_call(r, lw, k, v, a, b)[0]


def _wkv7_fwd(r, lw, k, v, a, b):
    y, s = _wkv_fwd_call(r, lw, k, v, a, b)
    return y, (r, lw, k, v, a, b, s)


wkv7.defvjp(_wkv7_fwd, lambda res, dy: tuple(_wkv_bwd_call(*res, dy)))


def _attn_block(tokens):
    return ATTN_BLOCK_BIG if tokens % ATTN_BLOCK_BIG == 0 else ATTN_BLOCK


def _fox_layouts(cum):
    tokens, heads = cum.shape
    t = _attn_block(tokens)
    cq = cum.reshape(tokens, heads // 2, 2).transpose(1, 0, 2)
    ck = cum.T.reshape(heads // 2, 2, tokens // t, t).transpose(0, 2, 1, 3)
    return cq, ck


def _head_lane_masks(rows):
    lane = lax.broadcasted_iota(jnp.int32, (rows, 2 * HEAD_DIM), 1)
    return [lane < HEAD_DIM, lane >= HEAD_DIM]


def _fox_fwd_call(q, k, v, cq, ck):
    tokens, width = q.shape
    t = _attn_block(tokens)
    nb = tokens // t
    hd = HEAD_DIM
    npair = width // (2 * hd)

    def body(q_ref, k_ref, v_ref, cq_ref, ck_ref, o_ref, lse_ref):
        i = pl.program_id(1)
        masks = _head_lane_masks(t)
        q2 = q_ref[...]
        qs = [jnp.where(mk, q2, 0.0).astype(BF16) for mk in masks]
        cqs = [cq_ref[0, :, hh:hh + 1] for hh in range(2)]

        def block(j, carry, diagonal):
            off = pl.multiple_of(j * t, t)
            ckj = ck_ref[0, j]
            k2 = k_ref[pl.ds(off, t), :].astype(BF16)
            v2 = v_ref[pl.ds(off, t), :].astype(BF16)
            out = []
            for hh in range(2):
                m, l, acc = carry[hh]
                s = _dg(qs[hh], k2, False, True) + (cqs[hh] - ckj[hh:hh + 1, :])
                if diagonal:
                    keep = lax.broadcasted_iota(jnp.int32, (t, t), 0) >= lax.broadcasted_iota(jnp.int32, (t, t), 1)
                    s = jnp.where(keep, s, NEG_BIG)
                m_new = jnp.maximum(m, jnp.max(s, axis=1, keepdims=True))
                alpha = jnp.exp(m - m_new)
                p = jnp.exp(s - m_new)
                l = alpha * l + jnp.sum(p, axis=1, keepdims=True)
                acc = alpha * acc + _dg(p.astype(BF16), v2, False, False)
                out.append((m_new, l, acc))
            return tuple(out)

        init = tuple((jnp.full((t, 1), NEG_BIG, F32), jnp.zeros((t, 1), F32), jnp.zeros((t, 2 * hd), F32)) for _ in range(2))
        res = lax.fori_loop(0, i, lambda j, c: block(j, c, False), init)
        res = block(i, res, True)
        o_ref[...] = jnp.where(masks[0], res[0][2] / res[0][1], res[1][2] / res[1][1])
        for hh in range(2):
            lse_ref[0, :, hh:hh + 1] = res[hh][0] + jnp.log(res[hh][1])

    blk = pl.BlockSpec((t, 2 * hd), lambda hp, i: (i, hp))
    full = pl.BlockSpec((tokens, 2 * hd), lambda hp, i: (0, hp))
    cq_spec = pl.BlockSpec((1, t, 2), lambda hp, i: (hp, i, 0))
    ck_spec = pl.BlockSpec((1, nb, 2, t), lambda hp, i: (hp, 0, 0, 0))
    return pl.pallas_call(
        body, name="fox_fwd", grid=(npair, nb),
        in_specs=[blk, full, full, cq_spec, ck_spec],
        out_specs=[blk, cq_spec],
        out_shape=[jax.ShapeDtypeStruct((tokens, width), F32), jax.ShapeDtypeStruct((npair, tokens, 2), F32)],
        compiler_params=pltpu.CompilerParams(dimension_semantics=("parallel", "arbitrary")),
    )(q, k, v, cq, ck)


def _fox_bwd_call(q, k, v, cq, ck, o, lse, do):
    tokens, width = q.shape
    t = _attn_block(tokens)
    nb = tokens // t
    hd = HEAD_DIM
    npair = width // (2 * hd)

    def body(q_ref, k_ref, v_ref, cq_ref, ck_ref, o_ref, lse_ref, do_ref, dq_ref, dk_ref, dv_ref, dck_ref, dcq_ref):
        i = pl.program_id(1)

        @pl.when(i == 0)
        def _():
            dk_ref[...] = jnp.zeros_like(dk_ref)
            dv_ref[...] = jnp.zeros_like(dv_ref)
            dck_ref[...] = jnp.zeros_like(dck_ref)

        masks = _head_lane_masks(t)
        q2, do2, o2 = q_ref[...], do_ref[...], o_ref[...]
        qs = [jnp.where(mk, q2, 0.0).astype(BF16) for mk in masks]
        dos = [jnp.where(mk, do2, 0.0).astype(BF16) for mk in masks]
        deltas = [jnp.sum(dos[hh].astype(F32) * o2, axis=1, keepdims=True) for hh in range(2)]
        bias = [cq_ref[0, :, hh:hh + 1] - lse_ref[0, :, hh:hh + 1] for hh in range(2)]

        def block(j, carry, diagonal):
            off = pl.multiple_of(j * t, t)
            ckj = ck_ref[0, j]
            k2 = k_ref[pl.ds(off, t), :].astype(BF16)
            v2 = v_ref[pl.ds(off, t), :].astype(BF16)
            out = []
            dk2 = jnp.zeros((t, 2 * hd), F32)
            dv2 = jnp.zeros((t, 2 * hd), F32)
            for hh in range(2):
                s = _dg(qs[hh], k2, False, True) + (bias[hh] - ckj[hh:hh + 1, :])
                if diagonal:
                    keep = lax.broadcasted_iota(jnp.int32, (t, t), 0) >= lax.broadcasted_iota(jnp.int32, (t, t), 1)
                    s = jnp.where(keep, s, NEG_BIG)
                p = jnp.exp(s)
                dp = _dg(dos[hh], v2, False, True)
                ds = p * (dp - deltas[hh])
                dsb = ds.astype(BF16)
                dq, rowsum = carry[hh]
                out.append((dq + _dg(dsb, k2, False, False), rowsum + jnp.sum(ds, axis=1, keepdims=True)))
                dk2 = dk2 + _dg(dsb, qs[hh], True, False)
                dv2 = dv2 + _dg(p.astype(BF16), dos[hh], True, False)
                dck_ref[0, j, hh:hh + 1, :] -= jnp.sum(ds, axis=0, keepdims=True)
            dk_ref[pl.ds(off, t), :] += dk2
            dv_ref[pl.ds(off, t), :] += dv2
            return tuple(out)

        init = tuple((jnp.zeros((t, 2 * hd), F32), jnp.zeros((t, 1), F32)) for _ in range(2))
        res = lax.fori_loop(0, i, lambda j, c: block(j, c, False), init)
        res = block(i, res, True)
        dq_ref[...] = jnp.where(masks[0], res[0][0], res[1][0])
        for hh in range(2):
            dcq_ref[0, :, hh:hh + 1] = res[hh][1]

    blk = pl.BlockSpec((t, 2 * hd), lambda hp, i: (i, hp))
    full = pl.BlockSpec((tokens, 2 * hd), lambda hp, i: (0, hp))
    cq_spec = pl.BlockSpec((1, t, 2), lambda hp, i: (hp, i, 0))
    ck_spec = pl.BlockSpec((1, nb, 2, t), lambda hp, i: (hp, 0, 0, 0))
    tshape = jax.ShapeDtypeStruct((tokens, width), F32)
    return pl.pallas_call(
        body, name="fox_bwd", grid=(npair, nb),
        in_specs=[blk, full, full, cq_spec, ck_spec, blk, cq_spec, blk],
        out_specs=[blk, full, full, ck_spec, cq_spec],
        out_shape=[tshape, tshape, tshape, jax.ShapeDtypeStruct((npair, nb, 2, t), F32),
                   jax.ShapeDtypeStruct((npair, tokens, 2), F32)],
        compiler_params=pltpu.CompilerParams(dimension_semantics=("parallel", "arbitrary")),
    )(q, k, v, cq, ck, o, lse, do)


@jax.custom_vjp
def fox_attention(q, k, v, cum):
    return _fox_fwd_call(q, k, v, *_fox_layouts(cum))[0]


def _fox_fwd(q, k, v, cum):
    cq, ck = _fox_layouts(cum)
    o, lse = _fox_fwd_call(q, k, v, cq, ck)
    return o, (q, k, v, cq, ck, o, lse)


def _fox_bwd(res, do):
    q, k, v, cq, ck, o, lse = res
    dq, dk, dv, dck, dcq = _fox_bwd_call(q, k, v, cq, ck, o, lse, do)
    npair, nb, _, t = dck.shape
    dcum = dck.transpose(0, 2, 1, 3).reshape(2 * npair, nb * t).T + dcq.transpose(1, 0, 2).reshape(nb * t, 2 * npair)
    return dq, dk, dv, dcum


fox_attention.defvjp(_fox_fwd, _fox_bwd)


def _loss_call(y, target):
    rows, d = y.shape
    tr = _row_tile(rows, d)

    def body(y_ref, t_ref, loss_ref, dy_ref):
        @pl.when(pl.program_id(0) == 0)
        def _():
            loss_ref[...] = jnp.zeros_like(loss_ref)

        diff = y_ref[...] - t_ref[...]
        dy_ref[...] = diff * (1.0 / d)
        loss_ref[...] += (0.5 / d) * jnp.sum(jnp.sum(diff * diff, axis=1, keepdims=True), axis=0, keepdims=True)

    return pl.pallas_call(
        body, name="loss", grid=(rows // tr,),
        in_specs=[pl.BlockSpec((tr, d), lambda i: (i, 0))] * 2,
        out_specs=[pl.BlockSpec((1, 1), lambda i: (0, 0)), pl.BlockSpec((tr, d), lambda i: (i, 0))],
        out_shape=[jax.ShapeDtypeStruct((1, 1), F32), jax.ShapeDtypeStruct((rows, d), F32)],
        compiler_params=pltpu.CompilerParams(dimension_semantics=("arbitrary",)),
    )(y, target)


def _adamw_call(w, g, m, v):
    rows, cols = w.shape
    tr = _row_tile_ragged(rows, cols, budget=1024 * 1024)
    c1 = 1.0 / (1.0 - ADAM_B1 ** ADAM_STEP)
    c2 = 1.0 / (1.0 - ADAM_B2 ** ADAM_STEP)

    def body(w_ref, g_ref, m_ref, v_ref, d_ref, nm_ref, nv_ref):
        gv = g_ref[...]
        nm = ADAM_B1 * m_ref[...] + (1.0 - ADAM_B1) * gv
        nv = ADAM_B2 * v_ref[...] + (1.0 - ADAM_B2) * (gv * gv)
        nm_ref[...] = nm
        nv_ref[...] = nv
        d_ref[...] = -ADAM_LR * ((nm * c1) / (jnp.sqrt(nv * c2) + ADAM_EPS) + ADAM_WD * w_ref[...])

    spec = pl.BlockSpec((tr, cols), lambda i: (i, 0))
    shape = jax.ShapeDtypeStruct((rows, cols), F32)
    return pl.pallas_call(
        body, name="adamw", grid=(pl.cdiv(rows, tr),),
        in_specs=[spec] * 4, out_specs=[spec] * 3, out_shape=[shape] * 3,
        compiler_params=pltpu.CompilerParams(dimension_semantics=("parallel",)),
    )(w, g, m, v)


def _my_place():
    return lax.axis_index("x"), lax.axis_index("y"), lax.axis_index("c")


def _place_index(px, py, pc):
    return 4 * px + 2 * py + pc


HBM_SPEC = pl.BlockSpec(memory_space=pltpu.HBM)


def _all_gather_call(block):
    def body(x_ref, out_ref, send_sems, recv_sems, local_sem):
        x, y, c = _my_place()
        me, sibling = (x, y, c), (x, y, 1 - c)
        chips = [(1 - x, y), (x, 1 - y), (1 - x, 1 - y)]

        def slot(px, py, pc):
            return out_ref.at[_place_index(px, py, pc)]

        def copy(k, blk, to, src=None):
            return pltpu.make_async_remote_copy(
                src_ref=slot(*blk) if src is None else src, dst_ref=slot(*blk),
                send_sem=send_sems.at[k], recv_sem=recv_sems.at[k],
                device_id=to, device_id_type=pl.DeviceIdType.MESH)

        mine = pltpu.make_async_copy(x_ref, slot(*me), local_sem)
        mine.start()
        first = [copy(0, me, sibling, src=x_ref)]
        first += [copy(1 + j, me, (*chip, c), src=x_ref) for j, chip in enumerate(chips)]
        for cp in first:
            cp.start()
        passed = [copy(4 + j, (*chip, c), sibling) for j, chip in enumerate(chips)]
        for j, chip in enumerate(chips):
            copy(1 + j, (*chip, c), me).wait_recv()
            passed[j].start()
        copy(0, sibling, me).wait_recv()
        for j, chip in enumerate(chips):
            copy(4 + j, (*chip, 1 - c), me).wait_recv()
        for cp in first + passed:
            cp.wait_send()
        mine.wait()

    return pl.pallas_call(
        body, name="all_gather",
        out_shape=jax.ShapeDtypeStruct((N_DEV,) + block.shape, block.dtype),
        in_specs=[HBM_SPEC], out_specs=HBM_SPEC,
        scratch_shapes=[pltpu.SemaphoreType.DMA((7,)), pltpu.SemaphoreType.DMA((7,)), pltpu.SemaphoreType.DMA],
    )(block)


SEM_SPEC = pl.BlockSpec(memory_space=pltpu.SEMAPHORE)
SIDE_EFFECT = pltpu.SideEffectType.DATAFLOW_SIDE_EFFECTING


def _peers():
    x, y, c = _my_place()
    out = []
    for k in range(1, N_DEV):
        peer = (x ^ (k >> 2), y ^ ((k >> 1) & 1), c ^ (k & 1))
        out.append((k - 1, peer, _place_index(*peer)))
    return _place_index(x, y, c), out


def _spread_start(src, per_peer, name, after=None):
    slot = src.shape[1:] if per_peer else src.shape
    order = () if after is None else (after,)

    def body(src_ref, land_ref, *rest):
        send_sems, recv_sems, src_thru, land_thru, token = rest[len(order):]
        mine, peers = _peers()
        for k, peer, peer_idx in peers:
            pltpu.make_async_remote_copy(
                src_ref=src_ref.at[peer_idx] if per_peer else src_ref, dst_ref=land_ref.at[mine],
                send_sem=send_sems.at[k], recv_sem=recv_sems.at[k],
                device_id=peer, device_id_type=pl.DeviceIdType.MESH).start()
        token[...] = jnp.zeros_like(token)

    return pl.pallas_call(
        body, name=name,
        out_shape=(pltpu.SemaphoreType.DMA((N_DEV - 1,)), pltpu.SemaphoreType.DMA((N_DEV - 1,)),
                   pltpu.HBM(src.shape, src.dtype), pltpu.HBM((N_DEV,) + slot, src.dtype),
                   jax.ShapeDtypeStruct((8, 128), F32)),
        in_specs=(HBM_SPEC, HBM_SPEC) + (pl.BlockSpec(memory_space=pl.ANY),) * len(order),
        out_specs=(SEM_SPEC, SEM_SPEC, HBM_SPEC, HBM_SPEC, pl.BlockSpec(memory_space=pltpu.VMEM)),
        input_output_aliases={0: 2, 1: 3},
        compiler_params=pltpu.CompilerParams(has_side_effects=SIDE_EFFECT),
    )(pltpu.with_memory_space_constraint(src, pltpu.HBM),
      pltpu.with_memory_space_constraint(lax.empty((N_DEV,) + slot, src.dtype), pltpu.HBM), *order)


def _spread_wait(handles, after, per_peer, name):
    send_sems, recv_sems, src_thru, land_thru = handles

    def body(src_ref, land_ref, send_sems, recv_sems, after_ref, src_dead, got_ref):
        _, peers = _peers()
        for k, peer, peer_idx in peers:
            copy = pltpu.make_async_remote_copy(
                src_ref=src_ref.at[peer_idx] if per_peer else src_ref, dst_ref=land_ref.at[peer_idx],
                send_sem=send_sems.at[k], recv_sem=recv_sems.at[k],
                device_id=peer, device_id_type=pl.DeviceIdType.MESH)
            copy.wait_send()
            copy.wait_recv()

    return pl.pallas_call(
        body, name=name,
        out_shape=(pltpu.HBM(src_thru.shape, src_thru.dtype), pltpu.HBM(land_thru.shape, land_thru.dtype)),
        in_specs=(HBM_SPEC, HBM_SPEC, SEM_SPEC, SEM_SPEC, pl.BlockSpec(memory_space=pl.ANY)),
        out_specs=(HBM_SPEC, HBM_SPEC), input_output_aliases={0: 0, 1: 1},
        compiler_params=pltpu.CompilerParams(has_side_effects=SIDE_EFFECT),
    )(src_thru, land_thru, send_sems, recv_sems, after)


def _sum_slots_call(slots):
    _, rows, cols = slots.shape
    tr = _row_tile_ragged(rows, cols, budget=512 * 1024)

    def body(s_ref, o_ref):
        acc = s_ref[0].astype(F32)
        for j in range(1, N_DEV):
            acc = acc + s_ref[j].astype(F32)
        o_ref[...] = acc

    return pl.pallas_call(
        body, name="sum_slots", grid=(pl.cdiv(rows, tr),),
        in_specs=[pl.BlockSpec((N_DEV, tr, cols), lambda i: (0, i, 0))],
        out_specs=pl.BlockSpec((tr, cols), lambda i: (i, 0)),
        out_shape=jax.ShapeDtypeStruct((rows, cols), F32),
        compiler_params=pltpu.CompilerParams(dimension_semantics=("parallel",)),
    )(slots)


def _with_own_slot(got, own, mine):
    return lax.dynamic_update_index_in_dim(got, own, mine, 0)


def _pack(vectors, width):
    flat = jnp.concatenate([v.reshape(-1) for v in vectors])
    return jnp.pad(flat, (0, width - flat.shape[0])).reshape(width // 128, 128)


def _unpack(packed, like):
    flat = packed.reshape(-1)
    out, at = [], 0
    for v in like:
        out.append(flat[at:at + v.size].reshape(v.shape))
        at += v.size
    return tuple(out)


def _sum_over_devices(grads):
    n = sum(v.size for v in grads)
    width = -(-n // 1024) * 1024
    return _unpack(_sum_slots_call(_all_gather_call(_pack(grads, width))), grads)


def _cols_from_slots(slots):
    n, rows, cols = slots.shape
    return slots.transpose(1, 0, 2).reshape(rows, n * cols)


def _rows_from_slots(slots):
    return slots.reshape(-1, slots.shape[2])


def _pad128(n):
    return -(-n // 128) * 128


def _pad_to_tiles(a, axis):
    n = a.shape[axis]
    pads = [(0, 0)] * a.ndim
    pads[axis] = (0, _pad128(n) - n)
    return jnp.pad(a, pads)


def _rwkv_group(a, rw, dl, al, gl, axis):
    take = lambda lo, hi: lax.slice_in_dim(a, lo, hi, axis=axis)
    at = 3 * rw
    parts = [take(0, at)]
    for n in (dl, al, gl):
        parts.append(_pad_to_tiles(take(at, at + n), axis))
        at += n
    return jnp.concatenate(parts, axis=axis)


def _in_proj_layout(slots, rw, fw, dl, al, gl):
    wt = _rows_from_slots(slots)
    rcols = 3 * rw + dl + al + gl
    fcols = 3 * fw + fw // HEAD_DIM
    return _rwkv_group(wt[:rcols], rw, dl, al, gl, 0), _pad_to_tiles(wt[rcols:rcols + fcols], 0), wt[rcols + fcols:]


def _low_rank_layout(slots):
    return _pad_to_tiles(_cols_from_slots(slots), 0)


def _stage_embed(meta, x, n1, lp):
    h0 = jnp.concatenate([meta, x, jnp.zeros((lp - meta.shape[0] - x.shape[0], x.shape[1]), F32)], axis=0)
    return h0, rmsnorm(h0, n1)


def _stage_mix(z_r, z_f, small, w2, a2, g2, dims):
    (mu, w0, a0, k_k, k_a, r_k, gn_w, gn_b, q_g, k_g, f_bias) = small
    rw, fw, dl, al, gl = dims
    fcols = 3 * fw + fw // HEAD_DIM

    r, lw, kf, v, na, b, g = rwkv_prep(z_r, _rwkv_group(mu, rw, dl, al, gl, 1), w0, a0, k_k, k_a, w2, a2, g2)
    y = wkv7(r, lw, kf, v, na, b)
    y_a = gn_bonus(y, r, kf, v, gn_w, gn_b, r_k.reshape(1, rw)) * g

    fq, fk, fv, fl = z_f[:, :fw], z_f[:, fw:2 * fw], z_f[:, 2 * fw:3 * fw], z_f[:, 3 * fw:fcols]
    fq = head_rms(fq, jnp.tile(q_g, (1, fw // HEAD_DIM))) * (HEAD_DIM ** -0.5)
    fk = head_rms(fk, jnp.tile(k_g, (1, fw // HEAD_DIM)))
    cum = jnp.cumsum(jax.nn.log_sigmoid(badd(fl, f_bias)), axis=0)
    y_b = fox_attention(fq, fk, fv, cum)
    return y_a, y_b


def _stage_merge(h0, y_a, y_b, z_g, w_a, w_b, w_o):
    merged = gated_merge(z_g, dense_cols_bf16(y_a, w_a), dense_cols_bf16(y_b, w_b))
    return h0 + dense(merged, w_o)


def _stage_ffn(h1, n2, w_gu, w_dn):
    return h1 + dense(swiglu(dense_cols_bf16(rmsnorm(h1, n2), w_gu)), w_dn)


SHARDED = ("meta_tokens", "w_in", "rwkv_w2", "rwkv_a2", "rwkv_g2", "w_branch_a", "w_branch_b", "w_o", "w_gate_up", "w_down")
SMALL = ("norm1_g", "rwkv_mu", "rwkv_w0", "rwkv_a0", "rwkv_k_k", "rwkv_k_a", "rwkv_r_k", "rwkv_gn_w", "rwkv_gn_b",
         "fox_q_norm_g", "fox_k_norm_g", "fox_f_bias", "norm2_g")
WEIGHTS = ("meta_tokens", "norm1_g", "w_in", "rwkv_mu", "rwkv_w0", "rwkv_w2", "rwkv_a0", "rwkv_a2", "rwkv_g2", "rwkv_k_k",
           "rwkv_k_a", "rwkv_r_k", "rwkv_gn_w", "rwkv_gn_b", "fox_q_norm_g", "fox_k_norm_g", "fox_f_bias", "w_branch_a",
           "w_branch_b", "w_o", "norm2_g", "w_gate_up", "w_down")


def _as2d(a):
    return a.reshape(-1, a.shape[-1])


def kernel(x, meta_tokens, norm1_g, w_in, rwkv_mu, rwkv_w0, rwkv_w2, rwkv_a0, rwkv_a2, rwkv_g2, rwkv_k_k, rwkv_k_a, rwkv_r_k, rwkv_gn_w, rwkv_gn_b, fox_q_norm_g, fox_k_norm_g, fox_f_bias, w_branch_a, w_branch_b, w_o, norm2_g, w_gate_up, w_down, loss_target, m_meta_tokens, m_norm1_g, m_w_in, m_rwkv_mu, m_rwkv_w0, m_rwkv_w2, m_rwkv_a0, m_rwkv_a2, m_rwkv_g2, m_rwkv_k_k, m_rwkv_k_a, m_rwkv_r_k, m_rwkv_gn_w, m_rwkv_gn_b, m_fox_q_norm_g, m_fox_k_norm_g, m_fox_f_bias, m_w_branch_a, m_w_branch_b, m_w_o, m_norm2_g, m_w_gate_up, m_w_down, v_meta_tokens, v_norm1_g, v_w_in, v_rwkv_mu, v_rwkv_w0, v_rwkv_w2, v_rwkv_a0, v_rwkv_a2, v_rwkv_g2, v_rwkv_k_k, v_rwkv_k_a, v_rwkv_r_k, v_rwkv_gn_w, v_rwkv_gn_b, v_fox_q_norm_g, v_fox_k_norm_g, v_fox_f_bias, v_w_branch_a, v_w_branch_b, v_w_o, v_norm2_g, v_w_gate_up, v_w_down):
    given = dict(locals())
    w = {n: given[n] for n in WEIGHTS}
    assert rwkv_r_k.shape[-1] == HEAD_DIM
    n_meta, seq = meta_tokens.shape[0], x.shape[1]
    tokens = n_meta + seq
    lp = -(-tokens // TOKEN_TILE) * TOKEN_TILE
    mine = _place_index(*(lax.axis_index(a) for a in MESH_AXES))
    x2 = x[0]

    local = {n: _as2d(given[n]) for n in given if n != "x" and n != "loss_target"}
    for n in ("w_in", "m_w_in", "v_w_in"):
        local[n] = jnp.transpose(given[n][0])
    blocks = {n: local[n].astype(F32 if n == "meta_tokens" else BF16) for n in SHARDED}
    first = ("meta_tokens", "rwkv_w2", "rwkv_a2", "rwkv_g2")
    started = {n: _spread_start(blocks[n], False, "gather_start_" + n) for n in first}
    zero = sum(started[n][4][0, 0] for n in first)

    def gathered(n, after):
        own, got = _spread_wait(started[n][:4], after, False, "gather_wait_" + n)
        return _with_own_slot(got, own, mine)

    sm = {n: _as2d(w[n]) for n in SMALL}
    small_mix = tuple(sm[n] for n in SMALL[1:-1])
    n1 = sm["norm1_g"] + zero
    rw, fw = w_branch_a.shape[-2], w_branch_b.shape[-2]
    dims = (rw, fw, rwkv_w2.shape[-2], rwkv_a2.shape[-2], rwkv_g2.shape[-2])
    same = lambda s: (s,)

    meta, un_meta = jax.vjp(_cols_from_slots, gathered("meta_tokens", x2))
    (h0, xn), vjp_embed = jax.vjp(lambda m, xs, g: _stage_embed(m, xs, g, lp), meta, x2, n1)
    in_slots = _all_gather_call(blocks["w_in"])
    later = [n for n in SHARDED if n not in first and n != "w_in"]
    started.update({n: _spread_start(blocks[n], False, "gather_start_" + n, after=in_slots) for n in later})
    w_groups, un_in = jax.vjp(lambda s: _in_proj_layout(s, *dims), in_slots)
    xn_b = xn.astype(BF16)
    behind = sum(started[n][4] for n in later)
    z_r, z_f, z_g = (_matmul(xn_b, wg, tb=True, name="in_proj_" + tag, after=behind) for wg, tag in zip(w_groups, "rfg"))
    (w2, un_w2), (a2, un_a2), (g2, un_g2) = (jax.vjp(_low_rank_layout, gathered(n, xn)) for n in ("rwkv_w2", "rwkv_a2", "rwkv_g2"))
    (y_a, y_b), vjp_mix = jax.vjp(lambda zr, zf, s, a, b, c: _stage_mix(zr, zf, s, a, b, c, dims),
                                  z_r, z_f, small_mix, w2, a2, g2)
    w_a, w_b = gathered("w_branch_a", y_a), gathered("w_branch_b", y_a)
    w_o_full, un_wo = jax.vjp(_rows_from_slots, gathered("w_o", y_a))
    h1, vjp_merge = jax.vjp(_stage_merge, h0, y_a, y_b, z_g, w_a, w_b, w_o_full)
    w_gu = gathered("w_gate_up", h1)
    w_dn, un_dn = jax.vjp(_rows_from_slots, gathered("w_down", h1))
    y, vjp_ffn = jax.vjp(_stage_ffn, h1, sm["norm2_g"], w_gu, w_dn)

    loss_part, dy_real = _loss_call(y[n_meta:tokens], loss_target[0])
    dy = jnp.pad(dy_real, ((n_meta, lp - tokens), (0, 0)))
    loss = lax.psum(loss_part[0, 0], MESH_AXES)

    sent = {}

    def send_grad(n, dmat, unlayout):
        sent[n] = _spread_start(unlayout(dmat)[0], True, "grad_start_" + n)
        return sent[n][4][0, 0]

    d_h1, d_n2, d_wgu, d_wdn = vjp_ffn(dy)
    behind = send_grad("w_gate_up", d_wgu, same) + send_grad("w_down", d_wdn, un_dn)
    d_h0, d_ya, d_yb, d_zg, d_wa, d_wb, d_wo = vjp_merge(d_h1 + behind)
    behind = send_grad("w_o", d_wo, un_wo) + send_grad("w_branch_a", d_wa, same) + send_grad("w_branch_b", d_wb, same)
    d_zr, d_zf, d_small_mix, d_w2, d_a2, d_g2 = vjp_mix((d_ya + behind, d_yb))
    dproj_b = jnp.concatenate([d_zr.astype(BF16), d_zf.astype(BF16), d_zg.astype(BF16)], axis=1)
    d_wcat = _matmul(dproj_b, xn_b, ta=True, out_dtype=BF16, name="in_proj_dw")
    ends = (w_groups[0].shape[0], w_groups[0].shape[0] + w_groups[1].shape[0])
    send_grad("w_in", (d_wcat[:ends[0]], d_wcat[ends[0]:ends[1]], d_wcat[ends[1]:]), un_in)
    d_xn = _matmul(dproj_b, jnp.concatenate(w_groups, axis=0), out_dtype=F32, name="in_proj_dx", after=sent["w_in"][4])
    send_grad("rwkv_w2", d_w2, un_w2)
    send_grad("rwkv_a2", d_a2, un_a2)
    send_grad("rwkv_g2", d_g2, un_g2)
    d_meta, g_x, d_n1 = vjp_embed((d_h0, d_xn))
    send_grad("meta_tokens", d_meta, un_meta)

    grads = dict(zip(SMALL, _sum_over_devices((d_n1, *d_small_mix, d_n2))))
    grads = {n: g.reshape(w[n].shape) for n, g in grads.items()}

    delta, new_m, new_v = {}, {}, {}
    after = g_x
    for n in ("w_gate_up", "w_down", "w_o", "w_branch_a", "w_branch_b", "rwkv_g2", "rwkv_a2", "rwkv_w2", "meta_tokens", "w_in"):
        src, got = _spread_wait(sent[n][:4], after, True, "grad_wait_" + n)
        g = _sum_slots_call(_with_own_slot(got, lax.dynamic_index_in_dim(src, mine, 0, keepdims=False), mine))
        d_, m_, v_ = _adamw_call(local[n], g, local["m_" + n], local["v_" + n])
        back = (lambda t: jnp.transpose(t)[None]) if n == "w_in" else (lambda t: t.reshape(w[n].shape))
        grads[n], delta[n], new_m[n], new_v[n] = (back(t) for t in (g, d_, m_, v_))
        after = m_
    n_small = sum(w[n].size for n in SMALL)
    width = -(-n_small // 1024) * 1024
    packs = [_pack([src[n] if p == "" else given[p + n] for n in SMALL], width)
             for p, src in (("", w), ("", grads), ("m_", None), ("v_", None))]
    like = [w[n] for n in SMALL]
    for out, packed in zip((delta, new_m, new_v), _adamw_call(*packs)):
        out.update(dict(zip(SMALL, _unpack(packed, like))))

    return (loss, g_x[None], *[grads[n] for n in WEIGHTS], *[delta[n] for n in WEIGHTS],
            *[new_m[n] for n in WEIGHTS], *[new_v[n] for n in WEIGHTS])
```

```python
import functools

import jax
import jax.numpy as jnp
from jax import lax
from jax.experimental import pallas as pl
from jax.experimental.pallas import tpu as pltpu

F32 = jnp.float32
BF16 = jnp.bfloat16

N_DEV = 8
MESH_AXES = ("x", "y", "c")
HEAD_DIM = 64
TOKEN_TILE = 128
WKV_CHUNK = 64
WKV_PAIRS_PER_STEP = 8
PAIR = 2 * HEAD_DIM
ATTN_BLOCK = 128
ATTN_BLOCK_BIG = 384
RMS_EPS = 1e-6
GN_EPS = 64e-5
L2_FLOOR = 1e-12
NEG_BIG = -1e30
ADAM_LR, ADAM_B1, ADAM_B2, ADAM_EPS, ADAM_WD, ADAM_STEP = 0.001, 0.9, 0.999, 1e-08, 0.01, 10
VMEM_BYTES_V7X = 64 * 1024 * 1024
VMEM_LIMIT_CAP = 56 * 1024 * 1024
VMEM_LIMIT_FLOOR = 32 * 1024 * 1024
MATMUL_VMEM_BUDGET = 36 * 1024 * 1024
GRID_STEP_BYTES = 1024 * 1024
ACC_BYTES_PER_HBM_BYTE = 6


def _vmem_limit(estimate_bytes):
    return int(min(max(estimate_bytes * 5 // 4, VMEM_LIMIT_FLOOR), VMEM_LIMIT_CAP))


def _pick(dim, cands):
    for c in cands:
        if dim % c == 0:
            return c
    return dim


def _row_tile(rows, width, itemsize=4, budget=2 * 1024 * 1024):
    for c in (1408, 1024, 704, 512, 384, 256, 128, 64, 32, 16, 8):
        if rows % c == 0 and c * width * itemsize <= budget:
            return c
    return rows


def _row_tile_ragged(rows, width, itemsize=4, budget=2 * 1024 * 1024):
    tile = _row_tile(rows, width, itemsize, budget)
    if tile * width * itemsize <= budget or rows < 16:
        return tile
    padded = -(-rows // 16) * 16
    for c in (1408, 1024, 704, 512, 384, 336, 256, 192, 128, 96, 64, 48, 32, 16):
        if padded % c == 0 and c * width * itemsize <= budget:
            return c
    return tile


def _dg(a, b, ta, tb):
    dims = (((0 if ta else 1,), (1 if tb else 0,)), ((), ()))
    return lax.dot_general(a, b, dims, preferred_element_type=F32)


def _split(x, n):
    parts = []
    for _ in range(n):
        h = x.astype(BF16)
        parts.append(h)
        x = x - h.astype(F32)
    return parts


def _mm(a, b, ta=False, tb=False):
    return _dg(a.astype(BF16), b.astype(BF16), ta, tb)


def _matmul(a, b, ta=False, tb=False, out_dtype=F32, name="matmul", after=None, b_slots=False, out_slots=0):
    if ta:
        kdim, m = a.shape
    else:
        m, kdim = a.shape
    if b_slots:
        n_slots, brows, bcols = b.shape
        n, k2 = (brows, n_slots * bcols) if tb else (n_slots * bcols, brows)
    elif tb:
        n, k2 = b.shape
    else:
        k2, n = b.shape
    assert kdim == k2, (a.shape, b.shape, ta, tb)
    sa, sb, so = a.dtype.itemsize, b.dtype.itemsize, jnp.dtype(out_dtype).itemsize
    n_unit = bcols if (b_slots and not tb) else (n // out_slots if out_slots else n)
    k_unit = bcols if (b_slots and tb) else kdim
    tm, tn, tk, n_outer = _matmul_tiles(m, n, kdim, ta, sa, sb, so, n_unit, k_unit)
    nk = kdim // tk
    ij = (lambda f: lambda j, i, k: f(i, j, k)) if n_outer else (lambda f: f)

    order = () if after is None else (after,)

    def body(a_ref, b_ref, *rest):
        o_ref, acc = rest[len(order)], rest[len(order) + 1:]
        part = _dg(a_ref[...].astype(BF16), b_ref[...].astype(BF16), ta, tb)
        if nk == 1:
            o_ref[...] = part.astype(o_ref.dtype)
            return
        kk = pl.program_id(2)

        @pl.when(kk == 0)
        def _():
            acc[0][...] = part

        @pl.when(kk > 0)
        def _():
            acc[0][...] += part

        @pl.when(kk == nk - 1)
        def _():
            o_ref[...] = acc[0][...].astype(o_ref.dtype)

    a_spec = pl.BlockSpec((tk, tm), ij(lambda i, j, k: (k, i))) if ta else pl.BlockSpec((tm, tk), ij(lambda i, j, k: (i, k)))
    if b_slots and tb:
        per = bcols // tk
        b_spec = pl.BlockSpec((None, tn, tk), ij(lambda i, j, k: (k // per, j, k % per)))
    elif b_slots:
        per = bcols // tn
        b_spec = pl.BlockSpec((None, tk, tn), ij(lambda i, j, k: (j // per, k, j % per)))
    elif tb:
        b_spec = pl.BlockSpec((tn, tk), ij(lambda i, j, k: (j, k)))
    else:
        b_spec = pl.BlockSpec((tk, tn), ij(lambda i, j, k: (k, j)))
    if out_slots:
        per_out = n // out_slots // tn
        out_spec = pl.BlockSpec((None, tm, tn), ij(lambda i, j, k: (j // per_out, i, j % per_out)))
        out_shape = jax.ShapeDtypeStruct((out_slots, m, n // out_slots), out_dtype)
    else:
        out_spec = pl.BlockSpec((tm, tn), ij(lambda i, j, k: (i, j)))
        out_shape = jax.ShapeDtypeStruct((m, n), out_dtype)
    return pl.pallas_call(
        body, name=name,
        grid=(n // tn, m // tm, nk) if n_outer else (m // tm, n // tn, nk),
        in_specs=[a_spec, b_spec] + [pl.BlockSpec(memory_space=pl.ANY)] * len(order),
        out_specs=out_spec,
        out_shape=out_shape,
        scratch_shapes=[pltpu.VMEM((tm, tn), F32)] if nk > 1 else [],
        compiler_params=pltpu.CompilerParams(dimension_semantics=("parallel", "parallel", "arbitrary"),
                                             vmem_limit_bytes=_vmem_limit(_matmul_vmem(tm, tn, tk, nk, sa, sb, so))),
    )(a, b, *order)


def _matmul_vmem(tm, tn, tk, nk, sa, sb, so):
    return 2 * (tm * tk * sa + tk * tn * sb + tm * tn * so) + tm * tn * 4 + (tm * tn * 4 if nk > 1 else 0)


def _matmul_tiles(m, n, kdim, ta, sa, sb, so, n_unit, k_unit):
    lane = (2816, 2176, 2048, 1408, 1024, 640, 512, 384, 256, 128)
    sublane = (2816, 2176, 2048, 1408, 1024, 704, 512, 384, 256, 128)
    divs = lambda dim, cands: [c for c in cands if dim % c == 0] or [dim]
    best = None
    for tm in divs(m, lane if ta else sublane):
        for tn in divs(n_unit, lane):
            for tk in divs(k_unit, sublane if ta else lane) + ([kdim] if k_unit == kdim and (ta or kdim <= 2048) else []):
                nk, nm, nn = kdim // tk, m // tm, n // tn
                if _matmul_vmem(tm, tn, tk, nk, sa, sb, so) > MATMUL_VMEM_BUDGET:
                    continue
                acc_bytes = m * n * 4 * 3 * nk // ACC_BYTES_PER_HBM_BYTE if nk > 1 else 0
                fixed = m * n * so + acc_bytes + nm * nn * nk * GRID_STEP_BYTES
                for n_outer in (False, True):
                    if n_outer:
                        a_reads, b_reads = (1 if (nk == 1 and nm == 1) else nn), (1 if nk == 1 else nm)
                    else:
                        a_reads, b_reads = (1 if nk == 1 else nn), (1 if (nk == 1 and nn == 1) else nm)
                    cost = m * kdim * sa * a_reads + kdim * n * sb * b_reads + fixed
                    if best is None or cost < best[0]:
                        best = (cost, tm, tn, tk, n_outer)
    return best[1:]


@jax.custom_vjp
def dense(x, w):
    return _matmul(x.astype(BF16), w, name="dense_fwd")


def _dense_fwd(x, w):
    return _matmul(x.astype(BF16), w, name="dense_fwd"), (x.astype(BF16), w, jnp.zeros((), x.dtype))


def _dense_bwd(res, dy):
    xb, w, like = res
    dyb = dy.astype(BF16)
    dx = _matmul(dyb, w, tb=True, out_dtype=like.dtype, name="dense_dx")
    dw = _matmul(xb, dyb, ta=True, out_dtype=w.dtype, name="dense_dw")
    return dx, dw


dense.defvjp(_dense_fwd, _dense_bwd)


def _make_dense_cols(out_dtype):
    @jax.custom_vjp
    def op(x, w_slots):
        return _matmul(x.astype(BF16), w_slots, b_slots=True, out_dtype=out_dtype, name="dense_cols_fwd")

    def fwd(x, w_slots):
        assert x.dtype == F32
        xb = x.astype(BF16)
        return _matmul(xb, w_slots, b_slots=True, out_dtype=out_dtype, name="dense_cols_fwd"), (xb, w_slots)

    def bwd(res, dy):
        xb, w_slots = res
        dyb = dy.astype(BF16)
        dx = _matmul(dyb, w_slots, tb=True, b_slots=True, out_dtype=F32, name="dense_cols_dx")
        dw = _matmul(xb, dyb, ta=True, out_slots=w_slots.shape[0], out_dtype=w_slots.dtype, name="dense_cols_dw")
        return dx, dw

    op.defvjp(fwd, bwd)
    return op


dense_cols = _make_dense_cols(F32)
dense_cols_bf16 = _make_dense_cols(BF16)


def _swiglu_call(gu, d_act=None):
    rows, two_f = gu.shape
    f = two_f // 2
    tr = _row_tile(rows, two_f, itemsize=2, budget=3 * 1024 * 1024)
    half = lambda j: pl.BlockSpec((tr, f), lambda i, j=j: (i, j))
    ops = (gu, gu) if d_act is None else (gu, gu, d_act)

    def body(*refs):
        g, u = refs[0][...].astype(F32), refs[1][...].astype(F32)
        s = 1.0 / (1.0 + jnp.exp(-g))
        if d_act is None:
            refs[2][...] = (g * s * u).astype(BF16)
        else:
            d = refs[2][...].astype(F32)
            refs[3][:, :f] = (d * u * s * (1.0 + g * (1.0 - s))).astype(BF16)
            refs[3][:, f:] = (d * g * s).astype(BF16)

    width = f if d_act is None else two_f
    return pl.pallas_call(
        body, name="swiglu_fwd" if d_act is None else "swiglu_bwd", grid=(rows // tr,),
        in_specs=[half(0), half(1)] + ([half(0)] if d_act is not None else []),
        out_specs=pl.BlockSpec((tr, width), lambda i: (i, 0)),
        out_shape=jax.ShapeDtypeStruct((rows, width), BF16),
        compiler_params=pltpu.CompilerParams(dimension_semantics=("parallel",)),
    )(*ops)


@jax.custom_vjp
def swiglu(gu):
    return _swiglu_call(gu)


swiglu.defvjp(lambda gu: (_swiglu_call(gu), gu), lambda gu, d_act: (_swiglu_call(gu, d_act),))


def _merge_call(zg, a, b, dm=None):
    rows, d = a.shape
    tr = _row_tile(rows, d, budget=1024 * 1024)
    half = lambda j: pl.BlockSpec((tr, d), lambda i, j=j: (i, j))
    tile = half(0)

    def body(*refs):
        ga = 1.0 / (1.0 + jnp.exp(-refs[0][...]))
        gb = 1.0 / (1.0 + jnp.exp(-refs[1][...]))
        av, bv = refs[2][...].astype(F32), refs[3][...].astype(F32)
        if dm is None:
            refs[4][...] = (ga * av + gb * bv).astype(BF16)
        else:
            dv = refs[4][...].astype(F32)
            dzg_ref, da_ref, db_ref = refs[5:]
            dzg_ref[:, :d] = dv * av * ga * (1.0 - ga)
            dzg_ref[:, d:] = dv * bv * gb * (1.0 - gb)
            da_ref[...] = (dv * ga).astype(BF16)
            db_ref[...] = (dv * gb).astype(BF16)

    shape_b = jax.ShapeDtypeStruct((rows, d), BF16)
    if dm is None:
        out_specs, out_shape, ops = tile, shape_b, (zg, zg, a, b)
    else:
        out_specs = [pl.BlockSpec((tr, 2 * d), lambda i: (i, 0)), tile, tile]
        out_shape = [jax.ShapeDtypeStruct((rows, 2 * d), F32), shape_b, shape_b]
        ops = (zg, zg, a, b, dm)
    return pl.pallas_call(
        body, name="merge_fwd" if dm is None else "merge_bwd", grid=(rows // tr,),
        in_specs=[half(0), half(1)] + [tile] * (len(ops) - 2),
        out_specs=out_specs, out_shape=out_shape,
        compiler_params=pltpu.CompilerParams(dimension_semantics=("parallel",)),
    )(*ops)


@jax.custom_vjp
def gated_merge(zg, a, b):
    return _merge_call(zg, a, b)


gated_merge.defvjp(lambda zg, a, b: (_merge_call(zg, a, b), (zg, a, b)),
                   lambda res, dm: tuple(_merge_call(*res, dm)))


def _rms_fwd_call(x, g):
    rows, d = x.shape
    tr = _row_tile(rows, d)

    def body(x_ref, g_ref, y_ref):
        xv = x_ref[...]
        rstd = lax.rsqrt(jnp.mean(xv * xv, axis=1, keepdims=True) + RMS_EPS)
        y_ref[...] = (xv * rstd) * g_ref[...]

    return pl.pallas_call(
        body, name="rms_fwd", grid=(rows // tr,),
        in_specs=[pl.BlockSpec((tr, d), lambda i: (i, 0)), pl.BlockSpec((1, d), lambda i: (0, 0))],
        out_specs=pl.BlockSpec((tr, d), lambda i: (i, 0)),
        out_shape=jax.ShapeDtypeStruct((rows, d), F32),
        compiler_params=pltpu.CompilerParams(dimension_semantics=("parallel",)),
    )(x, g)


def _rms_bwd_call(x, g, dy):
    rows, d = x.shape
    tr = _row_tile(rows, d)

    def body(x_ref, g_ref, dy_ref, dx_ref, dg_ref):
        @pl.when(pl.program_id(0) == 0)
        def _():
            dg_ref[...] = jnp.zeros_like(dg_ref)

        xv = x_ref[...]
        dyv = dy_ref[...]
        rstd = lax.rsqrt(jnp.mean(xv * xv, axis=1, keepdims=True) + RMS_EPS)
        xhat = xv * rstd
        dxhat = dyv * g_ref[...]
        dx_ref[...] = rstd * (dxhat - xhat * jnp.mean(dxhat * xhat, axis=1, keepdims=True))
        dg_ref[...] += jnp.sum(dyv * xhat, axis=0, keepdims=True)

    return pl.pallas_call(
        body, name="rms_bwd", grid=(rows // tr,),
        in_specs=[pl.BlockSpec((tr, d), lambda i: (i, 0)), pl.BlockSpec((1, d), lambda i: (0, 0)),
                  pl.BlockSpec((tr, d), lambda i: (i, 0))],
        out_specs=[pl.BlockSpec((tr, d), lambda i: (i, 0)), pl.BlockSpec((1, d), lambda i: (0, 0))],
        out_shape=[jax.ShapeDtypeStruct((rows, d), F32), jax.ShapeDtypeStruct((1, d), F32)],
        compiler_params=pltpu.CompilerParams(dimension_semantics=("arbitrary",)),
    )(x, g, dy)


@jax.custom_vjp
def rmsnorm(x, g):
    return _rms_fwd_call(x, g)


rmsnorm.defvjp(lambda x, g: (_rms_fwd_call(x, g), (x, g)), lambda res, dy: tuple(_rms_bwd_call(res[0], res[1], dy)))


def _bcast_call(x, p, mul):
    rows, d = x.shape
    tr = _row_tile(rows, d)

    def body(x_ref, p_ref, y_ref):
        y_ref[...] = x_ref[...] * p_ref[...] if mul else x_ref[...] + p_ref[...]

    return pl.pallas_call(
        body, name="bcast_mul" if mul else "bcast_add", grid=(rows // tr,),
        in_specs=[pl.BlockSpec((tr, d), lambda i: (i, 0)), pl.BlockSpec((1, d), lambda i: (0, 0))],
        out_specs=pl.BlockSpec((tr, d), lambda i: (i, 0)),
        out_shape=jax.ShapeDtypeStruct((rows, d), F32),
        compiler_params=pltpu.CompilerParams(dimension_semantics=("parallel",)),
    )(x, p)


def _colsum_call(a, b=None):
    rows, d = a.shape
    tr = _row_tile(rows, d)
    ops = (a,) if b is None else (a, b)

    def body(*refs):
        o_ref = refs[-1]

        @pl.when(pl.program_id(0) == 0)
        def _():
            o_ref[...] = jnp.zeros_like(o_ref)

        v = refs[0][...] if b is None else refs[0][...] * refs[1][...]
        o_ref[...] += jnp.sum(v, axis=0, keepdims=True)

    return pl.pallas_call(
        body, name="colsum", grid=(rows // tr,),
        in_specs=[pl.BlockSpec((tr, d), lambda i: (i, 0))] * len(ops),
        out_specs=pl.BlockSpec((1, d), lambda i: (0, 0)),
        out_shape=jax.ShapeDtypeStruct((1, d), F32),
        compiler_params=pltpu.CompilerParams(dimension_semantics=("arbitrary",)),
    )(*ops)


@jax.custom_vjp
def badd(x, p):
    return _bcast_call(x, p, False)


badd.defvjp(lambda x, p: (_bcast_call(x, p, False), None), lambda res, dy: (dy, _colsum_call(dy)))


def _head_sums(x):
    i = lax.broadcasted_iota(jnp.int32, (PAIR, PAIR), 0) // HEAD_DIM
    j = lax.broadcasted_iota(jnp.int32, (PAIR, PAIR), 1) // HEAD_DIM
    ones = jnp.where(i == j, 1.0, 0.0).astype(BF16)
    hi, lo = _split(x, 2)
    cols = [slice(p * PAIR, (p + 1) * PAIR) for p in range(x.shape[1] // PAIR)]
    return jnp.concatenate([_dg(hi[:, c], ones, False, False) + _dg(lo[:, c], ones, False, False) for c in cols], axis=1)


def _head_rms_fwd_call(x, g):
    rows, w = x.shape
    tr = _row_tile(rows, w, budget=1024 * 1024)

    def body(x_ref, g_ref, y_ref):
        xv = x_ref[...]
        rstd = lax.rsqrt(_head_sums(xv * xv) * (1.0 / HEAD_DIM) + RMS_EPS)
        y_ref[...] = (xv * rstd) * g_ref[...]

    return pl.pallas_call(
        body, name="head_rms_fwd", grid=(rows // tr,),
        in_specs=[pl.BlockSpec((tr, w), lambda i: (i, 0)), pl.BlockSpec((1, w), lambda i: (0, 0))],
        out_specs=pl.BlockSpec((tr, w), lambda i: (i, 0)),
        out_shape=jax.ShapeDtypeStruct((rows, w), F32),
        compiler_params=pltpu.CompilerParams(dimension_semantics=("parallel",)),
    )(x, g)


def _head_rms_bwd_call(x, g, dy):
    rows, w = x.shape
    tr = _row_tile(rows, w, budget=1024 * 1024)

    def body(x_ref, g_ref, dy_ref, dx_ref, dg_ref):
        @pl.when(pl.program_id(0) == 0)
        def _():
            dg_ref[...] = jnp.zeros_like(dg_ref)

        xv, dyv = x_ref[...], dy_ref[...]
        rstd = lax.rsqrt(_head_sums(xv * xv) * (1.0 / HEAD_DIM) + RMS_EPS)
        xhat = xv * rstd
        dxhat = dyv * g_ref[...]
        dx_ref[...] = rstd * (dxhat - xhat * (_head_sums(dxhat * xhat) * (1.0 / HEAD_DIM)))
        dg_ref[...] += jnp.sum(dyv * xhat, axis=0, keepdims=True)

    return pl.pallas_call(
        body, name="head_rms_bwd", grid=(rows // tr,),
        in_specs=[pl.BlockSpec((tr, w), lambda i: (i, 0)), pl.BlockSpec((1, w), lambda i: (0, 0)),
                  pl.BlockSpec((tr, w), lambda i: (i, 0))],
        out_specs=[pl.BlockSpec((tr, w), lambda i: (i, 0)), pl.BlockSpec((1, w), lambda i: (0, 0))],
        out_shape=[jax.ShapeDtypeStruct((rows, w), F32), jax.ShapeDtypeStruct((1, w), F32)],
        compiler_params=pltpu.CompilerParams(dimension_semantics=("arbitrary",)),
    )(x, g, dy)


@jax.custom_vjp
def head_rms(x, g):
    return _head_rms_fwd_call(x, g)


head_rms.defvjp(lambda x, g: (_head_rms_fwd_call(x, g), (x, g)),
                lambda res, dy: tuple(_head_rms_bwd_call(res[0], res[1], dy)))


def _gn_fwd_call(y, r, kf, v, gw, gb, rk):
    rows, w = y.shape
    tr = _row_tile(rows, w, budget=512 * 1024)

    def body(y_ref, r_ref, kf_ref, v_ref, gw_ref, gb_ref, rk_ref, o_ref):
        yv = y_ref[...]
        yc = yv - _head_sums(yv) * (1.0 / HEAD_DIM)
        rstd = lax.rsqrt(_head_sums(yc * yc) * (1.0 / HEAD_DIM) + GN_EPS)
        s = _head_sums(r_ref[...] * kf_ref[...] * rk_ref[...])
        o_ref[...] = (yc * rstd) * gw_ref[...] + gb_ref[...] + s * v_ref[...]

    tok = pl.BlockSpec((tr, w), lambda i: (i, 0))
    par = pl.BlockSpec((1, w), lambda i: (0, 0))
    return pl.pallas_call(
        body, name="gn_bonus_fwd", grid=(rows // tr,),
        in_specs=[tok] * 4 + [par] * 3, out_specs=tok,
        out_shape=jax.ShapeDtypeStruct((rows, w), F32),
        compiler_params=pltpu.CompilerParams(dimension_semantics=("parallel",)),
    )(y, r, kf, v, gw, gb, rk)


def _gn_bwd_call(y, r, kf, v, gw, gb, rk, do):
    rows, w = y.shape
    tr = _row_tile(rows, w, budget=512 * 1024)

    def body(y_ref, r_ref, kf_ref, v_ref, gw_ref, rk_ref, do_ref,
             dy_ref, dr_ref, dkf_ref, dv_ref, dgw_ref, dgb_ref, drk_ref):
        @pl.when(pl.program_id(0) == 0)
        def _():
            dgw_ref[...] = jnp.zeros_like(dgw_ref)
            dgb_ref[...] = jnp.zeros_like(dgb_ref)
            drk_ref[...] = jnp.zeros_like(drk_ref)

        yv, rv, kv, vv, dov, rkv = y_ref[...], r_ref[...], kf_ref[...], v_ref[...], do_ref[...], rk_ref[...]
        mean = lambda t: _head_sums(t) * (1.0 / HEAD_DIM)
        yc = yv - mean(yv)
        rstd = lax.rsqrt(mean(yc * yc) + GN_EPS)
        yhat = yc * rstd
        dyhat = dov * gw_ref[...]
        dy_ref[...] = rstd * (dyhat - mean(dyhat) - yhat * mean(dyhat * yhat))
        s = _head_sums(rv * kv * rkv)
        ds = _head_sums(dov * vv)
        dv_ref[...] = s * dov
        dr_ref[...] = ds * kv * rkv
        dkf_ref[...] = ds * rv * rkv
        dgw_ref[...] += jnp.sum(dov * yhat, axis=0, keepdims=True)
        dgb_ref[...] += jnp.sum(dov, axis=0, keepdims=True)
        drk_ref[...] += jnp.sum(ds * rv * kv, axis=0, keepdims=True)

    tok = pl.BlockSpec((tr, w), lambda i: (i, 0))
    par = pl.BlockSpec((1, w), lambda i: (0, 0))
    tshape = jax.ShapeDtypeStruct((rows, w), F32)
    pshape = jax.ShapeDtypeStruct((1, w), F32)
    return pl.pallas_call(
        body, name="gn_bonus_bwd", grid=(rows // tr,),
        in_specs=[tok] * 4 + [par] * 2 + [tok], out_specs=[tok] * 4 + [par] * 3,
        out_shape=[tshape] * 4 + [pshape] * 3,
        compiler_params=pltpu.CompilerParams(dimension_semantics=("arbitrary",)),
    )(y, r, kf, v, gw, rk, do)


@jax.custom_vjp
def gn_bonus(y, r, kf, v, gw, gb, rk):
    return _gn_fwd_call(y, r, kf, v, gw, gb, rk)


def _gn_bwd(res, do):
    y, r, kf, v, gw, gb, rk = res
    dy, dr, dkf, dv, dgw, dgb, drk = _gn_bwd_call(y, r, kf, v, gw, gb, rk, do)
    return dy, dr, dkf, dv, dgw, dgb, drk


gn_bonus.defvjp(lambda *a: (_gn_fwd_call(*a), a), _gn_bwd)


PREP_ROWS = 128


def _prep_segments(rw, lora_w, lora_a, lora_g):
    at = 3 * rw
    seg = {"r": (0, rw), "k": (rw, 2 * rw), "v": (2 * rw, 3 * rw)}
    for name, n in (("wd", lora_w), ("ad", lora_a), ("gd", lora_g)):
        seg[name] = (at, at + _pad128(n))
        at += _pad128(n)
    return seg, at


def _prep_shifted(z_ref, zlast_ref, mu_ref, seg, first_tile):
    lo, hi = seg
    zr = z_ref[:, lo:hi]
    rows = zr.shape[0]
    before = jnp.where(first_tile, 0.0, zlast_ref[7:8, lo:hi])
    row0 = lax.broadcasted_iota(jnp.int32, zr.shape, 0) == 0
    diff = jnp.where(row0, before, pltpu.roll(zr, 1, axis=0)) - zr
    return zr + diff * mu_ref[:, lo:hi], diff


def _prep_forward_values(z_ref, zlast_ref, mu_ref, w0_ref, a0_ref, kk_ref, ka_ref, w2_ref, a2_ref, g2_ref, segs, first_tile):
    z = {n: _prep_shifted(z_ref, zlast_ref, mu_ref, segs[n], first_tile) for n in segs}
    r, k, v, wd, ad, gd = (z[n][0] for n in ("r", "k", "v", "wd", "ad", "gd"))
    twd = jnp.tanh(wd)
    pw = _mm(twd, w2_ref[...]) + w0_ref[...]
    lw = -jnp.exp(-(jnp.maximum(-pw, 0.0) + jnp.log(1.0 + jnp.exp(-jnp.abs(pw)))) - 0.5)
    a_sig = 1.0 / (1.0 + jnp.exp(-(_mm(ad, a2_ref[...]) + a0_ref[...])))
    sg = 1.0 / (1.0 + jnp.exp(-gd))
    kx = k * kk_ref[...]
    nrm = jnp.sqrt(_head_sums(kx * kx))
    inv = 1.0 / jnp.maximum(nrm, L2_FLOOR)
    return dict(z=z, r=r, k=k, v=v, twd=twd, pw=pw, lw=lw, a_sig=a_sig, sg=sg, ad=ad, kk=kx * inv, inv=inv, live=nrm > L2_FLOOR)


def _prep_specs(tokens, rpad, rw, w2, a2, g2):
    tr = PREP_ROWS
    tile = lambda w: pl.BlockSpec((tr, w), lambda i: (i, 0))
    before = pl.BlockSpec((8, rpad), lambda i: (jnp.maximum(i * (tr // 8) - 1, 0), 0))
    whole = lambda a: pl.BlockSpec(a.shape, lambda i: (0, 0))
    par = pl.BlockSpec((1, rw), lambda i: (0, 0))
    return tile, before, whole, par, pl.BlockSpec((1, rpad), lambda i: (0, 0))


def _prep_fwd_call(zr, mu, w0, a0, k_k, k_a, w2, a2, g2):
    tokens, rpad = zr.shape
    rw = w0.shape[1]
    segs, _ = _prep_segments(rw, w2.shape[0], a2.shape[0], g2.shape[0])
    tile, before, whole, par, mu_spec = _prep_specs(tokens, rpad, rw, w2, a2, g2)

    def body(z_ref, zlast_ref, mu_ref, w0_ref, a0_ref, kk_ref, ka_ref, w2_ref, a2_ref, g2_ref,
             r_ref, lw_ref, kf_ref, v_ref, na_ref, b_ref, g_ref):
        f = _prep_forward_values(z_ref, zlast_ref, mu_ref, w0_ref, a0_ref, kk_ref, ka_ref, w2_ref, a2_ref, g2_ref,
                                 segs, pl.program_id(0) == 0)
        r_ref[...] = f["r"]
        v_ref[...] = f["v"]
        lw_ref[...] = f["lw"]
        kf_ref[...] = f["k"] * (1.0 + (f["a_sig"] - 1.0) * ka_ref[...])
        na_ref[...] = -f["kk"]
        b_ref[...] = f["kk"] * f["a_sig"]
        g_ref[...] = _mm(f["sg"], g2_ref[...])

    shape = jax.ShapeDtypeStruct((tokens, rw), F32)
    return pl.pallas_call(
        body, name="rwkv_prep_fwd", grid=(tokens // PREP_ROWS,),
        in_specs=[tile(rpad), before, mu_spec, par, par, par, par, whole(w2), whole(a2), whole(g2)],
        out_specs=[tile(rw)] * 7, out_shape=[shape] * 7,
        compiler_params=pltpu.CompilerParams(dimension_semantics=("parallel",), vmem_limit_bytes=VMEM_LIMIT_CAP),
    )(zr, zr, mu, w0, a0, k_k, k_a, w2, a2, g2)


def _prep_bwd_call(zr, mu, w0, a0, k_k, k_a, w2, a2, g2, cts):
    tokens, rpad = zr.shape
    rw = w0.shape[1]
    segs, _ = _prep_segments(rw, w2.shape[0], a2.shape[0], g2.shape[0])
    tile, before, whole, par, mu_spec = _prep_specs(tokens, rpad, rw, w2, a2, g2)
    nt = tokens // PREP_ROWS
    rev = lambda spec: pl.BlockSpec(spec.block_shape, lambda i, f=spec.index_map: f(nt - 1 - i))

    def body(z_ref, zlast_ref, mu_ref, w0_ref, a0_ref, kk_ref, ka_ref, w2_ref, a2_ref, g2_ref,
             dr_ref, dlw_ref, dkf_ref, dv_ref, dna_ref, db_ref, dg_ref,
             dz_ref, dmu_ref, dw0_ref, da0_ref, dkk_ref, dka_ref, dw2_ref, da2_ref, dg2_ref, carry):
        step = pl.program_id(0)

        @pl.when(step == 0)
        def _():
            for ref in (dmu_ref, dw0_ref, da0_ref, dkk_ref, dka_ref, dw2_ref, da2_ref, dg2_ref, carry):
                ref[...] = jnp.zeros_like(ref)

        f = _prep_forward_values(z_ref, zlast_ref, mu_ref, w0_ref, a0_ref, kk_ref, ka_ref, w2_ref, a2_ref, g2_ref,
                                 segs, step == nt - 1)
        k, kk, a_sig, sg, twd = f["k"], f["kk"], f["a_sig"], f["sg"], f["twd"]
        colsum = lambda t: jnp.sum(t, axis=0, keepdims=True)
        dkf, db, dg = dkf_ref[...], db_ref[...], dg_ref[...]
        ka = ka_ref[...]
        dgd = _mm(dg, g2_ref[...], tb=True) * sg * (1.0 - sg)
        dg2_ref[...] += _mm(sg, dg, ta=True)
        dkk = db * a_sig - dna_ref[...]
        da_sig = db * kk + dkf * k * ka
        dk = dkf * (1.0 + (a_sig - 1.0) * ka)
        dka_ref[...] += colsum(dkf * k * (a_sig - 1.0))
        along = jnp.where(f["live"], _head_sums(dkk * kk), 0.0)
        dkx = (dkk - kk * along) * f["inv"]
        dk = dk + dkx * kk_ref[...]
        dkk_ref[...] += colsum(dkx * k)
        dpa = da_sig * a_sig * (1.0 - a_sig)
        da0_ref[...] += colsum(dpa)
        dad = _mm(dpa, a2_ref[...], tb=True)
        da2_ref[...] += _mm(f["ad"], dpa, ta=True)
        dpw = dlw_ref[...] * f["lw"] / (1.0 + jnp.exp(f["pw"]))
        dw0_ref[...] += colsum(dpw)
        dwd = _mm(dpw, w2_ref[...], tb=True) * (1.0 - twd * twd)
        dw2_ref[...] += _mm(twd, dpw, ta=True)
        rows = PREP_ROWS
        last = lax.broadcasted_iota(jnp.int32, (rows, 1), 0) == rows - 1
        for name, dz in (("r", dr_ref[...]), ("k", dk), ("v", dv_ref[...]), ("wd", dwd), ("ad", dad), ("gd", dgd)):
            lo, hi = segs[name]
            mu_s = mu_ref[:, lo:hi]
            dmu_ref[:, lo:hi] += colsum(dz * f["z"][name][1])
            later = dz * mu_s
            dz_ref[:, lo:hi] = dz * (1.0 - mu_s) + jnp.where(last, carry[:, lo:hi], pltpu.roll(later, rows - 1, axis=0))
            carry[:, lo:hi] = later[0:1, :]

    tok = jax.ShapeDtypeStruct((tokens, rw), F32)
    acc = lambda a: jax.ShapeDtypeStruct(a.shape, F32)
    return pl.pallas_call(
        body, name="rwkv_prep_bwd", grid=(nt,),
        in_specs=[rev(tile(rpad)), rev(before), mu_spec, par, par, par, par, whole(w2), whole(a2), whole(g2)]
                 + [rev(tile(rw))] * 7,
        out_specs=[rev(tile(rpad)), mu_spec, par, par, par, par, whole(w2), whole(a2), whole(g2)],
        out_shape=[jax.ShapeDtypeStruct((tokens, rpad), F32), acc(mu), acc(w0), acc(a0), acc(k_k), acc(k_a), acc(w2), acc(a2), acc(g2)],
        scratch_shapes=[pltpu.VMEM((1, rpad), F32)],
        compiler_params=pltpu.CompilerParams(dimension_semantics=("arbitrary",), vmem_limit_bytes=VMEM_LIMIT_CAP),
    )(zr, zr, mu, w0, a0, k_k, k_a, w2, a2, g2, *cts)


@jax.custom_vjp
def rwkv_prep(zr, mu, w0, a0, k_k, k_a, w2, a2, g2):
    return tuple(_prep_fwd_call(zr, mu, w0, a0, k_k, k_a, w2, a2, g2))


def _rwkv_prep_bwd(res, cts):
    zr, mu, w0, a0, k_k, k_a, w2, a2, g2 = res
    dz, dmu, dw0, da0, dkk, dka, dw2, da2, dg2 = _prep_bwd_call(*res, cts)
    return dz, dmu, dw0, da0, dkk, dka, dw2.astype(w2.dtype), da2.astype(a2.dtype), dg2.astype(g2.dtype)


rwkv_prep.defvjp(lambda *a: (tuple(_prep_fwd_call(*a)), a), _rwkv_prep_bwd)


def _pair_masks(rows):
    lane = lax.broadcasted_iota(jnp.int32, (rows, PAIR), 1)
    return lane < HEAD_DIM, lane >= HEAD_DIM


def _bd(x):
    m0, m1 = _pair_masks(x.shape[0])
    return jnp.concatenate([jnp.where(m0, x, 0.0), jnp.where(m1, x, 0.0)], axis=0)


def _unbd(m, c):
    return jnp.where(_pair_masks(c)[0], m[:c], m[c:])


def _pair_a(l2, r2):
    return _mm(l2, _bd(r2), tb=True)


def _pair_mul(p2, x2):
    return _mm(p2, _bd(x2))


def _pair_mul_t(p2, x2):
    return _unbd(_mm(p2, x2, ta=True), p2.shape[0])


def _block_diag_mask():
    row = lax.broadcasted_iota(jnp.int32, (PAIR, PAIR), 0)
    lane = lax.broadcasted_iota(jnp.int32, (PAIR, PAIR), 1)
    return (row < HEAD_DIM) == (lane < HEAD_DIM), row == lane


def _wkv_pair_common(r, lw, k, a, b):
    c = r[0].shape[0]
    pairs = range(len(r))
    i = lax.broadcasted_iota(jnp.int32, (c, PAIR), 0)
    j = lax.broadcasted_iota(jnp.int32, (c, PAIR), 1) % c
    strict, incl = i > j, i >= j
    ti = lax.broadcasted_iota(jnp.int32, (c, c), 0)
    tj = lax.broadcasted_iota(jnp.int32, (c, c), 1)
    tri = jnp.where(ti >= tj, 1.0, 0.0).astype(BF16)
    lc = [sum(_dg(tri, part, False, False) for part in _split(lw[p], 3)) for p in pairs]
    lend = [lc[p][c - 1:c, :] for p in pairs]
    rt = [r[p] * jnp.exp(lc[p]) for p in pairs]
    at = [a[p] * jnp.exp(lc[p] - lw[p]) for p in pairs]
    pinv = [jnp.exp(-lc[p]) for p in pairs]
    kt = [k[p] * pinv[p] for p in pairs]
    bt = [b[p] * pinv[p] for p in pairs]
    e = [jnp.exp(lend[p] - lc[p]) for p in pairs]
    ktp = [k[p] * e[p] for p in pairs]
    btp = [b[p] * e[p] for p in pairs]
    a_ab = [jnp.where(strict, _pair_a(at[p], bt[p]), 0.0) for p in pairs]
    a_ak = [jnp.where(strict, _pair_a(at[p], kt[p]), 0.0) for p in pairs]
    a_rb = [jnp.where(incl, _pair_a(rt[p], bt[p]), 0.0) for p in pairs]
    a_rk = [jnp.where(incl, _pair_a(rt[p], kt[p]), 0.0) for p in pairs]
    t = [jnp.where(i == j, 1.0, 0.0) + a_ab[p] for p in pairs]
    xp = a_ab
    n = 2
    while n < c:
        xp = [_pair_mul(xp[p], xp[p]) for p in pairs]
        t = [t[p] + _pair_mul(t[p], xp[p]) for p in pairs]
        n *= 2
    bdm, eye = _block_diag_mask()
    pend_col = [jnp.sum(jnp.where(eye, jnp.exp(lend[p]), 0.0), axis=1, keepdims=True) for p in pairs]
    return dict(rt=rt, at=at, kt=kt, bt=bt, ktp=ktp, btp=btp, a_ak=a_ak, a_rb=a_rb, a_rk=a_rk, t=t,
                pend_col=pend_col, lend=lend, lc=lc, strict=strict, incl=incl, tri=tri, bdm=bdm)


def _wkv_group(width):
    npair = width // PAIR
    g = min(WKV_PAIRS_PER_STEP, npair)
    assert npair % g == 0
    return npair, g


def _wkv_fwd_call(r, lw, k, v, a, b):
    tokens, width = r.shape
    c = WKV_CHUNK
    nc = tokens // c
    npair, g = _wkv_group(width)

    def body(r_ref, lw_ref, k_ref, v_ref, a_ref, b_ref, y_ref, s_ref, st):
        @pl.when(pl.program_id(1) == 0)
        def _():
            st[...] = jnp.zeros_like(st)

        pairs = range(g)
        rv, lwv, kv, vv, av, bv = ([ref[:, p * PAIR:(p + 1) * PAIR] for p in pairs]
                                   for ref in (r_ref, lw_ref, k_ref, v_ref, a_ref, b_ref))
        s0 = [st[p] for p in pairs]
        q = _wkv_pair_common(rv, lwv, kv, av, bv)
        w1 = [_mm(q["at"][p], s0[p]) + _pair_mul(q["a_ak"][p], vv[p]) for p in pairs]
        u = [_pair_mul(q["t"][p], w1[p]) for p in pairs]
        y = [_mm(q["rt"][p], s0[p]) + _pair_mul(q["a_rb"][p], u[p]) + _pair_mul(q["a_rk"][p], vv[p]) for p in pairs]
        grow = [_mm(jnp.concatenate([q["btp"][p], q["ktp"][p]], axis=0), jnp.concatenate([u[p], vv[p]], axis=0), ta=True)
                for p in pairs]
        for p in pairs:
            y_ref[:, p * PAIR:(p + 1) * PAIR] = y[p]
            s_ref[0, p] = s0[p]
            st[p] = q["pend_col"][p] * s0[p] + jnp.where(q["bdm"], grow[p], 0.0)

    tok = pl.BlockSpec((c, g * PAIR), lambda gi, ci: (ci, gi))
    return pl.pallas_call(
        body, name="wkv_fwd", grid=(npair // g, nc),
        in_specs=[tok] * 6,
        out_specs=[tok, pl.BlockSpec((1, g, PAIR, PAIR), lambda gi, ci: (ci, gi, 0, 0))],
        out_shape=[jax.ShapeDtypeStruct((tokens, width), F32), jax.ShapeDtypeStruct((nc, npair, PAIR, PAIR), F32)],
        scratch_shapes=[pltpu.VMEM((g, PAIR, PAIR), F32)],
        compiler_params=pltpu.CompilerParams(dimension_semantics=("parallel", "arbitrary")),
    )(r, lw, k, v, a, b)


def _wkv_bwd_call(r, lw, k, v, a, b, s, dy):
    tokens, width = r.shape
    c = WKV_CHUNK
    nc = tokens // c
    npair, g = _wkv_group(width)

    def body(r_ref, lw_ref, k_ref, v_ref, a_ref, b_ref, s_ref, dy_ref,
             dr_ref, dlw_ref, dk_ref, dv_ref, da_ref, db_ref, dst):
        @pl.when(pl.program_id(1) == 0)
        def _():
            dst[...] = jnp.zeros_like(dst)

        pairs = range(g)
        rv, lwv, kv, vv, av, bv, dyv = ([ref[:, p * PAIR:(p + 1) * PAIR] for p in pairs]
                                        for ref in (r_ref, lw_ref, k_ref, v_ref, a_ref, b_ref, dy_ref))
        s0 = [s_ref[0, p] for p in pairs]
        dsc = [dst[p] for p in pairs]
        q = _wkv_pair_common(rv, lwv, kv, av, bv)
        rt, at, kt, bt, ktp, btp, t = (q[n] for n in ("rt", "at", "kt", "bt", "ktp", "btp", "t"))
        a_ak, a_rb, a_rk, strict, incl = (q[n] for n in ("a_ak", "a_rb", "a_rk", "strict", "incl"))
        w1 = [_mm(at[p], s0[p]) + _pair_mul(a_ak[p], vv[p]) for p in pairs]
        u = [_pair_mul(t[p], w1[p]) for p in pairs]
        du = [_pair_mul_t(a_rb[p], dyv[p]) + _mm(btp[p], dsc[p]) for p in pairs]
        dw1 = [_pair_mul_t(t[p], du[p]) for p in pairs]
        dv = [_pair_mul_t(a_rk[p], dyv[p]) + _mm(ktp[p], dsc[p]) + _pair_mul_t(a_ak[p], dw1[p]) for p in pairs]
        da_ab = [jnp.where(strict, _pair_a(dw1[p], u[p]), 0.0) for p in pairs]
        da_ak = [jnp.where(strict, _pair_a(dw1[p], vv[p]), 0.0) for p in pairs]
        da_rb = [jnp.where(incl, _pair_a(dyv[p], u[p]), 0.0) for p in pairs]
        da_rk = [jnp.where(incl, _pair_a(dyv[p], vv[p]), 0.0) for p in pairs]
        d_rt = [_mm(dyv[p], s0[p], tb=True) + _pair_mul(da_rb[p], bt[p]) + _pair_mul(da_rk[p], kt[p]) for p in pairs]
        d_at = [_mm(dw1[p], s0[p], tb=True) + _pair_mul(da_ab[p], bt[p]) + _pair_mul(da_ak[p], kt[p]) for p in pairs]
        d_bt = [_pair_mul_t(da_ab[p], at[p]) + _pair_mul_t(da_rb[p], rt[p]) for p in pairs]
        d_kt = [_pair_mul_t(da_ak[p], at[p]) + _pair_mul_t(da_rk[p], rt[p]) for p in pairs]
        d_btp = [_mm(u[p], dsc[p], tb=True) for p in pairs]
        d_ktp = [_mm(vv[p], dsc[p], tb=True) for p in pairs]
        ones = jnp.ones((8, PAIR), BF16)
        dpend = [sum(_dg(ones, part, False, True) for part in _split(dsc[p] * s0[p], 3))[0:1, :] * jnp.exp(q["lend"][p])
                 for p in pairs]
        grow = [_mm(jnp.concatenate([rt[p], at[p]], axis=0), jnp.concatenate([dyv[p], dw1[p]], axis=0), ta=True)
                for p in pairs]
        last = lax.broadcasted_iota(jnp.int32, (c, PAIR), 0) == c - 1
        for p in pairs:
            sl = slice(p * PAIR, (p + 1) * PAIR)
            dst[p] = q["pend_col"][p] * dsc[p] + jnp.where(q["bdm"], grow[p], 0.0)
            lc_e = d_ktp[p] * ktp[p] + d_btp[p] * btp[p]
            dlend = jnp.sum(lc_e, axis=0, keepdims=True) + dpend[p]
            dlc = d_rt[p] * rt[p] - d_kt[p] * kt[p] - d_bt[p] * bt[p] - lc_e + jnp.where(last, dlend, 0.0)
            dlp = d_at[p] * at[p]
            dlw_ref[:, sl] = sum(_dg(q["tri"], part, True, False) for part in _split(dlc + dlp, 3)) - dlp
            lc = q["lc"][p]
            pinv = jnp.exp(-lc)
            e = jnp.exp(q["lend"][p] - lc)
            dr_ref[:, sl] = d_rt[p] * jnp.exp(lc)
            da_ref[:, sl] = d_at[p] * jnp.exp(lc - lwv[p])
            dk_ref[:, sl] = d_kt[p] * pinv + d_ktp[p] * e
            db_ref[:, sl] = d_bt[p] * pinv + d_btp[p] * e
            dv_ref[:, sl] = dv[p]

    tok = pl.BlockSpec((c, g * PAIR), lambda gi, ci: (nc - 1 - ci, gi))
    tshape = jax.ShapeDtypeStruct((tokens, width), F32)
    return pl.pallas_call(
        body, name="wkv_bwd", grid=(npair // g, nc),
        in_specs=[tok] * 6 + [pl.BlockSpec((1, g, PAIR, PAIR), lambda gi, ci: (nc - 1 - ci, gi, 0, 0)), tok],
        out_specs=[tok] * 6, out_shape=[tshape] * 6,
        scratch_shapes=[pltpu.VMEM((g, PAIR, PAIR), F32)],
        compiler_params=pltpu.CompilerParams(dimension_semantics=("parallel", "arbitrary")),
    )(r, lw, k, v, a, b, s, dy)


@jax.custom_vjp
def wkv7(r, lw, k, v, a, b):
    return _wkv_fwd_call(r, lw, k, v, a, b)[0]


def _wkv7_fwd(r, lw, k, v, a, b):
    y, s = _wkv_fwd_call(r, lw, k, v, a, b)
    return y, (r, lw, k, v, a, b, s)


wkv7.defvjp(_wkv7_fwd, lambda res, dy: tuple(_wkv_bwd_call(*res, dy)))


def _attn_block(tokens):
    return ATTN_BLOCK_BIG if tokens % ATTN_BLOCK_BIG == 0 else ATTN_BLOCK


def _fox_layouts(cum):
    tokens, heads = cum.shape
    t = _attn_block(tokens)
    cq = cum.reshape(tokens, heads // 2, 2).transpose(1, 0, 2)
    ck = cum.T.reshape(heads // 2, 2, tokens // t, t).transpose(0, 2, 1, 3)
    return cq, ck


def _head_lane_masks(rows):
    lane = lax.broadcasted_iota(jnp.int32, (rows, 2 * HEAD_DIM), 1)
    return [lane < HEAD_DIM, lane >= HEAD_DIM]


def _fox_fwd_call(q, k, v, cq, ck):
    tokens, width = q.shape
    t = _attn_block(tokens)
    nb = tokens // t
    hd = HEAD_DIM
    npair = width // (2 * hd)

    def body(q_ref, k_ref, v_ref, cq_ref, ck_ref, o_ref, lse_ref):
        i = pl.program_id(1)
        masks = _head_lane_masks(t)
        q2 = q_ref[...]
        qs = [jnp.where(mk, q2, 0.0).astype(BF16) for mk in masks]
        cqs = [cq_ref[0, :, hh:hh + 1] for hh in range(2)]

        def block(j, carry, diagonal):
            off = pl.multiple_of(j * t, t)
            ckj = ck_ref[0, j]
            k2 = k_ref[pl.ds(off, t), :].astype(BF16)
            v2 = v_ref[pl.ds(off, t), :].astype(BF16)
            out = []
            for hh in range(2):
                m, l, acc = carry[hh]
                s = _dg(qs[hh], k2, False, True) + (cqs[hh] - ckj[hh:hh + 1, :])
                if diagonal:
                    keep = lax.broadcasted_iota(jnp.int32, (t, t), 0) >= lax.broadcasted_iota(jnp.int32, (t, t), 1)
                    s = jnp.where(keep, s, NEG_BIG)
                m_new = jnp.maximum(m, jnp.max(s, axis=1, keepdims=True))
                alpha = jnp.exp(m - m_new)
                p = jnp.exp(s - m_new)
                l = alpha * l + jnp.sum(p, axis=1, keepdims=True)
                acc = alpha * acc + _dg(p.astype(BF16), v2, False, False)
                out.append((m_new, l, acc))
            return tuple(out)

        init = tuple((jnp.full((t, 1), NEG_BIG, F32), jnp.zeros((t, 1), F32), jnp.zeros((t, 2 * hd), F32)) for _ in range(2))
        res = lax.fori_loop(0, i, lambda j, c: block(j, c, False), init)
        res = block(i, res, True)
        o_ref[...] = jnp.where(masks[0], res[0][2] / res[0][1], res[1][2] / res[1][1])
        for hh in range(2):
            lse_ref[0, :, hh:hh + 1] = res[hh][0] + jnp.log(res[hh][1])

    blk = pl.BlockSpec((t, 2 * hd), lambda hp, i: (i, hp))
    full = pl.BlockSpec((tokens, 2 * hd), lambda hp, i: (0, hp))
    cq_spec = pl.BlockSpec((1, t, 2), lambda hp, i: (hp, i, 0))
    ck_spec = pl.BlockSpec((1, nb, 2, t), lambda hp, i: (hp, 0, 0, 0))
    return pl.pallas_call(
        body, name="fox_fwd", grid=(npair, nb),
        in_specs=[blk, full, full, cq_spec, ck_spec],
        out_specs=[blk, cq_spec],
        out_shape=[jax.ShapeDtypeStruct((tokens, width), F32), jax.ShapeDtypeStruct((npair, tokens, 2), F32)],
        compiler_params=pltpu.CompilerParams(dimension_semantics=("parallel", "arbitrary")),
    )(q, k, v, cq, ck)


def _fox_bwd_call(q, k, v, cq, ck, o, lse, do):
    tokens, width = q.shape
    t = _attn_block(tokens)
    nb = tokens // t
    hd = HEAD_DIM
    npair = width // (2 * hd)

    def body(q_ref, k_ref, v_ref, cq_ref, ck_ref, o_ref, lse_ref, do_ref, dq_ref, dk_ref, dv_ref, dck_ref, dcq_ref):
        i = pl.program_id(1)

        @pl.when(i == 0)
        def _():
            dk_ref[...] = jnp.zeros_like(dk_ref)
            dv_ref[...] = jnp.zeros_like(dv_ref)
            dck_ref[...] = jnp.zeros_like(dck_ref)

        masks = _head_lane_masks(t)
        q2, do2, o2 = q_ref[...], do_ref[...], o_ref[...]
        qs = [jnp.where(mk, q2, 0.0).astype(BF16) for mk in masks]
        dos = [jnp.where(mk, do2, 0.0).astype(BF16) for mk in masks]
        deltas = [jnp.sum(dos[hh].astype(F32) * o2, axis=1, keepdims=True) for hh in range(2)]
        bias = [cq_ref[0, :, hh:hh + 1] - lse_ref[0, :, hh:hh + 1] for hh in range(2)]

        def block(j, carry, diagonal):
            off = pl.multiple_of(j * t, t)
            ckj = ck_ref[0, j]
            k2 = k_ref[pl.ds(off, t), :].astype(BF16)
            v2 = v_ref[pl.ds(off, t), :].astype(BF16)
            out = []
            dk2 = jnp.zeros((t, 2 * hd), F32)
            dv2 = jnp.zeros((t, 2 * hd), F32)
            for hh in range(2):
                s = _dg(qs[hh], k2, False, True) + (bias[hh] - ckj[hh:hh + 1, :])
                if diagonal:
                    keep = lax.broadcasted_iota(jnp.int32, (t, t), 0) >= lax.broadcasted_iota(jnp.int32, (t, t), 1)
                    s = jnp.where(keep, s, NEG_BIG)
                p = jnp.exp(s)
                dp = _dg(dos[hh], v2, False, True)
                ds = p * (dp - deltas[hh])
                dsb = ds.astype(BF16)
                dq, rowsum = carry[hh]
                out.append((dq + _dg(dsb, k2, False, False), rowsum + jnp.sum(ds, axis=1, keepdims=True)))
                dk2 = dk2 + _dg(dsb, qs[hh], True, False)
                dv2 = dv2 + _dg(p.astype(BF16), dos[hh], True, False)
                dck_ref[0, j, hh:hh + 1, :] -= jnp.sum(ds, axis=0, keepdims=True)
            dk_ref[pl.ds(off, t), :] += dk2
            dv_ref[pl.ds(off, t), :] += dv2
            return tuple(out)

        init = tuple((jnp.zeros((t, 2 * hd), F32), jnp.zeros((t, 1), F32)) for _ in range(2))
        res = lax.fori_loop(0, i, lambda j, c: block(j, c, False), init)
        res = block(i, res, True)
        dq_ref[...] = jnp.where(masks[0], res[0][0], res[1][0])
        for hh in range(2):
            dcq_ref[0, :, hh:hh + 1] = res[hh][1]

    blk = pl.BlockSpec((t, 2 * hd), lambda hp, i: (i, hp))
    full = pl.BlockSpec((tokens, 2 * hd), lambda hp, i: (0, hp))
    cq_spec = pl.BlockSpec((1, t, 2), lambda hp, i: (hp, i, 0))
    ck_spec = pl.BlockSpec((1, nb, 2, t), lambda hp, i: (hp, 0, 0, 0))
    tshape = jax.ShapeDtypeStruct((tokens, width), F32)
    return pl.pallas_call(
        body, name="fox_bwd", grid=(npair, nb),
        in_specs=[blk, full, full, cq_spec, ck_spec, blk, cq_spec, blk],
        out_specs=[blk, full, full, ck_spec, cq_spec],
        out_shape=[tshape, tshape, tshape, jax.ShapeDtypeStruct((npair, nb, 2, t), F32),
                   jax.ShapeDtypeStruct((npair, tokens, 2), F32)],
        compiler_params=pltpu.CompilerParams(dimension_semantics=("parallel", "arbitrary")),
    )(q, k, v, cq, ck, o, lse, do)


@jax.custom_vjp
def fox_attention(q, k, v, cum):
    return _fox_fwd_call(q, k, v, *_fox_layouts(cum))[0]


def _fox_fwd(q, k, v, cum):
    cq, ck = _fox_layouts(cum)
    o, lse = _fox_fwd_call(q, k, v, cq, ck)
    return o, (q, k, v, cq, ck, o, lse)


def _fox_bwd(res, do):
    q, k, v, cq, ck, o, lse = res
    dq, dk, dv, dck, dcq = _fox_bwd_call(q, k, v, cq, ck, o, lse, do)
    npair, nb, _, t = dck.shape
    dcum = dck.transpose(0, 2, 1, 3).reshape(2 * npair, nb * t).T + dcq.transpose(1, 0, 2).reshape(nb * t, 2 * npair)
    return dq, dk, dv, dcum


fox_attention.defvjp(_fox_fwd, _fox_bwd)


def _loss_call(y, target):
    rows, d = y.shape
    tr = _row_tile(rows, d)

    def body(y_ref, t_ref, loss_ref, dy_ref):
        @pl.when(pl.program_id(0) == 0)
        def _():
            loss_ref[...] = jnp.zeros_like(loss_ref)

        diff = y_ref[...] - t_ref[...]
        dy_ref[...] = diff * (1.0 / d)
        loss_ref[...] += (0.5 / d) * jnp.sum(jnp.sum(diff * diff, axis=1, keepdims=True), axis=0, keepdims=True)

    return pl.pallas_call(
        body, name="loss", grid=(rows // tr,),
        in_specs=[pl.BlockSpec((tr, d), lambda i: (i, 0))] * 2,
        out_specs=[pl.BlockSpec((1, 1), lambda i: (0, 0)), pl.BlockSpec((tr, d), lambda i: (i, 0))],
        out_shape=[jax.ShapeDtypeStruct((1, 1), F32), jax.ShapeDtypeStruct((rows, d), F32)],
        compiler_params=pltpu.CompilerParams(dimension_semantics=("arbitrary",)),
    )(y, target)


def _adamw_call(w, g, m, v):
    rows, cols = w.shape
    tr = _row_tile_ragged(rows, cols, budget=1024 * 1024)
    c1 = 1.0 / (1.0 - ADAM_B1 ** ADAM_STEP)
    c2 = 1.0 / (1.0 - ADAM_B2 ** ADAM_STEP)

    def body(w_ref, g_ref, m_ref, v_ref, d_ref, nm_ref, nv_ref):
        gv = g_ref[...]
        nm = ADAM_B1 * m_ref[...] + (1.0 - ADAM_B1) * gv
        nv = ADAM_B2 * v_ref[...] + (1.0 - ADAM_B2) * (gv * gv)
        nm_ref[...] = nm
        nv_ref[...] = nv
        d_ref[...] = -ADAM_LR * ((nm * c1) / (jnp.sqrt(nv * c2) + ADAM_EPS) + ADAM_WD * w_ref[...])

    spec = pl.BlockSpec((tr, cols), lambda i: (i, 0))
    shape = jax.ShapeDtypeStruct((rows, cols), F32)
    return pl.pallas_call(
        body, name="adamw", grid=(pl.cdiv(rows, tr),),
        in_specs=[spec] * 4, out_specs=[spec] * 3, out_shape=[shape] * 3,
        compiler_params=pltpu.CompilerParams(dimension_semantics=("parallel",)),
    )(w, g, m, v)


def _my_place():
    return lax.axis_index("x"), lax.axis_index("y"), lax.axis_index("c")


def _place_index(px, py, pc):
    return 4 * px + 2 * py + pc


HBM_SPEC = pl.BlockSpec(memory_space=pltpu.HBM)


def _all_gather_call(block):
    def body(x_ref, out_ref, send_sems, recv_sems, local_sem):
        x, y, c = _my_place()
        me, sibling = (x, y, c), (x, y, 1 - c)
        chips = [(1 - x, y), (x, 1 - y), (1 - x, 1 - y)]

        def slot(px, py, pc):
            return out_ref.at[_place_index(px, py, pc)]

        def copy(k, blk, to, src=None):
            return pltpu.make_async_remote_copy(
                src_ref=slot(*blk) if src is None else src, dst_ref=slot(*blk),
                send_sem=send_sems.at[k], recv_sem=recv_sems.at[k],
                device_id=to, device_id_type=pl.DeviceIdType.MESH)

        mine = pltpu.make_async_copy(x_ref, slot(*me), local_sem)
        mine.start()
        first = [copy(0, me, sibling, src=x_ref)]
        first += [copy(1 + j, me, (*chip, c), src=x_ref) for j, chip in enumerate(chips)]
        for cp in first:
            cp.start()
        passed = [copy(4 + j, (*chip, c), sibling) for j, chip in enumerate(chips)]
        for j, chip in enumerate(chips):
            copy(1 + j, (*chip, c), me).wait_recv()
            passed[j].start()
        copy(0, sibling, me).wait_recv()
        for j, chip in enumerate(chips):
            copy(4 + j, (*chip, 1 - c), me).wait_recv()
        for cp in first + passed:
            cp.wait_send()
        mine.wait()

    return pl.pallas_call(
        body, name="all_gather",
        out_shape=jax.ShapeDtypeStruct((N_DEV,) + block.shape, block.dtype),
        in_specs=[HBM_SPEC], out_specs=HBM_SPEC,
        scratch_shapes=[pltpu.SemaphoreType.DMA((7,)), pltpu.SemaphoreType.DMA((7,)), pltpu.SemaphoreType.DMA],
    )(block)


SEM_SPEC = pl.BlockSpec(memory_space=pltpu.SEMAPHORE)
SIDE_EFFECT = pltpu.SideEffectType.DATAFLOW_SIDE_EFFECTING


def _peers():
    x, y, c = _my_place()
    out = []
    for k in range(1, N_DEV):
        peer = (x ^ (k >> 2), y ^ ((k >> 1) & 1), c ^ (k & 1))
        out.append((k - 1, peer, _place_index(*peer)))
    return _place_index(x, y, c), out


def _spread_start(src, per_peer, name, after=None):
    slot = src.shape[1:] if per_peer else src.shape
    order = () if after is None else (after,)

    def body(src_ref, land_ref, *rest):
        send_sems, recv_sems, src_thru, land_thru, token = rest[len(order):]
        mine, peers = _peers()
        for k, peer, peer_idx in peers:
            pltpu.make_async_remote_copy(
                src_ref=src_ref.at[peer_idx] if per_peer else src_ref, dst_ref=land_ref.at[mine],
                send_sem=send_sems.at[k], recv_sem=recv_sems.at[k],
                device_id=peer, device_id_type=pl.DeviceIdType.MESH).start()
        token[...] = jnp.zeros_like(token)

    return pl.pallas_call(
        body, name=name,
        out_shape=(pltpu.SemaphoreType.DMA((N_DEV - 1,)), pltpu.SemaphoreType.DMA((N_DEV - 1,)),
                   pltpu.HBM(src.shape, src.dtype), pltpu.HBM((N_DEV,) + slot, src.dtype),
                   jax.ShapeDtypeStruct((8, 128), F32)),
        in_specs=(HBM_SPEC, HBM_SPEC) + (pl.BlockSpec(memory_space=pl.ANY),) * len(order),
        out_specs=(SEM_SPEC, SEM_SPEC, HBM_SPEC, HBM_SPEC, pl.BlockSpec(memory_space=pltpu.VMEM)),
        input_output_aliases={0: 2, 1: 3},
        compiler_params=pltpu.CompilerParams(has_side_effects=SIDE_EFFECT),
    )(pltpu.with_memory_space_constraint(src, pltpu.HBM),
      pltpu.with_memory_space_constraint(lax.empty((N_DEV,) + slot, src.dtype), pltpu.HBM), *order)


def _spread_wait(handles, after, per_peer, name):
    send_sems, recv_sems, src_thru, land_thru = handles

    def body(src_ref, land_ref, send_sems, recv_sems, after_ref, src_dead, got_ref):
        _, peers = _peers()
        for k, peer, peer_idx in peers:
            copy = pltpu.make_async_remote_copy(
                src_ref=src_ref.at[peer_idx] if per_peer else src_ref, dst_ref=land_ref.at[peer_idx],
                send_sem=send_sems.at[k], recv_sem=recv_sems.at[k],
                device_id=peer, device_id_type=pl.DeviceIdType.MESH)
            copy.wait_send()
            copy.wait_recv()

    return pl.pallas_call(
        body, name=name,
        out_shape=(pltpu.HBM(src_thru.shape, src_thru.dtype), pltpu.HBM(land_thru.shape, land_thru.dtype)),
        in_specs=(HBM_SPEC, HBM_SPEC, SEM_SPEC, SEM_SPEC, pl.BlockSpec(memory_space=pl.ANY)),
        out_specs=(HBM_SPEC, HBM_SPEC), input_output_aliases={0: 0, 1: 1},
        compiler_params=pltpu.CompilerParams(has_side_effects=SIDE_EFFECT),
    )(src_thru, land_thru, send_sems, recv_sems, after)


def _sum_slots_call(slots):
    _, rows, cols = slots.shape
    tr = _row_tile_ragged(rows, cols, budget=512 * 1024)

    def body(s_ref, o_ref):
        acc = s_ref[0].astype(F32)
        for j in range(1, N_DEV):
            acc = acc + s_ref[j].astype(F32)
        o_ref[...] = acc

    return pl.pallas_call(
        body, name="sum_slots", grid=(pl.cdiv(rows, tr),),
        in_specs=[pl.BlockSpec((N_DEV, tr, cols), lambda i: (0, i, 0))],
        out_specs=pl.BlockSpec((tr, cols), lambda i: (i, 0)),
        out_shape=jax.ShapeDtypeStruct((rows, cols), F32),
        compiler_params=pltpu.CompilerParams(dimension_semantics=("parallel",)),
    )(slots)


def _with_own_slot(got, own, mine):
    return lax.dynamic_update_index_in_dim(got, own, mine, 0)


def _pack(vectors, width):
    flat = jnp.concatenate([v.reshape(-1) for v in vectors])
    return jnp.pad(flat, (0, width - flat.shape[0])).reshape(width // 128, 128)


def _unpack(packed, like):
    flat = packed.reshape(-1)
    out, at = [], 0
    for v in like:
        out.append(flat[at:at + v.size].reshape(v.shape))
        at += v.size
    return tuple(out)


def _sum_over_devices(grads):
    n = sum(v.size for v in grads)
    width = -(-n // 1024) * 1024
    return _unpack(_sum_slots_call(_all_gather_call(_pack(grads, width))), grads)


def _cols_from_slots(slots):
    n, rows, cols = slots.shape
    return slots.transpose(1, 0, 2).reshape(rows, n * cols)


def _rows_from_slots(slots):
    return slots.reshape(-1, slots.shape[2])


def _pad128(n):
    return -(-n // 128) * 128


def _pad_to_tiles(a, axis):
    n = a.shape[axis]
    pads = [(0, 0)] * a.ndim
    pads[axis] = (0, _pad128(n) - n)
    return jnp.pad(a, pads)


def _rwkv_group(take, zeros, rw, dl, al, gl):
    at = 3 * rw
    parts = take(0, at)
    for n in (dl, al, gl):
        parts += take(at, at + n)
        if _pad128(n) > n:
            parts.append(zeros(_pad128(n) - n))
        at += n
    return parts


def _in_proj_layout(slots, rw, fw, dl, al, gl, whole):
    n_slots, rows, d = slots.shape
    wt = slots.reshape(n_slots * rows, d)
    take = lambda lo, hi: [wt[lo:hi]]
    zeros = lambda n: jnp.zeros((n, d), wt.dtype)
    rcols = 3 * rw + dl + al + gl
    fcols = 3 * fw + fw // HEAD_DIM
    group_r = _rwkv_group(take, zeros, rw, dl, al, gl)
    group_f = take(rcols, rcols + fcols) + ([zeros(_pad128(fcols) - fcols)] if _pad128(fcols) > fcols else [])
    group_g = take(rcols + fcols, n_slots * rows)
    if whole:
        return jnp.concatenate(group_r + group_f + group_g, axis=0)
    return tuple(jnp.concatenate(g, axis=0) for g in (group_r, group_f, group_g))


def _low_rank_layout(slots):
    return _pad_to_tiles(_cols_from_slots(slots), 0)


def _stage_embed(meta, x, n1, lp):
    h0 = jnp.concatenate([meta, x, jnp.zeros((lp - meta.shape[0] - x.shape[0], x.shape[1]), F32)], axis=0)
    return h0, rmsnorm(h0, n1)


def _stage_mix(z_r, z_f, small, w2, a2, g2, dims):
    (mu, w0, a0, k_k, k_a, r_k, gn_w, gn_b, q_g, k_g, f_bias) = small
    rw, fw, dl, al, gl = dims
    fcols = 3 * fw + fw // HEAD_DIM

    mu_group = jnp.concatenate(_rwkv_group(lambda lo, hi: [mu[:, lo:hi]], lambda n: jnp.zeros((1, n), F32), rw, dl, al, gl), axis=1)
    r, lw, kf, v, na, b, g = rwkv_prep(z_r, mu_group, w0, a0, k_k, k_a, w2, a2, g2)
    y = wkv7(r, lw, kf, v, na, b)
    y_a = gn_bonus(y, r, kf, v, gn_w, gn_b, r_k.reshape(1, rw)) * g

    fq, fk, fv, fl = z_f[:, :fw], z_f[:, fw:2 * fw], z_f[:, 2 * fw:3 * fw], z_f[:, 3 * fw:fcols]
    fq = head_rms(fq, jnp.tile(q_g, (1, fw // HEAD_DIM))) * (HEAD_DIM ** -0.5)
    fk = head_rms(fk, jnp.tile(k_g, (1, fw // HEAD_DIM)))
    cum = jnp.cumsum(jax.nn.log_sigmoid(badd(fl, f_bias)), axis=0)
    y_b = fox_attention(fq, fk, fv, cum)
    return y_a, y_b


def _stage_merge(h0, y_a, y_b, z_g, w_a, w_b, w_o):
    merged = gated_merge(z_g, dense_cols_bf16(y_a, w_a), dense_cols_bf16(y_b, w_b))
    return h0 + dense(merged, w_o)


def _stage_ffn(h1, n2, w_gu, w_dn):
    return h1 + dense(swiglu(dense_cols_bf16(rmsnorm(h1, n2), w_gu)), w_dn)


SHARDED = ("meta_tokens", "w_in", "rwkv_w2", "rwkv_a2", "rwkv_g2", "w_branch_a", "w_branch_b", "w_o", "w_gate_up", "w_down")
SMALL = ("norm1_g", "rwkv_mu", "rwkv_w0", "rwkv_a0", "rwkv_k_k", "rwkv_k_a", "rwkv_r_k", "rwkv_gn_w", "rwkv_gn_b",
         "fox_q_norm_g", "fox_k_norm_g", "fox_f_bias", "norm2_g")
WEIGHTS = ("meta_tokens", "norm1_g", "w_in", "rwkv_mu", "rwkv_w0", "rwkv_w2", "rwkv_a0", "rwkv_a2", "rwkv_g2", "rwkv_k_k",
           "rwkv_k_a", "rwkv_r_k", "rwkv_gn_w", "rwkv_gn_b", "fox_q_norm_g", "fox_k_norm_g", "fox_f_bias", "w_branch_a",
           "w_branch_b", "w_o", "norm2_g", "w_gate_up", "w_down")


def _as2d(a):
    return a.reshape(-1, a.shape[-1])


def kernel(x, meta_tokens, norm1_g, w_in, rwkv_mu, rwkv_w0, rwkv_w2, rwkv_a0, rwkv_a2, rwkv_g2, rwkv_k_k, rwkv_k_a, rwkv_r_k, rwkv_gn_w, rwkv_gn_b, fox_q_norm_g, fox_k_norm_g, fox_f_bias, w_branch_a, w_branch_b, w_o, norm2_g, w_gate_up, w_down, loss_target, m_meta_tokens, m_norm1_g, m_w_in, m_rwkv_mu, m_rwkv_w0, m_rwkv_w2, m_rwkv_a0, m_rwkv_a2, m_rwkv_g2, m_rwkv_k_k, m_rwkv_k_a, m_rwkv_r_k, m_rwkv_gn_w, m_rwkv_gn_b, m_fox_q_norm_g, m_fox_k_norm_g, m_fox_f_bias, m_w_branch_a, m_w_branch_b, m_w_o, m_norm2_g, m_w_gate_up, m_w_down, v_meta_tokens, v_norm1_g, v_w_in, v_rwkv_mu, v_rwkv_w0, v_rwkv_w2, v_rwkv_a0, v_rwkv_a2, v_rwkv_g2, v_rwkv_k_k, v_rwkv_k_a, v_rwkv_r_k, v_rwkv_gn_w, v_rwkv_gn_b, v_fox_q_norm_g, v_fox_k_norm_g, v_fox_f_bias, v_w_branch_a, v_w_branch_b, v_w_o, v_norm2_g, v_w_gate_up, v_w_down):
    given = dict(locals())
    w = {n: given[n] for n in WEIGHTS}
    assert rwkv_r_k.shape[-1] == HEAD_DIM
    n_meta, seq = meta_tokens.shape[0], x.shape[1]
    tokens = n_meta + seq
    lp = -(-tokens // TOKEN_TILE) * TOKEN_TILE
    mine = _place_index(*(lax.axis_index(a) for a in MESH_AXES))
    x2 = x[0]

    local = {n: _as2d(given[n]) for n in given if n != "x" and n != "loss_target"}
    for n in ("w_in", "m_w_in", "v_w_in"):
        local[n] = jnp.transpose(given[n][0])
    blocks = {n: local[n].astype(F32 if n == "meta_tokens" else BF16) for n in SHARDED}
    first = ("meta_tokens", "rwkv_w2", "rwkv_a2", "rwkv_g2")
    started = {n: _spread_start(blocks[n], False, "gather_start_" + n) for n in first}
    zero = sum(started[n][4][0, 0] for n in first)

    def gathered(n, after):
        own, got = _spread_wait(started[n][:4], after, False, "gather_wait_" + n)
        return _with_own_slot(got, own, mine)

    sm = {n: _as2d(w[n]) for n in SMALL}
    small_mix = tuple(sm[n] for n in SMALL[1:-1])
    n1 = sm["norm1_g"] + zero
    rw, fw = w_branch_a.shape[-2], w_branch_b.shape[-2]
    dims = (rw, fw, rwkv_w2.shape[-2], rwkv_a2.shape[-2], rwkv_g2.shape[-2])
    same = lambda s: (s,)

    meta, un_meta = jax.vjp(_cols_from_slots, gathered("meta_tokens", x2))
    (h0, xn), vjp_embed = jax.vjp(lambda m, xs, g: _stage_embed(m, xs, g, lp), meta, x2, n1)
    in_slots = _all_gather_call(blocks["w_in"])
    later = [n for n in SHARDED if n not in first and n != "w_in"]
    started.update({n: _spread_start(blocks[n], False, "gather_start_" + n, after=in_slots) for n in later})
    w_groups = _in_proj_layout(in_slots, *dims, whole=False)
    w_cat, un_in = jax.vjp(lambda s: _in_proj_layout(s, *dims, whole=True), in_slots)
    xn_b = xn.astype(BF16)
    behind = sum(started[n][4] for n in later)
    z_r, z_f, z_g = (_matmul(xn_b, wg, tb=True, name="in_proj_" + tag, after=behind) for wg, tag in zip(w_groups, "rfg"))
    (w2, un_w2), (a2, un_a2), (g2, un_g2) = (jax.vjp(_low_rank_layout, gathered(n, xn)) for n in ("rwkv_w2", "rwkv_a2", "rwkv_g2"))
    (y_a, y_b), vjp_mix = jax.vjp(lambda zr, zf, s, a, b, c: _stage_mix(zr, zf, s, a, b, c, dims),
                                  z_r, z_f, small_mix, w2, a2, g2)
    w_a, w_b = gathered("w_branch_a", y_a), gathered("w_branch_b", y_a)
    w_o_full, un_wo = jax.vjp(_rows_from_slots, gathered("w_o", y_a))
    h1, vjp_merge = jax.vjp(_stage_merge, h0, y_a, y_b, z_g, w_a, w_b, w_o_full)
    w_gu = gathered("w_gate_up", h1)
    w_dn, un_dn = jax.vjp(_rows_from_slots, gathered("w_down", h1))
    y, vjp_ffn = jax.vjp(_stage_ffn, h1, sm["norm2_g"], w_gu, w_dn)

    loss_part, dy_real = _loss_call(y[n_meta:tokens], loss_target[0])
    dy = jnp.pad(dy_real, ((n_meta, lp - tokens), (0, 0)))
    loss = lax.psum(loss_part[0, 0], MESH_AXES)

    sent = {}

    def send_grad(n, dmat, unlayout):
        sent[n] = _spread_start(unlayout(dmat)[0], True, "grad_start_" + n)
        return sent[n][4][0, 0]

    d_h1, d_n2, d_wgu, d_wdn = vjp_ffn(dy)
    behind = send_grad("w_gate_up", d_wgu, same) + send_grad("w_down", d_wdn, un_dn)
    d_h0, d_ya, d_yb, d_zg, d_wa, d_wb, d_wo = vjp_merge(d_h1 + behind)
    behind = send_grad("w_o", d_wo, un_wo) + send_grad("w_branch_a", d_wa, same) + send_grad("w_branch_b", d_wb, same)
    d_zr, d_zf, d_small_mix, d_w2, d_a2, d_g2 = vjp_mix((d_ya + behind, d_yb))
    dproj_b = jnp.concatenate([d_zr.astype(BF16), d_zf.astype(BF16), d_zg.astype(BF16)], axis=1)
    d_wcat = _matmul(dproj_b, xn_b, ta=True, out_dtype=BF16, name="in_proj_dw")
    send_grad("w_in", d_wcat, un_in)
    d_xn = _matmul(dproj_b, w_cat, out_dtype=F32, name="in_proj_dx", after=sent["w_in"][4])
    send_grad("rwkv_w2", d_w2, un_w2)
    send_grad("rwkv_a2", d_a2, un_a2)
    send_grad("rwkv_g2", d_g2, un_g2)
    d_meta, g_x, d_n1 = vjp_embed((d_h0, d_xn))
    send_grad("meta_tokens", d_meta, un_meta)

    grads = dict(zip(SMALL, _sum_over_devices((d_n1, *d_small_mix, d_n2))))
    grads = {n: g.reshape(w[n].shape) for n, g in grads.items()}

    delta, new_m, new_v = {}, {}, {}
    after = g_x
    for n in ("w_gate_up", "w_down", "w_o", "w_branch_a", "w_branch_b", "rwkv_g2", "rwkv_a2", "rwkv_w2", "meta_tokens", "w_in"):
        src, got = _spread_wait(sent[n][:4], after, True, "grad_wait_" + n)
        g = _sum_slots_call(_with_own_slot(got, lax.dynamic_index_in_dim(src, mine, 0, keepdims=False), mine))
        g = g[:local[n].shape[0]]
        d_, m_, v_ = _adamw_call(local[n], g, local["m_" + n], local["v_" + n])
        back = (lambda t: jnp.transpose(t)[None]) if n == "w_in" else (lambda t: t.reshape(w[n].shape))
        grads[n], delta[n], new_m[n], new_v[n] = (back(t) for t in (g, d_, m_, v_))
        after = m_
    n_small = sum(w[n].size for n in SMALL)
    width = -(-n_small // 1024) * 1024
    packs = [_pack([src[n] if p == "" else given[p + n] for n in SMALL], width)
             for p, src in (("", w), ("", grads), ("m_", None), ("v_", None))]
    like = [w[n] for n in SMALL]
    for out, packed in zip((delta, new_m, new_v), _adamw_call(*packs)):
        out.update(dict(zip(SMALL, _unpack(packed, like))))

    return (loss, g_x[None], *[grads[n] for n in WEIGHTS], *[delta[n] for n in WEIGHTS],
            *[new_m[n] for n in WEIGHTS], *[new_v[n] for n in WEIGHTS])
```

```python
import functools

import jax
import jax.numpy as jnp
from jax import lax
from jax.experimental import pallas as pl
from jax.experimental.pallas import tpu as pltpu

F32 = jnp.float32
BF16 = jnp.bfloat16

N_DEV = 8
MESH_AXES = ("x", "y", "c")
HEAD_DIM = 64
TOKEN_TILE = 128
WKV_CHUNK = 64
WKV_PAIRS_PER_STEP = 8
PAIR = 2 * HEAD_DIM
ATTN_BLOCK = 128
ATTN_BLOCK_BIG = 384
RMS_EPS = 1e-6
GN_EPS = 64e-5
L2_FLOOR = 1e-12
NEG_BIG = -1e30
ADAM_LR, ADAM_B1, ADAM_B2, ADAM_EPS, ADAM_WD, ADAM_STEP = 0.001, 0.9, 0.999, 1e-08, 0.01, 10
VMEM_BYTES_V7X = 64 * 1024 * 1024
VMEM_LIMIT_CAP = 56 * 1024 * 1024
VMEM_LIMIT_FLOOR = 32 * 1024 * 1024
MATMUL_VMEM_BUDGET = 36 * 1024 * 1024
GRID_STEP_BYTES = 1024 * 1024
ACC_BYTES_PER_HBM_BYTE = 6


def _vmem_limit(estimate_bytes):
    return int(min(max(estimate_bytes * 5 // 4, VMEM_LIMIT_FLOOR), VMEM_LIMIT_CAP))


def _pick(dim, cands):
    for c in cands:
        if dim % c == 0:
            return c
    return dim


def _row_tile(rows, width, itemsize=4, budget=2 * 1024 * 1024):
    for c in (1408, 1024, 704, 512, 384, 256, 128, 64, 32, 16, 8):
        if rows % c == 0 and c * width * itemsize <= budget:
            return c
    return rows


def _row_tile_ragged(rows, width, itemsize=4, budget=2 * 1024 * 1024):
    tile = _row_tile(rows, width, itemsize, budget)
    if tile * width * itemsize <= budget or rows < 16:
        return tile
    padded = -(-rows // 16) * 16
    for c in (1408, 1024, 704, 512, 384, 336, 256, 192, 128, 96, 64, 48, 32, 16):
        if padded % c == 0 and c * width * itemsize <= budget:
            return c
    return tile


def _dg(a, b, ta, tb):
    dims = (((0 if ta else 1,), (1 if tb else 0,)), ((), ()))
    return lax.dot_general(a, b, dims, preferred_element_type=F32)


def _split(x, n):
    parts = []
    for _ in range(n):
        h = x.astype(BF16)
        parts.append(h)
        x = x - h.astype(F32)
    return parts


def _mm(a, b, ta=False, tb=False):
    return _dg(a.astype(BF16), b.astype(BF16), ta, tb)


def _matmul(a, b, ta=False, tb=False, out_dtype=F32, name="matmul", after=None, b_slots=False, out_slots=0):
    if ta:
        kdim, m = a.shape
    else:
        m, kdim = a.shape
    if b_slots:
        n_slots, brows, bcols = b.shape
        n, k2 = (brows, n_slots * bcols) if tb else (n_slots * bcols, brows)
    elif tb:
        n, k2 = b.shape
    else:
        k2, n = b.shape
    assert kdim == k2, (a.shape, b.shape, ta, tb)
    sa, sb, so = a.dtype.itemsize, b.dtype.itemsize, jnp.dtype(out_dtype).itemsize
    n_unit = bcols if (b_slots and not tb) else (n // out_slots if out_slots else n)
    k_unit = bcols if (b_slots and tb) else kdim
    tm, tn, tk, n_outer = _matmul_tiles(m, n, kdim, ta, sa, sb, so, n_unit, k_unit)
    nk = kdim // tk
    ij = (lambda f: lambda j, i, k: f(i, j, k)) if n_outer else (lambda f: f)

    order = () if after is None else (after,)

    def body(a_ref, b_ref, *rest):
        o_ref, acc = rest[len(order)], rest[len(order) + 1:]
        part = _dg(a_ref[...].astype(BF16), b_ref[...].astype(BF16), ta, tb)
        if nk == 1:
            o_ref[...] = part.astype(o_ref.dtype)
            return
        kk = pl.program_id(2)

        @pl.when(kk == 0)
        def _():
            acc[0][...] = part

        @pl.when(kk > 0)
        def _():
            acc[0][...] += part

        @pl.when(kk == nk - 1)
        def _():
            o_ref[...] = acc[0][...].astype(o_ref.dtype)

    a_spec = pl.BlockSpec((tk, tm), ij(lambda i, j, k: (k, i))) if ta else pl.BlockSpec((tm, tk), ij(lambda i, j, k: (i, k)))
    if b_slots and tb:
        per = bcols // tk
        b_spec = pl.BlockSpec((None, tn, tk), ij(lambda i, j, k: (k // per, j, k % per)))
    elif b_slots:
        per = bcols // tn
        b_spec = pl.BlockSpec((None, tk, tn), ij(lambda i, j, k: (j // per, k, j % per)))
    elif tb:
        b_spec = pl.BlockSpec((tn, tk), ij(lambda i, j, k: (j, k)))
    else:
        b_spec = pl.BlockSpec((tk, tn), ij(lambda i, j, k: (k, j)))
    if out_slots:
        per_out = n // out_slots // tn
        out_spec = pl.BlockSpec((None, tm, tn), ij(lambda i, j, k: (j // per_out, i, j % per_out)))
        out_shape = jax.ShapeDtypeStruct((out_slots, m, n // out_slots), out_dtype)
    else:
        out_spec = pl.BlockSpec((tm, tn), ij(lambda i, j, k: (i, j)))
        out_shape = jax.ShapeDtypeStruct((m, n), out_dtype)
    return pl.pallas_call(
        body, name=name,
        grid=(n // tn, m // tm, nk) if n_outer else (m // tm, n // tn, nk),
        in_specs=[a_spec, b_spec] + [pl.BlockSpec(memory_space=pl.ANY)] * len(order),
        out_specs=out_spec,
        out_shape=out_shape,
        scratch_shapes=[pltpu.VMEM((tm, tn), F32)] if nk > 1 else [],
        compiler_params=pltpu.CompilerParams(dimension_semantics=("parallel", "parallel", "arbitrary"),
                                             vmem_limit_bytes=_vmem_limit(_matmul_vmem(tm, tn, tk, nk, sa, sb, so))),
    )(a, b, *order)


def _matmul_vmem(tm, tn, tk, nk, sa, sb, so):
    return 2 * (tm * tk * sa + tk * tn * sb + tm * tn * so) + tm * tn * 4 + (tm * tn * 4 if nk > 1 else 0)


def _matmul_tiles(m, n, kdim, ta, sa, sb, so, n_unit, k_unit):
    lane = (2816, 2176, 2048, 1408, 1024, 640, 512, 384, 256, 128)
    sublane = (2816, 2176, 2048, 1408, 1024, 704, 512, 384, 256, 128)
    divs = lambda dim, cands: [c for c in cands if dim % c == 0] or [dim]
    best = None
    for tm in divs(m, lane if ta else sublane):
        for tn in divs(n_unit, lane):
            for tk in divs(k_unit, sublane if ta else lane) + ([kdim] if k_unit == kdim and (ta or kdim <= 2048) else []):
                nk, nm, nn = kdim // tk, m // tm, n // tn
                if _matmul_vmem(tm, tn, tk, nk, sa, sb, so) > MATMUL_VMEM_BUDGET:
                    continue
                acc_bytes = m * n * 4 * 3 * nk // ACC_BYTES_PER_HBM_BYTE if nk > 1 else 0
                fixed = m * n * so + acc_bytes + nm * nn * nk * GRID_STEP_BYTES
                for n_outer in (False, True):
                    if n_outer:
                        a_reads, b_reads = (1 if (nk == 1 and nm == 1) else nn), (1 if nk == 1 else nm)
                    else:
                        a_reads, b_reads = (1 if nk == 1 else nn), (1 if (nk == 1 and nn == 1) else nm)
                    cost = m * kdim * sa * a_reads + kdim * n * sb * b_reads + fixed
                    if best is None or cost < best[0]:
                        best = (cost, tm, tn, tk, n_outer)
    return best[1:]


@jax.custom_vjp
def dense(x, w):
    return _matmul(x.astype(BF16), w, name="dense_fwd")


def _dense_fwd(x, w):
    return _matmul(x.astype(BF16), w, name="dense_fwd"), (x.astype(BF16), w, jnp.zeros((), x.dtype))


def _dense_bwd(res, dy):
    xb, w, like = res
    dyb = dy.astype(BF16)
    dx = _matmul(dyb, w, tb=True, out_dtype=like.dtype, name="dense_dx")
    dw = _matmul(xb, dyb, ta=True, out_dtype=w.dtype, name="dense_dw")
    return dx, dw


dense.defvjp(_dense_fwd, _dense_bwd)


def _make_dense_cols(out_dtype):
    @jax.custom_vjp
    def op(x, w_slots):
        return _matmul(x.astype(BF16), w_slots, b_slots=True, out_dtype=out_dtype, name="dense_cols_fwd")

    def fwd(x, w_slots):
        assert x.dtype == F32
        xb = x.astype(BF16)
        return _matmul(xb, w_slots, b_slots=True, out_dtype=out_dtype, name="dense_cols_fwd"), (xb, w_slots)

    def bwd(res, dy):
        xb, w_slots = res
        dyb = dy.astype(BF16)
        dx = _matmul(dyb, w_slots, tb=True, b_slots=True, out_dtype=F32, name="dense_cols_dx")
        dw = _matmul(xb, dyb, ta=True, out_slots=w_slots.shape[0], out_dtype=w_slots.dtype, name="dense_cols_dw")
        return dx, dw

    op.defvjp(fwd, bwd)
    return op


dense_cols = _make_dense_cols(F32)
dense_cols_bf16 = _make_dense_cols(BF16)


def _swiglu_call(gu, d_act=None):
    rows, two_f = gu.shape
    f = two_f // 2
    tr = _row_tile(rows, two_f, itemsize=2, budget=3 * 1024 * 1024)
    half = lambda j: pl.BlockSpec((tr, f), lambda i, j=j: (i, j))
    ops = (gu, gu) if d_act is None else (gu, gu, d_act)

    def body(*refs):
        g, u = refs[0][...].astype(F32), refs[1][...].astype(F32)
        s = 1.0 / (1.0 + jnp.exp(-g))
        if d_act is None:
            refs[2][...] = (g * s * u).astype(BF16)
        else:
            d = refs[2][...].astype(F32)
            refs[3][:, :f] = (d * u * s * (1.0 + g * (1.0 - s))).astype(BF16)
            refs[3][:, f:] = (d * g * s).astype(BF16)

    width = f if d_act is None else two_f
    return pl.pallas_call(
        body, name="swiglu_fwd" if d_act is None else "swiglu_bwd", grid=(rows // tr,),
        in_specs=[half(0), half(1)] + ([half(0)] if d_act is not None else []),
        out_specs=pl.BlockSpec((tr, width), lambda i: (i, 0)),
        out_shape=jax.ShapeDtypeStruct((rows, width), BF16),
        compiler_params=pltpu.CompilerParams(dimension_semantics=("parallel",)),
    )(*ops)


@jax.custom_vjp
def swiglu(gu):
    return _swiglu_call(gu)


swiglu.defvjp(lambda gu: (_swiglu_call(gu), gu), lambda gu, d_act: (_swiglu_call(gu, d_act),))


def _merge_call(zg, a, b, dm=None):
    rows, d = a.shape
    tr = _row_tile(rows, d, budget=1024 * 1024)
    half = lambda j: pl.BlockSpec((tr, d), lambda i, j=j: (i, j))
    tile = half(0)

    def body(*refs):
        ga = 1.0 / (1.0 + jnp.exp(-refs[0][...]))
        gb = 1.0 / (1.0 + jnp.exp(-refs[1][...]))
        av, bv = refs[2][...].astype(F32), refs[3][...].astype(F32)
        if dm is None:
            refs[4][...] = (ga * av + gb * bv).astype(BF16)
        else:
            dv = refs[4][...].astype(F32)
            dzg_ref, da_ref, db_ref = refs[5:]
            dzg_ref[:, :d] = dv * av * ga * (1.0 - ga)
            dzg_ref[:, d:] = dv * bv * gb * (1.0 - gb)
            da_ref[...] = (dv * ga).astype(BF16)
            db_ref[...] = (dv * gb).astype(BF16)

    shape_b = jax.ShapeDtypeStruct((rows, d), BF16)
    if dm is None:
        out_specs, out_shape, ops = tile, shape_b, (zg, zg, a, b)
    else:
        out_specs = [pl.BlockSpec((tr, 2 * d), lambda i: (i, 0)), tile, tile]
        out_shape = [jax.ShapeDtypeStruct((rows, 2 * d), F32), shape_b, shape_b]
        ops = (zg, zg, a, b, dm)
    return pl.pallas_call(
        body, name="merge_fwd" if dm is None else "merge_bwd", grid=(rows // tr,),
        in_specs=[half(0), half(1)] + [tile] * (len(ops) - 2),
        out_specs=out_specs, out_shape=out_shape,
        compiler_params=pltpu.CompilerParams(dimension_semantics=("parallel",)),
    )(*ops)


@jax.custom_vjp
def gated_merge(zg, a, b):
    return _merge_call(zg, a, b)


gated_merge.defvjp(lambda zg, a, b: (_merge_call(zg, a, b), (zg, a, b)),
                   lambda res, dm: tuple(_merge_call(*res, dm)))


def _rms_fwd_call(x, g):
    rows, d = x.shape
    tr = _row_tile(rows, d)

    def body(x_ref, g_ref, y_ref):
        xv = x_ref[...]
        rstd = lax.rsqrt(jnp.mean(xv * xv, axis=1, keepdims=True) + RMS_EPS)
        y_ref[...] = (xv * rstd) * g_ref[...]

    return pl.pallas_call(
        body, name="rms_fwd", grid=(rows // tr,),
        in_specs=[pl.BlockSpec((tr, d), lambda i: (i, 0)), pl.BlockSpec((1, d), lambda i: (0, 0))],
        out_specs=pl.BlockSpec((tr, d), lambda i: (i, 0)),
        out_shape=jax.ShapeDtypeStruct((rows, d), F32),
        compiler_params=pltpu.CompilerParams(dimension_semantics=("parallel",)),
    )(x, g)


def _rms_bwd_call(x, g, dy):
    rows, d = x.shape
    tr = _row_tile(rows, d)

    def body(x_ref, g_ref, dy_ref, dx_ref, dg_ref):
        @pl.when(pl.program_id(0) == 0)
        def _():
            dg_ref[...] = jnp.zeros_like(dg_ref)

        xv = x_ref[...]
        dyv = dy_ref[...]
        rstd = lax.rsqrt(jnp.mean(xv * xv, axis=1, keepdims=True) + RMS_EPS)
        xhat = xv * rstd
        dxhat = dyv * g_ref[...]
        dx_ref[...] = rstd * (dxhat - xhat * jnp.mean(dxhat * xhat, axis=1, keepdims=True))
        dg_ref[...] += jnp.sum(dyv * xhat, axis=0, keepdims=True)

    return pl.pallas_call(
        body, name="rms_bwd", grid=(rows // tr,),
        in_specs=[pl.BlockSpec((tr, d), lambda i: (i, 0)), pl.BlockSpec((1, d), lambda i: (0, 0)),
                  pl.BlockSpec((tr, d), lambda i: (i, 0))],
        out_specs=[pl.BlockSpec((tr, d), lambda i: (i, 0)), pl.BlockSpec((1, d), lambda i: (0, 0))],
        out_shape=[jax.ShapeDtypeStruct((rows, d), F32), jax.ShapeDtypeStruct((1, d), F32)],
        compiler_params=pltpu.CompilerParams(dimension_semantics=("arbitrary",)),
    )(x, g, dy)


@jax.custom_vjp
def rmsnorm(x, g):
    return _rms_fwd_call(x, g)


rmsnorm.defvjp(lambda x, g: (_rms_fwd_call(x, g), (x, g)), lambda res, dy: tuple(_rms_bwd_call(res[0], res[1], dy)))


def _bcast_call(x, p, mul):
    rows, d = x.shape
    tr = _row_tile(rows, d)

    def body(x_ref, p_ref, y_ref):
        y_ref[...] = x_ref[...] * p_ref[...] if mul else x_ref[...] + p_ref[...]

    return pl.pallas_call(
        body, name="bcast_mul" if mul else "bcast_add", grid=(rows // tr,),
        in_specs=[pl.BlockSpec((tr, d), lambda i: (i, 0)), pl.BlockSpec((1, d), lambda i: (0, 0))],
        out_specs=pl.BlockSpec((tr, d), lambda i: (i, 0)),
        out_shape=jax.ShapeDtypeStruct((rows, d), F32),
        compiler_params=pltpu.CompilerParams(dimension_semantics=("parallel",)),
    )(x, p)


def _colsum_call(a, b=None):
    rows, d = a.shape
    tr = _row_tile(rows, d)
    ops = (a,) if b is None else (a, b)

    def body(*refs):
        o_ref = refs[-1]

        @pl.when(pl.program_id(0) == 0)
        def _():
            o_ref[...] = jnp.zeros_like(o_ref)

        v = refs[0][...] if b is None else refs[0][...] * refs[1][...]
        o_ref[...] += jnp.sum(v, axis=0, keepdims=True)

    return pl.pallas_call(
        body, name="colsum", grid=(rows // tr,),
        in_specs=[pl.BlockSpec((tr, d), lambda i: (i, 0))] * len(ops),
        out_specs=pl.BlockSpec((1, d), lambda i: (0, 0)),
        out_shape=jax.ShapeDtypeStruct((1, d), F32),
        compiler_params=pltpu.CompilerParams(dimension_semantics=("arbitrary",)),
    )(*ops)


@jax.custom_vjp
def badd(x, p):
    return _bcast_call(x, p, False)


badd.defvjp(lambda x, p: (_bcast_call(x, p, False), None), lambda res, dy: (dy, _colsum_call(dy)))


def _head_sums(x):
    i = lax.broadcasted_iota(jnp.int32, (PAIR, PAIR), 0) // HEAD_DIM
    j = lax.broadcasted_iota(jnp.int32, (PAIR, PAIR), 1) // HEAD_DIM
    ones = jnp.where(i == j, 1.0, 0.0).astype(BF16)
    hi, lo = _split(x, 2)
    cols = [slice(p * PAIR, (p + 1) * PAIR) for p in range(x.shape[1] // PAIR)]
    return jnp.concatenate([_dg(hi[:, c], ones, False, False) + _dg(lo[:, c], ones, False, False) for c in cols], axis=1)


def _head_rms_fwd_call(x, g):
    rows, w = x.shape
    tr = _row_tile(rows, w, budget=1024 * 1024)

    def body(x_ref, g_ref, y_ref):
        xv = x_ref[...]
        rstd = lax.rsqrt(_head_sums(xv * xv) * (1.0 / HEAD_DIM) + RMS_EPS)
        y_ref[...] = (xv * rstd) * g_ref[...]

    return pl.pallas_call(
        body, name="head_rms_fwd", grid=(rows // tr,),
        in_specs=[pl.BlockSpec((tr, w), lambda i: (i, 0)), pl.BlockSpec((1, w), lambda i: (0, 0))],
        out_specs=pl.BlockSpec((tr, w), lambda i: (i, 0)),
        out_shape=jax.ShapeDtypeStruct((rows, w), F32),
        compiler_params=pltpu.CompilerParams(dimension_semantics=("parallel",)),
    )(x, g)


def _head_rms_bwd_call(x, g, dy):
    rows, w = x.shape
    tr = _row_tile(rows, w, budget=1024 * 1024)

    def body(x_ref, g_ref, dy_ref, dx_ref, dg_ref):
        @pl.when(pl.program_id(0) == 0)
        def _():
            dg_ref[...] = jnp.zeros_like(dg_ref)

        xv, dyv = x_ref[...], dy_ref[...]
        rstd = lax.rsqrt(_head_sums(xv * xv) * (1.0 / HEAD_DIM) + RMS_EPS)
        xhat = xv * rstd
        dxhat = dyv * g_ref[...]
        dx_ref[...] = rstd * (dxhat - xhat * (_head_sums(dxhat * xhat) * (1.0 / HEAD_DIM)))
        dg_ref[...] += jnp.sum(dyv * xhat, axis=0, keepdims=True)

    return pl.pallas_call(
        body, name="head_rms_bwd", grid=(rows // tr,),
        in_specs=[pl.BlockSpec((tr, w), lambda i: (i, 0)), pl.BlockSpec((1, w), lambda i: (0, 0)),
                  pl.BlockSpec((tr, w), lambda i: (i, 0))],
        out_specs=[pl.BlockSpec((tr, w), lambda i: (i, 0)), pl.BlockSpec((1, w), lambda i: (0, 0))],
        out_shape=[jax.ShapeDtypeStruct((rows, w), F32), jax.ShapeDtypeStruct((1, w), F32)],
        compiler_params=pltpu.CompilerParams(dimension_semantics=("arbitrary",)),
    )(x, g, dy)


@jax.custom_vjp
def head_rms(x, g):
    return _head_rms_fwd_call(x, g)


head_rms.defvjp(lambda x, g: (_head_rms_fwd_call(x, g), (x, g)),
                lambda res, dy: tuple(_head_rms_bwd_call(res[0], res[1], dy)))


def _gn_fwd_call(y, r, kf, v, g, gw, gb, rk):
    rows, w = y.shape
    tr = _row_tile(rows, w, budget=512 * 1024)

    def body(y_ref, r_ref, kf_ref, v_ref, g_ref, gw_ref, gb_ref, rk_ref, o_ref):
        yv = y_ref[...]
        yc = yv - _head_sums(yv) * (1.0 / HEAD_DIM)
        rstd = lax.rsqrt(_head_sums(yc * yc) * (1.0 / HEAD_DIM) + GN_EPS)
        s = _head_sums(r_ref[...] * kf_ref[...] * rk_ref[...])
        o_ref[...] = ((yc * rstd) * gw_ref[...] + gb_ref[...] + s * v_ref[...]) * g_ref[...]

    tok = pl.BlockSpec((tr, w), lambda i: (i, 0))
    par = pl.BlockSpec((1, w), lambda i: (0, 0))
    return pl.pallas_call(
        body, name="gn_bonus_fwd", grid=(rows // tr,),
        in_specs=[tok] * 5 + [par] * 3, out_specs=tok,
        out_shape=jax.ShapeDtypeStruct((rows, w), F32),
        compiler_params=pltpu.CompilerParams(dimension_semantics=("parallel",)),
    )(y, r, kf, v, g, gw, gb, rk)


def _gn_bwd_call(y, r, kf, v, g, gw, gb, rk, do):
    rows, w = y.shape
    tr = _row_tile(rows, w, budget=512 * 1024)

    def body(y_ref, r_ref, kf_ref, v_ref, g_ref, gw_ref, gb_ref, rk_ref, do_ref,
             dy_ref, dr_ref, dkf_ref, dv_ref, dg_ref, dgw_ref, dgb_ref, drk_ref):
        @pl.when(pl.program_id(0) == 0)
        def _():
            dgw_ref[...] = jnp.zeros_like(dgw_ref)
            dgb_ref[...] = jnp.zeros_like(dgb_ref)
            drk_ref[...] = jnp.zeros_like(drk_ref)

        yv, rv, kv, vv, rkv = y_ref[...], r_ref[...], kf_ref[...], v_ref[...], rk_ref[...]
        mean = lambda t: _head_sums(t) * (1.0 / HEAD_DIM)
        yc = yv - mean(yv)
        rstd = lax.rsqrt(mean(yc * yc) + GN_EPS)
        yhat = yc * rstd
        s = _head_sums(rv * kv * rkv)
        dg_ref[...] = do_ref[...] * (yhat * gw_ref[...] + gb_ref[...] + s * vv)
        dov = do_ref[...] * g_ref[...]
        dyhat = dov * gw_ref[...]
        dy_ref[...] = rstd * (dyhat - mean(dyhat) - yhat * mean(dyhat * yhat))
        ds = _head_sums(dov * vv)
        dv_ref[...] = s * dov
        dr_ref[...] = ds * kv * rkv
        dkf_ref[...] = ds * rv * rkv
        dgw_ref[...] += jnp.sum(dov * yhat, axis=0, keepdims=True)
        dgb_ref[...] += jnp.sum(dov, axis=0, keepdims=True)
        drk_ref[...] += jnp.sum(ds * rv * kv, axis=0, keepdims=True)

    tok = pl.BlockSpec((tr, w), lambda i: (i, 0))
    par = pl.BlockSpec((1, w), lambda i: (0, 0))
    tshape = jax.ShapeDtypeStruct((rows, w), F32)
    pshape = jax.ShapeDtypeStruct((1, w), F32)
    return pl.pallas_call(
        body, name="gn_bonus_bwd", grid=(rows // tr,),
        in_specs=[tok] * 5 + [par] * 3 + [tok], out_specs=[tok] * 5 + [par] * 3,
        out_shape=[tshape] * 5 + [pshape] * 3,
        compiler_params=pltpu.CompilerParams(dimension_semantics=("arbitrary",)),
    )(y, r, kf, v, g, gw, gb, rk, do)


@jax.custom_vjp
def gn_bonus(y, r, kf, v, g, gw, gb, rk):
    return _gn_fwd_call(y, r, kf, v, g, gw, gb, rk)


def _gn_bwd(res, do):
    return tuple(_gn_bwd_call(*res, do))


gn_bonus.defvjp(lambda *a: (_gn_fwd_call(*a), a), _gn_bwd)


PREP_ROWS = 128


def _prep_segments(rw, lora_w, lora_a, lora_g):
    at = 3 * rw
    seg = {"r": (0, rw), "k": (rw, 2 * rw), "v": (2 * rw, 3 * rw)}
    for name, n in (("wd", lora_w), ("ad", lora_a), ("gd", lora_g)):
        seg[name] = (at, at + _pad128(n))
        at += _pad128(n)
    return seg, at


def _prep_shifted(z_ref, zlast_ref, mu_ref, seg, first_tile):
    lo, hi = seg
    zr = z_ref[:, lo:hi]
    rows = zr.shape[0]
    before = jnp.where(first_tile, 0.0, zlast_ref[7:8, lo:hi])
    row0 = lax.broadcasted_iota(jnp.int32, zr.shape, 0) == 0
    diff = jnp.where(row0, before, pltpu.roll(zr, 1, axis=0)) - zr
    return zr + diff * mu_ref[:, lo:hi], diff


def _prep_forward_values(z_ref, zlast_ref, mu_ref, w0_ref, a0_ref, kk_ref, ka_ref, w2_ref, a2_ref, g2_ref, segs, first_tile):
    z = {n: _prep_shifted(z_ref, zlast_ref, mu_ref, segs[n], first_tile) for n in segs}
    r, k, v, wd, ad, gd = (z[n][0] for n in ("r", "k", "v", "wd", "ad", "gd"))
    twd = jnp.tanh(wd)
    pw = _mm(twd, w2_ref[...]) + w0_ref[...]
    lw = -jnp.exp(-(jnp.maximum(-pw, 0.0) + jnp.log(1.0 + jnp.exp(-jnp.abs(pw)))) - 0.5)
    a_sig = 1.0 / (1.0 + jnp.exp(-(_mm(ad, a2_ref[...]) + a0_ref[...])))
    sg = 1.0 / (1.0 + jnp.exp(-gd))
    kx = k * kk_ref[...]
    nrm = jnp.sqrt(_head_sums(kx * kx))
    inv = 1.0 / jnp.maximum(nrm, L2_FLOOR)
    return dict(z=z, r=r, k=k, v=v, twd=twd, pw=pw, lw=lw, a_sig=a_sig, sg=sg, ad=ad, kk=kx * inv, inv=inv, live=nrm > L2_FLOOR)


def _prep_specs(tokens, rpad, rw, w2, a2, g2):
    tr = PREP_ROWS
    tile = lambda w: pl.BlockSpec((tr, w), lambda i: (i, 0))
    before = pl.BlockSpec((8, rpad), lambda i: (jnp.maximum(i * (tr // 8) - 1, 0), 0))
    whole = lambda a: pl.BlockSpec(a.shape, lambda i: (0, 0))
    par = pl.BlockSpec((1, rw), lambda i: (0, 0))
    return tile, before, whole, par, pl.BlockSpec((1, rpad), lambda i: (0, 0))


def _prep_fwd_call(zr, mu, w0, a0, k_k, k_a, w2, a2, g2):
    tokens, rpad = zr.shape
    rw = w0.shape[1]
    segs, _ = _prep_segments(rw, w2.shape[0], a2.shape[0], g2.shape[0])
    tile, before, whole, par, mu_spec = _prep_specs(tokens, rpad, rw, w2, a2, g2)

    def body(z_ref, zlast_ref, mu_ref, w0_ref, a0_ref, kk_ref, ka_ref, w2_ref, a2_ref, g2_ref,
             r_ref, lw_ref, kf_ref, v_ref, na_ref, b_ref, g_ref):
        f = _prep_forward_values(z_ref, zlast_ref, mu_ref, w0_ref, a0_ref, kk_ref, ka_ref, w2_ref, a2_ref, g2_ref,
                                 segs, pl.program_id(0) == 0)
        r_ref[...] = f["r"]
        v_ref[...] = f["v"]
        lw_ref[...] = f["lw"]
        kf_ref[...] = f["k"] * (1.0 + (f["a_sig"] - 1.0) * ka_ref[...])
        na_ref[...] = -f["kk"]
        b_ref[...] = f["kk"] * f["a_sig"]
        g_ref[...] = _mm(f["sg"], g2_ref[...])

    shape = jax.ShapeDtypeStruct((tokens, rw), F32)
    return pl.pallas_call(
        body, name="rwkv_prep_fwd", grid=(tokens // PREP_ROWS,),
        in_specs=[tile(rpad), before, mu_spec, par, par, par, par, whole(w2), whole(a2), whole(g2)],
        out_specs=[tile(rw)] * 7, out_shape=[shape] * 7,
        compiler_params=pltpu.CompilerParams(dimension_semantics=("parallel",), vmem_limit_bytes=VMEM_LIMIT_CAP),
    )(zr, zr, mu, w0, a0, k_k, k_a, w2, a2, g2)


def _prep_bwd_call(zr, mu, w0, a0, k_k, k_a, w2, a2, g2, cts):
    tokens, rpad = zr.shape
    rw = w0.shape[1]
    segs, _ = _prep_segments(rw, w2.shape[0], a2.shape[0], g2.shape[0])
    tile, before, whole, par, mu_spec = _prep_specs(tokens, rpad, rw, w2, a2, g2)
    nt = tokens // PREP_ROWS
    rev = lambda spec: pl.BlockSpec(spec.block_shape, lambda i, f=spec.index_map: f(nt - 1 - i))

    def body(z_ref, zlast_ref, mu_ref, w0_ref, a0_ref, kk_ref, ka_ref, w2_ref, a2_ref, g2_ref,
             dr_ref, dlw_ref, dkf_ref, dv_ref, dna_ref, db_ref, dg_ref,
             dz_ref, dmu_ref, dw0_ref, da0_ref, dkk_ref, dka_ref, dw2_ref, da2_ref, dg2_ref, carry):
        step = pl.program_id(0)

        @pl.when(step == 0)
        def _():
            for ref in (dmu_ref, dw0_ref, da0_ref, dkk_ref, dka_ref, dw2_ref, da2_ref, dg2_ref, carry):
                ref[...] = jnp.zeros_like(ref)

        f = _prep_forward_values(z_ref, zlast_ref, mu_ref, w0_ref, a0_ref, kk_ref, ka_ref, w2_ref, a2_ref, g2_ref,
                                 segs, step == nt - 1)
        k, kk, a_sig, sg, twd = f["k"], f["kk"], f["a_sig"], f["sg"], f["twd"]
        colsum = lambda t: jnp.sum(t, axis=0, keepdims=True)
        dkf, db, dg = dkf_ref[...], db_ref[...], dg_ref[...]
        ka = ka_ref[...]
        dgd = _mm(dg, g2_ref[...], tb=True) * sg * (1.0 - sg)
        dg2_ref[...] += _mm(sg, dg, ta=True)
        dkk = db * a_sig - dna_ref[...]
        da_sig = db * kk + dkf * k * ka
        dk = dkf * (1.0 + (a_sig - 1.0) * ka)
        dka_ref[...] += colsum(dkf * k * (a_sig - 1.0))
        along = jnp.where(f["live"], _head_sums(dkk * kk), 0.0)
        dkx = (dkk - kk * along) * f["inv"]
        dk = dk + dkx * kk_ref[...]
        dkk_ref[...] += colsum(dkx * k)
        dpa = da_sig * a_sig * (1.0 - a_sig)
        da0_ref[...] += colsum(dpa)
        dad = _mm(dpa, a2_ref[...], tb=True)
        da2_ref[...] += _mm(f["ad"], dpa, ta=True)
        dpw = dlw_ref[...] * f["lw"] / (1.0 + jnp.exp(f["pw"]))
        dw0_ref[...] += colsum(dpw)
        dwd = _mm(dpw, w2_ref[...], tb=True) * (1.0 - twd * twd)
        dw2_ref[...] += _mm(twd, dpw, ta=True)
        rows = PREP_ROWS
        last = lax.broadcasted_iota(jnp.int32, (rows, 1), 0) == rows - 1
        for name, dz in (("r", dr_ref[...]), ("k", dk), ("v", dv_ref[...]), ("wd", dwd), ("ad", dad), ("gd", dgd)):
            lo, hi = segs[name]
            mu_s = mu_ref[:, lo:hi]
            dmu_ref[:, lo:hi] += colsum(dz * f["z"][name][1])
            later = dz * mu_s
            dz_ref[:, lo:hi] = dz * (1.0 - mu_s) + jnp.where(last, carry[:, lo:hi], pltpu.roll(later, rows - 1, axis=0))
            carry[:, lo:hi] = later[0:1, :]

    tok = jax.ShapeDtypeStruct((tokens, rw), F32)
    acc = lambda a: jax.ShapeDtypeStruct(a.shape, F32)
    return pl.pallas_call(
        body, name="rwkv_prep_bwd", grid=(nt,),
        in_specs=[rev(tile(rpad)), rev(before), mu_spec, par, par, par, par, whole(w2), whole(a2), whole(g2)]
                 + [rev(tile(rw))] * 7,
        out_specs=[rev(tile(rpad)), mu_spec, par, par, par, par, whole(w2), whole(a2), whole(g2)],
        out_shape=[jax.ShapeDtypeStruct((tokens, rpad), F32), acc(mu), acc(w0), acc(a0), acc(k_k), acc(k_a), acc(w2), acc(a2), acc(g2)],
        scratch_shapes=[pltpu.VMEM((1, rpad), F32)],
        compiler_params=pltpu.CompilerParams(dimension_semantics=("arbitrary",), vmem_limit_bytes=VMEM_LIMIT_CAP),
    )(zr, zr, mu, w0, a0, k_k, k_a, w2, a2, g2, *cts)


@jax.custom_vjp
def rwkv_prep(zr, mu, w0, a0, k_k, k_a, w2, a2, g2):
    return tuple(_prep_fwd_call(zr, mu, w0, a0, k_k, k_a, w2, a2, g2))


def _rwkv_prep_bwd(res, cts):
    zr, mu, w0, a0, k_k, k_a, w2, a2, g2 = res
    dz, dmu, dw0, da0, dkk, dka, dw2, da2, dg2 = _prep_bwd_call(*res, cts)
    return dz, dmu, dw0, da0, dkk, dka, dw2.astype(w2.dtype), da2.astype(a2.dtype), dg2.astype(g2.dtype)


rwkv_prep.defvjp(lambda *a: (tuple(_prep_fwd_call(*a)), a), _rwkv_prep_bwd)


def _pair_masks(rows):
    lane = lax.broadcasted_iota(jnp.int32, (rows, PAIR), 1)
    return lane < HEAD_DIM, lane >= HEAD_DIM


def _bd(x):
    m0, m1 = _pair_masks(x.shape[0])
    return jnp.concatenate([jnp.where(m0, x, 0.0), jnp.where(m1, x, 0.0)], axis=0)


def _unbd(m, c):
    return jnp.where(_pair_masks(c)[0], m[:c], m[c:])


def _pair_a(l2, r2):
    return _mm(l2, _bd(r2), tb=True)


def _pair_mul(p2, x2):
    return _mm(p2, _bd(x2))


def _pair_mul_t(p2, x2):
    return _unbd(_mm(p2, x2, ta=True), p2.shape[0])


def _block_diag_mask():
    row = lax.broadcasted_iota(jnp.int32, (PAIR, PAIR), 0)
    lane = lax.broadcasted_iota(jnp.int32, (PAIR, PAIR), 1)
    return (row < HEAD_DIM) == (lane < HEAD_DIM), row == lane


def _wkv_pair_common(r, lw, k, a, b):
    c = r[0].shape[0]
    pairs = range(len(r))
    i = lax.broadcasted_iota(jnp.int32, (c, PAIR), 0)
    j = lax.broadcasted_iota(jnp.int32, (c, PAIR), 1) % c
    strict, incl = i > j, i >= j
    ti = lax.broadcasted_iota(jnp.int32, (c, c), 0)
    tj = lax.broadcasted_iota(jnp.int32, (c, c), 1)
    tri = jnp.where(ti >= tj, 1.0, 0.0).astype(BF16)
    lc = [sum(_dg(tri, part, False, False) for part in _split(lw[p], 3)) for p in pairs]
    lend = [lc[p][c - 1:c, :] for p in pairs]
    rt = [r[p] * jnp.exp(lc[p]) for p in pairs]
    at = [a[p] * jnp.exp(lc[p] - lw[p]) for p in pairs]
    pinv = [jnp.exp(-lc[p]) for p in pairs]
    kt = [k[p] * pinv[p] for p in pairs]
    bt = [b[p] * pinv[p] for p in pairs]
    e = [jnp.exp(lend[p] - lc[p]) for p in pairs]
    ktp = [k[p] * e[p] for p in pairs]
    btp = [b[p] * e[p] for p in pairs]
    a_ab = [jnp.where(strict, _pair_a(at[p], bt[p]), 0.0) for p in pairs]
    a_ak = [jnp.where(strict, _pair_a(at[p], kt[p]), 0.0) for p in pairs]
    a_rb = [jnp.where(incl, _pair_a(rt[p], bt[p]), 0.0) for p in pairs]
    a_rk = [jnp.where(incl, _pair_a(rt[p], kt[p]), 0.0) for p in pairs]
    t = [jnp.where(i == j, 1.0, 0.0) + a_ab[p] for p in pairs]
    xp = a_ab
    n = 2
    while n < c:
        xp = [_pair_mul(xp[p], xp[p]) for p in pairs]
        t = [t[p] + _pair_mul(t[p], xp[p]) for p in pairs]
        n *= 2
    bdm, eye = _block_diag_mask()
    pend_col = [jnp.sum(jnp.where(eye, jnp.exp(lend[p]), 0.0), axis=1, keepdims=True) for p in pairs]
    return dict(rt=rt, at=at, kt=kt, bt=bt, ktp=ktp, btp=btp, a_ak=a_ak, a_rb=a_rb, a_rk=a_rk, t=t,
                pend_col=pend_col, lend=lend, lc=lc, strict=strict, incl=incl, tri=tri, bdm=bdm)


def _wkv_group(width):
    npair = width // PAIR
    g = min(WKV_PAIRS_PER_STEP, npair)
    assert npair % g == 0
    return npair, g


def _wkv_fwd_call(r, lw, k, v, a, b):
    tokens, width = r.shape
    c = WKV_CHUNK
    nc = tokens // c
    npair, g = _wkv_group(width)

    def body(r_ref, lw_ref, k_ref, v_ref, a_ref, b_ref, y_ref, s_ref, st):
        @pl.when(pl.program_id(1) == 0)
        def _():
            st[...] = jnp.zeros_like(st)

        pairs = range(g)
        rv, lwv, kv, vv, av, bv = ([ref[:, p * PAIR:(p + 1) * PAIR] for p in pairs]
                                   for ref in (r_ref, lw_ref, k_ref, v_ref, a_ref, b_ref))
        s0 = [st[p] for p in pairs]
        q = _wkv_pair_common(rv, lwv, kv, av, bv)
        w1 = [_mm(q["at"][p], s0[p]) + _pair_mul(q["a_ak"][p], vv[p]) for p in pairs]
        u = [_pair_mul(q["t"][p], w1[p]) for p in pairs]
        y = [_mm(q["rt"][p], s0[p]) + _pair_mul(q["a_rb"][p], u[p]) + _pair_mul(q["a_rk"][p], vv[p]) for p in pairs]
        grow = [_mm(jnp.concatenate([q["btp"][p], q["ktp"][p]], axis=0), jnp.concatenate([u[p], vv[p]], axis=0), ta=True)
                for p in pairs]
        for p in pairs:
            y_ref[:, p * PAIR:(p + 1) * PAIR] = y[p]
            s_ref[0, p] = s0[p]
            st[p] = q["pend_col"][p] * s0[p] + jnp.where(q["bdm"], grow[p], 0.0)

    tok = pl.BlockSpec((c, g * PAIR), lambda gi, ci: (ci, gi))
    return pl.pallas_call(
        body, name="wkv_fwd", grid=(npair // g, nc),
        in_specs=[tok] * 6,
        out_specs=[tok, pl.BlockSpec((1, g, PAIR, PAIR), lambda gi, ci: (ci, gi, 0, 0))],
        out_shape=[jax.ShapeDtypeStruct((tokens, width), F32), jax.ShapeDtypeStruct((nc, npair, PAIR, PAIR), F32)],
        scratch_shapes=[pltpu.VMEM((g, PAIR, PAIR), F32)],
        compiler_params=pltpu.CompilerParams(dimension_semantics=("parallel", "arbitrary")),
    )(r, lw, k, v, a, b)


def _wkv_bwd_call(r, lw, k, v, a, b, s, dy):
    tokens, width = r.shape
    c = WKV_CHUNK
    nc = tokens // c
    npair, g = _wkv_group(width)

    def body(r_ref, lw_ref, k_ref, v_ref, a_ref, b_ref, s_ref, dy_ref,
             dr_ref, dlw_ref, dk_ref, dv_ref, da_ref, db_ref, dst):
        @pl.when(pl.program_id(1) == 0)
        def _():
            dst[...] = jnp.zeros_like(dst)

        pairs = range(g)
        rv, lwv, kv, vv, av, bv, dyv = ([ref[:, p * PAIR:(p + 1) * PAIR] for p in pairs]
                                        for ref in (r_ref, lw_ref, k_ref, v_ref, a_ref, b_ref, dy_ref))
        s0 = [s_ref[0, p] for p in pairs]
        dsc = [dst[p] for p in pairs]
        q = _wkv_pair_common(rv, lwv, kv, av, bv)
        rt, at, kt, bt, ktp, btp, t = (q[n] for n in ("rt", "at", "kt", "bt", "ktp", "btp", "t"))
        a_ak, a_rb, a_rk, strict, incl = (q[n] for n in ("a_ak", "a_rb", "a_rk", "strict", "incl"))
        w1 = [_mm(at[p], s0[p]) + _pair_mul(a_ak[p], vv[p]) for p in pairs]
        u = [_pair_mul(t[p], w1[p]) for p in pairs]
        du = [_pair_mul_t(a_rb[p], dyv[p]) + _mm(btp[p], dsc[p]) for p in pairs]
        dw1 = [_pair_mul_t(t[p], du[p]) for p in pairs]
        dv = [_pair_mul_t(a_rk[p], dyv[p]) + _mm(ktp[p], dsc[p]) + _pair_mul_t(a_ak[p], dw1[p]) for p in pairs]
        da_ab = [jnp.where(strict, _pair_a(dw1[p], u[p]), 0.0) for p in pairs]
        da_ak = [jnp.where(strict, _pair_a(dw1[p], vv[p]), 0.0) for p in pairs]
        da_rb = [jnp.where(incl, _pair_a(dyv[p], u[p]), 0.0) for p in pairs]
        da_rk = [jnp.where(incl, _pair_a(dyv[p], vv[p]), 0.0) for p in pairs]
        d_rt = [_mm(dyv[p], s0[p], tb=True) + _pair_mul(da_rb[p], bt[p]) + _pair_mul(da_rk[p], kt[p]) for p in pairs]
        d_at = [_mm(dw1[p], s0[p], tb=True) + _pair_mul(da_ab[p], bt[p]) + _pair_mul(da_ak[p], kt[p]) for p in pairs]
        d_bt = [_pair_mul_t(da_ab[p], at[p]) + _pair_mul_t(da_rb[p], rt[p]) for p in pairs]
        d_kt = [_pair_mul_t(da_ak[p], at[p]) + _pair_mul_t(da_rk[p], rt[p]) for p in pairs]
        d_btp = [_mm(u[p], dsc[p], tb=True) for p in pairs]
        d_ktp = [_mm(vv[p], dsc[p], tb=True) for p in pairs]
        ones = jnp.ones((8, PAIR), BF16)
        dpend = [sum(_dg(ones, part, False, True) for part in _split(dsc[p] * s0[p], 3))[0:1, :] * jnp.exp(q["lend"][p])
                 for p in pairs]
        grow = [_mm(jnp.concatenate([rt[p], at[p]], axis=0), jnp.concatenate([dyv[p], dw1[p]], axis=0), ta=True)
                for p in pairs]
        last = lax.broadcasted_iota(jnp.int32, (c, PAIR), 0) == c - 1
        for p in pairs:
            sl = slice(p * PAIR, (p + 1) * PAIR)
            dst[p] = q["pend_col"][p] * dsc[p] + jnp.where(q["bdm"], grow[p], 0.0)
            lc_e = d_ktp[p] * ktp[p] + d_btp[p] * btp[p]
            dlend = jnp.sum(lc_e, axis=0, keepdims=True) + dpend[p]
            dlc = d_rt[p] * rt[p] - d_kt[p] * kt[p] - d_bt[p] * bt[p] - lc_e + jnp.where(last, dlend, 0.0)
            dlp = d_at[p] * at[p]
            dlw_ref[:, sl] = sum(_dg(q["tri"], part, True, False) for part in _split(dlc + dlp, 3)) - dlp
            lc = q["lc"][p]
            pinv = jnp.exp(-lc)
            e = jnp.exp(q["lend"][p] - lc)
            dr_ref[:, sl] = d_rt[p] * jnp.exp(lc)
            da_ref[:, sl] = d_at[p] * jnp.exp(lc - lwv[p])
            dk_ref[:, sl] = d_kt[p] * pinv + d_ktp[p] * e
            db_ref[:, sl] = d_bt[p] * pinv + d_btp[p] * e
            dv_ref[:, sl] = dv[p]

    tok = pl.BlockSpec((c, g * PAIR), lambda gi, ci: (nc - 1 - ci, gi))
    tshape = jax.ShapeDtypeStruct((tokens, width), F32)
    return pl.pallas_call(
        body, name="wkv_bwd", grid=(npair // g, nc),
        in_specs=[tok] * 6 + [pl.BlockSpec((1, g, PAIR, PAIR), lambda gi, ci: (nc - 1 - ci, gi, 0, 0)), tok],
        out_specs=[tok] * 6, out_shape=[tshape] * 6,
        scratch_shapes=[pltpu.VMEM((g, PAIR, PAIR), F32)],
        compiler_params=pltpu.CompilerParams(dimension_semantics=("parallel", "arbitrary")),
    )(r, lw, k, v, a, b, s, dy)


@jax.custom_vjp
def wkv7(r, lw, k, v, a, b):
    return _wkv_fwd_call(r, lw, k, v, a, b)[0]


def _wkv7_fwd(r, lw, k, v, a, b):
    y, s = _wkv_fwd_call(r, lw, k, v, a, b)
    return y, (r, lw, k, v, a, b, s)


wkv7.defvjp(_wkv7_fwd, lambda res, dy: tuple(_wkv_bwd_call(*res, dy)))


def _attn_block(tokens):
    return ATTN_BLOCK_BIG if tokens % ATTN_BLOCK_BIG == 0 else ATTN_BLOCK


def _fox_layouts(cum):
    tokens, heads = cum.shape
    t = _attn_block(tokens)
    cq = cum.reshape(tokens, heads // 2, 2).transpose(1, 0, 2)
    ck = cum.T.reshape(heads // 2, 2, tokens // t, t).transpose(0, 2, 1, 3)
    return cq, ck


def _head_lane_masks(rows):
    lane = lax.broadcasted_iota(jnp.int32, (rows, 2 * HEAD_DIM), 1)
    return [lane < HEAD_DIM, lane >= HEAD_DIM]


def _fox_fwd_call(q, k, v, cq, ck):
    tokens, width = q.shape
    t = _attn_block(tokens)
    nb = tokens // t
    hd = HEAD_DIM
    npair = width // (2 * hd)

    def body(q_ref, k_ref, v_ref, cq_ref, ck_ref, o_ref, lse_ref):
        i = pl.program_id(1)
        masks = _head_lane_masks(t)
        q2 = q_ref[...]
        qs = [jnp.where(mk, q2, 0.0).astype(BF16) for mk in masks]
        cqs = [cq_ref[0, :, hh:hh + 1] for hh in range(2)]

        def block(j, carry, diagonal):
            off = pl.multiple_of(j * t, t)
            ckj = ck_ref[0, j]
            k2 = k_ref[pl.ds(off, t), :].astype(BF16)
            v2 = v_ref[pl.ds(off, t), :].astype(BF16)
            out = []
            for hh in range(2):
                m, l, acc = carry[hh]
                s = _dg(qs[hh], k2, False, True) + (cqs[hh] - ckj[hh:hh + 1, :])
                if diagonal:
                    keep = lax.broadcasted_iota(jnp.int32, (t, t), 0) >= lax.broadcasted_iota(jnp.int32, (t, t), 1)
                    s = jnp.where(keep, s, NEG_BIG)
                m_new = jnp.maximum(m, jnp.max(s, axis=1, keepdims=True))
                alpha = jnp.exp(m - m_new)
                p = jnp.exp(s - m_new)
                l = alpha * l + jnp.sum(p, axis=1, keepdims=True)
                acc = alpha * acc + _dg(p.astype(BF16), v2, False, False)
                out.append((m_new, l, acc))
            return tuple(out)

        init = tuple((jnp.full((t, 1), NEG_BIG, F32), jnp.zeros((t, 1), F32), jnp.zeros((t, 2 * hd), F32)) for _ in range(2))
        res = lax.fori_loop(0, i, lambda j, c: block(j, c, False), init)
        res = block(i, res, True)
        o_ref[...] = jnp.where(masks[0], res[0][2] / res[0][1], res[1][2] / res[1][1])
        for hh in range(2):
            lse_ref[0, :, hh:hh + 1] = res[hh][0] + jnp.log(res[hh][1])

    blk = pl.BlockSpec((t, 2 * hd), lambda hp, i: (i, hp))
    full = pl.BlockSpec((tokens, 2 * hd), lambda hp, i: (0, hp))
    cq_spec = pl.BlockSpec((1, t, 2), lambda hp, i: (hp, i, 0))
    ck_spec = pl.BlockSpec((1, nb, 2, t), lambda hp, i: (hp, 0, 0, 0))
    return pl.pallas_call(
        body, name="fox_fwd", grid=(npair, nb),
        in_specs=[blk, full, full, cq_spec, ck_spec],
        out_specs=[blk, cq_spec],
        out_shape=[jax.ShapeDtypeStruct((tokens, width), F32), jax.ShapeDtypeStruct((npair, tokens, 2), F32)],
        compiler_params=pltpu.CompilerParams(dimension_semantics=("parallel", "arbitrary")),
    )(q, k, v, cq, ck)


def _fox_bwd_call(q, k, v, cq, ck, o, lse, do):
    tokens, width = q.shape
    t = _attn_block(tokens)
    nb = tokens // t
    hd = HEAD_DIM
    npair = width // (2 * hd)

    def body(q_ref, k_ref, v_ref, cq_ref, ck_ref, o_ref, lse_ref, do_ref, dq_ref, dk_ref, dv_ref, dck_ref, dcq_ref):
        i = pl.program_id(1)

        @pl.when(i == 0)
        def _():
            dk_ref[...] = jnp.zeros_like(dk_ref)
            dv_ref[...] = jnp.zeros_like(dv_ref)
            dck_ref[...] = jnp.zeros_like(dck_ref)

        masks = _head_lane_masks(t)
        q2, do2, o2 = q_ref[...], do_ref[...], o_ref[...]
        qs = [jnp.where(mk, q2, 0.0).astype(BF16) for mk in masks]
        dos = [jnp.where(mk, do2, 0.0).astype(BF16) for mk in masks]
        deltas = [jnp.sum(dos[hh].astype(F32) * o2, axis=1, keepdims=True) for hh in range(2)]
        bias = [cq_ref[0, :, hh:hh + 1] - lse_ref[0, :, hh:hh + 1] for hh in range(2)]

        def block(j, carry, diagonal):
            off = pl.multiple_of(j * t, t)
            ckj = ck_ref[0, j]
            k2 = k_ref[pl.ds(off, t), :].astype(BF16)
            v2 = v_ref[pl.ds(off, t), :].astype(BF16)
            out = []
            dk2 = jnp.zeros((t, 2 * hd), F32)
            dv2 = jnp.zeros((t, 2 * hd), F32)
            for hh in range(2):
                s = _dg(qs[hh], k2, False, True) + (bias[hh] - ckj[hh:hh + 1, :])
                if diagonal:
                    keep = lax.broadcasted_iota(jnp.int32, (t, t), 0) >= lax.broadcasted_iota(jnp.int32, (t, t), 1)
                    s = jnp.where(keep, s, NEG_BIG)
                p = jnp.exp(s)
                dp = _dg(dos[hh], v2, False, True)
                ds = p * (dp - deltas[hh])
                dsb = ds.astype(BF16)
                dq, rowsum = carry[hh]
                out.append((dq + _dg(dsb, k2, False, False), rowsum + jnp.sum(ds, axis=1, keepdims=True)))
                dk2 = dk2 + _dg(dsb, qs[hh], True, False)
                dv2 = dv2 + _dg(p.astype(BF16), dos[hh], True, False)
                dck_ref[0, j, hh:hh + 1, :] -= jnp.sum(ds, axis=0, keepdims=True)
            dk_ref[pl.ds(off, t), :] += dk2
            dv_ref[pl.ds(off, t), :] += dv2
            return tuple(out)

        init = tuple((jnp.zeros((t, 2 * hd), F32), jnp.zeros((t, 1), F32)) for _ in range(2))
        res = lax.fori_loop(0, i, lambda j, c: block(j, c, False), init)
        res = block(i, res, True)
        dq_ref[...] = jnp.where(masks[0], res[0][0], res[1][0])
        for hh in range(2):
            dcq_ref[0, :, hh:hh + 1] = res[hh][1]

    blk = pl.BlockSpec((t, 2 * hd), lambda hp, i: (i, hp))
    full = pl.BlockSpec((tokens, 2 * hd), lambda hp, i: (0, hp))
    cq_spec = pl.BlockSpec((1, t, 2), lambda hp, i: (hp, i, 0))
    ck_spec = pl.BlockSpec((1, nb, 2, t), lambda hp, i: (hp, 0, 0, 0))
    tshape = jax.ShapeDtypeStruct((tokens, width), F32)
    return pl.pallas_call(
        body, name="fox_bwd", grid=(npair, nb),
        in_specs=[blk, full, full, cq_spec, ck_spec, blk, cq_spec, blk],
        out_specs=[blk, full, full, ck_spec, cq_spec],
        out_shape=[tshape, tshape, tshape, jax.ShapeDtypeStruct((npair, nb, 2, t), F32),
                   jax.ShapeDtypeStruct((npair, tokens, 2), F32)],
        compiler_params=pltpu.CompilerParams(dimension_semantics=("parallel", "arbitrary")),
    )(q, k, v, cq, ck, o, lse, do)


@jax.custom_vjp
def fox_attention(q, k, v, cum):
    return _fox_fwd_call(q, k, v, *_fox_layouts(cum))[0]


def _fox_fwd(q, k, v, cum):
    cq, ck = _fox_layouts(cum)
    o, lse = _fox_fwd_call(q, k, v, cq, ck)
    return o, (q, k, v, cq, ck, o, lse)


def _fox_bwd(res, do):
    q, k, v, cq, ck, o, lse = res
    dq, dk, dv, dck, dcq = _fox_bwd_call(q, k, v, cq, ck, o, lse, do)
    npair, nb, _, t = dck.shape
    dcum = dck.transpose(0, 2, 1, 3).reshape(2 * npair, nb * t).T + dcq.transpose(1, 0, 2).reshape(nb * t, 2 * npair)
    return dq, dk, dv, dcum


fox_attention.defvjp(_fox_fwd, _fox_bwd)


def _loss_call(y, target):
    rows, d = y.shape
    tr = _row_tile(rows, d)

    def body(y_ref, t_ref, loss_ref, dy_ref):
        @pl.when(pl.program_id(0) == 0)
        def _():
            loss_ref[...] = jnp.zeros_like(loss_ref)

        diff = y_ref[...] - t_ref[...]
        dy_ref[...] = diff * (1.0 / d)
        loss_ref[...] += (0.5 / d) * jnp.sum(jnp.sum(diff * diff, axis=1, keepdims=True), axis=0, keepdims=True)

    return pl.pallas_call(
        body, name="loss", grid=(rows // tr,),
        in_specs=[pl.BlockSpec((tr, d), lambda i: (i, 0))] * 2,
        out_specs=[pl.BlockSpec((1, 1), lambda i: (0, 0)), pl.BlockSpec((tr, d), lambda i: (i, 0))],
        out_shape=[jax.ShapeDtypeStruct((1, 1), F32), jax.ShapeDtypeStruct((rows, d), F32)],
        compiler_params=pltpu.CompilerParams(dimension_semantics=("arbitrary",)),
    )(y, target)


def _adamw_call(w, g, m, v):
    rows, cols = w.shape
    tr = _row_tile_ragged(rows, cols, budget=1024 * 1024)
    c1 = 1.0 / (1.0 - ADAM_B1 ** ADAM_STEP)
    c2 = 1.0 / (1.0 - ADAM_B2 ** ADAM_STEP)

    def body(w_ref, g_ref, m_ref, v_ref, d_ref, nm_ref, nv_ref):
        gv = g_ref[...]
        nm = ADAM_B1 * m_ref[...] + (1.0 - ADAM_B1) * gv
        nv = ADAM_B2 * v_ref[...] + (1.0 - ADAM_B2) * (gv * gv)
        nm_ref[...] = nm
        nv_ref[...] = nv
        d_ref[...] = -ADAM_LR * ((nm * c1) / (jnp.sqrt(nv * c2) + ADAM_EPS) + ADAM_WD * w_ref[...])

    spec = pl.BlockSpec((tr, cols), lambda i: (i, 0))
    shape = jax.ShapeDtypeStruct((rows, cols), F32)
    return pl.pallas_call(
        body, name="adamw", grid=(pl.cdiv(rows, tr),),
        in_specs=[spec] * 4, out_specs=[spec] * 3, out_shape=[shape] * 3,
        compiler_params=pltpu.CompilerParams(dimension_semantics=("parallel",)),
    )(w, g, m, v)


def _my_place():
    return lax.axis_index("x"), lax.axis_index("y"), lax.axis_index("c")


def _place_index(px, py, pc):
    return 4 * px + 2 * py + pc


HBM_SPEC = pl.BlockSpec(memory_space=pltpu.HBM)


def _all_gather_call(block):
    def body(x_ref, out_ref, send_sems, recv_sems, local_sem):
        x, y, c = _my_place()
        me, sibling = (x, y, c), (x, y, 1 - c)
        chips = [(1 - x, y), (x, 1 - y), (1 - x, 1 - y)]

        def slot(px, py, pc):
            return out_ref.at[_place_index(px, py, pc)]

        def copy(k, blk, to, src=None):
            return pltpu.make_async_remote_copy(
                src_ref=slot(*blk) if src is None else src, dst_ref=slot(*blk),
                send_sem=send_sems.at[k], recv_sem=recv_sems.at[k],
                device_id=to, device_id_type=pl.DeviceIdType.MESH)

        mine = pltpu.make_async_copy(x_ref, slot(*me), local_sem)
        mine.start()
        first = [copy(0, me, sibling, src=x_ref)]
        first += [copy(1 + j, me, (*chip, c), src=x_ref) for j, chip in enumerate(chips)]
        for cp in first:
            cp.start()
        passed = [copy(4 + j, (*chip, c), sibling) for j, chip in enumerate(chips)]
        for j, chip in enumerate(chips):
            copy(1 + j, (*chip, c), me).wait_recv()
            passed[j].start()
        copy(0, sibling, me).wait_recv()
        for j, chip in enumerate(chips):
            copy(4 + j, (*chip, 1 - c), me).wait_recv()
        for cp in first + passed:
            cp.wait_send()
        mine.wait()

    return pl.pallas_call(
        body, name="all_gather",
        out_shape=jax.ShapeDtypeStruct((N_DEV,) + block.shape, block.dtype),
        in_specs=[HBM_SPEC], out_specs=HBM_SPEC,
        scratch_shapes=[pltpu.SemaphoreType.DMA((7,)), pltpu.SemaphoreType.DMA((7,)), pltpu.SemaphoreType.DMA],
    )(block)


SEM_SPEC = pl.BlockSpec(memory_space=pltpu.SEMAPHORE)
SIDE_EFFECT = pltpu.SideEffectType.DATAFLOW_SIDE_EFFECTING


def _peers():
    x, y, c = _my_place()
    out = []
    for k in range(1, N_DEV):
        peer = (x ^ (k >> 2), y ^ ((k >> 1) & 1), c ^ (k & 1))
        out.append((k - 1, peer, _place_index(*peer)))
    return _place_index(x, y, c), out


def _spread_start(src, per_peer, name, after=None):
    slot = src.shape[1:] if per_peer else src.shape
    order = () if after is None else (after,)

    def body(src_ref, land_ref, *rest):
        send_sems, recv_sems, src_thru, land_thru, token = rest[len(order):]
        mine, peers = _peers()
        for k, peer, peer_idx in peers:
            pltpu.make_async_remote_copy(
                src_ref=src_ref.at[peer_idx] if per_peer else src_ref, dst_ref=land_ref.at[mine],
                send_sem=send_sems.at[k], recv_sem=recv_sems.at[k],
                device_id=peer, device_id_type=pl.DeviceIdType.MESH).start()
        token[...] = jnp.zeros_like(token)

    return pl.pallas_call(
        body, name=name,
        out_shape=(pltpu.SemaphoreType.DMA((N_DEV - 1,)), pltpu.SemaphoreType.DMA((N_DEV - 1,)),
                   pltpu.HBM(src.shape, src.dtype), pltpu.HBM((N_DEV,) + slot, src.dtype),
                   jax.ShapeDtypeStruct((8, 128), F32)),
        in_specs=(HBM_SPEC, HBM_SPEC) + (pl.BlockSpec(memory_space=pl.ANY),) * len(order),
        out_specs=(SEM_SPEC, SEM_SPEC, HBM_SPEC, HBM_SPEC, pl.BlockSpec(memory_space=pltpu.VMEM)),
        input_output_aliases={0: 2, 1: 3},
        compiler_params=pltpu.CompilerParams(has_side_effects=SIDE_EFFECT),
    )(pltpu.with_memory_space_constraint(src, pltpu.HBM),
      pltpu.with_memory_space_constraint(lax.empty((N_DEV,) + slot, src.dtype), pltpu.HBM), *order)


def _spread_wait(handles, after, per_peer, name):
    send_sems, recv_sems, src_thru, land_thru = handles

    def body(src_ref, land_ref, send_sems, recv_sems, after_ref, src_dead, got_ref):
        _, peers = _peers()
        for k, peer, peer_idx in peers:
            copy = pltpu.make_async_remote_copy(
                src_ref=src_ref.at[peer_idx] if per_peer else src_ref, dst_ref=land_ref.at[peer_idx],
                send_sem=send_sems.at[k], recv_sem=recv_sems.at[k],
                device_id=peer, device_id_type=pl.DeviceIdType.MESH)
            copy.wait_send()
            copy.wait_recv()

    return pl.pallas_call(
        body, name=name,
        out_shape=(pltpu.HBM(src_thru.shape, src_thru.dtype), pltpu.HBM(land_thru.shape, land_thru.dtype)),
        in_specs=(HBM_SPEC, HBM_SPEC, SEM_SPEC, SEM_SPEC, pl.BlockSpec(memory_space=pl.ANY)),
        out_specs=(HBM_SPEC, HBM_SPEC), input_output_aliases={0: 0, 1: 1},
        compiler_params=pltpu.CompilerParams(has_side_effects=SIDE_EFFECT),
    )(src_thru, land_thru, send_sems, recv_sems, after)


def _sum_slots_call(slots):
    _, rows, cols = slots.shape
    tr = _row_tile_ragged(rows, cols, budget=512 * 1024)

    def body(s_ref, o_ref):
        acc = s_ref[0].astype(F32)
        for j in range(1, N_DEV):
            acc = acc + s_ref[j].astype(F32)
        o_ref[...] = acc

    return pl.pallas_call(
        body, name="sum_slots", grid=(pl.cdiv(rows, tr),),
        in_specs=[pl.BlockSpec((N_DEV, tr, cols), lambda i: (0, i, 0))],
        out_specs=pl.BlockSpec((tr, cols), lambda i: (i, 0)),
        out_shape=jax.ShapeDtypeStruct((rows, cols), F32),
        compiler_params=pltpu.CompilerParams(dimension_semantics=("parallel",)),
    )(slots)


def _sum_adamw_call(got, own, w, m, v):
    rows, cols = w.shape
    tr = _row_tile_ragged(rows, cols, budget=512 * 1024)
    c1 = 1.0 / (1.0 - ADAM_B1 ** ADAM_STEP)
    c2 = 1.0 / (1.0 - ADAM_B2 ** ADAM_STEP)

    def body(got_ref, own_ref, w_ref, m_ref, v_ref, g_ref, d_ref, nm_ref, nv_ref):
        mine = _place_index(*_my_place())
        gv = jnp.zeros(w_ref.shape, F32)
        for j in range(N_DEV):
            gv = gv + jnp.where(mine == j, own_ref[...], got_ref[j]).astype(F32)
        nm = ADAM_B1 * m_ref[...] + (1.0 - ADAM_B1) * gv
        nv = ADAM_B2 * v_ref[...] + (1.0 - ADAM_B2) * (gv * gv)
        g_ref[...] = gv
        nm_ref[...] = nm
        nv_ref[...] = nv
        d_ref[...] = -ADAM_LR * ((nm * c1) / (jnp.sqrt(nv * c2) + ADAM_EPS) + ADAM_WD * w_ref[...])

    spec = pl.BlockSpec((tr, cols), lambda i: (i, 0))
    shape = jax.ShapeDtypeStruct((rows, cols), F32)
    return pl.pallas_call(
        body, name="sum_adamw", grid=(pl.cdiv(rows, tr),),
        in_specs=[pl.BlockSpec((N_DEV, tr, cols), lambda i: (0, i, 0))] + [spec] * 4,
        out_specs=[spec] * 4, out_shape=[shape] * 4,
        compiler_params=pltpu.CompilerParams(dimension_semantics=("parallel",)),
    )(got, own, w, m, v)


def _with_own_slot(got, own, mine):
    return lax.dynamic_update_index_in_dim(got, own, mine, 0)


def _pack(vectors, width):
    flat = jnp.concatenate([v.reshape(-1) for v in vectors])
    return jnp.pad(flat, (0, width - flat.shape[0])).reshape(width // 128, 128)


def _unpack(packed, like):
    flat = packed.reshape(-1)
    out, at = [], 0
    for v in like:
        out.append(flat[at:at + v.size].reshape(v.shape))
        at += v.size
    return tuple(out)


def _sum_over_devices(grads):
    n = sum(v.size for v in grads)
    width = -(-n // 1024) * 1024
    return _unpack(_sum_slots_call(_all_gather_call(_pack(grads, width))), grads)


def _cols_from_slots(slots):
    n, rows, cols = slots.shape
    return slots.transpose(1, 0, 2).reshape(rows, n * cols)


def _rows_from_slots(slots):
    return slots.reshape(-1, slots.shape[2])


def _pad128(n):
    return -(-n // 128) * 128


def _pad_to_tiles(a, axis):
    n = a.shape[axis]
    pads = [(0, 0)] * a.ndim
    pads[axis] = (0, _pad128(n) - n)
    return jnp.pad(a, pads)


def _rwkv_group(take, zeros, rw, dl, al, gl):
    at = 3 * rw
    parts = take(0, at)
    for n in (dl, al, gl):
        parts += take(at, at + n)
        if _pad128(n) > n:
            parts.append(zeros(_pad128(n) - n))
        at += n
    return parts


def _in_proj_layout(slots, rw, fw, dl, al, gl, whole):
    n_slots, rows, d = slots.shape
    wt = slots.reshape(n_slots * rows, d)
    take = lambda lo, hi: [wt[lo:hi]]
    zeros = lambda n: jnp.zeros((n, d), wt.dtype)
    rcols = 3 * rw + dl + al + gl
    fcols = 3 * fw + fw // HEAD_DIM
    group_r = _rwkv_group(take, zeros, rw, dl, al, gl)
    group_f = take(rcols, rcols + fcols) + ([zeros(_pad128(fcols) - fcols)] if _pad128(fcols) > fcols else [])
    group_g = take(rcols + fcols, n_slots * rows)
    if whole:
        return jnp.concatenate(group_r + group_f + group_g, axis=0)
    return tuple(jnp.concatenate(g, axis=0) for g in (group_r, group_f, group_g))


def _low_rank_layout(slots):
    return _pad_to_tiles(_cols_from_slots(slots), 0)


def _stage_embed(meta, x, n1, lp):
    h0 = jnp.concatenate([meta, x, jnp.zeros((lp - meta.shape[0] - x.shape[0], x.shape[1]), F32)], axis=0)
    return h0, rmsnorm(h0, n1)


def _stage_mix(z_r, z_f, small, w2, a2, g2, dims):
    (mu, w0, a0, k_k, k_a, r_k, gn_w, gn_b, q_g, k_g, f_bias) = small
    rw, fw, dl, al, gl = dims
    fcols = 3 * fw + fw // HEAD_DIM

    mu_group = jnp.concatenate(_rwkv_group(lambda lo, hi: [mu[:, lo:hi]], lambda n: jnp.zeros((1, n), F32), rw, dl, al, gl), axis=1)
    r, lw, kf, v, na, b, g = rwkv_prep(z_r, mu_group, w0, a0, k_k, k_a, w2, a2, g2)
    y = wkv7(r, lw, kf, v, na, b)
    y_a = gn_bonus(y, r, kf, v, g, gn_w, gn_b, r_k.reshape(1, rw))

    fq, fk, fv, fl = z_f[:, :fw], z_f[:, fw:2 * fw], z_f[:, 2 * fw:3 * fw], z_f[:, 3 * fw:fcols]
    fq = head_rms(fq, jnp.tile(q_g, (1, fw // HEAD_DIM))) * (HEAD_DIM ** -0.5)
    fk = head_rms(fk, jnp.tile(k_g, (1, fw // HEAD_DIM)))
    cum = jnp.cumsum(jax.nn.log_sigmoid(badd(fl, f_bias)), axis=0)
    y_b = fox_attention(fq, fk, fv, cum)
    return y_a, y_b


def _stage_merge(h0, y_a, y_b, z_g, w_a, w_b, w_o):
    merged = gated_merge(z_g, dense_cols_bf16(y_a, w_a), dense_cols_bf16(y_b, w_b))
    return h0 + dense(merged, w_o)


def _stage_ffn(h1, n2, w_gu, w_dn):
    return h1 + dense(swiglu(dense_cols_bf16(rmsnorm(h1, n2), w_gu)), w_dn)


SHARDED = ("meta_tokens", "w_in", "rwkv_w2", "rwkv_a2", "rwkv_g2", "w_branch_a", "w_branch_b", "w_o", "w_gate_up", "w_down")
SMALL = ("norm1_g", "rwkv_mu", "rwkv_w0", "rwkv_a0", "rwkv_k_k", "rwkv_k_a", "rwkv_r_k", "rwkv_gn_w", "rwkv_gn_b",
         "fox_q_norm_g", "fox_k_norm_g", "fox_f_bias", "norm2_g")
WEIGHTS = ("meta_tokens", "norm1_g", "w_in", "rwkv_mu", "rwkv_w0", "rwkv_w2", "rwkv_a0", "rwkv_a2", "rwkv_g2", "rwkv_k_k",
           "rwkv_k_a", "rwkv_r_k", "rwkv_gn_w", "rwkv_gn_b", "fox_q_norm_g", "fox_k_norm_g", "fox_f_bias", "w_branch_a",
           "w_branch_b", "w_o", "norm2_g", "w_gate_up", "w_down")


def _as2d(a):
    return a.reshape(-1, a.shape[-1])


def kernel(x, meta_tokens, norm1_g, w_in, rwkv_mu, rwkv_w0, rwkv_w2, rwkv_a0, rwkv_a2, rwkv_g2, rwkv_k_k, rwkv_k_a, rwkv_r_k, rwkv_gn_w, rwkv_gn_b, fox_q_norm_g, fox_k_norm_g, fox_f_bias, w_branch_a, w_branch_b, w_o, norm2_g, w_gate_up, w_down, loss_target, m_meta_tokens, m_norm1_g, m_w_in, m_rwkv_mu, m_rwkv_w0, m_rwkv_w2, m_rwkv_a0, m_rwkv_a2, m_rwkv_g2, m_rwkv_k_k, m_rwkv_k_a, m_rwkv_r_k, m_rwkv_gn_w, m_rwkv_gn_b, m_fox_q_norm_g, m_fox_k_norm_g, m_fox_f_bias, m_w_branch_a, m_w_branch_b, m_w_o, m_norm2_g, m_w_gate_up, m_w_down, v_meta_tokens, v_norm1_g, v_w_in, v_rwkv_mu, v_rwkv_w0, v_rwkv_w2, v_rwkv_a0, v_rwkv_a2, v_rwkv_g2, v_rwkv_k_k, v_rwkv_k_a, v_rwkv_r_k, v_rwkv_gn_w, v_rwkv_gn_b, v_fox_q_norm_g, v_fox_k_norm_g, v_fox_f_bias, v_w_branch_a, v_w_branch_b, v_w_o, v_norm2_g, v_w_gate_up, v_w_down):
    given = dict(locals())
    w = {n: given[n] for n in WEIGHTS}
    assert rwkv_r_k.shape[-1] == HEAD_DIM
    n_meta, seq = meta_tokens.shape[0], x.shape[1]
    tokens = n_meta + seq
    lp = -(-tokens // TOKEN_TILE) * TOKEN_TILE
    mine = _place_index(*(lax.axis_index(a) for a in MESH_AXES))
    x2 = x[0]

    local = {n: _as2d(given[n]) for n in given if n != "x" and n != "loss_target"}
    for n in ("w_in", "m_w_in", "v_w_in"):
        local[n] = jnp.transpose(given[n][0])
    blocks = {n: local[n].astype(F32 if n == "meta_tokens" else BF16) for n in SHARDED}
    first = ("meta_tokens", "rwkv_w2", "rwkv_a2", "rwkv_g2")
    started = {n: _spread_start(blocks[n], False, "gather_start_" + n) for n in first}
    zero = sum(started[n][4][0, 0] for n in first)

    def gathered(n, after):
        own, got = _spread_wait(started[n][:4], after, False, "gather_wait_" + n)
        return _with_own_slot(got, own, mine)

    sm = {n: _as2d(w[n]) for n in SMALL}
    small_mix = tuple(sm[n] for n in SMALL[1:-1])
    n1 = sm["norm1_g"] + zero
    rw, fw = w_branch_a.shape[-2], w_branch_b.shape[-2]
    dims = (rw, fw, rwkv_w2.shape[-2], rwkv_a2.shape[-2], rwkv_g2.shape[-2])
    same = lambda s: (s,)

    meta, un_meta = jax.vjp(_cols_from_slots, gathered("meta_tokens", x2))
    (h0, xn), vjp_embed = jax.vjp(lambda m, xs, g: _stage_embed(m, xs, g, lp), meta, x2, n1)
    in_slots = _all_gather_call(blocks["w_in"])
    later = [n for n in SHARDED if n not in first and n != "w_in"]
    started.update({n: _spread_start(blocks[n], False, "gather_start_" + n, after=in_slots) for n in later})
    w_groups = _in_proj_layout(in_slots, *dims, whole=False)
    w_cat, un_in = jax.vjp(lambda s: _in_proj_layout(s, *dims, whole=True), in_slots)
    xn_b = xn.astype(BF16)
    behind = sum(started[n][4] for n in later)
    z_r, z_f, z_g = (_matmul(xn_b, wg, tb=True, name="in_proj_" + tag, after=behind) for wg, tag in zip(w_groups, "rfg"))
    (w2, un_w2), (a2, un_a2), (g2, un_g2) = (jax.vjp(_low_rank_layout, gathered(n, xn)) for n in ("rwkv_w2", "rwkv_a2", "rwkv_g2"))
    (y_a, y_b), vjp_mix = jax.vjp(lambda zr, zf, s, a, b, c: _stage_mix(zr, zf, s, a, b, c, dims),
                                  z_r, z_f, small_mix, w2, a2, g2)
    w_a, w_b = gathered("w_branch_a", y_a), gathered("w_branch_b", y_a)
    w_o_full, un_wo = jax.vjp(_rows_from_slots, gathered("w_o", y_a))
    h1, vjp_merge = jax.vjp(_stage_merge, h0, y_a, y_b, z_g, w_a, w_b, w_o_full)
    w_gu = gathered("w_gate_up", h1)
    w_dn, un_dn = jax.vjp(_rows_from_slots, gathered("w_down", h1))
    y, vjp_ffn = jax.vjp(_stage_ffn, h1, sm["norm2_g"], w_gu, w_dn)

    loss_part, dy_real = _loss_call(y[n_meta:tokens], loss_target[0])
    dy = jnp.pad(dy_real, ((n_meta, lp - tokens), (0, 0)))
    loss = lax.psum(loss_part[0, 0], MESH_AXES)

    sent = {}

    def send_grad(n, dmat, unlayout):
        sent[n] = _spread_start(unlayout(dmat)[0], True, "grad_start_" + n)
        return sent[n][4][0, 0]

    d_h1, d_n2, d_wgu, d_wdn = vjp_ffn(dy)
    behind = send_grad("w_gate_up", d_wgu, same) + send_grad("w_down", d_wdn, un_dn)
    d_h0, d_ya, d_yb, d_zg, d_wa, d_wb, d_wo = vjp_merge(d_h1 + behind)
    behind = send_grad("w_o", d_wo, un_wo) + send_grad("w_branch_a", d_wa, same) + send_grad("w_branch_b", d_wb, same)
    d_zr, d_zf, d_small_mix, d_w2, d_a2, d_g2 = vjp_mix((d_ya + behind, d_yb))
    dproj_b = jnp.concatenate([d_zr.astype(BF16), d_zf.astype(BF16), d_zg.astype(BF16)], axis=1)
    d_wcat = _matmul(dproj_b, xn_b, ta=True, out_dtype=BF16, name="in_proj_dw")
    send_grad("w_in", d_wcat, un_in)
    d_xn = _matmul(dproj_b, w_cat, out_dtype=F32, name="in_proj_dx", after=sent["w_in"][4])
    send_grad("rwkv_w2", d_w2, un_w2)
    send_grad("rwkv_a2", d_a2, un_a2)
    send_grad("rwkv_g2", d_g2, un_g2)
    d_meta, g_x, d_n1 = vjp_embed((d_h0, d_xn))
    send_grad("meta_tokens", d_meta, un_meta)

    grads = dict(zip(SMALL, _sum_over_devices((d_n1, *d_small_mix, d_n2))))
    grads = {n: g.reshape(w[n].shape) for n, g in grads.items()}

    delta, new_m, new_v = {}, {}, {}
    after = g_x
    for n in ("w_gate_up", "w_down", "w_o", "w_branch_a", "w_branch_b", "rwkv_g2", "rwkv_a2", "rwkv_w2", "meta_tokens", "w_in"):
        src, got = _spread_wait(sent[n][:4], after, True, "grad_wait_" + n)
        own = lax.dynamic_index_in_dim(src, mine, 0, keepdims=False)
        g, d_, m_, v_ = _sum_adamw_call(got, own, local[n], local["m_" + n], local["v_" + n])
        back = (lambda t: jnp.transpose(t)[None]) if n == "w_in" else (lambda t: t.reshape(w[n].shape))
        grads[n], delta[n], new_m[n], new_v[n] = (back(t) for t in (g, d_, m_, v_))
        after = m_
    n_small = sum(w[n].size for n in SMALL)
    width = -(-n_small // 1024) * 1024
    packs = [_pack([src[n] if p == "" else given[p + n] for n in SMALL], width)
             for p, src in (("", w), ("", grads), ("m_", None), ("v_", None))]
    like = [w[n] for n in SMALL]
    for out, packed in zip((delta, new_m, new_v), _adamw_call(*packs)):
        out.update(dict(zip(SMALL, _unpack(packed, like))))

    return (loss, g_x[None], *[grads[n] for n in WEIGHTS], *[delta[n] for n in WEIGHTS],
            *[new_m[n] for n in WEIGHTS], *[new_v[n] for n in WEIGHTS])
```

```python
import functools

import jax
import jax.numpy as jnp
from jax import lax
from jax.experimental import pallas as pl
from jax.experimental.pallas import tpu as pltpu

F32 = jnp.float32
BF16 = jnp.bfloat16

N_DEV = 8
MESH_AXES = ("x", "y", "c")
HEAD_DIM = 64
TOKEN_TILE = 128
WKV_CHUNK = 64
WKV_PAIRS_PER_STEP = 8
PAIR = 2 * HEAD_DIM
ATTN_BLOCK = 128
ATTN_BLOCK_BIG = 384
RMS_EPS = 1e-6
GN_EPS = 64e-5
L2_FLOOR = 1e-12
NEG_BIG = -1e30
ADAM_LR, ADAM_B1, ADAM_B2, ADAM_EPS, ADAM_WD, ADAM_STEP = 0.001, 0.9, 0.999, 1e-08, 0.01, 10
VMEM_BYTES_V7X = 64 * 1024 * 1024
VMEM_LIMIT_CAP = 56 * 1024 * 1024
VMEM_LIMIT_FLOOR = 32 * 1024 * 1024
MATMUL_VMEM_BUDGET = 36 * 1024 * 1024
GRID_STEP_BYTES = 1024 * 1024
ACC_BYTES_PER_HBM_BYTE = 6


def _vmem_limit(estimate_bytes):
    return int(min(max(estimate_bytes * 5 // 4, VMEM_LIMIT_FLOOR), VMEM_LIMIT_CAP))


def _pick(dim, cands):
    for c in cands:
        if dim % c == 0:
            return c
    return dim


def _row_tile(rows, width, itemsize=4, budget=2 * 1024 * 1024):
    for c in (1408, 1024, 704, 512, 384, 256, 128, 64, 32, 16, 8):
        if rows % c == 0 and c * width * itemsize <= budget:
            return c
    return rows


def _row_tile_ragged(rows, width, itemsize=4, budget=2 * 1024 * 1024):
    tile = _row_tile(rows, width, itemsize, budget)
    if tile * width * itemsize <= budget or rows < 16:
        return tile
    padded = -(-rows // 16) * 16
    for c in (1408, 1024, 704, 512, 384, 336, 256, 192, 128, 96, 64, 48, 32, 16):
        if padded % c == 0 and c * width * itemsize <= budget:
            return c
    return tile


def _dg(a, b, ta, tb):
    dims = (((0 if ta else 1,), (1 if tb else 0,)), ((), ()))
    return lax.dot_general(a, b, dims, preferred_element_type=F32)


def _split(x, n):
    parts = []
    for _ in range(n):
        h = x.astype(BF16)
        parts.append(h)
        x = x - h.astype(F32)
    return parts


def _mm(a, b, ta=False, tb=False):
    return _dg(a.astype(BF16), b.astype(BF16), ta, tb)


def _matmul(a, b, ta=False, tb=False, out_dtype=F32, name="matmul", after=None, b_slots=False, out_slots=0):
    if ta:
        kdim, m = a.shape
    else:
        m, kdim = a.shape
    if b_slots:
        n_slots, brows, bcols = b.shape
        n, k2 = (brows, n_slots * bcols) if tb else (n_slots * bcols, brows)
    elif tb:
        n, k2 = b.shape
    else:
        k2, n = b.shape
    assert kdim == k2, (a.shape, b.shape, ta, tb)
    sa, sb, so = a.dtype.itemsize, b.dtype.itemsize, jnp.dtype(out_dtype).itemsize
    n_unit = bcols if (b_slots and not tb) else (n // out_slots if out_slots else n)
    k_unit = bcols if (b_slots and tb) else kdim
    tm, tn, tk, n_outer = _matmul_tiles(m, n, kdim, ta, sa, sb, so, n_unit, k_unit)
    nk = kdim // tk
    ij = (lambda f: lambda j, i, k: f(i, j, k)) if n_outer else (lambda f: f)

    order = () if after is None else (after,)

    def body(a_ref, b_ref, *rest):
        o_ref, acc = rest[len(order)], rest[len(order) + 1:]
        part = _dg(a_ref[...].astype(BF16), b_ref[...].astype(BF16), ta, tb)
        if nk == 1:
            o_ref[...] = part.astype(o_ref.dtype)
            return
        kk = pl.program_id(2)

        @pl.when(kk == 0)
        def _():
            acc[0][...] = part

        @pl.when(kk > 0)
        def _():
            acc[0][...] += part

        @pl.when(kk == nk - 1)
        def _():
            o_ref[...] = acc[0][...].astype(o_ref.dtype)

    a_spec = pl.BlockSpec((tk, tm), ij(lambda i, j, k: (k, i))) if ta else pl.BlockSpec((tm, tk), ij(lambda i, j, k: (i, k)))
    if b_slots and tb:
        per = bcols // tk
        b_spec = pl.BlockSpec((None, tn, tk), ij(lambda i, j, k: (k // per, j, k % per)))
    elif b_slots:
        per = bcols // tn
        b_spec = pl.BlockSpec((None, tk, tn), ij(lambda i, j, k: (j // per, k, j % per)))
    elif tb:
        b_spec = pl.BlockSpec((tn, tk), ij(lambda i, j, k: (j, k)))
    else:
        b_spec = pl.BlockSpec((tk, tn), ij(lambda i, j, k: (k, j)))
    if out_slots:
        per_out = n // out_slots // tn
        out_spec = pl.BlockSpec((None, tm, tn), ij(lambda i, j, k: (j // per_out, i, j % per_out)))
        out_shape = jax.ShapeDtypeStruct((out_slots, m, n // out_slots), out_dtype)
    else:
        out_spec = pl.BlockSpec((tm, tn), ij(lambda i, j, k: (i, j)))
        out_shape = jax.ShapeDtypeStruct((m, n), out_dtype)
    return pl.pallas_call(
        body, name=name,
        grid=(n // tn, m // tm, nk) if n_outer else (m // tm, n // tn, nk),
        in_specs=[a_spec, b_spec] + [pl.BlockSpec(memory_space=pl.ANY)] * len(order),
        out_specs=out_spec,
        out_shape=out_shape,
        scratch_shapes=[pltpu.VMEM((tm, tn), F32)] if nk > 1 else [],
        compiler_params=pltpu.CompilerParams(dimension_semantics=("parallel", "parallel", "arbitrary"),
                                             vmem_limit_bytes=_vmem_limit(_matmul_vmem(tm, tn, tk, nk, sa, sb, so))),
    )(a, b, *order)


def _matmul_vmem(tm, tn, tk, nk, sa, sb, so):
    return 2 * (tm * tk * sa + tk * tn * sb + tm * tn * so) + tm * tn * 4 + (tm * tn * 4 if nk > 1 else 0)


def _matmul_tiles(m, n, kdim, ta, sa, sb, so, n_unit, k_unit):
    lane = (2816, 2176, 2048, 1408, 1024, 640, 512, 384, 256, 128)
    sublane = (2816, 2176, 2048, 1408, 1024, 704, 512, 384, 256, 128)
    divs = lambda dim, cands: [c for c in cands if dim % c == 0] or [dim]
    best = None
    for tm in divs(m, lane if ta else sublane):
        for tn in divs(n_unit, lane):
            for tk in divs(k_unit, sublane if ta else lane) + ([kdim] if k_unit == kdim and (ta or kdim <= 2048) else []):
                nk, nm, nn = kdim // tk, m // tm, n // tn
                if _matmul_vmem(tm, tn, tk, nk, sa, sb, so) > MATMUL_VMEM_BUDGET:
                    continue
                acc_bytes = m * n * 4 * 3 * nk // ACC_BYTES_PER_HBM_BYTE if nk > 1 else 0
                fixed = m * n * so + acc_bytes + nm * nn * nk * GRID_STEP_BYTES
                for n_outer in (False, True):
                    if n_outer:
                        a_reads, b_reads = (1 if (nk == 1 and nm == 1) else nn), (1 if nk == 1 else nm)
                    else:
                        a_reads, b_reads = (1 if nk == 1 else nn), (1 if (nk == 1 and nn == 1) else nm)
                    cost = m * kdim * sa * a_reads + kdim * n * sb * b_reads + fixed
                    if best is None or cost < best[0]:
                        best = (cost, tm, tn, tk, n_outer)
    return best[1:]


@jax.custom_vjp
def dense(x, w):
    return _matmul(x.astype(BF16), w, name="dense_fwd")


def _dense_fwd(x, w):
    return _matmul(x.astype(BF16), w, name="dense_fwd"), (x.astype(BF16), w, jnp.zeros((), x.dtype))


def _dense_bwd(res, dy):
    xb, w, like = res
    dyb = dy.astype(BF16)
    dx = _matmul(dyb, w, tb=True, out_dtype=like.dtype, name="dense_dx")
    dw = _matmul(xb, dyb, ta=True, out_dtype=w.dtype, name="dense_dw")
    return dx, dw


dense.defvjp(_dense_fwd, _dense_bwd)


def _make_dense_cols(out_dtype):
    @jax.custom_vjp
    def op(x, w_slots):
        return _matmul(x.astype(BF16), w_slots, b_slots=True, out_dtype=out_dtype, name="dense_cols_fwd")

    def fwd(x, w_slots):
        assert x.dtype == F32
        xb = x.astype(BF16)
        return _matmul(xb, w_slots, b_slots=True, out_dtype=out_dtype, name="dense_cols_fwd"), (xb, w_slots)

    def bwd(res, dy):
        xb, w_slots = res
        dyb = dy.astype(BF16)
        dx = _matmul(dyb, w_slots, tb=True, b_slots=True, out_dtype=F32, name="dense_cols_dx")
        dw = _matmul(xb, dyb, ta=True, out_slots=w_slots.shape[0], out_dtype=w_slots.dtype, name="dense_cols_dw")
        return dx, dw

    op.defvjp(fwd, bwd)
    return op


dense_cols = _make_dense_cols(F32)
dense_cols_bf16 = _make_dense_cols(BF16)


def _swiglu_call(gu, d_act=None):
    rows, two_f = gu.shape
    f = two_f // 2
    tr = _row_tile(rows, two_f, itemsize=2, budget=3 * 1024 * 1024)
    half = lambda j: pl.BlockSpec((tr, f), lambda i, j=j: (i, j))
    ops = (gu, gu) if d_act is None else (gu, gu, d_act)

    def body(*refs):
        g, u = refs[0][...].astype(F32), refs[1][...].astype(F32)
        s = 1.0 / (1.0 + jnp.exp(-g))
        if d_act is None:
            refs[2][...] = (g * s * u).astype(BF16)
        else:
            d = refs[2][...].astype(F32)
            refs[3][:, :f] = (d * u * s * (1.0 + g * (1.0 - s))).astype(BF16)
            refs[3][:, f:] = (d * g * s).astype(BF16)

    width = f if d_act is None else two_f
    return pl.pallas_call(
        body, name="swiglu_fwd" if d_act is None else "swiglu_bwd", grid=(rows // tr,),
        in_specs=[half(0), half(1)] + ([half(0)] if d_act is not None else []),
        out_specs=pl.BlockSpec((tr, width), lambda i: (i, 0)),
        out_shape=jax.ShapeDtypeStruct((rows, width), BF16),
        compiler_params=pltpu.CompilerParams(dimension_semantics=("parallel",)),
    )(*ops)


@jax.custom_vjp
def swiglu(gu):
    return _swiglu_call(gu)


swiglu.defvjp(lambda gu: (_swiglu_call(gu), gu), lambda gu, d_act: (_swiglu_call(gu, d_act),))


def _merge_call(zg, a, b, dm=None):
    rows, d = a.shape
    tr = _row_tile(rows, d, budget=1024 * 1024)
    half = lambda j: pl.BlockSpec((tr, d), lambda i, j=j: (i, j))
    tile = half(0)

    def body(*refs):
        ga = 1.0 / (1.0 + jnp.exp(-refs[0][...]))
        gb = 1.0 / (1.0 + jnp.exp(-refs[1][...]))
        av, bv = refs[2][...].astype(F32), refs[3][...].astype(F32)
        if dm is None:
            refs[4][...] = (ga * av + gb * bv).astype(BF16)
        else:
            dv = refs[4][...].astype(F32)
            dzg_ref, da_ref, db_ref = refs[5:]
            dzg_ref[:, :d] = dv * av * ga * (1.0 - ga)
            dzg_ref[:, d:] = dv * bv * gb * (1.0 - gb)
            da_ref[...] = (dv * ga).astype(BF16)
            db_ref[...] = (dv * gb).astype(BF16)

    shape_b = jax.ShapeDtypeStruct((rows, d), BF16)
    if dm is None:
        out_specs, out_shape, ops = tile, shape_b, (zg, zg, a, b)
    else:
        out_specs = [pl.BlockSpec((tr, 2 * d), lambda i: (i, 0)), tile, tile]
        out_shape = [jax.ShapeDtypeStruct((rows, 2 * d), F32), shape_b, shape_b]
        ops = (zg, zg, a, b, dm)
    return pl.pallas_call(
        body, name="merge_fwd" if dm is None else "merge_bwd", grid=(rows // tr,),
        in_specs=[half(0), half(1)] + [tile] * (len(ops) - 2),
        out_specs=out_specs, out_shape=out_shape,
        compiler_params=pltpu.CompilerParams(dimension_semantics=("parallel",)),
    )(*ops)


@jax.custom_vjp
def gated_merge(zg, a, b):
    return _merge_call(zg, a, b)


gated_merge.defvjp(lambda zg, a, b: (_merge_call(zg, a, b), (zg, a, b)),
                   lambda res, dm: tuple(_merge_call(*res, dm)))


def _rms_fwd_call(x, g):
    rows, d = x.shape
    tr = _row_tile(rows, d)

    def body(x_ref, g_ref, y_ref):
        xv = x_ref[...]
        rstd = lax.rsqrt(jnp.mean(xv * xv, axis=1, keepdims=True) + RMS_EPS)
        y_ref[...] = (xv * rstd) * g_ref[...]

    return pl.pallas_call(
        body, name="rms_fwd", grid=(rows // tr,),
        in_specs=[pl.BlockSpec((tr, d), lambda i: (i, 0)), pl.BlockSpec((1, d), lambda i: (0, 0))],
        out_specs=pl.BlockSpec((tr, d), lambda i: (i, 0)),
        out_shape=jax.ShapeDtypeStruct((rows, d), F32),
        compiler_params=pltpu.CompilerParams(dimension_semantics=("parallel",)),
    )(x, g)


def _rms_bwd_call(x, g, dy):
    rows, d = x.shape
    tr = _row_tile(rows, d)

    def body(x_ref, g_ref, dy_ref, dx_ref, dg_ref):
        @pl.when(pl.program_id(0) == 0)
        def _():
            dg_ref[...] = jnp.zeros_like(dg_ref)

        xv = x_ref[...]
        dyv = dy_ref[...]
        rstd = lax.rsqrt(jnp.mean(xv * xv, axis=1, keepdims=True) + RMS_EPS)
        xhat = xv * rstd
        dxhat = dyv * g_ref[...]
        dx_ref[...] = rstd * (dxhat - xhat * jnp.mean(dxhat * xhat, axis=1, keepdims=True))
        dg_ref[...] += jnp.sum(dyv * xhat, axis=0, keepdims=True)

    return pl.pallas_call(
        body, name="rms_bwd", grid=(rows // tr,),
        in_specs=[pl.BlockSpec((tr, d), lambda i: (i, 0)), pl.BlockSpec((1, d), lambda i: (0, 0)),
                  pl.BlockSpec((tr, d), lambda i: (i, 0))],
        out_specs=[pl.BlockSpec((tr, d), lambda i: (i, 0)), pl.BlockSpec((1, d), lambda i: (0, 0))],
        out_shape=[jax.ShapeDtypeStruct((rows, d), F32), jax.ShapeDtypeStruct((1, d), F32)],
        compiler_params=pltpu.CompilerParams(dimension_semantics=("arbitrary",)),
    )(x, g, dy)


@jax.custom_vjp
def rmsnorm(x, g):
    return _rms_fwd_call(x, g)


rmsnorm.defvjp(lambda x, g: (_rms_fwd_call(x, g), (x, g)), lambda res, dy: tuple(_rms_bwd_call(res[0], res[1], dy)))


def _bcast_call(x, p, mul):
    rows, d = x.shape
    tr = _row_tile(rows, d)

    def body(x_ref, p_ref, y_ref):
        y_ref[...] = x_ref[...] * p_ref[...] if mul else x_ref[...] + p_ref[...]

    return pl.pallas_call(
        body, name="bcast_mul" if mul else "bcast_add", grid=(rows // tr,),
        in_specs=[pl.BlockSpec((tr, d), lambda i: (i, 0)), pl.BlockSpec((1, d), lambda i: (0, 0))],
        out_specs=pl.BlockSpec((tr, d), lambda i: (i, 0)),
        out_shape=jax.ShapeDtypeStruct((rows, d), F32),
        compiler_params=pltpu.CompilerParams(dimension_semantics=("parallel",)),
    )(x, p)


def _colsum_call(a, b=None):
    rows, d = a.shape
    tr = _row_tile(rows, d)
    ops = (a,) if b is None else (a, b)

    def body(*refs):
        o_ref = refs[-1]

        @pl.when(pl.program_id(0) == 0)
        def _():
            o_ref[...] = jnp.zeros_like(o_ref)

        v = refs[0][...] if b is None else refs[0][...] * refs[1][...]
        o_ref[...] += jnp.sum(v, axis=0, keepdims=True)

    return pl.pallas_call(
        body, name="colsum", grid=(rows // tr,),
        in_specs=[pl.BlockSpec((tr, d), lambda i: (i, 0))] * len(ops),
        out_specs=pl.BlockSpec((1, d), lambda i: (0, 0)),
        out_shape=jax.ShapeDtypeStruct((1, d), F32),
        compiler_params=pltpu.CompilerParams(dimension_semantics=("arbitrary",)),
    )(*ops)


@jax.custom_vjp
def badd(x, p):
    return _bcast_call(x, p, False)


badd.defvjp(lambda x, p: (_bcast_call(x, p, False), None), lambda res, dy: (dy, _colsum_call(dy)))


def _head_sums(x):
    i = lax.broadcasted_iota(jnp.int32, (PAIR, PAIR), 0) // HEAD_DIM
    j = lax.broadcasted_iota(jnp.int32, (PAIR, PAIR), 1) // HEAD_DIM
    ones = jnp.where(i == j, 1.0, 0.0).astype(BF16)
    hi, lo = _split(x, 2)
    cols = [slice(p * PAIR, (p + 1) * PAIR) for p in range(x.shape[1] // PAIR)]
    return jnp.concatenate([_dg(hi[:, c], ones, False, False) + _dg(lo[:, c], ones, False, False) for c in cols], axis=1)


def _head_rms_fwd_call(x, g):
    rows, w = x.shape
    tr = _row_tile(rows, w, budget=1024 * 1024)

    def body(x_ref, g_ref, y_ref):
        xv = x_ref[...]
        rstd = lax.rsqrt(_head_sums(xv * xv) * (1.0 / HEAD_DIM) + RMS_EPS)
        y_ref[...] = (xv * rstd) * g_ref[...]

    return pl.pallas_call(
        body, name="head_rms_fwd", grid=(rows // tr,),
        in_specs=[pl.BlockSpec((tr, w), lambda i: (i, 0)), pl.BlockSpec((1, w), lambda i: (0, 0))],
        out_specs=pl.BlockSpec((tr, w), lambda i: (i, 0)),
        out_shape=jax.ShapeDtypeStruct((rows, w), F32),
        compiler_params=pltpu.CompilerParams(dimension_semantics=("parallel",)),
    )(x, g)


def _head_rms_bwd_call(x, g, dy):
    rows, w = x.shape
    tr = _row_tile(rows, w, budget=1024 * 1024)

    def body(x_ref, g_ref, dy_ref, dx_ref, dg_ref):
        @pl.when(pl.program_id(0) == 0)
        def _():
            dg_ref[...] = jnp.zeros_like(dg_ref)

        xv, dyv = x_ref[...], dy_ref[...]
        rstd = lax.rsqrt(_head_sums(xv * xv) * (1.0 / HEAD_DIM) + RMS_EPS)
        xhat = xv * rstd
        dxhat = dyv * g_ref[...]
        dx_ref[...] = rstd * (dxhat - xhat * (_head_sums(dxhat * xhat) * (1.0 / HEAD_DIM)))
        dg_ref[...] += jnp.sum(dyv * xhat, axis=0, keepdims=True)

    return pl.pallas_call(
        body, name="head_rms_bwd", grid=(rows // tr,),
        in_specs=[pl.BlockSpec((tr, w), lambda i: (i, 0)), pl.BlockSpec((1, w), lambda i: (0, 0)),
                  pl.BlockSpec((tr, w), lambda i: (i, 0))],
        out_specs=[pl.BlockSpec((tr, w), lambda i: (i, 0)), pl.BlockSpec((1, w), lambda i: (0, 0))],
        out_shape=[jax.ShapeDtypeStruct((rows, w), F32), jax.ShapeDtypeStruct((1, w), F32)],
        compiler_params=pltpu.CompilerParams(dimension_semantics=("arbitrary",)),
    )(x, g, dy)


@jax.custom_vjp
def head_rms(x, g):
    return _head_rms_fwd_call(x, g)


head_rms.defvjp(lambda x, g: (_head_rms_fwd_call(x, g), (x, g)),
                lambda res, dy: tuple(_head_rms_bwd_call(res[0], res[1], dy)))


def _gn_fwd_call(y, r, kf, v, g, gw, gb, rk):
    rows, w = y.shape
    tr = _row_tile(rows, w, budget=512 * 1024)

    def body(y_ref, r_ref, kf_ref, v_ref, g_ref, gw_ref, gb_ref, rk_ref, o_ref):
        yv = y_ref[...]
        yc = yv - _head_sums(yv) * (1.0 / HEAD_DIM)
        rstd = lax.rsqrt(_head_sums(yc * yc) * (1.0 / HEAD_DIM) + GN_EPS)
        s = _head_sums(r_ref[...] * kf_ref[...] * rk_ref[...])
        o_ref[...] = ((yc * rstd) * gw_ref[...] + gb_ref[...] + s * v_ref[...]) * g_ref[...]

    tok = pl.BlockSpec((tr, w), lambda i: (i, 0))
    par = pl.BlockSpec((1, w), lambda i: (0, 0))
    return pl.pallas_call(
        body, name="gn_bonus_fwd", grid=(rows // tr,),
        in_specs=[tok] * 5 + [par] * 3, out_specs=tok,
        out_shape=jax.ShapeDtypeStruct((rows, w), F32),
        compiler_params=pltpu.CompilerParams(dimension_semantics=("parallel",)),
    )(y, r, kf, v, g, gw, gb, rk)


def _gn_bwd_call(y, r, kf, v, g, gw, gb, rk, do):
    rows, w = y.shape
    tr = _row_tile(rows, w, budget=512 * 1024)

    def body(y_ref, r_ref, kf_ref, v_ref, g_ref, gw_ref, gb_ref, rk_ref, do_ref,
             dy_ref, dr_ref, dkf_ref, dv_ref, dg_ref, dgw_ref, dgb_ref, drk_ref):
        @pl.when(pl.program_id(0) == 0)
        def _():
            dgw_ref[...] = jnp.zeros_like(dgw_ref)
            dgb_ref[...] = jnp.zeros_like(dgb_ref)
            drk_ref[...] = jnp.zeros_like(drk_ref)

        yv, rv, kv, vv, rkv = y_ref[...], r_ref[...], kf_ref[...], v_ref[...], rk_ref[...]
        mean = lambda t: _head_sums(t) * (1.0 / HEAD_DIM)
        yc = yv - mean(yv)
        rstd = lax.rsqrt(mean(yc * yc) + GN_EPS)
        yhat = yc * rstd
        s = _head_sums(rv * kv * rkv)
        dg_ref[...] = do_ref[...] * (yhat * gw_ref[...] + gb_ref[...] + s * vv)
        dov = do_ref[...] * g_ref[...]
        dyhat = dov * gw_ref[...]
        dy_ref[...] = rstd * (dyhat - mean(dyhat) - yhat * mean(dyhat * yhat))
        ds = _head_sums(dov * vv)
        dv_ref[...] = s * dov
        dr_ref[...] = ds * kv * rkv
        dkf_ref[...] = ds * rv * rkv
        dgw_ref[...] += jnp.sum(dov * yhat, axis=0, keepdims=True)
        dgb_ref[...] += jnp.sum(dov, axis=0, keepdims=True)
        drk_ref[...] += jnp.sum(ds * rv * kv, axis=0, keepdims=True)

    tok = pl.BlockSpec((tr, w), lambda i: (i, 0))
    par = pl.BlockSpec((1, w), lambda i: (0, 0))
    tshape = jax.ShapeDtypeStruct((rows, w), F32)
    pshape = jax.ShapeDtypeStruct((1, w), F32)
    return pl.pallas_call(
        body, name="gn_bonus_bwd", grid=(rows // tr,),
        in_specs=[tok] * 5 + [par] * 3 + [tok], out_specs=[tok] * 5 + [par] * 3,
        out_shape=[tshape] * 5 + [pshape] * 3,
        compiler_params=pltpu.CompilerParams(dimension_semantics=("arbitrary",)),
    )(y, r, kf, v, g, gw, gb, rk, do)


@jax.custom_vjp
def gn_bonus(y, r, kf, v, g, gw, gb, rk):
    return _gn_fwd_call(y, r, kf, v, g, gw, gb, rk)


def _gn_bwd(res, do):
    return tuple(_gn_bwd_call(*res, do))


gn_bonus.defvjp(lambda *a: (_gn_fwd_call(*a), a), _gn_bwd)


PREP_ROWS = 128


def _prep_segments(rw, lora_w, lora_a, lora_g):
    at = 3 * rw
    seg = {"r": (0, rw), "k": (rw, 2 * rw), "v": (2 * rw, 3 * rw)}
    for name, n in (("wd", lora_w), ("ad", lora_a), ("gd", lora_g)):
        seg[name] = (at, at + _pad128(n))
        at += _pad128(n)
    return seg, at


def _prep_shifted(z_ref, zlast_ref, mu_ref, seg, first_tile):
    lo, hi = seg
    zr = z_ref[:, lo:hi]
    rows = zr.shape[0]
    before = jnp.where(first_tile, 0.0, zlast_ref[7:8, lo:hi])
    row0 = lax.broadcasted_iota(jnp.int32, zr.shape, 0) == 0
    diff = jnp.where(row0, before, pltpu.roll(zr, 1, axis=0)) - zr
    return zr + diff * mu_ref[:, lo:hi], diff


def _prep_forward_values(z_ref, zlast_ref, mu_ref, w0_ref, a0_ref, kk_ref, ka_ref, w2_ref, a2_ref, g2_ref, segs, first_tile):
    z = {n: _prep_shifted(z_ref, zlast_ref, mu_ref, segs[n], first_tile) for n in segs}
    r, k, v, wd, ad, gd = (z[n][0] for n in ("r", "k", "v", "wd", "ad", "gd"))
    twd = jnp.tanh(wd)
    pw = _mm(twd, w2_ref[...]) + w0_ref[...]
    lw = -jnp.exp(-(jnp.maximum(-pw, 0.0) + jnp.log(1.0 + jnp.exp(-jnp.abs(pw)))) - 0.5)
    a_sig = 1.0 / (1.0 + jnp.exp(-(_mm(ad, a2_ref[...]) + a0_ref[...])))
    sg = 1.0 / (1.0 + jnp.exp(-gd))
    kx = k * kk_ref[...]
    nrm = jnp.sqrt(_head_sums(kx * kx))
    inv = 1.0 / jnp.maximum(nrm, L2_FLOOR)
    return dict(z=z, r=r, k=k, v=v, twd=twd, pw=pw, lw=lw, a_sig=a_sig, sg=sg, ad=ad, kk=kx * inv, inv=inv, live=nrm > L2_FLOOR)


def _prep_specs(tokens, rpad, rw, w2, a2, g2):
    tr = PREP_ROWS
    tile = lambda w: pl.BlockSpec((tr, w), lambda i: (i, 0))
    before = pl.BlockSpec((8, rpad), lambda i: (jnp.maximum(i * (tr // 8) - 1, 0), 0))
    whole = lambda a: pl.BlockSpec(a.shape, lambda i: (0, 0))
    par = pl.BlockSpec((1, rw), lambda i: (0, 0))
    return tile, before, whole, par, pl.BlockSpec((1, rpad), lambda i: (0, 0))


def _prep_fwd_call(zr, mu, w0, a0, k_k, k_a, w2, a2, g2):
    tokens, rpad = zr.shape
    rw = w0.shape[1]
    segs, _ = _prep_segments(rw, w2.shape[0], a2.shape[0], g2.shape[0])
    tile, before, whole, par, mu_spec = _prep_specs(tokens, rpad, rw, w2, a2, g2)

    def body(z_ref, zlast_ref, mu_ref, w0_ref, a0_ref, kk_ref, ka_ref, w2_ref, a2_ref, g2_ref,
             r_ref, lw_ref, kf_ref, v_ref, na_ref, b_ref, g_ref):
        f = _prep_forward_values(z_ref, zlast_ref, mu_ref, w0_ref, a0_ref, kk_ref, ka_ref, w2_ref, a2_ref, g2_ref,
                                 segs, pl.program_id(0) == 0)
        r_ref[...] = f["r"]
        v_ref[...] = f["v"]
        lw_ref[...] = f["lw"]
        kf_ref[...] = f["k"] * (1.0 + (f["a_sig"] - 1.0) * ka_ref[...])
        na_ref[...] = -f["kk"]
        b_ref[...] = f["kk"] * f["a_sig"]
        g_ref[...] = _mm(f["sg"], g2_ref[...])

    shape = jax.ShapeDtypeStruct((tokens, rw), F32)
    return pl.pallas_call(
        body, name="rwkv_prep_fwd", grid=(tokens // PREP_ROWS,),
        in_specs=[tile(rpad), before, mu_spec, par, par, par, par, whole(w2), whole(a2), whole(g2)],
        out_specs=[tile(rw)] * 7, out_shape=[shape] * 7,
        compiler_params=pltpu.CompilerParams(dimension_semantics=("parallel",), vmem_limit_bytes=VMEM_LIMIT_CAP),
    )(zr, zr, mu, w0, a0, k_k, k_a, w2, a2, g2)


def _prep_bwd_call(zr, mu, w0, a0, k_k, k_a, w2, a2, g2, cts):
    tokens, rpad = zr.shape
    rw = w0.shape[1]
    segs, _ = _prep_segments(rw, w2.shape[0], a2.shape[0], g2.shape[0])
    tile, before, whole, par, mu_spec = _prep_specs(tokens, rpad, rw, w2, a2, g2)
    nt = tokens // PREP_ROWS
    rev = lambda spec: pl.BlockSpec(spec.block_shape, lambda i, f=spec.index_map: f(nt - 1 - i))

    def body(z_ref, zlast_ref, mu_ref, w0_ref, a0_ref, kk_ref, ka_ref, w2_ref, a2_ref, g2_ref,
             dr_ref, dlw_ref, dkf_ref, dv_ref, dna_ref, db_ref, dg_ref,
             dz_ref, dmu_ref, dw0_ref, da0_ref, dkk_ref, dka_ref, dw2_ref, da2_ref, dg2_ref, carry):
        step = pl.program_id(0)

        @pl.when(step == 0)
        def _():
            for ref in (dmu_ref, dw0_ref, da0_ref, dkk_ref, dka_ref, dw2_ref, da2_ref, dg2_ref, carry):
                ref[...] = jnp.zeros_like(ref)

        f = _prep_forward_values(z_ref, zlast_ref, mu_ref, w0_ref, a0_ref, kk_ref, ka_ref, w2_ref, a2_ref, g2_ref,
                                 segs, step == nt - 1)
        k, kk, a_sig, sg, twd = f["k"], f["kk"], f["a_sig"], f["sg"], f["twd"]
        colsum = lambda t: jnp.sum(t, axis=0, keepdims=True)
        dkf, db, dg = dkf_ref[...], db_ref[...], dg_ref[...]
        ka = ka_ref[...]
        dgd = _mm(dg, g2_ref[...], tb=True) * sg * (1.0 - sg)
        dg2_ref[...] += _mm(sg, dg, ta=True)
        dkk = db * a_sig - dna_ref[...]
        da_sig = db * kk + dkf * k * ka
        dk = dkf * (1.0 + (a_sig - 1.0) * ka)
        dka_ref[...] += colsum(dkf * k * (a_sig - 1.0))
        along = jnp.where(f["live"], _head_sums(dkk * kk), 0.0)
        dkx = (dkk - kk * along) * f["inv"]
        dk = dk + dkx * kk_ref[...]
        dkk_ref[...] += colsum(dkx * k)
        dpa = da_sig * a_sig * (1.0 - a_sig)
        da0_ref[...] += colsum(dpa)
        dad = _mm(dpa, a2_ref[...], tb=True)
        da2_ref[...] += _mm(f["ad"], dpa, ta=True)
        dpw = dlw_ref[...] * f["lw"] / (1.0 + jnp.exp(f["pw"]))
        dw0_ref[...] += colsum(dpw)
        dwd = _mm(dpw, w2_ref[...], tb=True) * (1.0 - twd * twd)
        dw2_ref[...] += _mm(twd, dpw, ta=True)
        rows = PREP_ROWS
        last = lax.broadcasted_iota(jnp.int32, (rows, 1), 0) == rows - 1
        for name, dz in (("r", dr_ref[...]), ("k", dk), ("v", dv_ref[...]), ("wd", dwd), ("ad", dad), ("gd", dgd)):
            lo, hi = segs[name]
            mu_s = mu_ref[:, lo:hi]
            dmu_ref[:, lo:hi] += colsum(dz * f["z"][name][1])
            later = dz * mu_s
            dz_ref[:, lo:hi] = dz * (1.0 - mu_s) + jnp.where(last, carry[:, lo:hi], pltpu.roll(later, rows - 1, axis=0))
            carry[:, lo:hi] = later[0:1, :]

    tok = jax.ShapeDtypeStruct((tokens, rw), F32)
    acc = lambda a: jax.ShapeDtypeStruct(a.shape, F32)
    return pl.pallas_call(
        body, name="rwkv_prep_bwd", grid=(nt,),
        in_specs=[rev(tile(rpad)), rev(before), mu_spec, par, par, par, par, whole(w2), whole(a2), whole(g2)]
                 + [rev(tile(rw))] * 7,
        out_specs=[rev(tile(rpad)), mu_spec, par, par, par, par, whole(w2), whole(a2), whole(g2)],
        out_shape=[jax.ShapeDtypeStruct((tokens, rpad), F32), acc(mu), acc(w0), acc(a0), acc(k_k), acc(k_a), acc(w2), acc(a2), acc(g2)],
        scratch_shapes=[pltpu.VMEM((1, rpad), F32)],
        compiler_params=pltpu.CompilerParams(dimension_semantics=("arbitrary",), vmem_limit_bytes=VMEM_LIMIT_CAP),
    )(zr, zr, mu, w0, a0, k_k, k_a, w2, a2, g2, *cts)


@jax.custom_vjp
def rwkv_prep(zr, mu, w0, a0, k_k, k_a, w2, a2, g2):
    return tuple(_prep_fwd_call(zr, mu, w0, a0, k_k, k_a, w2, a2, g2))


def _rwkv_prep_bwd(res, cts):
    zr, mu, w0, a0, k_k, k_a, w2, a2, g2 = res
    dz, dmu, dw0, da0, dkk, dka, dw2, da2, dg2 = _prep_bwd_call(*res, cts)
    return dz, dmu, dw0, da0, dkk, dka, dw2.astype(w2.dtype), da2.astype(a2.dtype), dg2.astype(g2.dtype)


rwkv_prep.defvjp(lambda *a: (tuple(_prep_fwd_call(*a)), a), _rwkv_prep_bwd)


def _pair_masks(rows):
    lane = lax.broadcasted_iota(jnp.int32, (rows, PAIR), 1)
    return lane < HEAD_DIM, lane >= HEAD_DIM


def _bd(x):
    m0, m1 = _pair_masks(x.shape[0])
    return jnp.concatenate([jnp.where(m0, x, 0.0), jnp.where(m1, x, 0.0)], axis=0)


def _unbd(m, c):
    return jnp.where(_pair_masks(c)[0], m[:c], m[c:])


def _pair_a(l2, r2):
    return _mm(l2, _bd(r2), tb=True)


def _pair_mul(p2, x2):
    return _mm(p2, _bd(x2))


def _pair_mul_t(p2, x2):
    return _unbd(_mm(p2, x2, ta=True), p2.shape[0])


def _block_diag_mask():
    row = lax.broadcasted_iota(jnp.int32, (PAIR, PAIR), 0)
    lane = lax.broadcasted_iota(jnp.int32, (PAIR, PAIR), 1)
    return (row < HEAD_DIM) == (lane < HEAD_DIM), row == lane


def _wkv_pair_common(r, lw, k, a, b):
    c = r[0].shape[0]
    pairs = range(len(r))
    i = lax.broadcasted_iota(jnp.int32, (c, PAIR), 0)
    j = lax.broadcasted_iota(jnp.int32, (c, PAIR), 1) % c
    strict, incl = i > j, i >= j
    ti = lax.broadcasted_iota(jnp.int32, (c, c), 0)
    tj = lax.broadcasted_iota(jnp.int32, (c, c), 1)
    tri = jnp.where(ti >= tj, 1.0, 0.0).astype(BF16)
    lc = [sum(_dg(tri, part, False, False) for part in _split(lw[p], 3)) for p in pairs]
    lend = [lc[p][c - 1:c, :] for p in pairs]
    rt = [r[p] * jnp.exp(lc[p]) for p in pairs]
    at = [a[p] * jnp.exp(lc[p] - lw[p]) for p in pairs]
    pinv = [jnp.exp(-lc[p]) for p in pairs]
    kt = [k[p] * pinv[p] for p in pairs]
    bt = [b[p] * pinv[p] for p in pairs]
    e = [jnp.exp(lend[p] - lc[p]) for p in pairs]
    ktp = [k[p] * e[p] for p in pairs]
    btp = [b[p] * e[p] for p in pairs]
    a_ab = [jnp.where(strict, _pair_a(at[p], bt[p]), 0.0) for p in pairs]
    a_ak = [jnp.where(strict, _pair_a(at[p], kt[p]), 0.0) for p in pairs]
    a_rb = [jnp.where(incl, _pair_a(rt[p], bt[p]), 0.0) for p in pairs]
    a_rk = [jnp.where(incl, _pair_a(rt[p], kt[p]), 0.0) for p in pairs]
    t = [jnp.where(i == j, 1.0, 0.0) + a_ab[p] for p in pairs]
    xp = a_ab
    n = 2
    while n < c:
        xp = [_pair_mul(xp[p], xp[p]) for p in pairs]
        t = [t[p] + _pair_mul(t[p], xp[p]) for p in pairs]
        n *= 2
    bdm, eye = _block_diag_mask()
    pend_col = [jnp.sum(jnp.where(eye, jnp.exp(lend[p]), 0.0), axis=1, keepdims=True) for p in pairs]
    return dict(rt=rt, at=at, kt=kt, bt=bt, ktp=ktp, btp=btp, a_ak=a_ak, a_rb=a_rb, a_rk=a_rk, t=t,
                pend_col=pend_col, lend=lend, lc=lc, strict=strict, incl=incl, tri=tri, bdm=bdm)


def _wkv_group(width):
    npair = width // PAIR
    g = min(WKV_PAIRS_PER_STEP, npair)
    assert npair % g == 0
    return npair, g


def _wkv_fwd_call(r, lw, k, v, a, b):
    tokens, width = r.shape
    c = WKV_CHUNK
    nc = tokens // c
    npair, g = _wkv_group(width)

    def body(r_ref, lw_ref, k_ref, v_ref, a_ref, b_ref, y_ref, s_ref, st):
        @pl.when(pl.program_id(1) == 0)
        def _():
            st[...] = jnp.zeros_like(st)

        pairs = range(g)
        rv, lwv, kv, vv, av, bv = ([ref[:, p * PAIR:(p + 1) * PAIR] for p in pairs]
                                   for ref in (r_ref, lw_ref, k_ref, v_ref, a_ref, b_ref))
        s0 = [st[p] for p in pairs]
        q = _wkv_pair_common(rv, lwv, kv, av, bv)
        w1 = [_mm(q["at"][p], s0[p]) + _pair_mul(q["a_ak"][p], vv[p]) for p in pairs]
        u = [_pair_mul(q["t"][p], w1[p]) for p in pairs]
        y = [_mm(q["rt"][p], s0[p]) + _pair_mul(q["a_rb"][p], u[p]) + _pair_mul(q["a_rk"][p], vv[p]) for p in pairs]
        grow = [_mm(jnp.concatenate([q["btp"][p], q["ktp"][p]], axis=0), jnp.concatenate([u[p], vv[p]], axis=0), ta=True)
                for p in pairs]
        for p in pairs:
            y_ref[:, p * PAIR:(p + 1) * PAIR] = y[p]
            s_ref[0, p] = s0[p]
            st[p] = q["pend_col"][p] * s0[p] + jnp.where(q["bdm"], grow[p], 0.0)

    tok = pl.BlockSpec((c, g * PAIR), lambda gi, ci: (ci, gi))
    return pl.pallas_call(
        body, name="wkv_fwd", grid=(npair // g, nc),
        in_specs=[tok] * 6,
        out_specs=[tok, pl.BlockSpec((1, g, PAIR, PAIR), lambda gi, ci: (ci, gi, 0, 0))],
        out_shape=[jax.ShapeDtypeStruct((tokens, width), F32), jax.ShapeDtypeStruct((nc, npair, PAIR, PAIR), F32)],
        scratch_shapes=[pltpu.VMEM((g, PAIR, PAIR), F32)],
        compiler_params=pltpu.CompilerParams(dimension_semantics=("parallel", "arbitrary")),
    )(r, lw, k, v, a, b)


def _wkv_bwd_call(r, lw, k, v, a, b, s, dy):
    tokens, width = r.shape
    c = WKV_CHUNK
    nc = tokens // c
    npair, g = _wkv_group(width)

    def body(r_ref, lw_ref, k_ref, v_ref, a_ref, b_ref, s_ref, dy_ref,
             dr_ref, dlw_ref, dk_ref, dv_ref, da_ref, db_ref, dst):
        @pl.when(pl.program_id(1) == 0)
        def _():
            dst[...] = jnp.zeros_like(dst)

        pairs = range(g)
        rv, lwv, kv, vv, av, bv, dyv = ([ref[:, p * PAIR:(p + 1) * PAIR] for p in pairs]
                                        for ref in (r_ref, lw_ref, k_ref, v_ref, a_ref, b_ref, dy_ref))
        s0 = [s_ref[0, p] for p in pairs]
        dsc = [dst[p] for p in pairs]
        q = _wkv_pair_common(rv, lwv, kv, av, bv)
        rt, at, kt, bt, ktp, btp, t = (q[n] for n in ("rt", "at", "kt", "bt", "ktp", "btp", "t"))
        a_ak, a_rb, a_rk, strict, incl = (q[n] for n in ("a_ak", "a_rb", "a_rk", "strict", "incl"))
        w1 = [_mm(at[p], s0[p]) + _pair_mul(a_ak[p], vv[p]) for p in pairs]
        u = [_pair_mul(t[p], w1[p]) for p in pairs]
        du = [_pair_mul_t(a_rb[p], dyv[p]) + _mm(btp[p], dsc[p]) for p in pairs]
        dw1 = [_pair_mul_t(t[p], du[p]) for p in pairs]
        dv = [_pair_mul_t(a_rk[p], dyv[p]) + _mm(ktp[p], dsc[p]) + _pair_mul_t(a_ak[p], dw1[p]) for p in pairs]
        da_ab = [jnp.where(strict, _pair_a(dw1[p], u[p]), 0.0) for p in pairs]
        da_ak = [jnp.where(strict, _pair_a(dw1[p], vv[p]), 0.0) for p in pairs]
        da_rb = [jnp.where(incl, _pair_a(dyv[p], u[p]), 0.0) for p in pairs]
        da_rk = [jnp.where(incl, _pair_a(dyv[p], vv[p]), 0.0) for p in pairs]
        d_rt = [_mm(dyv[p], s0[p], tb=True) + _pair_mul(da_rb[p], bt[p]) + _pair_mul(da_rk[p], kt[p]) for p in pairs]
        d_at = [_mm(dw1[p], s0[p], tb=True) + _pair_mul(da_ab[p], bt[p]) + _pair_mul(da_ak[p], kt[p]) for p in pairs]
        d_bt = [_pair_mul_t(da_ab[p], at[p]) + _pair_mul_t(da_rb[p], rt[p]) for p in pairs]
        d_kt = [_pair_mul_t(da_ak[p], at[p]) + _pair_mul_t(da_rk[p], rt[p]) for p in pairs]
        d_btp = [_mm(u[p], dsc[p], tb=True) for p in pairs]
        d_ktp = [_mm(vv[p], dsc[p], tb=True) for p in pairs]
        ones = jnp.ones((8, PAIR), BF16)
        dpend = [sum(_dg(ones, part, False, True) for part in _split(dsc[p] * s0[p], 3))[0:1, :] * jnp.exp(q["lend"][p])
                 for p in pairs]
        grow = [_mm(jnp.concatenate([rt[p], at[p]], axis=0), jnp.concatenate([dyv[p], dw1[p]], axis=0), ta=True)
                for p in pairs]
        last = lax.broadcasted_iota(jnp.int32, (c, PAIR), 0) == c - 1
        for p in pairs:
            sl = slice(p * PAIR, (p + 1) * PAIR)
            dst[p] = q["pend_col"][p] * dsc[p] + jnp.where(q["bdm"], grow[p], 0.0)
            lc_e = d_ktp[p] * ktp[p] + d_btp[p] * btp[p]
            dlend = jnp.sum(lc_e, axis=0, keepdims=True) + dpend[p]
            dlc = d_rt[p] * rt[p] - d_kt[p] * kt[p] - d_bt[p] * bt[p] - lc_e + jnp.where(last, dlend, 0.0)
            dlp = d_at[p] * at[p]
            dlw_ref[:, sl] = sum(_dg(q["tri"], part, True, False) for part in _split(dlc + dlp, 3)) - dlp
            lc = q["lc"][p]
            pinv = jnp.exp(-lc)
            e = jnp.exp(q["lend"][p] - lc)
            dr_ref[:, sl] = d_rt[p] * jnp.exp(lc)
            da_ref[:, sl] = d_at[p] * jnp.exp(lc - lwv[p])
            dk_ref[:, sl] = d_kt[p] * pinv + d_ktp[p] * e
            db_ref[:, sl] = d_bt[p] * pinv + d_btp[p] * e
            dv_ref[:, sl] = dv[p]

    tok = pl.BlockSpec((c, g * PAIR), lambda gi, ci: (nc - 1 - ci, gi))
    tshape = jax.ShapeDtypeStruct((tokens, width), F32)
    return pl.pallas_call(
        body, name="wkv_bwd", grid=(npair // g, nc),
        in_specs=[tok] * 6 + [pl.BlockSpec((1, g, PAIR, PAIR), lambda gi, ci: (nc - 1 - ci, gi, 0, 0)), tok],
        out_specs=[tok] * 6, out_shape=[tshape] * 6,
        scratch_shapes=[pltpu.VMEM((g, PAIR, PAIR), F32)],
        compiler_params=pltpu.CompilerParams(dimension_semantics=("parallel", "arbitrary")),
    )(r, lw, k, v, a, b, s, dy)


@jax.custom_vjp
def wkv7(r, lw, k, v, a, b):
    return _wkv_fwd_call(r, lw, k, v, a, b)[0]


def _wkv7_fwd(r, lw, k, v, a, b):
    y, s = _wkv_fwd_call(r, lw, k, v, a, b)
    return y, (r, lw, k, v, a, b, s)


wkv7.defvjp(_wkv7_fwd, lambda res, dy: tuple(_wkv_bwd_call(*res, dy)))


def _attn_block(tokens):
    return ATTN_BLOCK_BIG if tokens % ATTN_BLOCK_BIG == 0 else ATTN_BLOCK


def _fox_layouts(cum):
    tokens, heads = cum.shape
    t = _attn_block(tokens)
    cq = cum.reshape(tokens, heads // 2, 2).transpose(1, 0, 2)
    ck = cum.T.reshape(heads // 2, 2, tokens // t, t).transpose(0, 2, 1, 3)
    return cq, ck


def _head_lane_masks(rows):
    lane = lax.broadcasted_iota(jnp.int32, (rows, 2 * HEAD_DIM), 1)
    return [lane < HEAD_DIM, lane >= HEAD_DIM]


def _fox_fwd_call(q, k, v, cq, ck):
    tokens, width = q.shape
    t = _attn_block(tokens)
    nb = tokens // t
    hd = HEAD_DIM
    npair = width // (2 * hd)

    def body(q_ref, k_ref, v_ref, cq_ref, ck_ref, o_ref, lse_ref):
        i = pl.program_id(1)
        masks = _head_lane_masks(t)
        q2 = q_ref[...]
        qs = [jnp.where(mk, q2, 0.0).astype(BF16) for mk in masks]
        cqs = [cq_ref[0, :, hh:hh + 1] for hh in range(2)]

        def block(j, carry, diagonal):
            off = pl.multiple_of(j * t, t)
            ckj = ck_ref[0, j]
            k2 = k_ref[pl.ds(off, t), :].astype(BF16)
            v2 = v_ref[pl.ds(off, t), :].astype(BF16)
            out = []
            for hh in range(2):
                m, l, acc = carry[hh]
                s = _dg(qs[hh], k2, False, True) + (cqs[hh] - ckj[hh:hh + 1, :])
                if diagonal:
                    keep = lax.broadcasted_iota(jnp.int32, (t, t), 0) >= lax.broadcasted_iota(jnp.int32, (t, t), 1)
                    s = jnp.where(keep, s, NEG_BIG)
                m_new = jnp.maximum(m, jnp.max(s, axis=1, keepdims=True))
                alpha = jnp.exp(m - m_new)
                p = jnp.exp(s - m_new)
                l = alpha * l + jnp.sum(p, axis=1, keepdims=True)
                acc = alpha * acc + _dg(p.astype(BF16), v2, False, False)
                out.append((m_new, l, acc))
            return tuple(out)

        init = tuple((jnp.full((t, 1), NEG_BIG, F32), jnp.zeros((t, 1), F32), jnp.zeros((t, 2 * hd), F32)) for _ in range(2))
        res = lax.fori_loop(0, i, lambda j, c: block(j, c, False), init)
        res = block(i, res, True)
        o_ref[...] = jnp.where(masks[0], res[0][2] / res[0][1], res[1][2] / res[1][1])
        for hh in range(2):
            lse_ref[0, :, hh:hh + 1] = res[hh][0] + jnp.log(res[hh][1])

    blk = pl.BlockSpec((t, 2 * hd), lambda hp, i: (i, hp))
    full = pl.BlockSpec((tokens, 2 * hd), lambda hp, i: (0, hp))
    cq_spec = pl.BlockSpec((1, t, 2), lambda hp, i: (hp, i, 0))
    ck_spec = pl.BlockSpec((1, nb, 2, t), lambda hp, i: (hp, 0, 0, 0))
    return pl.pallas_call(
        body, name="fox_fwd", grid=(npair, nb),
        in_specs=[blk, full, full, cq_spec, ck_spec],
        out_specs=[blk, cq_spec],
        out_shape=[jax.ShapeDtypeStruct((tokens, width), F32), jax.ShapeDtypeStruct((npair, tokens, 2), F32)],
        compiler_params=pltpu.CompilerParams(dimension_semantics=("parallel", "arbitrary")),
    )(q, k, v, cq, ck)


def _fox_bwd_call(q, k, v, cq, ck, o, lse, do):
    tokens, width = q.shape
    t = _attn_block(tokens)
    nb = tokens // t
    hd = HEAD_DIM
    npair = width // (2 * hd)

    def body(q_ref, k_ref, v_ref, cq_ref, ck_ref, o_ref, lse_ref, do_ref, dq_ref, dk_ref, dv_ref, dck_ref, dcq_ref):
        i = pl.program_id(1)

        @pl.when(i == 0)
        def _():
            dk_ref[...] = jnp.zeros_like(dk_ref)
            dv_ref[...] = jnp.zeros_like(dv_ref)
            dck_ref[...] = jnp.zeros_like(dck_ref)

        masks = _head_lane_masks(t)
        q2, do2, o2 = q_ref[...], do_ref[...], o_ref[...]
        qs = [jnp.where(mk, q2, 0.0).astype(BF16) for mk in masks]
        dos = [jnp.where(mk, do2, 0.0).astype(BF16) for mk in masks]
        deltas = [jnp.sum(dos[hh].astype(F32) * o2, axis=1, keepdims=True) for hh in range(2)]
        bias = [cq_ref[0, :, hh:hh + 1] - lse_ref[0, :, hh:hh + 1] for hh in range(2)]

        def block(j, carry, diagonal):
            off = pl.multiple_of(j * t, t)
            ckj = ck_ref[0, j]
            k2 = k_ref[pl.ds(off, t), :].astype(BF16)
            v2 = v_ref[pl.ds(off, t), :].astype(BF16)
            out = []
            dk2 = jnp.zeros((t, 2 * hd), F32)
            dv2 = jnp.zeros((t, 2 * hd), F32)
            for hh in range(2):
                s = _dg(qs[hh], k2, False, True) + (bias[hh] - ckj[hh:hh + 1, :])
                if diagonal:
                    keep = lax.broadcasted_iota(jnp.int32, (t, t), 0) >= lax.broadcasted_iota(jnp.int32, (t, t), 1)
                    s = jnp.where(keep, s, NEG_BIG)
                p = jnp.exp(s)
                dp = _dg(dos[hh], v2, False, True)
                ds = p * (dp - deltas[hh])
                dsb = ds.astype(BF16)
                dq, rowsum = carry[hh]
                out.append((dq + _dg(dsb, k2, False, False), rowsum + jnp.sum(ds, axis=1, keepdims=True)))
                dk2 = dk2 + _dg(dsb, qs[hh], True, False)
                dv2 = dv2 + _dg(p.astype(BF16), dos[hh], True, False)
                dck_ref[0, j, hh:hh + 1, :] -= jnp.sum(ds, axis=0, keepdims=True)
            dk_ref[pl.ds(off, t), :] += dk2
            dv_ref[pl.ds(off, t), :] += dv2
            return tuple(out)

        init = tuple((jnp.zeros((t, 2 * hd), F32), jnp.zeros((t, 1), F32)) for _ in range(2))
        res = lax.fori_loop(0, i, lambda j, c: block(j, c, False), init)
        res = block(i, res, True)
        dq_ref[...] = jnp.where(masks[0], res[0][0], res[1][0])
        for hh in range(2):
            dcq_ref[0, :, hh:hh + 1] = res[hh][1]

    blk = pl.BlockSpec((t, 2 * hd), lambda hp, i: (i, hp))
    full = pl.BlockSpec((tokens, 2 * hd), lambda hp, i: (0, hp))
    cq_spec = pl.BlockSpec((1, t, 2), lambda hp, i: (hp, i, 0))
    ck_spec = pl.BlockSpec((1, nb, 2, t), lambda hp, i: (hp, 0, 0, 0))
    tshape = jax.ShapeDtypeStruct((tokens, width), F32)
    return pl.pallas_call(
        body, name="fox_bwd", grid=(npair, nb),
        in_specs=[blk, full, full, cq_spec, ck_spec, blk, cq_spec, blk],
        out_specs=[blk, full, full, ck_spec, cq_spec],
        out_shape=[tshape, tshape, tshape, jax.ShapeDtypeStruct((npair, nb, 2, t), F32),
                   jax.ShapeDtypeStruct((npair, tokens, 2), F32)],
        compiler_params=pltpu.CompilerParams(dimension_semantics=("parallel", "arbitrary")),
    )(q, k, v, cq, ck, o, lse, do)


@jax.custom_vjp
def fox_attention(q, k, v, cum):
    return _fox_fwd_call(q, k, v, *_fox_layouts(cum))[0]


def _fox_fwd(q, k, v, cum):
    cq, ck = _fox_layouts(cum)
    o, lse = _fox_fwd_call(q, k, v, cq, ck)
    return o, (q, k, v, cq, ck, o, lse)


def _fox_bwd(res, do):
    q, k, v, cq, ck, o, lse = res
    dq, dk, dv, dck, dcq = _fox_bwd_call(q, k, v, cq, ck, o, lse, do)
    npair, nb, _, t = dck.shape
    dcum = dck.transpose(0, 2, 1, 3).reshape(2 * npair, nb * t).T + dcq.transpose(1, 0, 2).reshape(nb * t, 2 * npair)
    return dq, dk, dv, dcum


fox_attention.defvjp(_fox_fwd, _fox_bwd)


def _loss_call(y, target):
    rows, d = y.shape
    tr = _row_tile(rows, d)

    def body(y_ref, t_ref, loss_ref, dy_ref):
        @pl.when(pl.program_id(0) == 0)
        def _():
            loss_ref[...] = jnp.zeros_like(loss_ref)

        diff = y_ref[...] - t_ref[...]
        dy_ref[...] = diff * (1.0 / d)
        loss_ref[...] += (0.5 / d) * jnp.sum(jnp.sum(diff * diff, axis=1, keepdims=True), axis=0, keepdims=True)

    return pl.pallas_call(
        body, name="loss", grid=(rows // tr,),
        in_specs=[pl.BlockSpec((tr, d), lambda i: (i, 0))] * 2,
        out_specs=[pl.BlockSpec((1, 1), lambda i: (0, 0)), pl.BlockSpec((tr, d), lambda i: (i, 0))],
        out_shape=[jax.ShapeDtypeStruct((1, 1), F32), jax.ShapeDtypeStruct((rows, d), F32)],
        compiler_params=pltpu.CompilerParams(dimension_semantics=("arbitrary",)),
    )(y, target)


def _adamw_call(w, g, m, v):
    rows, cols = w.shape
    tr = _row_tile_ragged(rows, cols, budget=1024 * 1024)
    c1 = 1.0 / (1.0 - ADAM_B1 ** ADAM_STEP)
    c2 = 1.0 / (1.0 - ADAM_B2 ** ADAM_STEP)

    def body(w_ref, g_ref, m_ref, v_ref, d_ref, nm_ref, nv_ref):
        gv = g_ref[...]
        nm = ADAM_B1 * m_ref[...] + (1.0 - ADAM_B1) * gv
        nv = ADAM_B2 * v_ref[...] + (1.0 - ADAM_B2) * (gv * gv)
        nm_ref[...] = nm
        nv_ref[...] = nv
        d_ref[...] = -ADAM_LR * ((nm * c1) / (jnp.sqrt(nv * c2) + ADAM_EPS) + ADAM_WD * w_ref[...])

    spec = pl.BlockSpec((tr, cols), lambda i: (i, 0))
    shape = jax.ShapeDtypeStruct((rows, cols), F32)
    return pl.pallas_call(
        body, name="adamw", grid=(pl.cdiv(rows, tr),),
        in_specs=[spec] * 4, out_specs=[spec] * 3, out_shape=[shape] * 3,
        compiler_params=pltpu.CompilerParams(dimension_semantics=("parallel",)),
    )(w, g, m, v)


def _my_place():
    return lax.axis_index("x"), lax.axis_index("y"), lax.axis_index("c")


def _place_index(px, py, pc):
    return 4 * px + 2 * py + pc


HBM_SPEC = pl.BlockSpec(memory_space=pltpu.HBM)


def _all_gather_call(block):
    def body(x_ref, out_ref, send_sems, recv_sems, local_sem):
        x, y, c = _my_place()
        me, sibling = (x, y, c), (x, y, 1 - c)
        chips = [(1 - x, y), (x, 1 - y), (1 - x, 1 - y)]

        def slot(px, py, pc):
            return out_ref.at[_place_index(px, py, pc)]

        def copy(k, blk, to, src=None):
            return pltpu.make_async_remote_copy(
                src_ref=slot(*blk) if src is None else src, dst_ref=slot(*blk),
                send_sem=send_sems.at[k], recv_sem=recv_sems.at[k],
                device_id=to, device_id_type=pl.DeviceIdType.MESH)

        mine = pltpu.make_async_copy(x_ref, slot(*me), local_sem)
        mine.start()
        first = [copy(0, me, sibling, src=x_ref)]
        first += [copy(1 + j, me, (*chip, c), src=x_ref) for j, chip in enumerate(chips)]
        for cp in first:
            cp.start()
        passed = [copy(4 + j, (*chip, c), sibling) for j, chip in enumerate(chips)]
        for j, chip in enumerate(chips):
            copy(1 + j, (*chip, c), me).wait_recv()
            passed[j].start()
        copy(0, sibling, me).wait_recv()
        for j, chip in enumerate(chips):
            copy(4 + j, (*chip, 1 - c), me).wait_recv()
        for cp in first + passed:
            cp.wait_send()
        mine.wait()

    return pl.pallas_call(
        body, name="all_gather",
        out_shape=jax.ShapeDtypeStruct((N_DEV,) + block.shape, block.dtype),
        in_specs=[HBM_SPEC], out_specs=HBM_SPEC,
        scratch_shapes=[pltpu.SemaphoreType.DMA((7,)), pltpu.SemaphoreType.DMA((7,)), pltpu.SemaphoreType.DMA],
    )(block)


SEM_SPEC = pl.BlockSpec(memory_space=pltpu.SEMAPHORE)
SIDE_EFFECT = pltpu.SideEffectType.DATAFLOW_SIDE_EFFECTING


def _peers():
    x, y, c = _my_place()
    out = []
    for k in range(1, N_DEV):
        peer = (x ^ (k >> 2), y ^ ((k >> 1) & 1), c ^ (k & 1))
        out.append((k - 1, peer, _place_index(*peer)))
    return _place_index(x, y, c), out


def _spread_start(src, per_peer, name, after=None):
    slot = src.shape[1:] if per_peer else src.shape
    order = () if after is None else (after,)

    def body(src_ref, land_ref, *rest):
        send_sems, recv_sems, src_thru, land_thru, token = rest[len(order):]
        mine, peers = _peers()
        for k, peer, peer_idx in peers:
            pltpu.make_async_remote_copy(
                src_ref=src_ref.at[peer_idx] if per_peer else src_ref, dst_ref=land_ref.at[mine],
                send_sem=send_sems.at[k], recv_sem=recv_sems.at[k],
                device_id=peer, device_id_type=pl.DeviceIdType.MESH).start()
        token[...] = jnp.zeros_like(token)

    return pl.pallas_call(
        body, name=name,
        out_shape=(pltpu.SemaphoreType.DMA((N_DEV - 1,)), pltpu.SemaphoreType.DMA((N_DEV - 1,)),
                   pltpu.HBM(src.shape, src.dtype), pltpu.HBM((N_DEV,) + slot, src.dtype),
                   jax.ShapeDtypeStruct((8, 128), F32)),
        in_specs=(HBM_SPEC, HBM_SPEC) + (pl.BlockSpec(memory_space=pl.ANY),) * len(order),
        out_specs=(SEM_SPEC, SEM_SPEC, HBM_SPEC, HBM_SPEC, pl.BlockSpec(memory_space=pltpu.VMEM)),
        input_output_aliases={0: 2, 1: 3},
        compiler_params=pltpu.CompilerParams(has_side_effects=SIDE_EFFECT),
    )(pltpu.with_memory_space_constraint(src, pltpu.HBM),
      pltpu.with_memory_space_constraint(lax.empty((N_DEV,) + slot, src.dtype), pltpu.HBM), *order)


def _spread_wait(handles, after, per_peer, name):
    send_sems, recv_sems, src_thru, land_thru = handles

    def body(src_ref, land_ref, send_sems, recv_sems, after_ref, src_dead, got_ref):
        _, peers = _peers()
        for k, peer, peer_idx in peers:
            copy = pltpu.make_async_remote_copy(
                src_ref=src_ref.at[peer_idx] if per_peer else src_ref, dst_ref=land_ref.at[peer_idx],
                send_sem=send_sems.at[k], recv_sem=recv_sems.at[k],
                device_id=peer, device_id_type=pl.DeviceIdType.MESH)
            copy.wait_send()
            copy.wait_recv()

    return pl.pallas_call(
        body, name=name,
        out_shape=(pltpu.HBM(src_thru.shape, src_thru.dtype), pltpu.HBM(land_thru.shape, land_thru.dtype)),
        in_specs=(HBM_SPEC, HBM_SPEC, SEM_SPEC, SEM_SPEC, pl.BlockSpec(memory_space=pl.ANY)),
        out_specs=(HBM_SPEC, HBM_SPEC), input_output_aliases={0: 0, 1: 1},
        compiler_params=pltpu.CompilerParams(has_side_effects=SIDE_EFFECT),
    )(src_thru, land_thru, send_sems, recv_sems, after)


def _sum_slots_call(slots):
    _, rows, cols = slots.shape
    tr = _row_tile_ragged(rows, cols, budget=512 * 1024)

    def body(s_ref, o_ref):
        acc = s_ref[0].astype(F32)
        for j in range(1, N_DEV):
            acc = acc + s_ref[j].astype(F32)
        o_ref[...] = acc

    return pl.pallas_call(
        body, name="sum_slots", grid=(pl.cdiv(rows, tr),),
        in_specs=[pl.BlockSpec((N_DEV, tr, cols), lambda i: (0, i, 0))],
        out_specs=pl.BlockSpec((tr, cols), lambda i: (i, 0)),
        out_shape=jax.ShapeDtypeStruct((rows, cols), F32),
        compiler_params=pltpu.CompilerParams(dimension_semantics=("parallel",)),
    )(slots)


def _sum_adamw_call(got, own, w, m, v):
    rows, cols = w.shape
    tr = _row_tile_ragged(rows, cols, budget=512 * 1024)
    c1 = 1.0 / (1.0 - ADAM_B1 ** ADAM_STEP)
    c2 = 1.0 / (1.0 - ADAM_B2 ** ADAM_STEP)

    def body(got_ref, own_ref, w_ref, m_ref, v_ref, g_ref, d_ref, nm_ref, nv_ref):
        mine = _place_index(*_my_place())
        gv = jnp.zeros(w_ref.shape, F32)
        for j in range(N_DEV):
            gv = gv + jnp.where(mine == j, own_ref[...], got_ref[j]).astype(F32)
        nm = ADAM_B1 * m_ref[...] + (1.0 - ADAM_B1) * gv
        nv = ADAM_B2 * v_ref[...] + (1.0 - ADAM_B2) * (gv * gv)
        g_ref[...] = gv
        nm_ref[...] = nm
        nv_ref[...] = nv
        d_ref[...] = -ADAM_LR * ((nm * c1) / (jnp.sqrt(nv * c2) + ADAM_EPS) + ADAM_WD * w_ref[...])

    spec = pl.BlockSpec((tr, cols), lambda i: (i, 0))
    shape = jax.ShapeDtypeStruct((rows, cols), F32)
    return pl.pallas_call(
        body, name="sum_adamw", grid=(pl.cdiv(rows, tr),),
        in_specs=[pl.BlockSpec((N_DEV, tr, cols), lambda i: (0, i, 0))] + [spec] * 4,
        out_specs=[spec] * 4, out_shape=[shape] * 4,
        compiler_params=pltpu.CompilerParams(dimension_semantics=("parallel",)),
    )(got, own, w, m, v)


def _with_own_slot(got, own, mine):
    return lax.dynamic_update_index_in_dim(got, own, mine, 0)


def _pack(vectors, width):
    flat = jnp.concatenate([v.reshape(-1) for v in vectors])
    return jnp.pad(flat, (0, width - flat.shape[0])).reshape(width // 128, 128)


def _unpack(packed, like):
    flat = packed.reshape(-1)
    out, at = [], 0
    for v in like:
        out.append(flat[at:at + v.size].reshape(v.shape))
        at += v.size
    return tuple(out)


def _cols_from_slots(slots):
    n, rows, cols = slots.shape
    return slots.transpose(1, 0, 2).reshape(rows, n * cols)


def _rows_from_slots(slots):
    return slots.reshape(-1, slots.shape[2])


def _pad128(n):
    return -(-n // 128) * 128


def _pad_to_tiles(a, axis):
    n = a.shape[axis]
    pads = [(0, 0)] * a.ndim
    pads[axis] = (0, _pad128(n) - n)
    return jnp.pad(a, pads)


def _rwkv_group(take, zeros, rw, dl, al, gl):
    at = 3 * rw
    parts = take(0, at)
    for n in (dl, al, gl):
        parts += take(at, at + n)
        if _pad128(n) > n:
            parts.append(zeros(_pad128(n) - n))
        at += n
    return parts


def _in_proj_layout(slots, rw, fw, dl, al, gl, whole):
    n_slots, rows, d = slots.shape
    wt = slots.reshape(n_slots * rows, d)
    take = lambda lo, hi: [wt[lo:hi]]
    zeros = lambda n: jnp.zeros((n, d), wt.dtype)
    rcols = 3 * rw + dl + al + gl
    fcols = 3 * fw + fw // HEAD_DIM
    group_r = _rwkv_group(take, zeros, rw, dl, al, gl)
    group_f = take(rcols, rcols + fcols) + ([zeros(_pad128(fcols) - fcols)] if _pad128(fcols) > fcols else [])
    group_g = take(rcols + fcols, n_slots * rows)
    if whole:
        return jnp.concatenate(group_r + group_f + group_g, axis=0)
    return tuple(jnp.concatenate(g, axis=0) for g in (group_r, group_f, group_g))


def _low_rank_layout(slots):
    return _pad_to_tiles(_cols_from_slots(slots), 0)


def _stage_embed(meta, x, n1, lp):
    h0 = jnp.concatenate([meta, x, jnp.zeros((lp - meta.shape[0] - x.shape[0], x.shape[1]), F32)], axis=0)
    return h0, rmsnorm(h0, n1)


def _stage_mix(z_r, z_f, small, w2, a2, g2, dims):
    (mu, w0, a0, k_k, k_a, r_k, gn_w, gn_b, q_g, k_g, f_bias) = small
    rw, fw, dl, al, gl = dims
    fcols = 3 * fw + fw // HEAD_DIM

    mu_group = jnp.concatenate(_rwkv_group(lambda lo, hi: [mu[:, lo:hi]], lambda n: jnp.zeros((1, n), F32), rw, dl, al, gl), axis=1)
    r, lw, kf, v, na, b, g = rwkv_prep(z_r, mu_group, w0, a0, k_k, k_a, w2, a2, g2)
    y = wkv7(r, lw, kf, v, na, b)
    y_a = gn_bonus(y, r, kf, v, g, gn_w, gn_b, r_k.reshape(1, rw))

    fq, fk, fv, fl = z_f[:, :fw], z_f[:, fw:2 * fw], z_f[:, 2 * fw:3 * fw], z_f[:, 3 * fw:fcols]
    fq = head_rms(fq, jnp.tile(q_g, (1, fw // HEAD_DIM))) * (HEAD_DIM ** -0.5)
    fk = head_rms(fk, jnp.tile(k_g, (1, fw // HEAD_DIM)))
    cum = jnp.cumsum(jax.nn.log_sigmoid(badd(fl, f_bias)), axis=0)
    y_b = fox_attention(fq, fk, fv, cum)
    return y_a, y_b


def _stage_merge(h0, y_a, y_b, z_g, w_a, w_b, w_o):
    merged = gated_merge(z_g, dense_cols_bf16(y_a, w_a), dense_cols_bf16(y_b, w_b))
    return h0 + dense(merged, w_o)


def _stage_ffn(h1, n2, w_gu, w_dn):
    return h1 + dense(swiglu(dense_cols_bf16(rmsnorm(h1, n2), w_gu)), w_dn)


SHARDED = ("meta_tokens", "w_in", "rwkv_w2", "rwkv_a2", "rwkv_g2", "w_branch_a", "w_branch_b", "w_o", "w_gate_up", "w_down")
SMALL = ("norm1_g", "rwkv_mu", "rwkv_w0", "rwkv_a0", "rwkv_k_k", "rwkv_k_a", "rwkv_r_k", "rwkv_gn_w", "rwkv_gn_b",
         "fox_q_norm_g", "fox_k_norm_g", "fox_f_bias", "norm2_g")
WEIGHTS = ("meta_tokens", "norm1_g", "w_in", "rwkv_mu", "rwkv_w0", "rwkv_w2", "rwkv_a0", "rwkv_a2", "rwkv_g2", "rwkv_k_k",
           "rwkv_k_a", "rwkv_r_k", "rwkv_gn_w", "rwkv_gn_b", "fox_q_norm_g", "fox_k_norm_g", "fox_f_bias", "w_branch_a",
           "w_branch_b", "w_o", "norm2_g", "w_gate_up", "w_down")


def _as2d(a):
    return a.reshape(-1, a.shape[-1])


def kernel(x, meta_tokens, norm1_g, w_in, rwkv_mu, rwkv_w0, rwkv_w2, rwkv_a0, rwkv_a2, rwkv_g2, rwkv_k_k, rwkv_k_a, rwkv_r_k, rwkv_gn_w, rwkv_gn_b, fox_q_norm_g, fox_k_norm_g, fox_f_bias, w_branch_a, w_branch_b, w_o, norm2_g, w_gate_up, w_down, loss_target, m_meta_tokens, m_norm1_g, m_w_in, m_rwkv_mu, m_rwkv_w0, m_rwkv_w2, m_rwkv_a0, m_rwkv_a2, m_rwkv_g2, m_rwkv_k_k, m_rwkv_k_a, m_rwkv_r_k, m_rwkv_gn_w, m_rwkv_gn_b, m_fox_q_norm_g, m_fox_k_norm_g, m_fox_f_bias, m_w_branch_a, m_w_branch_b, m_w_o, m_norm2_g, m_w_gate_up, m_w_down, v_meta_tokens, v_norm1_g, v_w_in, v_rwkv_mu, v_rwkv_w0, v_rwkv_w2, v_rwkv_a0, v_rwkv_a2, v_rwkv_g2, v_rwkv_k_k, v_rwkv_k_a, v_rwkv_r_k, v_rwkv_gn_w, v_rwkv_gn_b, v_fox_q_norm_g, v_fox_k_norm_g, v_fox_f_bias, v_w_branch_a, v_w_branch_b, v_w_o, v_norm2_g, v_w_gate_up, v_w_down):
    given = dict(locals())
    w = {n: given[n] for n in WEIGHTS}
    assert rwkv_r_k.shape[-1] == HEAD_DIM
    n_meta, seq = meta_tokens.shape[0], x.shape[1]
    tokens = n_meta + seq
    lp = -(-tokens // TOKEN_TILE) * TOKEN_TILE
    mine = _place_index(*(lax.axis_index(a) for a in MESH_AXES))
    x2 = x[0]

    local = {n: _as2d(given[n]) for n in given if n != "x" and n != "loss_target"}
    for n in ("w_in", "m_w_in", "v_w_in"):
        local[n] = jnp.transpose(given[n][0])
    blocks = {n: local[n].astype(F32 if n == "meta_tokens" else BF16) for n in SHARDED}
    first = ("meta_tokens", "rwkv_w2", "rwkv_a2", "rwkv_g2")
    started = {n: _spread_start(blocks[n], False, "gather_start_" + n) for n in first}
    zero = sum(started[n][4][0, 0] for n in first)

    def gathered(n, after):
        own, got = _spread_wait(started[n][:4], after, False, "gather_wait_" + n)
        return _with_own_slot(got, own, mine)

    sm = {n: _as2d(w[n]) for n in SMALL}
    small_mix = tuple(sm[n] for n in SMALL[1:-1])
    n1 = sm["norm1_g"] + zero
    rw, fw = w_branch_a.shape[-2], w_branch_b.shape[-2]
    dims = (rw, fw, rwkv_w2.shape[-2], rwkv_a2.shape[-2], rwkv_g2.shape[-2])
    same = lambda s: (s,)

    meta, un_meta = jax.vjp(_cols_from_slots, gathered("meta_tokens", x2))
    (h0, xn), vjp_embed = jax.vjp(lambda m, xs, g: _stage_embed(m, xs, g, lp), meta, x2, n1)
    in_slots = _all_gather_call(blocks["w_in"])
    later = [n for n in SHARDED if n not in first and n != "w_in"]
    started.update({n: _spread_start(blocks[n], False, "gather_start_" + n, after=in_slots) for n in later})
    w_groups = _in_proj_layout(in_slots, *dims, whole=False)
    w_cat, un_in = jax.vjp(lambda s: _in_proj_layout(s, *dims, whole=True), in_slots)
    xn_b = xn.astype(BF16)
    behind = sum(started[n][4] for n in later)
    z_r, z_f, z_g = (_matmul(xn_b, wg, tb=True, name="in_proj_" + tag, after=behind) for wg, tag in zip(w_groups, "rfg"))
    (w2, un_w2), (a2, un_a2), (g2, un_g2) = (jax.vjp(_low_rank_layout, gathered(n, xn)) for n in ("rwkv_w2", "rwkv_a2", "rwkv_g2"))
    (y_a, y_b), vjp_mix = jax.vjp(lambda zr, zf, s, a, b, c: _stage_mix(zr, zf, s, a, b, c, dims),
                                  z_r, z_f, small_mix, w2, a2, g2)
    w_a, w_b = gathered("w_branch_a", y_a), gathered("w_branch_b", y_a)
    w_o_full, un_wo = jax.vjp(_rows_from_slots, gathered("w_o", y_a))
    h1, vjp_merge = jax.vjp(_stage_merge, h0, y_a, y_b, z_g, w_a, w_b, w_o_full)
    w_gu = gathered("w_gate_up", h1)
    w_dn, un_dn = jax.vjp(_rows_from_slots, gathered("w_down", h1))
    y, vjp_ffn = jax.vjp(_stage_ffn, h1, sm["norm2_g"], w_gu, w_dn)

    loss_part, dy_real = _loss_call(y[n_meta:tokens], loss_target[0])
    dy = jnp.pad(dy_real, ((n_meta, lp - tokens), (0, 0)))
    loss = lax.psum(loss_part[0, 0], MESH_AXES)

    sent = {}

    def send_grad(n, dmat, unlayout):
        sent[n] = _spread_start(unlayout(dmat)[0], True, "grad_start_" + n)
        return sent[n][4][0, 0]

    d_h1, d_n2, d_wgu, d_wdn = vjp_ffn(dy)
    behind = send_grad("w_gate_up", d_wgu, same) + send_grad("w_down", d_wdn, un_dn)
    d_h0, d_ya, d_yb, d_zg, d_wa, d_wb, d_wo = vjp_merge(d_h1 + behind)
    behind = send_grad("w_o", d_wo, un_wo) + send_grad("w_branch_a", d_wa, same) + send_grad("w_branch_b", d_wb, same)
    d_zr, d_zf, d_small_mix, d_w2, d_a2, d_g2 = vjp_mix((d_ya + behind, d_yb))
    dproj_b = jnp.concatenate([d_zr.astype(BF16), d_zf.astype(BF16), d_zg.astype(BF16)], axis=1)
    d_wcat = _matmul(dproj_b, xn_b, ta=True, out_dtype=BF16, name="in_proj_dw")
    send_grad("w_in", d_wcat, un_in)
    d_xn = _matmul(dproj_b, w_cat, out_dtype=F32, name="in_proj_dx", after=sent["w_in"][4])
    send_grad("rwkv_w2", d_w2, un_w2)
    send_grad("rwkv_a2", d_a2, un_a2)
    send_grad("rwkv_g2", d_g2, un_g2)
    d_meta, g_x, d_n1 = vjp_embed((d_h0, d_xn))
    send_grad("meta_tokens", d_meta, un_meta)

    small_grads = (d_n1, *d_small_mix, d_n2)
    n_small = sum(g.size for g in small_grads)
    width = -(-n_small // 1024) * 1024
    small_sent = _spread_start(_pack(small_grads, width), False, "small_grad_start")

    grads, delta, new_m, new_v = {}, {}, {}, {}
    after = g_x
    for n in ("w_gate_up", "w_down", "w_o", "w_branch_a", "w_branch_b", "rwkv_g2", "rwkv_a2", "rwkv_w2", "meta_tokens", "w_in"):
        src, got = _spread_wait(sent[n][:4], after, True, "grad_wait_" + n)
        own = lax.dynamic_index_in_dim(src, mine, 0, keepdims=False)
        g, d_, m_, v_ = _sum_adamw_call(got, own, local[n], local["m_" + n], local["v_" + n])
        back = (lambda t: jnp.transpose(t)[None]) if n == "w_in" else (lambda t: t.reshape(w[n].shape))
        grads[n], delta[n], new_m[n], new_v[n] = (back(t) for t in (g, d_, m_, v_))
        after = m_
    own_small, got_small = _spread_wait(small_sent[:4], after, False, "small_grad_wait")
    small_total = _unpack(_sum_slots_call(_with_own_slot(got_small, own_small, mine)), small_grads)
    grads.update({n: g.reshape(w[n].shape) for n, g in zip(SMALL, small_total)})
    packs = [_pack([src[n] if p == "" else given[p + n] for n in SMALL], width)
             for p, src in (("", w), ("", grads), ("m_", None), ("v_", None))]
    like = [w[n] for n in SMALL]
    for out, packed in zip((delta, new_m, new_v), _adamw_call(*packs)):
        out.update(dict(zip(SMALL, _unpack(packed, like))))

    return (loss, g_x[None], *[grads[n] for n in WEIGHTS], *[delta[n] for n in WEIGHTS],
            *[new_m[n] for n in WEIGHTS], *[new_v[n] for n in WEIGHTS])
```

```python
import jax
import jax.numpy as jnp
from jax import lax
from jax.experimental import pallas as pl
from jax.experimental.pallas import tpu as pltpu

F32 = jnp.float32
BF16 = jnp.bfloat16

N_DEV = 8
MESH_AXES = ("x", "y", "c")
HEAD_DIM = 64
TOKEN_TILE = 128
WKV_CHUNK = 64
WKV_PAIRS_PER_STEP = 8
PAIR = 2 * HEAD_DIM
ATTN_BLOCK = 128
ATTN_BLOCK_BIG = 384
RMS_EPS = 1e-6
GN_EPS = 64e-5
L2_FLOOR = 1e-12
NEG_BIG = -1e30
ADAM_LR, ADAM_B1, ADAM_B2, ADAM_EPS, ADAM_WD, ADAM_STEP = 0.001, 0.9, 0.999, 1e-08, 0.01, 10
VMEM_LIMIT_CAP = 56 * 1024 * 1024
VMEM_LIMIT_FLOOR = 32 * 1024 * 1024
MATMUL_VMEM_BUDGET = 36 * 1024 * 1024
GRID_STEP_BYTES = 1024 * 1024
ACC_BYTES_PER_HBM_BYTE = 6


def _vmem_limit(estimate_bytes):
    return int(min(max(estimate_bytes * 5 // 4, VMEM_LIMIT_FLOOR), VMEM_LIMIT_CAP))


def _row_tile(rows, width, itemsize=4, budget=2 * 1024 * 1024):
    for c in (1408, 1024, 704, 512, 384, 256, 128, 64, 32, 16, 8):
        if rows % c == 0 and c * width * itemsize <= budget:
            return c
    return rows


def _row_tile_ragged(rows, width, itemsize=4, budget=2 * 1024 * 1024):
    tile = _row_tile(rows, width, itemsize, budget)
    if tile * width * itemsize <= budget or rows < 16:
        return tile
    padded = -(-rows // 16) * 16
    for c in (1408, 1024, 704, 512, 384, 336, 256, 192, 128, 96, 64, 48, 32, 16):
        if padded % c == 0 and c * width * itemsize <= budget:
            return c
    return tile


def _dg(a, b, ta, tb):
    dims = (((0 if ta else 1,), (1 if tb else 0,)), ((), ()))
    return lax.dot_general(a, b, dims, preferred_element_type=F32)


def _split(x, n):
    parts = []
    for _ in range(n):
        h = x.astype(BF16)
        parts.append(h)
        x = x - h.astype(F32)
    return parts


def _mm(a, b, ta=False, tb=False):
    return _dg(a.astype(BF16), b.astype(BF16), ta, tb)


def _matmul(a, b, ta=False, tb=False, out_dtype=F32, name="matmul", after=None, b_slots=False, out_slots=0, add=None):
    if ta:
        kdim, m = a.shape
    else:
        m, kdim = a.shape
    if b_slots:
        n_slots, brows, bcols = b.shape
        n, k2 = (brows, n_slots * bcols) if tb else (n_slots * bcols, brows)
    elif tb:
        n, k2 = b.shape
    else:
        k2, n = b.shape
    assert kdim == k2, (a.shape, b.shape, ta, tb)
    sa, sb, so = a.dtype.itemsize, b.dtype.itemsize, jnp.dtype(out_dtype).itemsize
    n_unit = bcols if (b_slots and not tb) else (n // out_slots if out_slots else n)
    k_unit = bcols if (b_slots and tb) else kdim
    tm, tn, tk, n_outer = _matmul_tiles(m, n, kdim, ta, sa, sb, so, n_unit, k_unit)
    nk = kdim // tk
    ij = (lambda f: lambda j, i, k: f(i, j, k)) if n_outer else (lambda f: f)

    order = () if after is None else (after,)
    extra = () if add is None else (add,)

    def body(a_ref, b_ref, *rest):
        rest = rest[len(order):]
        add_ref = rest[0] if extra else None
        o_ref, acc = rest[len(extra)], rest[len(extra) + 1:]
        part = _dg(a_ref[...].astype(BF16), b_ref[...].astype(BF16), ta, tb)
        done = lambda total: (total if add_ref is None else total + add_ref[...]).astype(o_ref.dtype)
        if nk == 1:
            o_ref[...] = done(part)
            return
        kk = pl.program_id(2)

        @pl.when(kk == 0)
        def _():
            acc[0][...] = part

        @pl.when(kk > 0)
        def _():
            acc[0][...] += part

        @pl.when(kk == nk - 1)
        def _():
            o_ref[...] = done(acc[0][...])

    a_spec = pl.BlockSpec((tk, tm), ij(lambda i, j, k: (k, i))) if ta else pl.BlockSpec((tm, tk), ij(lambda i, j, k: (i, k)))
    if b_slots and tb:
        per = bcols // tk
        b_spec = pl.BlockSpec((None, tn, tk), ij(lambda i, j, k: (k // per, j, k % per)))
    elif b_slots:
        per = bcols // tn
        b_spec = pl.BlockSpec((None, tk, tn), ij(lambda i, j, k: (j // per, k, j % per)))
    elif tb:
        b_spec = pl.BlockSpec((tn, tk), ij(lambda i, j, k: (j, k)))
    else:
        b_spec = pl.BlockSpec((tk, tn), ij(lambda i, j, k: (k, j)))
    if out_slots:
        per_out = n // out_slots // tn
        out_spec = pl.BlockSpec((None, tm, tn), ij(lambda i, j, k: (j // per_out, i, j % per_out)))
        out_shape = jax.ShapeDtypeStruct((out_slots, m, n // out_slots), out_dtype)
    else:
        out_spec = pl.BlockSpec((tm, tn), ij(lambda i, j, k: (i, j)))
        out_shape = jax.ShapeDtypeStruct((m, n), out_dtype)
    return pl.pallas_call(
        body, name=name,
        grid=(n // tn, m // tm, nk) if n_outer else (m // tm, n // tn, nk),
        in_specs=[a_spec, b_spec] + [pl.BlockSpec(memory_space=pl.ANY)] * len(order)
                 + [pl.BlockSpec((tm, tn), ij(lambda i, j, k: (i, j)))] * len(extra),
        out_specs=out_spec,
        out_shape=out_shape,
        scratch_shapes=[pltpu.VMEM((tm, tn), F32)] if nk > 1 else [],
        compiler_params=pltpu.CompilerParams(
            dimension_semantics=("parallel", "parallel", "arbitrary"),
            vmem_limit_bytes=_vmem_limit(_matmul_vmem(tm, tn, tk, nk, sa, sb, so) + 2 * tm * tn * 4 * len(extra))),
    )(a, b, *order, *extra)


def _matmul_vmem(tm, tn, tk, nk, sa, sb, so):
    return 2 * (tm * tk * sa + tk * tn * sb + tm * tn * so) + tm * tn * 4 + (tm * tn * 4 if nk > 1 else 0)


def _matmul_tiles(m, n, kdim, ta, sa, sb, so, n_unit, k_unit):
    lane = (2816, 2176, 2048, 1408, 1024, 640, 512, 384, 256, 128)
    sublane = (2816, 2176, 2048, 1408, 1024, 704, 512, 384, 256, 128)
    divs = lambda dim, cands: [c for c in cands if dim % c == 0] or [dim]
    best = None
    for tm in divs(m, lane if ta else sublane):
        for tn in divs(n_unit, lane):
            for tk in divs(k_unit, sublane if ta else lane) + ([kdim] if k_unit == kdim and (ta or kdim <= 2048) else []):
                nk, nm, nn = kdim // tk, m // tm, n // tn
                if _matmul_vmem(tm, tn, tk, nk, sa, sb, so) > MATMUL_VMEM_BUDGET:
                    continue
                acc_bytes = m * n * 4 * 3 * nk // ACC_BYTES_PER_HBM_BYTE if nk > 1 else 0
                fixed = m * n * so + acc_bytes + nm * nn * nk * GRID_STEP_BYTES
                for n_outer in (False, True):
                    if n_outer:
                        a_reads, b_reads = (1 if (nk == 1 and nm == 1) else nn), (1 if nk == 1 else nm)
                    else:
                        a_reads, b_reads = (1 if nk == 1 else nn), (1 if (nk == 1 and nn == 1) else nm)
                    cost = m * kdim * sa * a_reads + kdim * n * sb * b_reads + fixed
                    if best is None or cost < best[0]:
                        best = (cost, tm, tn, tk, n_outer)
    return best[1:]


@jax.custom_vjp
def dense(x, w):
    return _matmul(x.astype(BF16), w, name="dense_fwd")


def _dense_fwd(x, w):
    return _matmul(x.astype(BF16), w, name="dense_fwd"), (x.astype(BF16), w, jnp.zeros((), x.dtype))


def _dense_bwd(res, dy):
    xb, w, like = res
    dyb = dy.astype(BF16)
    dx = _matmul(dyb, w, tb=True, out_dtype=like.dtype, name="dense_dx")
    dw = _matmul(xb, dyb, ta=True, out_dtype=w.dtype, name="dense_dw")
    return dx, dw


dense.defvjp(_dense_fwd, _dense_bwd)


@jax.custom_vjp
def dense_add(x, w, res):
    return _matmul(x.astype(BF16), w, name="dense_add_fwd", add=res)


def _dense_add_fwd(x, w, res):
    return _matmul(x.astype(BF16), w, name="dense_add_fwd", add=res), (x.astype(BF16), w, jnp.zeros((), x.dtype))


def _dense_add_bwd(res, dy):
    return (*_dense_bwd(res, dy), dy)


dense_add.defvjp(_dense_add_fwd, _dense_add_bwd)


def _make_dense_cols(out_dtype):
    @jax.custom_vjp
    def op(x, w_slots):
        return _matmul(x.astype(BF16), w_slots, b_slots=True, out_dtype=out_dtype, name="dense_cols_fwd")

    def fwd(x, w_slots):
        assert x.dtype == F32
        xb = x.astype(BF16)
        return _matmul(xb, w_slots, b_slots=True, out_dtype=out_dtype, name="dense_cols_fwd"), (xb, w_slots)

    def bwd(res, dy):
        xb, w_slots = res
        dyb = dy.astype(BF16)
        dx = _matmul(dyb, w_slots, tb=True, b_slots=True, out_dtype=F32, name="dense_cols_dx")
        dw = _matmul(xb, dyb, ta=True, out_slots=w_slots.shape[0], out_dtype=w_slots.dtype, name="dense_cols_dw")
        return dx, dw

    op.defvjp(fwd, bwd)
    return op


dense_cols_bf16 = _make_dense_cols(BF16)


def _swiglu_call(gu, d_act=None):
    rows, two_f = gu.shape
    f = two_f // 2
    tr = _row_tile(rows, two_f, itemsize=2, budget=3 * 1024 * 1024)
    half = lambda j: pl.BlockSpec((tr, f), lambda i, j=j: (i, j))
    ops = (gu, gu) if d_act is None else (gu, gu, d_act)

    def body(*refs):
        g, u = refs[0][...].astype(F32), refs[1][...].astype(F32)
        s = 1.0 / (1.0 + jnp.exp(-g))
        if d_act is None:
            refs[2][...] = (g * s * u).astype(BF16)
        else:
            d = refs[2][...].astype(F32)
            refs[3][:, :f] = (d * u * s * (1.0 + g * (1.0 - s))).astype(BF16)
            refs[3][:, f:] = (d * g * s).astype(BF16)

    width = f if d_act is None else two_f
    return pl.pallas_call(
        body, name="swiglu_fwd" if d_act is None else "swiglu_bwd", grid=(rows // tr,),
        in_specs=[half(0), half(1)] + ([half(0)] if d_act is not None else []),
        out_specs=pl.BlockSpec((tr, width), lambda i: (i, 0)),
        out_shape=jax.ShapeDtypeStruct((rows, width), BF16),
        compiler_params=pltpu.CompilerParams(dimension_semantics=("parallel",)),
    )(*ops)


@jax.custom_vjp
def swiglu(gu):
    return _swiglu_call(gu)


swiglu.defvjp(lambda gu: (_swiglu_call(gu), gu), lambda gu, d_act: (_swiglu_call(gu, d_act),))


def _merge_call(zg, a, b, dm=None):
    rows, d = a.shape
    tr = _row_tile(rows, d, budget=1024 * 1024)
    half = lambda j: pl.BlockSpec((tr, d), lambda i, j=j: (i, j))
    tile = half(0)

    def body(*refs):
        ga = 1.0 / (1.0 + jnp.exp(-refs[0][...]))
        gb = 1.0 / (1.0 + jnp.exp(-refs[1][...]))
        av, bv = refs[2][...].astype(F32), refs[3][...].astype(F32)
        if dm is None:
            refs[4][...] = (ga * av + gb * bv).astype(BF16)
        else:
            dv = refs[4][...].astype(F32)
            dzg_ref, da_ref, db_ref = refs[5:]
            dzg_ref[:, :d] = dv * av * ga * (1.0 - ga)
            dzg_ref[:, d:] = dv * bv * gb * (1.0 - gb)
            da_ref[...] = (dv * ga).astype(BF16)
            db_ref[...] = (dv * gb).astype(BF16)

    shape_b = jax.ShapeDtypeStruct((rows, d), BF16)
    if dm is None:
        out_specs, out_shape, ops = tile, shape_b, (zg, zg, a, b)
    else:
        out_specs = [pl.BlockSpec((tr, 2 * d), lambda i: (i, 0)), tile, tile]
        out_shape = [jax.ShapeDtypeStruct((rows, 2 * d), F32), shape_b, shape_b]
        ops = (zg, zg, a, b, dm)
    return pl.pallas_call(
        body, name="merge_fwd" if dm is None else "merge_bwd", grid=(rows // tr,),
        in_specs=[half(0), half(1)] + [tile] * (len(ops) - 2),
        out_specs=out_specs, out_shape=out_shape,
        compiler_params=pltpu.CompilerParams(dimension_semantics=("parallel",)),
    )(*ops)


@jax.custom_vjp
def gated_merge(zg, a, b):
    return _merge_call(zg, a, b)


gated_merge.defvjp(lambda zg, a, b: (_merge_call(zg, a, b), (zg, a, b)),
                   lambda res, dm: tuple(_merge_call(*res, dm)))


def _rms_fwd_call(x, g):
    rows, d = x.shape
    tr = _row_tile(rows, d)

    def body(x_ref, g_ref, y_ref):
        xv = x_ref[...]
        rstd = lax.rsqrt(jnp.mean(xv * xv, axis=1, keepdims=True) + RMS_EPS)
        y_ref[...] = (xv * rstd) * g_ref[...]

    return pl.pallas_call(
        body, name="rms_fwd", grid=(rows // tr,),
        in_specs=[pl.BlockSpec((tr, d), lambda i: (i, 0)), pl.BlockSpec((1, d), lambda i: (0, 0))],
        out_specs=pl.BlockSpec((tr, d), lambda i: (i, 0)),
        out_shape=jax.ShapeDtypeStruct((rows, d), F32),
        compiler_params=pltpu.CompilerParams(dimension_semantics=("parallel",)),
    )(x, g)


def _rms_bwd_call(x, g, dy):
    rows, d = x.shape
    tr = _row_tile(rows, d)

    def body(x_ref, g_ref, dy_ref, dx_ref, dg_ref):
        @pl.when(pl.program_id(0) == 0)
        def _():
            dg_ref[...] = jnp.zeros_like(dg_ref)

        xv = x_ref[...]
        dyv = dy_ref[...]
        rstd = lax.rsqrt(jnp.mean(xv * xv, axis=1, keepdims=True) + RMS_EPS)
        xhat = xv * rstd
        dxhat = dyv * g_ref[...]
        dx_ref[...] = rstd * (dxhat - xhat * jnp.mean(dxhat * xhat, axis=1, keepdims=True))
        dg_ref[...] += jnp.sum(dyv * xhat, axis=0, keepdims=True)

    return pl.pallas_call(
        body, name="rms_bwd", grid=(rows // tr,),
        in_specs=[pl.BlockSpec((tr, d), lambda i: (i, 0)), pl.BlockSpec((1, d), lambda i: (0, 0)),
                  pl.BlockSpec((tr, d), lambda i: (i, 0))],
        out_specs=[pl.BlockSpec((tr, d), lambda i: (i, 0)), pl.BlockSpec((1, d), lambda i: (0, 0))],
        out_shape=[jax.ShapeDtypeStruct((rows, d), F32), jax.ShapeDtypeStruct((1, d), F32)],
        compiler_params=pltpu.CompilerParams(dimension_semantics=("arbitrary",)),
    )(x, g, dy)


@jax.custom_vjp
def rmsnorm(x, g):
    return _rms_fwd_call(x, g)


rmsnorm.defvjp(lambda x, g: (_rms_fwd_call(x, g), (x, g)), lambda res, dy: tuple(_rms_bwd_call(res[0], res[1], dy)))


def _bcast_add_call(x, p):
    rows, d = x.shape
    tr = _row_tile(rows, d)

    def body(x_ref, p_ref, y_ref):
        y_ref[...] = x_ref[...] + p_ref[...]

    return pl.pallas_call(
        body, name="bcast_add", grid=(rows // tr,),
        in_specs=[pl.BlockSpec((tr, d), lambda i: (i, 0)), pl.BlockSpec((1, d), lambda i: (0, 0))],
        out_specs=pl.BlockSpec((tr, d), lambda i: (i, 0)),
        out_shape=jax.ShapeDtypeStruct((rows, d), F32),
        compiler_params=pltpu.CompilerParams(dimension_semantics=("parallel",)),
    )(x, p)


def _colsum_call(a):
    rows, d = a.shape
    tr = _row_tile(rows, d)

    def body(a_ref, o_ref):
        @pl.when(pl.program_id(0) == 0)
        def _():
            o_ref[...] = jnp.zeros_like(o_ref)

        o_ref[...] += jnp.sum(a_ref[...], axis=0, keepdims=True)

    return pl.pallas_call(
        body, name="colsum", grid=(rows // tr,),
        in_specs=[pl.BlockSpec((tr, d), lambda i: (i, 0))],
        out_specs=pl.BlockSpec((1, d), lambda i: (0, 0)),
        out_shape=jax.ShapeDtypeStruct((1, d), F32),
        compiler_params=pltpu.CompilerParams(dimension_semantics=("arbitrary",)),
    )(a)


@jax.custom_vjp
def badd(x, p):
    return _bcast_add_call(x, p)


badd.defvjp(lambda x, p: (_bcast_add_call(x, p), None), lambda res, dy: (dy, _colsum_call(dy)))


def _head_sums(x):
    i = lax.broadcasted_iota(jnp.int32, (PAIR, PAIR), 0) // HEAD_DIM
    j = lax.broadcasted_iota(jnp.int32, (PAIR, PAIR), 1) // HEAD_DIM
    ones = jnp.where(i == j, 1.0, 0.0).astype(BF16)
    hi, lo = _split(x, 2)
    cols = [slice(p * PAIR, (p + 1) * PAIR) for p in range(x.shape[1] // PAIR)]
    return jnp.concatenate([_dg(hi[:, c], ones, False, False) + _dg(lo[:, c], ones, False, False) for c in cols], axis=1)


def _head_rms_fwd_call(x, g):
    rows, w = x.shape
    tr = _row_tile(rows, w, budget=1024 * 1024)

    def body(x_ref, g_ref, y_ref):
        xv = x_ref[...]
        rstd = lax.rsqrt(_head_sums(xv * xv) * (1.0 / HEAD_DIM) + RMS_EPS)
        y_ref[...] = (xv * rstd) * g_ref[...]

    return pl.pallas_call(
        body, name="head_rms_fwd", grid=(rows // tr,),
        in_specs=[pl.BlockSpec((tr, w), lambda i: (i, 0)), pl.BlockSpec((1, w), lambda i: (0, 0))],
        out_specs=pl.BlockSpec((tr, w), lambda i: (i, 0)),
        out_shape=jax.ShapeDtypeStruct((rows, w), F32),
        compiler_params=pltpu.CompilerParams(dimension_semantics=("parallel",)),
    )(x, g)


def _head_rms_bwd_call(x, g, dy):
    rows, w = x.shape
    tr = _row_tile(rows, w, budget=1024 * 1024)

    def body(x_ref, g_ref, dy_ref, dx_ref, dg_ref):
        @pl.when(pl.program_id(0) == 0)
        def _():
            dg_ref[...] = jnp.zeros_like(dg_ref)

        xv, dyv = x_ref[...], dy_ref[...]
        rstd = lax.rsqrt(_head_sums(xv * xv) * (1.0 / HEAD_DIM) + RMS_EPS)
        xhat = xv * rstd
        dxhat = dyv * g_ref[...]
        dx_ref[...] = rstd * (dxhat - xhat * (_head_sums(dxhat * xhat) * (1.0 / HEAD_DIM)))
        dg_ref[...] += jnp.sum(dyv * xhat, axis=0, keepdims=True)

    return pl.pallas_call(
        body, name="head_rms_bwd", grid=(rows // tr,),
        in_specs=[pl.BlockSpec((tr, w), lambda i: (i, 0)), pl.BlockSpec((1, w), lambda i: (0, 0)),
                  pl.BlockSpec((tr, w), lambda i: (i, 0))],
        out_specs=[pl.BlockSpec((tr, w), lambda i: (i, 0)), pl.BlockSpec((1, w), lambda i: (0, 0))],
        out_shape=[jax.ShapeDtypeStruct((rows, w), F32), jax.ShapeDtypeStruct((1, w), F32)],
        compiler_params=pltpu.CompilerParams(dimension_semantics=("arbitrary",)),
    )(x, g, dy)


@jax.custom_vjp
def head_rms(x, g):
    return _head_rms_fwd_call(x, g)


head_rms.defvjp(lambda x, g: (_head_rms_fwd_call(x, g), (x, g)),
                lambda res, dy: tuple(_head_rms_bwd_call(res[0], res[1], dy)))


def _gn_fwd_call(y, r, kf, v, g, gw, gb, rk):
    rows, w = y.shape
    tr = _row_tile(rows, w, budget=512 * 1024)

    def body(y_ref, r_ref, kf_ref, v_ref, g_ref, gw_ref, gb_ref, rk_ref, o_ref):
        yv = y_ref[...]
        yc = yv - _head_sums(yv) * (1.0 / HEAD_DIM)
        rstd = lax.rsqrt(_head_sums(yc * yc) * (1.0 / HEAD_DIM) + GN_EPS)
        s = _head_sums(r_ref[...] * kf_ref[...] * rk_ref[...])
        o_ref[...] = ((yc * rstd) * gw_ref[...] + gb_ref[...] + s * v_ref[...]) * g_ref[...]

    tok = pl.BlockSpec((tr, w), lambda i: (i, 0))
    par = pl.BlockSpec((1, w), lambda i: (0, 0))
    return pl.pallas_call(
        body, name="gn_bonus_fwd", grid=(rows // tr,),
        in_specs=[tok] * 5 + [par] * 3, out_specs=tok,
        out_shape=jax.ShapeDtypeStruct((rows, w), F32),
        compiler_params=pltpu.CompilerParams(dimension_semantics=("parallel",)),
    )(y, r, kf, v, g, gw, gb, rk)


def _gn_bwd_call(y, r, kf, v, g, gw, gb, rk, do):
    rows, w = y.shape
    tr = _row_tile(rows, w, budget=512 * 1024)

    def body(y_ref, r_ref, kf_ref, v_ref, g_ref, gw_ref, gb_ref, rk_ref, do_ref,
             dy_ref, dr_ref, dkf_ref, dv_ref, dg_ref, dgw_ref, dgb_ref, drk_ref):
        @pl.when(pl.program_id(0) == 0)
        def _():
            dgw_ref[...] = jnp.zeros_like(dgw_ref)
            dgb_ref[...] = jnp.zeros_like(dgb_ref)
            drk_ref[...] = jnp.zeros_like(drk_ref)

        yv, rv, kv, vv, rkv = y_ref[...], r_ref[...], kf_ref[...], v_ref[...], rk_ref[...]
        mean = lambda t: _head_sums(t) * (1.0 / HEAD_DIM)
        yc = yv - mean(yv)
        rstd = lax.rsqrt(mean(yc * yc) + GN_EPS)
        yhat = yc * rstd
        s = _head_sums(rv * kv * rkv)
        dg_ref[...] = do_ref[...] * (yhat * gw_ref[...] + gb_ref[...] + s * vv)
        dov = do_ref[...] * g_ref[...]
        dyhat = dov * gw_ref[...]
        dy_ref[...] = rstd * (dyhat - mean(dyhat) - yhat * mean(dyhat * yhat))
        ds = _head_sums(dov * vv)
        dv_ref[...] = s * dov
        dr_ref[...] = ds * kv * rkv
        dkf_ref[...] = ds * rv * rkv
        dgw_ref[...] += jnp.sum(dov * yhat, axis=0, keepdims=True)
        dgb_ref[...] += jnp.sum(dov, axis=0, keepdims=True)
        drk_ref[...] += jnp.sum(ds * rv * kv, axis=0, keepdims=True)

    tok = pl.BlockSpec((tr, w), lambda i: (i, 0))
    par = pl.BlockSpec((1, w), lambda i: (0, 0))
    tshape = jax.ShapeDtypeStruct((rows, w), F32)
    pshape = jax.ShapeDtypeStruct((1, w), F32)
    return pl.pallas_call(
        body, name="gn_bonus_bwd", grid=(rows // tr,),
        in_specs=[tok] * 5 + [par] * 3 + [tok], out_specs=[tok] * 5 + [par] * 3,
        out_shape=[tshape] * 5 + [pshape] * 3,
        compiler_params=pltpu.CompilerParams(dimension_semantics=("arbitrary",)),
    )(y, r, kf, v, g, gw, gb, rk, do)


@jax.custom_vjp
def gn_bonus(y, r, kf, v, g, gw, gb, rk):
    return _gn_fwd_call(y, r, kf, v, g, gw, gb, rk)


def _gn_bwd(res, do):
    return tuple(_gn_bwd_call(*res, do))


gn_bonus.defvjp(lambda *a: (_gn_fwd_call(*a), a), _gn_bwd)


PREP_ROWS = 128


def _prep_segments(rw, lora_w, lora_a, lora_g):
    at = 3 * rw
    seg = {"r": (0, rw), "k": (rw, 2 * rw), "v": (2 * rw, 3 * rw)}
    for name, n in (("wd", lora_w), ("ad", lora_a), ("gd", lora_g)):
        seg[name] = (at, at + _pad128(n))
        at += _pad128(n)
    return seg, at


def _prep_shifted(z_ref, zlast_ref, mu_ref, seg, first_tile):
    lo, hi = seg
    zr = z_ref[:, lo:hi]
    rows = zr.shape[0]
    before = jnp.where(first_tile, 0.0, zlast_ref[7:8, lo:hi])
    row0 = lax.broadcasted_iota(jnp.int32, zr.shape, 0) == 0
    diff = jnp.where(row0, before, pltpu.roll(zr, 1, axis=0)) - zr
    return zr + diff * mu_ref[:, lo:hi], diff


def _prep_forward_values(z_ref, zlast_ref, mu_ref, w0_ref, a0_ref, kk_ref, ka_ref, w2_ref, a2_ref, g2_ref, segs, first_tile):
    z = {n: _prep_shifted(z_ref, zlast_ref, mu_ref, segs[n], first_tile) for n in segs}
    r, k, v, wd, ad, gd = (z[n][0] for n in ("r", "k", "v", "wd", "ad", "gd"))
    twd = jnp.tanh(wd)
    pw = _mm(twd, w2_ref[...]) + w0_ref[...]
    lw = -jnp.exp(-(jnp.maximum(-pw, 0.0) + jnp.log(1.0 + jnp.exp(-jnp.abs(pw)))) - 0.5)
    a_sig = 1.0 / (1.0 + jnp.exp(-(_mm(ad, a2_ref[...]) + a0_ref[...])))
    sg = 1.0 / (1.0 + jnp.exp(-gd))
    kx = k * kk_ref[...]
    nrm = jnp.sqrt(_head_sums(kx * kx))
    inv = 1.0 / jnp.maximum(nrm, L2_FLOOR)
    return dict(z=z, r=r, k=k, v=v, twd=twd, pw=pw, lw=lw, a_sig=a_sig, sg=sg, ad=ad, kk=kx * inv, inv=inv, live=nrm > L2_FLOOR)


def _prep_specs(tokens, rpad, rw, w2, a2, g2):
    tr = PREP_ROWS
    tile = lambda w: pl.BlockSpec((tr, w), lambda i: (i, 0))
    before = pl.BlockSpec((8, rpad), lambda i: (jnp.maximum(i * (tr // 8) - 1, 0), 0))
    whole = lambda a: pl.BlockSpec(a.shape, lambda i: (0, 0))
    par = pl.BlockSpec((1, rw), lambda i: (0, 0))
    return tile, before, whole, par, pl.BlockSpec((1, rpad), lambda i: (0, 0))


def _prep_fwd_call(zr, mu, w0, a0, k_k, k_a, w2, a2, g2):
    tokens, rpad = zr.shape
    rw = w0.shape[1]
    segs, _ = _prep_segments(rw, w2.shape[0], a2.shape[0], g2.shape[0])
    tile, before, whole, par, mu_spec = _prep_specs(tokens, rpad, rw, w2, a2, g2)

    def body(z_ref, zlast_ref, mu_ref, w0_ref, a0_ref, kk_ref, ka_ref, w2_ref, a2_ref, g2_ref,
             r_ref, lw_ref, kf_ref, v_ref, na_ref, b_ref, g_ref):
        f = _prep_forward_values(z_ref, zlast_ref, mu_ref, w0_ref, a0_ref, kk_ref, ka_ref, w2_ref, a2_ref, g2_ref,
                                 segs, pl.program_id(0) == 0)
        r_ref[...] = f["r"]
        v_ref[...] = f["v"]
        lw_ref[...] = f["lw"]
        kf_ref[...] = f["k"] * (1.0 + (f["a_sig"] - 1.0) * ka_ref[...])
        na_ref[...] = -f["kk"]
        b_ref[...] = f["kk"] * f["a_sig"]
        g_ref[...] = _mm(f["sg"], g2_ref[...])

    shape = jax.ShapeDtypeStruct((tokens, rw), F32)
    return pl.pallas_call(
        body, name="rwkv_prep_fwd", grid=(tokens // PREP_ROWS,),
        in_specs=[tile(rpad), before, mu_spec, par, par, par, par, whole(w2), whole(a2), whole(g2)],
        out_specs=[tile(rw)] * 7, out_shape=[shape] * 7,
        compiler_params=pltpu.CompilerParams(dimension_semantics=("parallel",), vmem_limit_bytes=VMEM_LIMIT_CAP),
    )(zr, zr, mu, w0, a0, k_k, k_a, w2, a2, g2)


def _prep_bwd_call(zr, mu, w0, a0, k_k, k_a, w2, a2, g2, cts):
    tokens, rpad = zr.shape
    rw = w0.shape[1]
    segs, _ = _prep_segments(rw, w2.shape[0], a2.shape[0], g2.shape[0])
    tile, before, whole, par, mu_spec = _prep_specs(tokens, rpad, rw, w2, a2, g2)
    nt = tokens // PREP_ROWS
    rev = lambda spec: pl.BlockSpec(spec.block_shape, lambda i, f=spec.index_map: f(nt - 1 - i))

    def body(z_ref, zlast_ref, mu_ref, w0_ref, a0_ref, kk_ref, ka_ref, w2_ref, a2_ref, g2_ref,
             dr_ref, dlw_ref, dkf_ref, dv_ref, dna_ref, db_ref, dg_ref,
             dz_ref, dmu_ref, dw0_ref, da0_ref, dkk_ref, dka_ref, dw2_ref, da2_ref, dg2_ref, carry):
        step = pl.program_id(0)

        @pl.when(step == 0)
        def _():
            for ref in (dmu_ref, dw0_ref, da0_ref, dkk_ref, dka_ref, dw2_ref, da2_ref, dg2_ref, carry):
                ref[...] = jnp.zeros_like(ref)

        f = _prep_forward_values(z_ref, zlast_ref, mu_ref, w0_ref, a0_ref, kk_ref, ka_ref, w2_ref, a2_ref, g2_ref,
                                 segs, step == nt - 1)
        k, kk, a_sig, sg, twd = f["k"], f["kk"], f["a_sig"], f["sg"], f["twd"]
        colsum = lambda t: jnp.sum(t, axis=0, keepdims=True)
        dkf, db, dg = dkf_ref[...], db_ref[...], dg_ref[...]
        ka = ka_ref[...]
        dgd = _mm(dg, g2_ref[...], tb=True) * sg * (1.0 - sg)
        dg2_ref[...] += _mm(sg, dg, ta=True)
        dkk = db * a_sig - dna_ref[...]
        da_sig = db * kk + dkf * k * ka
        dk = dkf * (1.0 + (a_sig - 1.0) * ka)
        dka_ref[...] += colsum(dkf * k * (a_sig - 1.0))
        along = jnp.where(f["live"], _head_sums(dkk * kk), 0.0)
        dkx = (dkk - kk * along) * f["inv"]
        dk = dk + dkx * kk_ref[...]
        dkk_ref[...] += colsum(dkx * k)
        dpa = da_sig * a_sig * (1.0 - a_sig)
        da0_ref[...] += colsum(dpa)
        dad = _mm(dpa, a2_ref[...], tb=True)
        da2_ref[...] += _mm(f["ad"], dpa, ta=True)
        dpw = dlw_ref[...] * f["lw"] / (1.0 + jnp.exp(f["pw"]))
        dw0_ref[...] += colsum(dpw)
        dwd = _mm(dpw, w2_ref[...], tb=True) * (1.0 - twd * twd)
        dw2_ref[...] += _mm(twd, dpw, ta=True)
        rows = PREP_ROWS
        last = lax.broadcasted_iota(jnp.int32, (rows, 1), 0) == rows - 1
        for name, dz in (("r", dr_ref[...]), ("k", dk), ("v", dv_ref[...]), ("wd", dwd), ("ad", dad), ("gd", dgd)):
            lo, hi = segs[name]
            mu_s = mu_ref[:, lo:hi]
            dmu_ref[:, lo:hi] += colsum(dz * f["z"][name][1])
            later = dz * mu_s
            dz_ref[:, lo:hi] = dz * (1.0 - mu_s) + jnp.where(last, carry[:, lo:hi], pltpu.roll(later, rows - 1, axis=0))
            carry[:, lo:hi] = later[0:1, :]

    tok = jax.ShapeDtypeStruct((tokens, rw), F32)
    acc = lambda a: jax.ShapeDtypeStruct(a.shape, F32)
    return pl.pallas_call(
        body, name="rwkv_prep_bwd", grid=(nt,),
        in_specs=[rev(tile(rpad)), rev(before), mu_spec, par, par, par, par, whole(w2), whole(a2), whole(g2)]
                 + [rev(tile(rw))] * 7,
        out_specs=[rev(tile(rpad)), mu_spec, par, par, par, par, whole(w2), whole(a2), whole(g2)],
        out_shape=[jax.ShapeDtypeStruct((tokens, rpad), F32), acc(mu), acc(w0), acc(a0), acc(k_k), acc(k_a), acc(w2), acc(a2), acc(g2)],
        scratch_shapes=[pltpu.VMEM((1, rpad), F32)],
        compiler_params=pltpu.CompilerParams(dimension_semantics=("arbitrary",), vmem_limit_bytes=VMEM_LIMIT_CAP),
    )(zr, zr, mu, w0, a0, k_k, k_a, w2, a2, g2, *cts)


@jax.custom_vjp
def rwkv_prep(zr, mu, w0, a0, k_k, k_a, w2, a2, g2):
    return tuple(_prep_fwd_call(zr, mu, w0, a0, k_k, k_a, w2, a2, g2))


def _rwkv_prep_bwd(res, cts):
    zr, mu, w0, a0, k_k, k_a, w2, a2, g2 = res
    dz, dmu, dw0, da0, dkk, dka, dw2, da2, dg2 = _prep_bwd_call(*res, cts)
    return dz, dmu, dw0, da0, dkk, dka, dw2.astype(w2.dtype), da2.astype(a2.dtype), dg2.astype(g2.dtype)


rwkv_prep.defvjp(lambda *a: (tuple(_prep_fwd_call(*a)), a), _rwkv_prep_bwd)


def _pair_masks(rows):
    lane = lax.broadcasted_iota(jnp.int32, (rows, PAIR), 1)
    return lane < HEAD_DIM, lane >= HEAD_DIM


def _bd(x):
    m0, m1 = _pair_masks(x.shape[0])
    return jnp.concatenate([jnp.where(m0, x, 0.0), jnp.where(m1, x, 0.0)], axis=0)


def _unbd(m, c):
    return jnp.where(_pair_masks(c)[0], m[:c], m[c:])


def _pair_a(l2, r2):
    return _mm(l2, _bd(r2), tb=True)


def _pair_mul(p2, x2):
    return _mm(p2, _bd(x2))


def _pair_mul_t(p2, x2):
    return _unbd(_mm(p2, x2, ta=True), p2.shape[0])


def _block_diag_mask():
    row = lax.broadcasted_iota(jnp.int32, (PAIR, PAIR), 0)
    lane = lax.broadcasted_iota(jnp.int32, (PAIR, PAIR), 1)
    return (row < HEAD_DIM) == (lane < HEAD_DIM), row == lane


def _wkv_pair_common(r, lw, k, a, b):
    c = r[0].shape[0]
    pairs = range(len(r))
    i = lax.broadcasted_iota(jnp.int32, (c, PAIR), 0)
    j = lax.broadcasted_iota(jnp.int32, (c, PAIR), 1) % c
    strict, incl = i > j, i >= j
    ti = lax.broadcasted_iota(jnp.int32, (c, c), 0)
    tj = lax.broadcasted_iota(jnp.int32, (c, c), 1)
    tri = jnp.where(ti >= tj, 1.0, 0.0).astype(BF16)
    lc = [sum(_dg(tri, part, False, False) for part in _split(lw[p], 3)) for p in pairs]
    lend = [lc[p][c - 1:c, :] for p in pairs]
    rt = [r[p] * jnp.exp(lc[p]) for p in pairs]
    at = [a[p] * jnp.exp(lc[p] - lw[p]) for p in pairs]
    pinv = [jnp.exp(-lc[p]) for p in pairs]
    kt = [k[p] * pinv[p] for p in pairs]
    bt = [b[p] * pinv[p] for p in pairs]
    e = [jnp.exp(lend[p] - lc[p]) for p in pairs]
    ktp = [k[p] * e[p] for p in pairs]
    btp = [b[p] * e[p] for p in pairs]
    a_ab = [jnp.where(strict, _pair_a(at[p], bt[p]), 0.0) for p in pairs]
    a_ak = [jnp.where(strict, _pair_a(at[p], kt[p]), 0.0) for p in pairs]
    a_rb = [jnp.where(incl, _pair_a(rt[p], bt[p]), 0.0) for p in pairs]
    a_rk = [jnp.where(incl, _pair_a(rt[p], kt[p]), 0.0) for p in pairs]
    t = [jnp.where(i == j, 1.0, 0.0) + a_ab[p] for p in pairs]
    xp = a_ab
    n = 2
    while n < c:
        xp = [_pair_mul(xp[p], xp[p]) for p in pairs]
        t = [t[p] + _pair_mul(t[p], xp[p]) for p in pairs]
        n *= 2
    bdm, eye = _block_diag_mask()
    pend_col = [jnp.sum(jnp.where(eye, jnp.exp(lend[p]), 0.0), axis=1, keepdims=True) for p in pairs]
    return dict(rt=rt, at=at, kt=kt, bt=bt, ktp=ktp, btp=btp, a_ak=a_ak, a_rb=a_rb, a_rk=a_rk, t=t,
                pend_col=pend_col, lend=lend, lc=lc, strict=strict, incl=incl, tri=tri, bdm=bdm)


def _wkv_group(width):
    npair = width // PAIR
    g = min(WKV_PAIRS_PER_STEP, npair)
    assert npair % g == 0
    return npair, g


def _wkv_fwd_call(r, lw, k, v, a, b):
    tokens, width = r.shape
    c = WKV_CHUNK
    nc = tokens // c
    npair, g = _wkv_group(width)

    def body(r_ref, lw_ref, k_ref, v_ref, a_ref, b_ref, y_ref, s_ref, st):
        @pl.when(pl.program_id(1) == 0)
        def _():
            st[...] = jnp.zeros_like(st)

        pairs = range(g)
        rv, lwv, kv, vv, av, bv = ([ref[:, p * PAIR:(p + 1) * PAIR] for p in pairs]
                                   for ref in (r_ref, lw_ref, k_ref, v_ref, a_ref, b_ref))
        s0 = [st[p] for p in pairs]
        q = _wkv_pair_common(rv, lwv, kv, av, bv)
        w1 = [_mm(q["at"][p], s0[p]) + _pair_mul(q["a_ak"][p], vv[p]) for p in pairs]
        u = [_pair_mul(q["t"][p], w1[p]) for p in pairs]
        y = [_mm(q["rt"][p], s0[p]) + _pair_mul(q["a_rb"][p], u[p]) + _pair_mul(q["a_rk"][p], vv[p]) for p in pairs]
        grow = [_mm(jnp.concatenate([q["btp"][p], q["ktp"][p]], axis=0), jnp.concatenate([u[p], vv[p]], axis=0), ta=True)
                for p in pairs]
        for p in pairs:
            y_ref[:, p * PAIR:(p + 1) * PAIR] = y[p]
            s_ref[0, p] = s0[p]
            st[p] = q["pend_col"][p] * s0[p] + jnp.where(q["bdm"], grow[p], 0.0)

    tok = pl.BlockSpec((c, g * PAIR), lambda gi, ci: (ci, gi))
    return pl.pallas_call(
        body, name="wkv_fwd", grid=(npair // g, nc),
        in_specs=[tok] * 6,
        out_specs=[tok, pl.BlockSpec((1, g, PAIR, PAIR), lambda gi, ci: (ci, gi, 0, 0))],
        out_shape=[jax.ShapeDtypeStruct((tokens, width), F32), jax.ShapeDtypeStruct((nc, npair, PAIR, PAIR), F32)],
        scratch_shapes=[pltpu.VMEM((g, PAIR, PAIR), F32)],
        compiler_params=pltpu.CompilerParams(dimension_semantics=("parallel", "arbitrary")),
    )(r, lw, k, v, a, b)


def _wkv_bwd_call(r, lw, k, v, a, b, s, dy):
    tokens, width = r.shape
    c = WKV_CHUNK
    nc = tokens // c
    npair, g = _wkv_group(width)

    def body(r_ref, lw_ref, k_ref, v_ref, a_ref, b_ref, s_ref, dy_ref,
             dr_ref, dlw_ref, dk_ref, dv_ref, da_ref, db_ref, dst):
        @pl.when(pl.program_id(1) == 0)
        def _():
            dst[...] = jnp.zeros_like(dst)

        pairs = range(g)
        rv, lwv, kv, vv, av, bv, dyv = ([ref[:, p * PAIR:(p + 1) * PAIR] for p in pairs]
                                        for ref in (r_ref, lw_ref, k_ref, v_ref, a_ref, b_ref, dy_ref))
        s0 = [s_ref[0, p] for p in pairs]
        dsc = [dst[p] for p in pairs]
        q = _wkv_pair_common(rv, lwv, kv, av, bv)
        rt, at, kt, bt, ktp, btp, t = (q[n] for n in ("rt", "at", "kt", "bt", "ktp", "btp", "t"))
        a_ak, a_rb, a_rk, strict, incl = (q[n] for n in ("a_ak", "a_rb", "a_rk", "strict", "incl"))
        w1 = [_mm(at[p], s0[p]) + _pair_mul(a_ak[p], vv[p]) for p in pairs]
        u = [_pair_mul(t[p], w1[p]) for p in pairs]
        du = [_pair_mul_t(a_rb[p], dyv[p]) + _mm(btp[p], dsc[p]) for p in pairs]
        dw1 = [_pair_mul_t(t[p], du[p]) for p in pairs]
        dv = [_pair_mul_t(a_rk[p], dyv[p]) + _mm(ktp[p], dsc[p]) + _pair_mul_t(a_ak[p], dw1[p]) for p in pairs]
        da_ab = [jnp.where(strict, _pair_a(dw1[p], u[p]), 0.0) for p in pairs]
        da_ak = [jnp.where(strict, _pair_a(dw1[p], vv[p]), 0.0) for p in pairs]
        da_rb = [jnp.where(incl, _pair_a(dyv[p], u[p]), 0.0) for p in pairs]
        da_rk = [jnp.where(incl, _pair_a(dyv[p], vv[p]), 0.0) for p in pairs]
        d_rt = [_mm(dyv[p], s0[p], tb=True) + _pair_mul(da_rb[p], bt[p]) + _pair_mul(da_rk[p], kt[p]) for p in pairs]
        d_at = [_mm(dw1[p], s0[p], tb=True) + _pair_mul(da_ab[p], bt[p]) + _pair_mul(da_ak[p], kt[p]) for p in pairs]
        d_bt = [_pair_mul_t(da_ab[p], at[p]) + _pair_mul_t(da_rb[p], rt[p]) for p in pairs]
        d_kt = [_pair_mul_t(da_ak[p], at[p]) + _pair_mul_t(da_rk[p], rt[p]) for p in pairs]
        d_btp = [_mm(u[p], dsc[p], tb=True) for p in pairs]
        d_ktp = [_mm(vv[p], dsc[p], tb=True) for p in pairs]
        ones = jnp.ones((8, PAIR), BF16)
        dpend = [sum(_dg(ones, part, False, True) for part in _split(dsc[p] * s0[p], 3))[0:1, :] * jnp.exp(q["lend"][p])
                 for p in pairs]
        grow = [_mm(jnp.concatenate([rt[p], at[p]], axis=0), jnp.concatenate([dyv[p], dw1[p]], axis=0), ta=True)
                for p in pairs]
        last = lax.broadcasted_iota(jnp.int32, (c, PAIR), 0) == c - 1
        for p in pairs:
            sl = slice(p * PAIR, (p + 1) * PAIR)
            dst[p] = q["pend_col"][p] * dsc[p] + jnp.where(q["bdm"], grow[p], 0.0)
            lc_e = d_ktp[p] * ktp[p] + d_btp[p] * btp[p]
            dlend = jnp.sum(lc_e, axis=0, keepdims=True) + dpend[p]
            dlc = d_rt[p] * rt[p] - d_kt[p] * kt[p] - d_bt[p] * bt[p] - lc_e + jnp.where(last, dlend, 0.0)
            dlp = d_at[p] * at[p]
            dlw_ref[:, sl] = sum(_dg(q["tri"], part, True, False) for part in _split(dlc + dlp, 3)) - dlp
            lc = q["lc"][p]
            pinv = jnp.exp(-lc)
            e = jnp.exp(q["lend"][p] - lc)
            dr_ref[:, sl] = d_rt[p] * jnp.exp(lc)
            da_ref[:, sl] = d_at[p] * jnp.exp(lc - lwv[p])
            dk_ref[:, sl] = d_kt[p] * pinv + d_ktp[p] * e
            db_ref[:, sl] = d_bt[p] * pinv + d_btp[p] * e
            dv_ref[:, sl] = dv[p]

    tok = pl.BlockSpec((c, g * PAIR), lambda gi, ci: (nc - 1 - ci, gi))
    tshape = jax.ShapeDtypeStruct((tokens, width), F32)
    return pl.pallas_call(
        body, name="wkv_bwd", grid=(npair // g, nc),
        in_specs=[tok] * 6 + [pl.BlockSpec((1, g, PAIR, PAIR), lambda gi, ci: (nc - 1 - ci, gi, 0, 0)), tok],
        out_specs=[tok] * 6, out_shape=[tshape] * 6,
        scratch_shapes=[pltpu.VMEM((g, PAIR, PAIR), F32)],
        compiler_params=pltpu.CompilerParams(dimension_semantics=("parallel", "arbitrary")),
    )(r, lw, k, v, a, b, s, dy)


@jax.custom_vjp
def wkv7(r, lw, k, v, a, b):
    return _wkv_fwd_call(r, lw, k, v, a, b)[0]


def _wkv7_fwd(r, lw, k, v, a, b):
    y, s = _wkv_fwd_call(r, lw, k, v, a, b)
    return y, (r, lw, k, v, a, b, s)


wkv7.defvjp(_wkv7_fwd, lambda res, dy: tuple(_wkv_bwd_call(*res, dy)))


def _attn_block(tokens):
    return ATTN_BLOCK_BIG if tokens % ATTN_BLOCK_BIG == 0 else ATTN_BLOCK


def _fox_layouts(cum):
    tokens, heads = cum.shape
    t = _attn_block(tokens)
    cq = cum.reshape(tokens, heads // 2, 2).transpose(1, 0, 2)
    ck = cum.T.reshape(heads // 2, 2, tokens // t, t).transpose(0, 2, 1, 3)
    return cq, ck


def _head_lane_masks(rows):
    lane = lax.broadcasted_iota(jnp.int32, (rows, 2 * HEAD_DIM), 1)
    return [lane < HEAD_DIM, lane >= HEAD_DIM]


def _fox_fwd_call(q, k, v, cq, ck):
    tokens, width = q.shape
    t = _attn_block(tokens)
    nb = tokens // t
    hd = HEAD_DIM
    npair = width // (2 * hd)

    def body(q_ref, k_ref, v_ref, cq_ref, ck_ref, o_ref, lse_ref):
        i = pl.program_id(1)
        masks = _head_lane_masks(t)
        q2 = q_ref[...]
        qs = [jnp.where(mk, q2, 0.0).astype(BF16) for mk in masks]
        cqs = [cq_ref[0, :, hh:hh + 1] for hh in range(2)]

        def block(j, carry, diagonal):
            off = pl.multiple_of(j * t, t)
            ckj = ck_ref[0, j]
            k2 = k_ref[pl.ds(off, t), :].astype(BF16)
            v2 = v_ref[pl.ds(off, t), :].astype(BF16)
            out = []
            for hh in range(2):
                m, l, acc = carry[hh]
                s = _dg(qs[hh], k2, False, True) + (cqs[hh] - ckj[hh:hh + 1, :])
                if diagonal:
                    keep = lax.broadcasted_iota(jnp.int32, (t, t), 0) >= lax.broadcasted_iota(jnp.int32, (t, t), 1)
                    s = jnp.where(keep, s, NEG_BIG)
                m_new = jnp.maximum(m, jnp.max(s, axis=1, keepdims=True))
                alpha = jnp.exp(m - m_new)
                p = jnp.exp(s - m_new)
                l = alpha * l + jnp.sum(p, axis=1, keepdims=True)
                acc = alpha * acc + _dg(p.astype(BF16), v2, False, False)
                out.append((m_new, l, acc))
            return tuple(out)

        init = tuple((jnp.full((t, 1), NEG_BIG, F32), jnp.zeros((t, 1), F32), jnp.zeros((t, 2 * hd), F32)) for _ in range(2))
        res = lax.fori_loop(0, i, lambda j, c: block(j, c, False), init)
        res = block(i, res, True)
        o_ref[...] = jnp.where(masks[0], res[0][2] / res[0][1], res[1][2] / res[1][1])
        for hh in range(2):
            lse_ref[0, :, hh:hh + 1] = res[hh][0] + jnp.log(res[hh][1])

    blk = pl.BlockSpec((t, 2 * hd), lambda hp, i: (i, hp))
    full = pl.BlockSpec((tokens, 2 * hd), lambda hp, i: (0, hp))
    cq_spec = pl.BlockSpec((1, t, 2), lambda hp, i: (hp, i, 0))
    ck_spec = pl.BlockSpec((1, nb, 2, t), lambda hp, i: (hp, 0, 0, 0))
    return pl.pallas_call(
        body, name="fox_fwd", grid=(npair, nb),
        in_specs=[blk, full, full, cq_spec, ck_spec],
        out_specs=[blk, cq_spec],
        out_shape=[jax.ShapeDtypeStruct((tokens, width), F32), jax.ShapeDtypeStruct((npair, tokens, 2), F32)],
        compiler_params=pltpu.CompilerParams(dimension_semantics=("parallel", "arbitrary")),
    )(q, k, v, cq, ck)


def _fox_bwd_call(q, k, v, cq, ck, o, lse, do):
    tokens, width = q.shape
    t = _attn_block(tokens)
    nb = tokens // t
    hd = HEAD_DIM
    npair = width // (2 * hd)

    def body(q_ref, k_ref, v_ref, cq_ref, ck_ref, o_ref, lse_ref, do_ref, dq_ref, dk_ref, dv_ref, dck_ref, dcq_ref):
        i = pl.program_id(1)

        @pl.when(i == 0)
        def _():
            dk_ref[...] = jnp.zeros_like(dk_ref)
            dv_ref[...] = jnp.zeros_like(dv_ref)
            dck_ref[...] = jnp.zeros_like(dck_ref)

        masks = _head_lane_masks(t)
        q2, do2, o2 = q_ref[...], do_ref[...], o_ref[...]
        qs = [jnp.where(mk, q2, 0.0).astype(BF16) for mk in masks]
        dos = [jnp.where(mk, do2, 0.0).astype(BF16) for mk in masks]
        deltas = [jnp.sum(dos[hh].astype(F32) * o2, axis=1, keepdims=True) for hh in range(2)]
        bias = [cq_ref[0, :, hh:hh + 1] - lse_ref[0, :, hh:hh + 1] for hh in range(2)]

        def block(j, carry, diagonal):
            off = pl.multiple_of(j * t, t)
            ckj = ck_ref[0, j]
            k2 = k_ref[pl.ds(off, t), :].astype(BF16)
            v2 = v_ref[pl.ds(off, t), :].astype(BF16)
            out = []
            dk2 = jnp.zeros((t, 2 * hd), F32)
            dv2 = jnp.zeros((t, 2 * hd), F32)
            for hh in range(2):
                s = _dg(qs[hh], k2, False, True) + (bias[hh] - ckj[hh:hh + 1, :])
                if diagonal:
                    keep = lax.broadcasted_iota(jnp.int32, (t, t), 0) >= lax.broadcasted_iota(jnp.int32, (t, t), 1)
                    s = jnp.where(keep, s, NEG_BIG)
                p = jnp.exp(s)
                dp = _dg(dos[hh], v2, False, True)
                ds = p * (dp - deltas[hh])
                dsb = ds.astype(BF16)
                dq, rowsum = carry[hh]
                out.append((dq + _dg(dsb, k2, False, False), rowsum + jnp.sum(ds, axis=1, keepdims=True)))
                dk2 = dk2 + _dg(dsb, qs[hh], True, False)
                dv2 = dv2 + _dg(p.astype(BF16), dos[hh], True, False)
                dck_ref[0, j, hh:hh + 1, :] -= jnp.sum(ds, axis=0, keepdims=True)
            dk_ref[pl.ds(off, t), :] += dk2
            dv_ref[pl.ds(off, t), :] += dv2
            return tuple(out)

        init = tuple((jnp.zeros((t, 2 * hd), F32), jnp.zeros((t, 1), F32)) for _ in range(2))
        res = lax.fori_loop(0, i, lambda j, c: block(j, c, False), init)
        res = block(i, res, True)
        dq_ref[...] = jnp.where(masks[0], res[0][0], res[1][0])
        for hh in range(2):
            dcq_ref[0, :, hh:hh + 1] = res[hh][1]

    blk = pl.BlockSpec((t, 2 * hd), lambda hp, i: (i, hp))
    full = pl.BlockSpec((tokens, 2 * hd), lambda hp, i: (0, hp))
    cq_spec = pl.BlockSpec((1, t, 2), lambda hp, i: (hp, i, 0))
    ck_spec = pl.BlockSpec((1, nb, 2, t), lambda hp, i: (hp, 0, 0, 0))
    tshape = jax.ShapeDtypeStruct((tokens, width), F32)
    return pl.pallas_call(
        body, name="fox_bwd", grid=(npair, nb),
        in_specs=[blk, full, full, cq_spec, ck_spec, blk, cq_spec, blk],
        out_specs=[blk, full, full, ck_spec, cq_spec],
        out_shape=[tshape, tshape, tshape, jax.ShapeDtypeStruct((npair, nb, 2, t), F32),
                   jax.ShapeDtypeStruct((npair, tokens, 2), F32)],
        compiler_params=pltpu.CompilerParams(dimension_semantics=("parallel", "arbitrary")),
    )(q, k, v, cq, ck, o, lse, do)


@jax.custom_vjp
def fox_attention(q, k, v, cum):
    return _fox_fwd_call(q, k, v, *_fox_layouts(cum))[0]


def _fox_fwd(q, k, v, cum):
    cq, ck = _fox_layouts(cum)
    o, lse = _fox_fwd_call(q, k, v, cq, ck)
    return o, (q, k, v, cq, ck, o, lse)


def _fox_bwd(res, do):
    q, k, v, cq, ck, o, lse = res
    dq, dk, dv, dck, dcq = _fox_bwd_call(q, k, v, cq, ck, o, lse, do)
    npair, nb, _, t = dck.shape
    dcum = dck.transpose(0, 2, 1, 3).reshape(2 * npair, nb * t).T + dcq.transpose(1, 0, 2).reshape(nb * t, 2 * npair)
    return dq, dk, dv, dcum


fox_attention.defvjp(_fox_fwd, _fox_bwd)


def _loss_call(y, target):
    rows, d = y.shape
    tr = _row_tile(rows, d)

    def body(y_ref, t_ref, loss_ref, dy_ref):
        @pl.when(pl.program_id(0) == 0)
        def _():
            loss_ref[...] = jnp.zeros_like(loss_ref)

        diff = y_ref[...] - t_ref[...]
        dy_ref[...] = diff * (1.0 / d)
        loss_ref[...] += (0.5 / d) * jnp.sum(jnp.sum(diff * diff, axis=1, keepdims=True), axis=0, keepdims=True)

    return pl.pallas_call(
        body, name="loss", grid=(rows // tr,),
        in_specs=[pl.BlockSpec((tr, d), lambda i: (i, 0))] * 2,
        out_specs=[pl.BlockSpec((1, 1), lambda i: (0, 0)), pl.BlockSpec((tr, d), lambda i: (i, 0))],
        out_shape=[jax.ShapeDtypeStruct((1, 1), F32), jax.ShapeDtypeStruct((rows, d), F32)],
        compiler_params=pltpu.CompilerParams(dimension_semantics=("arbitrary",)),
    )(y, target)


def _adamw_call(w, g, m, v):
    rows, cols = w.shape
    tr = _row_tile_ragged(rows, cols, budget=1024 * 1024)
    c1 = 1.0 / (1.0 - ADAM_B1 ** ADAM_STEP)
    c2 = 1.0 / (1.0 - ADAM_B2 ** ADAM_STEP)

    def body(w_ref, g_ref, m_ref, v_ref, d_ref, nm_ref, nv_ref):
        gv = g_ref[...]
        nm = ADAM_B1 * m_ref[...] + (1.0 - ADAM_B1) * gv
        nv = ADAM_B2 * v_ref[...] + (1.0 - ADAM_B2) * (gv * gv)
        nm_ref[...] = nm
        nv_ref[...] = nv
        d_ref[...] = -ADAM_LR * ((nm * c1) / (jnp.sqrt(nv * c2) + ADAM_EPS) + ADAM_WD * w_ref[...])

    spec = pl.BlockSpec((tr, cols), lambda i: (i, 0))
    shape = jax.ShapeDtypeStruct((rows, cols), F32)
    return pl.pallas_call(
        body, name="adamw", grid=(pl.cdiv(rows, tr),),
        in_specs=[spec] * 4, out_specs=[spec] * 3, out_shape=[shape] * 3,
        compiler_params=pltpu.CompilerParams(dimension_semantics=("parallel",)),
    )(w, g, m, v)


def _my_place():
    return lax.axis_index("x"), lax.axis_index("y"), lax.axis_index("c")


def _place_index(px, py, pc):
    return 4 * px + 2 * py + pc


HBM_SPEC = pl.BlockSpec(memory_space=pltpu.HBM)


def _all_gather_call(block):
    def body(x_ref, out_ref, send_sems, recv_sems, local_sem):
        x, y, c = _my_place()
        me, sibling = (x, y, c), (x, y, 1 - c)
        chips = [(1 - x, y), (x, 1 - y), (1 - x, 1 - y)]

        def slot(px, py, pc):
            return out_ref.at[_place_index(px, py, pc)]

        def copy(k, blk, to, src=None):
            return pltpu.make_async_remote_copy(
                src_ref=slot(*blk) if src is None else src, dst_ref=slot(*blk),
                send_sem=send_sems.at[k], recv_sem=recv_sems.at[k],
                device_id=to, device_id_type=pl.DeviceIdType.MESH)

        mine = pltpu.make_async_copy(x_ref, slot(*me), local_sem)
        mine.start()
        first = [copy(0, me, sibling, src=x_ref)]
        first += [copy(1 + j, me, (*chip, c), src=x_ref) for j, chip in enumerate(chips)]
        for cp in first:
            cp.start()
        passed = [copy(4 + j, (*chip, c), sibling) for j, chip in enumerate(chips)]
        for j, chip in enumerate(chips):
            copy(1 + j, (*chip, c), me).wait_recv()
            passed[j].start()
        copy(0, sibling, me).wait_recv()
        for j, chip in enumerate(chips):
            copy(4 + j, (*chip, 1 - c), me).wait_recv()
        for cp in first + passed:
            cp.wait_send()
        mine.wait()

    return pl.pallas_call(
        body, name="all_gather",
        out_shape=jax.ShapeDtypeStruct((N_DEV,) + block.shape, block.dtype),
        in_specs=[HBM_SPEC], out_specs=HBM_SPEC,
        scratch_shapes=[pltpu.SemaphoreType.DMA((7,)), pltpu.SemaphoreType.DMA((7,)), pltpu.SemaphoreType.DMA],
    )(block)


SEM_SPEC = pl.BlockSpec(memory_space=pltpu.SEMAPHORE)
SIDE_EFFECT = pltpu.SideEffectType.DATAFLOW_SIDE_EFFECTING


def _peers():
    x, y, c = _my_place()
    out = []
    for k in range(1, N_DEV):
        peer = (x ^ (k >> 2), y ^ ((k >> 1) & 1), c ^ (k & 1))
        out.append((k - 1, peer, _place_index(*peer)))
    return _place_index(x, y, c), out


def _spread_start(src, per_peer, name, after=None):
    slot = src.shape[1:] if per_peer else src.shape
    order = () if after is None else (after,)

    def body(src_ref, land_ref, *rest):
        send_sems, recv_sems, src_thru, land_thru, token = rest[len(order):]
        mine, peers = _peers()
        for k, peer, peer_idx in peers:
            pltpu.make_async_remote_copy(
                src_ref=src_ref.at[peer_idx] if per_peer else src_ref, dst_ref=land_ref.at[mine],
                send_sem=send_sems.at[k], recv_sem=recv_sems.at[k],
                device_id=peer, device_id_type=pl.DeviceIdType.MESH).start()
        token[...] = jnp.zeros_like(token)

    return pl.pallas_call(
        body, name=name,
        out_shape=(pltpu.SemaphoreType.DMA((N_DEV - 1,)), pltpu.SemaphoreType.DMA((N_DEV - 1,)),
                   pltpu.HBM(src.shape, src.dtype), pltpu.HBM((N_DEV,) + slot, src.dtype),
                   jax.ShapeDtypeStruct((8, 128), F32)),
        in_specs=(HBM_SPEC, HBM_SPEC) + (pl.BlockSpec(memory_space=pl.ANY),) * len(order),
        out_specs=(SEM_SPEC, SEM_SPEC, HBM_SPEC, HBM_SPEC, pl.BlockSpec(memory_space=pltpu.VMEM)),
        input_output_aliases={0: 2, 1: 3},
        compiler_params=pltpu.CompilerParams(has_side_effects=SIDE_EFFECT),
    )(pltpu.with_memory_space_constraint(src, pltpu.HBM),
      pltpu.with_memory_space_constraint(lax.empty((N_DEV,) + slot, src.dtype), pltpu.HBM), *order)


def _spread_wait(handles, after, per_peer, name):
    send_sems, recv_sems, src_thru, land_thru = handles

    def body(src_ref, land_ref, send_sems, recv_sems, after_ref, src_dead, got_ref):
        _, peers = _peers()
        for k, peer, peer_idx in peers:
            copy = pltpu.make_async_remote_copy(
                src_ref=src_ref.at[peer_idx] if per_peer else src_ref, dst_ref=land_ref.at[peer_idx],
                send_sem=send_sems.at[k], recv_sem=recv_sems.at[k],
                device_id=peer, device_id_type=pl.DeviceIdType.MESH)
            copy.wait_send()
            copy.wait_recv()

    return pl.pallas_call(
        body, name=name,
        out_shape=(pltpu.HBM(src_thru.shape, src_thru.dtype), pltpu.HBM(land_thru.shape, land_thru.dtype)),
        in_specs=(HBM_SPEC, HBM_SPEC, SEM_SPEC, SEM_SPEC, pl.BlockSpec(memory_space=pl.ANY)),
        out_specs=(HBM_SPEC, HBM_SPEC), input_output_aliases={0: 0, 1: 1},
        compiler_params=pltpu.CompilerParams(has_side_effects=SIDE_EFFECT),
    )(src_thru, land_thru, send_sems, recv_sems, after)


def _sum_slots_call(slots):
    _, rows, cols = slots.shape
    tr = _row_tile_ragged(rows, cols, budget=512 * 1024)

    def body(s_ref, o_ref):
        acc = s_ref[0].astype(F32)
        for j in range(1, N_DEV):
            acc = acc + s_ref[j].astype(F32)
        o_ref[...] = acc

    return pl.pallas_call(
        body, name="sum_slots", grid=(pl.cdiv(rows, tr),),
        in_specs=[pl.BlockSpec((N_DEV, tr, cols), lambda i: (0, i, 0))],
        out_specs=pl.BlockSpec((tr, cols), lambda i: (i, 0)),
        out_shape=jax.ShapeDtypeStruct((rows, cols), F32),
        compiler_params=pltpu.CompilerParams(dimension_semantics=("parallel",)),
    )(slots)


def _sum_adamw_call(got, own, w, m, v):
    rows, cols = w.shape
    tr = _row_tile_ragged(rows, cols, budget=512 * 1024)
    c1 = 1.0 / (1.0 - ADAM_B1 ** ADAM_STEP)
    c2 = 1.0 / (1.0 - ADAM_B2 ** ADAM_STEP)

    def body(got_ref, own_ref, w_ref, m_ref, v_ref, g_ref, d_ref, nm_ref, nv_ref):
        mine = _place_index(*_my_place())
        gv = jnp.zeros(w_ref.shape, F32)
        for j in range(N_DEV):
            gv = gv + jnp.where(mine == j, own_ref[...], got_ref[j]).astype(F32)
        nm = ADAM_B1 * m_ref[...] + (1.0 - ADAM_B1) * gv
        nv = ADAM_B2 * v_ref[...] + (1.0 - ADAM_B2) * (gv * gv)
        g_ref[...] = gv
        nm_ref[...] = nm
        nv_ref[...] = nv
        d_ref[...] = -ADAM_LR * ((nm * c1) / (jnp.sqrt(nv * c2) + ADAM_EPS) + ADAM_WD * w_ref[...])

    spec = pl.BlockSpec((tr, cols), lambda i: (i, 0))
    shape = jax.ShapeDtypeStruct((rows, cols), F32)
    return pl.pallas_call(
        body, name="sum_adamw", grid=(pl.cdiv(rows, tr),),
        in_specs=[pl.BlockSpec((N_DEV, tr, cols), lambda i: (0, i, 0))] + [spec] * 4,
        out_specs=[spec] * 4, out_shape=[shape] * 4,
        compiler_params=pltpu.CompilerParams(dimension_semantics=("parallel",)),
    )(got, own, w, m, v)


def _with_own_slot(got, own, mine):
    return lax.dynamic_update_index_in_dim(got, own, mine, 0)


def _pack(vectors, width):
    flat = jnp.concatenate([v.reshape(-1) for v in vectors])
    return jnp.pad(flat, (0, width - flat.shape[0])).reshape(width // 128, 128)


def _unpack(packed, like):
    flat = packed.reshape(-1)
    out, at = [], 0
    for v in like:
        out.append(flat[at:at + v.size].reshape(v.shape))
        at += v.size
    return tuple(out)


def _cols_from_slots(slots):
    n, rows, cols = slots.shape
    return slots.transpose(1, 0, 2).reshape(rows, n * cols)


def _rows_from_slots(slots):
    return slots.reshape(-1, slots.shape[2])


def _pad128(n):
    return -(-n // 128) * 128


def _pad_to_tiles(a, axis):
    n = a.shape[axis]
    pads = [(0, 0)] * a.ndim
    pads[axis] = (0, _pad128(n) - n)
    return jnp.pad(a, pads)


def _rwkv_group(take, zeros, rw, dl, al, gl):
    at = 3 * rw
    parts = take(0, at)
    for n in (dl, al, gl):
        parts += take(at, at + n)
        if _pad128(n) > n:
            parts.append(zeros(_pad128(n) - n))
        at += n
    return parts


def _in_proj_layout(slots, rw, fw, dl, al, gl, whole):
    n_slots, rows, d = slots.shape
    wt = slots.reshape(n_slots * rows, d)
    take = lambda lo, hi: [wt[lo:hi]]
    zeros = lambda n: jnp.zeros((n, d), wt.dtype)
    rcols = 3 * rw + dl + al + gl
    fcols = 3 * fw + fw // HEAD_DIM
    group_r = _rwkv_group(take, zeros, rw, dl, al, gl)
    group_f = take(rcols, rcols + fcols) + ([zeros(_pad128(fcols) - fcols)] if _pad128(fcols) > fcols else [])
    group_g = take(rcols + fcols, n_slots * rows)
    if whole:
        return jnp.concatenate(group_r + group_f + group_g, axis=0)
    return tuple(jnp.concatenate(g, axis=0) for g in (group_r, group_f, group_g))


def _low_rank_layout(slots):
    return _pad_to_tiles(_cols_from_slots(slots), 0)


def _stage_embed(meta, x, n1, lp):
    h0 = jnp.concatenate([meta, x, jnp.zeros((lp - meta.shape[0] - x.shape[0], x.shape[1]), F32)], axis=0)
    return h0, rmsnorm(h0, n1)


def _stage_mix(z_r, z_f, small, w2, a2, g2, dims):
    (mu, w0, a0, k_k, k_a, r_k, gn_w, gn_b, q_g, k_g, f_bias) = small
    rw, fw, dl, al, gl = dims
    fcols = 3 * fw + fw // HEAD_DIM

    mu_group = jnp.concatenate(_rwkv_group(lambda lo, hi: [mu[:, lo:hi]], lambda n: jnp.zeros((1, n), F32), rw, dl, al, gl), axis=1)
    r, lw, kf, v, na, b, g = rwkv_prep(z_r, mu_group, w0, a0, k_k, k_a, w2, a2, g2)
    y = wkv7(r, lw, kf, v, na, b)
    y_a = gn_bonus(y, r, kf, v, g, gn_w, gn_b, r_k.reshape(1, rw))

    fq, fk, fv, fl = z_f[:, :fw], z_f[:, fw:2 * fw], z_f[:, 2 * fw:3 * fw], z_f[:, 3 * fw:fcols]
    fq = head_rms(fq, jnp.tile(q_g, (1, fw // HEAD_DIM))) * (HEAD_DIM ** -0.5)
    fk = head_rms(fk, jnp.tile(k_g, (1, fw // HEAD_DIM)))
    cum = jnp.cumsum(jax.nn.log_sigmoid(badd(fl, f_bias)), axis=0)
    y_b = fox_attention(fq, fk, fv, cum)
    return y_a, y_b


def _stage_merge(h0, y_a, y_b, z_g, w_a, w_b, w_o):
    merged = gated_merge(z_g, dense_cols_bf16(y_a, w_a), dense_cols_bf16(y_b, w_b))
    return dense_add(merged, w_o, h0)


def _stage_ffn(h1, n2, w_gu, w_dn):
    return dense_add(swiglu(dense_cols_bf16(rmsnorm(h1, n2), w_gu)), w_dn, h1)


SHARDED = ("meta_tokens", "w_in", "rwkv_w2", "rwkv_a2", "rwkv_g2", "w_branch_a", "w_branch_b", "w_o", "w_gate_up", "w_down")
SMALL = ("norm1_g", "rwkv_mu", "rwkv_w0", "rwkv_a0", "rwkv_k_k", "rwkv_k_a", "rwkv_r_k", "rwkv_gn_w", "rwkv_gn_b",
         "fox_q_norm_g", "fox_k_norm_g", "fox_f_bias", "norm2_g")
WEIGHTS = ("meta_tokens", "norm1_g", "w_in", "rwkv_mu", "rwkv_w0", "rwkv_w2", "rwkv_a0", "rwkv_a2", "rwkv_g2", "rwkv_k_k",
           "rwkv_k_a", "rwkv_r_k", "rwkv_gn_w", "rwkv_gn_b", "fox_q_norm_g", "fox_k_norm_g", "fox_f_bias", "w_branch_a",
           "w_branch_b", "w_o", "norm2_g", "w_gate_up", "w_down")


def _as2d(a):
    return a.reshape(-1, a.shape[-1])


def kernel(x, meta_tokens, norm1_g, w_in, rwkv_mu, rwkv_w0, rwkv_w2, rwkv_a0, rwkv_a2, rwkv_g2, rwkv_k_k, rwkv_k_a, rwkv_r_k, rwkv_gn_w, rwkv_gn_b, fox_q_norm_g, fox_k_norm_g, fox_f_bias, w_branch_a, w_branch_b, w_o, norm2_g, w_gate_up, w_down, loss_target, m_meta_tokens, m_norm1_g, m_w_in, m_rwkv_mu, m_rwkv_w0, m_rwkv_w2, m_rwkv_a0, m_rwkv_a2, m_rwkv_g2, m_rwkv_k_k, m_rwkv_k_a, m_rwkv_r_k, m_rwkv_gn_w, m_rwkv_gn_b, m_fox_q_norm_g, m_fox_k_norm_g, m_fox_f_bias, m_w_branch_a, m_w_branch_b, m_w_o, m_norm2_g, m_w_gate_up, m_w_down, v_meta_tokens, v_norm1_g, v_w_in, v_rwkv_mu, v_rwkv_w0, v_rwkv_w2, v_rwkv_a0, v_rwkv_a2, v_rwkv_g2, v_rwkv_k_k, v_rwkv_k_a, v_rwkv_r_k, v_rwkv_gn_w, v_rwkv_gn_b, v_fox_q_norm_g, v_fox_k_norm_g, v_fox_f_bias, v_w_branch_a, v_w_branch_b, v_w_o, v_norm2_g, v_w_gate_up, v_w_down):
    given = dict(locals())
    w = {n: given[n] for n in WEIGHTS}
    assert rwkv_r_k.shape[-1] == HEAD_DIM
    n_meta, seq = meta_tokens.shape[0], x.shape[1]
    tokens = n_meta + seq
    lp = -(-tokens // TOKEN_TILE) * TOKEN_TILE
    mine = _place_index(*(lax.axis_index(a) for a in MESH_AXES))
    x2 = x[0]

    local = {n: _as2d(given[n]) for n in given if n != "x" and n != "loss_target"}
    for n in ("w_in", "m_w_in", "v_w_in"):
        local[n] = jnp.transpose(given[n][0])
    blocks = {n: local[n].astype(F32 if n == "meta_tokens" else BF16) for n in SHARDED}
    first = ("meta_tokens", "rwkv_w2", "rwkv_a2", "rwkv_g2")
    started = {n: _spread_start(blocks[n], False, "gather_start_" + n) for n in first}
    zero = sum(started[n][4][0, 0] for n in first)

    def gathered(n, after):
        own, got = _spread_wait(started[n][:4], after, False, "gather_wait_" + n)
        return _with_own_slot(got, own, mine)

    sm = {n: _as2d(w[n]) for n in SMALL}
    small_mix = tuple(sm[n] for n in SMALL[1:-1])
    n1 = sm["norm1_g"] + zero
    rw, fw = w_branch_a.shape[-2], w_branch_b.shape[-2]
    dims = (rw, fw, rwkv_w2.shape[-2], rwkv_a2.shape[-2], rwkv_g2.shape[-2])
    same = lambda s: (s,)

    meta, un_meta = jax.vjp(_cols_from_slots, gathered("meta_tokens", x2))
    (h0, xn), vjp_embed = jax.vjp(lambda m, xs, g: _stage_embed(m, xs, g, lp), meta, x2, n1)
    in_slots = _all_gather_call(blocks["w_in"])
    later = [n for n in SHARDED if n not in first and n != "w_in"]
    started.update({n: _spread_start(blocks[n], False, "gather_start_" + n, after=in_slots) for n in later})
    w_groups = _in_proj_layout(in_slots, *dims, whole=False)
    w_cat, un_in = jax.vjp(lambda s: _in_proj_layout(s, *dims, whole=True), in_slots)
    xn_b = xn.astype(BF16)
    behind = sum(started[n][4] for n in later)
    z_r, z_f, z_g = (_matmul(xn_b, wg, tb=True, name="in_proj_" + tag, after=behind) for wg, tag in zip(w_groups, "rfg"))
    (w2, un_w2), (a2, un_a2), (g2, un_g2) = (jax.vjp(_low_rank_layout, gathered(n, xn)) for n in ("rwkv_w2", "rwkv_a2", "rwkv_g2"))
    (y_a, y_b), vjp_mix = jax.vjp(lambda zr, zf, s, a, b, c: _stage_mix(zr, zf, s, a, b, c, dims),
                                  z_r, z_f, small_mix, w2, a2, g2)
    w_a, w_b = gathered("w_branch_a", y_a), gathered("w_branch_b", y_a)
    w_o_full, un_wo = jax.vjp(_rows_from_slots, gathered("w_o", y_a))
    h1, vjp_merge = jax.vjp(_stage_merge, h0, y_a, y_b, z_g, w_a, w_b, w_o_full)
    w_gu = gathered("w_gate_up", h1)
    w_dn, un_dn = jax.vjp(_rows_from_slots, gathered("w_down", h1))
    y, vjp_ffn = jax.vjp(_stage_ffn, h1, sm["norm2_g"], w_gu, w_dn)

    loss_part, dy_real = _loss_call(y[n_meta:tokens], loss_target[0])
    dy = jnp.pad(dy_real, ((n_meta, lp - tokens), (0, 0)))
    loss = lax.psum(loss_part[0, 0], MESH_AXES)

    sent = {}

    def send_grad(n, dmat, unlayout):
        sent[n] = _spread_start(unlayout(dmat)[0], True, "grad_start_" + n)
        return sent[n][4][0, 0]

    d_h1, d_n2, d_wgu, d_wdn = vjp_ffn(dy)
    behind = send_grad("w_gate_up", d_wgu, same) + send_grad("w_down", d_wdn, un_dn)
    d_h0, d_ya, d_yb, d_zg, d_wa, d_wb, d_wo = vjp_merge(d_h1 + behind)
    behind = send_grad("w_o", d_wo, un_wo) + send_grad("w_branch_a", d_wa, same) + send_grad("w_branch_b", d_wb, same)
    d_zr, d_zf, d_small_mix, d_w2, d_a2, d_g2 = vjp_mix((d_ya + behind, d_yb))
    dproj_b = jnp.concatenate([d_zr.astype(BF16), d_zf.astype(BF16), d_zg.astype(BF16)], axis=1)
    d_wcat = _matmul(dproj_b, xn_b, ta=True, out_dtype=BF16, name="in_proj_dw")
    send_grad("w_in", d_wcat, un_in)
    d_xn = _matmul(dproj_b, w_cat, out_dtype=F32, name="in_proj_dx", after=sent["w_in"][4])
    send_grad("rwkv_w2", d_w2, un_w2)
    send_grad("rwkv_a2", d_a2, un_a2)
    send_grad("rwkv_g2", d_g2, un_g2)
    d_meta, g_x, d_n1 = vjp_embed((d_h0, d_xn))
    send_grad("meta_tokens", d_meta, un_meta)

    small_grads = (d_n1, *d_small_mix, d_n2)
    n_small = sum(g.size for g in small_grads)
    width = -(-n_small // 1024) * 1024
    small_sent = _spread_start(_pack(small_grads, width), False, "small_grad_start")

    grads, delta, new_m, new_v = {}, {}, {}, {}
    after = g_x
    for n in ("w_gate_up", "w_down", "w_o", "w_branch_a", "w_branch_b", "rwkv_g2", "rwkv_a2", "rwkv_w2", "meta_tokens", "w_in"):
        src, got = _spread_wait(sent[n][:4], after, True, "grad_wait_" + n)
        own = lax.dynamic_index_in_dim(src, mine, 0, keepdims=False)
        g, d_, m_, v_ = _sum_adamw_call(got, own, local[n], local["m_" + n], local["v_" + n])
        back = (lambda t: jnp.transpose(t)[None]) if n == "w_in" else (lambda t: t.reshape(w[n].shape))
        grads[n], delta[n], new_m[n], new_v[n] = (back(t) for t in (g, d_, m_, v_))
        after = m_
    own_small, got_small = _spread_wait(small_sent[:4], after, False, "small_grad_wait")
    small_total = _unpack(_sum_slots_call(_with_own_slot(got_small, own_small, mine)), small_grads)
    grads.update({n: g.reshape(w[n].shape) for n, g in zip(SMALL, small_total)})
    packs = [_pack([src[n] if p == "" else given[p + n] for n in SMALL], width)
             for p, src in (("", w), ("", grads), ("m_", None), ("v_", None))]
    like = [w[n] for n in SMALL]
    for out, packed in zip((delta, new_m, new_v), _adamw_call(*packs)):
        out.update(dict(zip(SMALL, _unpack(packed, like))))

    return (loss, g_x[None], *[grads[n] for n in WEIGHTS], *[delta[n] for n in WEIGHTS],
            *[new_m[n] for n in WEIGHTS], *[new_v[n] for n in WEIGHTS])
```

```python
import jax
import jax.numpy as jnp
from jax import lax
from jax.experimental import pallas as pl
from jax.experimental.pallas import tpu as pltpu

F32 = jnp.float32
BF16 = jnp.bfloat16

N_DEV = 8
MESH_AXES = ("x", "y", "c")
HEAD_DIM = 64
TOKEN_TILE = 128
WKV_CHUNK = 64
WKV_PAIRS_PER_STEP = 8
PAIR = 2 * HEAD_DIM
ATTN_BLOCK = 128
ATTN_BLOCK_BIG = 384
RMS_EPS = 1e-6
GN_EPS = 64e-5
L2_FLOOR = 1e-12
NEG_BIG = -1e30
ADAM_LR, ADAM_B1, ADAM_B2, ADAM_EPS, ADAM_WD, ADAM_STEP = 0.001, 0.9, 0.999, 1e-08, 0.01, 10
VMEM_LIMIT_CAP = 56 * 1024 * 1024
VMEM_LIMIT_FLOOR = 32 * 1024 * 1024
MATMUL_VMEM_BUDGET = 36 * 1024 * 1024
GRID_STEP_BYTES = 1024 * 1024
ACC_BYTES_PER_HBM_BYTE = 6


def _vmem_limit(estimate_bytes):
    return int(min(max(estimate_bytes * 5 // 4, VMEM_LIMIT_FLOOR), VMEM_LIMIT_CAP))


def _row_tile(rows, width, itemsize=4, budget=2 * 1024 * 1024):
    for c in (1408, 1024, 704, 512, 384, 256, 128, 64, 32, 16, 8):
        if rows % c == 0 and c * width * itemsize <= budget:
            return c
    return rows


def _row_tile_ragged(rows, width, itemsize=4, budget=2 * 1024 * 1024):
    tile = _row_tile(rows, width, itemsize, budget)
    if tile * width * itemsize <= budget or rows < 16:
        return tile
    padded = -(-rows // 16) * 16
    for c in (1408, 1024, 704, 512, 384, 336, 256, 192, 128, 96, 64, 48, 32, 16):
        if padded % c == 0 and c * width * itemsize <= budget:
            return c
    return tile


def _dg(a, b, ta, tb):
    dims = (((0 if ta else 1,), (1 if tb else 0,)), ((), ()))
    return lax.dot_general(a, b, dims, preferred_element_type=F32)


def _split(x, n):
    parts = []
    for _ in range(n):
        h = x.astype(BF16)
        parts.append(h)
        x = x - h.astype(F32)
    return parts


def _mm(a, b, ta=False, tb=False):
    return _dg(a.astype(BF16), b.astype(BF16), ta, tb)


def _matmul(a, b, ta=False, tb=False, out_dtype=F32, name="matmul", after=None, b_slots=False, out_slots=0, add=None):
    if ta:
        kdim, m = a.shape
    else:
        m, kdim = a.shape
    if b_slots:
        n_slots, brows, bcols = b.shape
        n, k2 = (brows, n_slots * bcols) if tb else (n_slots * bcols, brows)
    elif tb:
        n, k2 = b.shape
    else:
        k2, n = b.shape
    assert kdim == k2, (a.shape, b.shape, ta, tb)
    sa, sb, so = a.dtype.itemsize, b.dtype.itemsize, jnp.dtype(out_dtype).itemsize
    n_unit = bcols if (b_slots and not tb) else (n // out_slots if out_slots else n)
    k_unit = bcols if (b_slots and tb) else kdim
    tm, tn, tk, n_outer = _matmul_tiles(m, n, kdim, ta, sa, sb, so, n_unit, k_unit)
    nk = kdim // tk
    ij = (lambda f: lambda j, i, k: f(i, j, k)) if n_outer else (lambda f: f)

    order = () if after is None else (after,)
    extra = () if add is None else (add,)

    def body(a_ref, b_ref, *rest):
        rest = rest[len(order):]
        add_ref = rest[0] if extra else None
        o_ref, acc = rest[len(extra)], rest[len(extra) + 1:]
        part = _dg(a_ref[...].astype(BF16), b_ref[...].astype(BF16), ta, tb)
        done = lambda total: (total if add_ref is None else total + add_ref[...]).astype(o_ref.dtype)
        if nk == 1:
            o_ref[...] = done(part)
            return
        kk = pl.program_id(2)

        @pl.when(kk == 0)
        def _():
            acc[0][...] = part

        @pl.when(kk > 0)
        def _():
            acc[0][...] += part

        @pl.when(kk == nk - 1)
        def _():
            o_ref[...] = done(acc[0][...])

    a_spec = pl.BlockSpec((tk, tm), ij(lambda i, j, k: (k, i))) if ta else pl.BlockSpec((tm, tk), ij(lambda i, j, k: (i, k)))
    if b_slots and tb:
        per = bcols // tk
        b_spec = pl.BlockSpec((None, tn, tk), ij(lambda i, j, k: (k // per, j, k % per)))
    elif b_slots:
        per = bcols // tn
        b_spec = pl.BlockSpec((None, tk, tn), ij(lambda i, j, k: (j // per, k, j % per)))
    elif tb:
        b_spec = pl.BlockSpec((tn, tk), ij(lambda i, j, k: (j, k)))
    else:
        b_spec = pl.BlockSpec((tk, tn), ij(lambda i, j, k: (k, j)))
    if out_slots:
        per_out = n // out_slots // tn
        out_spec = pl.BlockSpec((None, tm, tn), ij(lambda i, j, k: (j // per_out, i, j % per_out)))
        out_shape = jax.ShapeDtypeStruct((out_slots, m, n // out_slots), out_dtype)
    else:
        out_spec = pl.BlockSpec((tm, tn), ij(lambda i, j, k: (i, j)))
        out_shape = jax.ShapeDtypeStruct((m, n), out_dtype)
    return pl.pallas_call(
        body, name=name,
        grid=(n // tn, m // tm, nk) if n_outer else (m // tm, n // tn, nk),
        in_specs=[a_spec, b_spec] + [pl.BlockSpec(memory_space=pl.ANY)] * len(order)
                 + [pl.BlockSpec((tm, tn), ij(lambda i, j, k: (i, j)))] * len(extra),
        out_specs=out_spec,
        out_shape=out_shape,
        scratch_shapes=[pltpu.VMEM((tm, tn), F32)] if nk > 1 else [],
        compiler_params=pltpu.CompilerParams(
            dimension_semantics=("parallel", "parallel", "arbitrary"),
            vmem_limit_bytes=_vmem_limit(_matmul_vmem(tm, tn, tk, nk, sa, sb, so) + 2 * tm * tn * 4 * len(extra))),
    )(a, b, *order, *extra)


def _matmul_vmem(tm, tn, tk, nk, sa, sb, so):
    return 2 * (tm * tk * sa + tk * tn * sb + tm * tn * so) + tm * tn * 4 + (tm * tn * 4 if nk > 1 else 0)


def _matmul_tiles(m, n, kdim, ta, sa, sb, so, n_unit, k_unit):
    lane = (2816, 2176, 2048, 1408, 1024, 640, 512, 384, 256, 128)
    sublane = (2816, 2176, 2048, 1408, 1024, 704, 512, 384, 256, 128)
    divs = lambda dim, cands: [c for c in cands if dim % c == 0] or [dim]
    best = None
    for tm in divs(m, lane if ta else sublane):
        for tn in divs(n_unit, lane):
            for tk in divs(k_unit, sublane if ta else lane) + ([kdim] if k_unit == kdim and (ta or kdim <= 2048) else []):
                nk, nm, nn = kdim // tk, m // tm, n // tn
                if _matmul_vmem(tm, tn, tk, nk, sa, sb, so) > MATMUL_VMEM_BUDGET:
                    continue
                acc_bytes = m * n * 4 * 3 * nk // ACC_BYTES_PER_HBM_BYTE if nk > 1 else 0
                fixed = m * n * so + acc_bytes + nm * nn * nk * GRID_STEP_BYTES
                for n_outer in (False, True):
                    if n_outer:
                        a_reads, b_reads = (1 if (nk == 1 and nm == 1) else nn), (1 if nk == 1 else nm)
                    else:
                        a_reads, b_reads = (1 if nk == 1 else nn), (1 if (nk == 1 and nn == 1) else nm)
                    cost = m * kdim * sa * a_reads + kdim * n * sb * b_reads + fixed
                    if best is None or cost < best[0]:
                        best = (cost, tm, tn, tk, n_outer)
    return best[1:]


@jax.custom_vjp
def dense(x, w):
    return _matmul(x.astype(BF16), w, name="dense_fwd")


def _dense_fwd(x, w):
    return _matmul(x.astype(BF16), w, name="dense_fwd"), (x.astype(BF16), w, jnp.zeros((), x.dtype))


def _dense_bwd(res, dy):
    xb, w, like = res
    dyb = dy.astype(BF16)
    dx = _matmul(dyb, w, tb=True, out_dtype=like.dtype, name="dense_dx")
    dw = _matmul(xb, dyb, ta=True, out_dtype=w.dtype, name="dense_dw")
    return dx, dw


dense.defvjp(_dense_fwd, _dense_bwd)


@jax.custom_vjp
def dense_add(x, w, res):
    return _matmul(x.astype(BF16), w, name="dense_add_fwd", add=res)


def _dense_add_fwd(x, w, res):
    return _matmul(x.astype(BF16), w, name="dense_add_fwd", add=res), (x.astype(BF16), w, jnp.zeros((), x.dtype))


def _dense_add_bwd(res, dy):
    return (*_dense_bwd(res, dy), dy)


dense_add.defvjp(_dense_add_fwd, _dense_add_bwd)


def _make_dense_cols(out_dtype):
    @jax.custom_vjp
    def op(x, w_slots):
        return _matmul(x.astype(BF16), w_slots, b_slots=True, out_dtype=out_dtype, name="dense_cols_fwd")

    def fwd(x, w_slots):
        xb = x.astype(BF16)
        return (_matmul(xb, w_slots, b_slots=True, out_dtype=out_dtype, name="dense_cols_fwd"),
                (xb, w_slots, jnp.zeros((), x.dtype)))

    def bwd(res, dy):
        xb, w_slots, like = res
        dyb = dy.astype(BF16)
        dx = _matmul(dyb, w_slots, tb=True, b_slots=True, out_dtype=like.dtype, name="dense_cols_dx")
        dw = _matmul(xb, dyb, ta=True, out_slots=w_slots.shape[0], out_dtype=w_slots.dtype, name="dense_cols_dw")
        return dx, dw

    op.defvjp(fwd, bwd)
    return op


dense_cols_bf16 = _make_dense_cols(BF16)


def _swiglu_call(gu, d_act=None):
    rows, two_f = gu.shape
    f = two_f // 2
    tr = _row_tile(rows, two_f, itemsize=2, budget=3 * 1024 * 1024)
    half = lambda j: pl.BlockSpec((tr, f), lambda i, j=j: (i, j))
    ops = (gu, gu) if d_act is None else (gu, gu, d_act)

    def body(*refs):
        g, u = refs[0][...].astype(F32), refs[1][...].astype(F32)
        s = 1.0 / (1.0 + jnp.exp(-g))
        if d_act is None:
            refs[2][...] = (g * s * u).astype(BF16)
        else:
            d = refs[2][...].astype(F32)
            refs[3][:, :f] = (d * u * s * (1.0 + g * (1.0 - s))).astype(BF16)
            refs[3][:, f:] = (d * g * s).astype(BF16)

    width = f if d_act is None else two_f
    return pl.pallas_call(
        body, name="swiglu_fwd" if d_act is None else "swiglu_bwd", grid=(rows // tr,),
        in_specs=[half(0), half(1)] + ([half(0)] if d_act is not None else []),
        out_specs=pl.BlockSpec((tr, width), lambda i: (i, 0)),
        out_shape=jax.ShapeDtypeStruct((rows, width), BF16),
        compiler_params=pltpu.CompilerParams(dimension_semantics=("parallel",)),
    )(*ops)


@jax.custom_vjp
def swiglu(gu):
    return _swiglu_call(gu)


swiglu.defvjp(lambda gu: (_swiglu_call(gu), gu), lambda gu, d_act: (_swiglu_call(gu, d_act),))


def _merge_call(zg, a, b, dm=None):
    rows, d = a.shape
    tr = _row_tile(rows, d, budget=1024 * 1024)
    half = lambda j: pl.BlockSpec((tr, d), lambda i, j=j: (i, j))
    tile = half(0)

    def body(*refs):
        ga = 1.0 / (1.0 + jnp.exp(-refs[0][...]))
        gb = 1.0 / (1.0 + jnp.exp(-refs[1][...]))
        av, bv = refs[2][...].astype(F32), refs[3][...].astype(F32)
        if dm is None:
            refs[4][...] = (ga * av + gb * bv).astype(BF16)
        else:
            dv = refs[4][...].astype(F32)
            dzg_ref, da_ref, db_ref = refs[5:]
            dzg_ref[:, :d] = dv * av * ga * (1.0 - ga)
            dzg_ref[:, d:] = dv * bv * gb * (1.0 - gb)
            da_ref[...] = (dv * ga).astype(BF16)
            db_ref[...] = (dv * gb).astype(BF16)

    shape_b = jax.ShapeDtypeStruct((rows, d), BF16)
    if dm is None:
        out_specs, out_shape, ops = tile, shape_b, (zg, zg, a, b)
    else:
        out_specs = [pl.BlockSpec((tr, 2 * d), lambda i: (i, 0)), tile, tile]
        out_shape = [jax.ShapeDtypeStruct((rows, 2 * d), F32), shape_b, shape_b]
        ops = (zg, zg, a, b, dm)
    return pl.pallas_call(
        body, name="merge_fwd" if dm is None else "merge_bwd", grid=(rows // tr,),
        in_specs=[half(0), half(1)] + [tile] * (len(ops) - 2),
        out_specs=out_specs, out_shape=out_shape,
        compiler_params=pltpu.CompilerParams(dimension_semantics=("parallel",)),
    )(*ops)


@jax.custom_vjp
def gated_merge(zg, a, b):
    return _merge_call(zg, a, b)


gated_merge.defvjp(lambda zg, a, b: (_merge_call(zg, a, b), (zg, a, b)),
                   lambda res, dm: tuple(_merge_call(*res, dm)))


def _rms_fwd_call(x, g):
    rows, d = x.shape
    tr = _row_tile(rows, d)

    def body(x_ref, g_ref, y_ref):
        xv = x_ref[...]
        rstd = lax.rsqrt(jnp.mean(xv * xv, axis=1, keepdims=True) + RMS_EPS)
        y_ref[...] = ((xv * rstd) * g_ref[...]).astype(BF16)

    return pl.pallas_call(
        body, name="rms_fwd", grid=(rows // tr,),
        in_specs=[pl.BlockSpec((tr, d), lambda i: (i, 0)), pl.BlockSpec((1, d), lambda i: (0, 0))],
        out_specs=pl.BlockSpec((tr, d), lambda i: (i, 0)),
        out_shape=jax.ShapeDtypeStruct((rows, d), BF16),
        compiler_params=pltpu.CompilerParams(dimension_semantics=("parallel",)),
    )(x, g)


def _rms_bwd_call(x, g, dy):
    rows, d = x.shape
    tr = _row_tile(rows, d)

    def body(x_ref, g_ref, dy_ref, dx_ref, dg_ref):
        @pl.when(pl.program_id(0) == 0)
        def _():
            dg_ref[...] = jnp.zeros_like(dg_ref)

        xv = x_ref[...]
        dyv = dy_ref[...].astype(F32)
        rstd = lax.rsqrt(jnp.mean(xv * xv, axis=1, keepdims=True) + RMS_EPS)
        xhat = xv * rstd
        dxhat = dyv * g_ref[...]
        dx_ref[...] = rstd * (dxhat - xhat * jnp.mean(dxhat * xhat, axis=1, keepdims=True))
        dg_ref[...] += jnp.sum(dyv * xhat, axis=0, keepdims=True)

    return pl.pallas_call(
        body, name="rms_bwd", grid=(rows // tr,),
        in_specs=[pl.BlockSpec((tr, d), lambda i: (i, 0)), pl.BlockSpec((1, d), lambda i: (0, 0)),
                  pl.BlockSpec((tr, d), lambda i: (i, 0))],
        out_specs=[pl.BlockSpec((tr, d), lambda i: (i, 0)), pl.BlockSpec((1, d), lambda i: (0, 0))],
        out_shape=[jax.ShapeDtypeStruct((rows, d), F32), jax.ShapeDtypeStruct((1, d), F32)],
        compiler_params=pltpu.CompilerParams(dimension_semantics=("arbitrary",)),
    )(x, g, dy)


@jax.custom_vjp
def rmsnorm(x, g):
    return _rms_fwd_call(x, g)


rmsnorm.defvjp(lambda x, g: (_rms_fwd_call(x, g), (x, g)), lambda res, dy: tuple(_rms_bwd_call(res[0], res[1], dy)))


def _bcast_add_call(x, p):
    rows, d = x.shape
    tr = _row_tile(rows, d)

    def body(x_ref, p_ref, y_ref):
        y_ref[...] = x_ref[...] + p_ref[...]

    return pl.pallas_call(
        body, name="bcast_add", grid=(rows // tr,),
        in_specs=[pl.BlockSpec((tr, d), lambda i: (i, 0)), pl.BlockSpec((1, d), lambda i: (0, 0))],
        out_specs=pl.BlockSpec((tr, d), lambda i: (i, 0)),
        out_shape=jax.ShapeDtypeStruct((rows, d), F32),
        compiler_params=pltpu.CompilerParams(dimension_semantics=("parallel",)),
    )(x, p)


def _colsum_call(a):
    rows, d = a.shape
    tr = _row_tile(rows, d)

    def body(a_ref, o_ref):
        @pl.when(pl.program_id(0) == 0)
        def _():
            o_ref[...] = jnp.zeros_like(o_ref)

        o_ref[...] += jnp.sum(a_ref[...], axis=0, keepdims=True)

    return pl.pallas_call(
        body, name="colsum", grid=(rows // tr,),
        in_specs=[pl.BlockSpec((tr, d), lambda i: (i, 0))],
        out_specs=pl.BlockSpec((1, d), lambda i: (0, 0)),
        out_shape=jax.ShapeDtypeStruct((1, d), F32),
        compiler_params=pltpu.CompilerParams(dimension_semantics=("arbitrary",)),
    )(a)


@jax.custom_vjp
def badd(x, p):
    return _bcast_add_call(x, p)


badd.defvjp(lambda x, p: (_bcast_add_call(x, p), None), lambda res, dy: (dy, _colsum_call(dy)))


def _head_sums(x):
    i = lax.broadcasted_iota(jnp.int32, (PAIR, PAIR), 0) // HEAD_DIM
    j = lax.broadcasted_iota(jnp.int32, (PAIR, PAIR), 1) // HEAD_DIM
    ones = jnp.where(i == j, 1.0, 0.0).astype(BF16)
    hi, lo = _split(x, 2)
    cols = [slice(p * PAIR, (p + 1) * PAIR) for p in range(x.shape[1] // PAIR)]
    return jnp.concatenate([_dg(hi[:, c], ones, False, False) + _dg(lo[:, c], ones, False, False) for c in cols], axis=1)


def _head_rms_fwd_call(x, g):
    rows, w = x.shape
    tr = _row_tile(rows, w, budget=1024 * 1024)

    def body(x_ref, g_ref, y_ref):
        xv = x_ref[...]
        rstd = lax.rsqrt(_head_sums(xv * xv) * (1.0 / HEAD_DIM) + RMS_EPS)
        y_ref[...] = (xv * rstd) * g_ref[...]

    return pl.pallas_call(
        body, name="head_rms_fwd", grid=(rows // tr,),
        in_specs=[pl.BlockSpec((tr, w), lambda i: (i, 0)), pl.BlockSpec((1, w), lambda i: (0, 0))],
        out_specs=pl.BlockSpec((tr, w), lambda i: (i, 0)),
        out_shape=jax.ShapeDtypeStruct((rows, w), F32),
        compiler_params=pltpu.CompilerParams(dimension_semantics=("parallel",)),
    )(x, g)


def _head_rms_bwd_call(x, g, dy):
    rows, w = x.shape
    tr = _row_tile(rows, w, budget=1024 * 1024)

    def body(x_ref, g_ref, dy_ref, dx_ref, dg_ref):
        @pl.when(pl.program_id(0) == 0)
        def _():
            dg_ref[...] = jnp.zeros_like(dg_ref)

        xv, dyv = x_ref[...], dy_ref[...]
        rstd = lax.rsqrt(_head_sums(xv * xv) * (1.0 / HEAD_DIM) + RMS_EPS)
        xhat = xv * rstd
        dxhat = dyv * g_ref[...]
        dx_ref[...] = rstd * (dxhat - xhat * (_head_sums(dxhat * xhat) * (1.0 / HEAD_DIM)))
        dg_ref[...] += jnp.sum(dyv * xhat, axis=0, keepdims=True)

    return pl.pallas_call(
        body, name="head_rms_bwd", grid=(rows // tr,),
        in_specs=[pl.BlockSpec((tr, w), lambda i: (i, 0)), pl.BlockSpec((1, w), lambda i: (0, 0)),
                  pl.BlockSpec((tr, w), lambda i: (i, 0))],
        out_specs=[pl.BlockSpec((tr, w), lambda i: (i, 0)), pl.BlockSpec((1, w), lambda i: (0, 0))],
        out_shape=[jax.ShapeDtypeStruct((rows, w), F32), jax.ShapeDtypeStruct((1, w), F32)],
        compiler_params=pltpu.CompilerParams(dimension_semantics=("arbitrary",)),
    )(x, g, dy)


@jax.custom_vjp
def head_rms(x, g):
    return _head_rms_fwd_call(x, g)


head_rms.defvjp(lambda x, g: (_head_rms_fwd_call(x, g), (x, g)),
                lambda res, dy: tuple(_head_rms_bwd_call(res[0], res[1], dy)))


def _gn_fwd_call(y, r, kf, v, g, gw, gb, rk):
    rows, w = y.shape
    tr = _row_tile(rows, w, budget=512 * 1024)

    def body(y_ref, r_ref, kf_ref, v_ref, g_ref, gw_ref, gb_ref, rk_ref, o_ref):
        yv = y_ref[...]
        yc = yv - _head_sums(yv) * (1.0 / HEAD_DIM)
        rstd = lax.rsqrt(_head_sums(yc * yc) * (1.0 / HEAD_DIM) + GN_EPS)
        s = _head_sums(r_ref[...] * kf_ref[...] * rk_ref[...])
        o_ref[...] = ((yc * rstd) * gw_ref[...] + gb_ref[...] + s * v_ref[...]) * g_ref[...]

    tok = pl.BlockSpec((tr, w), lambda i: (i, 0))
    par = pl.BlockSpec((1, w), lambda i: (0, 0))
    return pl.pallas_call(
        body, name="gn_bonus_fwd", grid=(rows // tr,),
        in_specs=[tok] * 5 + [par] * 3, out_specs=tok,
        out_shape=jax.ShapeDtypeStruct((rows, w), F32),
        compiler_params=pltpu.CompilerParams(dimension_semantics=("parallel",)),
    )(y, r, kf, v, g, gw, gb, rk)


def _gn_bwd_call(y, r, kf, v, g, gw, gb, rk, do):
    rows, w = y.shape
    tr = _row_tile(rows, w, budget=512 * 1024)

    def body(y_ref, r_ref, kf_ref, v_ref, g_ref, gw_ref, gb_ref, rk_ref, do_ref,
             dy_ref, dr_ref, dkf_ref, dv_ref, dg_ref, dgw_ref, dgb_ref, drk_ref):
        @pl.when(pl.program_id(0) == 0)
        def _():
            dgw_ref[...] = jnp.zeros_like(dgw_ref)
            dgb_ref[...] = jnp.zeros_like(dgb_ref)
            drk_ref[...] = jnp.zeros_like(drk_ref)

        yv, rv, kv, vv, rkv = y_ref[...], r_ref[...], kf_ref[...], v_ref[...], rk_ref[...]
        mean = lambda t: _head_sums(t) * (1.0 / HEAD_DIM)
        yc = yv - mean(yv)
        rstd = lax.rsqrt(mean(yc * yc) + GN_EPS)
        yhat = yc * rstd
        s = _head_sums(rv * kv * rkv)
        dg_ref[...] = do_ref[...] * (yhat * gw_ref[...] + gb_ref[...] + s * vv)
        dov = do_ref[...] * g_ref[...]
        dyhat = dov * gw_ref[...]
        dy_ref[...] = rstd * (dyhat - mean(dyhat) - yhat * mean(dyhat * yhat))
        ds = _head_sums(dov * vv)
        dv_ref[...] = s * dov
        dr_ref[...] = ds * kv * rkv
        dkf_ref[...] = ds * rv * rkv
        dgw_ref[...] += jnp.sum(dov * yhat, axis=0, keepdims=True)
        dgb_ref[...] += jnp.sum(dov, axis=0, keepdims=True)
        drk_ref[...] += jnp.sum(ds * rv * kv, axis=0, keepdims=True)

    tok = pl.BlockSpec((tr, w), lambda i: (i, 0))
    par = pl.BlockSpec((1, w), lambda i: (0, 0))
    tshape = jax.ShapeDtypeStruct((rows, w), F32)
    pshape = jax.ShapeDtypeStruct((1, w), F32)
    return pl.pallas_call(
        body, name="gn_bonus_bwd", grid=(rows // tr,),
        in_specs=[tok] * 5 + [par] * 3 + [tok], out_specs=[tok] * 5 + [par] * 3,
        out_shape=[tshape] * 5 + [pshape] * 3,
        compiler_params=pltpu.CompilerParams(dimension_semantics=("arbitrary",)),
    )(y, r, kf, v, g, gw, gb, rk, do)


@jax.custom_vjp
def gn_bonus(y, r, kf, v, g, gw, gb, rk):
    return _gn_fwd_call(y, r, kf, v, g, gw, gb, rk)


def _gn_bwd(res, do):
    return tuple(_gn_bwd_call(*res, do))


gn_bonus.defvjp(lambda *a: (_gn_fwd_call(*a), a), _gn_bwd)


PREP_ROWS = 128


def _prep_segments(rw, lora_w, lora_a, lora_g):
    at = 3 * rw
    seg = {"r": (0, rw), "k": (rw, 2 * rw), "v": (2 * rw, 3 * rw)}
    for name, n in (("wd", lora_w), ("ad", lora_a), ("gd", lora_g)):
        seg[name] = (at, at + _pad128(n))
        at += _pad128(n)
    return seg, at


def _prep_shifted(z_ref, zlast_ref, mu_ref, seg, first_tile):
    lo, hi = seg
    zr = z_ref[:, lo:hi]
    rows = zr.shape[0]
    before = jnp.where(first_tile, 0.0, zlast_ref[7:8, lo:hi])
    row0 = lax.broadcasted_iota(jnp.int32, zr.shape, 0) == 0
    diff = jnp.where(row0, before, pltpu.roll(zr, 1, axis=0)) - zr
    return zr + diff * mu_ref[:, lo:hi], diff


def _prep_forward_values(z_ref, zlast_ref, mu_ref, w0_ref, a0_ref, kk_ref, ka_ref, w2_ref, a2_ref, g2_ref, segs, first_tile):
    z = {n: _prep_shifted(z_ref, zlast_ref, mu_ref, segs[n], first_tile) for n in segs}
    r, k, v, wd, ad, gd = (z[n][0] for n in ("r", "k", "v", "wd", "ad", "gd"))
    twd = jnp.tanh(wd)
    pw = _mm(twd, w2_ref[...]) + w0_ref[...]
    lw = -jnp.exp(-(jnp.maximum(-pw, 0.0) + jnp.log(1.0 + jnp.exp(-jnp.abs(pw)))) - 0.5)
    a_sig = 1.0 / (1.0 + jnp.exp(-(_mm(ad, a2_ref[...]) + a0_ref[...])))
    sg = 1.0 / (1.0 + jnp.exp(-gd))
    kx = k * kk_ref[...]
    nrm = jnp.sqrt(_head_sums(kx * kx))
    inv = 1.0 / jnp.maximum(nrm, L2_FLOOR)
    return dict(z=z, r=r, k=k, v=v, twd=twd, pw=pw, lw=lw, a_sig=a_sig, sg=sg, ad=ad, kk=kx * inv, inv=inv, live=nrm > L2_FLOOR)


def _prep_specs(tokens, rpad, rw, w2, a2, g2):
    tr = PREP_ROWS
    tile = lambda w: pl.BlockSpec((tr, w), lambda i: (i, 0))
    before = pl.BlockSpec((8, rpad), lambda i: (jnp.maximum(i * (tr // 8) - 1, 0), 0))
    whole = lambda a: pl.BlockSpec(a.shape, lambda i: (0, 0))
    par = pl.BlockSpec((1, rw), lambda i: (0, 0))
    return tile, before, whole, par, pl.BlockSpec((1, rpad), lambda i: (0, 0))


def _prep_fwd_call(zr, mu, w0, a0, k_k, k_a, w2, a2, g2):
    tokens, rpad = zr.shape
    rw = w0.shape[1]
    segs, _ = _prep_segments(rw, w2.shape[0], a2.shape[0], g2.shape[0])
    tile, before, whole, par, mu_spec = _prep_specs(tokens, rpad, rw, w2, a2, g2)

    def body(z_ref, zlast_ref, mu_ref, w0_ref, a0_ref, kk_ref, ka_ref, w2_ref, a2_ref, g2_ref,
             r_ref, lw_ref, kf_ref, v_ref, na_ref, b_ref, g_ref):
        f = _prep_forward_values(z_ref, zlast_ref, mu_ref, w0_ref, a0_ref, kk_ref, ka_ref, w2_ref, a2_ref, g2_ref,
                                 segs, pl.program_id(0) == 0)
        r_ref[...] = f["r"]
        v_ref[...] = f["v"]
        lw_ref[...] = f["lw"]
        kf_ref[...] = f["k"] * (1.0 + (f["a_sig"] - 1.0) * ka_ref[...])
        na_ref[...] = -f["kk"]
        b_ref[...] = f["kk"] * f["a_sig"]
        g_ref[...] = _mm(f["sg"], g2_ref[...])

    shape = jax.ShapeDtypeStruct((tokens, rw), F32)
    return pl.pallas_call(
        body, name="rwkv_prep_fwd", grid=(tokens // PREP_ROWS,),
        in_specs=[tile(rpad), before, mu_spec, par, par, par, par, whole(w2), whole(a2), whole(g2)],
        out_specs=[tile(rw)] * 7, out_shape=[shape] * 7,
        compiler_params=pltpu.CompilerParams(dimension_semantics=("parallel",), vmem_limit_bytes=VMEM_LIMIT_CAP),
    )(zr, zr, mu, w0, a0, k_k, k_a, w2, a2, g2)


def _prep_bwd_call(zr, mu, w0, a0, k_k, k_a, w2, a2, g2, cts):
    tokens, rpad = zr.shape
    rw = w0.shape[1]
    segs, _ = _prep_segments(rw, w2.shape[0], a2.shape[0], g2.shape[0])
    tile, before, whole, par, mu_spec = _prep_specs(tokens, rpad, rw, w2, a2, g2)
    nt = tokens // PREP_ROWS
    rev = lambda spec: pl.BlockSpec(spec.block_shape, lambda i, f=spec.index_map: f(nt - 1 - i))

    def body(z_ref, zlast_ref, mu_ref, w0_ref, a0_ref, kk_ref, ka_ref, w2_ref, a2_ref, g2_ref,
             dr_ref, dlw_ref, dkf_ref, dv_ref, dna_ref, db_ref, dg_ref,
             dz_ref, dmu_ref, dw0_ref, da0_ref, dkk_ref, dka_ref, dw2_ref, da2_ref, dg2_ref, carry):
        step = pl.program_id(0)

        @pl.when(step == 0)
        def _():
            for ref in (dmu_ref, dw0_ref, da0_ref, dkk_ref, dka_ref, dw2_ref, da2_ref, dg2_ref, carry):
                ref[...] = jnp.zeros_like(ref)

        f = _prep_forward_values(z_ref, zlast_ref, mu_ref, w0_ref, a0_ref, kk_ref, ka_ref, w2_ref, a2_ref, g2_ref,
                                 segs, step == nt - 1)
        k, kk, a_sig, sg, twd = f["k"], f["kk"], f["a_sig"], f["sg"], f["twd"]
        colsum = lambda t: jnp.sum(t, axis=0, keepdims=True)
        dkf, db, dg = dkf_ref[...], db_ref[...], dg_ref[...]
        ka = ka_ref[...]
        dgd = _mm(dg, g2_ref[...], tb=True) * sg * (1.0 - sg)
        dg2_ref[...] += _mm(sg, dg, ta=True)
        dkk = db * a_sig - dna_ref[...]
        da_sig = db * kk + dkf * k * ka
        dk = dkf * (1.0 + (a_sig - 1.0) * ka)
        dka_ref[...] += colsum(dkf * k * (a_sig - 1.0))
        along = jnp.where(f["live"], _head_sums(dkk * kk), 0.0)
        dkx = (dkk - kk * along) * f["inv"]
        dk = dk + dkx * kk_ref[...]
        dkk_ref[...] += colsum(dkx * k)
        dpa = da_sig * a_sig * (1.0 - a_sig)
        da0_ref[...] += colsum(dpa)
        dad = _mm(dpa, a2_ref[...], tb=True)
        da2_ref[...] += _mm(f["ad"], dpa, ta=True)
        dpw = dlw_ref[...] * f["lw"] / (1.0 + jnp.exp(f["pw"]))
        dw0_ref[...] += colsum(dpw)
        dwd = _mm(dpw, w2_ref[...], tb=True) * (1.0 - twd * twd)
        dw2_ref[...] += _mm(twd, dpw, ta=True)
        rows = PREP_ROWS
        last = lax.broadcasted_iota(jnp.int32, (rows, 1), 0) == rows - 1
        for name, dz in (("r", dr_ref[...]), ("k", dk), ("v", dv_ref[...]), ("wd", dwd), ("ad", dad), ("gd", dgd)):
            lo, hi = segs[name]
            mu_s = mu_ref[:, lo:hi]
            dmu_ref[:, lo:hi] += colsum(dz * f["z"][name][1])
            later = dz * mu_s
            dz_ref[:, lo:hi] = dz * (1.0 - mu_s) + jnp.where(last, carry[:, lo:hi], pltpu.roll(later, rows - 1, axis=0))
            carry[:, lo:hi] = later[0:1, :]

    tok = jax.ShapeDtypeStruct((tokens, rw), F32)
    acc = lambda a: jax.ShapeDtypeStruct(a.shape, F32)
    return pl.pallas_call(
        body, name="rwkv_prep_bwd", grid=(nt,),
        in_specs=[rev(tile(rpad)), rev(before), mu_spec, par, par, par, par, whole(w2), whole(a2), whole(g2)]
                 + [rev(tile(rw))] * 7,
        out_specs=[rev(tile(rpad)), mu_spec, par, par, par, par, whole(w2), whole(a2), whole(g2)],
        out_shape=[jax.ShapeDtypeStruct((tokens, rpad), F32), acc(mu), acc(w0), acc(a0), acc(k_k), acc(k_a), acc(w2), acc(a2), acc(g2)],
        scratch_shapes=[pltpu.VMEM((1, rpad), F32)],
        compiler_params=pltpu.CompilerParams(dimension_semantics=("arbitrary",), vmem_limit_bytes=VMEM_LIMIT_CAP),
    )(zr, zr, mu, w0, a0, k_k, k_a, w2, a2, g2, *cts)


@jax.custom_vjp
def rwkv_prep(zr, mu, w0, a0, k_k, k_a, w2, a2, g2):
    return tuple(_prep_fwd_call(zr, mu, w0, a0, k_k, k_a, w2, a2, g2))


def _rwkv_prep_bwd(res, cts):
    zr, mu, w0, a0, k_k, k_a, w2, a2, g2 = res
    dz, dmu, dw0, da0, dkk, dka, dw2, da2, dg2 = _prep_bwd_call(*res, cts)
    return dz, dmu, dw0, da0, dkk, dka, dw2.astype(w2.dtype), da2.astype(a2.dtype), dg2.astype(g2.dtype)


rwkv_prep.defvjp(lambda *a: (tuple(_prep_fwd_call(*a)), a), _rwkv_prep_bwd)


def _pair_masks(rows):
    lane = lax.broadcasted_iota(jnp.int32, (rows, PAIR), 1)
    return lane < HEAD_DIM, lane >= HEAD_DIM


def _bd(x):
    m0, m1 = _pair_masks(x.shape[0])
    return jnp.concatenate([jnp.where(m0, x, 0.0), jnp.where(m1, x, 0.0)], axis=0)


def _unbd(m, c):
    return jnp.where(_pair_masks(c)[0], m[:c], m[c:])


def _pair_a(l2, r2):
    return _mm(l2, _bd(r2), tb=True)


def _pair_mul(p2, x2):
    return _mm(p2, _bd(x2))


def _pair_mul_t(p2, x2):
    return _unbd(_mm(p2, x2, ta=True), p2.shape[0])


def _block_diag_mask():
    row = lax.broadcasted_iota(jnp.int32, (PAIR, PAIR), 0)
    lane = lax.broadcasted_iota(jnp.int32, (PAIR, PAIR), 1)
    return (row < HEAD_DIM) == (lane < HEAD_DIM), row == lane


def _wkv_pair_common(r, lw, k, a, b):
    c = r[0].shape[0]
    pairs = range(len(r))
    i = lax.broadcasted_iota(jnp.int32, (c, PAIR), 0)
    j = lax.broadcasted_iota(jnp.int32, (c, PAIR), 1) % c
    strict, incl = i > j, i >= j
    ti = lax.broadcasted_iota(jnp.int32, (c, c), 0)
    tj = lax.broadcasted_iota(jnp.int32, (c, c), 1)
    tri = jnp.where(ti >= tj, 1.0, 0.0).astype(BF16)
    lc = [sum(_dg(tri, part, False, False) for part in _split(lw[p], 3)) for p in pairs]
    lend = [lc[p][c - 1:c, :] for p in pairs]
    rt = [r[p] * jnp.exp(lc[p]) for p in pairs]
    at = [a[p] * jnp.exp(lc[p] - lw[p]) for p in pairs]
    pinv = [jnp.exp(-lc[p]) for p in pairs]
    kt = [k[p] * pinv[p] for p in pairs]
    bt = [b[p] * pinv[p] for p in pairs]
    e = [jnp.exp(lend[p] - lc[p]) for p in pairs]
    ktp = [k[p] * e[p] for p in pairs]
    btp = [b[p] * e[p] for p in pairs]
    a_ab = [jnp.where(strict, _pair_a(at[p], bt[p]), 0.0) for p in pairs]
    a_ak = [jnp.where(strict, _pair_a(at[p], kt[p]), 0.0) for p in pairs]
    a_rb = [jnp.where(incl, _pair_a(rt[p], bt[p]), 0.0) for p in pairs]
    a_rk = [jnp.where(incl, _pair_a(rt[p], kt[p]), 0.0) for p in pairs]
    t = [jnp.where(i == j, 1.0, 0.0) + a_ab[p] for p in pairs]
    xp = a_ab
    n = 2
    while n < c:
        xp = [_pair_mul(xp[p], xp[p]) for p in pairs]
        t = [t[p] + _pair_mul(t[p], xp[p]) for p in pairs]
        n *= 2
    bdm, eye = _block_diag_mask()
    pend_col = [jnp.sum(jnp.where(eye, jnp.exp(lend[p]), 0.0), axis=1, keepdims=True) for p in pairs]
    return dict(rt=rt, at=at, kt=kt, bt=bt, ktp=ktp, btp=btp, a_ak=a_ak, a_rb=a_rb, a_rk=a_rk, t=t,
                pend_col=pend_col, lend=lend, lc=lc, strict=strict, incl=incl, tri=tri, bdm=bdm)


def _wkv_group(width):
    npair = width // PAIR
    g = min(WKV_PAIRS_PER_STEP, npair)
    assert npair % g == 0
    return npair, g


def _wkv_fwd_call(r, lw, k, v, a, b):
    tokens, width = r.shape
    c = WKV_CHUNK
    nc = tokens // c
    npair, g = _wkv_group(width)

    def body(r_ref, lw_ref, k_ref, v_ref, a_ref, b_ref, y_ref, s_ref, st):
        @pl.when(pl.program_id(1) == 0)
        def _():
            st[...] = jnp.zeros_like(st)

        pairs = range(g)
        rv, lwv, kv, vv, av, bv = ([ref[:, p * PAIR:(p + 1) * PAIR] for p in pairs]
                                   for ref in (r_ref, lw_ref, k_ref, v_ref, a_ref, b_ref))
        s0 = [st[p] for p in pairs]
        q = _wkv_pair_common(rv, lwv, kv, av, bv)
        w1 = [_mm(q["at"][p], s0[p]) + _pair_mul(q["a_ak"][p], vv[p]) for p in pairs]
        u = [_pair_mul(q["t"][p], w1[p]) for p in pairs]
        y = [_mm(q["rt"][p], s0[p]) + _pair_mul(q["a_rb"][p], u[p]) + _pair_mul(q["a_rk"][p], vv[p]) for p in pairs]
        grow = [_mm(jnp.concatenate([q["btp"][p], q["ktp"][p]], axis=0), jnp.concatenate([u[p], vv[p]], axis=0), ta=True)
                for p in pairs]
        for p in pairs:
            y_ref[:, p * PAIR:(p + 1) * PAIR] = y[p]
            s_ref[0, p] = s0[p]
            st[p] = q["pend_col"][p] * s0[p] + jnp.where(q["bdm"], grow[p], 0.0)

    tok = pl.BlockSpec((c, g * PAIR), lambda gi, ci: (ci, gi))
    return pl.pallas_call(
        body, name="wkv_fwd", grid=(npair // g, nc),
        in_specs=[tok] * 6,
        out_specs=[tok, pl.BlockSpec((1, g, PAIR, PAIR), lambda gi, ci: (ci, gi, 0, 0))],
        out_shape=[jax.ShapeDtypeStruct((tokens, width), F32), jax.ShapeDtypeStruct((nc, npair, PAIR, PAIR), F32)],
        scratch_shapes=[pltpu.VMEM((g, PAIR, PAIR), F32)],
        compiler_params=pltpu.CompilerParams(dimension_semantics=("parallel", "arbitrary")),
    )(r, lw, k, v, a, b)


def _wkv_bwd_call(r, lw, k, v, a, b, s, dy):
    tokens, width = r.shape
    c = WKV_CHUNK
    nc = tokens // c
    npair, g = _wkv_group(width)

    def body(r_ref, lw_ref, k_ref, v_ref, a_ref, b_ref, s_ref, dy_ref,
             dr_ref, dlw_ref, dk_ref, dv_ref, da_ref, db_ref, dst):
        @pl.when(pl.program_id(1) == 0)
        def _():
            dst[...] = jnp.zeros_like(dst)

        pairs = range(g)
        rv, lwv, kv, vv, av, bv, dyv = ([ref[:, p * PAIR:(p + 1) * PAIR] for p in pairs]
                                        for ref in (r_ref, lw_ref, k_ref, v_ref, a_ref, b_ref, dy_ref))
        s0 = [s_ref[0, p] for p in pairs]
        dsc = [dst[p] for p in pairs]
        q = _wkv_pair_common(rv, lwv, kv, av, bv)
        rt, at, kt, bt, ktp, btp, t = (q[n] for n in ("rt", "at", "kt", "bt", "ktp", "btp", "t"))
        a_ak, a_rb, a_rk, strict, incl = (q[n] for n in ("a_ak", "a_rb", "a_rk", "strict", "incl"))
        w1 = [_mm(at[p], s0[p]) + _pair_mul(a_ak[p], vv[p]) for p in pairs]
        u = [_pair_mul(t[p], w1[p]) for p in pairs]
        du = [_pair_mul_t(a_rb[p], dyv[p]) + _mm(btp[p], dsc[p]) for p in pairs]
        dw1 = [_pair_mul_t(t[p], du[p]) for p in pairs]
        dv = [_pair_mul_t(a_rk[p], dyv[p]) + _mm(ktp[p], dsc[p]) + _pair_mul_t(a_ak[p], dw1[p]) for p in pairs]
        da_ab = [jnp.where(strict, _pair_a(dw1[p], u[p]), 0.0) for p in pairs]
        da_ak = [jnp.where(strict, _pair_a(dw1[p], vv[p]), 0.0) for p in pairs]
        da_rb = [jnp.where(incl, _pair_a(dyv[p], u[p]), 0.0) for p in pairs]
        da_rk = [jnp.where(incl, _pair_a(dyv[p], vv[p]), 0.0) for p in pairs]
        d_rt = [_mm(dyv[p], s0[p], tb=True) + _pair_mul(da_rb[p], bt[p]) + _pair_mul(da_rk[p], kt[p]) for p in pairs]
        d_at = [_mm(dw1[p], s0[p], tb=True) + _pair_mul(da_ab[p], bt[p]) + _pair_mul(da_ak[p], kt[p]) for p in pairs]
        d_bt = [_pair_mul_t(da_ab[p], at[p]) + _pair_mul_t(da_rb[p], rt[p]) for p in pairs]
        d_kt = [_pair_mul_t(da_ak[p], at[p]) + _pair_mul_t(da_rk[p], rt[p]) for p in pairs]
        d_btp = [_mm(u[p], dsc[p], tb=True) for p in pairs]
        d_ktp = [_mm(vv[p], dsc[p], tb=True) for p in pairs]
        ones = jnp.ones((8, PAIR), BF16)
        dpend = [sum(_dg(ones, part, False, True) for part in _split(dsc[p] * s0[p], 3))[0:1, :] * jnp.exp(q["lend"][p])
                 for p in pairs]
        grow = [_mm(jnp.concatenate([rt[p], at[p]], axis=0), jnp.concatenate([dyv[p], dw1[p]], axis=0), ta=True)
                for p in pairs]
        last = lax.broadcasted_iota(jnp.int32, (c, PAIR), 0) == c - 1
        for p in pairs:
            sl = slice(p * PAIR, (p + 1) * PAIR)
            dst[p] = q["pend_col"][p] * dsc[p] + jnp.where(q["bdm"], grow[p], 0.0)
            lc_e = d_ktp[p] * ktp[p] + d_btp[p] * btp[p]
            dlend = jnp.sum(lc_e, axis=0, keepdims=True) + dpend[p]
            dlc = d_rt[p] * rt[p] - d_kt[p] * kt[p] - d_bt[p] * bt[p] - lc_e + jnp.where(last, dlend, 0.0)
            dlp = d_at[p] * at[p]
            dlw_ref[:, sl] = sum(_dg(q["tri"], part, True, False) for part in _split(dlc + dlp, 3)) - dlp
            lc = q["lc"][p]
            pinv = jnp.exp(-lc)
            e = jnp.exp(q["lend"][p] - lc)
            dr_ref[:, sl] = d_rt[p] * jnp.exp(lc)
            da_ref[:, sl] = d_at[p] * jnp.exp(lc - lwv[p])
            dk_ref[:, sl] = d_kt[p] * pinv + d_ktp[p] * e
            db_ref[:, sl] = d_bt[p] * pinv + d_btp[p] * e
            dv_ref[:, sl] = dv[p]

    tok = pl.BlockSpec((c, g * PAIR), lambda gi, ci: (nc - 1 - ci, gi))
    tshape = jax.ShapeDtypeStruct((tokens, width), F32)
    return pl.pallas_call(
        body, name="wkv_bwd", grid=(npair // g, nc),
        in_specs=[tok] * 6 + [pl.BlockSpec((1, g, PAIR, PAIR), lambda gi, ci: (nc - 1 - ci, gi, 0, 0)), tok],
        out_specs=[tok] * 6, out_shape=[tshape] * 6,
        scratch_shapes=[pltpu.VMEM((g, PAIR, PAIR), F32)],
        compiler_params=pltpu.CompilerParams(dimension_semantics=("parallel", "arbitrary")),
    )(r, lw, k, v, a, b, s, dy)


@jax.custom_vjp
def wkv7(r, lw, k, v, a, b):
    return _wkv_fwd_call(r, lw, k, v, a, b)[0]


def _wkv7_fwd(r, lw, k, v, a, b):
    y, s = _wkv_fwd_call(r, lw, k, v, a, b)
    return y, (r, lw, k, v, a, b, s)


wkv7.defvjp(_wkv7_fwd, lambda res, dy: tuple(_wkv_bwd_call(*res, dy)))


def _attn_block(tokens):
    return ATTN_BLOCK_BIG if tokens % ATTN_BLOCK_BIG == 0 else ATTN_BLOCK


def _fox_layouts(cum):
    tokens, heads = cum.shape
    t = _attn_block(tokens)
    cq = cum.reshape(tokens, heads // 2, 2).transpose(1, 0, 2)
    ck = cum.T.reshape(heads // 2, 2, tokens // t, t).transpose(0, 2, 1, 3)
    return cq, ck


def _head_lane_masks(rows):
    lane = lax.broadcasted_iota(jnp.int32, (rows, 2 * HEAD_DIM), 1)
    return [lane < HEAD_DIM, lane >= HEAD_DIM]


def _fox_fwd_call(q, k, v, cq, ck):
    tokens, width = q.shape
    t = _attn_block(tokens)
    nb = tokens // t
    hd = HEAD_DIM
    npair = width // (2 * hd)

    def body(q_ref, k_ref, v_ref, cq_ref, ck_ref, o_ref, lse_ref):
        i = pl.program_id(1)
        masks = _head_lane_masks(t)
        q2 = q_ref[...]
        qs = [jnp.where(mk, q2, 0.0).astype(BF16) for mk in masks]
        cqs = [cq_ref[0, :, hh:hh + 1] for hh in range(2)]

        def block(j, carry, diagonal):
            off = pl.multiple_of(j * t, t)
            ckj = ck_ref[0, j]
            k2 = k_ref[pl.ds(off, t), :].astype(BF16)
            v2 = v_ref[pl.ds(off, t), :].astype(BF16)
            out = []
            for hh in range(2):
                m, l, acc = carry[hh]
                s = _dg(qs[hh], k2, False, True) + (cqs[hh] - ckj[hh:hh + 1, :])
                if diagonal:
                    keep = lax.broadcasted_iota(jnp.int32, (t, t), 0) >= lax.broadcasted_iota(jnp.int32, (t, t), 1)
                    s = jnp.where(keep, s, NEG_BIG)
                m_new = jnp.maximum(m, jnp.max(s, axis=1, keepdims=True))
                alpha = jnp.exp(m - m_new)
                p = jnp.exp(s - m_new)
                l = alpha * l + jnp.sum(p, axis=1, keepdims=True)
                acc = alpha * acc + _dg(p.astype(BF16), v2, False, False)
                out.append((m_new, l, acc))
            return tuple(out)

        init = tuple((jnp.full((t, 1), NEG_BIG, F32), jnp.zeros((t, 1), F32), jnp.zeros((t, 2 * hd), F32)) for _ in range(2))
        res = lax.fori_loop(0, i, lambda j, c: block(j, c, False), init)
        res = block(i, res, True)
        o_ref[...] = jnp.where(masks[0], res[0][2] / res[0][1], res[1][2] / res[1][1])
        for hh in range(2):
            lse_ref[0, :, hh:hh + 1] = res[hh][0] + jnp.log(res[hh][1])

    blk = pl.BlockSpec((t, 2 * hd), lambda hp, i: (i, hp))
    full = pl.BlockSpec((tokens, 2 * hd), lambda hp, i: (0, hp))
    cq_spec = pl.BlockSpec((1, t, 2), lambda hp, i: (hp, i, 0))
    ck_spec = pl.BlockSpec((1, nb, 2, t), lambda hp, i: (hp, 0, 0, 0))
    return pl.pallas_call(
        body, name="fox_fwd", grid=(npair, nb),
        in_specs=[blk, full, full, cq_spec, ck_spec],
        out_specs=[blk, cq_spec],
        out_shape=[jax.ShapeDtypeStruct((tokens, width), F32), jax.ShapeDtypeStruct((npair, tokens, 2), F32)],
        compiler_params=pltpu.CompilerParams(dimension_semantics=("parallel", "arbitrary")),
    )(q, k, v, cq, ck)


def _fox_bwd_call(q, k, v, cq, ck, o, lse, do):
    tokens, width = q.shape
    t = _attn_block(tokens)
    nb = tokens // t
    hd = HEAD_DIM
    npair = width // (2 * hd)

    def body(q_ref, k_ref, v_ref, cq_ref, ck_ref, o_ref, lse_ref, do_ref, dq_ref, dk_ref, dv_ref, dck_ref, dcq_ref):
        i = pl.program_id(1)

        @pl.when(i == 0)
        def _():
            dk_ref[...] = jnp.zeros_like(dk_ref)
            dv_ref[...] = jnp.zeros_like(dv_ref)
            dck_ref[...] = jnp.zeros_like(dck_ref)

        masks = _head_lane_masks(t)
        q2, do2, o2 = q_ref[...], do_ref[...], o_ref[...]
        qs = [jnp.where(mk, q2, 0.0).astype(BF16) for mk in masks]
        dos = [jnp.where(mk, do2, 0.0).astype(BF16) for mk in masks]
        deltas = [jnp.sum(dos[hh].astype(F32) * o2, axis=1, keepdims=True) for hh in range(2)]
        bias = [cq_ref[0, :, hh:hh + 1] - lse_ref[0, :, hh:hh + 1] for hh in range(2)]

        def block(j, carry, diagonal):
            off = pl.multiple_of(j * t, t)
            ckj = ck_ref[0, j]
            k2 = k_ref[pl.ds(off, t), :].astype(BF16)
            v2 = v_ref[pl.ds(off, t), :].astype(BF16)
            out = []
            dk2 = jnp.zeros((t, 2 * hd), F32)
            dv2 = jnp.zeros((t, 2 * hd), F32)
            for hh in range(2):
                s = _dg(qs[hh], k2, False, True) + (bias[hh] - ckj[hh:hh + 1, :])
                if diagonal:
                    keep = lax.broadcasted_iota(jnp.int32, (t, t), 0) >= lax.broadcasted_iota(jnp.int32, (t, t), 1)
                    s = jnp.where(keep, s, NEG_BIG)
                p = jnp.exp(s)
                dp = _dg(dos[hh], v2, False, True)
                ds = p * (dp - deltas[hh])
                dsb = ds.astype(BF16)
                dq, rowsum = carry[hh]
                out.append((dq + _dg(dsb, k2, False, False), rowsum + jnp.sum(ds, axis=1, keepdims=True)))
                dk2 = dk2 + _dg(dsb, qs[hh], True, False)
                dv2 = dv2 + _dg(p.astype(BF16), dos[hh], True, False)
                dck_ref[0, j, hh:hh + 1, :] -= jnp.sum(ds, axis=0, keepdims=True)
            dk_ref[pl.ds(off, t), :] += dk2
            dv_ref[pl.ds(off, t), :] += dv2
            return tuple(out)

        init = tuple((jnp.zeros((t, 2 * hd), F32), jnp.zeros((t, 1), F32)) for _ in range(2))
        res = lax.fori_loop(0, i, lambda j, c: block(j, c, False), init)
        res = block(i, res, True)
        dq_ref[...] = jnp.where(masks[0], res[0][0], res[1][0])
        for hh in range(2):
            dcq_ref[0, :, hh:hh + 1] = res[hh][1]

    blk = pl.BlockSpec((t, 2 * hd), lambda hp, i: (i, hp))
    full = pl.BlockSpec((tokens, 2 * hd), lambda hp, i: (0, hp))
    cq_spec = pl.BlockSpec((1, t, 2), lambda hp, i: (hp, i, 0))
    ck_spec = pl.BlockSpec((1, nb, 2, t), lambda hp, i: (hp, 0, 0, 0))
    tshape = jax.ShapeDtypeStruct((tokens, width), F32)
    return pl.pallas_call(
        body, name="fox_bwd", grid=(npair, nb),
        in_specs=[blk, full, full, cq_spec, ck_spec, blk, cq_spec, blk],
        out_specs=[blk, full, full, ck_spec, cq_spec],
        out_shape=[tshape, tshape, tshape, jax.ShapeDtypeStruct((npair, nb, 2, t), F32),
                   jax.ShapeDtypeStruct((npair, tokens, 2), F32)],
        compiler_params=pltpu.CompilerParams(dimension_semantics=("parallel", "arbitrary")),
    )(q, k, v, cq, ck, o, lse, do)


@jax.custom_vjp
def fox_attention(q, k, v, cum):
    return _fox_fwd(q, k, v, cum)[0]


def _fox_fwd(q, k, v, cum):
    cq, ck = _fox_layouts(cum)
    q, k, v = q.astype(BF16), k.astype(BF16), v.astype(BF16)
    o, lse = _fox_fwd_call(q, k, v, cq, ck)
    return o, (q, k, v, cq, ck, o, lse)


def _fox_bwd(res, do):
    q, k, v, cq, ck, o, lse = res
    dq, dk, dv, dck, dcq = _fox_bwd_call(q, k, v, cq, ck, o, lse, do)
    npair, nb, _, t = dck.shape
    dcum = dck.transpose(0, 2, 1, 3).reshape(2 * npair, nb * t).T + dcq.transpose(1, 0, 2).reshape(nb * t, 2 * npair)
    return dq, dk, dv, dcum


fox_attention.defvjp(_fox_fwd, _fox_bwd)


def _loss_call(y, target):
    rows, d = y.shape
    tr = _row_tile(rows, d)

    def body(y_ref, t_ref, loss_ref, dy_ref):
        @pl.when(pl.program_id(0) == 0)
        def _():
            loss_ref[...] = jnp.zeros_like(loss_ref)

        diff = y_ref[...] - t_ref[...]
        dy_ref[...] = diff * (1.0 / d)
        loss_ref[...] += (0.5 / d) * jnp.sum(jnp.sum(diff * diff, axis=1, keepdims=True), axis=0, keepdims=True)

    return pl.pallas_call(
        body, name="loss", grid=(rows // tr,),
        in_specs=[pl.BlockSpec((tr, d), lambda i: (i, 0))] * 2,
        out_specs=[pl.BlockSpec((1, 1), lambda i: (0, 0)), pl.BlockSpec((tr, d), lambda i: (i, 0))],
        out_shape=[jax.ShapeDtypeStruct((1, 1), F32), jax.ShapeDtypeStruct((rows, d), F32)],
        compiler_params=pltpu.CompilerParams(dimension_semantics=("arbitrary",)),
    )(y, target)


def _adamw_call(w, g, m, v):
    rows, cols = w.shape
    tr = _row_tile_ragged(rows, cols, budget=1024 * 1024)
    c1 = 1.0 / (1.0 - ADAM_B1 ** ADAM_STEP)
    c2 = 1.0 / (1.0 - ADAM_B2 ** ADAM_STEP)

    def body(w_ref, g_ref, m_ref, v_ref, d_ref, nm_ref, nv_ref):
        gv = g_ref[...]
        nm = ADAM_B1 * m_ref[...] + (1.0 - ADAM_B1) * gv
        nv = ADAM_B2 * v_ref[...] + (1.0 - ADAM_B2) * (gv * gv)
        nm_ref[...] = nm
        nv_ref[...] = nv
        d_ref[...] = -ADAM_LR * ((nm * c1) / (jnp.sqrt(nv * c2) + ADAM_EPS) + ADAM_WD * w_ref[...])

    spec = pl.BlockSpec((tr, cols), lambda i: (i, 0))
    shape = jax.ShapeDtypeStruct((rows, cols), F32)
    return pl.pallas_call(
        body, name="adamw", grid=(pl.cdiv(rows, tr),),
        in_specs=[spec] * 4, out_specs=[spec] * 3, out_shape=[shape] * 3,
        compiler_params=pltpu.CompilerParams(dimension_semantics=("parallel",)),
    )(w, g, m, v)


def _my_place():
    return lax.axis_index("x"), lax.axis_index("y"), lax.axis_index("c")


def _place_index(px, py, pc):
    return 4 * px + 2 * py + pc


HBM_SPEC = pl.BlockSpec(memory_space=pltpu.HBM)


def _all_gather_call(block):
    def body(x_ref, out_ref, send_sems, recv_sems, local_sem):
        x, y, c = _my_place()
        me, sibling = (x, y, c), (x, y, 1 - c)
        chips = [(1 - x, y), (x, 1 - y), (1 - x, 1 - y)]

        def slot(px, py, pc):
            return out_ref.at[_place_index(px, py, pc)]

        def copy(k, blk, to, src=None):
            return pltpu.make_async_remote_copy(
                src_ref=slot(*blk) if src is None else src, dst_ref=slot(*blk),
                send_sem=send_sems.at[k], recv_sem=recv_sems.at[k],
                device_id=to, device_id_type=pl.DeviceIdType.MESH)

        mine = pltpu.make_async_copy(x_ref, slot(*me), local_sem)
        mine.start()
        first = [copy(0, me, sibling, src=x_ref)]
        first += [copy(1 + j, me, (*chip, c), src=x_ref) for j, chip in enumerate(chips)]
        for cp in first:
            cp.start()
        passed = [copy(4 + j, (*chip, c), sibling) for j, chip in enumerate(chips)]
        for j, chip in enumerate(chips):
            copy(1 + j, (*chip, c), me).wait_recv()
            passed[j].start()
        copy(0, sibling, me).wait_recv()
        for j, chip in enumerate(chips):
            copy(4 + j, (*chip, 1 - c), me).wait_recv()
        for cp in first + passed:
            cp.wait_send()
        mine.wait()

    return pl.pallas_call(
        body, name="all_gather",
        out_shape=jax.ShapeDtypeStruct((N_DEV,) + block.shape, block.dtype),
        in_specs=[HBM_SPEC], out_specs=HBM_SPEC,
        scratch_shapes=[pltpu.SemaphoreType.DMA((7,)), pltpu.SemaphoreType.DMA((7,)), pltpu.SemaphoreType.DMA],
    )(block)


SEM_SPEC = pl.BlockSpec(memory_space=pltpu.SEMAPHORE)
SIDE_EFFECT = pltpu.SideEffectType.DATAFLOW_SIDE_EFFECTING


def _peers():
    x, y, c = _my_place()
    out = []
    for k in range(1, N_DEV):
        peer = (x ^ (k >> 2), y ^ ((k >> 1) & 1), c ^ (k & 1))
        out.append((k - 1, peer, _place_index(*peer)))
    return _place_index(x, y, c), out


def _spread_start(src, per_peer, name, after=None):
    slot = src.shape[1:] if per_peer else src.shape
    order = () if after is None else (after,)

    def body(src_ref, land_ref, *rest):
        send_sems, recv_sems, src_thru, land_thru, token = rest[len(order):]
        mine, peers = _peers()
        for k, peer, peer_idx in peers:
            pltpu.make_async_remote_copy(
                src_ref=src_ref.at[peer_idx] if per_peer else src_ref, dst_ref=land_ref.at[mine],
                send_sem=send_sems.at[k], recv_sem=recv_sems.at[k],
                device_id=peer, device_id_type=pl.DeviceIdType.MESH).start()
        token[...] = jnp.zeros_like(token)

    return pl.pallas_call(
        body, name=name,
        out_shape=(pltpu.SemaphoreType.DMA((N_DEV - 1,)), pltpu.SemaphoreType.DMA((N_DEV - 1,)),
                   pltpu.HBM(src.shape, src.dtype), pltpu.HBM((N_DEV,) + slot, src.dtype),
                   jax.ShapeDtypeStruct((8, 128), F32)),
        in_specs=(HBM_SPEC, HBM_SPEC) + (pl.BlockSpec(memory_space=pl.ANY),) * len(order),
        out_specs=(SEM_SPEC, SEM_SPEC, HBM_SPEC, HBM_SPEC, pl.BlockSpec(memory_space=pltpu.VMEM)),
        input_output_aliases={0: 2, 1: 3},
        compiler_params=pltpu.CompilerParams(has_side_effects=SIDE_EFFECT),
    )(pltpu.with_memory_space_constraint(src, pltpu.HBM),
      pltpu.with_memory_space_constraint(lax.empty((N_DEV,) + slot, src.dtype), pltpu.HBM), *order)


def _spread_wait(handles, after, per_peer, name):
    send_sems, recv_sems, src_thru, land_thru = handles

    def body(src_ref, land_ref, send_sems, recv_sems, after_ref, src_dead, got_ref):
        _, peers = _peers()
        for k, peer, peer_idx in peers:
            copy = pltpu.make_async_remote_copy(
                src_ref=src_ref.at[peer_idx] if per_peer else src_ref, dst_ref=land_ref.at[peer_idx],
                send_sem=send_sems.at[k], recv_sem=recv_sems.at[k],
                device_id=peer, device_id_type=pl.DeviceIdType.MESH)
            copy.wait_send()
            copy.wait_recv()

    return pl.pallas_call(
        body, name=name,
        out_shape=(pltpu.HBM(src_thru.shape, src_thru.dtype), pltpu.HBM(land_thru.shape, land_thru.dtype)),
        in_specs=(HBM_SPEC, HBM_SPEC, SEM_SPEC, SEM_SPEC, pl.BlockSpec(memory_space=pl.ANY)),
        out_specs=(HBM_SPEC, HBM_SPEC), input_output_aliases={0: 0, 1: 1},
        compiler_params=pltpu.CompilerParams(has_side_effects=SIDE_EFFECT),
    )(src_thru, land_thru, send_sems, recv_sems, after)


def _sum_slots_call(slots):
    _, rows, cols = slots.shape
    tr = _row_tile_ragged(rows, cols, budget=512 * 1024)

    def body(s_ref, o_ref):
        acc = s_ref[0].astype(F32)
        for j in range(1, N_DEV):
            acc = acc + s_ref[j].astype(F32)
        o_ref[...] = acc

    return pl.pallas_call(
        body, name="sum_slots", grid=(pl.cdiv(rows, tr),),
        in_specs=[pl.BlockSpec((N_DEV, tr, cols), lambda i: (0, i, 0))],
        out_specs=pl.BlockSpec((tr, cols), lambda i: (i, 0)),
        out_shape=jax.ShapeDtypeStruct((rows, cols), F32),
        compiler_params=pltpu.CompilerParams(dimension_semantics=("parallel",)),
    )(slots)


def _sum_adamw_call(got, own, w, m, v):
    rows, cols = w.shape
    tr = _row_tile_ragged(rows, cols, budget=512 * 1024)
    c1 = 1.0 / (1.0 - ADAM_B1 ** ADAM_STEP)
    c2 = 1.0 / (1.0 - ADAM_B2 ** ADAM_STEP)

    def body(got_ref, own_ref, w_ref, m_ref, v_ref, g_ref, d_ref, nm_ref, nv_ref):
        mine = _place_index(*_my_place())
        gv = jnp.zeros(w_ref.shape, F32)
        for j in range(N_DEV):
            gv = gv + jnp.where(mine == j, own_ref[...], got_ref[j]).astype(F32)
        nm = ADAM_B1 * m_ref[...] + (1.0 - ADAM_B1) * gv
        nv = ADAM_B2 * v_ref[...] + (1.0 - ADAM_B2) * (gv * gv)
        g_ref[...] = gv
        nm_ref[...] = nm
        nv_ref[...] = nv
        d_ref[...] = -ADAM_LR * ((nm * c1) / (jnp.sqrt(nv * c2) + ADAM_EPS) + ADAM_WD * w_ref[...])

    spec = pl.BlockSpec((tr, cols), lambda i: (i, 0))
    shape = jax.ShapeDtypeStruct((rows, cols), F32)
    return pl.pallas_call(
        body, name="sum_adamw", grid=(pl.cdiv(rows, tr),),
        in_specs=[pl.BlockSpec((N_DEV, tr, cols), lambda i: (0, i, 0))] + [spec] * 4,
        out_specs=[spec] * 4, out_shape=[shape] * 4,
        compiler_params=pltpu.CompilerParams(dimension_semantics=("parallel",)),
    )(got, own, w, m, v)


def _with_own_slot(got, own, mine):
    return lax.dynamic_update_index_in_dim(got, own, mine, 0)


def _pack(vectors, width):
    flat = jnp.concatenate([v.reshape(-1) for v in vectors])
    return jnp.pad(flat, (0, width - flat.shape[0])).reshape(width // 128, 128)


def _unpack(packed, like):
    flat = packed.reshape(-1)
    out, at = [], 0
    for v in like:
        out.append(flat[at:at + v.size].reshape(v.shape))
        at += v.size
    return tuple(out)


def _cols_from_slots(slots):
    n, rows, cols = slots.shape
    return slots.transpose(1, 0, 2).reshape(rows, n * cols)


def _rows_from_slots(slots):
    return slots.reshape(-1, slots.shape[2])


def _pad128(n):
    return -(-n // 128) * 128


def _pad_to_tiles(a, axis):
    n = a.shape[axis]
    pads = [(0, 0)] * a.ndim
    pads[axis] = (0, _pad128(n) - n)
    return jnp.pad(a, pads)


def _rwkv_group(take, zeros, rw, dl, al, gl):
    at = 3 * rw
    parts = take(0, at)
    for n in (dl, al, gl):
        parts += take(at, at + n)
        if _pad128(n) > n:
            parts.append(zeros(_pad128(n) - n))
        at += n
    return parts


def _in_proj_layout(slots, rw, fw, dl, al, gl, whole):
    n_slots, rows, d = slots.shape
    wt = slots.reshape(n_slots * rows, d)
    take = lambda lo, hi: [wt[lo:hi]]
    zeros = lambda n: jnp.zeros((n, d), wt.dtype)
    rcols = 3 * rw + dl + al + gl
    fcols = 3 * fw + fw // HEAD_DIM
    group_r = _rwkv_group(take, zeros, rw, dl, al, gl)
    group_f = take(rcols, rcols + fcols) + ([zeros(_pad128(fcols) - fcols)] if _pad128(fcols) > fcols else [])
    group_g = take(rcols + fcols, n_slots * rows)
    if whole:
        return jnp.concatenate(group_r + group_f + group_g, axis=0)
    return tuple(jnp.concatenate(g, axis=0) for g in (group_r, group_f, group_g))


def _low_rank_layout(slots):
    return _pad_to_tiles(_cols_from_slots(slots), 0)


def _stage_embed(meta, x, n1, lp):
    h0 = jnp.concatenate([meta, x, jnp.zeros((lp - meta.shape[0] - x.shape[0], x.shape[1]), F32)], axis=0)
    return h0, rmsnorm(h0, n1)


def _stage_mix(z_r, z_f, small, w2, a2, g2, dims):
    (mu, w0, a0, k_k, k_a, r_k, gn_w, gn_b, q_g, k_g, f_bias) = small
    rw, fw, dl, al, gl = dims
    fcols = 3 * fw + fw // HEAD_DIM

    mu_group = jnp.concatenate(_rwkv_group(lambda lo, hi: [mu[:, lo:hi]], lambda n: jnp.zeros((1, n), F32), rw, dl, al, gl), axis=1)
    r, lw, kf, v, na, b, g = rwkv_prep(z_r, mu_group, w0, a0, k_k, k_a, w2, a2, g2)
    y = wkv7(r, lw, kf, v, na, b)
    y_a = gn_bonus(y, r, kf, v, g, gn_w, gn_b, r_k.reshape(1, rw))

    fq, fk, fv, fl = z_f[:, :fw], z_f[:, fw:2 * fw], z_f[:, 2 * fw:3 * fw], z_f[:, 3 * fw:fcols]
    fq = head_rms(fq, jnp.tile(q_g, (1, fw // HEAD_DIM))) * (HEAD_DIM ** -0.5)
    fk = head_rms(fk, jnp.tile(k_g, (1, fw // HEAD_DIM)))
    cum = jnp.cumsum(jax.nn.log_sigmoid(badd(fl, f_bias)), axis=0)
    y_b = fox_attention(fq, fk, fv, cum)
    return y_a, y_b


def _stage_merge(h0, y_a, y_b, z_g, w_a, w_b, w_o):
    merged = gated_merge(z_g, dense_cols_bf16(y_a, w_a), dense_cols_bf16(y_b, w_b))
    return dense_add(merged, w_o, h0)


def _stage_ffn(h1, n2, w_gu, w_dn):
    return dense_add(swiglu(dense_cols_bf16(rmsnorm(h1, n2), w_gu)), w_dn, h1)


SHARDED = ("meta_tokens", "w_in", "rwkv_w2", "rwkv_a2", "rwkv_g2", "w_branch_a", "w_branch_b", "w_o", "w_gate_up", "w_down")
SMALL = ("norm1_g", "rwkv_mu", "rwkv_w0", "rwkv_a0", "rwkv_k_k", "rwkv_k_a", "rwkv_r_k", "rwkv_gn_w", "rwkv_gn_b",
         "fox_q_norm_g", "fox_k_norm_g", "fox_f_bias", "norm2_g")
WEIGHTS = ("meta_tokens", "norm1_g", "w_in", "rwkv_mu", "rwkv_w0", "rwkv_w2", "rwkv_a0", "rwkv_a2", "rwkv_g2", "rwkv_k_k",
           "rwkv_k_a", "rwkv_r_k", "rwkv_gn_w", "rwkv_gn_b", "fox_q_norm_g", "fox_k_norm_g", "fox_f_bias", "w_branch_a",
           "w_branch_b", "w_o", "norm2_g", "w_gate_up", "w_down")


def _as2d(a):
    return a.reshape(-1, a.shape[-1])


def kernel(x, meta_tokens, norm1_g, w_in, rwkv_mu, rwkv_w0, rwkv_w2, rwkv_a0, rwkv_a2, rwkv_g2, rwkv_k_k, rwkv_k_a, rwkv_r_k, rwkv_gn_w, rwkv_gn_b, fox_q_norm_g, fox_k_norm_g, fox_f_bias, w_branch_a, w_branch_b, w_o, norm2_g, w_gate_up, w_down, loss_target, m_meta_tokens, m_norm1_g, m_w_in, m_rwkv_mu, m_rwkv_w0, m_rwkv_w2, m_rwkv_a0, m_rwkv_a2, m_rwkv_g2, m_rwkv_k_k, m_rwkv_k_a, m_rwkv_r_k, m_rwkv_gn_w, m_rwkv_gn_b, m_fox_q_norm_g, m_fox_k_norm_g, m_fox_f_bias, m_w_branch_a, m_w_branch_b, m_w_o, m_norm2_g, m_w_gate_up, m_w_down, v_meta_tokens, v_norm1_g, v_w_in, v_rwkv_mu, v_rwkv_w0, v_rwkv_w2, v_rwkv_a0, v_rwkv_a2, v_rwkv_g2, v_rwkv_k_k, v_rwkv_k_a, v_rwkv_r_k, v_rwkv_gn_w, v_rwkv_gn_b, v_fox_q_norm_g, v_fox_k_norm_g, v_fox_f_bias, v_w_branch_a, v_w_branch_b, v_w_o, v_norm2_g, v_w_gate_up, v_w_down):
    given = dict(locals())
    w = {n: given[n] for n in WEIGHTS}
    assert rwkv_r_k.shape[-1] == HEAD_DIM
    n_meta, seq = meta_tokens.shape[0], x.shape[1]
    tokens = n_meta + seq
    lp = -(-tokens // TOKEN_TILE) * TOKEN_TILE
    mine = _place_index(*(lax.axis_index(a) for a in MESH_AXES))
    x2 = x[0]

    local = {n: _as2d(given[n]) for n in given if n != "x" and n != "loss_target"}
    for n in ("w_in", "m_w_in", "v_w_in"):
        local[n] = jnp.transpose(given[n][0])
    blocks = {n: local[n].astype(F32 if n == "meta_tokens" else BF16) for n in SHARDED}
    first = ("meta_tokens", "rwkv_w2", "rwkv_a2", "rwkv_g2")
    started = {n: _spread_start(blocks[n], False, "gather_start_" + n) for n in first}
    zero = sum(started[n][4][0, 0] for n in first)

    def gathered(n, after):
        own, got = _spread_wait(started[n][:4], after, False, "gather_wait_" + n)
        return _with_own_slot(got, own, mine)

    sm = {n: _as2d(w[n]) for n in SMALL}
    small_mix = tuple(sm[n] for n in SMALL[1:-1])
    n1 = sm["norm1_g"] + zero
    rw, fw = w_branch_a.shape[-2], w_branch_b.shape[-2]
    dims = (rw, fw, rwkv_w2.shape[-2], rwkv_a2.shape[-2], rwkv_g2.shape[-2])
    same = lambda s: (s,)

    meta, un_meta = jax.vjp(_cols_from_slots, gathered("meta_tokens", x2))
    (h0, xn), vjp_embed = jax.vjp(lambda m, xs, g: _stage_embed(m, xs, g, lp), meta, x2, n1)
    in_slots = _all_gather_call(blocks["w_in"])
    later = [n for n in SHARDED if n not in first and n != "w_in"]
    started.update({n: _spread_start(blocks[n], False, "gather_start_" + n, after=in_slots) for n in later})
    w_groups = _in_proj_layout(in_slots, *dims, whole=False)
    w_cat, un_in = jax.vjp(lambda s: _in_proj_layout(s, *dims, whole=True), in_slots)
    xn_b = xn.astype(BF16)
    behind = sum(started[n][4] for n in later)
    z_r, z_f, z_g = (_matmul(xn_b, wg, tb=True, name="in_proj_" + tag, after=behind) for wg, tag in zip(w_groups, "rfg"))
    (w2, un_w2), (a2, un_a2), (g2, un_g2) = (jax.vjp(_low_rank_layout, gathered(n, xn)) for n in ("rwkv_w2", "rwkv_a2", "rwkv_g2"))
    (y_a, y_b), vjp_mix = jax.vjp(lambda zr, zf, s, a, b, c: _stage_mix(zr, zf, s, a, b, c, dims),
                                  z_r, z_f, small_mix, w2, a2, g2)
    w_a, w_b = gathered("w_branch_a", y_a), gathered("w_branch_b", y_a)
    w_o_full, un_wo = jax.vjp(_rows_from_slots, gathered("w_o", y_a))
    h1, vjp_merge = jax.vjp(_stage_merge, h0, y_a, y_b, z_g, w_a, w_b, w_o_full)
    w_gu = gathered("w_gate_up", h1)
    w_dn, un_dn = jax.vjp(_rows_from_slots, gathered("w_down", h1))
    y, vjp_ffn = jax.vjp(_stage_ffn, h1, sm["norm2_g"], w_gu, w_dn)

    loss_part, dy_real = _loss_call(y[n_meta:tokens], loss_target[0])
    dy = jnp.pad(dy_real, ((n_meta, lp - tokens), (0, 0)))
    loss = lax.psum(loss_part[0, 0], MESH_AXES)

    sent = {}

    def send_grad(n, dmat, unlayout):
        sent[n] = _spread_start(unlayout(dmat)[0], True, "grad_start_" + n)
        return sent[n][4][0, 0]

    d_h1, d_n2, d_wgu, d_wdn = vjp_ffn(dy)
    behind = send_grad("w_gate_up", d_wgu, same) + send_grad("w_down", d_wdn, un_dn)
    d_h0, d_ya, d_yb, d_zg, d_wa, d_wb, d_wo = vjp_merge(d_h1 + behind)
    behind = send_grad("w_o", d_wo, un_wo) + send_grad("w_branch_a", d_wa, same) + send_grad("w_branch_b", d_wb, same)
    d_zr, d_zf, d_small_mix, d_w2, d_a2, d_g2 = vjp_mix((d_ya + behind, d_yb))
    dproj_b = jnp.concatenate([d_zr.astype(BF16), d_zf.astype(BF16), d_zg.astype(BF16)], axis=1)
    d_wcat = _matmul(dproj_b, xn_b, ta=True, out_dtype=BF16, name="in_proj_dw")
    send_grad("w_in", d_wcat, un_in)
    d_xn = _matmul(dproj_b, w_cat, out_dtype=BF16, name="in_proj_dx", after=sent["w_in"][4])
    send_grad("rwkv_w2", d_w2, un_w2)
    send_grad("rwkv_a2", d_a2, un_a2)
    send_grad("rwkv_g2", d_g2, un_g2)
    d_meta, g_x, d_n1 = vjp_embed((d_h0, d_xn))
    send_grad("meta_tokens", d_meta, un_meta)

    small_grads = (d_n1, *d_small_mix, d_n2)
    n_small = sum(g.size for g in small_grads)
    width = -(-n_small // 1024) * 1024
    small_sent = _spread_start(_pack(small_grads, width), False, "small_grad_start")

    grads, delta, new_m, new_v = {}, {}, {}, {}
    after = g_x
    for n in ("w_gate_up", "w_down", "w_o", "w_branch_a", "w_branch_b", "rwkv_g2", "rwkv_a2", "rwkv_w2", "meta_tokens", "w_in"):
        src, got = _spread_wait(sent[n][:4], after, True, "grad_wait_" + n)
        own = lax.dynamic_index_in_dim(src, mine, 0, keepdims=False)
        g, d_, m_, v_ = _sum_adamw_call(got, own, local[n], local["m_" + n], local["v_" + n])
        back = (lambda t: jnp.transpose(t)[None]) if n == "w_in" else (lambda t: t.reshape(w[n].shape))
        grads[n], delta[n], new_m[n], new_v[n] = (back(t) for t in (g, d_, m_, v_))
        after = m_
    own_small, got_small = _spread_wait(small_sent[:4], after, False, "small_grad_wait")
    small_total = _unpack(_sum_slots_call(_with_own_slot(got_small, own_small, mine)), small_grads)
    grads.update({n: g.reshape(w[n].shape) for n, g in zip(SMALL, small_total)})
    packs = [_pack([src[n] if p == "" else given[p + n] for n in SMALL], width)
             for p, src in (("", w), ("", grads), ("m_", None), ("v_", None))]
    like = [w[n] for n in SMALL]
    for out, packed in zip((delta, new_m, new_v), _adamw_call(*packs)):
        out.update(dict(zip(SMALL, _unpack(packed, like))))

    return (loss, g_x[None], *[grads[n] for n in WEIGHTS], *[delta[n] for n in WEIGHTS],
            *[new_m[n] for n in WEIGHTS], *[new_v[n] for n in WEIGHTS])
```

```python
import jax
import jax.numpy as jnp
from jax import lax
from jax.experimental import pallas as pl
from jax.experimental.pallas import tpu as pltpu

F32 = jnp.float32
BF16 = jnp.bfloat16

N_DEV = 8
MESH_AXES = ("x", "y", "c")
HEAD_DIM = 64
TOKEN_TILE = 128
WKV_CHUNK = 64
WKV_PAIRS_PER_STEP = 8
PAIR = 2 * HEAD_DIM
ATTN_BLOCK = 128
ATTN_BLOCK_BIG = 384
RMS_EPS = 1e-6
GN_EPS = 64e-5
L2_FLOOR = 1e-12
NEG_BIG = -1e30
ADAM_LR, ADAM_B1, ADAM_B2, ADAM_EPS, ADAM_WD, ADAM_STEP = 0.001, 0.9, 0.999, 1e-08, 0.01, 10
VMEM_LIMIT_CAP = 56 * 1024 * 1024
VMEM_LIMIT_FLOOR = 32 * 1024 * 1024
MATMUL_VMEM_BUDGET = 36 * 1024 * 1024
GRID_STEP_BYTES = 1024 * 1024
ACC_BYTES_PER_HBM_BYTE = 6


def _vmem_limit(estimate_bytes):
    return int(min(max(estimate_bytes * 5 // 4, VMEM_LIMIT_FLOOR), VMEM_LIMIT_CAP))


def _row_tile(rows, width, itemsize=4, budget=2 * 1024 * 1024):
    for c in (1408, 1024, 704, 512, 384, 256, 128, 64, 32, 16, 8):
        if rows % c == 0 and c * width * itemsize <= budget:
            return c
    return rows


def _row_tile_ragged(rows, width, itemsize=4, budget=2 * 1024 * 1024):
    tile = _row_tile(rows, width, itemsize, budget)
    if tile * width * itemsize <= budget or rows < 16:
        return tile
    padded = -(-rows // 16) * 16
    for c in (1408, 1024, 704, 512, 384, 336, 256, 192, 128, 96, 64, 48, 32, 16):
        if padded % c == 0 and c * width * itemsize <= budget:
            return c
    return tile


def _dg(a, b, ta, tb):
    dims = (((0 if ta else 1,), (1 if tb else 0,)), ((), ()))
    return lax.dot_general(a, b, dims, preferred_element_type=F32)


def _split(x, n):
    parts = []
    for _ in range(n):
        h = x.astype(BF16)
        parts.append(h)
        x = x - h.astype(F32)
    return parts


def _mm(a, b, ta=False, tb=False):
    return _dg(a.astype(BF16), b.astype(BF16), ta, tb)


def _matmul(a, b, ta=False, tb=False, out_dtype=F32, name="matmul", after=None, b_slots=False, out_slots=0, add=None):
    if ta:
        kdim, m = a.shape
    else:
        m, kdim = a.shape
    if b_slots:
        n_slots, brows, bcols = b.shape
        n, k2 = (brows, n_slots * bcols) if tb else (n_slots * bcols, brows)
    elif tb:
        n, k2 = b.shape
    else:
        k2, n = b.shape
    assert kdim == k2, (a.shape, b.shape, ta, tb)
    sa, sb, so = a.dtype.itemsize, b.dtype.itemsize, jnp.dtype(out_dtype).itemsize
    n_unit = bcols if (b_slots and not tb) else (n // out_slots if out_slots else n)
    k_unit = bcols if (b_slots and tb) else kdim
    tm, tn, tk, n_outer = _matmul_tiles(m, n, kdim, ta, sa, sb, so, n_unit, k_unit)
    nk = kdim // tk
    ij = (lambda f: lambda j, i, k: f(i, j, k)) if n_outer else (lambda f: f)

    order = () if after is None else (after,)
    extra = () if add is None else (add,)

    def body(a_ref, b_ref, *rest):
        rest = rest[len(order):]
        add_ref = rest[0] if extra else None
        o_ref, acc = rest[len(extra)], rest[len(extra) + 1:]
        part = _dg(a_ref[...].astype(BF16), b_ref[...].astype(BF16), ta, tb)
        done = lambda total: (total if add_ref is None else total + add_ref[...]).astype(o_ref.dtype)
        if nk == 1:
            o_ref[...] = done(part)
            return
        kk = pl.program_id(2)

        @pl.when(kk == 0)
        def _():
            acc[0][...] = part

        @pl.when(kk > 0)
        def _():
            acc[0][...] += part

        @pl.when(kk == nk - 1)
        def _():
            o_ref[...] = done(acc[0][...])

    a_spec = pl.BlockSpec((tk, tm), ij(lambda i, j, k: (k, i))) if ta else pl.BlockSpec((tm, tk), ij(lambda i, j, k: (i, k)))
    if b_slots and tb:
        per = bcols // tk
        b_spec = pl.BlockSpec((None, tn, tk), ij(lambda i, j, k: (k // per, j, k % per)))
    elif b_slots:
        per = bcols // tn
        b_spec = pl.BlockSpec((None, tk, tn), ij(lambda i, j, k: (j // per, k, j % per)))
    elif tb:
        b_spec = pl.BlockSpec((tn, tk), ij(lambda i, j, k: (j, k)))
    else:
        b_spec = pl.BlockSpec((tk, tn), ij(lambda i, j, k: (k, j)))
    if out_slots:
        per_out = n // out_slots // tn
        out_spec = pl.BlockSpec((None, tm, tn), ij(lambda i, j, k: (j // per_out, i, j % per_out)))
        out_shape = jax.ShapeDtypeStruct((out_slots, m, n // out_slots), out_dtype)
    else:
        out_spec = pl.BlockSpec((tm, tn), ij(lambda i, j, k: (i, j)))
        out_shape = jax.ShapeDtypeStruct((m, n), out_dtype)
    return pl.pallas_call(
        body, name=name,
        grid=(n // tn, m // tm, nk) if n_outer else (m // tm, n // tn, nk),
        in_specs=[a_spec, b_spec] + [pl.BlockSpec(memory_space=pl.ANY)] * len(order)
                 + [pl.BlockSpec((tm, tn), ij(lambda i, j, k: (i, j)))] * len(extra),
        out_specs=out_spec,
        out_shape=out_shape,
        scratch_shapes=[pltpu.VMEM((tm, tn), F32)] if nk > 1 else [],
        compiler_params=pltpu.CompilerParams(
            dimension_semantics=("parallel", "parallel", "arbitrary"),
            vmem_limit_bytes=_vmem_limit(_matmul_vmem(tm, tn, tk, nk, sa, sb, so) + 2 * tm * tn * 4 * len(extra))),
    )(a, b, *order, *extra)


def _matmul_vmem(tm, tn, tk, nk, sa, sb, so):
    return 2 * (tm * tk * sa + tk * tn * sb + tm * tn * so) + tm * tn * 4 + (tm * tn * 4 if nk > 1 else 0)


def _matmul_tiles(m, n, kdim, ta, sa, sb, so, n_unit, k_unit):
    lane = (2816, 2176, 2048, 1408, 1024, 640, 512, 384, 256, 128)
    sublane = (2816, 2176, 2048, 1408, 1024, 704, 512, 384, 256, 128)
    divs = lambda dim, cands: [c for c in cands if dim % c == 0] or [dim]
    best = None
    for tm in divs(m, lane if ta else sublane):
        for tn in divs(n_unit, lane):
            for tk in divs(k_unit, sublane if ta else lane) + ([kdim] if k_unit == kdim and (ta or kdim <= 2048) else []):
                nk, nm, nn = kdim // tk, m // tm, n // tn
                if _matmul_vmem(tm, tn, tk, nk, sa, sb, so) > MATMUL_VMEM_BUDGET:
                    continue
                acc_bytes = m * n * 4 * 3 * nk // ACC_BYTES_PER_HBM_BYTE if nk > 1 else 0
                fixed = m * n * so + acc_bytes + nm * nn * nk * GRID_STEP_BYTES
                for n_outer in (False, True):
                    if n_outer:
                        a_reads, b_reads = (1 if (nk == 1 and nm == 1) else nn), (1 if nk == 1 else nm)
                    else:
                        a_reads, b_reads = (1 if nk == 1 else nn), (1 if (nk == 1 and nn == 1) else nm)
                    cost = m * kdim * sa * a_reads + kdim * n * sb * b_reads + fixed
                    if best is None or cost < best[0]:
                        best = (cost, tm, tn, tk, n_outer)
    return best[1:]


@jax.custom_vjp
def dense(x, w):
    return _matmul(x.astype(BF16), w, name="dense_fwd")


def _dense_fwd(x, w):
    return _matmul(x.astype(BF16), w, name="dense_fwd"), (x.astype(BF16), w, jnp.zeros((), x.dtype))


def _dense_bwd(res, dy):
    xb, w, like = res
    dyb = dy.astype(BF16)
    dx = _matmul(dyb, w, tb=True, out_dtype=like.dtype, name="dense_dx")
    dw = _matmul(xb, dyb, ta=True, out_dtype=w.dtype, name="dense_dw")
    return dx, dw


dense.defvjp(_dense_fwd, _dense_bwd)


@jax.custom_vjp
def dense_add(x, w, res):
    return _matmul(x.astype(BF16), w, name="dense_add_fwd", add=res)


def _dense_add_fwd(x, w, res):
    return _matmul(x.astype(BF16), w, name="dense_add_fwd", add=res), (x.astype(BF16), w, jnp.zeros((), x.dtype))


def _dense_add_bwd(res, dy):
    return (*_dense_bwd(res, dy), dy)


dense_add.defvjp(_dense_add_fwd, _dense_add_bwd)


def _make_dense_cols(out_dtype):
    @jax.custom_vjp
    def op(x, w_slots):
        return _matmul(x.astype(BF16), w_slots, b_slots=True, out_dtype=out_dtype, name="dense_cols_fwd")

    def fwd(x, w_slots):
        xb = x.astype(BF16)
        return (_matmul(xb, w_slots, b_slots=True, out_dtype=out_dtype, name="dense_cols_fwd"),
                (xb, w_slots, jnp.zeros((), x.dtype)))

    def bwd(res, dy):
        xb, w_slots, like = res
        dyb = dy.astype(BF16)
        dx = _matmul(dyb, w_slots, tb=True, b_slots=True, out_dtype=like.dtype, name="dense_cols_dx")
        dw = _matmul(xb, dyb, ta=True, out_slots=w_slots.shape[0], out_dtype=w_slots.dtype, name="dense_cols_dw")
        return dx, dw

    op.defvjp(fwd, bwd)
    return op


dense_cols_bf16 = _make_dense_cols(BF16)


def _swiglu_call(gu, d_act=None):
    rows, two_f = gu.shape
    f = two_f // 2
    tr = _row_tile(rows, two_f, itemsize=2, budget=3 * 1024 * 1024)
    half = lambda j: pl.BlockSpec((tr, f), lambda i, j=j: (i, j))
    ops = (gu, gu) if d_act is None else (gu, gu, d_act)

    def body(*refs):
        g, u = refs[0][...].astype(F32), refs[1][...].astype(F32)
        s = 1.0 / (1.0 + jnp.exp(-g))
        if d_act is None:
            refs[2][...] = (g * s * u).astype(BF16)
        else:
            d = refs[2][...].astype(F32)
            refs[3][:, :f] = (d * u * s * (1.0 + g * (1.0 - s))).astype(BF16)
            refs[3][:, f:] = (d * g * s).astype(BF16)

    width = f if d_act is None else two_f
    return pl.pallas_call(
        body, name="swiglu_fwd" if d_act is None else "swiglu_bwd", grid=(rows // tr,),
        in_specs=[half(0), half(1)] + ([half(0)] if d_act is not None else []),
        out_specs=pl.BlockSpec((tr, width), lambda i: (i, 0)),
        out_shape=jax.ShapeDtypeStruct((rows, width), BF16),
        compiler_params=pltpu.CompilerParams(dimension_semantics=("parallel",)),
    )(*ops)


@jax.custom_vjp
def swiglu(gu):
    return _swiglu_call(gu)


swiglu.defvjp(lambda gu: (_swiglu_call(gu), gu), lambda gu, d_act: (_swiglu_call(gu, d_act),))


def _merge_call(zg, a, b, dm=None):
    rows, d = a.shape
    tr = _row_tile(rows, d, budget=1024 * 1024)
    half = lambda j: pl.BlockSpec((tr, d), lambda i, j=j: (i, j))
    tile = half(0)

    def body(*refs):
        ga = 1.0 / (1.0 + jnp.exp(-refs[0][...].astype(F32)))
        gb = 1.0 / (1.0 + jnp.exp(-refs[1][...].astype(F32)))
        av, bv = refs[2][...].astype(F32), refs[3][...].astype(F32)
        if dm is None:
            refs[4][...] = (ga * av + gb * bv).astype(BF16)
        else:
            dv = refs[4][...].astype(F32)
            dzg_ref, da_ref, db_ref = refs[5:]
            dzg_ref[:, :d] = (dv * av * ga * (1.0 - ga)).astype(dzg_ref.dtype)
            dzg_ref[:, d:] = (dv * bv * gb * (1.0 - gb)).astype(dzg_ref.dtype)
            da_ref[...] = (dv * ga).astype(BF16)
            db_ref[...] = (dv * gb).astype(BF16)

    shape_b = jax.ShapeDtypeStruct((rows, d), BF16)
    if dm is None:
        out_specs, out_shape, ops = tile, shape_b, (zg, zg, a, b)
    else:
        out_specs = [pl.BlockSpec((tr, 2 * d), lambda i: (i, 0)), tile, tile]
        out_shape = [jax.ShapeDtypeStruct((rows, 2 * d), zg.dtype), shape_b, shape_b]
        ops = (zg, zg, a, b, dm)
    return pl.pallas_call(
        body, name="merge_fwd" if dm is None else "merge_bwd", grid=(rows // tr,),
        in_specs=[half(0), half(1)] + [tile] * (len(ops) - 2),
        out_specs=out_specs, out_shape=out_shape,
        compiler_params=pltpu.CompilerParams(dimension_semantics=("parallel",)),
    )(*ops)


@jax.custom_vjp
def gated_merge(zg, a, b):
    return _merge_call(zg, a, b)


gated_merge.defvjp(lambda zg, a, b: (_merge_call(zg, a, b), (zg, a, b)),
                   lambda res, dm: tuple(_merge_call(*res, dm)))


def _rms_fwd_call(x, g):
    rows, d = x.shape
    tr = _row_tile(rows, d)

    def body(x_ref, g_ref, y_ref):
        xv = x_ref[...]
        rstd = lax.rsqrt(jnp.mean(xv * xv, axis=1, keepdims=True) + RMS_EPS)
        y_ref[...] = ((xv * rstd) * g_ref[...]).astype(BF16)

    return pl.pallas_call(
        body, name="rms_fwd", grid=(rows // tr,),
        in_specs=[pl.BlockSpec((tr, d), lambda i: (i, 0)), pl.BlockSpec((1, d), lambda i: (0, 0))],
        out_specs=pl.BlockSpec((tr, d), lambda i: (i, 0)),
        out_shape=jax.ShapeDtypeStruct((rows, d), BF16),
        compiler_params=pltpu.CompilerParams(dimension_semantics=("parallel",)),
    )(x, g)


def _rms_bwd_call(x, g, dy):
    rows, d = x.shape
    tr = _row_tile(rows, d)

    def body(x_ref, g_ref, dy_ref, dx_ref, dg_ref):
        @pl.when(pl.program_id(0) == 0)
        def _():
            dg_ref[...] = jnp.zeros_like(dg_ref)

        xv = x_ref[...]
        dyv = dy_ref[...].astype(F32)
        rstd = lax.rsqrt(jnp.mean(xv * xv, axis=1, keepdims=True) + RMS_EPS)
        xhat = xv * rstd
        dxhat = dyv * g_ref[...]
        dx_ref[...] = rstd * (dxhat - xhat * jnp.mean(dxhat * xhat, axis=1, keepdims=True))
        dg_ref[...] += jnp.sum(dyv * xhat, axis=0, keepdims=True)

    return pl.pallas_call(
        body, name="rms_bwd", grid=(rows // tr,),
        in_specs=[pl.BlockSpec((tr, d), lambda i: (i, 0)), pl.BlockSpec((1, d), lambda i: (0, 0)),
                  pl.BlockSpec((tr, d), lambda i: (i, 0))],
        out_specs=[pl.BlockSpec((tr, d), lambda i: (i, 0)), pl.BlockSpec((1, d), lambda i: (0, 0))],
        out_shape=[jax.ShapeDtypeStruct((rows, d), F32), jax.ShapeDtypeStruct((1, d), F32)],
        compiler_params=pltpu.CompilerParams(dimension_semantics=("arbitrary",)),
    )(x, g, dy)


@jax.custom_vjp
def rmsnorm(x, g):
    return _rms_fwd_call(x, g)


rmsnorm.defvjp(lambda x, g: (_rms_fwd_call(x, g), (x, g)), lambda res, dy: tuple(_rms_bwd_call(res[0], res[1], dy)))


def _bcast_add_call(x, p):
    rows, d = x.shape
    tr = _row_tile(rows, d)

    def body(x_ref, p_ref, y_ref):
        y_ref[...] = x_ref[...] + p_ref[...]

    return pl.pallas_call(
        body, name="bcast_add", grid=(rows // tr,),
        in_specs=[pl.BlockSpec((tr, d), lambda i: (i, 0)), pl.BlockSpec((1, d), lambda i: (0, 0))],
        out_specs=pl.BlockSpec((tr, d), lambda i: (i, 0)),
        out_shape=jax.ShapeDtypeStruct((rows, d), F32),
        compiler_params=pltpu.CompilerParams(dimension_semantics=("parallel",)),
    )(x, p)


def _colsum_call(a):
    rows, d = a.shape
    tr = _row_tile(rows, d)

    def body(a_ref, o_ref):
        @pl.when(pl.program_id(0) == 0)
        def _():
            o_ref[...] = jnp.zeros_like(o_ref)

        o_ref[...] += jnp.sum(a_ref[...], axis=0, keepdims=True)

    return pl.pallas_call(
        body, name="colsum", grid=(rows // tr,),
        in_specs=[pl.BlockSpec((tr, d), lambda i: (i, 0))],
        out_specs=pl.BlockSpec((1, d), lambda i: (0, 0)),
        out_shape=jax.ShapeDtypeStruct((1, d), F32),
        compiler_params=pltpu.CompilerParams(dimension_semantics=("arbitrary",)),
    )(a)


@jax.custom_vjp
def badd(x, p):
    return _bcast_add_call(x, p)


badd.defvjp(lambda x, p: (_bcast_add_call(x, p), None), lambda res, dy: (dy, _colsum_call(dy)))


def _head_sums(x):
    i = lax.broadcasted_iota(jnp.int32, (PAIR, PAIR), 0) // HEAD_DIM
    j = lax.broadcasted_iota(jnp.int32, (PAIR, PAIR), 1) // HEAD_DIM
    ones = jnp.where(i == j, 1.0, 0.0).astype(BF16)
    hi, lo = _split(x, 2)
    cols = [slice(p * PAIR, (p + 1) * PAIR) for p in range(x.shape[1] // PAIR)]
    return jnp.concatenate([_dg(hi[:, c], ones, False, False) + _dg(lo[:, c], ones, False, False) for c in cols], axis=1)


def _head_rms_fwd_call(x, g):
    rows, w = x.shape
    tr = _row_tile(rows, w, budget=1024 * 1024)

    def body(x_ref, g_ref, y_ref):
        xv = x_ref[...]
        rstd = lax.rsqrt(_head_sums(xv * xv) * (1.0 / HEAD_DIM) + RMS_EPS)
        y_ref[...] = (xv * rstd) * g_ref[...]

    return pl.pallas_call(
        body, name="head_rms_fwd", grid=(rows // tr,),
        in_specs=[pl.BlockSpec((tr, w), lambda i: (i, 0)), pl.BlockSpec((1, w), lambda i: (0, 0))],
        out_specs=pl.BlockSpec((tr, w), lambda i: (i, 0)),
        out_shape=jax.ShapeDtypeStruct((rows, w), F32),
        compiler_params=pltpu.CompilerParams(dimension_semantics=("parallel",)),
    )(x, g)


def _head_rms_bwd_call(x, g, dy):
    rows, w = x.shape
    tr = _row_tile(rows, w, budget=1024 * 1024)

    def body(x_ref, g_ref, dy_ref, dx_ref, dg_ref):
        @pl.when(pl.program_id(0) == 0)
        def _():
            dg_ref[...] = jnp.zeros_like(dg_ref)

        xv, dyv = x_ref[...], dy_ref[...]
        rstd = lax.rsqrt(_head_sums(xv * xv) * (1.0 / HEAD_DIM) + RMS_EPS)
        xhat = xv * rstd
        dxhat = dyv * g_ref[...]
        dx_ref[...] = rstd * (dxhat - xhat * (_head_sums(dxhat * xhat) * (1.0 / HEAD_DIM)))
        dg_ref[...] += jnp.sum(dyv * xhat, axis=0, keepdims=True)

    return pl.pallas_call(
        body, name="head_rms_bwd", grid=(rows // tr,),
        in_specs=[pl.BlockSpec((tr, w), lambda i: (i, 0)), pl.BlockSpec((1, w), lambda i: (0, 0)),
                  pl.BlockSpec((tr, w), lambda i: (i, 0))],
        out_specs=[pl.BlockSpec((tr, w), lambda i: (i, 0)), pl.BlockSpec((1, w), lambda i: (0, 0))],
        out_shape=[jax.ShapeDtypeStruct((rows, w), F32), jax.ShapeDtypeStruct((1, w), F32)],
        compiler_params=pltpu.CompilerParams(dimension_semantics=("arbitrary",)),
    )(x, g, dy)


@jax.custom_vjp
def head_rms(x, g):
    return _head_rms_fwd_call(x, g)


head_rms.defvjp(lambda x, g: (_head_rms_fwd_call(x, g), (x, g)),
                lambda res, dy: tuple(_head_rms_bwd_call(res[0], res[1], dy)))


def _gn_fwd_call(y, r, kf, v, g, gw, gb, rk):
    rows, w = y.shape
    tr = _row_tile(rows, w, budget=512 * 1024)

    def body(y_ref, r_ref, kf_ref, v_ref, g_ref, gw_ref, gb_ref, rk_ref, o_ref):
        yv = y_ref[...]
        yc = yv - _head_sums(yv) * (1.0 / HEAD_DIM)
        rstd = lax.rsqrt(_head_sums(yc * yc) * (1.0 / HEAD_DIM) + GN_EPS)
        s = _head_sums(r_ref[...] * kf_ref[...] * rk_ref[...])
        o_ref[...] = ((yc * rstd) * gw_ref[...] + gb_ref[...] + s * v_ref[...]) * g_ref[...]

    tok = pl.BlockSpec((tr, w), lambda i: (i, 0))
    par = pl.BlockSpec((1, w), lambda i: (0, 0))
    return pl.pallas_call(
        body, name="gn_bonus_fwd", grid=(rows // tr,),
        in_specs=[tok] * 5 + [par] * 3, out_specs=tok,
        out_shape=jax.ShapeDtypeStruct((rows, w), F32),
        compiler_params=pltpu.CompilerParams(dimension_semantics=("parallel",)),
    )(y, r, kf, v, g, gw, gb, rk)


def _gn_bwd_call(y, r, kf, v, g, gw, gb, rk, do):
    rows, w = y.shape
    tr = _row_tile(rows, w, budget=512 * 1024)

    def body(y_ref, r_ref, kf_ref, v_ref, g_ref, gw_ref, gb_ref, rk_ref, do_ref,
             dy_ref, dr_ref, dkf_ref, dv_ref, dg_ref, dgw_ref, dgb_ref, drk_ref):
        @pl.when(pl.program_id(0) == 0)
        def _():
            dgw_ref[...] = jnp.zeros_like(dgw_ref)
            dgb_ref[...] = jnp.zeros_like(dgb_ref)
            drk_ref[...] = jnp.zeros_like(drk_ref)

        yv, rv, kv, vv, rkv = y_ref[...], r_ref[...], kf_ref[...], v_ref[...], rk_ref[...]
        mean = lambda t: _head_sums(t) * (1.0 / HEAD_DIM)
        yc = yv - mean(yv)
        rstd = lax.rsqrt(mean(yc * yc) + GN_EPS)
        yhat = yc * rstd
        s = _head_sums(rv * kv * rkv)
        dg_ref[...] = do_ref[...] * (yhat * gw_ref[...] + gb_ref[...] + s * vv)
        dov = do_ref[...] * g_ref[...]
        dyhat = dov * gw_ref[...]
        dy_ref[...] = rstd * (dyhat - mean(dyhat) - yhat * mean(dyhat * yhat))
        ds = _head_sums(dov * vv)
        dv_ref[...] = s * dov
        dr_ref[...] = ds * kv * rkv
        dkf_ref[...] = ds * rv * rkv
        dgw_ref[...] += jnp.sum(dov * yhat, axis=0, keepdims=True)
        dgb_ref[...] += jnp.sum(dov, axis=0, keepdims=True)
        drk_ref[...] += jnp.sum(ds * rv * kv, axis=0, keepdims=True)

    tok = pl.BlockSpec((tr, w), lambda i: (i, 0))
    par = pl.BlockSpec((1, w), lambda i: (0, 0))
    tshape = jax.ShapeDtypeStruct((rows, w), F32)
    pshape = jax.ShapeDtypeStruct((1, w), F32)
    return pl.pallas_call(
        body, name="gn_bonus_bwd", grid=(rows // tr,),
        in_specs=[tok] * 5 + [par] * 3 + [tok], out_specs=[tok] * 5 + [par] * 3,
        out_shape=[tshape] * 5 + [pshape] * 3,
        compiler_params=pltpu.CompilerParams(dimension_semantics=("arbitrary",)),
    )(y, r, kf, v, g, gw, gb, rk, do)


@jax.custom_vjp
def gn_bonus(y, r, kf, v, g, gw, gb, rk):
    return _gn_fwd_call(y, r, kf, v, g, gw, gb, rk)


def _gn_bwd(res, do):
    return tuple(_gn_bwd_call(*res, do))


gn_bonus.defvjp(lambda *a: (_gn_fwd_call(*a), a), _gn_bwd)


PREP_ROWS = 128


def _prep_segments(rw, lora_w, lora_a, lora_g):
    at = 3 * rw
    seg = {"r": (0, rw), "k": (rw, 2 * rw), "v": (2 * rw, 3 * rw)}
    for name, n in (("wd", lora_w), ("ad", lora_a), ("gd", lora_g)):
        seg[name] = (at, at + _pad128(n))
        at += _pad128(n)
    return seg, at


def _prep_shifted(z_ref, zlast_ref, mu_ref, seg, first_tile):
    lo, hi = seg
    zr = z_ref[:, lo:hi]
    rows = zr.shape[0]
    before = jnp.where(first_tile, 0.0, zlast_ref[7:8, lo:hi])
    row0 = lax.broadcasted_iota(jnp.int32, zr.shape, 0) == 0
    diff = jnp.where(row0, before, pltpu.roll(zr, 1, axis=0)) - zr
    return zr + diff * mu_ref[:, lo:hi], diff


def _prep_forward_values(z_ref, zlast_ref, mu_ref, w0_ref, a0_ref, kk_ref, ka_ref, w2_ref, a2_ref, g2_ref, segs, first_tile):
    z = {n: _prep_shifted(z_ref, zlast_ref, mu_ref, segs[n], first_tile) for n in segs}
    r, k, v, wd, ad, gd = (z[n][0] for n in ("r", "k", "v", "wd", "ad", "gd"))
    twd = jnp.tanh(wd)
    pw = _mm(twd, w2_ref[...]) + w0_ref[...]
    lw = -jnp.exp(-(jnp.maximum(-pw, 0.0) + jnp.log(1.0 + jnp.exp(-jnp.abs(pw)))) - 0.5)
    a_sig = 1.0 / (1.0 + jnp.exp(-(_mm(ad, a2_ref[...]) + a0_ref[...])))
    sg = 1.0 / (1.0 + jnp.exp(-gd))
    kx = k * kk_ref[...]
    nrm = jnp.sqrt(_head_sums(kx * kx))
    inv = 1.0 / jnp.maximum(nrm, L2_FLOOR)
    return dict(z=z, r=r, k=k, v=v, twd=twd, pw=pw, lw=lw, a_sig=a_sig, sg=sg, ad=ad, kk=kx * inv, inv=inv, live=nrm > L2_FLOOR)


def _prep_specs(tokens, rpad, rw, w2, a2, g2):
    tr = PREP_ROWS
    tile = lambda w: pl.BlockSpec((tr, w), lambda i: (i, 0))
    before = pl.BlockSpec((8, rpad), lambda i: (jnp.maximum(i * (tr // 8) - 1, 0), 0))
    whole = lambda a: pl.BlockSpec(a.shape, lambda i: (0, 0))
    par = pl.BlockSpec((1, rw), lambda i: (0, 0))
    return tile, before, whole, par, pl.BlockSpec((1, rpad), lambda i: (0, 0))


def _prep_fwd_call(zr, mu, w0, a0, k_k, k_a, w2, a2, g2):
    tokens, rpad = zr.shape
    rw = w0.shape[1]
    segs, _ = _prep_segments(rw, w2.shape[0], a2.shape[0], g2.shape[0])
    tile, before, whole, par, mu_spec = _prep_specs(tokens, rpad, rw, w2, a2, g2)

    def body(z_ref, zlast_ref, mu_ref, w0_ref, a0_ref, kk_ref, ka_ref, w2_ref, a2_ref, g2_ref,
             r_ref, lw_ref, kf_ref, v_ref, na_ref, b_ref, g_ref):
        f = _prep_forward_values(z_ref, zlast_ref, mu_ref, w0_ref, a0_ref, kk_ref, ka_ref, w2_ref, a2_ref, g2_ref,
                                 segs, pl.program_id(0) == 0)
        r_ref[...] = f["r"]
        v_ref[...] = f["v"]
        lw_ref[...] = f["lw"]
        kf_ref[...] = f["k"] * (1.0 + (f["a_sig"] - 1.0) * ka_ref[...])
        na_ref[...] = -f["kk"]
        b_ref[...] = f["kk"] * f["a_sig"]
        g_ref[...] = _mm(f["sg"], g2_ref[...])

    shape = jax.ShapeDtypeStruct((tokens, rw), F32)
    return pl.pallas_call(
        body, name="rwkv_prep_fwd", grid=(tokens // PREP_ROWS,),
        in_specs=[tile(rpad), before, mu_spec, par, par, par, par, whole(w2), whole(a2), whole(g2)],
        out_specs=[tile(rw)] * 7, out_shape=[shape] * 7,
        compiler_params=pltpu.CompilerParams(dimension_semantics=("parallel",), vmem_limit_bytes=VMEM_LIMIT_CAP),
    )(zr, zr, mu, w0, a0, k_k, k_a, w2, a2, g2)


def _prep_bwd_call(zr, mu, w0, a0, k_k, k_a, w2, a2, g2, cts):
    tokens, rpad = zr.shape
    rw = w0.shape[1]
    segs, _ = _prep_segments(rw, w2.shape[0], a2.shape[0], g2.shape[0])
    tile, before, whole, par, mu_spec = _prep_specs(tokens, rpad, rw, w2, a2, g2)
    nt = tokens // PREP_ROWS
    rev = lambda spec: pl.BlockSpec(spec.block_shape, lambda i, f=spec.index_map: f(nt - 1 - i))

    def body(z_ref, zlast_ref, mu_ref, w0_ref, a0_ref, kk_ref, ka_ref, w2_ref, a2_ref, g2_ref,
             dr_ref, dlw_ref, dkf_ref, dv_ref, dna_ref, db_ref, dg_ref,
             dz_ref, dmu_ref, dw0_ref, da0_ref, dkk_ref, dka_ref, dw2_ref, da2_ref, dg2_ref, carry):
        step = pl.program_id(0)

        @pl.when(step == 0)
        def _():
            for ref in (dmu_ref, dw0_ref, da0_ref, dkk_ref, dka_ref, dw2_ref, da2_ref, dg2_ref, carry):
                ref[...] = jnp.zeros_like(ref)

        f = _prep_forward_values(z_ref, zlast_ref, mu_ref, w0_ref, a0_ref, kk_ref, ka_ref, w2_ref, a2_ref, g2_ref,
                                 segs, step == nt - 1)
        k, kk, a_sig, sg, twd = f["k"], f["kk"], f["a_sig"], f["sg"], f["twd"]
        colsum = lambda t: jnp.sum(t, axis=0, keepdims=True)
        dkf, db, dg = dkf_ref[...], db_ref[...], dg_ref[...]
        ka = ka_ref[...]
        dgd = _mm(dg, g2_ref[...], tb=True) * sg * (1.0 - sg)
        dg2_ref[...] += _mm(sg, dg, ta=True)
        dkk = db * a_sig - dna_ref[...]
        da_sig = db * kk + dkf * k * ka
        dk = dkf * (1.0 + (a_sig - 1.0) * ka)
        dka_ref[...] += colsum(dkf * k * (a_sig - 1.0))
        along = jnp.where(f["live"], _head_sums(dkk * kk), 0.0)
        dkx = (dkk - kk * along) * f["inv"]
        dk = dk + dkx * kk_ref[...]
        dkk_ref[...] += colsum(dkx * k)
        dpa = da_sig * a_sig * (1.0 - a_sig)
        da0_ref[...] += colsum(dpa)
        dad = _mm(dpa, a2_ref[...], tb=True)
        da2_ref[...] += _mm(f["ad"], dpa, ta=True)
        dpw = dlw_ref[...] * f["lw"] / (1.0 + jnp.exp(f["pw"]))
        dw0_ref[...] += colsum(dpw)
        dwd = _mm(dpw, w2_ref[...], tb=True) * (1.0 - twd * twd)
        dw2_ref[...] += _mm(twd, dpw, ta=True)
        rows = PREP_ROWS
        last = lax.broadcasted_iota(jnp.int32, (rows, 1), 0) == rows - 1
        for name, dz in (("r", dr_ref[...]), ("k", dk), ("v", dv_ref[...]), ("wd", dwd), ("ad", dad), ("gd", dgd)):
            lo, hi = segs[name]
            mu_s = mu_ref[:, lo:hi]
            dmu_ref[:, lo:hi] += colsum(dz * f["z"][name][1])
            later = dz * mu_s
            dz_ref[:, lo:hi] = dz * (1.0 - mu_s) + jnp.where(last, carry[:, lo:hi], pltpu.roll(later, rows - 1, axis=0))
            carry[:, lo:hi] = later[0:1, :]

    tok = jax.ShapeDtypeStruct((tokens, rw), F32)
    acc = lambda a: jax.ShapeDtypeStruct(a.shape, F32)
    return pl.pallas_call(
        body, name="rwkv_prep_bwd", grid=(nt,),
        in_specs=[rev(tile(rpad)), rev(before), mu_spec, par, par, par, par, whole(w2), whole(a2), whole(g2)]
                 + [rev(tile(rw))] * 7,
        out_specs=[rev(tile(rpad)), mu_spec, par, par, par, par, whole(w2), whole(a2), whole(g2)],
        out_shape=[jax.ShapeDtypeStruct((tokens, rpad), F32), acc(mu), acc(w0), acc(a0), acc(k_k), acc(k_a), acc(w2), acc(a2), acc(g2)],
        scratch_shapes=[pltpu.VMEM((1, rpad), F32)],
        compiler_params=pltpu.CompilerParams(dimension_semantics=("arbitrary",), vmem_limit_bytes=VMEM_LIMIT_CAP),
    )(zr, zr, mu, w0, a0, k_k, k_a, w2, a2, g2, *cts)


@jax.custom_vjp
def rwkv_prep(zr, mu, w0, a0, k_k, k_a, w2, a2, g2):
    return tuple(_prep_fwd_call(zr, mu, w0, a0, k_k, k_a, w2, a2, g2))


def _rwkv_prep_bwd(res, cts):
    zr, mu, w0, a0, k_k, k_a, w2, a2, g2 = res
    dz, dmu, dw0, da0, dkk, dka, dw2, da2, dg2 = _prep_bwd_call(*res, cts)
    return dz, dmu, dw0, da0, dkk, dka, dw2.astype(w2.dtype), da2.astype(a2.dtype), dg2.astype(g2.dtype)


rwkv_prep.defvjp(lambda *a: (tuple(_prep_fwd_call(*a)), a), _rwkv_prep_bwd)


def _pair_masks(rows):
    lane = lax.broadcasted_iota(jnp.int32, (rows, PAIR), 1)
    return lane < HEAD_DIM, lane >= HEAD_DIM


def _bd(x):
    m0, m1 = _pair_masks(x.shape[0])
    return jnp.concatenate([jnp.where(m0, x, 0.0), jnp.where(m1, x, 0.0)], axis=0)


def _unbd(m, c):
    return jnp.where(_pair_masks(c)[0], m[:c], m[c:])


def _pair_a(l2, r2):
    return _mm(l2, _bd(r2), tb=True)


def _pair_mul(p2, x2):
    return _mm(p2, _bd(x2))


def _pair_mul_t(p2, x2):
    return _unbd(_mm(p2, x2, ta=True), p2.shape[0])


def _block_diag_mask():
    row = lax.broadcasted_iota(jnp.int32, (PAIR, PAIR), 0)
    lane = lax.broadcasted_iota(jnp.int32, (PAIR, PAIR), 1)
    return (row < HEAD_DIM) == (lane < HEAD_DIM), row == lane


def _wkv_pair_common(r, lw, k, a, b):
    c = r[0].shape[0]
    pairs = range(len(r))
    i = lax.broadcasted_iota(jnp.int32, (c, PAIR), 0)
    j = lax.broadcasted_iota(jnp.int32, (c, PAIR), 1) % c
    strict, incl = i > j, i >= j
    ti = lax.broadcasted_iota(jnp.int32, (c, c), 0)
    tj = lax.broadcasted_iota(jnp.int32, (c, c), 1)
    tri = jnp.where(ti >= tj, 1.0, 0.0).astype(BF16)
    lc = [sum(_dg(tri, part, False, False) for part in _split(lw[p], 3)) for p in pairs]
    lend = [lc[p][c - 1:c, :] for p in pairs]
    rt = [r[p] * jnp.exp(lc[p]) for p in pairs]
    at = [a[p] * jnp.exp(lc[p] - lw[p]) for p in pairs]
    pinv = [jnp.exp(-lc[p]) for p in pairs]
    kt = [k[p] * pinv[p] for p in pairs]
    bt = [b[p] * pinv[p] for p in pairs]
    e = [jnp.exp(lend[p] - lc[p]) for p in pairs]
    ktp = [k[p] * e[p] for p in pairs]
    btp = [b[p] * e[p] for p in pairs]
    a_ab = [jnp.where(strict, _pair_a(at[p], bt[p]), 0.0) for p in pairs]
    a_ak = [jnp.where(strict, _pair_a(at[p], kt[p]), 0.0) for p in pairs]
    a_rb = [jnp.where(incl, _pair_a(rt[p], bt[p]), 0.0) for p in pairs]
    a_rk = [jnp.where(incl, _pair_a(rt[p], kt[p]), 0.0) for p in pairs]
    t = [jnp.where(i == j, 1.0, 0.0) + a_ab[p] for p in pairs]
    xp = a_ab
    n = 2
    while n < c:
        xp = [_pair_mul(xp[p], xp[p]) for p in pairs]
        t = [t[p] + _pair_mul(t[p], xp[p]) for p in pairs]
        n *= 2
    bdm, eye = _block_diag_mask()
    pend_col = [jnp.sum(jnp.where(eye, jnp.exp(lend[p]), 0.0), axis=1, keepdims=True) for p in pairs]
    return dict(rt=rt, at=at, kt=kt, bt=bt, ktp=ktp, btp=btp, a_ak=a_ak, a_rb=a_rb, a_rk=a_rk, t=t,
                pend_col=pend_col, lend=lend, lc=lc, strict=strict, incl=incl, tri=tri, bdm=bdm)


def _wkv_group(width):
    npair = width // PAIR
    g = min(WKV_PAIRS_PER_STEP, npair)
    assert npair % g == 0
    return npair, g


def _wkv_fwd_call(r, lw, k, v, a, b):
    tokens, width = r.shape
    c = WKV_CHUNK
    nc = tokens // c
    npair, g = _wkv_group(width)

    def body(r_ref, lw_ref, k_ref, v_ref, a_ref, b_ref, y_ref, s_ref, st):
        @pl.when(pl.program_id(1) == 0)
        def _():
            st[...] = jnp.zeros_like(st)

        pairs = range(g)
        rv, lwv, kv, vv, av, bv = ([ref[:, p * PAIR:(p + 1) * PAIR] for p in pairs]
                                   for ref in (r_ref, lw_ref, k_ref, v_ref, a_ref, b_ref))
        s0 = [st[p] for p in pairs]
        q = _wkv_pair_common(rv, lwv, kv, av, bv)
        w1 = [_mm(q["at"][p], s0[p]) + _pair_mul(q["a_ak"][p], vv[p]) for p in pairs]
        u = [_pair_mul(q["t"][p], w1[p]) for p in pairs]
        y = [_mm(q["rt"][p], s0[p]) + _pair_mul(q["a_rb"][p], u[p]) + _pair_mul(q["a_rk"][p], vv[p]) for p in pairs]
        grow = [_mm(jnp.concatenate([q["btp"][p], q["ktp"][p]], axis=0), jnp.concatenate([u[p], vv[p]], axis=0), ta=True)
                for p in pairs]
        for p in pairs:
            y_ref[:, p * PAIR:(p + 1) * PAIR] = y[p]
            s_ref[0, p] = s0[p]
            st[p] = q["pend_col"][p] * s0[p] + jnp.where(q["bdm"], grow[p], 0.0)

    tok = pl.BlockSpec((c, g * PAIR), lambda gi, ci: (ci, gi))
    return pl.pallas_call(
        body, name="wkv_fwd", grid=(npair // g, nc),
        in_specs=[tok] * 6,
        out_specs=[tok, pl.BlockSpec((1, g, PAIR, PAIR), lambda gi, ci: (ci, gi, 0, 0))],
        out_shape=[jax.ShapeDtypeStruct((tokens, width), F32), jax.ShapeDtypeStruct((nc, npair, PAIR, PAIR), F32)],
        scratch_shapes=[pltpu.VMEM((g, PAIR, PAIR), F32)],
        compiler_params=pltpu.CompilerParams(dimension_semantics=("parallel", "arbitrary")),
    )(r, lw, k, v, a, b)


def _wkv_bwd_call(r, lw, k, v, a, b, s, dy):
    tokens, width = r.shape
    c = WKV_CHUNK
    nc = tokens // c
    npair, g = _wkv_group(width)

    def body(r_ref, lw_ref, k_ref, v_ref, a_ref, b_ref, s_ref, dy_ref,
             dr_ref, dlw_ref, dk_ref, dv_ref, da_ref, db_ref, dst):
        @pl.when(pl.program_id(1) == 0)
        def _():
            dst[...] = jnp.zeros_like(dst)

        pairs = range(g)
        rv, lwv, kv, vv, av, bv, dyv = ([ref[:, p * PAIR:(p + 1) * PAIR] for p in pairs]
                                        for ref in (r_ref, lw_ref, k_ref, v_ref, a_ref, b_ref, dy_ref))
        s0 = [s_ref[0, p] for p in pairs]
        dsc = [dst[p] for p in pairs]
        q = _wkv_pair_common(rv, lwv, kv, av, bv)
        rt, at, kt, bt, ktp, btp, t = (q[n] for n in ("rt", "at", "kt", "bt", "ktp", "btp", "t"))
        a_ak, a_rb, a_rk, strict, incl = (q[n] for n in ("a_ak", "a_rb", "a_rk", "strict", "incl"))
        w1 = [_mm(at[p], s0[p]) + _pair_mul(a_ak[p], vv[p]) for p in pairs]
        u = [_pair_mul(t[p], w1[p]) for p in pairs]
        du = [_pair_mul_t(a_rb[p], dyv[p]) + _mm(btp[p], dsc[p]) for p in pairs]
        dw1 = [_pair_mul_t(t[p], du[p]) for p in pairs]
        dv = [_pair_mul_t(a_rk[p], dyv[p]) + _mm(ktp[p], dsc[p]) + _pair_mul_t(a_ak[p], dw1[p]) for p in pairs]
        da_ab = [jnp.where(strict, _pair_a(dw1[p], u[p]), 0.0) for p in pairs]
        da_ak = [jnp.where(strict, _pair_a(dw1[p], vv[p]), 0.0) for p in pairs]
        da_rb = [jnp.where(incl, _pair_a(dyv[p], u[p]), 0.0) for p in pairs]
        da_rk = [jnp.where(incl, _pair_a(dyv[p], vv[p]), 0.0) for p in pairs]
        d_rt = [_mm(dyv[p], s0[p], tb=True) + _pair_mul(da_rb[p], bt[p]) + _pair_mul(da_rk[p], kt[p]) for p in pairs]
        d_at = [_mm(dw1[p], s0[p], tb=True) + _pair_mul(da_ab[p], bt[p]) + _pair_mul(da_ak[p], kt[p]) for p in pairs]
        d_bt = [_pair_mul_t(da_ab[p], at[p]) + _pair_mul_t(da_rb[p], rt[p]) for p in pairs]
        d_kt = [_pair_mul_t(da_ak[p], at[p]) + _pair_mul_t(da_rk[p], rt[p]) for p in pairs]
        d_btp = [_mm(u[p], dsc[p], tb=True) for p in pairs]
        d_ktp = [_mm(vv[p], dsc[p], tb=True) for p in pairs]
        ones = jnp.ones((8, PAIR), BF16)
        dpend = [sum(_dg(ones, part, False, True) for part in _split(dsc[p] * s0[p], 3))[0:1, :] * jnp.exp(q["lend"][p])
                 for p in pairs]
        grow = [_mm(jnp.concatenate([rt[p], at[p]], axis=0), jnp.concatenate([dyv[p], dw1[p]], axis=0), ta=True)
                for p in pairs]
        last = lax.broadcasted_iota(jnp.int32, (c, PAIR), 0) == c - 1
        for p in pairs:
            sl = slice(p * PAIR, (p + 1) * PAIR)
            dst[p] = q["pend_col"][p] * dsc[p] + jnp.where(q["bdm"], grow[p], 0.0)
            lc_e = d_ktp[p] * ktp[p] + d_btp[p] * btp[p]
            dlend = jnp.sum(lc_e, axis=0, keepdims=True) + dpend[p]
            dlc = d_rt[p] * rt[p] - d_kt[p] * kt[p] - d_bt[p] * bt[p] - lc_e + jnp.where(last, dlend, 0.0)
            dlp = d_at[p] * at[p]
            dlw_ref[:, sl] = sum(_dg(q["tri"], part, True, False) for part in _split(dlc + dlp, 3)) - dlp
            lc = q["lc"][p]
            pinv = jnp.exp(-lc)
            e = jnp.exp(q["lend"][p] - lc)
            dr_ref[:, sl] = d_rt[p] * jnp.exp(lc)
            da_ref[:, sl] = d_at[p] * jnp.exp(lc - lwv[p])
            dk_ref[:, sl] = d_kt[p] * pinv + d_ktp[p] * e
            db_ref[:, sl] = d_bt[p] * pinv + d_btp[p] * e
            dv_ref[:, sl] = dv[p]

    tok = pl.BlockSpec((c, g * PAIR), lambda gi, ci: (nc - 1 - ci, gi))
    tshape = jax.ShapeDtypeStruct((tokens, width), F32)
    return pl.pallas_call(
        body, name="wkv_bwd", grid=(npair // g, nc),
        in_specs=[tok] * 6 + [pl.BlockSpec((1, g, PAIR, PAIR), lambda gi, ci: (nc - 1 - ci, gi, 0, 0)), tok],
        out_specs=[tok] * 6, out_shape=[tshape] * 6,
        scratch_shapes=[pltpu.VMEM((g, PAIR, PAIR), F32)],
        compiler_params=pltpu.CompilerParams(dimension_semantics=("parallel", "arbitrary")),
    )(r, lw, k, v, a, b, s, dy)


@jax.custom_vjp
def wkv7(r, lw, k, v, a, b):
    return _wkv_fwd_call(r, lw, k, v, a, b)[0]


def _wkv7_fwd(r, lw, k, v, a, b):
    y, s = _wkv_fwd_call(r, lw, k, v, a, b)
    return y, (r, lw, k, v, a, b, s)


wkv7.defvjp(_wkv7_fwd, lambda res, dy: tuple(_wkv_bwd_call(*res, dy)))


def _attn_block(tokens):
    return ATTN_BLOCK_BIG if tokens % ATTN_BLOCK_BIG == 0 else ATTN_BLOCK


def _fox_layouts(cum):
    tokens, heads = cum.shape
    t = _attn_block(tokens)
    cq = cum.reshape(tokens, heads // 2, 2).transpose(1, 0, 2)
    ck = cum.T.reshape(heads // 2, 2, tokens // t, t).transpose(0, 2, 1, 3)
    return cq, ck


def _head_lane_masks(rows):
    lane = lax.broadcasted_iota(jnp.int32, (rows, 2 * HEAD_DIM), 1)
    return [lane < HEAD_DIM, lane >= HEAD_DIM]


def _fox_fwd_call(q, k, v, cq, ck):
    tokens, width = q.shape
    t = _attn_block(tokens)
    nb = tokens // t
    hd = HEAD_DIM
    npair = width // (2 * hd)

    def body(q_ref, k_ref, v_ref, cq_ref, ck_ref, o_ref, lse_ref):
        i = pl.program_id(1)
        masks = _head_lane_masks(t)
        q2 = q_ref[...]
        qs = [jnp.where(mk, q2, 0.0).astype(BF16) for mk in masks]
        cqs = [cq_ref[0, :, hh:hh + 1] for hh in range(2)]

        def block(j, carry, diagonal):
            off = pl.multiple_of(j * t, t)
            ckj = ck_ref[0, j]
            k2 = k_ref[pl.ds(off, t), :].astype(BF16)
            v2 = v_ref[pl.ds(off, t), :].astype(BF16)
            out = []
            for hh in range(2):
                m, l, acc = carry[hh]
                s = _dg(qs[hh], k2, False, True) + (cqs[hh] - ckj[hh:hh + 1, :])
                if diagonal:
                    keep = lax.broadcasted_iota(jnp.int32, (t, t), 0) >= lax.broadcasted_iota(jnp.int32, (t, t), 1)
                    s = jnp.where(keep, s, NEG_BIG)
                m_new = jnp.maximum(m, jnp.max(s, axis=1, keepdims=True))
                alpha = jnp.exp(m - m_new)
                p = jnp.exp(s - m_new)
                l = alpha * l + jnp.sum(p, axis=1, keepdims=True)
                acc = alpha * acc + _dg(p.astype(BF16), v2, False, False)
                out.append((m_new, l, acc))
            return tuple(out)

        init = tuple((jnp.full((t, 1), NEG_BIG, F32), jnp.zeros((t, 1), F32), jnp.zeros((t, 2 * hd), F32)) for _ in range(2))
        res = lax.fori_loop(0, i, lambda j, c: block(j, c, False), init)
        res = block(i, res, True)
        o_ref[...] = jnp.where(masks[0], res[0][2] / res[0][1], res[1][2] / res[1][1])
        for hh in range(2):
            lse_ref[0, :, hh:hh + 1] = res[hh][0] + jnp.log(res[hh][1])

    blk = pl.BlockSpec((t, 2 * hd), lambda hp, i: (i, hp))
    full = pl.BlockSpec((tokens, 2 * hd), lambda hp, i: (0, hp))
    cq_spec = pl.BlockSpec((1, t, 2), lambda hp, i: (hp, i, 0))
    ck_spec = pl.BlockSpec((1, nb, 2, t), lambda hp, i: (hp, 0, 0, 0))
    return pl.pallas_call(
        body, name="fox_fwd", grid=(npair, nb),
        in_specs=[blk, full, full, cq_spec, ck_spec],
        out_specs=[blk, cq_spec],
        out_shape=[jax.ShapeDtypeStruct((tokens, width), F32), jax.ShapeDtypeStruct((npair, tokens, 2), F32)],
        compiler_params=pltpu.CompilerParams(dimension_semantics=("parallel", "arbitrary")),
    )(q, k, v, cq, ck)


def _fox_bwd_call(q, k, v, cq, ck, o, lse, do):
    tokens, width = q.shape
    t = _attn_block(tokens)
    nb = tokens // t
    hd = HEAD_DIM
    npair = width // (2 * hd)

    def body(q_ref, k_ref, v_ref, cq_ref, ck_ref, o_ref, lse_ref, do_ref, dq_ref, dk_ref, dv_ref, dck_ref, dcq_ref):
        i = pl.program_id(1)

        @pl.when(i == 0)
        def _():
            dk_ref[...] = jnp.zeros_like(dk_ref)
            dv_ref[...] = jnp.zeros_like(dv_ref)
            dck_ref[...] = jnp.zeros_like(dck_ref)

        masks = _head_lane_masks(t)
        q2, do2, o2 = q_ref[...], do_ref[...], o_ref[...]
        qs = [jnp.where(mk, q2, 0.0).astype(BF16) for mk in masks]
        dos = [jnp.where(mk, do2, 0.0).astype(BF16) for mk in masks]
        deltas = [jnp.sum(dos[hh].astype(F32) * o2, axis=1, keepdims=True) for hh in range(2)]
        bias = [cq_ref[0, :, hh:hh + 1] - lse_ref[0, :, hh:hh + 1] for hh in range(2)]

        def block(j, carry, diagonal):
            off = pl.multiple_of(j * t, t)
            ckj = ck_ref[0, j]
            k2 = k_ref[pl.ds(off, t), :].astype(BF16)
            v2 = v_ref[pl.ds(off, t), :].astype(BF16)
            out = []
            dk2 = jnp.zeros((t, 2 * hd), F32)
            dv2 = jnp.zeros((t, 2 * hd), F32)
            for hh in range(2):
                s = _dg(qs[hh], k2, False, True) + (bias[hh] - ckj[hh:hh + 1, :])
                if diagonal:
                    keep = lax.broadcasted_iota(jnp.int32, (t, t), 0) >= lax.broadcasted_iota(jnp.int32, (t, t), 1)
                    s = jnp.where(keep, s, NEG_BIG)
                p = jnp.exp(s)
                dp = _dg(dos[hh], v2, False, True)
                ds = p * (dp - deltas[hh])
                dsb = ds.astype(BF16)
                dq, rowsum = carry[hh]
                out.append((dq + _dg(dsb, k2, False, False), rowsum + jnp.sum(ds, axis=1, keepdims=True)))
                dk2 = dk2 + _dg(dsb, qs[hh], True, False)
                dv2 = dv2 + _dg(p.astype(BF16), dos[hh], True, False)
                dck_ref[0, j, hh:hh + 1, :] -= jnp.sum(ds, axis=0, keepdims=True)
            dk_ref[pl.ds(off, t), :] += dk2
            dv_ref[pl.ds(off, t), :] += dv2
            return tuple(out)

        init = tuple((jnp.zeros((t, 2 * hd), F32), jnp.zeros((t, 1), F32)) for _ in range(2))
        res = lax.fori_loop(0, i, lambda j, c: block(j, c, False), init)
        res = block(i, res, True)
        dq_ref[...] = jnp.where(masks[0], res[0][0], res[1][0])
        for hh in range(2):
            dcq_ref[0, :, hh:hh + 1] = res[hh][1]

    blk = pl.BlockSpec((t, 2 * hd), lambda hp, i: (i, hp))
    full = pl.BlockSpec((tokens, 2 * hd), lambda hp, i: (0, hp))
    cq_spec = pl.BlockSpec((1, t, 2), lambda hp, i: (hp, i, 0))
    ck_spec = pl.BlockSpec((1, nb, 2, t), lambda hp, i: (hp, 0, 0, 0))
    tshape = jax.ShapeDtypeStruct((tokens, width), F32)
    return pl.pallas_call(
        body, name="fox_bwd", grid=(npair, nb),
        in_specs=[blk, full, full, cq_spec, ck_spec, blk, cq_spec, blk],
        out_specs=[blk, full, full, ck_spec, cq_spec],
        out_shape=[tshape, tshape, tshape, jax.ShapeDtypeStruct((npair, nb, 2, t), F32),
                   jax.ShapeDtypeStruct((npair, tokens, 2), F32)],
        compiler_params=pltpu.CompilerParams(dimension_semantics=("parallel", "arbitrary")),
    )(q, k, v, cq, ck, o, lse, do)


@jax.custom_vjp
def fox_attention(q, k, v, cum):
    return _fox_fwd(q, k, v, cum)[0]


def _fox_fwd(q, k, v, cum):
    cq, ck = _fox_layouts(cum)
    q, k, v = q.astype(BF16), k.astype(BF16), v.astype(BF16)
    o, lse = _fox_fwd_call(q, k, v, cq, ck)
    return o, (q, k, v, cq, ck, o, lse)


def _fox_bwd(res, do):
    q, k, v, cq, ck, o, lse = res
    dq, dk, dv, dck, dcq = _fox_bwd_call(q, k, v, cq, ck, o, lse, do)
    npair, nb, _, t = dck.shape
    dcum = dck.transpose(0, 2, 1, 3).reshape(2 * npair, nb * t).T + dcq.transpose(1, 0, 2).reshape(nb * t, 2 * npair)
    return dq, dk, dv, dcum


fox_attention.defvjp(_fox_fwd, _fox_bwd)


def _loss_call(y, target):
    rows, d = y.shape
    tr = _row_tile(rows, d)

    def body(y_ref, t_ref, loss_ref, dy_ref):
        @pl.when(pl.program_id(0) == 0)
        def _():
            loss_ref[...] = jnp.zeros_like(loss_ref)

        diff = y_ref[...] - t_ref[...]
        dy_ref[...] = diff * (1.0 / d)
        loss_ref[...] += (0.5 / d) * jnp.sum(jnp.sum(diff * diff, axis=1, keepdims=True), axis=0, keepdims=True)

    return pl.pallas_call(
        body, name="loss", grid=(rows // tr,),
        in_specs=[pl.BlockSpec((tr, d), lambda i: (i, 0))] * 2,
        out_specs=[pl.BlockSpec((1, 1), lambda i: (0, 0)), pl.BlockSpec((tr, d), lambda i: (i, 0))],
        out_shape=[jax.ShapeDtypeStruct((1, 1), F32), jax.ShapeDtypeStruct((rows, d), F32)],
        compiler_params=pltpu.CompilerParams(dimension_semantics=("arbitrary",)),
    )(y, target)


def _adamw_call(w, g, m, v):
    rows, cols = w.shape
    tr = _row_tile_ragged(rows, cols, budget=1024 * 1024)
    c1 = 1.0 / (1.0 - ADAM_B1 ** ADAM_STEP)
    c2 = 1.0 / (1.0 - ADAM_B2 ** ADAM_STEP)

    def body(w_ref, g_ref, m_ref, v_ref, d_ref, nm_ref, nv_ref):
        gv = g_ref[...]
        nm = ADAM_B1 * m_ref[...] + (1.0 - ADAM_B1) * gv
        nv = ADAM_B2 * v_ref[...] + (1.0 - ADAM_B2) * (gv * gv)
        nm_ref[...] = nm
        nv_ref[...] = nv
        d_ref[...] = -ADAM_LR * ((nm * c1) / (jnp.sqrt(nv * c2) + ADAM_EPS) + ADAM_WD * w_ref[...])

    spec = pl.BlockSpec((tr, cols), lambda i: (i, 0))
    shape = jax.ShapeDtypeStruct((rows, cols), F32)
    return pl.pallas_call(
        body, name="adamw", grid=(pl.cdiv(rows, tr),),
        in_specs=[spec] * 4, out_specs=[spec] * 3, out_shape=[shape] * 3,
        compiler_params=pltpu.CompilerParams(dimension_semantics=("parallel",)),
    )(w, g, m, v)


def _my_place():
    return lax.axis_index("x"), lax.axis_index("y"), lax.axis_index("c")


def _place_index(px, py, pc):
    return 4 * px + 2 * py + pc


HBM_SPEC = pl.BlockSpec(memory_space=pltpu.HBM)


def _all_gather_call(block):
    def body(x_ref, out_ref, send_sems, recv_sems, local_sem):
        x, y, c = _my_place()
        me, sibling = (x, y, c), (x, y, 1 - c)
        chips = [(1 - x, y), (x, 1 - y), (1 - x, 1 - y)]

        def slot(px, py, pc):
            return out_ref.at[_place_index(px, py, pc)]

        def copy(k, blk, to, src=None):
            return pltpu.make_async_remote_copy(
                src_ref=slot(*blk) if src is None else src, dst_ref=slot(*blk),
                send_sem=send_sems.at[k], recv_sem=recv_sems.at[k],
                device_id=to, device_id_type=pl.DeviceIdType.MESH)

        mine = pltpu.make_async_copy(x_ref, slot(*me), local_sem)
        mine.start()
        first = [copy(0, me, sibling, src=x_ref)]
        first += [copy(1 + j, me, (*chip, c), src=x_ref) for j, chip in enumerate(chips)]
        for cp in first:
            cp.start()
        passed = [copy(4 + j, (*chip, c), sibling) for j, chip in enumerate(chips)]
        for j, chip in enumerate(chips):
            copy(1 + j, (*chip, c), me).wait_recv()
            passed[j].start()
        copy(0, sibling, me).wait_recv()
        for j, chip in enumerate(chips):
            copy(4 + j, (*chip, 1 - c), me).wait_recv()
        for cp in first + passed:
            cp.wait_send()
        mine.wait()

    return pl.pallas_call(
        body, name="all_gather",
        out_shape=jax.ShapeDtypeStruct((N_DEV,) + block.shape, block.dtype),
        in_specs=[HBM_SPEC], out_specs=HBM_SPEC,
        scratch_shapes=[pltpu.SemaphoreType.DMA((7,)), pltpu.SemaphoreType.DMA((7,)), pltpu.SemaphoreType.DMA],
    )(block)


SEM_SPEC = pl.BlockSpec(memory_space=pltpu.SEMAPHORE)
SIDE_EFFECT = pltpu.SideEffectType.DATAFLOW_SIDE_EFFECTING


def _peers():
    x, y, c = _my_place()
    out = []
    for k in range(1, N_DEV):
        peer = (x ^ (k >> 2), y ^ ((k >> 1) & 1), c ^ (k & 1))
        out.append((k - 1, peer, _place_index(*peer)))
    return _place_index(x, y, c), out


def _spread_start(src, per_peer, name, after=None):
    slot = src.shape[1:] if per_peer else src.shape
    order = () if after is None else (after,)

    def body(src_ref, land_ref, *rest):
        send_sems, recv_sems, src_thru, land_thru, token = rest[len(order):]
        mine, peers = _peers()
        for k, peer, peer_idx in peers:
            pltpu.make_async_remote_copy(
                src_ref=src_ref.at[peer_idx] if per_peer else src_ref, dst_ref=land_ref.at[mine],
                send_sem=send_sems.at[k], recv_sem=recv_sems.at[k],
                device_id=peer, device_id_type=pl.DeviceIdType.MESH).start()
        token[...] = jnp.zeros_like(token)

    return pl.pallas_call(
        body, name=name,
        out_shape=(pltpu.SemaphoreType.DMA((N_DEV - 1,)), pltpu.SemaphoreType.DMA((N_DEV - 1,)),
                   pltpu.HBM(src.shape, src.dtype), pltpu.HBM((N_DEV,) + slot, src.dtype),
                   jax.ShapeDtypeStruct((8, 128), F32)),
        in_specs=(HBM_SPEC, HBM_SPEC) + (pl.BlockSpec(memory_space=pl.ANY),) * len(order),
        out_specs=(SEM_SPEC, SEM_SPEC, HBM_SPEC, HBM_SPEC, pl.BlockSpec(memory_space=pltpu.VMEM)),
        input_output_aliases={0: 2, 1: 3},
        compiler_params=pltpu.CompilerParams(has_side_effects=SIDE_EFFECT),
    )(pltpu.with_memory_space_constraint(src, pltpu.HBM),
      pltpu.with_memory_space_constraint(lax.empty((N_DEV,) + slot, src.dtype), pltpu.HBM), *order)


def _spread_wait(handles, after, per_peer, name):
    send_sems, recv_sems, src_thru, land_thru = handles

    def body(src_ref, land_ref, send_sems, recv_sems, after_ref, src_dead, got_ref):
        _, peers = _peers()
        for k, peer, peer_idx in peers:
            copy = pltpu.make_async_remote_copy(
                src_ref=src_ref.at[peer_idx] if per_peer else src_ref, dst_ref=land_ref.at[peer_idx],
                send_sem=send_sems.at[k], recv_sem=recv_sems.at[k],
                device_id=peer, device_id_type=pl.DeviceIdType.MESH)
            copy.wait_send()
            copy.wait_recv()

    return pl.pallas_call(
        body, name=name,
        out_shape=(pltpu.HBM(src_thru.shape, src_thru.dtype), pltpu.HBM(land_thru.shape, land_thru.dtype)),
        in_specs=(HBM_SPEC, HBM_SPEC, SEM_SPEC, SEM_SPEC, pl.BlockSpec(memory_space=pl.ANY)),
        out_specs=(HBM_SPEC, HBM_SPEC), input_output_aliases={0: 0, 1: 1},
        compiler_params=pltpu.CompilerParams(has_side_effects=SIDE_EFFECT),
    )(src_thru, land_thru, send_sems, recv_sems, after)


def _sum_slots_call(slots):
    _, rows, cols = slots.shape
    tr = _row_tile_ragged(rows, cols, budget=512 * 1024)

    def body(s_ref, o_ref):
        acc = s_ref[0].astype(F32)
        for j in range(1, N_DEV):
            acc = acc + s_ref[j].astype(F32)
        o_ref[...] = acc

    return pl.pallas_call(
        body, name="sum_slots", grid=(pl.cdiv(rows, tr),),
        in_specs=[pl.BlockSpec((N_DEV, tr, cols), lambda i: (0, i, 0))],
        out_specs=pl.BlockSpec((tr, cols), lambda i: (i, 0)),
        out_shape=jax.ShapeDtypeStruct((rows, cols), F32),
        compiler_params=pltpu.CompilerParams(dimension_semantics=("parallel",)),
    )(slots)


def _sum_adamw_call(got, own, w, m, v):
    rows, cols = w.shape
    tr = _row_tile_ragged(rows, cols, budget=512 * 1024)
    c1 = 1.0 / (1.0 - ADAM_B1 ** ADAM_STEP)
    c2 = 1.0 / (1.0 - ADAM_B2 ** ADAM_STEP)

    def body(got_ref, own_ref, w_ref, m_ref, v_ref, g_ref, d_ref, nm_ref, nv_ref):
        mine = _place_index(*_my_place())
        gv = jnp.zeros(w_ref.shape, F32)
        for j in range(N_DEV):
            gv = gv + jnp.where(mine == j, own_ref[...], got_ref[j]).astype(F32)
        nm = ADAM_B1 * m_ref[...] + (1.0 - ADAM_B1) * gv
        nv = ADAM_B2 * v_ref[...] + (1.0 - ADAM_B2) * (gv * gv)
        g_ref[...] = gv
        nm_ref[...] = nm
        nv_ref[...] = nv
        d_ref[...] = -ADAM_LR * ((nm * c1) / (jnp.sqrt(nv * c2) + ADAM_EPS) + ADAM_WD * w_ref[...])

    spec = pl.BlockSpec((tr, cols), lambda i: (i, 0))
    shape = jax.ShapeDtypeStruct((rows, cols), F32)
    return pl.pallas_call(
        body, name="sum_adamw", grid=(pl.cdiv(rows, tr),),
        in_specs=[pl.BlockSpec((N_DEV, tr, cols), lambda i: (0, i, 0))] + [spec] * 4,
        out_specs=[spec] * 4, out_shape=[shape] * 4,
        compiler_params=pltpu.CompilerParams(dimension_semantics=("parallel",)),
    )(got, own, w, m, v)


def _with_own_slot(got, own, mine):
    return lax.dynamic_update_index_in_dim(got, own, mine, 0)


def _pack(vectors, width):
    flat = jnp.concatenate([v.reshape(-1) for v in vectors])
    return jnp.pad(flat, (0, width - flat.shape[0])).reshape(width // 128, 128)


def _unpack(packed, like):
    flat = packed.reshape(-1)
    out, at = [], 0
    for v in like:
        out.append(flat[at:at + v.size].reshape(v.shape))
        at += v.size
    return tuple(out)


def _cols_from_slots(slots):
    n, rows, cols = slots.shape
    return slots.transpose(1, 0, 2).reshape(rows, n * cols)


def _rows_from_slots(slots):
    return slots.reshape(-1, slots.shape[2])


def _pad128(n):
    return -(-n // 128) * 128


def _pad_to_tiles(a, axis):
    n = a.shape[axis]
    pads = [(0, 0)] * a.ndim
    pads[axis] = (0, _pad128(n) - n)
    return jnp.pad(a, pads)


def _rwkv_group(take, zeros, rw, dl, al, gl):
    at = 3 * rw
    parts = take(0, at)
    for n in (dl, al, gl):
        parts += take(at, at + n)
        if _pad128(n) > n:
            parts.append(zeros(_pad128(n) - n))
        at += n
    return parts


def _in_proj_layout(slots, rw, fw, dl, al, gl, whole):
    n_slots, rows, d = slots.shape
    wt = slots.reshape(n_slots * rows, d)
    take = lambda lo, hi: [wt[lo:hi]]
    zeros = lambda n: jnp.zeros((n, d), wt.dtype)
    rcols = 3 * rw + dl + al + gl
    fcols = 3 * fw + fw // HEAD_DIM
    group_r = _rwkv_group(take, zeros, rw, dl, al, gl)
    group_f = take(rcols, rcols + fcols) + ([zeros(_pad128(fcols) - fcols)] if _pad128(fcols) > fcols else [])
    group_g = take(rcols + fcols, n_slots * rows)
    if whole:
        return jnp.concatenate(group_r + group_f + group_g, axis=0)
    return tuple(jnp.concatenate(g, axis=0) for g in (group_r, group_f, group_g))


def _low_rank_layout(slots):
    return _pad_to_tiles(_cols_from_slots(slots), 0)


def _stage_embed(meta, x, n1, lp):
    h0 = jnp.concatenate([meta, x, jnp.zeros((lp - meta.shape[0] - x.shape[0], x.shape[1]), F32)], axis=0)
    return h0, rmsnorm(h0, n1)


def _stage_mix(z_r, z_f, small, w2, a2, g2, dims):
    (mu, w0, a0, k_k, k_a, r_k, gn_w, gn_b, q_g, k_g, f_bias) = small
    rw, fw, dl, al, gl = dims
    fcols = 3 * fw + fw // HEAD_DIM

    mu_group = jnp.concatenate(_rwkv_group(lambda lo, hi: [mu[:, lo:hi]], lambda n: jnp.zeros((1, n), F32), rw, dl, al, gl), axis=1)
    r, lw, kf, v, na, b, g = rwkv_prep(z_r, mu_group, w0, a0, k_k, k_a, w2, a2, g2)
    y = wkv7(r, lw, kf, v, na, b)
    y_a = gn_bonus(y, r, kf, v, g, gn_w, gn_b, r_k.reshape(1, rw))

    fq, fk, fv, fl = z_f[:, :fw], z_f[:, fw:2 * fw], z_f[:, 2 * fw:3 * fw], z_f[:, 3 * fw:fcols]
    fq = head_rms(fq, jnp.tile(q_g, (1, fw // HEAD_DIM))) * (HEAD_DIM ** -0.5)
    fk = head_rms(fk, jnp.tile(k_g, (1, fw // HEAD_DIM)))
    cum = jnp.cumsum(jax.nn.log_sigmoid(badd(fl, f_bias)), axis=0)
    y_b = fox_attention(fq, fk, fv, cum)
    return y_a, y_b


def _stage_merge(h0, y_a, y_b, z_g, w_a, w_b, w_o):
    merged = gated_merge(z_g, dense_cols_bf16(y_a, w_a), dense_cols_bf16(y_b, w_b))
    return dense_add(merged, w_o, h0)


def _stage_ffn(h1, n2, w_gu, w_dn):
    return dense_add(swiglu(dense_cols_bf16(rmsnorm(h1, n2), w_gu)), w_dn, h1)


SHARDED = ("meta_tokens", "w_in", "rwkv_w2", "rwkv_a2", "rwkv_g2", "w_branch_a", "w_branch_b", "w_o", "w_gate_up", "w_down")
SMALL = ("norm1_g", "rwkv_mu", "rwkv_w0", "rwkv_a0", "rwkv_k_k", "rwkv_k_a", "rwkv_r_k", "rwkv_gn_w", "rwkv_gn_b",
         "fox_q_norm_g", "fox_k_norm_g", "fox_f_bias", "norm2_g")
WEIGHTS = ("meta_tokens", "norm1_g", "w_in", "rwkv_mu", "rwkv_w0", "rwkv_w2", "rwkv_a0", "rwkv_a2", "rwkv_g2", "rwkv_k_k",
           "rwkv_k_a", "rwkv_r_k", "rwkv_gn_w", "rwkv_gn_b", "fox_q_norm_g", "fox_k_norm_g", "fox_f_bias", "w_branch_a",
           "w_branch_b", "w_o", "norm2_g", "w_gate_up", "w_down")


def _as2d(a):
    return a.reshape(-1, a.shape[-1])


def kernel(x, meta_tokens, norm1_g, w_in, rwkv_mu, rwkv_w0, rwkv_w2, rwkv_a0, rwkv_a2, rwkv_g2, rwkv_k_k, rwkv_k_a, rwkv_r_k, rwkv_gn_w, rwkv_gn_b, fox_q_norm_g, fox_k_norm_g, fox_f_bias, w_branch_a, w_branch_b, w_o, norm2_g, w_gate_up, w_down, loss_target, m_meta_tokens, m_norm1_g, m_w_in, m_rwkv_mu, m_rwkv_w0, m_rwkv_w2, m_rwkv_a0, m_rwkv_a2, m_rwkv_g2, m_rwkv_k_k, m_rwkv_k_a, m_rwkv_r_k, m_rwkv_gn_w, m_rwkv_gn_b, m_fox_q_norm_g, m_fox_k_norm_g, m_fox_f_bias, m_w_branch_a, m_w_branch_b, m_w_o, m_norm2_g, m_w_gate_up, m_w_down, v_meta_tokens, v_norm1_g, v_w_in, v_rwkv_mu, v_rwkv_w0, v_rwkv_w2, v_rwkv_a0, v_rwkv_a2, v_rwkv_g2, v_rwkv_k_k, v_rwkv_k_a, v_rwkv_r_k, v_rwkv_gn_w, v_rwkv_gn_b, v_fox_q_norm_g, v_fox_k_norm_g, v_fox_f_bias, v_w_branch_a, v_w_branch_b, v_w_o, v_norm2_g, v_w_gate_up, v_w_down):
    given = dict(locals())
    w = {n: given[n] for n in WEIGHTS}
    assert rwkv_r_k.shape[-1] == HEAD_DIM
    n_meta, seq = meta_tokens.shape[0], x.shape[1]
    tokens = n_meta + seq
    lp = -(-tokens // TOKEN_TILE) * TOKEN_TILE
    mine = _place_index(*(lax.axis_index(a) for a in MESH_AXES))
    x2 = x[0]

    local = {n: _as2d(given[n]) for n in given if n != "x" and n != "loss_target"}
    for n in ("w_in", "m_w_in", "v_w_in"):
        local[n] = jnp.transpose(given[n][0])
    blocks = {n: local[n].astype(F32 if n == "meta_tokens" else BF16) for n in SHARDED}
    first = ("meta_tokens", "rwkv_w2", "rwkv_a2", "rwkv_g2")
    started = {n: _spread_start(blocks[n], False, "gather_start_" + n) for n in first}
    zero = sum(started[n][4][0, 0] for n in first)

    def gathered(n, after):
        own, got = _spread_wait(started[n][:4], after, False, "gather_wait_" + n)
        return _with_own_slot(got, own, mine)

    sm = {n: _as2d(w[n]) for n in SMALL}
    small_mix = tuple(sm[n] for n in SMALL[1:-1])
    n1 = sm["norm1_g"] + zero
    rw, fw = w_branch_a.shape[-2], w_branch_b.shape[-2]
    dims = (rw, fw, rwkv_w2.shape[-2], rwkv_a2.shape[-2], rwkv_g2.shape[-2])
    same = lambda s: (s,)

    meta, un_meta = jax.vjp(_cols_from_slots, gathered("meta_tokens", x2))
    (h0, xn), vjp_embed = jax.vjp(lambda m, xs, g: _stage_embed(m, xs, g, lp), meta, x2, n1)
    in_slots = _all_gather_call(blocks["w_in"])
    later = [n for n in SHARDED if n not in first and n != "w_in"]
    started.update({n: _spread_start(blocks[n], False, "gather_start_" + n, after=in_slots) for n in later})
    w_groups = _in_proj_layout(in_slots, *dims, whole=False)
    w_cat, un_in = jax.vjp(lambda s: _in_proj_layout(s, *dims, whole=True), in_slots)
    xn_b = xn.astype(BF16)
    behind = sum(started[n][4] for n in later)
    z_r, z_f, z_g = (_matmul(xn_b, wg, tb=True, name="in_proj_" + tag, after=behind, out_dtype=BF16 if tag == "g" else F32)
                     for wg, tag in zip(w_groups, "rfg"))
    (w2, un_w2), (a2, un_a2), (g2, un_g2) = (jax.vjp(_low_rank_layout, gathered(n, xn)) for n in ("rwkv_w2", "rwkv_a2", "rwkv_g2"))
    (y_a, y_b), vjp_mix = jax.vjp(lambda zr, zf, s, a, b, c: _stage_mix(zr, zf, s, a, b, c, dims),
                                  z_r, z_f, small_mix, w2, a2, g2)
    w_a, w_b = gathered("w_branch_a", y_a), gathered("w_branch_b", y_a)
    w_o_full, un_wo = jax.vjp(_rows_from_slots, gathered("w_o", y_a))
    h1, vjp_merge = jax.vjp(_stage_merge, h0, y_a, y_b, z_g, w_a, w_b, w_o_full)
    w_gu = gathered("w_gate_up", h1)
    w_dn, un_dn = jax.vjp(_rows_from_slots, gathered("w_down", h1))
    y, vjp_ffn = jax.vjp(_stage_ffn, h1, sm["norm2_g"], w_gu, w_dn)

    loss_part, dy_real = _loss_call(y[n_meta:tokens], loss_target[0])
    dy = jnp.pad(dy_real, ((n_meta, lp - tokens), (0, 0)))
    loss = lax.psum(loss_part[0, 0], MESH_AXES)

    sent = {}

    def send_grad(n, dmat, unlayout):
        sent[n] = _spread_start(unlayout(dmat)[0], True, "grad_start_" + n)
        return sent[n][4][0, 0]

    d_h1, d_n2, d_wgu, d_wdn = vjp_ffn(dy)
    behind = send_grad("w_gate_up", d_wgu, same) + send_grad("w_down", d_wdn, un_dn)
    d_h0, d_ya, d_yb, d_zg, d_wa, d_wb, d_wo = vjp_merge(d_h1 + behind)
    behind = send_grad("w_o", d_wo, un_wo) + send_grad("w_branch_a", d_wa, same) + send_grad("w_branch_b", d_wb, same)
    d_zr, d_zf, d_small_mix, d_w2, d_a2, d_g2 = vjp_mix((d_ya + behind, d_yb))
    dproj_b = jnp.concatenate([d_zr.astype(BF16), d_zf.astype(BF16), d_zg.astype(BF16)], axis=1)
    d_wcat = _matmul(dproj_b, xn_b, ta=True, out_dtype=BF16, name="in_proj_dw")
    send_grad("w_in", d_wcat, un_in)
    d_xn = _matmul(dproj_b, w_cat, out_dtype=BF16, name="in_proj_dx", after=sent["w_in"][4])
    send_grad("rwkv_w2", d_w2, un_w2)
    send_grad("rwkv_a2", d_a2, un_a2)
    send_grad("rwkv_g2", d_g2, un_g2)
    d_meta, g_x, d_n1 = vjp_embed((d_h0, d_xn))
    send_grad("meta_tokens", d_meta, un_meta)

    small_grads = (d_n1, *d_small_mix, d_n2)
    n_small = sum(g.size for g in small_grads)
    width = -(-n_small // 1024) * 1024
    small_sent = _spread_start(_pack(small_grads, width), False, "small_grad_start")

    grads, delta, new_m, new_v = {}, {}, {}, {}
    after = g_x
    for n in ("w_gate_up", "w_down", "w_o", "w_branch_a", "w_branch_b", "rwkv_g2", "rwkv_a2", "rwkv_w2", "meta_tokens", "w_in"):
        src, got = _spread_wait(sent[n][:4], after, True, "grad_wait_" + n)
        own = lax.dynamic_index_in_dim(src, mine, 0, keepdims=False)
        g, d_, m_, v_ = _sum_adamw_call(got, own, local[n], local["m_" + n], local["v_" + n])
        back = (lambda t: jnp.transpose(t)[None]) if n == "w_in" else (lambda t: t.reshape(w[n].shape))
        grads[n], delta[n], new_m[n], new_v[n] = (back(t) for t in (g, d_, m_, v_))
        after = m_
    own_small, got_small = _spread_wait(small_sent[:4], after, False, "small_grad_wait")
    small_total = _unpack(_sum_slots_call(_with_own_slot(got_small, own_small, mine)), small_grads)
    grads.update({n: g.reshape(w[n].shape) for n, g in zip(SMALL, small_total)})
    packs = [_pack([src[n] if p == "" else given[p + n] for n in SMALL], width)
             for p, src in (("", w), ("", grads), ("m_", None), ("v_", None))]
    like = [w[n] for n in SMALL]
    for out, packed in zip((delta, new_m, new_v), _adamw_call(*packs)):
        out.update(dict(zip(SMALL, _unpack(packed, like))))

    return (loss, g_x[None], *[grads[n] for n in WEIGHTS], *[delta[n] for n in WEIGHTS],
            *[new_m[n] for n in WEIGHTS], *[new_v[n] for n in WEIGHTS])
```

```python
import jax
import jax.numpy as jnp
from jax import lax
from jax.experimental import pallas as pl
from jax.experimental.pallas import tpu as pltpu

F32 = jnp.float32
BF16 = jnp.bfloat16

N_DEV = 8
MESH_AXES = ("x", "y", "c")
HEAD_DIM = 64
TOKEN_TILE = 128
WKV_CHUNK = 64
WKV_PAIRS_PER_STEP = 8
PAIR = 2 * HEAD_DIM
ATTN_BLOCK = 128
ATTN_BLOCK_BIG = 384
RMS_EPS = 1e-6
GN_EPS = 64e-5
L2_FLOOR = 1e-12
NEG_BIG = -1e30
ADAM_LR, ADAM_B1, ADAM_B2, ADAM_EPS, ADAM_WD, ADAM_STEP = 0.001, 0.9, 0.999, 1e-08, 0.01, 10
VMEM_LIMIT_CAP = 56 * 1024 * 1024
VMEM_LIMIT_FLOOR = 32 * 1024 * 1024
MATMUL_VMEM_BUDGET = 36 * 1024 * 1024
GRID_STEP_BYTES = 1024 * 1024
ACC_BYTES_PER_HBM_BYTE = 6


def _vmem_limit(estimate_bytes):
    return int(min(max(estimate_bytes * 5 // 4, VMEM_LIMIT_FLOOR), VMEM_LIMIT_CAP))


def _row_tile(rows, width, itemsize=4, budget=2 * 1024 * 1024):
    for c in (1408, 1024, 704, 512, 384, 256, 128, 64, 32, 16, 8):
        if rows % c == 0 and c * width * itemsize <= budget:
            return c
    return rows


def _row_tile_ragged(rows, width, itemsize=4, budget=2 * 1024 * 1024):
    tile = _row_tile(rows, width, itemsize, budget)
    if tile * width * itemsize <= budget or rows < 16:
        return tile
    padded = -(-rows // 16) * 16
    for c in (1408, 1024, 704, 512, 384, 336, 256, 192, 128, 96, 64, 48, 32, 16):
        if padded % c == 0 and c * width * itemsize <= budget:
            return c
    return tile


def _dg(a, b, ta, tb):
    dims = (((0 if ta else 1,), (1 if tb else 0,)), ((), ()))
    return lax.dot_general(a, b, dims, preferred_element_type=F32)


def _split(x, n):
    parts = []
    for _ in range(n):
        h = x.astype(BF16)
        parts.append(h)
        x = x - h.astype(F32)
    return parts


def _mm(a, b, ta=False, tb=False):
    return _dg(a.astype(BF16), b.astype(BF16), ta, tb)


def _matmul(a, b, ta=False, tb=False, out_dtype=F32, name="matmul", after=None, b_slots=False, out_slots=0, add=None):
    if ta:
        kdim, m = a.shape
    else:
        m, kdim = a.shape
    if b_slots:
        n_slots, brows, bcols = b.shape
        n, k2 = (brows, n_slots * bcols) if tb else (n_slots * bcols, brows)
    elif tb:
        n, k2 = b.shape
    else:
        k2, n = b.shape
    assert kdim == k2, (a.shape, b.shape, ta, tb)
    sa, sb, so = a.dtype.itemsize, b.dtype.itemsize, jnp.dtype(out_dtype).itemsize
    n_unit = bcols if (b_slots and not tb) else (n // out_slots if out_slots else n)
    k_unit = bcols if (b_slots and tb) else kdim
    tm, tn, tk, n_outer = _matmul_tiles(m, n, kdim, ta, sa, sb, so, n_unit, k_unit)
    nk = kdim // tk
    ij = (lambda f: lambda j, i, k: f(i, j, k)) if n_outer else (lambda f: f)

    order = () if after is None else (after,)
    extra = () if add is None else (add,)

    def body(a_ref, b_ref, *rest):
        rest = rest[len(order):]
        add_ref = rest[0] if extra else None
        o_ref, acc = rest[len(extra)], rest[len(extra) + 1:]
        part = _dg(a_ref[...].astype(BF16), b_ref[...].astype(BF16), ta, tb)
        done = lambda total: (total if add_ref is None else total + add_ref[...]).astype(o_ref.dtype)
        if nk == 1:
            o_ref[...] = done(part)
            return
        kk = pl.program_id(2)

        @pl.when(kk == 0)
        def _():
            acc[0][...] = part

        @pl.when(kk > 0)
        def _():
            acc[0][...] += part

        @pl.when(kk == nk - 1)
        def _():
            o_ref[...] = done(acc[0][...])

    a_spec = pl.BlockSpec((tk, tm), ij(lambda i, j, k: (k, i))) if ta else pl.BlockSpec((tm, tk), ij(lambda i, j, k: (i, k)))
    if b_slots and tb:
        per = bcols // tk
        b_spec = pl.BlockSpec((None, tn, tk), ij(lambda i, j, k: (k // per, j, k % per)))
    elif b_slots:
        per = bcols // tn
        b_spec = pl.BlockSpec((None, tk, tn), ij(lambda i, j, k: (j // per, k, j % per)))
    elif tb:
        b_spec = pl.BlockSpec((tn, tk), ij(lambda i, j, k: (j, k)))
    else:
        b_spec = pl.BlockSpec((tk, tn), ij(lambda i, j, k: (k, j)))
    if out_slots:
        per_out = n // out_slots // tn
        out_spec = pl.BlockSpec((None, tm, tn), ij(lambda i, j, k: (j // per_out, i, j % per_out)))
        out_shape = jax.ShapeDtypeStruct((out_slots, m, n // out_slots), out_dtype)
    else:
        out_spec = pl.BlockSpec((tm, tn), ij(lambda i, j, k: (i, j)))
        out_shape = jax.ShapeDtypeStruct((m, n), out_dtype)
    return pl.pallas_call(
        body, name=name,
        grid=(n // tn, m // tm, nk) if n_outer else (m // tm, n // tn, nk),
        in_specs=[a_spec, b_spec] + [pl.BlockSpec(memory_space=pl.ANY)] * len(order)
                 + [pl.BlockSpec((tm, tn), ij(lambda i, j, k: (i, j)))] * len(extra),
        out_specs=out_spec,
        out_shape=out_shape,
        scratch_shapes=[pltpu.VMEM((tm, tn), F32)] if nk > 1 else [],
        compiler_params=pltpu.CompilerParams(
            dimension_semantics=("parallel", "parallel", "arbitrary"),
            vmem_limit_bytes=_vmem_limit(_matmul_vmem(tm, tn, tk, nk, sa, sb, so) + 2 * tm * tn * 4 * len(extra))),
    )(a, b, *order, *extra)


def _matmul_vmem(tm, tn, tk, nk, sa, sb, so):
    return 2 * (tm * tk * sa + tk * tn * sb + tm * tn * so) + tm * tn * 4 + (tm * tn * 4 if nk > 1 else 0)


def _matmul_tiles(m, n, kdim, ta, sa, sb, so, n_unit, k_unit):
    lane = (2816, 2176, 2048, 1408, 1024, 640, 512, 384, 256, 128)
    sublane = (2816, 2176, 2048, 1408, 1024, 704, 512, 384, 256, 128)
    divs = lambda dim, cands: [c for c in cands if dim % c == 0] or [dim]
    best = None
    for tm in divs(m, lane if ta else sublane):
        for tn in divs(n_unit, lane):
            for tk in divs(k_unit, sublane if ta else lane) + ([kdim] if k_unit == kdim and (ta or kdim <= 2048) else []):
                nk, nm, nn = kdim // tk, m // tm, n // tn
                if _matmul_vmem(tm, tn, tk, nk, sa, sb, so) > MATMUL_VMEM_BUDGET:
                    continue
                acc_bytes = m * n * 4 * 3 * nk // ACC_BYTES_PER_HBM_BYTE if nk > 1 else 0
                fixed = m * n * so + acc_bytes + nm * nn * nk * GRID_STEP_BYTES
                for n_outer in (False, True):
                    if n_outer:
                        a_reads, b_reads = (1 if (nk == 1 and nm == 1) else nn), (1 if nk == 1 else nm)
                    else:
                        a_reads, b_reads = (1 if nk == 1 else nn), (1 if (nk == 1 and nn == 1) else nm)
                    cost = m * kdim * sa * a_reads + kdim * n * sb * b_reads + fixed
                    if best is None or cost < best[0]:
                        best = (cost, tm, tn, tk, n_outer)
    return best[1:]


@jax.custom_vjp
def dense(x, w):
    return _matmul(x.astype(BF16), w, name="dense_fwd")


def _dense_fwd(x, w):
    return _matmul(x.astype(BF16), w, name="dense_fwd"), (x.astype(BF16), w, jnp.zeros((), x.dtype))


def _dense_bwd(res, dy):
    xb, w, like = res
    dyb = dy.astype(BF16)
    dx = _matmul(dyb, w, tb=True, out_dtype=like.dtype, name="dense_dx")
    dw = _matmul(xb, dyb, ta=True, out_dtype=w.dtype, name="dense_dw")
    return dx, dw


dense.defvjp(_dense_fwd, _dense_bwd)


@jax.custom_vjp
def dense_add(x, w, res):
    return _matmul(x.astype(BF16), w, name="dense_add_fwd", add=res)


def _dense_add_fwd(x, w, res):
    return _matmul(x.astype(BF16), w, name="dense_add_fwd", add=res), (x.astype(BF16), w, jnp.zeros((), x.dtype))


def _dense_add_bwd(res, dy):
    return (*_dense_bwd(res, dy), dy)


dense_add.defvjp(_dense_add_fwd, _dense_add_bwd)


def _make_dense_cols(out_dtype):
    @jax.custom_vjp
    def op(x, w_slots):
        return _matmul(x.astype(BF16), w_slots, b_slots=True, out_dtype=out_dtype, name="dense_cols_fwd")

    def fwd(x, w_slots):
        xb = x.astype(BF16)
        return (_matmul(xb, w_slots, b_slots=True, out_dtype=out_dtype, name="dense_cols_fwd"),
                (xb, w_slots, jnp.zeros((), x.dtype)))

    def bwd(res, dy):
        xb, w_slots, like = res
        dyb = dy.astype(BF16)
        dx = _matmul(dyb, w_slots, tb=True, b_slots=True, out_dtype=like.dtype, name="dense_cols_dx")
        dw = _matmul(xb, dyb, ta=True, out_slots=w_slots.shape[0], out_dtype=w_slots.dtype, name="dense_cols_dw")
        return dx, dw

    op.defvjp(fwd, bwd)
    return op


dense_cols_bf16 = _make_dense_cols(BF16)


def _swiglu_call(gu, d_act=None):
    rows, two_f = gu.shape
    f = two_f // 2
    tr = _row_tile(rows, two_f, itemsize=2, budget=3 * 1024 * 1024)
    half = lambda j: pl.BlockSpec((tr, f), lambda i, j=j: (i, j))
    ops = (gu, gu) if d_act is None else (gu, gu, d_act)

    def body(*refs):
        g, u = refs[0][...].astype(F32), refs[1][...].astype(F32)
        s = 1.0 / (1.0 + jnp.exp(-g))
        if d_act is None:
            refs[2][...] = (g * s * u).astype(BF16)
        else:
            d = refs[2][...].astype(F32)
            refs[3][:, :f] = (d * u * s * (1.0 + g * (1.0 - s))).astype(BF16)
            refs[3][:, f:] = (d * g * s).astype(BF16)

    width = f if d_act is None else two_f
    return pl.pallas_call(
        body, name="swiglu_fwd" if d_act is None else "swiglu_bwd", grid=(rows // tr,),
        in_specs=[half(0), half(1)] + ([half(0)] if d_act is not None else []),
        out_specs=pl.BlockSpec((tr, width), lambda i: (i, 0)),
        out_shape=jax.ShapeDtypeStruct((rows, width), BF16),
        compiler_params=pltpu.CompilerParams(dimension_semantics=("parallel",)),
    )(*ops)


@jax.custom_vjp
def swiglu(gu):
    return _swiglu_call(gu)


swiglu.defvjp(lambda gu: (_swiglu_call(gu), gu), lambda gu, d_act: (_swiglu_call(gu, d_act),))


def _merge_call(zg, a, b, dm=None):
    rows, d = a.shape
    tr = _row_tile(rows, d, budget=1024 * 1024)
    half = lambda j: pl.BlockSpec((tr, d), lambda i, j=j: (i, j))
    tile = half(0)

    def body(*refs):
        ga = 1.0 / (1.0 + jnp.exp(-refs[0][...].astype(F32)))
        gb = 1.0 / (1.0 + jnp.exp(-refs[1][...].astype(F32)))
        av, bv = refs[2][...].astype(F32), refs[3][...].astype(F32)
        if dm is None:
            refs[4][...] = (ga * av + gb * bv).astype(BF16)
        else:
            dv = refs[4][...].astype(F32)
            dzg_ref, da_ref, db_ref = refs[5:]
            dzg_ref[:, :d] = (dv * av * ga * (1.0 - ga)).astype(dzg_ref.dtype)
            dzg_ref[:, d:] = (dv * bv * gb * (1.0 - gb)).astype(dzg_ref.dtype)
            da_ref[...] = (dv * ga).astype(BF16)
            db_ref[...] = (dv * gb).astype(BF16)

    shape_b = jax.ShapeDtypeStruct((rows, d), BF16)
    if dm is None:
        out_specs, out_shape, ops = tile, shape_b, (zg, zg, a, b)
    else:
        out_specs = [pl.BlockSpec((tr, 2 * d), lambda i: (i, 0)), tile, tile]
        out_shape = [jax.ShapeDtypeStruct((rows, 2 * d), zg.dtype), shape_b, shape_b]
        ops = (zg, zg, a, b, dm)
    return pl.pallas_call(
        body, name="merge_fwd" if dm is None else "merge_bwd", grid=(rows // tr,),
        in_specs=[half(0), half(1)] + [tile] * (len(ops) - 2),
        out_specs=out_specs, out_shape=out_shape,
        compiler_params=pltpu.CompilerParams(dimension_semantics=("parallel",)),
    )(*ops)


@jax.custom_vjp
def gated_merge(zg, a, b):
    return _merge_call(zg, a, b)


gated_merge.defvjp(lambda zg, a, b: (_merge_call(zg, a, b), (zg, a, b)),
                   lambda res, dm: tuple(_merge_call(*res, dm)))


def _rms_fwd_call(x, g):
    rows, d = x.shape
    tr = _row_tile(rows, d)

    def body(x_ref, g_ref, y_ref):
        xv = x_ref[...]
        rstd = lax.rsqrt(jnp.mean(xv * xv, axis=1, keepdims=True) + RMS_EPS)
        y_ref[...] = ((xv * rstd) * g_ref[...]).astype(BF16)

    return pl.pallas_call(
        body, name="rms_fwd", grid=(rows // tr,),
        in_specs=[pl.BlockSpec((tr, d), lambda i: (i, 0)), pl.BlockSpec((1, d), lambda i: (0, 0))],
        out_specs=pl.BlockSpec((tr, d), lambda i: (i, 0)),
        out_shape=jax.ShapeDtypeStruct((rows, d), BF16),
        compiler_params=pltpu.CompilerParams(dimension_semantics=("parallel",)),
    )(x, g)


def _rms_bwd_call(x, g, dy):
    rows, d = x.shape
    tr = _row_tile(rows, d)

    def body(x_ref, g_ref, dy_ref, dx_ref, dg_ref):
        @pl.when(pl.program_id(0) == 0)
        def _():
            dg_ref[...] = jnp.zeros_like(dg_ref)

        xv = x_ref[...]
        dyv = dy_ref[...].astype(F32)
        rstd = lax.rsqrt(jnp.mean(xv * xv, axis=1, keepdims=True) + RMS_EPS)
        xhat = xv * rstd
        dxhat = dyv * g_ref[...]
        dx_ref[...] = rstd * (dxhat - xhat * jnp.mean(dxhat * xhat, axis=1, keepdims=True))
        dg_ref[...] += jnp.sum(dyv * xhat, axis=0, keepdims=True)

    return pl.pallas_call(
        body, name="rms_bwd", grid=(rows // tr,),
        in_specs=[pl.BlockSpec((tr, d), lambda i: (i, 0)), pl.BlockSpec((1, d), lambda i: (0, 0)),
                  pl.BlockSpec((tr, d), lambda i: (i, 0))],
        out_specs=[pl.BlockSpec((tr, d), lambda i: (i, 0)), pl.BlockSpec((1, d), lambda i: (0, 0))],
        out_shape=[jax.ShapeDtypeStruct((rows, d), F32), jax.ShapeDtypeStruct((1, d), F32)],
        compiler_params=pltpu.CompilerParams(dimension_semantics=("arbitrary",)),
    )(x, g, dy)


@jax.custom_vjp
def rmsnorm(x, g):
    return _rms_fwd_call(x, g)


rmsnorm.defvjp(lambda x, g: (_rms_fwd_call(x, g), (x, g)), lambda res, dy: tuple(_rms_bwd_call(res[0], res[1], dy)))


def _bcast_add_call(x, p):
    rows, d = x.shape
    tr = _row_tile(rows, d)

    def body(x_ref, p_ref, y_ref):
        y_ref[...] = x_ref[...] + p_ref[...]

    return pl.pallas_call(
        body, name="bcast_add", grid=(rows // tr,),
        in_specs=[pl.BlockSpec((tr, d), lambda i: (i, 0)), pl.BlockSpec((1, d), lambda i: (0, 0))],
        out_specs=pl.BlockSpec((tr, d), lambda i: (i, 0)),
        out_shape=jax.ShapeDtypeStruct((rows, d), F32),
        compiler_params=pltpu.CompilerParams(dimension_semantics=("parallel",)),
    )(x, p)


def _colsum_call(a):
    rows, d = a.shape
    tr = _row_tile(rows, d)

    def body(a_ref, o_ref):
        @pl.when(pl.program_id(0) == 0)
        def _():
            o_ref[...] = jnp.zeros_like(o_ref)

        o_ref[...] += jnp.sum(a_ref[...], axis=0, keepdims=True)

    return pl.pallas_call(
        body, name="colsum", grid=(rows // tr,),
        in_specs=[pl.BlockSpec((tr, d), lambda i: (i, 0))],
        out_specs=pl.BlockSpec((1, d), lambda i: (0, 0)),
        out_shape=jax.ShapeDtypeStruct((1, d), F32),
        compiler_params=pltpu.CompilerParams(dimension_semantics=("arbitrary",)),
    )(a)


@jax.custom_vjp
def badd(x, p):
    return _bcast_add_call(x, p)


badd.defvjp(lambda x, p: (_bcast_add_call(x, p), None), lambda res, dy: (dy, _colsum_call(dy)))


def _head_sums(x):
    i = lax.broadcasted_iota(jnp.int32, (PAIR, PAIR), 0) // HEAD_DIM
    j = lax.broadcasted_iota(jnp.int32, (PAIR, PAIR), 1) // HEAD_DIM
    ones = jnp.where(i == j, 1.0, 0.0).astype(BF16)
    hi, lo = _split(x, 2)
    cols = [slice(p * PAIR, (p + 1) * PAIR) for p in range(x.shape[1] // PAIR)]
    return jnp.concatenate([_dg(hi[:, c], ones, False, False) + _dg(lo[:, c], ones, False, False) for c in cols], axis=1)


def _head_rms_fwd_call(x, g):
    rows, w = x.shape
    tr = _row_tile(rows, w, budget=1024 * 1024)

    def body(x_ref, g_ref, y_ref):
        xv = x_ref[...]
        rstd = lax.rsqrt(_head_sums(xv * xv) * (1.0 / HEAD_DIM) + RMS_EPS)
        y_ref[...] = (xv * rstd) * g_ref[...]

    return pl.pallas_call(
        body, name="head_rms_fwd", grid=(rows // tr,),
        in_specs=[pl.BlockSpec((tr, w), lambda i: (i, 0)), pl.BlockSpec((1, w), lambda i: (0, 0))],
        out_specs=pl.BlockSpec((tr, w), lambda i: (i, 0)),
        out_shape=jax.ShapeDtypeStruct((rows, w), F32),
        compiler_params=pltpu.CompilerParams(dimension_semantics=("parallel",)),
    )(x, g)


def _head_rms_bwd_call(x, g, dy):
    rows, w = x.shape
    tr = _row_tile(rows, w, budget=1024 * 1024)

    def body(x_ref, g_ref, dy_ref, dx_ref, dg_ref):
        @pl.when(pl.program_id(0) == 0)
        def _():
            dg_ref[...] = jnp.zeros_like(dg_ref)

        xv, dyv = x_ref[...], dy_ref[...]
        rstd = lax.rsqrt(_head_sums(xv * xv) * (1.0 / HEAD_DIM) + RMS_EPS)
        xhat = xv * rstd
        dxhat = dyv * g_ref[...]
        dx_ref[...] = rstd * (dxhat - xhat * (_head_sums(dxhat * xhat) * (1.0 / HEAD_DIM)))
        dg_ref[...] += jnp.sum(dyv * xhat, axis=0, keepdims=True)

    return pl.pallas_call(
        body, name="head_rms_bwd", grid=(rows // tr,),
        in_specs=[pl.BlockSpec((tr, w), lambda i: (i, 0)), pl.BlockSpec((1, w), lambda i: (0, 0)),
                  pl.BlockSpec((tr, w), lambda i: (i, 0))],
        out_specs=[pl.BlockSpec((tr, w), lambda i: (i, 0)), pl.BlockSpec((1, w), lambda i: (0, 0))],
        out_shape=[jax.ShapeDtypeStruct((rows, w), F32), jax.ShapeDtypeStruct((1, w), F32)],
        compiler_params=pltpu.CompilerParams(dimension_semantics=("arbitrary",)),
    )(x, g, dy)


@jax.custom_vjp
def head_rms(x, g):
    return _head_rms_fwd_call(x, g)


head_rms.defvjp(lambda x, g: (_head_rms_fwd_call(x, g), (x, g)),
                lambda res, dy: tuple(_head_rms_bwd_call(res[0], res[1], dy)))


def _gn_fwd_call(y, r, kf, v, g, gw, gb, rk):
    rows, w = y.shape
    tr = _row_tile(rows, w, budget=512 * 1024)

    def body(y_ref, r_ref, kf_ref, v_ref, g_ref, gw_ref, gb_ref, rk_ref, o_ref):
        yv = y_ref[...]
        yc = yv - _head_sums(yv) * (1.0 / HEAD_DIM)
        rstd = lax.rsqrt(_head_sums(yc * yc) * (1.0 / HEAD_DIM) + GN_EPS)
        s = _head_sums(r_ref[...] * kf_ref[...] * rk_ref[...])
        o_ref[...] = ((yc * rstd) * gw_ref[...] + gb_ref[...] + s * v_ref[...]) * g_ref[...]

    tok = pl.BlockSpec((tr, w), lambda i: (i, 0))
    par = pl.BlockSpec((1, w), lambda i: (0, 0))
    return pl.pallas_call(
        body, name="gn_bonus_fwd", grid=(rows // tr,),
        in_specs=[tok] * 5 + [par] * 3, out_specs=tok,
        out_shape=jax.ShapeDtypeStruct((rows, w), F32),
        compiler_params=pltpu.CompilerParams(dimension_semantics=("parallel",)),
    )(y, r, kf, v, g, gw, gb, rk)


def _gn_bwd_call(y, r, kf, v, g, gw, gb, rk, do):
    rows, w = y.shape
    tr = _row_tile(rows, w, budget=512 * 1024)

    def body(y_ref, r_ref, kf_ref, v_ref, g_ref, gw_ref, gb_ref, rk_ref, do_ref,
             dy_ref, dr_ref, dkf_ref, dv_ref, dg_ref, dgw_ref, dgb_ref, drk_ref):
        @pl.when(pl.program_id(0) == 0)
        def _():
            dgw_ref[...] = jnp.zeros_like(dgw_ref)
            dgb_ref[...] = jnp.zeros_like(dgb_ref)
            drk_ref[...] = jnp.zeros_like(drk_ref)

        yv, rv, kv, vv, rkv = y_ref[...], r_ref[...], kf_ref[...], v_ref[...], rk_ref[...]
        mean = lambda t: _head_sums(t) * (1.0 / HEAD_DIM)
        yc = yv - mean(yv)
        rstd = lax.rsqrt(mean(yc * yc) + GN_EPS)
        yhat = yc * rstd
        s = _head_sums(rv * kv * rkv)
        dg_ref[...] = do_ref[...] * (yhat * gw_ref[...] + gb_ref[...] + s * vv)
        dov = do_ref[...] * g_ref[...]
        dyhat = dov * gw_ref[...]
        dy_ref[...] = rstd * (dyhat - mean(dyhat) - yhat * mean(dyhat * yhat))
        ds = _head_sums(dov * vv)
        dv_ref[...] = s * dov
        dr_ref[...] = ds * kv * rkv
        dkf_ref[...] = ds * rv * rkv
        dgw_ref[...] += jnp.sum(dov * yhat, axis=0, keepdims=True)
        dgb_ref[...] += jnp.sum(dov, axis=0, keepdims=True)
        drk_ref[...] += jnp.sum(ds * rv * kv, axis=0, keepdims=True)

    tok = pl.BlockSpec((tr, w), lambda i: (i, 0))
    par = pl.BlockSpec((1, w), lambda i: (0, 0))
    tshape = jax.ShapeDtypeStruct((rows, w), F32)
    pshape = jax.ShapeDtypeStruct((1, w), F32)
    return pl.pallas_call(
        body, name="gn_bonus_bwd", grid=(rows // tr,),
        in_specs=[tok] * 5 + [par] * 3 + [tok], out_specs=[tok] * 5 + [par] * 3,
        out_shape=[tshape] * 5 + [pshape] * 3,
        compiler_params=pltpu.CompilerParams(dimension_semantics=("arbitrary",)),
    )(y, r, kf, v, g, gw, gb, rk, do)


@jax.custom_vjp
def gn_bonus(y, r, kf, v, g, gw, gb, rk):
    return _gn_fwd_call(y, r, kf, v, g, gw, gb, rk)


def _gn_bwd(res, do):
    return tuple(_gn_bwd_call(*res, do))


gn_bonus.defvjp(lambda *a: (_gn_fwd_call(*a), a), _gn_bwd)


PREP_ROWS = 128


def _prep_segments(rw, lora_w, lora_a, lora_g):
    at = 3 * rw
    seg = {"r": (0, rw), "k": (rw, 2 * rw), "v": (2 * rw, 3 * rw)}
    for name, n in (("wd", lora_w), ("ad", lora_a), ("gd", lora_g)):
        seg[name] = (at, at + _pad128(n))
        at += _pad128(n)
    return seg, at


def _prep_shifted(z_ref, zlast_ref, mu_ref, seg, first_tile):
    lo, hi = seg
    zr = z_ref[:, lo:hi]
    rows = zr.shape[0]
    before = jnp.where(first_tile, 0.0, zlast_ref[7:8, lo:hi])
    row0 = lax.broadcasted_iota(jnp.int32, zr.shape, 0) == 0
    diff = jnp.where(row0, before, pltpu.roll(zr, 1, axis=0)) - zr
    return zr + diff * mu_ref[:, lo:hi], diff


def _prep_forward_values(z_ref, zlast_ref, mu_ref, w0_ref, a0_ref, kk_ref, ka_ref, w2_ref, a2_ref, g2_ref, segs, first_tile):
    z = {n: _prep_shifted(z_ref, zlast_ref, mu_ref, segs[n], first_tile) for n in segs}
    r, k, v, wd, ad, gd = (z[n][0] for n in ("r", "k", "v", "wd", "ad", "gd"))
    twd = jnp.tanh(wd)
    pw = _mm(twd, w2_ref[...]) + w0_ref[...]
    lw = -jnp.exp(-(jnp.maximum(-pw, 0.0) + jnp.log(1.0 + jnp.exp(-jnp.abs(pw)))) - 0.5)
    a_sig = 1.0 / (1.0 + jnp.exp(-(_mm(ad, a2_ref[...]) + a0_ref[...])))
    sg = 1.0 / (1.0 + jnp.exp(-gd))
    kx = k * kk_ref[...]
    nrm = jnp.sqrt(_head_sums(kx * kx))
    inv = 1.0 / jnp.maximum(nrm, L2_FLOOR)
    return dict(z=z, r=r, k=k, v=v, twd=twd, pw=pw, lw=lw, a_sig=a_sig, sg=sg, ad=ad, kk=kx * inv, inv=inv, live=nrm > L2_FLOOR)


def _prep_specs(tokens, rpad, rw, w2, a2, g2):
    tr = PREP_ROWS
    tile = lambda w: pl.BlockSpec((tr, w), lambda i: (i, 0))
    before = pl.BlockSpec((8, rpad), lambda i: (jnp.maximum(i * (tr // 8) - 1, 0), 0))
    whole = lambda a: pl.BlockSpec(a.shape, lambda i: (0, 0))
    par = pl.BlockSpec((1, rw), lambda i: (0, 0))
    return tile, before, whole, par, pl.BlockSpec((1, rpad), lambda i: (0, 0))


def _prep_fwd_call(zr, mu, w0, a0, k_k, k_a, w2, a2, g2):
    tokens, rpad = zr.shape
    rw = w0.shape[1]
    segs, _ = _prep_segments(rw, w2.shape[0], a2.shape[0], g2.shape[0])
    tile, before, whole, par, mu_spec = _prep_specs(tokens, rpad, rw, w2, a2, g2)

    def body(z_ref, zlast_ref, mu_ref, w0_ref, a0_ref, kk_ref, ka_ref, w2_ref, a2_ref, g2_ref,
             r_ref, lw_ref, kf_ref, v_ref, na_ref, b_ref, g_ref):
        f = _prep_forward_values(z_ref, zlast_ref, mu_ref, w0_ref, a0_ref, kk_ref, ka_ref, w2_ref, a2_ref, g2_ref,
                                 segs, pl.program_id(0) == 0)
        r_ref[...] = f["r"]
        v_ref[...] = f["v"]
        lw_ref[...] = f["lw"]
        kf_ref[...] = f["k"] * (1.0 + (f["a_sig"] - 1.0) * ka_ref[...])
        na_ref[...] = -f["kk"]
        b_ref[...] = f["kk"] * f["a_sig"]
        g_ref[...] = _mm(f["sg"], g2_ref[...])

    shape = jax.ShapeDtypeStruct((tokens, rw), F32)
    return pl.pallas_call(
        body, name="rwkv_prep_fwd", grid=(tokens // PREP_ROWS,),
        in_specs=[tile(rpad), before, mu_spec, par, par, par, par, whole(w2), whole(a2), whole(g2)],
        out_specs=[tile(rw)] * 7, out_shape=[shape] * 7,
        compiler_params=pltpu.CompilerParams(dimension_semantics=("parallel",), vmem_limit_bytes=VMEM_LIMIT_CAP),
    )(zr, zr, mu, w0, a0, k_k, k_a, w2, a2, g2)


def _prep_bwd_call(zr, mu, w0, a0, k_k, k_a, w2, a2, g2, cts):
    tokens, rpad = zr.shape
    rw = w0.shape[1]
    segs, _ = _prep_segments(rw, w2.shape[0], a2.shape[0], g2.shape[0])
    tile, before, whole, par, mu_spec = _prep_specs(tokens, rpad, rw, w2, a2, g2)
    nt = tokens // PREP_ROWS
    rev = lambda spec: pl.BlockSpec(spec.block_shape, lambda i, f=spec.index_map: f(nt - 1 - i))

    def body(z_ref, zlast_ref, mu_ref, w0_ref, a0_ref, kk_ref, ka_ref, w2_ref, a2_ref, g2_ref,
             dr_ref, dlw_ref, dkf_ref, dv_ref, dna_ref, db_ref, dg_ref,
             dz_ref, dmu_ref, dw0_ref, da0_ref, dkk_ref, dka_ref, dw2_ref, da2_ref, dg2_ref, carry):
        step = pl.program_id(0)

        @pl.when(step == 0)
        def _():
            for ref in (dmu_ref, dw0_ref, da0_ref, dkk_ref, dka_ref, dw2_ref, da2_ref, dg2_ref, carry):
                ref[...] = jnp.zeros_like(ref)

        f = _prep_forward_values(z_ref, zlast_ref, mu_ref, w0_ref, a0_ref, kk_ref, ka_ref, w2_ref, a2_ref, g2_ref,
                                 segs, step == nt - 1)
        k, kk, a_sig, sg, twd = f["k"], f["kk"], f["a_sig"], f["sg"], f["twd"]
        colsum = lambda t: jnp.sum(t, axis=0, keepdims=True)
        dkf, db, dg = dkf_ref[...], db_ref[...], dg_ref[...]
        ka = ka_ref[...]
        dgd = _mm(dg, g2_ref[...], tb=True) * sg * (1.0 - sg)
        dg2_ref[...] += _mm(sg, dg, ta=True)
        dkk = db * a_sig - dna_ref[...]
        da_sig = db * kk + dkf * k * ka
        dk = dkf * (1.0 + (a_sig - 1.0) * ka)
        dka_ref[...] += colsum(dkf * k * (a_sig - 1.0))
        along = jnp.where(f["live"], _head_sums(dkk * kk), 0.0)
        dkx = (dkk - kk * along) * f["inv"]
        dk = dk + dkx * kk_ref[...]
        dkk_ref[...] += colsum(dkx * k)
        dpa = da_sig * a_sig * (1.0 - a_sig)
        da0_ref[...] += colsum(dpa)
        dad = _mm(dpa, a2_ref[...], tb=True)
        da2_ref[...] += _mm(f["ad"], dpa, ta=True)
        dpw = dlw_ref[...] * f["lw"] / (1.0 + jnp.exp(f["pw"]))
        dw0_ref[...] += colsum(dpw)
        dwd = _mm(dpw, w2_ref[...], tb=True) * (1.0 - twd * twd)
        dw2_ref[...] += _mm(twd, dpw, ta=True)
        rows = PREP_ROWS
        last = lax.broadcasted_iota(jnp.int32, (rows, 1), 0) == rows - 1
        for name, dz in (("r", dr_ref[...]), ("k", dk), ("v", dv_ref[...]), ("wd", dwd), ("ad", dad), ("gd", dgd)):
            lo, hi = segs[name]
            mu_s = mu_ref[:, lo:hi]
            dmu_ref[:, lo:hi] += colsum(dz * f["z"][name][1])
            later = dz * mu_s
            dz_ref[:, lo:hi] = dz * (1.0 - mu_s) + jnp.where(last, carry[:, lo:hi], pltpu.roll(later, rows - 1, axis=0))
            carry[:, lo:hi] = later[0:1, :]

    tok = jax.ShapeDtypeStruct((tokens, rw), F32)
    acc = lambda a: jax.ShapeDtypeStruct(a.shape, F32)
    return pl.pallas_call(
        body, name="rwkv_prep_bwd", grid=(nt,),
        in_specs=[rev(tile(rpad)), rev(before), mu_spec, par, par, par, par, whole(w2), whole(a2), whole(g2)]
                 + [rev(tile(rw))] * 7,
        out_specs=[rev(tile(rpad)), mu_spec, par, par, par, par, whole(w2), whole(a2), whole(g2)],
        out_shape=[jax.ShapeDtypeStruct((tokens, rpad), F32), acc(mu), acc(w0), acc(a0), acc(k_k), acc(k_a), acc(w2), acc(a2), acc(g2)],
        scratch_shapes=[pltpu.VMEM((1, rpad), F32)],
        compiler_params=pltpu.CompilerParams(dimension_semantics=("arbitrary",), vmem_limit_bytes=VMEM_LIMIT_CAP),
    )(zr, zr, mu, w0, a0, k_k, k_a, w2, a2, g2, *cts)


@jax.custom_vjp
def rwkv_prep(zr, mu, w0, a0, k_k, k_a, w2, a2, g2):
    return tuple(_prep_fwd_call(zr, mu, w0, a0, k_k, k_a, w2, a2, g2))


def _rwkv_prep_bwd(res, cts):
    zr, mu, w0, a0, k_k, k_a, w2, a2, g2 = res
    dz, dmu, dw0, da0, dkk, dka, dw2, da2, dg2 = _prep_bwd_call(*res, cts)
    return dz, dmu, dw0, da0, dkk, dka, dw2.astype(w2.dtype), da2.astype(a2.dtype), dg2.astype(g2.dtype)


rwkv_prep.defvjp(lambda *a: (tuple(_prep_fwd_call(*a)), a), _rwkv_prep_bwd)


def _pair_masks(rows):
    lane = lax.broadcasted_iota(jnp.int32, (rows, PAIR), 1)
    return lane < HEAD_DIM, lane >= HEAD_DIM


def _bd(x):
    m0, m1 = _pair_masks(x.shape[0])
    return jnp.concatenate([jnp.where(m0, x, 0.0), jnp.where(m1, x, 0.0)], axis=0)


def _unbd(m, c):
    return jnp.where(_pair_masks(c)[0], m[:c], m[c:])


def _pair_a(l2, r2):
    return _mm(l2, _bd(r2), tb=True)


def _pair_mul(p2, x2):
    return _mm(p2, _bd(x2))


def _pair_mul_t(p2, x2):
    return _unbd(_mm(p2, x2, ta=True), p2.shape[0])


def _block_diag_mask():
    row = lax.broadcasted_iota(jnp.int32, (PAIR, PAIR), 0)
    lane = lax.broadcasted_iota(jnp.int32, (PAIR, PAIR), 1)
    return (row < HEAD_DIM) == (lane < HEAD_DIM), row == lane


def _wkv_pair_common(r, lw, k, a, b):
    c = r[0].shape[0]
    pairs = range(len(r))
    i = lax.broadcasted_iota(jnp.int32, (c, PAIR), 0)
    j = lax.broadcasted_iota(jnp.int32, (c, PAIR), 1) % c
    strict, incl = i > j, i >= j
    ti = lax.broadcasted_iota(jnp.int32, (c, c), 0)
    tj = lax.broadcasted_iota(jnp.int32, (c, c), 1)
    tri = jnp.where(ti >= tj, 1.0, 0.0).astype(BF16)
    lc = [sum(_dg(tri, part, False, False) for part in _split(lw[p], 3)) for p in pairs]
    lend = [lc[p][c - 1:c, :] for p in pairs]
    rt = [r[p] * jnp.exp(lc[p]) for p in pairs]
    at = [a[p] * jnp.exp(lc[p] - lw[p]) for p in pairs]
    pinv = [jnp.exp(-lc[p]) for p in pairs]
    kt = [k[p] * pinv[p] for p in pairs]
    bt = [b[p] * pinv[p] for p in pairs]
    e = [jnp.exp(lend[p] - lc[p]) for p in pairs]
    ktp = [k[p] * e[p] for p in pairs]
    btp = [b[p] * e[p] for p in pairs]
    a_ab = [jnp.where(strict, _pair_a(at[p], bt[p]), 0.0) for p in pairs]
    a_ak = [jnp.where(strict, _pair_a(at[p], kt[p]), 0.0) for p in pairs]
    a_rb = [jnp.where(incl, _pair_a(rt[p], bt[p]), 0.0) for p in pairs]
    a_rk = [jnp.where(incl, _pair_a(rt[p], kt[p]), 0.0) for p in pairs]
    t = [jnp.where(i == j, 1.0, 0.0) + a_ab[p] for p in pairs]
    xp = a_ab
    n = 2
    while n < c:
        xp = [_pair_mul(xp[p], xp[p]) for p in pairs]
        t = [t[p] + _pair_mul(t[p], xp[p]) for p in pairs]
        n *= 2
    bdm, eye = _block_diag_mask()
    pend_col = [jnp.sum(jnp.where(eye, jnp.exp(lend[p]), 0.0), axis=1, keepdims=True) for p in pairs]
    return dict(rt=rt, at=at, kt=kt, bt=bt, ktp=ktp, btp=btp, a_ak=a_ak, a_rb=a_rb, a_rk=a_rk, t=t,
                pend_col=pend_col, lend=lend, lc=lc, strict=strict, incl=incl, tri=tri, bdm=bdm)


def _wkv_group(width):
    npair = width // PAIR
    g = min(WKV_PAIRS_PER_STEP, npair)
    assert npair % g == 0
    return npair, g


def _wkv_fwd_call(r, lw, k, v, a, b):
    tokens, width = r.shape
    c = WKV_CHUNK
    nc = tokens // c
    npair, g = _wkv_group(width)

    def body(r_ref, lw_ref, k_ref, v_ref, a_ref, b_ref, y_ref, s_ref, st):
        @pl.when(pl.program_id(1) == 0)
        def _():
            st[...] = jnp.zeros_like(st)

        pairs = range(g)
        rv, lwv, kv, vv, av, bv = ([ref[:, p * PAIR:(p + 1) * PAIR] for p in pairs]
                                   for ref in (r_ref, lw_ref, k_ref, v_ref, a_ref, b_ref))
        s0 = [st[p] for p in pairs]
        q = _wkv_pair_common(rv, lwv, kv, av, bv)
        w1 = [_mm(q["at"][p], s0[p]) + _pair_mul(q["a_ak"][p], vv[p]) for p in pairs]
        u = [_pair_mul(q["t"][p], w1[p]) for p in pairs]
        y = [_mm(q["rt"][p], s0[p]) + _pair_mul(q["a_rb"][p], u[p]) + _pair_mul(q["a_rk"][p], vv[p]) for p in pairs]
        grow = [_mm(jnp.concatenate([q["btp"][p], q["ktp"][p]], axis=0), jnp.concatenate([u[p], vv[p]], axis=0), ta=True)
                for p in pairs]
        for p in pairs:
            y_ref[:, p * PAIR:(p + 1) * PAIR] = y[p]
            s_ref[0, p] = s0[p]
            st[p] = q["pend_col"][p] * s0[p] + jnp.where(q["bdm"], grow[p], 0.0)

    tok = pl.BlockSpec((c, g * PAIR), lambda gi, ci: (ci, gi))
    return pl.pallas_call(
        body, name="wkv_fwd", grid=(npair // g, nc),
        in_specs=[tok] * 6,
        out_specs=[tok, pl.BlockSpec((1, g, PAIR, PAIR), lambda gi, ci: (ci, gi, 0, 0))],
        out_shape=[jax.ShapeDtypeStruct((tokens, width), F32), jax.ShapeDtypeStruct((nc, npair, PAIR, PAIR), F32)],
        scratch_shapes=[pltpu.VMEM((g, PAIR, PAIR), F32)],
        compiler_params=pltpu.CompilerParams(dimension_semantics=("parallel", "arbitrary")),
    )(r, lw, k, v, a, b)


def _wkv_bwd_call(r, lw, k, v, a, b, s, dy):
    tokens, width = r.shape
    c = WKV_CHUNK
    nc = tokens // c
    npair, g = _wkv_group(width)

    def body(r_ref, lw_ref, k_ref, v_ref, a_ref, b_ref, s_ref, dy_ref,
             dr_ref, dlw_ref, dk_ref, dv_ref, da_ref, db_ref, dst):
        @pl.when(pl.program_id(1) == 0)
        def _():
            dst[...] = jnp.zeros_like(dst)

        pairs = range(g)
        rv, lwv, kv, vv, av, bv, dyv = ([ref[:, p * PAIR:(p + 1) * PAIR] for p in pairs]
                                        for ref in (r_ref, lw_ref, k_ref, v_ref, a_ref, b_ref, dy_ref))
        s0 = [s_ref[0, p] for p in pairs]
        dsc = [dst[p] for p in pairs]
        q = _wkv_pair_common(rv, lwv, kv, av, bv)
        rt, at, kt, bt, ktp, btp, t = (q[n] for n in ("rt", "at", "kt", "bt", "ktp", "btp", "t"))
        a_ak, a_rb, a_rk, strict, incl = (q[n] for n in ("a_ak", "a_rb", "a_rk", "strict", "incl"))
        w1 = [_mm(at[p], s0[p]) + _pair_mul(a_ak[p], vv[p]) for p in pairs]
        u = [_pair_mul(t[p], w1[p]) for p in pairs]
        du = [_pair_mul_t(a_rb[p], dyv[p]) + _mm(btp[p], dsc[p]) for p in pairs]
        dw1 = [_pair_mul_t(t[p], du[p]) for p in pairs]
        dv = [_pair_mul_t(a_rk[p], dyv[p]) + _mm(ktp[p], dsc[p]) + _pair_mul_t(a_ak[p], dw1[p]) for p in pairs]
        da_ab = [jnp.where(strict, _pair_a(dw1[p], u[p]), 0.0) for p in pairs]
        da_ak = [jnp.where(strict, _pair_a(dw1[p], vv[p]), 0.0) for p in pairs]
        da_rb = [jnp.where(incl, _pair_a(dyv[p], u[p]), 0.0) for p in pairs]
        da_rk = [jnp.where(incl, _pair_a(dyv[p], vv[p]), 0.0) for p in pairs]
        d_rt = [_mm(dyv[p], s0[p], tb=True) + _pair_mul(da_rb[p], bt[p]) + _pair_mul(da_rk[p], kt[p]) for p in pairs]
        d_at = [_mm(dw1[p], s0[p], tb=True) + _pair_mul(da_ab[p], bt[p]) + _pair_mul(da_ak[p], kt[p]) for p in pairs]
        d_bt = [_pair_mul_t(da_ab[p], at[p]) + _pair_mul_t(da_rb[p], rt[p]) for p in pairs]
        d_kt = [_pair_mul_t(da_ak[p], at[p]) + _pair_mul_t(da_rk[p], rt[p]) for p in pairs]
        d_btp = [_mm(u[p], dsc[p], tb=True) for p in pairs]
        d_ktp = [_mm(vv[p], dsc[p], tb=True) for p in pairs]
        ones = jnp.ones((8, PAIR), BF16)
        dpend = [sum(_dg(ones, part, False, True) for part in _split(dsc[p] * s0[p], 3))[0:1, :] * jnp.exp(q["lend"][p])
                 for p in pairs]
        grow = [_mm(jnp.concatenate([rt[p], at[p]], axis=0), jnp.concatenate([dyv[p], dw1[p]], axis=0), ta=True)
                for p in pairs]
        last = lax.broadcasted_iota(jnp.int32, (c, PAIR), 0) == c - 1
        for p in pairs:
            sl = slice(p * PAIR, (p + 1) * PAIR)
            dst[p] = q["pend_col"][p] * dsc[p] + jnp.where(q["bdm"], grow[p], 0.0)
            lc_e = d_ktp[p] * ktp[p] + d_btp[p] * btp[p]
            dlend = jnp.sum(lc_e, axis=0, keepdims=True) + dpend[p]
            dlc = d_rt[p] * rt[p] - d_kt[p] * kt[p] - d_bt[p] * bt[p] - lc_e + jnp.where(last, dlend, 0.0)
            dlp = d_at[p] * at[p]
            dlw_ref[:, sl] = sum(_dg(q["tri"], part, True, False) for part in _split(dlc + dlp, 3)) - dlp
            lc = q["lc"][p]
            pinv = jnp.exp(-lc)
            e = jnp.exp(q["lend"][p] - lc)
            dr_ref[:, sl] = d_rt[p] * jnp.exp(lc)
            da_ref[:, sl] = d_at[p] * jnp.exp(lc - lwv[p])
            dk_ref[:, sl] = d_kt[p] * pinv + d_ktp[p] * e
            db_ref[:, sl] = d_bt[p] * pinv + d_btp[p] * e
            dv_ref[:, sl] = dv[p]

    tok = pl.BlockSpec((c, g * PAIR), lambda gi, ci: (nc - 1 - ci, gi))
    tshape = jax.ShapeDtypeStruct((tokens, width), F32)
    return pl.pallas_call(
        body, name="wkv_bwd", grid=(npair // g, nc),
        in_specs=[tok] * 6 + [pl.BlockSpec((1, g, PAIR, PAIR), lambda gi, ci: (nc - 1 - ci, gi, 0, 0)), tok],
        out_specs=[tok] * 6, out_shape=[tshape] * 6,
        scratch_shapes=[pltpu.VMEM((g, PAIR, PAIR), F32)],
        compiler_params=pltpu.CompilerParams(dimension_semantics=("parallel", "arbitrary")),
    )(r, lw, k, v, a, b, s, dy)


@jax.custom_vjp
def wkv7(r, lw, k, v, a, b):
    return _wkv_fwd_call(r, lw, k, v, a, b)[0]


def _wkv7_fwd(r, lw, k, v, a, b):
    y, s = _wkv_fwd_call(r, lw, k, v, a, b)
    return y, (r, lw, k, v, a, b, s)


wkv7.defvjp(_wkv7_fwd, lambda res, dy: tuple(_wkv_bwd_call(*res, dy)))


def _attn_block(tokens):
    return ATTN_BLOCK_BIG if tokens % ATTN_BLOCK_BIG == 0 else ATTN_BLOCK


def _fox_layouts(cum):
    tokens, heads = cum.shape
    t = _attn_block(tokens)
    cq = cum.reshape(tokens, heads // 2, 2).transpose(1, 0, 2)
    ck = cum.T.reshape(heads // 2, 2, tokens // t, t).transpose(0, 2, 1, 3)
    return cq, ck


def _head_lane_masks(rows):
    lane = lax.broadcasted_iota(jnp.int32, (rows, 2 * HEAD_DIM), 1)
    return [lane < HEAD_DIM, lane >= HEAD_DIM]


def _fox_fwd_call(q, k, v, cq, ck):
    tokens, width = q.shape
    t = _attn_block(tokens)
    nb = tokens // t
    hd = HEAD_DIM
    npair = width // (2 * hd)

    def body(q_ref, k_ref, v_ref, cq_ref, ck_ref, o_ref, lse_ref):
        i = pl.program_id(1)
        masks = _head_lane_masks(t)
        q2 = q_ref[...]
        qs = [jnp.where(mk, q2, 0.0).astype(BF16) for mk in masks]
        cqs = [cq_ref[0, :, hh:hh + 1] for hh in range(2)]

        def block(j, carry, diagonal):
            off = pl.multiple_of(j * t, t)
            ckj = ck_ref[0, j]
            k2 = k_ref[pl.ds(off, t), :].astype(BF16)
            v2 = v_ref[pl.ds(off, t), :].astype(BF16)
            out = []
            for hh in range(2):
                m, l, acc = carry[hh]
                s = _dg(qs[hh], k2, False, True) + (cqs[hh] - ckj[hh:hh + 1, :])
                if diagonal:
                    keep = lax.broadcasted_iota(jnp.int32, (t, t), 0) >= lax.broadcasted_iota(jnp.int32, (t, t), 1)
                    s = jnp.where(keep, s, NEG_BIG)
                m_new = jnp.maximum(m, jnp.max(s, axis=1, keepdims=True))
                alpha = jnp.exp(m - m_new)
                p = jnp.exp(s - m_new)
                l = alpha * l + jnp.sum(p, axis=1, keepdims=True)
                acc = alpha * acc + _dg(p.astype(BF16), v2, False, False)
                out.append((m_new, l, acc))
            return tuple(out)

        init = tuple((jnp.full((t, 1), NEG_BIG, F32), jnp.zeros((t, 1), F32), jnp.zeros((t, 2 * hd), F32)) for _ in range(2))
        res = lax.fori_loop(0, i, lambda j, c: block(j, c, False), init)
        res = block(i, res, True)
        o_ref[...] = jnp.where(masks[0], res[0][2] / res[0][1], res[1][2] / res[1][1])
        for hh in range(2):
            lse_ref[0, :, hh:hh + 1] = res[hh][0] + jnp.log(res[hh][1])

    blk = pl.BlockSpec((t, 2 * hd), lambda hp, i: (i, hp))
    full = pl.BlockSpec((tokens, 2 * hd), lambda hp, i: (0, hp))
    cq_spec = pl.BlockSpec((1, t, 2), lambda hp, i: (hp, i, 0))
    ck_spec = pl.BlockSpec((1, nb, 2, t), lambda hp, i: (hp, 0, 0, 0))
    return pl.pallas_call(
        body, name="fox_fwd", grid=(npair, nb),
        in_specs=[blk, full, full, cq_spec, ck_spec],
        out_specs=[blk, cq_spec],
        out_shape=[jax.ShapeDtypeStruct((tokens, width), F32), jax.ShapeDtypeStruct((npair, tokens, 2), F32)],
        compiler_params=pltpu.CompilerParams(dimension_semantics=("parallel", "arbitrary")),
    )(q, k, v, cq, ck)


def _fox_bwd_call(q, k, v, cq, ck, o, lse, do):
    tokens, width = q.shape
    t = _attn_block(tokens)
    nb = tokens // t
    hd = HEAD_DIM
    npair = width // (2 * hd)

    def body(q_ref, k_ref, v_ref, cq_ref, ck_ref, o_ref, lse_ref, do_ref, dq_ref, dk_ref, dv_ref, dck_ref, dcq_ref):
        i = pl.program_id(1)

        @pl.when(i == 0)
        def _():
            dk_ref[...] = jnp.zeros_like(dk_ref)
            dv_ref[...] = jnp.zeros_like(dv_ref)
            dck_ref[...] = jnp.zeros_like(dck_ref)

        masks = _head_lane_masks(t)
        q2, do2, o2 = q_ref[...], do_ref[...], o_ref[...]
        qs = [jnp.where(mk, q2, 0.0).astype(BF16) for mk in masks]
        dos = [jnp.where(mk, do2, 0.0).astype(BF16) for mk in masks]
        deltas = [jnp.sum(dos[hh].astype(F32) * o2, axis=1, keepdims=True) for hh in range(2)]
        bias = [cq_ref[0, :, hh:hh + 1] - lse_ref[0, :, hh:hh + 1] for hh in range(2)]

        def block(j, carry, diagonal):
            off = pl.multiple_of(j * t, t)
            ckj = ck_ref[0, j]
            k2 = k_ref[pl.ds(off, t), :].astype(BF16)
            v2 = v_ref[pl.ds(off, t), :].astype(BF16)
            out = []
            dk2 = jnp.zeros((t, 2 * hd), F32)
            dv2 = jnp.zeros((t, 2 * hd), F32)
            for hh in range(2):
                s = _dg(qs[hh], k2, False, True) + (bias[hh] - ckj[hh:hh + 1, :])
                if diagonal:
                    keep = lax.broadcasted_iota(jnp.int32, (t, t), 0) >= lax.broadcasted_iota(jnp.int32, (t, t), 1)
                    s = jnp.where(keep, s, NEG_BIG)
                p = jnp.exp(s)
                dp = _dg(dos[hh], v2, False, True)
                ds = p * (dp - deltas[hh])
                dsb = ds.astype(BF16)
                dq, rowsum = carry[hh]
                out.append((dq + _dg(dsb, k2, False, False), rowsum + jnp.sum(ds, axis=1, keepdims=True)))
                dk2 = dk2 + _dg(dsb, qs[hh], True, False)
                dv2 = dv2 + _dg(p.astype(BF16), dos[hh], True, False)
                dck_ref[0, j, hh:hh + 1, :] -= jnp.sum(ds, axis=0, keepdims=True)
            dk_ref[pl.ds(off, t), :] += dk2
            dv_ref[pl.ds(off, t), :] += dv2
            return tuple(out)

        init = tuple((jnp.zeros((t, 2 * hd), F32), jnp.zeros((t, 1), F32)) for _ in range(2))
        res = lax.fori_loop(0, i, lambda j, c: block(j, c, False), init)
        res = block(i, res, True)
        dq_ref[...] = jnp.where(masks[0], res[0][0], res[1][0])
        for hh in range(2):
            dcq_ref[0, :, hh:hh + 1] = res[hh][1]

    blk = pl.BlockSpec((t, 2 * hd), lambda hp, i: (i, hp))
    full = pl.BlockSpec((tokens, 2 * hd), lambda hp, i: (0, hp))
    cq_spec = pl.BlockSpec((1, t, 2), lambda hp, i: (hp, i, 0))
    ck_spec = pl.BlockSpec((1, nb, 2, t), lambda hp, i: (hp, 0, 0, 0))
    tshape = jax.ShapeDtypeStruct((tokens, width), F32)
    return pl.pallas_call(
        body, name="fox_bwd", grid=(npair, nb),
        in_specs=[blk, full, full, cq_spec, ck_spec, blk, cq_spec, blk],
        out_specs=[blk, full, full, ck_spec, cq_spec],
        out_shape=[tshape, tshape, tshape, jax.ShapeDtypeStruct((npair, nb, 2, t), F32),
                   jax.ShapeDtypeStruct((npair, tokens, 2), F32)],
        compiler_params=pltpu.CompilerParams(dimension_semantics=("parallel", "arbitrary")),
    )(q, k, v, cq, ck, o, lse, do)


@jax.custom_vjp
def fox_attention(q, k, v, cum):
    return _fox_fwd(q, k, v, cum)[0]


def _fox_fwd(q, k, v, cum):
    cq, ck = _fox_layouts(cum)
    q, k, v = q.astype(BF16), k.astype(BF16), v.astype(BF16)
    o, lse = _fox_fwd_call(q, k, v, cq, ck)
    return o, (q, k, v, cq, ck, o, lse)


def _fox_bwd(res, do):
    q, k, v, cq, ck, o, lse = res
    dq, dk, dv, dck, dcq = _fox_bwd_call(q, k, v, cq, ck, o, lse, do)
    npair, nb, _, t = dck.shape
    dcum = dck.transpose(0, 2, 1, 3).reshape(2 * npair, nb * t).T + dcq.transpose(1, 0, 2).reshape(nb * t, 2 * npair)
    return dq, dk, dv, dcum


fox_attention.defvjp(_fox_fwd, _fox_bwd)


def _loss_call(y, target):
    rows, d = y.shape
    tr = _row_tile(rows, d)

    def body(y_ref, t_ref, loss_ref, dy_ref):
        @pl.when(pl.program_id(0) == 0)
        def _():
            loss_ref[...] = jnp.zeros_like(loss_ref)

        diff = y_ref[...] - t_ref[...]
        dy_ref[...] = diff * (1.0 / d)
        loss_ref[...] += (0.5 / d) * jnp.sum(jnp.sum(diff * diff, axis=1, keepdims=True), axis=0, keepdims=True)

    return pl.pallas_call(
        body, name="loss", grid=(rows // tr,),
        in_specs=[pl.BlockSpec((tr, d), lambda i: (i, 0))] * 2,
        out_specs=[pl.BlockSpec((1, 1), lambda i: (0, 0)), pl.BlockSpec((tr, d), lambda i: (i, 0))],
        out_shape=[jax.ShapeDtypeStruct((1, 1), F32), jax.ShapeDtypeStruct((rows, d), F32)],
        compiler_params=pltpu.CompilerParams(dimension_semantics=("arbitrary",)),
    )(y, target)


def _adamw_call(w, g, m, v):
    rows, cols = w.shape
    tr = _row_tile_ragged(rows, cols, budget=1024 * 1024)
    c1 = 1.0 / (1.0 - ADAM_B1 ** ADAM_STEP)
    c2 = 1.0 / (1.0 - ADAM_B2 ** ADAM_STEP)

    def body(w_ref, g_ref, m_ref, v_ref, d_ref, nm_ref, nv_ref):
        gv = g_ref[...]
        nm = ADAM_B1 * m_ref[...] + (1.0 - ADAM_B1) * gv
        nv = ADAM_B2 * v_ref[...] + (1.0 - ADAM_B2) * (gv * gv)
        nm_ref[...] = nm
        nv_ref[...] = nv
        d_ref[...] = -ADAM_LR * ((nm * c1) / (jnp.sqrt(nv * c2) + ADAM_EPS) + ADAM_WD * w_ref[...])

    spec = pl.BlockSpec((tr, cols), lambda i: (i, 0))
    shape = jax.ShapeDtypeStruct((rows, cols), F32)
    return pl.pallas_call(
        body, name="adamw", grid=(pl.cdiv(rows, tr),),
        in_specs=[spec] * 4, out_specs=[spec] * 3, out_shape=[shape] * 3,
        compiler_params=pltpu.CompilerParams(dimension_semantics=("parallel",)),
    )(w, g, m, v)


def _my_place():
    return lax.axis_index("x"), lax.axis_index("y"), lax.axis_index("c")


def _place_index(px, py, pc):
    return 4 * px + 2 * py + pc


HBM_SPEC = pl.BlockSpec(memory_space=pltpu.HBM)


def _all_gather_call(block):
    def body(x_ref, out_ref, send_sems, recv_sems, local_sem):
        x, y, c = _my_place()
        me, sibling = (x, y, c), (x, y, 1 - c)
        chips = [(1 - x, y), (x, 1 - y), (1 - x, 1 - y)]

        def slot(px, py, pc):
            return out_ref.at[_place_index(px, py, pc)]

        def copy(k, blk, to, src=None):
            return pltpu.make_async_remote_copy(
                src_ref=slot(*blk) if src is None else src, dst_ref=slot(*blk),
                send_sem=send_sems.at[k], recv_sem=recv_sems.at[k],
                device_id=to, device_id_type=pl.DeviceIdType.MESH)

        mine = pltpu.make_async_copy(x_ref, slot(*me), local_sem)
        mine.start()
        first = [copy(0, me, sibling, src=x_ref)]
        first += [copy(1 + j, me, (*chip, c), src=x_ref) for j, chip in enumerate(chips)]
        for cp in first:
            cp.start()
        passed = [copy(4 + j, (*chip, c), sibling) for j, chip in enumerate(chips)]
        for j, chip in enumerate(chips):
            copy(1 + j, (*chip, c), me).wait_recv()
            passed[j].start()
        copy(0, sibling, me).wait_recv()
        for j, chip in enumerate(chips):
            copy(4 + j, (*chip, 1 - c), me).wait_recv()
        for cp in first + passed:
            cp.wait_send()
        mine.wait()

    return pl.pallas_call(
        body, name="all_gather",
        out_shape=jax.ShapeDtypeStruct((N_DEV,) + block.shape, block.dtype),
        in_specs=[HBM_SPEC], out_specs=HBM_SPEC,
        scratch_shapes=[pltpu.SemaphoreType.DMA((7,)), pltpu.SemaphoreType.DMA((7,)), pltpu.SemaphoreType.DMA],
    )(block)


SEM_SPEC = pl.BlockSpec(memory_space=pltpu.SEMAPHORE)
SIDE_EFFECT = pltpu.SideEffectType.DATAFLOW_SIDE_EFFECTING


def _peers():
    x, y, c = _my_place()
    out = []
    for k in range(1, N_DEV):
        peer = (x ^ (k >> 2), y ^ ((k >> 1) & 1), c ^ (k & 1))
        out.append((k - 1, peer, _place_index(*peer)))
    return _place_index(x, y, c), out


def _spread_start(src, per_peer, name, after=None):
    slot = src.shape[1:] if per_peer else src.shape
    order = () if after is None else (after,)

    def body(src_ref, land_ref, *rest):
        send_sems, recv_sems, src_thru, land_thru, token = rest[len(order):]
        mine, peers = _peers()
        for k, peer, peer_idx in peers:
            pltpu.make_async_remote_copy(
                src_ref=src_ref.at[peer_idx] if per_peer else src_ref, dst_ref=land_ref.at[mine],
                send_sem=send_sems.at[k], recv_sem=recv_sems.at[k],
                device_id=peer, device_id_type=pl.DeviceIdType.MESH).start()
        token[...] = jnp.zeros_like(token)

    return pl.pallas_call(
        body, name=name,
        out_shape=(pltpu.SemaphoreType.DMA((N_DEV - 1,)), pltpu.SemaphoreType.DMA((N_DEV - 1,)),
                   pltpu.HBM(src.shape, src.dtype), pltpu.HBM((N_DEV,) + slot, src.dtype),
                   jax.ShapeDtypeStruct((8, 128), F32)),
        in_specs=(HBM_SPEC, HBM_SPEC) + (pl.BlockSpec(memory_space=pl.ANY),) * len(order),
        out_specs=(SEM_SPEC, SEM_SPEC, HBM_SPEC, HBM_SPEC, pl.BlockSpec(memory_space=pltpu.VMEM)),
        input_output_aliases={0: 2, 1: 3},
        compiler_params=pltpu.CompilerParams(has_side_effects=SIDE_EFFECT),
    )(pltpu.with_memory_space_constraint(src, pltpu.HBM),
      pltpu.with_memory_space_constraint(lax.empty((N_DEV,) + slot, src.dtype), pltpu.HBM), *order)


def _spread_wait(handles, after, per_peer, name):
    send_sems, recv_sems, src_thru, land_thru = handles

    def body(src_ref, land_ref, send_sems, recv_sems, after_ref, src_dead, got_ref):
        _, peers = _peers()
        for k, peer, peer_idx in peers:
            copy = pltpu.make_async_remote_copy(
                src_ref=src_ref.at[peer_idx] if per_peer else src_ref, dst_ref=land_ref.at[peer_idx],
                send_sem=send_sems.at[k], recv_sem=recv_sems.at[k],
                device_id=peer, device_id_type=pl.DeviceIdType.MESH)
            copy.wait_send()
            copy.wait_recv()

    return pl.pallas_call(
        body, name=name,
        out_shape=(pltpu.HBM(src_thru.shape, src_thru.dtype), pltpu.HBM(land_thru.shape, land_thru.dtype)),
        in_specs=(HBM_SPEC, HBM_SPEC, SEM_SPEC, SEM_SPEC, pl.BlockSpec(memory_space=pl.ANY)),
        out_specs=(HBM_SPEC, HBM_SPEC), input_output_aliases={0: 0, 1: 1},
        compiler_params=pltpu.CompilerParams(has_side_effects=SIDE_EFFECT),
    )(src_thru, land_thru, send_sems, recv_sems, after)


def _sum_slots_call(slots):
    _, rows, cols = slots.shape
    tr = _row_tile_ragged(rows, cols, budget=512 * 1024)

    def body(s_ref, o_ref):
        acc = s_ref[0].astype(F32)
        for j in range(1, N_DEV):
            acc = acc + s_ref[j].astype(F32)
        o_ref[...] = acc

    return pl.pallas_call(
        body, name="sum_slots", grid=(pl.cdiv(rows, tr),),
        in_specs=[pl.BlockSpec((N_DEV, tr, cols), lambda i: (0, i, 0))],
        out_specs=pl.BlockSpec((tr, cols), lambda i: (i, 0)),
        out_shape=jax.ShapeDtypeStruct((rows, cols), F32),
        compiler_params=pltpu.CompilerParams(dimension_semantics=("parallel",)),
    )(slots)


def _sum_adamw_call(got, own, w, m, v):
    rows, cols = w.shape
    tr = _row_tile_ragged(rows, cols, budget=512 * 1024)
    c1 = 1.0 / (1.0 - ADAM_B1 ** ADAM_STEP)
    c2 = 1.0 / (1.0 - ADAM_B2 ** ADAM_STEP)

    def body(got_ref, own_ref, w_ref, m_ref, v_ref, g_ref, d_ref, nm_ref, nv_ref):
        mine = _place_index(*_my_place())
        gv = jnp.zeros(w_ref.shape, F32)
        for j in range(N_DEV):
            gv = gv + jnp.where(mine == j, own_ref[...], got_ref[j]).astype(F32)
        nm = ADAM_B1 * m_ref[...] + (1.0 - ADAM_B1) * gv
        nv = ADAM_B2 * v_ref[...] + (1.0 - ADAM_B2) * (gv * gv)
        g_ref[...] = gv
        nm_ref[...] = nm
        nv_ref[...] = nv
        d_ref[...] = -ADAM_LR * ((nm * c1) / (jnp.sqrt(nv * c2) + ADAM_EPS) + ADAM_WD * w_ref[...])

    spec = pl.BlockSpec((tr, cols), lambda i: (i, 0))
    shape = jax.ShapeDtypeStruct((rows, cols), F32)
    return pl.pallas_call(
        body, name="sum_adamw", grid=(pl.cdiv(rows, tr),),
        in_specs=[pl.BlockSpec((N_DEV, tr, cols), lambda i: (0, i, 0))] + [spec] * 4,
        out_specs=[spec] * 4, out_shape=[shape] * 4,
        compiler_params=pltpu.CompilerParams(dimension_semantics=("parallel",)),
    )(got, own, w, m, v)


def _with_own_slot(got, own, mine):
    return lax.dynamic_update_index_in_dim(got, own, mine, 0)


def _pack(vectors, width):
    flat = jnp.concatenate([v.reshape(-1) for v in vectors])
    return jnp.pad(flat, (0, width - flat.shape[0])).reshape(width // 128, 128)


def _unpack(packed, like):
    flat = packed.reshape(-1)
    out, at = [], 0
    for v in like:
        out.append(flat[at:at + v.size].reshape(v.shape))
        at += v.size
    return tuple(out)


def _cols_from_slots(slots):
    n, rows, cols = slots.shape
    return slots.transpose(1, 0, 2).reshape(rows, n * cols)


def _rows_from_slots(slots):
    return slots.reshape(-1, slots.shape[2])


def _pad128(n):
    return -(-n // 128) * 128


def _pad_to_tiles(a, axis):
    n = a.shape[axis]
    pads = [(0, 0)] * a.ndim
    pads[axis] = (0, _pad128(n) - n)
    return jnp.pad(a, pads)


def _rwkv_group(take, zeros, rw, dl, al, gl):
    at = 3 * rw
    parts = take(0, at)
    for n in (dl, al, gl):
        parts += take(at, at + n)
        if _pad128(n) > n:
            parts.append(zeros(_pad128(n) - n))
        at += n
    return parts


def _in_proj_layout(slots, rw, fw, dl, al, gl, whole):
    n_slots, rows, d = slots.shape
    wt = slots.reshape(n_slots * rows, d)
    take = lambda lo, hi: [wt[lo:hi]]
    zeros = lambda n: jnp.zeros((n, d), wt.dtype)
    rcols = 3 * rw + dl + al + gl
    fcols = 3 * fw + fw // HEAD_DIM
    group_r = _rwkv_group(take, zeros, rw, dl, al, gl)
    group_f = take(rcols, rcols + fcols) + ([zeros(_pad128(fcols) - fcols)] if _pad128(fcols) > fcols else [])
    group_g = take(rcols + fcols, n_slots * rows)
    if whole:
        return jnp.concatenate(group_r + group_f + group_g, axis=0)
    return tuple(jnp.concatenate(g, axis=0) for g in (group_r, group_f, group_g))


def _low_rank_layout(slots):
    return _pad_to_tiles(_cols_from_slots(slots), 0)


def _stage_embed(meta, x, n1, lp):
    h0 = jnp.concatenate([meta, x, jnp.zeros((lp - meta.shape[0] - x.shape[0], x.shape[1]), F32)], axis=0)
    return h0, rmsnorm(h0, n1)


def _stage_mix(z_r, z_f, small, w2, a2, g2, dims):
    (mu, w0, a0, k_k, k_a, r_k, gn_w, gn_b, q_g, k_g, f_bias) = small
    rw, fw, dl, al, gl = dims
    fcols = 3 * fw + fw // HEAD_DIM

    mu_group = jnp.concatenate(_rwkv_group(lambda lo, hi: [mu[:, lo:hi]], lambda n: jnp.zeros((1, n), F32), rw, dl, al, gl), axis=1)
    r, lw, kf, v, na, b, g = rwkv_prep(z_r, mu_group, w0, a0, k_k, k_a, w2, a2, g2)
    y = wkv7(r, lw, kf, v, na, b)
    y_a = gn_bonus(y, r, kf, v, g, gn_w, gn_b, r_k.reshape(1, rw))

    fq, fk, fv, fl = z_f[:, :fw], z_f[:, fw:2 * fw], z_f[:, 2 * fw:3 * fw], z_f[:, 3 * fw:fcols]
    fq = head_rms(fq, jnp.tile(q_g, (1, fw // HEAD_DIM)) * (HEAD_DIM ** -0.5))
    fk = head_rms(fk, jnp.tile(k_g, (1, fw // HEAD_DIM)))
    cum = jnp.cumsum(jax.nn.log_sigmoid(badd(fl, f_bias)), axis=0)
    y_b = fox_attention(fq, fk, fv, cum)
    return y_a, y_b


def _stage_merge(h0, y_a, y_b, z_g, w_a, w_b, w_o):
    merged = gated_merge(z_g, dense_cols_bf16(y_a, w_a), dense_cols_bf16(y_b, w_b))
    return dense_add(merged, w_o, h0)


def _stage_ffn(h1, n2, w_gu, w_dn):
    return dense_add(swiglu(dense_cols_bf16(rmsnorm(h1, n2), w_gu)), w_dn, h1)


SHARDED = ("meta_tokens", "w_in", "rwkv_w2", "rwkv_a2", "rwkv_g2", "w_branch_a", "w_branch_b", "w_o", "w_gate_up", "w_down")
LOW_RANK = ("rwkv_w2", "rwkv_a2", "rwkv_g2")
SMALL = ("norm1_g", "rwkv_mu", "rwkv_w0", "rwkv_a0", "rwkv_k_k", "rwkv_k_a", "rwkv_r_k", "rwkv_gn_w", "rwkv_gn_b",
         "fox_q_norm_g", "fox_k_norm_g", "fox_f_bias", "norm2_g")
WEIGHTS = ("meta_tokens", "norm1_g", "w_in", "rwkv_mu", "rwkv_w0", "rwkv_w2", "rwkv_a0", "rwkv_a2", "rwkv_g2", "rwkv_k_k",
           "rwkv_k_a", "rwkv_r_k", "rwkv_gn_w", "rwkv_gn_b", "fox_q_norm_g", "fox_k_norm_g", "fox_f_bias", "w_branch_a",
           "w_branch_b", "w_o", "norm2_g", "w_gate_up", "w_down")


def _as2d(a):
    return a.reshape(-1, a.shape[-1])


def kernel(x, meta_tokens, norm1_g, w_in, rwkv_mu, rwkv_w0, rwkv_w2, rwkv_a0, rwkv_a2, rwkv_g2, rwkv_k_k, rwkv_k_a, rwkv_r_k, rwkv_gn_w, rwkv_gn_b, fox_q_norm_g, fox_k_norm_g, fox_f_bias, w_branch_a, w_branch_b, w_o, norm2_g, w_gate_up, w_down, loss_target, m_meta_tokens, m_norm1_g, m_w_in, m_rwkv_mu, m_rwkv_w0, m_rwkv_w2, m_rwkv_a0, m_rwkv_a2, m_rwkv_g2, m_rwkv_k_k, m_rwkv_k_a, m_rwkv_r_k, m_rwkv_gn_w, m_rwkv_gn_b, m_fox_q_norm_g, m_fox_k_norm_g, m_fox_f_bias, m_w_branch_a, m_w_branch_b, m_w_o, m_norm2_g, m_w_gate_up, m_w_down, v_meta_tokens, v_norm1_g, v_w_in, v_rwkv_mu, v_rwkv_w0, v_rwkv_w2, v_rwkv_a0, v_rwkv_a2, v_rwkv_g2, v_rwkv_k_k, v_rwkv_k_a, v_rwkv_r_k, v_rwkv_gn_w, v_rwkv_gn_b, v_fox_q_norm_g, v_fox_k_norm_g, v_fox_f_bias, v_w_branch_a, v_w_branch_b, v_w_o, v_norm2_g, v_w_gate_up, v_w_down):
    given = dict(locals())
    w = {n: given[n] for n in WEIGHTS}
    assert rwkv_r_k.shape[-1] == HEAD_DIM
    n_meta, seq = meta_tokens.shape[0], x.shape[1]
    tokens = n_meta + seq
    lp = -(-tokens // TOKEN_TILE) * TOKEN_TILE
    mine = _place_index(*(lax.axis_index(a) for a in MESH_AXES))
    x2 = x[0]

    local = {n: _as2d(given[n]) for n in given if n != "x" and n != "loss_target"}
    for n in ("w_in", "m_w_in", "v_w_in"):
        local[n] = jnp.transpose(given[n][0])
    blocks = {n: local[n].astype(F32 if n == "meta_tokens" else BF16) for n in SHARDED}
    for prefix in ("", "m_", "v_"):
        local[prefix + "low_rank"] = jnp.concatenate([local[prefix + n] for n in LOW_RANK], axis=0)
    blocks["low_rank"] = jnp.concatenate([blocks[n] for n in LOW_RANK], axis=0)
    low_rank_ends = [sum(local[n].shape[0] for n in LOW_RANK[:i + 1]) for i in range(len(LOW_RANK))]
    low_rank_rows = lambda a, axis: [lax.slice_in_dim(a, lo, hi, axis=axis) for lo, hi in zip([0] + low_rank_ends, low_rank_ends)]
    first = ("meta_tokens", "low_rank")
    started = {n: _spread_start(blocks[n], False, "gather_start_" + n) for n in first}
    zero = sum(started[n][4][0, 0] for n in first)

    def gathered(n, after):
        own, got = _spread_wait(started[n][:4], after, False, "gather_wait_" + n)
        return _with_own_slot(got, own, mine)

    sm = {n: _as2d(w[n]) for n in SMALL}
    small_mix = tuple(sm[n] for n in SMALL[1:-1])
    n1 = sm["norm1_g"] + zero
    rw, fw = w_branch_a.shape[-2], w_branch_b.shape[-2]
    dims = (rw, fw, rwkv_w2.shape[-2], rwkv_a2.shape[-2], rwkv_g2.shape[-2])
    same = lambda s: (s,)

    meta, un_meta = jax.vjp(_cols_from_slots, gathered("meta_tokens", x2))
    (h0, xn), vjp_embed = jax.vjp(lambda m, xs, g: _stage_embed(m, xs, g, lp), meta, x2, n1)
    in_slots = _all_gather_call(blocks["w_in"])
    later = [n for n in SHARDED if n not in first and n not in LOW_RANK and n != "w_in"]
    started.update({n: _spread_start(blocks[n], False, "gather_start_" + n, after=in_slots) for n in later})
    w_groups = _in_proj_layout(in_slots, *dims, whole=False)
    w_cat, un_in = jax.vjp(lambda s: _in_proj_layout(s, *dims, whole=True), in_slots)
    xn_b = xn.astype(BF16)
    behind = sum(started[n][4] for n in later)
    z_r, z_f, z_g = (_matmul(xn_b, wg, tb=True, name="in_proj_" + tag, after=behind, out_dtype=BF16 if tag == "g" else F32)
                     for wg, tag in zip(w_groups, "rfg"))
    (w2, un_w2), (a2, un_a2), (g2, un_g2) = (jax.vjp(_low_rank_layout, s) for s in low_rank_rows(gathered("low_rank", xn), 1))
    (y_a, y_b), vjp_mix = jax.vjp(lambda zr, zf, s, a, b, c: _stage_mix(zr, zf, s, a, b, c, dims),
                                  z_r, z_f, small_mix, w2, a2, g2)
    w_a, w_b = gathered("w_branch_a", y_a), gathered("w_branch_b", y_a)
    w_o_full, un_wo = jax.vjp(_rows_from_slots, gathered("w_o", y_a))
    h1, vjp_merge = jax.vjp(_stage_merge, h0, y_a, y_b, z_g, w_a, w_b, w_o_full)
    w_gu = gathered("w_gate_up", h1)
    w_dn, un_dn = jax.vjp(_rows_from_slots, gathered("w_down", h1))
    y, vjp_ffn = jax.vjp(_stage_ffn, h1, sm["norm2_g"], w_gu, w_dn)

    loss_part, dy_real = _loss_call(y[n_meta:tokens], loss_target[0])
    dy = jnp.pad(dy_real, ((n_meta, lp - tokens), (0, 0)))
    loss = lax.psum(loss_part[0, 0], MESH_AXES)

    sent = {}

    def send_grad(n, dmat, unlayout):
        sent[n] = _spread_start(unlayout(dmat)[0], True, "grad_start_" + n)
        return sent[n][4][0, 0]

    d_h1, d_n2, d_wgu, d_wdn = vjp_ffn(dy)
    behind = send_grad("w_gate_up", d_wgu, same) + send_grad("w_down", d_wdn, un_dn)
    d_h0, d_ya, d_yb, d_zg, d_wa, d_wb, d_wo = vjp_merge(d_h1 + behind)
    behind = send_grad("w_o", d_wo, un_wo) + send_grad("w_branch_a", d_wa, same) + send_grad("w_branch_b", d_wb, same)
    d_zr, d_zf, d_small_mix, d_w2, d_a2, d_g2 = vjp_mix((d_ya + behind, d_yb))
    dproj_b = jnp.concatenate([d_zr.astype(BF16), d_zf.astype(BF16), d_zg.astype(BF16)], axis=1)
    d_wcat = _matmul(dproj_b, xn_b, ta=True, out_dtype=BF16, name="in_proj_dw")
    send_grad("w_in", d_wcat, un_in)
    d_xn = _matmul(dproj_b, w_cat, out_dtype=BF16, name="in_proj_dx", after=sent["w_in"][4])
    send_grad("low_rank", jnp.concatenate([un_w2(d_w2)[0], un_a2(d_a2)[0], un_g2(d_g2)[0]], axis=1), same)
    d_meta, g_x, d_n1 = vjp_embed((d_h0, d_xn))
    send_grad("meta_tokens", d_meta, un_meta)

    small_grads = (d_n1, *d_small_mix, d_n2)
    n_small = sum(g.size for g in small_grads)
    width = -(-n_small // 1024) * 1024
    small_sent = _spread_start(_pack(small_grads, width), False, "small_grad_start")

    grads, delta, new_m, new_v = {}, {}, {}, {}
    after = g_x
    for n in ("w_gate_up", "w_down", "w_o", "w_branch_a", "w_branch_b", "low_rank", "meta_tokens", "w_in"):
        src, got = _spread_wait(sent[n][:4], after, True, "grad_wait_" + n)
        own = lax.dynamic_index_in_dim(src, mine, 0, keepdims=False)
        stepped = _sum_adamw_call(got, own, local[n], local["m_" + n], local["v_" + n])
        after = stepped[2]
        if n == "low_rank":
            for out, t in zip((grads, delta, new_m, new_v), stepped):
                out.update({name: part.reshape(w[name].shape) for name, part in zip(LOW_RANK, low_rank_rows(t, 0))})
            continue
        back = (lambda t: jnp.transpose(t)[None]) if n == "w_in" else (lambda t: t.reshape(w[n].shape))
        grads[n], delta[n], new_m[n], new_v[n] = (back(t) for t in stepped)
    own_small, got_small = _spread_wait(small_sent[:4], after, False, "small_grad_wait")
    small_total = _unpack(_sum_slots_call(_with_own_slot(got_small, own_small, mine)), small_grads)
    grads.update({n: g.reshape(w[n].shape) for n, g in zip(SMALL, small_total)})
    packs = [_pack([src[n] if p == "" else given[p + n] for n in SMALL], width)
             for p, src in (("", w), ("", grads), ("m_", None), ("v_", None))]
    like = [w[n] for n in SMALL]
    for out, packed in zip((delta, new_m, new_v), _adamw_call(*packs)):
        out.update(dict(zip(SMALL, _unpack(packed, like))))

    return (loss, g_x[None], *[grads[n] for n in WEIGHTS], *[delta[n] for n in WEIGHTS],
            *[new_m[n] for n in WEIGHTS], *[new_v[n] for n in WEIGHTS])
```

```python
import jax
import jax.numpy as jnp
from jax import lax
from jax.experimental import pallas as pl
from jax.experimental.pallas import tpu as pltpu

F32 = jnp.float32
BF16 = jnp.bfloat16

N_DEV = 8
MESH_AXES = ("x", "y", "c")
HEAD_DIM = 64
TOKEN_TILE = 128
WKV_CHUNK = 64
WKV_PAIRS_PER_STEP = 8
PAIR = 2 * HEAD_DIM
ATTN_BLOCK = 128
ATTN_BLOCK_BIG = 384
RMS_EPS = 1e-6
GN_EPS = 64e-5
L2_FLOOR = 1e-12
NEG_BIG = -1e30
ADAM_LR, ADAM_B1, ADAM_B2, ADAM_EPS, ADAM_WD, ADAM_STEP = 0.001, 0.9, 0.999, 1e-08, 0.01, 10
VMEM_LIMIT_CAP = 56 * 1024 * 1024
VMEM_LIMIT_FLOOR = 32 * 1024 * 1024
MATMUL_VMEM_BUDGET = 36 * 1024 * 1024
GRID_STEP_BYTES = 1024 * 1024
ACC_BYTES_PER_HBM_BYTE = 6


def _vmem_limit(estimate_bytes):
    return int(min(max(estimate_bytes * 5 // 4, VMEM_LIMIT_FLOOR), VMEM_LIMIT_CAP))


def _row_tile(rows, width, itemsize=4, budget=2 * 1024 * 1024):
    for c in (1408, 1024, 704, 512, 384, 256, 128, 64, 32, 16, 8):
        if rows % c == 0 and c * width * itemsize <= budget:
            return c
    return rows


def _row_tile_ragged(rows, width, itemsize=4, budget=2 * 1024 * 1024):
    tile = _row_tile(rows, width, itemsize, budget)
    if tile * width * itemsize <= budget or rows < 16:
        return tile
    padded = -(-rows // 16) * 16
    for c in (1408, 1024, 704, 512, 384, 336, 256, 192, 128, 96, 64, 48, 32, 16):
        if padded % c == 0 and c * width * itemsize <= budget:
            return c
    return tile


def _dg(a, b, ta, tb):
    dims = (((0 if ta else 1,), (1 if tb else 0,)), ((), ()))
    return lax.dot_general(a, b, dims, preferred_element_type=F32)


def _split(x, n):
    parts = []
    for _ in range(n):
        h = x.astype(BF16)
        parts.append(h)
        x = x - h.astype(F32)
    return parts


def _mm(a, b, ta=False, tb=False):
    return _dg(a.astype(BF16), b.astype(BF16), ta, tb)


def _matmul(a, b, ta=False, tb=False, out_dtype=F32, name="matmul", after=None, b_slots=False, out_slots=0, add=None):
    if ta:
        kdim, m = a.shape
    else:
        m, kdim = a.shape
    if b_slots:
        n_slots, brows, bcols = b.shape
        n, k2 = (brows, n_slots * bcols) if tb else (n_slots * bcols, brows)
    elif tb:
        n, k2 = b.shape
    else:
        k2, n = b.shape
    assert kdim == k2, (a.shape, b.shape, ta, tb)
    sa, sb, so = a.dtype.itemsize, b.dtype.itemsize, jnp.dtype(out_dtype).itemsize
    n_unit = bcols if (b_slots and not tb) else (n // out_slots if out_slots else n)
    k_unit = bcols if (b_slots and tb) else kdim
    tm, tn, tk, n_outer = _matmul_tiles(m, n, kdim, ta, sa, sb, so, n_unit, k_unit)
    nk = kdim // tk
    ij = (lambda f: lambda j, i, k: f(i, j, k)) if n_outer else (lambda f: f)

    order = () if after is None else (after,)
    extra = () if add is None else (add,)

    def body(a_ref, b_ref, *rest):
        rest = rest[len(order):]
        add_ref = rest[0] if extra else None
        o_ref, acc = rest[len(extra)], rest[len(extra) + 1:]
        part = _dg(a_ref[...].astype(BF16), b_ref[...].astype(BF16), ta, tb)
        done = lambda total: (total if add_ref is None else total + add_ref[...]).astype(o_ref.dtype)
        if nk == 1:
            o_ref[...] = done(part)
            return
        kk = pl.program_id(2)

        @pl.when(kk == 0)
        def _():
            acc[0][...] = part

        @pl.when(kk > 0)
        def _():
            acc[0][...] += part

        @pl.when(kk == nk - 1)
        def _():
            o_ref[...] = done(acc[0][...])

    a_spec = pl.BlockSpec((tk, tm), ij(lambda i, j, k: (k, i))) if ta else pl.BlockSpec((tm, tk), ij(lambda i, j, k: (i, k)))
    if b_slots and tb:
        per = bcols // tk
        b_spec = pl.BlockSpec((None, tn, tk), ij(lambda i, j, k: (k // per, j, k % per)))
    elif b_slots:
        per = bcols // tn
        b_spec = pl.BlockSpec((None, tk, tn), ij(lambda i, j, k: (j // per, k, j % per)))
    elif tb:
        b_spec = pl.BlockSpec((tn, tk), ij(lambda i, j, k: (j, k)))
    else:
        b_spec = pl.BlockSpec((tk, tn), ij(lambda i, j, k: (k, j)))
    if out_slots:
        per_out = n // out_slots // tn
        out_spec = pl.BlockSpec((None, tm, tn), ij(lambda i, j, k: (j // per_out, i, j % per_out)))
        out_shape = jax.ShapeDtypeStruct((out_slots, m, n // out_slots), out_dtype)
    else:
        out_spec = pl.BlockSpec((tm, tn), ij(lambda i, j, k: (i, j)))
        out_shape = jax.ShapeDtypeStruct((m, n), out_dtype)
    return pl.pallas_call(
        body, name=name,
        grid=(n // tn, m // tm, nk) if n_outer else (m // tm, n // tn, nk),
        in_specs=[a_spec, b_spec] + [pl.BlockSpec(memory_space=pl.ANY)] * len(order)
                 + [pl.BlockSpec((tm, tn), ij(lambda i, j, k: (i, j)))] * len(extra),
        out_specs=out_spec,
        out_shape=out_shape,
        scratch_shapes=[pltpu.VMEM((tm, tn), F32)] if nk > 1 else [],
        compiler_params=pltpu.CompilerParams(
            dimension_semantics=("parallel", "parallel", "arbitrary"),
            vmem_limit_bytes=_vmem_limit(_matmul_vmem(tm, tn, tk, nk, sa, sb, so) + 2 * tm * tn * 4 * len(extra))),
    )(a, b, *order, *extra)


def _matmul_vmem(tm, tn, tk, nk, sa, sb, so):
    return 2 * (tm * tk * sa + tk * tn * sb + tm * tn * so) + tm * tn * 4 + (tm * tn * 4 if nk > 1 else 0)


def _matmul_tiles(m, n, kdim, ta, sa, sb, so, n_unit, k_unit):
    lane = (2816, 2176, 2048, 1408, 1024, 640, 512, 384, 256, 128)
    sublane = (2816, 2176, 2048, 1408, 1024, 704, 512, 384, 256, 128)
    divs = lambda dim, cands: [c for c in cands if dim % c == 0] or [dim]
    best = None
    for tm in divs(m, lane if ta else sublane):
        for tn in divs(n_unit, lane):
            for tk in divs(k_unit, sublane if ta else lane) + ([kdim] if k_unit == kdim and (ta or kdim <= 2048) else []):
                nk, nm, nn = kdim // tk, m // tm, n // tn
                if _matmul_vmem(tm, tn, tk, nk, sa, sb, so) > MATMUL_VMEM_BUDGET:
                    continue
                acc_bytes = m * n * 4 * 3 * nk // ACC_BYTES_PER_HBM_BYTE if nk > 1 else 0
                fixed = m * n * so + acc_bytes + nm * nn * nk * GRID_STEP_BYTES
                for n_outer in (False, True):
                    if n_outer:
                        a_reads, b_reads = (1 if (nk == 1 and nm == 1) else nn), (1 if nk == 1 else nm)
                    else:
                        a_reads, b_reads = (1 if nk == 1 else nn), (1 if (nk == 1 and nn == 1) else nm)
                    cost = m * kdim * sa * a_reads + kdim * n * sb * b_reads + fixed
                    if best is None or cost < best[0]:
                        best = (cost, tm, tn, tk, n_outer)
    return best[1:]


@jax.custom_vjp
def dense(x, w):
    return _matmul(x.astype(BF16), w, name="dense_fwd")


def _dense_fwd(x, w):
    return _matmul(x.astype(BF16), w, name="dense_fwd"), (x.astype(BF16), w, jnp.zeros((), x.dtype))


def _dense_bwd(res, dy):
    xb, w, like = res
    dyb = dy.astype(BF16)
    dx = _matmul(dyb, w, tb=True, out_dtype=like.dtype, name="dense_dx")
    dw = _matmul(xb, dyb, ta=True, out_dtype=w.dtype, name="dense_dw")
    return dx, dw


dense.defvjp(_dense_fwd, _dense_bwd)


@jax.custom_vjp
def dense_add(x, w, res):
    return _matmul(x.astype(BF16), w, name="dense_add_fwd", add=res)


def _dense_add_fwd(x, w, res):
    return _matmul(x.astype(BF16), w, name="dense_add_fwd", add=res), (x.astype(BF16), w, jnp.zeros((), x.dtype))


def _dense_add_bwd(res, dy):
    return (*_dense_bwd(res, dy), dy)


dense_add.defvjp(_dense_add_fwd, _dense_add_bwd)


def _make_dense_cols(out_dtype):
    @jax.custom_vjp
    def op(x, w_slots):
        return _matmul(x.astype(BF16), w_slots, b_slots=True, out_dtype=out_dtype, name="dense_cols_fwd")

    def fwd(x, w_slots):
        xb = x.astype(BF16)
        return (_matmul(xb, w_slots, b_slots=True, out_dtype=out_dtype, name="dense_cols_fwd"),
                (xb, w_slots, jnp.zeros((), x.dtype)))

    def bwd(res, dy):
        xb, w_slots, like = res
        dyb = dy.astype(BF16)
        dx = _matmul(dyb, w_slots, tb=True, b_slots=True, out_dtype=like.dtype, name="dense_cols_dx")
        dw = _matmul(xb, dyb, ta=True, out_slots=w_slots.shape[0], out_dtype=w_slots.dtype, name="dense_cols_dw")
        return dx, dw

    op.defvjp(fwd, bwd)
    return op


dense_cols_bf16 = _make_dense_cols(BF16)


def _swiglu_call(gu, d_act=None):
    rows, two_f = gu.shape
    f = two_f // 2
    tr = _row_tile(rows, two_f, itemsize=2, budget=3 * 1024 * 1024)
    half = lambda j: pl.BlockSpec((tr, f), lambda i, j=j: (i, j))
    ops = (gu, gu) if d_act is None else (gu, gu, d_act)

    def body(*refs):
        g, u = refs[0][...].astype(F32), refs[1][...].astype(F32)
        s = 1.0 / (1.0 + jnp.exp(-g))
        if d_act is None:
            refs[2][...] = (g * s * u).astype(BF16)
        else:
            d = refs[2][...].astype(F32)
            refs[3][:, :f] = (d * u * s * (1.0 + g * (1.0 - s))).astype(BF16)
            refs[3][:, f:] = (d * g * s).astype(BF16)

    width = f if d_act is None else two_f
    return pl.pallas_call(
        body, name="swiglu_fwd" if d_act is None else "swiglu_bwd", grid=(rows // tr,),
        in_specs=[half(0), half(1)] + ([half(0)] if d_act is not None else []),
        out_specs=pl.BlockSpec((tr, width), lambda i: (i, 0)),
        out_shape=jax.ShapeDtypeStruct((rows, width), BF16),
        compiler_params=pltpu.CompilerParams(dimension_semantics=("parallel",)),
    )(*ops)


@jax.custom_vjp
def swiglu(gu):
    return _swiglu_call(gu)


swiglu.defvjp(lambda gu: (_swiglu_call(gu), gu), lambda gu, d_act: (_swiglu_call(gu, d_act),))


def _merge_call(zg, a, b, dm=None):
    rows, d = a.shape
    tr = _row_tile(rows, d, budget=1024 * 1024)
    half = lambda j: pl.BlockSpec((tr, d), lambda i, j=j: (i, j))
    tile = half(0)

    def body(*refs):
        ga = 1.0 / (1.0 + jnp.exp(-refs[0][...].astype(F32)))
        gb = 1.0 / (1.0 + jnp.exp(-refs[1][...].astype(F32)))
        av, bv = refs[2][...].astype(F32), refs[3][...].astype(F32)
        if dm is None:
            refs[4][...] = (ga * av + gb * bv).astype(BF16)
        else:
            dv = refs[4][...].astype(F32)
            dzg_ref, da_ref, db_ref = refs[5:]
            dzg_ref[:, :d] = (dv * av * ga * (1.0 - ga)).astype(dzg_ref.dtype)
            dzg_ref[:, d:] = (dv * bv * gb * (1.0 - gb)).astype(dzg_ref.dtype)
            da_ref[...] = (dv * ga).astype(BF16)
            db_ref[...] = (dv * gb).astype(BF16)

    shape_b = jax.ShapeDtypeStruct((rows, d), BF16)
    if dm is None:
        out_specs, out_shape, ops = tile, shape_b, (zg, zg, a, b)
    else:
        out_specs = [pl.BlockSpec((tr, 2 * d), lambda i: (i, 0)), tile, tile]
        out_shape = [jax.ShapeDtypeStruct((rows, 2 * d), zg.dtype), shape_b, shape_b]
        ops = (zg, zg, a, b, dm)
    return pl.pallas_call(
        body, name="merge_fwd" if dm is None else "merge_bwd", grid=(rows // tr,),
        in_specs=[half(0), half(1)] + [tile] * (len(ops) - 2),
        out_specs=out_specs, out_shape=out_shape,
        compiler_params=pltpu.CompilerParams(dimension_semantics=("parallel",)),
    )(*ops)


@jax.custom_vjp
def gated_merge(zg, a, b):
    return _merge_call(zg, a, b)


gated_merge.defvjp(lambda zg, a, b: (_merge_call(zg, a, b), (zg, a, b)),
                   lambda res, dm: tuple(_merge_call(*res, dm)))


def _rms_fwd_call(x, g):
    rows, d = x.shape
    tr = _row_tile(rows, d)

    def body(x_ref, g_ref, y_ref):
        xv = x_ref[...]
        rstd = lax.rsqrt(jnp.mean(xv * xv, axis=1, keepdims=True) + RMS_EPS)
        y_ref[...] = ((xv * rstd) * g_ref[...]).astype(BF16)

    return pl.pallas_call(
        body, name="rms_fwd", grid=(rows // tr,),
        in_specs=[pl.BlockSpec((tr, d), lambda i: (i, 0)), pl.BlockSpec((1, d), lambda i: (0, 0))],
        out_specs=pl.BlockSpec((tr, d), lambda i: (i, 0)),
        out_shape=jax.ShapeDtypeStruct((rows, d), BF16),
        compiler_params=pltpu.CompilerParams(dimension_semantics=("parallel",)),
    )(x, g)


def _rms_bwd_call(x, g, dy):
    rows, d = x.shape
    tr = _row_tile(rows, d)

    def body(x_ref, g_ref, dy_ref, dx_ref, dg_ref):
        @pl.when(pl.program_id(0) == 0)
        def _():
            dg_ref[...] = jnp.zeros_like(dg_ref)

        xv = x_ref[...]
        dyv = dy_ref[...].astype(F32)
        rstd = lax.rsqrt(jnp.mean(xv * xv, axis=1, keepdims=True) + RMS_EPS)
        xhat = xv * rstd
        dxhat = dyv * g_ref[...]
        dx_ref[...] = rstd * (dxhat - xhat * jnp.mean(dxhat * xhat, axis=1, keepdims=True))
        dg_ref[...] += jnp.sum(dyv * xhat, axis=0, keepdims=True)

    return pl.pallas_call(
        body, name="rms_bwd", grid=(rows // tr,),
        in_specs=[pl.BlockSpec((tr, d), lambda i: (i, 0)), pl.BlockSpec((1, d), lambda i: (0, 0)),
                  pl.BlockSpec((tr, d), lambda i: (i, 0))],
        out_specs=[pl.BlockSpec((tr, d), lambda i: (i, 0)), pl.BlockSpec((1, d), lambda i: (0, 0))],
        out_shape=[jax.ShapeDtypeStruct((rows, d), F32), jax.ShapeDtypeStruct((1, d), F32)],
        compiler_params=pltpu.CompilerParams(dimension_semantics=("arbitrary",)),
    )(x, g, dy)


@jax.custom_vjp
def rmsnorm(x, g):
    return _rms_fwd_call(x, g)


rmsnorm.defvjp(lambda x, g: (_rms_fwd_call(x, g), (x, g)), lambda res, dy: tuple(_rms_bwd_call(res[0], res[1], dy)))


def _bcast_add_call(x, p):
    rows, d = x.shape
    tr = _row_tile(rows, d)

    def body(x_ref, p_ref, y_ref):
        y_ref[...] = x_ref[...] + p_ref[...]

    return pl.pallas_call(
        body, name="bcast_add", grid=(rows // tr,),
        in_specs=[pl.BlockSpec((tr, d), lambda i: (i, 0)), pl.BlockSpec((1, d), lambda i: (0, 0))],
        out_specs=pl.BlockSpec((tr, d), lambda i: (i, 0)),
        out_shape=jax.ShapeDtypeStruct((rows, d), F32),
        compiler_params=pltpu.CompilerParams(dimension_semantics=("parallel",)),
    )(x, p)


def _colsum_call(a):
    rows, d = a.shape
    tr = _row_tile(rows, d)

    def body(a_ref, o_ref):
        @pl.when(pl.program_id(0) == 0)
        def _():
            o_ref[...] = jnp.zeros_like(o_ref)

        o_ref[...] += jnp.sum(a_ref[...], axis=0, keepdims=True)

    return pl.pallas_call(
        body, name="colsum", grid=(rows // tr,),
        in_specs=[pl.BlockSpec((tr, d), lambda i: (i, 0))],
        out_specs=pl.BlockSpec((1, d), lambda i: (0, 0)),
        out_shape=jax.ShapeDtypeStruct((1, d), F32),
        compiler_params=pltpu.CompilerParams(dimension_semantics=("arbitrary",)),
    )(a)


@jax.custom_vjp
def badd(x, p):
    return _bcast_add_call(x, p)


badd.defvjp(lambda x, p: (_bcast_add_call(x, p), None), lambda res, dy: (dy, _colsum_call(dy)))


def _head_sums(x):
    i = lax.broadcasted_iota(jnp.int32, (PAIR, PAIR), 0) // HEAD_DIM
    j = lax.broadcasted_iota(jnp.int32, (PAIR, PAIR), 1) // HEAD_DIM
    ones = jnp.where(i == j, 1.0, 0.0).astype(BF16)
    hi, lo = _split(x, 2)
    cols = [slice(p * PAIR, (p + 1) * PAIR) for p in range(x.shape[1] // PAIR)]
    return jnp.concatenate([_dg(hi[:, c], ones, False, False) + _dg(lo[:, c], ones, False, False) for c in cols], axis=1)


def _head_rms_fwd_call(x, g):
    rows, w = x.shape
    tr = _row_tile(rows, w, budget=1024 * 1024)

    def body(x_ref, g_ref, y_ref):
        xv = x_ref[...]
        rstd = lax.rsqrt(_head_sums(xv * xv) * (1.0 / HEAD_DIM) + RMS_EPS)
        y_ref[...] = (xv * rstd) * g_ref[...]

    return pl.pallas_call(
        body, name="head_rms_fwd", grid=(rows // tr,),
        in_specs=[pl.BlockSpec((tr, w), lambda i: (i, 0)), pl.BlockSpec((1, w), lambda i: (0, 0))],
        out_specs=pl.BlockSpec((tr, w), lambda i: (i, 0)),
        out_shape=jax.ShapeDtypeStruct((rows, w), F32),
        compiler_params=pltpu.CompilerParams(dimension_semantics=("parallel",)),
    )(x, g)


def _head_rms_bwd_call(x, g, dy):
    rows, w = x.shape
    tr = _row_tile(rows, w, budget=1024 * 1024)

    def body(x_ref, g_ref, dy_ref, dx_ref, dg_ref):
        @pl.when(pl.program_id(0) == 0)
        def _():
            dg_ref[...] = jnp.zeros_like(dg_ref)

        xv, dyv = x_ref[...], dy_ref[...]
        rstd = lax.rsqrt(_head_sums(xv * xv) * (1.0 / HEAD_DIM) + RMS_EPS)
        xhat = xv * rstd
        dxhat = dyv * g_ref[...]
        dx_ref[...] = rstd * (dxhat - xhat * (_head_sums(dxhat * xhat) * (1.0 / HEAD_DIM)))
        dg_ref[...] += jnp.sum(dyv * xhat, axis=0, keepdims=True)

    return pl.pallas_call(
        body, name="head_rms_bwd", grid=(rows // tr,),
        in_specs=[pl.BlockSpec((tr, w), lambda i: (i, 0)), pl.BlockSpec((1, w), lambda i: (0, 0)),
                  pl.BlockSpec((tr, w), lambda i: (i, 0))],
        out_specs=[pl.BlockSpec((tr, w), lambda i: (i, 0)), pl.BlockSpec((1, w), lambda i: (0, 0))],
        out_shape=[jax.ShapeDtypeStruct((rows, w), F32), jax.ShapeDtypeStruct((1, w), F32)],
        compiler_params=pltpu.CompilerParams(dimension_semantics=("arbitrary",)),
    )(x, g, dy)


@jax.custom_vjp
def head_rms(x, g):
    return _head_rms_fwd_call(x, g)


head_rms.defvjp(lambda x, g: (_head_rms_fwd_call(x, g), (x, g)),
                lambda res, dy: tuple(_head_rms_bwd_call(res[0], res[1], dy)))


def _gn_fwd_call(y, r, kf, v, g, gw, gb, rk):
    rows, w = y.shape
    tr = _row_tile(rows, w, budget=512 * 1024)

    def body(y_ref, r_ref, kf_ref, v_ref, g_ref, gw_ref, gb_ref, rk_ref, o_ref):
        yv = y_ref[...]
        yc = yv - _head_sums(yv) * (1.0 / HEAD_DIM)
        rstd = lax.rsqrt(_head_sums(yc * yc) * (1.0 / HEAD_DIM) + GN_EPS)
        s = _head_sums(r_ref[...] * kf_ref[...] * rk_ref[...])
        o_ref[...] = (((yc * rstd) * gw_ref[...] + gb_ref[...] + s * v_ref[...]) * g_ref[...]).astype(BF16)

    tok = pl.BlockSpec((tr, w), lambda i: (i, 0))
    par = pl.BlockSpec((1, w), lambda i: (0, 0))
    return pl.pallas_call(
        body, name="gn_bonus_fwd", grid=(rows // tr,),
        in_specs=[tok] * 5 + [par] * 3, out_specs=tok,
        out_shape=jax.ShapeDtypeStruct((rows, w), BF16),
        compiler_params=pltpu.CompilerParams(dimension_semantics=("parallel",)),
    )(y, r, kf, v, g, gw, gb, rk)


def _gn_bwd_call(y, r, kf, v, g, gw, gb, rk, do):
    rows, w = y.shape
    tr = _row_tile(rows, w, budget=512 * 1024)

    def body(y_ref, r_ref, kf_ref, v_ref, g_ref, gw_ref, gb_ref, rk_ref, do_ref,
             dy_ref, dr_ref, dkf_ref, dv_ref, dg_ref, dgw_ref, dgb_ref, drk_ref):
        @pl.when(pl.program_id(0) == 0)
        def _():
            dgw_ref[...] = jnp.zeros_like(dgw_ref)
            dgb_ref[...] = jnp.zeros_like(dgb_ref)
            drk_ref[...] = jnp.zeros_like(drk_ref)

        yv, rv, kv, vv, rkv = y_ref[...], r_ref[...], kf_ref[...], v_ref[...], rk_ref[...]
        mean = lambda t: _head_sums(t) * (1.0 / HEAD_DIM)
        yc = yv - mean(yv)
        rstd = lax.rsqrt(mean(yc * yc) + GN_EPS)
        yhat = yc * rstd
        s = _head_sums(rv * kv * rkv)
        do = do_ref[...].astype(F32)
        dg_ref[...] = do * (yhat * gw_ref[...] + gb_ref[...] + s * vv)
        dov = do * g_ref[...]
        dyhat = dov * gw_ref[...]
        dy_ref[...] = rstd * (dyhat - mean(dyhat) - yhat * mean(dyhat * yhat))
        ds = _head_sums(dov * vv)
        dv_ref[...] = s * dov
        dr_ref[...] = ds * kv * rkv
        dkf_ref[...] = ds * rv * rkv
        dgw_ref[...] += jnp.sum(dov * yhat, axis=0, keepdims=True)
        dgb_ref[...] += jnp.sum(dov, axis=0, keepdims=True)
        drk_ref[...] += jnp.sum(ds * rv * kv, axis=0, keepdims=True)

    tok = pl.BlockSpec((tr, w), lambda i: (i, 0))
    par = pl.BlockSpec((1, w), lambda i: (0, 0))
    tshape = jax.ShapeDtypeStruct((rows, w), F32)
    pshape = jax.ShapeDtypeStruct((1, w), F32)
    return pl.pallas_call(
        body, name="gn_bonus_bwd", grid=(rows // tr,),
        in_specs=[tok] * 5 + [par] * 3 + [tok], out_specs=[tok] * 5 + [par] * 3,
        out_shape=[tshape] * 5 + [pshape] * 3,
        compiler_params=pltpu.CompilerParams(dimension_semantics=("arbitrary",)),
    )(y, r, kf, v, g, gw, gb, rk, do)


@jax.custom_vjp
def gn_bonus(y, r, kf, v, g, gw, gb, rk):
    return _gn_fwd_call(y, r, kf, v, g, gw, gb, rk)


def _gn_bwd(res, do):
    return tuple(_gn_bwd_call(*res, do))


gn_bonus.defvjp(lambda *a: (_gn_fwd_call(*a), a), _gn_bwd)


PREP_ROWS = 128


def _prep_segments(rw, lora_w, lora_a, lora_g):
    at = 3 * rw
    seg = {"r": (0, rw), "k": (rw, 2 * rw), "v": (2 * rw, 3 * rw)}
    for name, n in (("wd", lora_w), ("ad", lora_a), ("gd", lora_g)):
        seg[name] = (at, at + _pad128(n))
        at += _pad128(n)
    return seg, at


def _prep_shifted(z_ref, zlast_ref, mu_ref, seg, first_tile):
    lo, hi = seg
    zr = z_ref[:, lo:hi]
    rows = zr.shape[0]
    before = jnp.where(first_tile, 0.0, zlast_ref[7:8, lo:hi])
    row0 = lax.broadcasted_iota(jnp.int32, zr.shape, 0) == 0
    diff = jnp.where(row0, before, pltpu.roll(zr, 1, axis=0)) - zr
    return zr + diff * mu_ref[:, lo:hi], diff


def _prep_forward_values(z_ref, zlast_ref, mu_ref, w0_ref, a0_ref, kk_ref, ka_ref, w2_ref, a2_ref, g2_ref, segs, first_tile):
    z = {n: _prep_shifted(z_ref, zlast_ref, mu_ref, segs[n], first_tile) for n in segs}
    r, k, v, wd, ad, gd = (z[n][0] for n in ("r", "k", "v", "wd", "ad", "gd"))
    twd = jnp.tanh(wd)
    pw = _mm(twd, w2_ref[...]) + w0_ref[...]
    lw = -jnp.exp(-(jnp.maximum(-pw, 0.0) + jnp.log(1.0 + jnp.exp(-jnp.abs(pw)))) - 0.5)
    a_sig = 1.0 / (1.0 + jnp.exp(-(_mm(ad, a2_ref[...]) + a0_ref[...])))
    sg = 1.0 / (1.0 + jnp.exp(-gd))
    kx = k * kk_ref[...]
    nrm = jnp.sqrt(_head_sums(kx * kx))
    inv = 1.0 / jnp.maximum(nrm, L2_FLOOR)
    return dict(z=z, r=r, k=k, v=v, twd=twd, pw=pw, lw=lw, a_sig=a_sig, sg=sg, ad=ad, kk=kx * inv, inv=inv, live=nrm > L2_FLOOR)


def _prep_specs(tokens, rpad, rw, w2, a2, g2):
    tr = PREP_ROWS
    tile = lambda w: pl.BlockSpec((tr, w), lambda i: (i, 0))
    before = pl.BlockSpec((8, rpad), lambda i: (jnp.maximum(i * (tr // 8) - 1, 0), 0))
    whole = lambda a: pl.BlockSpec(a.shape, lambda i: (0, 0))
    par = pl.BlockSpec((1, rw), lambda i: (0, 0))
    return tile, before, whole, par, pl.BlockSpec((1, rpad), lambda i: (0, 0))


def _prep_fwd_call(zr, mu, w0, a0, k_k, k_a, w2, a2, g2):
    tokens, rpad = zr.shape
    rw = w0.shape[1]
    segs, _ = _prep_segments(rw, w2.shape[0], a2.shape[0], g2.shape[0])
    tile, before, whole, par, mu_spec = _prep_specs(tokens, rpad, rw, w2, a2, g2)

    def body(z_ref, zlast_ref, mu_ref, w0_ref, a0_ref, kk_ref, ka_ref, w2_ref, a2_ref, g2_ref,
             r_ref, lw_ref, kf_ref, v_ref, na_ref, b_ref, g_ref):
        f = _prep_forward_values(z_ref, zlast_ref, mu_ref, w0_ref, a0_ref, kk_ref, ka_ref, w2_ref, a2_ref, g2_ref,
                                 segs, pl.program_id(0) == 0)
        r_ref[...] = f["r"]
        v_ref[...] = f["v"]
        lw_ref[...] = f["lw"]
        kf_ref[...] = f["k"] * (1.0 + (f["a_sig"] - 1.0) * ka_ref[...])
        na_ref[...] = -f["kk"]
        b_ref[...] = f["kk"] * f["a_sig"]
        g_ref[...] = _mm(f["sg"], g2_ref[...])

    shape = jax.ShapeDtypeStruct((tokens, rw), F32)
    return pl.pallas_call(
        body, name="rwkv_prep_fwd", grid=(tokens // PREP_ROWS,),
        in_specs=[tile(rpad), before, mu_spec, par, par, par, par, whole(w2), whole(a2), whole(g2)],
        out_specs=[tile(rw)] * 7, out_shape=[shape] * 7,
        compiler_params=pltpu.CompilerParams(dimension_semantics=("parallel",), vmem_limit_bytes=VMEM_LIMIT_CAP),
    )(zr, zr, mu, w0, a0, k_k, k_a, w2, a2, g2)


def _prep_bwd_call(zr, mu, w0, a0, k_k, k_a, w2, a2, g2, cts):
    tokens, rpad = zr.shape
    rw = w0.shape[1]
    segs, _ = _prep_segments(rw, w2.shape[0], a2.shape[0], g2.shape[0])
    tile, before, whole, par, mu_spec = _prep_specs(tokens, rpad, rw, w2, a2, g2)
    nt = tokens // PREP_ROWS
    rev = lambda spec: pl.BlockSpec(spec.block_shape, lambda i, f=spec.index_map: f(nt - 1 - i))

    def body(z_ref, zlast_ref, mu_ref, w0_ref, a0_ref, kk_ref, ka_ref, w2_ref, a2_ref, g2_ref,
             dr_ref, dlw_ref, dkf_ref, dv_ref, dna_ref, db_ref, dg_ref,
             dz_ref, dmu_ref, dw0_ref, da0_ref, dkk_ref, dka_ref, dw2_ref, da2_ref, dg2_ref, carry):
        step = pl.program_id(0)

        @pl.when(step == 0)
        def _():
            for ref in (dmu_ref, dw0_ref, da0_ref, dkk_ref, dka_ref, dw2_ref, da2_ref, dg2_ref, carry):
                ref[...] = jnp.zeros_like(ref)

        f = _prep_forward_values(z_ref, zlast_ref, mu_ref, w0_ref, a0_ref, kk_ref, ka_ref, w2_ref, a2_ref, g2_ref,
                                 segs, step == nt - 1)
        k, kk, a_sig, sg, twd = f["k"], f["kk"], f["a_sig"], f["sg"], f["twd"]
        colsum = lambda t: jnp.sum(t, axis=0, keepdims=True)
        dkf, db, dg = dkf_ref[...], db_ref[...], dg_ref[...]
        ka = ka_ref[...]
        dgd = _mm(dg, g2_ref[...], tb=True) * sg * (1.0 - sg)
        dg2_ref[...] += _mm(sg, dg, ta=True)
        dkk = db * a_sig - dna_ref[...]
        da_sig = db * kk + dkf * k * ka
        dk = dkf * (1.0 + (a_sig - 1.0) * ka)
        dka_ref[...] += colsum(dkf * k * (a_sig - 1.0))
        along = jnp.where(f["live"], _head_sums(dkk * kk), 0.0)
        dkx = (dkk - kk * along) * f["inv"]
        dk = dk + dkx * kk_ref[...]
        dkk_ref[...] += colsum(dkx * k)
        dpa = da_sig * a_sig * (1.0 - a_sig)
        da0_ref[...] += colsum(dpa)
        dad = _mm(dpa, a2_ref[...], tb=True)
        da2_ref[...] += _mm(f["ad"], dpa, ta=True)
        dpw = dlw_ref[...] * f["lw"] / (1.0 + jnp.exp(f["pw"]))
        dw0_ref[...] += colsum(dpw)
        dwd = _mm(dpw, w2_ref[...], tb=True) * (1.0 - twd * twd)
        dw2_ref[...] += _mm(twd, dpw, ta=True)
        rows = PREP_ROWS
        last = lax.broadcasted_iota(jnp.int32, (rows, 1), 0) == rows - 1
        for name, dz in (("r", dr_ref[...]), ("k", dk), ("v", dv_ref[...]), ("wd", dwd), ("ad", dad), ("gd", dgd)):
            lo, hi = segs[name]
            mu_s = mu_ref[:, lo:hi]
            dmu_ref[:, lo:hi] += colsum(dz * f["z"][name][1])
            later = dz * mu_s
            dz_ref[:, lo:hi] = dz * (1.0 - mu_s) + jnp.where(last, carry[:, lo:hi], pltpu.roll(later, rows - 1, axis=0))
            carry[:, lo:hi] = later[0:1, :]

    tok = jax.ShapeDtypeStruct((tokens, rw), F32)
    acc = lambda a: jax.ShapeDtypeStruct(a.shape, F32)
    return pl.pallas_call(
        body, name="rwkv_prep_bwd", grid=(nt,),
        in_specs=[rev(tile(rpad)), rev(before), mu_spec, par, par, par, par, whole(w2), whole(a2), whole(g2)]
                 + [rev(tile(rw))] * 7,
        out_specs=[rev(tile(rpad)), mu_spec, par, par, par, par, whole(w2), whole(a2), whole(g2)],
        out_shape=[jax.ShapeDtypeStruct((tokens, rpad), F32), acc(mu), acc(w0), acc(a0), acc(k_k), acc(k_a), acc(w2), acc(a2), acc(g2)],
        scratch_shapes=[pltpu.VMEM((1, rpad), F32)],
        compiler_params=pltpu.CompilerParams(dimension_semantics=("arbitrary",), vmem_limit_bytes=VMEM_LIMIT_CAP),
    )(zr, zr, mu, w0, a0, k_k, k_a, w2, a2, g2, *cts)


@jax.custom_vjp
def rwkv_prep(zr, mu, w0, a0, k_k, k_a, w2, a2, g2):
    return tuple(_prep_fwd_call(zr, mu, w0, a0, k_k, k_a, w2, a2, g2))


def _rwkv_prep_bwd(res, cts):
    zr, mu, w0, a0, k_k, k_a, w2, a2, g2 = res
    dz, dmu, dw0, da0, dkk, dka, dw2, da2, dg2 = _prep_bwd_call(*res, cts)
    return dz, dmu, dw0, da0, dkk, dka, dw2.astype(w2.dtype), da2.astype(a2.dtype), dg2.astype(g2.dtype)


rwkv_prep.defvjp(lambda *a: (tuple(_prep_fwd_call(*a)), a), _rwkv_prep_bwd)


def _pair_masks(rows):
    lane = lax.broadcasted_iota(jnp.int32, (rows, PAIR), 1)
    return lane < HEAD_DIM, lane >= HEAD_DIM


def _bd(x):
    m0, m1 = _pair_masks(x.shape[0])
    return jnp.concatenate([jnp.where(m0, x, 0.0), jnp.where(m1, x, 0.0)], axis=0)


def _unbd(m, c):
    return jnp.where(_pair_masks(c)[0], m[:c], m[c:])


def _pair_a(l2, r2):
    return _mm(l2, _bd(r2), tb=True)


def _pair_mul(p2, x2):
    return _mm(p2, _bd(x2))


def _pair_mul_t(p2, x2):
    return _unbd(_mm(p2, x2, ta=True), p2.shape[0])


def _block_diag_mask():
    row = lax.broadcasted_iota(jnp.int32, (PAIR, PAIR), 0)
    lane = lax.broadcasted_iota(jnp.int32, (PAIR, PAIR), 1)
    return (row < HEAD_DIM) == (lane < HEAD_DIM), row == lane


def _wkv_pair_common(r, lw, k, a, b):
    c = r[0].shape[0]
    pairs = range(len(r))
    i = lax.broadcasted_iota(jnp.int32, (c, PAIR), 0)
    j = lax.broadcasted_iota(jnp.int32, (c, PAIR), 1) % c
    strict, incl = i > j, i >= j
    ti = lax.broadcasted_iota(jnp.int32, (c, c), 0)
    tj = lax.broadcasted_iota(jnp.int32, (c, c), 1)
    tri = jnp.where(ti >= tj, 1.0, 0.0).astype(BF16)
    lc = [sum(_dg(tri, part, False, False) for part in _split(lw[p], 3)) for p in pairs]
    lend = [lc[p][c - 1:c, :] for p in pairs]
    rt = [r[p] * jnp.exp(lc[p]) for p in pairs]
    at = [a[p] * jnp.exp(lc[p] - lw[p]) for p in pairs]
    pinv = [jnp.exp(-lc[p]) for p in pairs]
    kt = [k[p] * pinv[p] for p in pairs]
    bt = [b[p] * pinv[p] for p in pairs]
    e = [jnp.exp(lend[p] - lc[p]) for p in pairs]
    ktp = [k[p] * e[p] for p in pairs]
    btp = [b[p] * e[p] for p in pairs]
    a_ab = [jnp.where(strict, _pair_a(at[p], bt[p]), 0.0) for p in pairs]
    a_ak = [jnp.where(strict, _pair_a(at[p], kt[p]), 0.0) for p in pairs]
    a_rb = [jnp.where(incl, _pair_a(rt[p], bt[p]), 0.0) for p in pairs]
    a_rk = [jnp.where(incl, _pair_a(rt[p], kt[p]), 0.0) for p in pairs]
    t = [jnp.where(i == j, 1.0, 0.0) + a_ab[p] for p in pairs]
    xp = a_ab
    n = 2
    while n < c:
        xp = [_pair_mul(xp[p], xp[p]) for p in pairs]
        t = [t[p] + _pair_mul(t[p], xp[p]) for p in pairs]
        n *= 2
    bdm, eye = _block_diag_mask()
    pend_col = [jnp.sum(jnp.where(eye, jnp.exp(lend[p]), 0.0), axis=1, keepdims=True) for p in pairs]
    return dict(rt=rt, at=at, kt=kt, bt=bt, ktp=ktp, btp=btp, a_ak=a_ak, a_rb=a_rb, a_rk=a_rk, t=t,
                pend_col=pend_col, lend=lend, lc=lc, strict=strict, incl=incl, tri=tri, bdm=bdm)


def _wkv_group(width):
    npair = width // PAIR
    g = min(WKV_PAIRS_PER_STEP, npair)
    assert npair % g == 0
    return npair, g


def _wkv_fwd_call(r, lw, k, v, a, b):
    tokens, width = r.shape
    c = WKV_CHUNK
    nc = tokens // c
    npair, g = _wkv_group(width)

    def body(r_ref, lw_ref, k_ref, v_ref, a_ref, b_ref, y_ref, s_ref, st):
        @pl.when(pl.program_id(1) == 0)
        def _():
            st[...] = jnp.zeros_like(st)

        pairs = range(g)
        rv, lwv, kv, vv, av, bv = ([ref[:, p * PAIR:(p + 1) * PAIR] for p in pairs]
                                   for ref in (r_ref, lw_ref, k_ref, v_ref, a_ref, b_ref))
        s0 = [st[p] for p in pairs]
        q = _wkv_pair_common(rv, lwv, kv, av, bv)
        w1 = [_mm(q["at"][p], s0[p]) + _pair_mul(q["a_ak"][p], vv[p]) for p in pairs]
        u = [_pair_mul(q["t"][p], w1[p]) for p in pairs]
        y = [_mm(q["rt"][p], s0[p]) + _pair_mul(q["a_rb"][p], u[p]) + _pair_mul(q["a_rk"][p], vv[p]) for p in pairs]
        grow = [_mm(jnp.concatenate([q["btp"][p], q["ktp"][p]], axis=0), jnp.concatenate([u[p], vv[p]], axis=0), ta=True)
                for p in pairs]
        for p in pairs:
            y_ref[:, p * PAIR:(p + 1) * PAIR] = y[p]
            s_ref[0, p] = s0[p]
            st[p] = q["pend_col"][p] * s0[p] + jnp.where(q["bdm"], grow[p], 0.0)

    tok = pl.BlockSpec((c, g * PAIR), lambda gi, ci: (ci, gi))
    return pl.pallas_call(
        body, name="wkv_fwd", grid=(npair // g, nc),
        in_specs=[tok] * 6,
        out_specs=[tok, pl.BlockSpec((1, g, PAIR, PAIR), lambda gi, ci: (ci, gi, 0, 0))],
        out_shape=[jax.ShapeDtypeStruct((tokens, width), F32), jax.ShapeDtypeStruct((nc, npair, PAIR, PAIR), F32)],
        scratch_shapes=[pltpu.VMEM((g, PAIR, PAIR), F32)],
        compiler_params=pltpu.CompilerParams(dimension_semantics=("parallel", "arbitrary")),
    )(r, lw, k, v, a, b)


def _wkv_bwd_call(r, lw, k, v, a, b, s, dy):
    tokens, width = r.shape
    c = WKV_CHUNK
    nc = tokens // c
    npair, g = _wkv_group(width)

    def body(r_ref, lw_ref, k_ref, v_ref, a_ref, b_ref, s_ref, dy_ref,
             dr_ref, dlw_ref, dk_ref, dv_ref, da_ref, db_ref, dst):
        @pl.when(pl.program_id(1) == 0)
        def _():
            dst[...] = jnp.zeros_like(dst)

        pairs = range(g)
        rv, lwv, kv, vv, av, bv, dyv = ([ref[:, p * PAIR:(p + 1) * PAIR] for p in pairs]
                                        for ref in (r_ref, lw_ref, k_ref, v_ref, a_ref, b_ref, dy_ref))
        s0 = [s_ref[0, p] for p in pairs]
        dsc = [dst[p] for p in pairs]
        q = _wkv_pair_common(rv, lwv, kv, av, bv)
        rt, at, kt, bt, ktp, btp, t = (q[n] for n in ("rt", "at", "kt", "bt", "ktp", "btp", "t"))
        a_ak, a_rb, a_rk, strict, incl = (q[n] for n in ("a_ak", "a_rb", "a_rk", "strict", "incl"))
        w1 = [_mm(at[p], s0[p]) + _pair_mul(a_ak[p], vv[p]) for p in pairs]
        u = [_pair_mul(t[p], w1[p]) for p in pairs]
        du = [_pair_mul_t(a_rb[p], dyv[p]) + _mm(btp[p], dsc[p]) for p in pairs]
        dw1 = [_pair_mul_t(t[p], du[p]) for p in pairs]
        dv = [_pair_mul_t(a_rk[p], dyv[p]) + _mm(ktp[p], dsc[p]) + _pair_mul_t(a_ak[p], dw1[p]) for p in pairs]
        da_ab = [jnp.where(strict, _pair_a(dw1[p], u[p]), 0.0) for p in pairs]
        da_ak = [jnp.where(strict, _pair_a(dw1[p], vv[p]), 0.0) for p in pairs]
        da_rb = [jnp.where(incl, _pair_a(dyv[p], u[p]), 0.0) for p in pairs]
        da_rk = [jnp.where(incl, _pair_a(dyv[p], vv[p]), 0.0) for p in pairs]
        d_rt = [_mm(dyv[p], s0[p], tb=True) + _pair_mul(da_rb[p], bt[p]) + _pair_mul(da_rk[p], kt[p]) for p in pairs]
        d_at = [_mm(dw1[p], s0[p], tb=True) + _pair_mul(da_ab[p], bt[p]) + _pair_mul(da_ak[p], kt[p]) for p in pairs]
        d_bt = [_pair_mul_t(da_ab[p], at[p]) + _pair_mul_t(da_rb[p], rt[p]) for p in pairs]
        d_kt = [_pair_mul_t(da_ak[p], at[p]) + _pair_mul_t(da_rk[p], rt[p]) for p in pairs]
        d_btp = [_mm(u[p], dsc[p], tb=True) for p in pairs]
        d_ktp = [_mm(vv[p], dsc[p], tb=True) for p in pairs]
        ones = jnp.ones((8, PAIR), BF16)
        dpend = [sum(_dg(ones, part, False, True) for part in _split(dsc[p] * s0[p], 3))[0:1, :] * jnp.exp(q["lend"][p])
                 for p in pairs]
        grow = [_mm(jnp.concatenate([rt[p], at[p]], axis=0), jnp.concatenate([dyv[p], dw1[p]], axis=0), ta=True)
                for p in pairs]
        last = lax.broadcasted_iota(jnp.int32, (c, PAIR), 0) == c - 1
        for p in pairs:
            sl = slice(p * PAIR, (p + 1) * PAIR)
            dst[p] = q["pend_col"][p] * dsc[p] + jnp.where(q["bdm"], grow[p], 0.0)
            lc_e = d_ktp[p] * ktp[p] + d_btp[p] * btp[p]
            dlend = jnp.sum(lc_e, axis=0, keepdims=True) + dpend[p]
            dlc = d_rt[p] * rt[p] - d_kt[p] * kt[p] - d_bt[p] * bt[p] - lc_e + jnp.where(last, dlend, 0.0)
            dlp = d_at[p] * at[p]
            dlw_ref[:, sl] = sum(_dg(q["tri"], part, True, False) for part in _split(dlc + dlp, 3)) - dlp
            lc = q["lc"][p]
            pinv = jnp.exp(-lc)
            e = jnp.exp(q["lend"][p] - lc)
            dr_ref[:, sl] = d_rt[p] * jnp.exp(lc)
            da_ref[:, sl] = d_at[p] * jnp.exp(lc - lwv[p])
            dk_ref[:, sl] = d_kt[p] * pinv + d_ktp[p] * e
            db_ref[:, sl] = d_bt[p] * pinv + d_btp[p] * e
            dv_ref[:, sl] = dv[p]

    tok = pl.BlockSpec((c, g * PAIR), lambda gi, ci: (nc - 1 - ci, gi))
    tshape = jax.ShapeDtypeStruct((tokens, width), F32)
    return pl.pallas_call(
        body, name="wkv_bwd", grid=(npair // g, nc),
        in_specs=[tok] * 6 + [pl.BlockSpec((1, g, PAIR, PAIR), lambda gi, ci: (nc - 1 - ci, gi, 0, 0)), tok],
        out_specs=[tok] * 6, out_shape=[tshape] * 6,
        scratch_shapes=[pltpu.VMEM((g, PAIR, PAIR), F32)],
        compiler_params=pltpu.CompilerParams(dimension_semantics=("parallel", "arbitrary")),
    )(r, lw, k, v, a, b, s, dy)


@jax.custom_vjp
def wkv7(r, lw, k, v, a, b):
    return _wkv_fwd_call(r, lw, k, v, a, b)[0]


def _wkv7_fwd(r, lw, k, v, a, b):
    y, s = _wkv_fwd_call(r, lw, k, v, a, b)
    return y, (r, lw, k, v, a, b, s)


wkv7.defvjp(_wkv7_fwd, lambda res, dy: tuple(_wkv_bwd_call(*res, dy)))


def _attn_block(tokens):
    return ATTN_BLOCK_BIG if tokens % ATTN_BLOCK_BIG == 0 else ATTN_BLOCK


def _fox_layouts(cum):
    tokens, heads = cum.shape
    t = _attn_block(tokens)
    cq = cum.reshape(tokens, heads // 2, 2).transpose(1, 0, 2)
    ck = cum.T.reshape(heads // 2, 2, tokens // t, t).transpose(0, 2, 1, 3)
    return cq, ck


def _head_lane_masks(rows):
    lane = lax.broadcasted_iota(jnp.int32, (rows, 2 * HEAD_DIM), 1)
    return [lane < HEAD_DIM, lane >= HEAD_DIM]


def _fox_fwd_call(q, k, v, cq, ck):
    tokens, width = q.shape
    t = _attn_block(tokens)
    nb = tokens // t
    hd = HEAD_DIM
    npair = width // (2 * hd)

    def body(q_ref, k_ref, v_ref, cq_ref, ck_ref, o_ref, lse_ref):
        i = pl.program_id(1)
        masks = _head_lane_masks(t)
        q2 = q_ref[...]
        qs = [jnp.where(mk, q2, 0.0).astype(BF16) for mk in masks]
        cqs = [cq_ref[0, :, hh:hh + 1] for hh in range(2)]

        def block(j, carry, diagonal):
            off = pl.multiple_of(j * t, t)
            ckj = ck_ref[0, j]
            k2 = k_ref[pl.ds(off, t), :].astype(BF16)
            v2 = v_ref[pl.ds(off, t), :].astype(BF16)
            out = []
            for hh in range(2):
                m, l, acc = carry[hh]
                s = _dg(qs[hh], k2, False, True) + (cqs[hh] - ckj[hh:hh + 1, :])
                if diagonal:
                    keep = lax.broadcasted_iota(jnp.int32, (t, t), 0) >= lax.broadcasted_iota(jnp.int32, (t, t), 1)
                    s = jnp.where(keep, s, NEG_BIG)
                m_new = jnp.maximum(m, jnp.max(s, axis=1, keepdims=True))
                alpha = jnp.exp(m - m_new)
                p = jnp.exp(s - m_new)
                l = alpha * l + jnp.sum(p, axis=1, keepdims=True)
                acc = alpha * acc + _dg(p.astype(BF16), v2, False, False)
                out.append((m_new, l, acc))
            return tuple(out)

        init = tuple((jnp.full((t, 1), NEG_BIG, F32), jnp.zeros((t, 1), F32), jnp.zeros((t, 2 * hd), F32)) for _ in range(2))
        res = lax.fori_loop(0, i, lambda j, c: block(j, c, False), init)
        res = block(i, res, True)
        o_ref[...] = jnp.where(masks[0], res[0][2] / res[0][1], res[1][2] / res[1][1])
        for hh in range(2):
            lse_ref[0, :, hh:hh + 1] = res[hh][0] + jnp.log(res[hh][1])

    blk = pl.BlockSpec((t, 2 * hd), lambda hp, i: (i, hp))
    full = pl.BlockSpec((tokens, 2 * hd), lambda hp, i: (0, hp))
    cq_spec = pl.BlockSpec((1, t, 2), lambda hp, i: (hp, i, 0))
    ck_spec = pl.BlockSpec((1, nb, 2, t), lambda hp, i: (hp, 0, 0, 0))
    return pl.pallas_call(
        body, name="fox_fwd", grid=(npair, nb),
        in_specs=[blk, full, full, cq_spec, ck_spec],
        out_specs=[blk, cq_spec],
        out_shape=[jax.ShapeDtypeStruct((tokens, width), F32), jax.ShapeDtypeStruct((npair, tokens, 2), F32)],
        compiler_params=pltpu.CompilerParams(dimension_semantics=("parallel", "arbitrary")),
    )(q, k, v, cq, ck)


def _fox_bwd_call(q, k, v, cq, ck, o, lse, do):
    tokens, width = q.shape
    t = _attn_block(tokens)
    nb = tokens // t
    hd = HEAD_DIM
    npair = width // (2 * hd)

    def body(q_ref, k_ref, v_ref, cq_ref, ck_ref, o_ref, lse_ref, do_ref, dq_ref, dk_ref, dv_ref, dck_ref, dcq_ref):
        i = pl.program_id(1)

        @pl.when(i == 0)
        def _():
            dk_ref[...] = jnp.zeros_like(dk_ref)
            dv_ref[...] = jnp.zeros_like(dv_ref)
            dck_ref[...] = jnp.zeros_like(dck_ref)

        masks = _head_lane_masks(t)
        q2, do2, o2 = q_ref[...], do_ref[...], o_ref[...]
        qs = [jnp.where(mk, q2, 0.0).astype(BF16) for mk in masks]
        dos = [jnp.where(mk, do2, 0.0).astype(BF16) for mk in masks]
        deltas = [jnp.sum(dos[hh].astype(F32) * o2, axis=1, keepdims=True) for hh in range(2)]
        bias = [cq_ref[0, :, hh:hh + 1] - lse_ref[0, :, hh:hh + 1] for hh in range(2)]

        def block(j, carry, diagonal):
            off = pl.multiple_of(j * t, t)
            ckj = ck_ref[0, j]
            k2 = k_ref[pl.ds(off, t), :].astype(BF16)
            v2 = v_ref[pl.ds(off, t), :].astype(BF16)
            out = []
            dk2 = jnp.zeros((t, 2 * hd), F32)
            dv2 = jnp.zeros((t, 2 * hd), F32)
            for hh in range(2):
                s = _dg(qs[hh], k2, False, True) + (bias[hh] - ckj[hh:hh + 1, :])
                if diagonal:
                    keep = lax.broadcasted_iota(jnp.int32, (t, t), 0) >= lax.broadcasted_iota(jnp.int32, (t, t), 1)
                    s = jnp.where(keep, s, NEG_BIG)
                p = jnp.exp(s)
                dp = _dg(dos[hh], v2, False, True)
                ds = p * (dp - deltas[hh])
                dsb = ds.astype(BF16)
                dq, rowsum = carry[hh]
                out.append((dq + _dg(dsb, k2, False, False), rowsum + jnp.sum(ds, axis=1, keepdims=True)))
                dk2 = dk2 + _dg(dsb, qs[hh], True, False)
                dv2 = dv2 + _dg(p.astype(BF16), dos[hh], True, False)
                dck_ref[0, j, hh:hh + 1, :] -= jnp.sum(ds, axis=0, keepdims=True)
            dk_ref[pl.ds(off, t), :] += dk2
            dv_ref[pl.ds(off, t), :] += dv2
            return tuple(out)

        init = tuple((jnp.zeros((t, 2 * hd), F32), jnp.zeros((t, 1), F32)) for _ in range(2))
        res = lax.fori_loop(0, i, lambda j, c: block(j, c, False), init)
        res = block(i, res, True)
        dq_ref[...] = jnp.where(masks[0], res[0][0], res[1][0])
        for hh in range(2):
            dcq_ref[0, :, hh:hh + 1] = res[hh][1]

    blk = pl.BlockSpec((t, 2 * hd), lambda hp, i: (i, hp))
    full = pl.BlockSpec((tokens, 2 * hd), lambda hp, i: (0, hp))
    cq_spec = pl.BlockSpec((1, t, 2), lambda hp, i: (hp, i, 0))
    ck_spec = pl.BlockSpec((1, nb, 2, t), lambda hp, i: (hp, 0, 0, 0))
    tshape = jax.ShapeDtypeStruct((tokens, width), F32)
    return pl.pallas_call(
        body, name="fox_bwd", grid=(npair, nb),
        in_specs=[blk, full, full, cq_spec, ck_spec, blk, cq_spec, blk],
        out_specs=[blk, full, full, ck_spec, cq_spec],
        out_shape=[tshape, tshape, tshape, jax.ShapeDtypeStruct((npair, nb, 2, t), F32),
                   jax.ShapeDtypeStruct((npair, tokens, 2), F32)],
        compiler_params=pltpu.CompilerParams(dimension_semantics=("parallel", "arbitrary")),
    )(q, k, v, cq, ck, o, lse, do)


@jax.custom_vjp
def fox_attention(q, k, v, cum):
    return _fox_fwd(q, k, v, cum)[0]


def _fox_fwd(q, k, v, cum):
    cq, ck = _fox_layouts(cum)
    q, k, v = q.astype(BF16), k.astype(BF16), v.astype(BF16)
    o, lse = _fox_fwd_call(q, k, v, cq, ck)
    return o, (q, k, v, cq, ck, o, lse)


def _fox_bwd(res, do):
    q, k, v, cq, ck, o, lse = res
    dq, dk, dv, dck, dcq = _fox_bwd_call(q, k, v, cq, ck, o, lse, do)
    npair, nb, _, t = dck.shape
    dcum = dck.transpose(0, 2, 1, 3).reshape(2 * npair, nb * t).T + dcq.transpose(1, 0, 2).reshape(nb * t, 2 * npair)
    return dq, dk, dv, dcum


fox_attention.defvjp(_fox_fwd, _fox_bwd)


def _loss_call(y, target):
    rows, d = y.shape
    tr = _row_tile(rows, d)

    def body(y_ref, t_ref, loss_ref, dy_ref):
        @pl.when(pl.program_id(0) == 0)
        def _():
            loss_ref[...] = jnp.zeros_like(loss_ref)

        diff = y_ref[...] - t_ref[...]
        dy_ref[...] = diff * (1.0 / d)
        loss_ref[...] += (0.5 / d) * jnp.sum(jnp.sum(diff * diff, axis=1, keepdims=True), axis=0, keepdims=True)

    return pl.pallas_call(
        body, name="loss", grid=(rows // tr,),
        in_specs=[pl.BlockSpec((tr, d), lambda i: (i, 0))] * 2,
        out_specs=[pl.BlockSpec((1, 1), lambda i: (0, 0)), pl.BlockSpec((tr, d), lambda i: (i, 0))],
        out_shape=[jax.ShapeDtypeStruct((1, 1), F32), jax.ShapeDtypeStruct((rows, d), F32)],
        compiler_params=pltpu.CompilerParams(dimension_semantics=("arbitrary",)),
    )(y, target)


def _adamw_call(w, g, m, v):
    rows, cols = w.shape
    tr = _row_tile_ragged(rows, cols, budget=1024 * 1024)
    c1 = 1.0 / (1.0 - ADAM_B1 ** ADAM_STEP)
    c2 = 1.0 / (1.0 - ADAM_B2 ** ADAM_STEP)

    def body(w_ref, g_ref, m_ref, v_ref, d_ref, nm_ref, nv_ref):
        gv = g_ref[...]
        nm = ADAM_B1 * m_ref[...] + (1.0 - ADAM_B1) * gv
        nv = ADAM_B2 * v_ref[...] + (1.0 - ADAM_B2) * (gv * gv)
        nm_ref[...] = nm
        nv_ref[...] = nv
        d_ref[...] = -ADAM_LR * ((nm * c1) / (jnp.sqrt(nv * c2) + ADAM_EPS) + ADAM_WD * w_ref[...])

    spec = pl.BlockSpec((tr, cols), lambda i: (i, 0))
    shape = jax.ShapeDtypeStruct((rows, cols), F32)
    return pl.pallas_call(
        body, name="adamw", grid=(pl.cdiv(rows, tr),),
        in_specs=[spec] * 4, out_specs=[spec] * 3, out_shape=[shape] * 3,
        compiler_params=pltpu.CompilerParams(dimension_semantics=("parallel",)),
    )(w, g, m, v)


def _my_place():
    return lax.axis_index("x"), lax.axis_index("y"), lax.axis_index("c")


def _place_index(px, py, pc):
    return 4 * px + 2 * py + pc


HBM_SPEC = pl.BlockSpec(memory_space=pltpu.HBM)


def _all_gather_call(block):
    def body(x_ref, out_ref, send_sems, recv_sems, local_sem):
        x, y, c = _my_place()
        me, sibling = (x, y, c), (x, y, 1 - c)
        chips = [(1 - x, y), (x, 1 - y), (1 - x, 1 - y)]

        def slot(px, py, pc):
            return out_ref.at[_place_index(px, py, pc)]

        def copy(k, blk, to, src=None):
            return pltpu.make_async_remote_copy(
                src_ref=slot(*blk) if src is None else src, dst_ref=slot(*blk),
                send_sem=send_sems.at[k], recv_sem=recv_sems.at[k],
                device_id=to, device_id_type=pl.DeviceIdType.MESH)

        mine = pltpu.make_async_copy(x_ref, slot(*me), local_sem)
        mine.start()
        first = [copy(0, me, sibling, src=x_ref)]
        first += [copy(1 + j, me, (*chip, c), src=x_ref) for j, chip in enumerate(chips)]
        for cp in first:
            cp.start()
        passed = [copy(4 + j, (*chip, c), sibling) for j, chip in enumerate(chips)]
        for j, chip in enumerate(chips):
            copy(1 + j, (*chip, c), me).wait_recv()
            passed[j].start()
        copy(0, sibling, me).wait_recv()
        for j, chip in enumerate(chips):
            copy(4 + j, (*chip, 1 - c), me).wait_recv()
        for cp in first + passed:
            cp.wait_send()
        mine.wait()

    return pl.pallas_call(
        body, name="all_gather",
        out_shape=jax.ShapeDtypeStruct((N_DEV,) + block.shape, block.dtype),
        in_specs=[HBM_SPEC], out_specs=HBM_SPEC,
        scratch_shapes=[pltpu.SemaphoreType.DMA((7,)), pltpu.SemaphoreType.DMA((7,)), pltpu.SemaphoreType.DMA],
    )(block)


SEM_SPEC = pl.BlockSpec(memory_space=pltpu.SEMAPHORE)
SIDE_EFFECT = pltpu.SideEffectType.DATAFLOW_SIDE_EFFECTING


def _peers():
    x, y, c = _my_place()
    out = []
    for k in range(1, N_DEV):
        peer = (x ^ (k >> 2), y ^ ((k >> 1) & 1), c ^ (k & 1))
        out.append((k - 1, peer, _place_index(*peer)))
    return _place_index(x, y, c), out


def _spread_start(src, per_peer, name, after=None):
    slot = src.shape[1:] if per_peer else src.shape
    order = () if after is None else (after,)

    def body(src_ref, land_ref, *rest):
        send_sems, recv_sems, src_thru, land_thru, token = rest[len(order):]
        mine, peers = _peers()
        for k, peer, peer_idx in peers:
            pltpu.make_async_remote_copy(
                src_ref=src_ref.at[peer_idx] if per_peer else src_ref, dst_ref=land_ref.at[mine],
                send_sem=send_sems.at[k], recv_sem=recv_sems.at[k],
                device_id=peer, device_id_type=pl.DeviceIdType.MESH).start()
        token[...] = jnp.zeros_like(token)

    return pl.pallas_call(
        body, name=name,
        out_shape=(pltpu.SemaphoreType.DMA((N_DEV - 1,)), pltpu.SemaphoreType.DMA((N_DEV - 1,)),
                   pltpu.HBM(src.shape, src.dtype), pltpu.HBM((N_DEV,) + slot, src.dtype),
                   jax.ShapeDtypeStruct((8, 128), F32)),
        in_specs=(HBM_SPEC, HBM_SPEC) + (pl.BlockSpec(memory_space=pl.ANY),) * len(order),
        out_specs=(SEM_SPEC, SEM_SPEC, HBM_SPEC, HBM_SPEC, pl.BlockSpec(memory_space=pltpu.VMEM)),
        input_output_aliases={0: 2, 1: 3},
        compiler_params=pltpu.CompilerParams(has_side_effects=SIDE_EFFECT),
    )(pltpu.with_memory_space_constraint(src, pltpu.HBM),
      pltpu.with_memory_space_constraint(lax.empty((N_DEV,) + slot, src.dtype), pltpu.HBM), *order)


def _spread_wait(handles, after, per_peer, name):
    send_sems, recv_sems, src_thru, land_thru = handles

    def body(src_ref, land_ref, send_sems, recv_sems, after_ref, src_dead, got_ref):
        _, peers = _peers()
        for k, peer, peer_idx in peers:
            copy = pltpu.make_async_remote_copy(
                src_ref=src_ref.at[peer_idx] if per_peer else src_ref, dst_ref=land_ref.at[peer_idx],
                send_sem=send_sems.at[k], recv_sem=recv_sems.at[k],
                device_id=peer, device_id_type=pl.DeviceIdType.MESH)
            copy.wait_send()
            copy.wait_recv()

    return pl.pallas_call(
        body, name=name,
        out_shape=(pltpu.HBM(src_thru.shape, src_thru.dtype), pltpu.HBM(land_thru.shape, land_thru.dtype)),
        in_specs=(HBM_SPEC, HBM_SPEC, SEM_SPEC, SEM_SPEC, pl.BlockSpec(memory_space=pl.ANY)),
        out_specs=(HBM_SPEC, HBM_SPEC), input_output_aliases={0: 0, 1: 1},
        compiler_params=pltpu.CompilerParams(has_side_effects=SIDE_EFFECT),
    )(src_thru, land_thru, send_sems, recv_sems, after)


def _sum_slots_call(slots):
    _, rows, cols = slots.shape
    tr = _row_tile_ragged(rows, cols, budget=512 * 1024)

    def body(s_ref, o_ref):
        acc = s_ref[0].astype(F32)
        for j in range(1, N_DEV):
            acc = acc + s_ref[j].astype(F32)
        o_ref[...] = acc

    return pl.pallas_call(
        body, name="sum_slots", grid=(pl.cdiv(rows, tr),),
        in_specs=[pl.BlockSpec((N_DEV, tr, cols), lambda i: (0, i, 0))],
        out_specs=pl.BlockSpec((tr, cols), lambda i: (i, 0)),
        out_shape=jax.ShapeDtypeStruct((rows, cols), F32),
        compiler_params=pltpu.CompilerParams(dimension_semantics=("parallel",)),
    )(slots)


def _sum_adamw_call(got, own, w, m, v):
    rows, cols = w.shape
    tr = _row_tile_ragged(rows, cols, budget=512 * 1024)
    c1 = 1.0 / (1.0 - ADAM_B1 ** ADAM_STEP)
    c2 = 1.0 / (1.0 - ADAM_B2 ** ADAM_STEP)

    def body(got_ref, own_ref, w_ref, m_ref, v_ref, g_ref, d_ref, nm_ref, nv_ref):
        mine = _place_index(*_my_place())
        gv = jnp.zeros(w_ref.shape, F32)
        for j in range(N_DEV):
            gv = gv + jnp.where(mine == j, own_ref[...], got_ref[j]).astype(F32)
        nm = ADAM_B1 * m_ref[...] + (1.0 - ADAM_B1) * gv
        nv = ADAM_B2 * v_ref[...] + (1.0 - ADAM_B2) * (gv * gv)
        g_ref[...] = gv
        nm_ref[...] = nm
        nv_ref[...] = nv
        d_ref[...] = -ADAM_LR * ((nm * c1) / (jnp.sqrt(nv * c2) + ADAM_EPS) + ADAM_WD * w_ref[...])

    spec = pl.BlockSpec((tr, cols), lambda i: (i, 0))
    shape = jax.ShapeDtypeStruct((rows, cols), F32)
    return pl.pallas_call(
        body, name="sum_adamw", grid=(pl.cdiv(rows, tr),),
        in_specs=[pl.BlockSpec((N_DEV, tr, cols), lambda i: (0, i, 0))] + [spec] * 4,
        out_specs=[spec] * 4, out_shape=[shape] * 4,
        compiler_params=pltpu.CompilerParams(dimension_semantics=("parallel",)),
    )(got, own, w, m, v)


def _with_own_slot(got, own, mine):
    return lax.dynamic_update_index_in_dim(got, own, mine, 0)


def _pack(vectors, width):
    flat = jnp.concatenate([v.reshape(-1) for v in vectors])
    return jnp.pad(flat, (0, width - flat.shape[0])).reshape(width // 128, 128)


def _unpack(packed, like):
    flat = packed.reshape(-1)
    out, at = [], 0
    for v in like:
        out.append(flat[at:at + v.size].reshape(v.shape))
        at += v.size
    return tuple(out)


def _cols_from_slots(slots):
    n, rows, cols = slots.shape
    return slots.transpose(1, 0, 2).reshape(rows, n * cols)


def _rows_from_slots(slots):
    return slots.reshape(-1, slots.shape[2])


def _pad128(n):
    return -(-n // 128) * 128


def _pad_to_tiles(a, axis):
    n = a.shape[axis]
    pads = [(0, 0)] * a.ndim
    pads[axis] = (0, _pad128(n) - n)
    return jnp.pad(a, pads)


def _rwkv_group(take, zeros, rw, dl, al, gl):
    at = 3 * rw
    parts = take(0, at)
    for n in (dl, al, gl):
        parts += take(at, at + n)
        if _pad128(n) > n:
            parts.append(zeros(_pad128(n) - n))
        at += n
    return parts


def _in_proj_layout(slots, rw, fw, dl, al, gl, whole):
    n_slots, rows, d = slots.shape
    wt = slots.reshape(n_slots * rows, d)
    take = lambda lo, hi: [wt[lo:hi]]
    zeros = lambda n: jnp.zeros((n, d), wt.dtype)
    rcols = 3 * rw + dl + al + gl
    fcols = 3 * fw + fw // HEAD_DIM
    group_r = _rwkv_group(take, zeros, rw, dl, al, gl)
    group_f = take(rcols, rcols + fcols) + ([zeros(_pad128(fcols) - fcols)] if _pad128(fcols) > fcols else [])
    group_g = take(rcols + fcols, n_slots * rows)
    if whole:
        return jnp.concatenate(group_r + group_f + group_g, axis=0)
    return tuple(jnp.concatenate(g, axis=0) for g in (group_r, group_f, group_g))


def _low_rank_layout(slots):
    return _pad_to_tiles(_cols_from_slots(slots), 0)


def _stage_embed(meta, x, n1, lp):
    h0 = jnp.concatenate([meta, x, jnp.zeros((lp - meta.shape[0] - x.shape[0], x.shape[1]), F32)], axis=0)
    return h0, rmsnorm(h0, n1)


def _stage_mix(z_r, z_f, small, w2, a2, g2, dims):
    (mu, w0, a0, k_k, k_a, r_k, gn_w, gn_b, q_g, k_g, f_bias) = small
    rw, fw, dl, al, gl = dims
    fcols = 3 * fw + fw // HEAD_DIM

    mu_group = jnp.concatenate(_rwkv_group(lambda lo, hi: [mu[:, lo:hi]], lambda n: jnp.zeros((1, n), F32), rw, dl, al, gl), axis=1)
    r, lw, kf, v, na, b, g = rwkv_prep(z_r, mu_group, w0, a0, k_k, k_a, w2, a2, g2)
    y = wkv7(r, lw, kf, v, na, b)
    y_a = gn_bonus(y, r, kf, v, g, gn_w, gn_b, r_k.reshape(1, rw))

    fq, fk, fv, fl = z_f[:, :fw], z_f[:, fw:2 * fw], z_f[:, 2 * fw:3 * fw], z_f[:, 3 * fw:fcols]
    fq = head_rms(fq, jnp.tile(q_g, (1, fw // HEAD_DIM)) * (HEAD_DIM ** -0.5))
    fk = head_rms(fk, jnp.tile(k_g, (1, fw // HEAD_DIM)))
    cum = jnp.cumsum(jax.nn.log_sigmoid(badd(fl, f_bias)), axis=0)
    y_b = fox_attention(fq, fk, fv, cum)
    return y_a, y_b


def _stage_merge(h0, y_a, y_b, z_g, w_a, w_b, w_o):
    merged = gated_merge(z_g, dense_cols_bf16(y_a, w_a), dense_cols_bf16(y_b, w_b))
    return dense_add(merged, w_o, h0)


def _stage_ffn(h1, n2, w_gu, w_dn):
    return dense_add(swiglu(dense_cols_bf16(rmsnorm(h1, n2), w_gu)), w_dn, h1)


SHARDED = ("meta_tokens", "w_in", "rwkv_w2", "rwkv_a2", "rwkv_g2", "w_branch_a", "w_branch_b", "w_o", "w_gate_up", "w_down")
LOW_RANK = ("rwkv_w2", "rwkv_a2", "rwkv_g2")
SMALL = ("norm1_g", "rwkv_mu", "rwkv_w0", "rwkv_a0", "rwkv_k_k", "rwkv_k_a", "rwkv_r_k", "rwkv_gn_w", "rwkv_gn_b",
         "fox_q_norm_g", "fox_k_norm_g", "fox_f_bias", "norm2_g")
WEIGHTS = ("meta_tokens", "norm1_g", "w_in", "rwkv_mu", "rwkv_w0", "rwkv_w2", "rwkv_a0", "rwkv_a2", "rwkv_g2", "rwkv_k_k",
           "rwkv_k_a", "rwkv_r_k", "rwkv_gn_w", "rwkv_gn_b", "fox_q_norm_g", "fox_k_norm_g", "fox_f_bias", "w_branch_a",
           "w_branch_b", "w_o", "norm2_g", "w_gate_up", "w_down")


def _as2d(a):
    return a.reshape(-1, a.shape[-1])


def kernel(x, meta_tokens, norm1_g, w_in, rwkv_mu, rwkv_w0, rwkv_w2, rwkv_a0, rwkv_a2, rwkv_g2, rwkv_k_k, rwkv_k_a, rwkv_r_k, rwkv_gn_w, rwkv_gn_b, fox_q_norm_g, fox_k_norm_g, fox_f_bias, w_branch_a, w_branch_b, w_o, norm2_g, w_gate_up, w_down, loss_target, m_meta_tokens, m_norm1_g, m_w_in, m_rwkv_mu, m_rwkv_w0, m_rwkv_w2, m_rwkv_a0, m_rwkv_a2, m_rwkv_g2, m_rwkv_k_k, m_rwkv_k_a, m_rwkv_r_k, m_rwkv_gn_w, m_rwkv_gn_b, m_fox_q_norm_g, m_fox_k_norm_g, m_fox_f_bias, m_w_branch_a, m_w_branch_b, m_w_o, m_norm2_g, m_w_gate_up, m_w_down, v_meta_tokens, v_norm1_g, v_w_in, v_rwkv_mu, v_rwkv_w0, v_rwkv_w2, v_rwkv_a0, v_rwkv_a2, v_rwkv_g2, v_rwkv_k_k, v_rwkv_k_a, v_rwkv_r_k, v_rwkv_gn_w, v_rwkv_gn_b, v_fox_q_norm_g, v_fox_k_norm_g, v_fox_f_bias, v_w_branch_a, v_w_branch_b, v_w_o, v_norm2_g, v_w_gate_up, v_w_down):
    given = dict(locals())
    w = {n: given[n] for n in WEIGHTS}
    assert rwkv_r_k.shape[-1] == HEAD_DIM
    n_meta, seq = meta_tokens.shape[0], x.shape[1]
    tokens = n_meta + seq
    lp = -(-tokens // TOKEN_TILE) * TOKEN_TILE
    mine = _place_index(*(lax.axis_index(a) for a in MESH_AXES))
    x2 = x[0]

    local = {n: _as2d(given[n]) for n in given if n != "x" and n != "loss_target"}
    for n in ("w_in", "m_w_in", "v_w_in"):
        local[n] = jnp.transpose(given[n][0])
    blocks = {n: local[n].astype(F32 if n == "meta_tokens" else BF16) for n in SHARDED}
    for prefix in ("", "m_", "v_"):
        local[prefix + "low_rank"] = jnp.concatenate([local[prefix + n] for n in LOW_RANK], axis=0)
    blocks["low_rank"] = jnp.concatenate([blocks[n] for n in LOW_RANK], axis=0)
    low_rank_ends = [sum(local[n].shape[0] for n in LOW_RANK[:i + 1]) for i in range(len(LOW_RANK))]
    low_rank_rows = lambda a, axis: [lax.slice_in_dim(a, lo, hi, axis=axis) for lo, hi in zip([0] + low_rank_ends, low_rank_ends)]
    first = ("meta_tokens", "low_rank")
    started = {n: _spread_start(blocks[n], False, "gather_start_" + n) for n in first}
    zero = sum(started[n][4][0, 0] for n in first)

    def gathered(n, after):
        own, got = _spread_wait(started[n][:4], after, False, "gather_wait_" + n)
        return _with_own_slot(got, own, mine)

    sm = {n: _as2d(w[n]) for n in SMALL}
    small_mix = tuple(sm[n] for n in SMALL[1:-1])
    n1 = sm["norm1_g"] + zero
    rw, fw = w_branch_a.shape[-2], w_branch_b.shape[-2]
    dims = (rw, fw, rwkv_w2.shape[-2], rwkv_a2.shape[-2], rwkv_g2.shape[-2])
    same = lambda s: (s,)

    meta, un_meta = jax.vjp(_cols_from_slots, gathered("meta_tokens", x2))
    (h0, xn), vjp_embed = jax.vjp(lambda m, xs, g: _stage_embed(m, xs, g, lp), meta, x2, n1)
    in_slots = _all_gather_call(blocks["w_in"])
    later = [n for n in SHARDED if n not in first and n not in LOW_RANK and n != "w_in"]
    started.update({n: _spread_start(blocks[n], False, "gather_start_" + n, after=in_slots) for n in later})
    w_groups = _in_proj_layout(in_slots, *dims, whole=False)
    w_cat, un_in = jax.vjp(lambda s: _in_proj_layout(s, *dims, whole=True), in_slots)
    xn_b = xn.astype(BF16)
    behind = sum(started[n][4] for n in later)
    z_r, z_f, z_g = (_matmul(xn_b, wg, tb=True, name="in_proj_" + tag, after=behind, out_dtype=BF16 if tag == "g" else F32)
                     for wg, tag in zip(w_groups, "rfg"))
    (w2, un_w2), (a2, un_a2), (g2, un_g2) = (jax.vjp(_low_rank_layout, s) for s in low_rank_rows(gathered("low_rank", xn), 1))
    (y_a, y_b), vjp_mix = jax.vjp(lambda zr, zf, s, a, b, c: _stage_mix(zr, zf, s, a, b, c, dims),
                                  z_r, z_f, small_mix, w2, a2, g2)
    w_a, w_b = gathered("w_branch_a", y_a), gathered("w_branch_b", y_a)
    w_o_full, un_wo = jax.vjp(_rows_from_slots, gathered("w_o", y_a))
    h1, vjp_merge = jax.vjp(_stage_merge, h0, y_a, y_b, z_g, w_a, w_b, w_o_full)
    w_gu = gathered("w_gate_up", h1)
    w_dn, un_dn = jax.vjp(_rows_from_slots, gathered("w_down", h1))
    y, vjp_ffn = jax.vjp(_stage_ffn, h1, sm["norm2_g"], w_gu, w_dn)

    loss_part, dy_real = _loss_call(y[n_meta:tokens], loss_target[0])
    dy = jnp.pad(dy_real, ((n_meta, lp - tokens), (0, 0)))
    loss = lax.psum(loss_part[0, 0], MESH_AXES)

    sent = {}

    def send_grad(n, dmat, unlayout):
        sent[n] = _spread_start(unlayout(dmat)[0], True, "grad_start_" + n)
        return sent[n][4][0, 0]

    d_h1, d_n2, d_wgu, d_wdn = vjp_ffn(dy)
    behind = send_grad("w_gate_up", d_wgu, same) + send_grad("w_down", d_wdn, un_dn)
    d_h0, d_ya, d_yb, d_zg, d_wa, d_wb, d_wo = vjp_merge(d_h1 + behind)
    behind = send_grad("w_o", d_wo, un_wo) + send_grad("w_branch_a", d_wa, same) + send_grad("w_branch_b", d_wb, same)
    d_zr, d_zf, d_small_mix, d_w2, d_a2, d_g2 = vjp_mix((d_ya + behind.astype(d_ya.dtype), d_yb))
    dproj_b = jnp.concatenate([d_zr.astype(BF16), d_zf.astype(BF16), d_zg.astype(BF16)], axis=1)
    d_wcat = _matmul(dproj_b, xn_b, ta=True, out_dtype=BF16, name="in_proj_dw")
    send_grad("w_in", d_wcat, un_in)
    d_xn = _matmul(dproj_b, w_cat, out_dtype=BF16, name="in_proj_dx", after=sent["w_in"][4])
    send_grad("low_rank", jnp.concatenate([un_w2(d_w2)[0], un_a2(d_a2)[0], un_g2(d_g2)[0]], axis=1), same)
    d_meta, g_x, d_n1 = vjp_embed((d_h0, d_xn))
    send_grad("meta_tokens", d_meta, un_meta)

    small_grads = (d_n1, *d_small_mix, d_n2)
    n_small = sum(g.size for g in small_grads)
    width = -(-n_small // 1024) * 1024
    small_sent = _spread_start(_pack(small_grads, width), False, "small_grad_start")

    grads, delta, new_m, new_v = {}, {}, {}, {}
    after = g_x
    for n in ("w_gate_up", "w_down", "w_o", "w_branch_a", "w_branch_b", "low_rank", "meta_tokens", "w_in"):
        src, got = _spread_wait(sent[n][:4], after, True, "grad_wait_" + n)
        own = lax.dynamic_index_in_dim(src, mine, 0, keepdims=False)
        stepped = _sum_adamw_call(got, own, local[n], local["m_" + n], local["v_" + n])
        after = stepped[2]
        if n == "low_rank":
            for out, t in zip((grads, delta, new_m, new_v), stepped):
                out.update({name: part.reshape(w[name].shape) for name, part in zip(LOW_RANK, low_rank_rows(t, 0))})
            continue
        back = (lambda t: jnp.transpose(t)[None]) if n == "w_in" else (lambda t: t.reshape(w[n].shape))
        grads[n], delta[n], new_m[n], new_v[n] = (back(t) for t in stepped)
    own_small, got_small = _spread_wait(small_sent[:4], after, False, "small_grad_wait")
    small_total = _unpack(_sum_slots_call(_with_own_slot(got_small, own_small, mine)), small_grads)
    grads.update({n: g.reshape(w[n].shape) for n, g in zip(SMALL, small_total)})
    packs = [_pack([src[n] if p == "" else given[p + n] for n in SMALL], width)
             for p, src in (("", w), ("", grads), ("m_", None), ("v_", None))]
    like = [w[n] for n in SMALL]
    for out, packed in zip((delta, new_m, new_v), _adamw_call(*packs)):
        out.update(dict(zip(SMALL, _unpack(packed, like))))

    return (loss, g_x[None], *[grads[n] for n in WEIGHTS], *[delta[n] for n in WEIGHTS],
            *[new_m[n] for n in WEIGHTS], *[new_v[n] for n in WEIGHTS])
```

```python
import jax
import jax.numpy as jnp
from jax import lax
from jax.experimental import pallas as pl
from jax.experimental.pallas import tpu as pltpu

F32 = jnp.float32
BF16 = jnp.bfloat16

N_DEV = 8
MESH_AXES = ("x", "y", "c")
HEAD_DIM = 64
TOKEN_TILE = 128
WKV_CHUNK = 64
WKV_PAIRS_PER_STEP = 8
PAIR = 2 * HEAD_DIM
ATTN_BLOCK = 128
ATTN_BLOCK_BIG = 384
ATTN_PAIRS_PER_STEP = 2
RMS_EPS = 1e-6
GN_EPS = 64e-5
L2_FLOOR = 1e-12
NEG_BIG = -1e30
ADAM_LR, ADAM_B1, ADAM_B2, ADAM_EPS, ADAM_WD, ADAM_STEP = 0.001, 0.9, 0.999, 1e-08, 0.01, 10
VMEM_LIMIT_CAP = 56 * 1024 * 1024
VMEM_LIMIT_FLOOR = 32 * 1024 * 1024
MATMUL_VMEM_BUDGET = 36 * 1024 * 1024
GRID_STEP_BYTES = 1024 * 1024
ACC_BYTES_PER_HBM_BYTE = 6


def _vmem_limit(estimate_bytes):
    return int(min(max(estimate_bytes * 5 // 4, VMEM_LIMIT_FLOOR), VMEM_LIMIT_CAP))


def _row_tile(rows, width, itemsize=4, budget=2 * 1024 * 1024):
    for c in (1408, 1024, 704, 512, 384, 256, 128, 64, 32, 16, 8):
        if rows % c == 0 and c * width * itemsize <= budget:
            return c
    return rows


def _row_tile_ragged(rows, width, itemsize=4, budget=2 * 1024 * 1024):
    tile = _row_tile(rows, width, itemsize, budget)
    if tile * width * itemsize <= budget or rows < 16:
        return tile
    padded = -(-rows // 16) * 16
    for c in (1408, 1024, 704, 512, 384, 336, 256, 192, 128, 96, 64, 48, 32, 16):
        if padded % c == 0 and c * width * itemsize <= budget:
            return c
    return tile


def _dg(a, b, ta, tb):
    dims = (((0 if ta else 1,), (1 if tb else 0,)), ((), ()))
    return lax.dot_general(a, b, dims, preferred_element_type=F32)


def _split(x, n):
    parts = []
    for _ in range(n):
        h = x.astype(BF16)
        parts.append(h)
        x = x - h.astype(F32)
    return parts


def _mm(a, b, ta=False, tb=False):
    return _dg(a.astype(BF16), b.astype(BF16), ta, tb)


def _matmul(a, b, ta=False, tb=False, out_dtype=F32, name="matmul", after=None, b_slots=False, out_slots=0, add=None):
    if ta:
        kdim, m = a.shape
    else:
        m, kdim = a.shape
    if b_slots:
        n_slots, brows, bcols = b.shape
        n, k2 = (brows, n_slots * bcols) if tb else (n_slots * bcols, brows)
    elif tb:
        n, k2 = b.shape
    else:
        k2, n = b.shape
    assert kdim == k2, (a.shape, b.shape, ta, tb)
    sa, sb, so = a.dtype.itemsize, b.dtype.itemsize, jnp.dtype(out_dtype).itemsize
    n_unit = bcols if (b_slots and not tb) else (n // out_slots if out_slots else n)
    k_unit = bcols if (b_slots and tb) else kdim
    tm, tn, tk, n_outer = _matmul_tiles(m, n, kdim, ta, sa, sb, so, n_unit, k_unit)
    nk = kdim // tk
    ij = (lambda f: lambda j, i, k: f(i, j, k)) if n_outer else (lambda f: f)

    order = () if after is None else (after,)
    extra = () if add is None else (add,)

    def body(a_ref, b_ref, *rest):
        rest = rest[len(order):]
        add_ref = rest[0] if extra else None
        o_ref, acc = rest[len(extra)], rest[len(extra) + 1:]
        part = _dg(a_ref[...].astype(BF16), b_ref[...].astype(BF16), ta, tb)
        done = lambda total: (total if add_ref is None else total + add_ref[...]).astype(o_ref.dtype)
        if nk == 1:
            o_ref[...] = done(part)
            return
        kk = pl.program_id(2)

        @pl.when(kk == 0)
        def _():
            acc[0][...] = part

        @pl.when(kk > 0)
        def _():
            acc[0][...] += part

        @pl.when(kk == nk - 1)
        def _():
            o_ref[...] = done(acc[0][...])

    a_spec = pl.BlockSpec((tk, tm), ij(lambda i, j, k: (k, i))) if ta else pl.BlockSpec((tm, tk), ij(lambda i, j, k: (i, k)))
    if b_slots and tb:
        per = bcols // tk
        b_spec = pl.BlockSpec((None, tn, tk), ij(lambda i, j, k: (k // per, j, k % per)))
    elif b_slots:
        per = bcols // tn
        b_spec = pl.BlockSpec((None, tk, tn), ij(lambda i, j, k: (j // per, k, j % per)))
    elif tb:
        b_spec = pl.BlockSpec((tn, tk), ij(lambda i, j, k: (j, k)))
    else:
        b_spec = pl.BlockSpec((tk, tn), ij(lambda i, j, k: (k, j)))
    if out_slots:
        per_out = n // out_slots // tn
        out_spec = pl.BlockSpec((None, tm, tn), ij(lambda i, j, k: (j // per_out, i, j % per_out)))
        out_shape = jax.ShapeDtypeStruct((out_slots, m, n // out_slots), out_dtype)
    else:
        out_spec = pl.BlockSpec((tm, tn), ij(lambda i, j, k: (i, j)))
        out_shape = jax.ShapeDtypeStruct((m, n), out_dtype)
    return pl.pallas_call(
        body, name=name,
        grid=(n // tn, m // tm, nk) if n_outer else (m // tm, n // tn, nk),
        in_specs=[a_spec, b_spec] + [pl.BlockSpec(memory_space=pl.ANY)] * len(order)
                 + [pl.BlockSpec((tm, tn), ij(lambda i, j, k: (i, j)))] * len(extra),
        out_specs=out_spec,
        out_shape=out_shape,
        scratch_shapes=[pltpu.VMEM((tm, tn), F32)] if nk > 1 else [],
        compiler_params=pltpu.CompilerParams(
            dimension_semantics=("parallel", "parallel", "arbitrary"),
            vmem_limit_bytes=_vmem_limit(_matmul_vmem(tm, tn, tk, nk, sa, sb, so) + 2 * tm * tn * 4 * len(extra))),
    )(a, b, *order, *extra)


def _matmul_vmem(tm, tn, tk, nk, sa, sb, so):
    return 2 * (tm * tk * sa + tk * tn * sb + tm * tn * so) + tm * tn * 4 + (tm * tn * 4 if nk > 1 else 0)


def _matmul_tiles(m, n, kdim, ta, sa, sb, so, n_unit, k_unit):
    lane = (2816, 2176, 2048, 1408, 1024, 640, 512, 384, 256, 128)
    sublane = (2816, 2176, 2048, 1408, 1024, 704, 512, 384, 256, 128)
    divs = lambda dim, cands: [c for c in cands if dim % c == 0] or [dim]
    best = None
    for tm in divs(m, lane if ta else sublane):
        for tn in divs(n_unit, lane):
            for tk in divs(k_unit, sublane if ta else lane) + ([kdim] if k_unit == kdim and (ta or kdim <= 2048) else []):
                nk, nm, nn = kdim // tk, m // tm, n // tn
                if _matmul_vmem(tm, tn, tk, nk, sa, sb, so) > MATMUL_VMEM_BUDGET:
                    continue
                acc_bytes = m * n * 4 * 3 * nk // ACC_BYTES_PER_HBM_BYTE if nk > 1 else 0
                fixed = m * n * so + acc_bytes + nm * nn * nk * GRID_STEP_BYTES
                for n_outer in (False, True):
                    if n_outer:
                        a_reads, b_reads = (1 if (nk == 1 and nm == 1) else nn), (1 if nk == 1 else nm)
                    else:
                        a_reads, b_reads = (1 if nk == 1 else nn), (1 if (nk == 1 and nn == 1) else nm)
                    cost = m * kdim * sa * a_reads + kdim * n * sb * b_reads + fixed
                    if best is None or cost < best[0]:
                        best = (cost, tm, tn, tk, n_outer)
    return best[1:]


@jax.custom_vjp
def dense(x, w):
    return _matmul(x.astype(BF16), w, name="dense_fwd")


def _dense_fwd(x, w):
    return _matmul(x.astype(BF16), w, name="dense_fwd"), (x.astype(BF16), w, jnp.zeros((), x.dtype))


def _dense_bwd(res, dy):
    xb, w, like = res
    dyb = dy.astype(BF16)
    dx = _matmul(dyb, w, tb=True, out_dtype=like.dtype, name="dense_dx")
    dw = _matmul(xb, dyb, ta=True, out_dtype=w.dtype, name="dense_dw")
    return dx, dw


dense.defvjp(_dense_fwd, _dense_bwd)


@jax.custom_vjp
def dense_add(x, w, res):
    return _matmul(x.astype(BF16), w, name="dense_add_fwd", add=res)


def _dense_add_fwd(x, w, res):
    return _matmul(x.astype(BF16), w, name="dense_add_fwd", add=res), (x.astype(BF16), w, jnp.zeros((), x.dtype))


def _dense_add_bwd(res, dy):
    return (*_dense_bwd(res, dy), dy)


dense_add.defvjp(_dense_add_fwd, _dense_add_bwd)


def _make_dense_cols(out_dtype):
    @jax.custom_vjp
    def op(x, w_slots):
        return _matmul(x.astype(BF16), w_slots, b_slots=True, out_dtype=out_dtype, name="dense_cols_fwd")

    def fwd(x, w_slots):
        xb = x.astype(BF16)
        return (_matmul(xb, w_slots, b_slots=True, out_dtype=out_dtype, name="dense_cols_fwd"),
                (xb, w_slots, jnp.zeros((), x.dtype)))

    def bwd(res, dy):
        xb, w_slots, like = res
        dyb = dy.astype(BF16)
        dx = _matmul(dyb, w_slots, tb=True, b_slots=True, out_dtype=like.dtype, name="dense_cols_dx")
        dw = _matmul(xb, dyb, ta=True, out_slots=w_slots.shape[0], out_dtype=w_slots.dtype, name="dense_cols_dw")
        return dx, dw

    op.defvjp(fwd, bwd)
    return op


dense_cols_bf16 = _make_dense_cols(BF16)


def _swiglu_call(gu, d_act=None):
    rows, two_f = gu.shape
    f = two_f // 2
    tr = _row_tile(rows, two_f, itemsize=2, budget=3 * 1024 * 1024)
    half = lambda j: pl.BlockSpec((tr, f), lambda i, j=j: (i, j))
    ops = (gu, gu) if d_act is None else (gu, gu, d_act)

    def body(*refs):
        g, u = refs[0][...].astype(F32), refs[1][...].astype(F32)
        s = 1.0 / (1.0 + jnp.exp(-g))
        if d_act is None:
            refs[2][...] = (g * s * u).astype(BF16)
        else:
            d = refs[2][...].astype(F32)
            refs[3][:, :f] = (d * u * s * (1.0 + g * (1.0 - s))).astype(BF16)
            refs[3][:, f:] = (d * g * s).astype(BF16)

    width = f if d_act is None else two_f
    return pl.pallas_call(
        body, name="swiglu_fwd" if d_act is None else "swiglu_bwd", grid=(rows // tr,),
        in_specs=[half(0), half(1)] + ([half(0)] if d_act is not None else []),
        out_specs=pl.BlockSpec((tr, width), lambda i: (i, 0)),
        out_shape=jax.ShapeDtypeStruct((rows, width), BF16),
        compiler_params=pltpu.CompilerParams(dimension_semantics=("parallel",)),
    )(*ops)


@jax.custom_vjp
def swiglu(gu):
    return _swiglu_call(gu)


swiglu.defvjp(lambda gu: (_swiglu_call(gu), gu), lambda gu, d_act: (_swiglu_call(gu, d_act),))


def _merge_call(zg, a, b, dm=None):
    rows, d = a.shape
    tr = _row_tile(rows, d, budget=1024 * 1024)
    half = lambda j: pl.BlockSpec((tr, d), lambda i, j=j: (i, j))
    tile = half(0)

    def body(*refs):
        ga = 1.0 / (1.0 + jnp.exp(-refs[0][...].astype(F32)))
        gb = 1.0 / (1.0 + jnp.exp(-refs[1][...].astype(F32)))
        av, bv = refs[2][...].astype(F32), refs[3][...].astype(F32)
        if dm is None:
            refs[4][...] = (ga * av + gb * bv).astype(BF16)
        else:
            dv = refs[4][...].astype(F32)
            dzg_ref, da_ref, db_ref = refs[5:]
            dzg_ref[:, :d] = (dv * av * ga * (1.0 - ga)).astype(dzg_ref.dtype)
            dzg_ref[:, d:] = (dv * bv * gb * (1.0 - gb)).astype(dzg_ref.dtype)
            da_ref[...] = (dv * ga).astype(BF16)
            db_ref[...] = (dv * gb).astype(BF16)

    shape_b = jax.ShapeDtypeStruct((rows, d), BF16)
    if dm is None:
        out_specs, out_shape, ops = tile, shape_b, (zg, zg, a, b)
    else:
        out_specs = [pl.BlockSpec((tr, 2 * d), lambda i: (i, 0)), tile, tile]
        out_shape = [jax.ShapeDtypeStruct((rows, 2 * d), zg.dtype), shape_b, shape_b]
        ops = (zg, zg, a, b, dm)
    return pl.pallas_call(
        body, name="merge_fwd" if dm is None else "merge_bwd", grid=(rows // tr,),
        in_specs=[half(0), half(1)] + [tile] * (len(ops) - 2),
        out_specs=out_specs, out_shape=out_shape,
        compiler_params=pltpu.CompilerParams(dimension_semantics=("parallel",)),
    )(*ops)


@jax.custom_vjp
def gated_merge(zg, a, b):
    return _merge_call(zg, a, b)


gated_merge.defvjp(lambda zg, a, b: (_merge_call(zg, a, b), (zg, a, b)),
                   lambda res, dm: tuple(_merge_call(*res, dm)))


def _rms_fwd_call(x, g):
    rows, d = x.shape
    tr = _row_tile(rows, d)

    def body(x_ref, g_ref, y_ref):
        xv = x_ref[...]
        rstd = lax.rsqrt(jnp.mean(xv * xv, axis=1, keepdims=True) + RMS_EPS)
        y_ref[...] = ((xv * rstd) * g_ref[...]).astype(BF16)

    return pl.pallas_call(
        body, name="rms_fwd", grid=(rows // tr,),
        in_specs=[pl.BlockSpec((tr, d), lambda i: (i, 0)), pl.BlockSpec((1, d), lambda i: (0, 0))],
        out_specs=pl.BlockSpec((tr, d), lambda i: (i, 0)),
        out_shape=jax.ShapeDtypeStruct((rows, d), BF16),
        compiler_params=pltpu.CompilerParams(dimension_semantics=("parallel",)),
    )(x, g)


def _rms_bwd_call(x, g, dy):
    rows, d = x.shape
    tr = _row_tile(rows, d)

    def body(x_ref, g_ref, dy_ref, dx_ref, dg_ref):
        @pl.when(pl.program_id(0) == 0)
        def _():
            dg_ref[...] = jnp.zeros_like(dg_ref)

        xv = x_ref[...]
        dyv = dy_ref[...].astype(F32)
        rstd = lax.rsqrt(jnp.mean(xv * xv, axis=1, keepdims=True) + RMS_EPS)
        xhat = xv * rstd
        dxhat = dyv * g_ref[...]
        dx_ref[...] = rstd * (dxhat - xhat * jnp.mean(dxhat * xhat, axis=1, keepdims=True))
        dg_ref[...] += jnp.sum(dyv * xhat, axis=0, keepdims=True)

    return pl.pallas_call(
        body, name="rms_bwd", grid=(rows // tr,),
        in_specs=[pl.BlockSpec((tr, d), lambda i: (i, 0)), pl.BlockSpec((1, d), lambda i: (0, 0)),
                  pl.BlockSpec((tr, d), lambda i: (i, 0))],
        out_specs=[pl.BlockSpec((tr, d), lambda i: (i, 0)), pl.BlockSpec((1, d), lambda i: (0, 0))],
        out_shape=[jax.ShapeDtypeStruct((rows, d), F32), jax.ShapeDtypeStruct((1, d), F32)],
        compiler_params=pltpu.CompilerParams(dimension_semantics=("arbitrary",)),
    )(x, g, dy)


@jax.custom_vjp
def rmsnorm(x, g):
    return _rms_fwd_call(x, g)


rmsnorm.defvjp(lambda x, g: (_rms_fwd_call(x, g), (x, g)), lambda res, dy: tuple(_rms_bwd_call(res[0], res[1], dy)))


def _bcast_add_call(x, p):
    rows, d = x.shape
    tr = _row_tile(rows, d)

    def body(x_ref, p_ref, y_ref):
        y_ref[...] = x_ref[...] + p_ref[...]

    return pl.pallas_call(
        body, name="bcast_add", grid=(rows // tr,),
        in_specs=[pl.BlockSpec((tr, d), lambda i: (i, 0)), pl.BlockSpec((1, d), lambda i: (0, 0))],
        out_specs=pl.BlockSpec((tr, d), lambda i: (i, 0)),
        out_shape=jax.ShapeDtypeStruct((rows, d), F32),
        compiler_params=pltpu.CompilerParams(dimension_semantics=("parallel",)),
    )(x, p)


def _colsum_call(a):
    rows, d = a.shape
    tr = _row_tile(rows, d)

    def body(a_ref, o_ref):
        @pl.when(pl.program_id(0) == 0)
        def _():
            o_ref[...] = jnp.zeros_like(o_ref)

        o_ref[...] += jnp.sum(a_ref[...], axis=0, keepdims=True)

    return pl.pallas_call(
        body, name="colsum", grid=(rows // tr,),
        in_specs=[pl.BlockSpec((tr, d), lambda i: (i, 0))],
        out_specs=pl.BlockSpec((1, d), lambda i: (0, 0)),
        out_shape=jax.ShapeDtypeStruct((1, d), F32),
        compiler_params=pltpu.CompilerParams(dimension_semantics=("arbitrary",)),
    )(a)


@jax.custom_vjp
def badd(x, p):
    return _bcast_add_call(x, p)


badd.defvjp(lambda x, p: (_bcast_add_call(x, p), None), lambda res, dy: (dy, _colsum_call(dy)))


def _head_sums(x):
    i = lax.broadcasted_iota(jnp.int32, (PAIR, PAIR), 0) // HEAD_DIM
    j = lax.broadcasted_iota(jnp.int32, (PAIR, PAIR), 1) // HEAD_DIM
    ones = jnp.where(i == j, 1.0, 0.0).astype(BF16)
    hi, lo = _split(x, 2)
    cols = [slice(p * PAIR, (p + 1) * PAIR) for p in range(x.shape[1] // PAIR)]
    return jnp.concatenate([_dg(hi[:, c], ones, False, False) + _dg(lo[:, c], ones, False, False) for c in cols], axis=1)


def _head_rms_fwd_call(x, g):
    rows, w = x.shape
    tr = _row_tile(rows, w, budget=1024 * 1024)

    def body(x_ref, g_ref, y_ref):
        xv = x_ref[...]
        rstd = lax.rsqrt(_head_sums(xv * xv) * (1.0 / HEAD_DIM) + RMS_EPS)
        y_ref[...] = (xv * rstd) * g_ref[...]

    return pl.pallas_call(
        body, name="head_rms_fwd", grid=(rows // tr,),
        in_specs=[pl.BlockSpec((tr, w), lambda i: (i, 0)), pl.BlockSpec((1, w), lambda i: (0, 0))],
        out_specs=pl.BlockSpec((tr, w), lambda i: (i, 0)),
        out_shape=jax.ShapeDtypeStruct((rows, w), F32),
        compiler_params=pltpu.CompilerParams(dimension_semantics=("parallel",)),
    )(x, g)


def _head_rms_bwd_call(x, g, dy):
    rows, w = x.shape
    tr = _row_tile(rows, w, budget=1024 * 1024)

    def body(x_ref, g_ref, dy_ref, dx_ref, dg_ref):
        @pl.when(pl.program_id(0) == 0)
        def _():
            dg_ref[...] = jnp.zeros_like(dg_ref)

        xv, dyv = x_ref[...], dy_ref[...]
        rstd = lax.rsqrt(_head_sums(xv * xv) * (1.0 / HEAD_DIM) + RMS_EPS)
        xhat = xv * rstd
        dxhat = dyv * g_ref[...]
        dx_ref[...] = rstd * (dxhat - xhat * (_head_sums(dxhat * xhat) * (1.0 / HEAD_DIM)))
        dg_ref[...] += jnp.sum(dyv * xhat, axis=0, keepdims=True)

    return pl.pallas_call(
        body, name="head_rms_bwd", grid=(rows // tr,),
        in_specs=[pl.BlockSpec((tr, w), lambda i: (i, 0)), pl.BlockSpec((1, w), lambda i: (0, 0)),
                  pl.BlockSpec((tr, w), lambda i: (i, 0))],
        out_specs=[pl.BlockSpec((tr, w), lambda i: (i, 0)), pl.BlockSpec((1, w), lambda i: (0, 0))],
        out_shape=[jax.ShapeDtypeStruct((rows, w), F32), jax.ShapeDtypeStruct((1, w), F32)],
        compiler_params=pltpu.CompilerParams(dimension_semantics=("arbitrary",)),
    )(x, g, dy)


@jax.custom_vjp
def head_rms(x, g):
    return _head_rms_fwd_call(x, g)


head_rms.defvjp(lambda x, g: (_head_rms_fwd_call(x, g), (x, g)),
                lambda res, dy: tuple(_head_rms_bwd_call(res[0], res[1], dy)))


def _gn_fwd_call(y, r, kf, v, g, gw, gb, rk):
    rows, w = y.shape
    tr = _row_tile(rows, w, budget=512 * 1024)

    def body(y_ref, r_ref, kf_ref, v_ref, g_ref, gw_ref, gb_ref, rk_ref, o_ref):
        yv = y_ref[...]
        yc = yv - _head_sums(yv) * (1.0 / HEAD_DIM)
        rstd = lax.rsqrt(_head_sums(yc * yc) * (1.0 / HEAD_DIM) + GN_EPS)
        s = _head_sums(r_ref[...] * kf_ref[...] * rk_ref[...])
        o_ref[...] = (((yc * rstd) * gw_ref[...] + gb_ref[...] + s * v_ref[...]) * g_ref[...]).astype(BF16)

    tok = pl.BlockSpec((tr, w), lambda i: (i, 0))
    par = pl.BlockSpec((1, w), lambda i: (0, 0))
    return pl.pallas_call(
        body, name="gn_bonus_fwd", grid=(rows // tr,),
        in_specs=[tok] * 5 + [par] * 3, out_specs=tok,
        out_shape=jax.ShapeDtypeStruct((rows, w), BF16),
        compiler_params=pltpu.CompilerParams(dimension_semantics=("parallel",)),
    )(y, r, kf, v, g, gw, gb, rk)


def _gn_bwd_call(y, r, kf, v, g, gw, gb, rk, do):
    rows, w = y.shape
    tr = _row_tile(rows, w, budget=512 * 1024)

    def body(y_ref, r_ref, kf_ref, v_ref, g_ref, gw_ref, gb_ref, rk_ref, do_ref,
             dy_ref, dr_ref, dkf_ref, dv_ref, dg_ref, dgw_ref, dgb_ref, drk_ref):
        @pl.when(pl.program_id(0) == 0)
        def _():
            dgw_ref[...] = jnp.zeros_like(dgw_ref)
            dgb_ref[...] = jnp.zeros_like(dgb_ref)
            drk_ref[...] = jnp.zeros_like(drk_ref)

        yv, rv, kv, vv, rkv = y_ref[...], r_ref[...], kf_ref[...], v_ref[...], rk_ref[...]
        mean = lambda t: _head_sums(t) * (1.0 / HEAD_DIM)
        yc = yv - mean(yv)
        rstd = lax.rsqrt(mean(yc * yc) + GN_EPS)
        yhat = yc * rstd
        s = _head_sums(rv * kv * rkv)
        do = do_ref[...].astype(F32)
        dg_ref[...] = do * (yhat * gw_ref[...] + gb_ref[...] + s * vv)
        dov = do * g_ref[...]
        dyhat = dov * gw_ref[...]
        dy_ref[...] = rstd * (dyhat - mean(dyhat) - yhat * mean(dyhat * yhat))
        ds = _head_sums(dov * vv)
        dv_ref[...] = s * dov
        dr_ref[...] = ds * kv * rkv
        dkf_ref[...] = ds * rv * rkv
        dgw_ref[...] += jnp.sum(dov * yhat, axis=0, keepdims=True)
        dgb_ref[...] += jnp.sum(dov, axis=0, keepdims=True)
        drk_ref[...] += jnp.sum(ds * rv * kv, axis=0, keepdims=True)

    tok = pl.BlockSpec((tr, w), lambda i: (i, 0))
    par = pl.BlockSpec((1, w), lambda i: (0, 0))
    tshape = jax.ShapeDtypeStruct((rows, w), F32)
    pshape = jax.ShapeDtypeStruct((1, w), F32)
    return pl.pallas_call(
        body, name="gn_bonus_bwd", grid=(rows // tr,),
        in_specs=[tok] * 5 + [par] * 3 + [tok], out_specs=[tok] * 5 + [par] * 3,
        out_shape=[tshape] * 5 + [pshape] * 3,
        compiler_params=pltpu.CompilerParams(dimension_semantics=("arbitrary",)),
    )(y, r, kf, v, g, gw, gb, rk, do)


@jax.custom_vjp
def gn_bonus(y, r, kf, v, g, gw, gb, rk):
    return _gn_fwd_call(y, r, kf, v, g, gw, gb, rk)


def _gn_bwd(res, do):
    return tuple(_gn_bwd_call(*res, do))


gn_bonus.defvjp(lambda *a: (_gn_fwd_call(*a), a), _gn_bwd)


PREP_ROWS = 128


def _prep_segments(rw, lora_w, lora_a, lora_g):
    at = 3 * rw
    seg = {"r": (0, rw), "k": (rw, 2 * rw), "v": (2 * rw, 3 * rw)}
    for name, n in (("wd", lora_w), ("ad", lora_a), ("gd", lora_g)):
        seg[name] = (at, at + _pad128(n))
        at += _pad128(n)
    return seg, at


def _prep_shifted(z_ref, zlast_ref, mu_ref, seg, first_tile):
    lo, hi = seg
    zr = z_ref[:, lo:hi]
    rows = zr.shape[0]
    before = jnp.where(first_tile, 0.0, zlast_ref[7:8, lo:hi])
    row0 = lax.broadcasted_iota(jnp.int32, zr.shape, 0) == 0
    diff = jnp.where(row0, before, pltpu.roll(zr, 1, axis=0)) - zr
    return zr + diff * mu_ref[:, lo:hi], diff


def _prep_forward_values(z_ref, zlast_ref, mu_ref, w0_ref, a0_ref, kk_ref, ka_ref, w2_ref, a2_ref, g2_ref, segs, first_tile):
    z = {n: _prep_shifted(z_ref, zlast_ref, mu_ref, segs[n], first_tile) for n in segs}
    r, k, v, wd, ad, gd = (z[n][0] for n in ("r", "k", "v", "wd", "ad", "gd"))
    twd = jnp.tanh(wd)
    pw = _mm(twd, w2_ref[...]) + w0_ref[...]
    lw = -jnp.exp(-(jnp.maximum(-pw, 0.0) + jnp.log(1.0 + jnp.exp(-jnp.abs(pw)))) - 0.5)
    a_sig = 1.0 / (1.0 + jnp.exp(-(_mm(ad, a2_ref[...]) + a0_ref[...])))
    sg = 1.0 / (1.0 + jnp.exp(-gd))
    kx = k * kk_ref[...]
    nrm = jnp.sqrt(_head_sums(kx * kx))
    inv = 1.0 / jnp.maximum(nrm, L2_FLOOR)
    return dict(z=z, r=r, k=k, v=v, twd=twd, pw=pw, lw=lw, a_sig=a_sig, sg=sg, ad=ad, kk=kx * inv, inv=inv, live=nrm > L2_FLOOR)


def _prep_specs(tokens, rpad, rw, w2, a2, g2):
    tr = PREP_ROWS
    tile = lambda w: pl.BlockSpec((tr, w), lambda i: (i, 0))
    before = pl.BlockSpec((8, rpad), lambda i: (jnp.maximum(i * (tr // 8) - 1, 0), 0))
    whole = lambda a: pl.BlockSpec(a.shape, lambda i: (0, 0))
    par = pl.BlockSpec((1, rw), lambda i: (0, 0))
    return tile, before, whole, par, pl.BlockSpec((1, rpad), lambda i: (0, 0))


def _prep_fwd_call(zr, mu, w0, a0, k_k, k_a, w2, a2, g2):
    tokens, rpad = zr.shape
    rw = w0.shape[1]
    segs, _ = _prep_segments(rw, w2.shape[0], a2.shape[0], g2.shape[0])
    tile, before, whole, par, mu_spec = _prep_specs(tokens, rpad, rw, w2, a2, g2)

    def body(z_ref, zlast_ref, mu_ref, w0_ref, a0_ref, kk_ref, ka_ref, w2_ref, a2_ref, g2_ref,
             r_ref, lw_ref, kf_ref, v_ref, na_ref, b_ref, g_ref):
        f = _prep_forward_values(z_ref, zlast_ref, mu_ref, w0_ref, a0_ref, kk_ref, ka_ref, w2_ref, a2_ref, g2_ref,
                                 segs, pl.program_id(0) == 0)
        r_ref[...] = f["r"]
        v_ref[...] = f["v"]
        lw_ref[...] = f["lw"]
        kf_ref[...] = f["k"] * (1.0 + (f["a_sig"] - 1.0) * ka_ref[...])
        na_ref[...] = -f["kk"]
        b_ref[...] = f["kk"] * f["a_sig"]
        g_ref[...] = _mm(f["sg"], g2_ref[...])

    shape = jax.ShapeDtypeStruct((tokens, rw), F32)
    return pl.pallas_call(
        body, name="rwkv_prep_fwd", grid=(tokens // PREP_ROWS,),
        in_specs=[tile(rpad), before, mu_spec, par, par, par, par, whole(w2), whole(a2), whole(g2)],
        out_specs=[tile(rw)] * 7, out_shape=[shape] * 7,
        compiler_params=pltpu.CompilerParams(dimension_semantics=("parallel",), vmem_limit_bytes=VMEM_LIMIT_CAP),
    )(zr, zr, mu, w0, a0, k_k, k_a, w2, a2, g2)


def _prep_bwd_call(zr, mu, w0, a0, k_k, k_a, w2, a2, g2, cts):
    tokens, rpad = zr.shape
    rw = w0.shape[1]
    segs, _ = _prep_segments(rw, w2.shape[0], a2.shape[0], g2.shape[0])
    tile, before, whole, par, mu_spec = _prep_specs(tokens, rpad, rw, w2, a2, g2)
    nt = tokens // PREP_ROWS
    rev = lambda spec: pl.BlockSpec(spec.block_shape, lambda i, f=spec.index_map: f(nt - 1 - i))

    def body(z_ref, zlast_ref, mu_ref, w0_ref, a0_ref, kk_ref, ka_ref, w2_ref, a2_ref, g2_ref,
             dr_ref, dlw_ref, dkf_ref, dv_ref, dna_ref, db_ref, dg_ref,
             dz_ref, dmu_ref, dw0_ref, da0_ref, dkk_ref, dka_ref, dw2_ref, da2_ref, dg2_ref, carry):
        step = pl.program_id(0)

        @pl.when(step == 0)
        def _():
            for ref in (dmu_ref, dw0_ref, da0_ref, dkk_ref, dka_ref, dw2_ref, da2_ref, dg2_ref, carry):
                ref[...] = jnp.zeros_like(ref)

        f = _prep_forward_values(z_ref, zlast_ref, mu_ref, w0_ref, a0_ref, kk_ref, ka_ref, w2_ref, a2_ref, g2_ref,
                                 segs, step == nt - 1)
        k, kk, a_sig, sg, twd = f["k"], f["kk"], f["a_sig"], f["sg"], f["twd"]
        colsum = lambda t: jnp.sum(t, axis=0, keepdims=True)
        dkf, db, dg = dkf_ref[...], db_ref[...], dg_ref[...]
        ka = ka_ref[...]
        dgd = _mm(dg, g2_ref[...], tb=True) * sg * (1.0 - sg)
        dg2_ref[...] += _mm(sg, dg, ta=True)
        dkk = db * a_sig - dna_ref[...]
        da_sig = db * kk + dkf * k * ka
        dk = dkf * (1.0 + (a_sig - 1.0) * ka)
        dka_ref[...] += colsum(dkf * k * (a_sig - 1.0))
        along = jnp.where(f["live"], _head_sums(dkk * kk), 0.0)
        dkx = (dkk - kk * along) * f["inv"]
        dk = dk + dkx * kk_ref[...]
        dkk_ref[...] += colsum(dkx * k)
        dpa = da_sig * a_sig * (1.0 - a_sig)
        da0_ref[...] += colsum(dpa)
        dad = _mm(dpa, a2_ref[...], tb=True)
        da2_ref[...] += _mm(f["ad"], dpa, ta=True)
        dpw = dlw_ref[...] * f["lw"] / (1.0 + jnp.exp(f["pw"]))
        dw0_ref[...] += colsum(dpw)
        dwd = _mm(dpw, w2_ref[...], tb=True) * (1.0 - twd * twd)
        dw2_ref[...] += _mm(twd, dpw, ta=True)
        rows = PREP_ROWS
        last = lax.broadcasted_iota(jnp.int32, (rows, 1), 0) == rows - 1
        for name, dz in (("r", dr_ref[...]), ("k", dk), ("v", dv_ref[...]), ("wd", dwd), ("ad", dad), ("gd", dgd)):
            lo, hi = segs[name]
            mu_s = mu_ref[:, lo:hi]
            dmu_ref[:, lo:hi] += colsum(dz * f["z"][name][1])
            later = dz * mu_s
            dz_ref[:, lo:hi] = dz * (1.0 - mu_s) + jnp.where(last, carry[:, lo:hi], pltpu.roll(later, rows - 1, axis=0))
            carry[:, lo:hi] = later[0:1, :]

    tok = jax.ShapeDtypeStruct((tokens, rw), F32)
    acc = lambda a: jax.ShapeDtypeStruct(a.shape, F32)
    return pl.pallas_call(
        body, name="rwkv_prep_bwd", grid=(nt,),
        in_specs=[rev(tile(rpad)), rev(before), mu_spec, par, par, par, par, whole(w2), whole(a2), whole(g2)]
                 + [rev(tile(rw))] * 7,
        out_specs=[rev(tile(rpad)), mu_spec, par, par, par, par, whole(w2), whole(a2), whole(g2)],
        out_shape=[jax.ShapeDtypeStruct((tokens, rpad), F32), acc(mu), acc(w0), acc(a0), acc(k_k), acc(k_a), acc(w2), acc(a2), acc(g2)],
        scratch_shapes=[pltpu.VMEM((1, rpad), F32)],
        compiler_params=pltpu.CompilerParams(dimension_semantics=("arbitrary",), vmem_limit_bytes=VMEM_LIMIT_CAP),
    )(zr, zr, mu, w0, a0, k_k, k_a, w2, a2, g2, *cts)


@jax.custom_vjp
def rwkv_prep(zr, mu, w0, a0, k_k, k_a, w2, a2, g2):
    return tuple(_prep_fwd_call(zr, mu, w0, a0, k_k, k_a, w2, a2, g2))


def _rwkv_prep_bwd(res, cts):
    zr, mu, w0, a0, k_k, k_a, w2, a2, g2 = res
    dz, dmu, dw0, da0, dkk, dka, dw2, da2, dg2 = _prep_bwd_call(*res, cts)
    return dz, dmu, dw0, da0, dkk, dka, dw2.astype(w2.dtype), da2.astype(a2.dtype), dg2.astype(g2.dtype)


rwkv_prep.defvjp(lambda *a: (tuple(_prep_fwd_call(*a)), a), _rwkv_prep_bwd)


def _pair_masks(rows):
    lane = lax.broadcasted_iota(jnp.int32, (rows, PAIR), 1)
    return lane < HEAD_DIM, lane >= HEAD_DIM


def _bd(x):
    m0, m1 = _pair_masks(x.shape[0])
    return jnp.concatenate([jnp.where(m0, x, 0.0), jnp.where(m1, x, 0.0)], axis=0)


def _unbd(m, c):
    return jnp.where(_pair_masks(c)[0], m[:c], m[c:])


def _pair_a(l2, r2):
    return _mm(l2, _bd(r2), tb=True)


def _pair_mul(p2, x2):
    return _mm(p2, _bd(x2))


def _pair_mul_t(p2, x2):
    return _unbd(_mm(p2, x2, ta=True), p2.shape[0])


def _block_diag_mask():
    row = lax.broadcasted_iota(jnp.int32, (PAIR, PAIR), 0)
    lane = lax.broadcasted_iota(jnp.int32, (PAIR, PAIR), 1)
    return (row < HEAD_DIM) == (lane < HEAD_DIM), row == lane


def _wkv_pair_common(r, lw, k, a, b):
    c = r[0].shape[0]
    pairs = range(len(r))
    i = lax.broadcasted_iota(jnp.int32, (c, PAIR), 0)
    j = lax.broadcasted_iota(jnp.int32, (c, PAIR), 1) % c
    strict, incl = i > j, i >= j
    ti = lax.broadcasted_iota(jnp.int32, (c, c), 0)
    tj = lax.broadcasted_iota(jnp.int32, (c, c), 1)
    tri = jnp.where(ti >= tj, 1.0, 0.0).astype(BF16)
    lc = [sum(_dg(tri, part, False, False) for part in _split(lw[p], 3)) for p in pairs]
    lend = [lc[p][c - 1:c, :] for p in pairs]
    rt = [r[p] * jnp.exp(lc[p]) for p in pairs]
    at = [a[p] * jnp.exp(lc[p] - lw[p]) for p in pairs]
    pinv = [jnp.exp(-lc[p]) for p in pairs]
    kt = [k[p] * pinv[p] for p in pairs]
    bt = [b[p] * pinv[p] for p in pairs]
    e = [jnp.exp(lend[p] - lc[p]) for p in pairs]
    ktp = [k[p] * e[p] for p in pairs]
    btp = [b[p] * e[p] for p in pairs]
    a_ab = [jnp.where(strict, _pair_a(at[p], bt[p]), 0.0) for p in pairs]
    a_ak = [jnp.where(strict, _pair_a(at[p], kt[p]), 0.0) for p in pairs]
    a_rb = [jnp.where(incl, _pair_a(rt[p], bt[p]), 0.0) for p in pairs]
    a_rk = [jnp.where(incl, _pair_a(rt[p], kt[p]), 0.0) for p in pairs]
    t = [jnp.where(i == j, 1.0, 0.0) + a_ab[p] for p in pairs]
    xp = a_ab
    n = 2
    while n < c:
        xp = [_pair_mul(xp[p], xp[p]) for p in pairs]
        t = [t[p] + _pair_mul(t[p], xp[p]) for p in pairs]
        n *= 2
    bdm, eye = _block_diag_mask()
    pend_col = [jnp.sum(jnp.where(eye, jnp.exp(lend[p]), 0.0), axis=1, keepdims=True) for p in pairs]
    return dict(rt=rt, at=at, kt=kt, bt=bt, ktp=ktp, btp=btp, a_ak=a_ak, a_rb=a_rb, a_rk=a_rk, t=t,
                pend_col=pend_col, lend=lend, lc=lc, strict=strict, incl=incl, tri=tri, bdm=bdm)


def _wkv_group(width):
    npair = width // PAIR
    g = min(WKV_PAIRS_PER_STEP, npair)
    assert npair % g == 0
    return npair, g


def _wkv_fwd_call(r, lw, k, v, a, b):
    tokens, width = r.shape
    c = WKV_CHUNK
    nc = tokens // c
    npair, g = _wkv_group(width)

    def body(r_ref, lw_ref, k_ref, v_ref, a_ref, b_ref, y_ref, s_ref, st):
        @pl.when(pl.program_id(1) == 0)
        def _():
            st[...] = jnp.zeros_like(st)

        pairs = range(g)
        rv, lwv, kv, vv, av, bv = ([ref[:, p * PAIR:(p + 1) * PAIR] for p in pairs]
                                   for ref in (r_ref, lw_ref, k_ref, v_ref, a_ref, b_ref))
        s0 = [st[p] for p in pairs]
        q = _wkv_pair_common(rv, lwv, kv, av, bv)
        w1 = [_mm(q["at"][p], s0[p]) + _pair_mul(q["a_ak"][p], vv[p]) for p in pairs]
        u = [_pair_mul(q["t"][p], w1[p]) for p in pairs]
        y = [_mm(q["rt"][p], s0[p]) + _pair_mul(q["a_rb"][p], u[p]) + _pair_mul(q["a_rk"][p], vv[p]) for p in pairs]
        grow = [_mm(jnp.concatenate([q["btp"][p], q["ktp"][p]], axis=0), jnp.concatenate([u[p], vv[p]], axis=0), ta=True)
                for p in pairs]
        for p in pairs:
            y_ref[:, p * PAIR:(p + 1) * PAIR] = y[p]
            s_ref[0, p] = s0[p]
            st[p] = q["pend_col"][p] * s0[p] + jnp.where(q["bdm"], grow[p], 0.0)

    tok = pl.BlockSpec((c, g * PAIR), lambda gi, ci: (ci, gi))
    return pl.pallas_call(
        body, name="wkv_fwd", grid=(npair // g, nc),
        in_specs=[tok] * 6,
        out_specs=[tok, pl.BlockSpec((1, g, PAIR, PAIR), lambda gi, ci: (ci, gi, 0, 0))],
        out_shape=[jax.ShapeDtypeStruct((tokens, width), F32), jax.ShapeDtypeStruct((nc, npair, PAIR, PAIR), F32)],
        scratch_shapes=[pltpu.VMEM((g, PAIR, PAIR), F32)],
        compiler_params=pltpu.CompilerParams(dimension_semantics=("parallel", "arbitrary")),
    )(r, lw, k, v, a, b)


def _wkv_bwd_call(r, lw, k, v, a, b, s, dy):
    tokens, width = r.shape
    c = WKV_CHUNK
    nc = tokens // c
    npair, g = _wkv_group(width)

    def body(r_ref, lw_ref, k_ref, v_ref, a_ref, b_ref, s_ref, dy_ref,
             dr_ref, dlw_ref, dk_ref, dv_ref, da_ref, db_ref, dst):
        @pl.when(pl.program_id(1) == 0)
        def _():
            dst[...] = jnp.zeros_like(dst)

        pairs = range(g)
        rv, lwv, kv, vv, av, bv, dyv = ([ref[:, p * PAIR:(p + 1) * PAIR] for p in pairs]
                                        for ref in (r_ref, lw_ref, k_ref, v_ref, a_ref, b_ref, dy_ref))
        s0 = [s_ref[0, p] for p in pairs]
        dsc = [dst[p] for p in pairs]
        q = _wkv_pair_common(rv, lwv, kv, av, bv)
        rt, at, kt, bt, ktp, btp, t = (q[n] for n in ("rt", "at", "kt", "bt", "ktp", "btp", "t"))
        a_ak, a_rb, a_rk, strict, incl = (q[n] for n in ("a_ak", "a_rb", "a_rk", "strict", "incl"))
        w1 = [_mm(at[p], s0[p]) + _pair_mul(a_ak[p], vv[p]) for p in pairs]
        u = [_pair_mul(t[p], w1[p]) for p in pairs]
        du = [_pair_mul_t(a_rb[p], dyv[p]) + _mm(btp[p], dsc[p]) for p in pairs]
        dw1 = [_pair_mul_t(t[p], du[p]) for p in pairs]
        dv = [_pair_mul_t(a_rk[p], dyv[p]) + _mm(ktp[p], dsc[p]) + _pair_mul_t(a_ak[p], dw1[p]) for p in pairs]
        da_ab = [jnp.where(strict, _pair_a(dw1[p], u[p]), 0.0) for p in pairs]
        da_ak = [jnp.where(strict, _pair_a(dw1[p], vv[p]), 0.0) for p in pairs]
        da_rb = [jnp.where(incl, _pair_a(dyv[p], u[p]), 0.0) for p in pairs]
        da_rk = [jnp.where(incl, _pair_a(dyv[p], vv[p]), 0.0) for p in pairs]
        d_rt = [_mm(dyv[p], s0[p], tb=True) + _pair_mul(da_rb[p], bt[p]) + _pair_mul(da_rk[p], kt[p]) for p in pairs]
        d_at = [_mm(dw1[p], s0[p], tb=True) + _pair_mul(da_ab[p], bt[p]) + _pair_mul(da_ak[p], kt[p]) for p in pairs]
        d_bt = [_pair_mul_t(da_ab[p], at[p]) + _pair_mul_t(da_rb[p], rt[p]) for p in pairs]
        d_kt = [_pair_mul_t(da_ak[p], at[p]) + _pair_mul_t(da_rk[p], rt[p]) for p in pairs]
        d_btp = [_mm(u[p], dsc[p], tb=True) for p in pairs]
        d_ktp = [_mm(vv[p], dsc[p], tb=True) for p in pairs]
        ones = jnp.ones((8, PAIR), BF16)
        dpend = [sum(_dg(ones, part, False, True) for part in _split(dsc[p] * s0[p], 3))[0:1, :] * jnp.exp(q["lend"][p])
                 for p in pairs]
        grow = [_mm(jnp.concatenate([rt[p], at[p]], axis=0), jnp.concatenate([dyv[p], dw1[p]], axis=0), ta=True)
                for p in pairs]
        last = lax.broadcasted_iota(jnp.int32, (c, PAIR), 0) == c - 1
        for p in pairs:
            sl = slice(p * PAIR, (p + 1) * PAIR)
            dst[p] = q["pend_col"][p] * dsc[p] + jnp.where(q["bdm"], grow[p], 0.0)
            lc_e = d_ktp[p] * ktp[p] + d_btp[p] * btp[p]
            dlend = jnp.sum(lc_e, axis=0, keepdims=True) + dpend[p]
            dlc = d_rt[p] * rt[p] - d_kt[p] * kt[p] - d_bt[p] * bt[p] - lc_e + jnp.where(last, dlend, 0.0)
            dlp = d_at[p] * at[p]
            dlw_ref[:, sl] = sum(_dg(q["tri"], part, True, False) for part in _split(dlc + dlp, 3)) - dlp
            lc = q["lc"][p]
            pinv = jnp.exp(-lc)
            e = jnp.exp(q["lend"][p] - lc)
            dr_ref[:, sl] = d_rt[p] * jnp.exp(lc)
            da_ref[:, sl] = d_at[p] * jnp.exp(lc - lwv[p])
            dk_ref[:, sl] = d_kt[p] * pinv + d_ktp[p] * e
            db_ref[:, sl] = d_bt[p] * pinv + d_btp[p] * e
            dv_ref[:, sl] = dv[p]

    tok = pl.BlockSpec((c, g * PAIR), lambda gi, ci: (nc - 1 - ci, gi))
    tshape = jax.ShapeDtypeStruct((tokens, width), F32)
    return pl.pallas_call(
        body, name="wkv_bwd", grid=(npair // g, nc),
        in_specs=[tok] * 6 + [pl.BlockSpec((1, g, PAIR, PAIR), lambda gi, ci: (nc - 1 - ci, gi, 0, 0)), tok],
        out_specs=[tok] * 6, out_shape=[tshape] * 6,
        scratch_shapes=[pltpu.VMEM((g, PAIR, PAIR), F32)],
        compiler_params=pltpu.CompilerParams(dimension_semantics=("parallel", "arbitrary")),
    )(r, lw, k, v, a, b, s, dy)


@jax.custom_vjp
def wkv7(r, lw, k, v, a, b):
    return _wkv_fwd_call(r, lw, k, v, a, b)[0]


def _wkv7_fwd(r, lw, k, v, a, b):
    y, s = _wkv_fwd_call(r, lw, k, v, a, b)
    return y, (r, lw, k, v, a, b, s)


wkv7.defvjp(_wkv7_fwd, lambda res, dy: tuple(_wkv_bwd_call(*res, dy)))


def _attn_block(tokens):
    return ATTN_BLOCK_BIG if tokens % ATTN_BLOCK_BIG == 0 else ATTN_BLOCK


def _fox_layouts(cum):
    tokens, heads = cum.shape
    t = _attn_block(tokens)
    cq = cum.reshape(tokens, heads // 2, 2).transpose(1, 0, 2)
    ck = cum.T.reshape(heads // 2, 2, tokens // t, t).transpose(0, 2, 1, 3)
    return cq, ck


def _head_lane_masks(rows):
    lane = lax.broadcasted_iota(jnp.int32, (rows, 2 * HEAD_DIM), 1)
    return [lane < HEAD_DIM, lane >= HEAD_DIM]


def _fox_fwd_call(q, k, v, cq, ck):
    tokens, width = q.shape
    t = _attn_block(tokens)
    nb = tokens // t
    hd = HEAD_DIM
    npair = width // (2 * hd)

    g = ATTN_PAIRS_PER_STEP if npair % ATTN_PAIRS_PER_STEP == 0 else 1
    heads = [(pp, hh) for pp in range(g) for hh in range(2)]

    def body(q_ref, k_ref, v_ref, cq_ref, ck_ref, o_ref, lse_ref):
        i = pl.program_id(1)
        masks = _head_lane_masks(t)
        lanes = [slice(pp * PAIR, (pp + 1) * PAIR) for pp in range(g)]
        qs = [jnp.where(masks[hh], q_ref[:, lanes[pp]], 0.0).astype(BF16) for pp, hh in heads]
        cqs = [cq_ref[pp, :, hh:hh + 1] for pp, hh in heads]

        def block(j, carry, diagonal):
            off = pl.multiple_of(j * t, t)
            k2 = [k_ref[pl.ds(off, t), lanes[pp]].astype(BF16) for pp in range(g)]
            v2 = [v_ref[pl.ds(off, t), lanes[pp]].astype(BF16) for pp in range(g)]
            s = [_dg(qs[n], k2[pp], False, True) + (cqs[n] - ck_ref[pp, j][hh:hh + 1, :]) for n, (pp, hh) in enumerate(heads)]
            if diagonal:
                keep = lax.broadcasted_iota(jnp.int32, (t, t), 0) >= lax.broadcasted_iota(jnp.int32, (t, t), 1)
                s = [jnp.where(keep, x, NEG_BIG) for x in s]
            m_new = [jnp.maximum(carry[n][0], jnp.max(s[n], axis=1, keepdims=True)) for n in range(len(heads))]
            alpha = [jnp.exp(carry[n][0] - m_new[n]) for n in range(len(heads))]
            p = [jnp.exp(s[n] - m_new[n]) for n in range(len(heads))]
            l = [alpha[n] * carry[n][1] + jnp.sum(p[n], axis=1, keepdims=True) for n in range(len(heads))]
            acc = [alpha[n] * carry[n][2] + _dg(p[n].astype(BF16), v2[pp], False, False) for n, (pp, hh) in enumerate(heads)]
            return tuple(zip(m_new, l, acc))

        init = tuple((jnp.full((t, 1), NEG_BIG, F32), jnp.zeros((t, 1), F32), jnp.zeros((t, 2 * hd), F32)) for _ in heads)
        res = lax.fori_loop(0, i, lambda j, c: block(j, c, False), init)
        res = block(i, res, True)
        for pp in range(g):
            a, b = res[2 * pp], res[2 * pp + 1]
            o_ref[:, lanes[pp]] = jnp.where(masks[0], a[2] / a[1], b[2] / b[1])
        for n, (pp, hh) in enumerate(heads):
            lse_ref[pp, :, hh:hh + 1] = res[n][0] + jnp.log(res[n][1])

    blk = pl.BlockSpec((t, g * PAIR), lambda hp, i: (i, hp))
    full = pl.BlockSpec((tokens, g * PAIR), lambda hp, i: (0, hp))
    cq_spec = pl.BlockSpec((g, t, 2), lambda hp, i: (hp, i, 0))
    ck_spec = pl.BlockSpec((g, nb, 2, t), lambda hp, i: (hp, 0, 0, 0))
    return pl.pallas_call(
        body, name="fox_fwd", grid=(npair // g, nb),
        in_specs=[blk, full, full, cq_spec, ck_spec],
        out_specs=[blk, cq_spec],
        out_shape=[jax.ShapeDtypeStruct((tokens, width), F32), jax.ShapeDtypeStruct((npair, tokens, 2), F32)],
        compiler_params=pltpu.CompilerParams(dimension_semantics=("parallel", "arbitrary")),
    )(q, k, v, cq, ck)


def _fox_bwd_call(q, k, v, cq, ck, o, lse, do):
    tokens, width = q.shape
    t = _attn_block(tokens)
    nb = tokens // t
    hd = HEAD_DIM
    npair = width // (2 * hd)

    def body(q_ref, k_ref, v_ref, cq_ref, ck_ref, o_ref, lse_ref, do_ref, dq_ref, dk_ref, dv_ref, dck_ref, dcq_ref):
        i = pl.program_id(1)

        @pl.when(i == 0)
        def _():
            dk_ref[...] = jnp.zeros_like(dk_ref)
            dv_ref[...] = jnp.zeros_like(dv_ref)
            dck_ref[...] = jnp.zeros_like(dck_ref)

        masks = _head_lane_masks(t)
        q2, do2, o2 = q_ref[...], do_ref[...], o_ref[...]
        qs = [jnp.where(mk, q2, 0.0).astype(BF16) for mk in masks]
        dos = [jnp.where(mk, do2, 0.0).astype(BF16) for mk in masks]
        deltas = [jnp.sum(dos[hh].astype(F32) * o2, axis=1, keepdims=True) for hh in range(2)]
        bias = [cq_ref[0, :, hh:hh + 1] - lse_ref[0, :, hh:hh + 1] for hh in range(2)]

        def block(j, carry, diagonal):
            off = pl.multiple_of(j * t, t)
            ckj = ck_ref[0, j]
            k2 = k_ref[pl.ds(off, t), :].astype(BF16)
            v2 = v_ref[pl.ds(off, t), :].astype(BF16)
            out = []
            dk2 = jnp.zeros((t, 2 * hd), F32)
            dv2 = jnp.zeros((t, 2 * hd), F32)
            for hh in range(2):
                s = _dg(qs[hh], k2, False, True) + (bias[hh] - ckj[hh:hh + 1, :])
                if diagonal:
                    keep = lax.broadcasted_iota(jnp.int32, (t, t), 0) >= lax.broadcasted_iota(jnp.int32, (t, t), 1)
                    s = jnp.where(keep, s, NEG_BIG)
                p = jnp.exp(s)
                dp = _dg(dos[hh], v2, False, True)
                ds = p * (dp - deltas[hh])
                dsb = ds.astype(BF16)
                dq, rowsum = carry[hh]
                out.append((dq + _dg(dsb, k2, False, False), rowsum + jnp.sum(ds, axis=1, keepdims=True)))
                dk2 = dk2 + _dg(dsb, qs[hh], True, False)
                dv2 = dv2 + _dg(p.astype(BF16), dos[hh], True, False)
                dck_ref[0, j, hh:hh + 1, :] -= jnp.sum(ds, axis=0, keepdims=True)
            dk_ref[pl.ds(off, t), :] += dk2
            dv_ref[pl.ds(off, t), :] += dv2
            return tuple(out)

        init = tuple((jnp.zeros((t, 2 * hd), F32), jnp.zeros((t, 1), F32)) for _ in range(2))
        res = lax.fori_loop(0, i, lambda j, c: block(j, c, False), init)
        res = block(i, res, True)
        dq_ref[...] = jnp.where(masks[0], res[0][0], res[1][0])
        for hh in range(2):
            dcq_ref[0, :, hh:hh + 1] = res[hh][1]

    blk = pl.BlockSpec((t, 2 * hd), lambda hp, i: (i, hp))
    full = pl.BlockSpec((tokens, 2 * hd), lambda hp, i: (0, hp))
    cq_spec = pl.BlockSpec((1, t, 2), lambda hp, i: (hp, i, 0))
    ck_spec = pl.BlockSpec((1, nb, 2, t), lambda hp, i: (hp, 0, 0, 0))
    tshape = jax.ShapeDtypeStruct((tokens, width), F32)
    return pl.pallas_call(
        body, name="fox_bwd", grid=(npair, nb),
        in_specs=[blk, full, full, cq_spec, ck_spec, blk, cq_spec, blk],
        out_specs=[blk, full, full, ck_spec, cq_spec],
        out_shape=[tshape, tshape, tshape, jax.ShapeDtypeStruct((npair, nb, 2, t), F32),
                   jax.ShapeDtypeStruct((npair, tokens, 2), F32)],
        compiler_params=pltpu.CompilerParams(dimension_semantics=("parallel", "arbitrary")),
    )(q, k, v, cq, ck, o, lse, do)


@jax.custom_vjp
def fox_attention(q, k, v, cum):
    return _fox_fwd(q, k, v, cum)[0]


def _fox_fwd(q, k, v, cum):
    cq, ck = _fox_layouts(cum)
    q, k, v = q.astype(BF16), k.astype(BF16), v.astype(BF16)
    o, lse = _fox_fwd_call(q, k, v, cq, ck)
    return o, (q, k, v, cq, ck, o, lse)


def _fox_bwd(res, do):
    q, k, v, cq, ck, o, lse = res
    dq, dk, dv, dck, dcq = _fox_bwd_call(q, k, v, cq, ck, o, lse, do)
    npair, nb, _, t = dck.shape
    dcum = dck.transpose(0, 2, 1, 3).reshape(2 * npair, nb * t).T + dcq.transpose(1, 0, 2).reshape(nb * t, 2 * npair)
    return dq, dk, dv, dcum


fox_attention.defvjp(_fox_fwd, _fox_bwd)


def _loss_call(y, target):
    rows, d = y.shape
    tr = _row_tile(rows, d)

    def body(y_ref, t_ref, loss_ref, dy_ref):
        @pl.when(pl.program_id(0) == 0)
        def _():
            loss_ref[...] = jnp.zeros_like(loss_ref)

        diff = y_ref[...] - t_ref[...]
        dy_ref[...] = diff * (1.0 / d)
        loss_ref[...] += (0.5 / d) * jnp.sum(jnp.sum(diff * diff, axis=1, keepdims=True), axis=0, keepdims=True)

    return pl.pallas_call(
        body, name="loss", grid=(rows // tr,),
        in_specs=[pl.BlockSpec((tr, d), lambda i: (i, 0))] * 2,
        out_specs=[pl.BlockSpec((1, 1), lambda i: (0, 0)), pl.BlockSpec((tr, d), lambda i: (i, 0))],
        out_shape=[jax.ShapeDtypeStruct((1, 1), F32), jax.ShapeDtypeStruct((rows, d), F32)],
        compiler_params=pltpu.CompilerParams(dimension_semantics=("arbitrary",)),
    )(y, target)


def _adamw_call(w, g, m, v):
    rows, cols = w.shape
    tr = _row_tile_ragged(rows, cols, budget=1024 * 1024)
    c1 = 1.0 / (1.0 - ADAM_B1 ** ADAM_STEP)
    c2 = 1.0 / (1.0 - ADAM_B2 ** ADAM_STEP)

    def body(w_ref, g_ref, m_ref, v_ref, d_ref, nm_ref, nv_ref):
        gv = g_ref[...]
        nm = ADAM_B1 * m_ref[...] + (1.0 - ADAM_B1) * gv
        nv = ADAM_B2 * v_ref[...] + (1.0 - ADAM_B2) * (gv * gv)
        nm_ref[...] = nm
        nv_ref[...] = nv
        d_ref[...] = -ADAM_LR * ((nm * c1) / (jnp.sqrt(nv * c2) + ADAM_EPS) + ADAM_WD * w_ref[...])

    spec = pl.BlockSpec((tr, cols), lambda i: (i, 0))
    shape = jax.ShapeDtypeStruct((rows, cols), F32)
    return pl.pallas_call(
        body, name="adamw", grid=(pl.cdiv(rows, tr),),
        in_specs=[spec] * 4, out_specs=[spec] * 3, out_shape=[shape] * 3,
        compiler_params=pltpu.CompilerParams(dimension_semantics=("parallel",)),
    )(w, g, m, v)


def _my_place():
    return lax.axis_index("x"), lax.axis_index("y"), lax.axis_index("c")


def _place_index(px, py, pc):
    return 4 * px + 2 * py + pc


HBM_SPEC = pl.BlockSpec(memory_space=pltpu.HBM)


def _all_gather_call(block):
    def body(x_ref, out_ref, send_sems, recv_sems, local_sem):
        x, y, c = _my_place()
        me, sibling = (x, y, c), (x, y, 1 - c)
        chips = [(1 - x, y), (x, 1 - y), (1 - x, 1 - y)]

        def slot(px, py, pc):
            return out_ref.at[_place_index(px, py, pc)]

        def copy(k, blk, to, src=None):
            return pltpu.make_async_remote_copy(
                src_ref=slot(*blk) if src is None else src, dst_ref=slot(*blk),
                send_sem=send_sems.at[k], recv_sem=recv_sems.at[k],
                device_id=to, device_id_type=pl.DeviceIdType.MESH)

        mine = pltpu.make_async_copy(x_ref, slot(*me), local_sem)
        mine.start()
        first = [copy(0, me, sibling, src=x_ref)]
        first += [copy(1 + j, me, (*chip, c), src=x_ref) for j, chip in enumerate(chips)]
        for cp in first:
            cp.start()
        passed = [copy(4 + j, (*chip, c), sibling) for j, chip in enumerate(chips)]
        for j, chip in enumerate(chips):
            copy(1 + j, (*chip, c), me).wait_recv()
            passed[j].start()
        copy(0, sibling, me).wait_recv()
        for j, chip in enumerate(chips):
            copy(4 + j, (*chip, 1 - c), me).wait_recv()
        for cp in first + passed:
            cp.wait_send()
        mine.wait()

    return pl.pallas_call(
        body, name="all_gather",
        out_shape=jax.ShapeDtypeStruct((N_DEV,) + block.shape, block.dtype),
        in_specs=[HBM_SPEC], out_specs=HBM_SPEC,
        scratch_shapes=[pltpu.SemaphoreType.DMA((7,)), pltpu.SemaphoreType.DMA((7,)), pltpu.SemaphoreType.DMA],
    )(block)


SEM_SPEC = pl.BlockSpec(memory_space=pltpu.SEMAPHORE)
SIDE_EFFECT = pltpu.SideEffectType.DATAFLOW_SIDE_EFFECTING


def _peers():
    x, y, c = _my_place()
    out = []
    for k in range(1, N_DEV):
        peer = (x ^ (k >> 2), y ^ ((k >> 1) & 1), c ^ (k & 1))
        out.append((k - 1, peer, _place_index(*peer)))
    return _place_index(x, y, c), out


def _spread_start(src, per_peer, name, after=None):
    slot = src.shape[1:] if per_peer else src.shape
    order = () if after is None else (after,)

    def body(src_ref, land_ref, *rest):
        send_sems, recv_sems, src_thru, land_thru, token = rest[len(order):]
        mine, peers = _peers()
        for k, peer, peer_idx in peers:
            pltpu.make_async_remote_copy(
                src_ref=src_ref.at[peer_idx] if per_peer else src_ref, dst_ref=land_ref.at[mine],
                send_sem=send_sems.at[k], recv_sem=recv_sems.at[k],
                device_id=peer, device_id_type=pl.DeviceIdType.MESH).start()
        token[...] = jnp.zeros_like(token)

    return pl.pallas_call(
        body, name=name,
        out_shape=(pltpu.SemaphoreType.DMA((N_DEV - 1,)), pltpu.SemaphoreType.DMA((N_DEV - 1,)),
                   pltpu.HBM(src.shape, src.dtype), pltpu.HBM((N_DEV,) + slot, src.dtype),
                   jax.ShapeDtypeStruct((8, 128), F32)),
        in_specs=(HBM_SPEC, HBM_SPEC) + (pl.BlockSpec(memory_space=pl.ANY),) * len(order),
        out_specs=(SEM_SPEC, SEM_SPEC, HBM_SPEC, HBM_SPEC, pl.BlockSpec(memory_space=pltpu.VMEM)),
        input_output_aliases={0: 2, 1: 3},
        compiler_params=pltpu.CompilerParams(has_side_effects=SIDE_EFFECT),
    )(pltpu.with_memory_space_constraint(src, pltpu.HBM),
      pltpu.with_memory_space_constraint(lax.empty((N_DEV,) + slot, src.dtype), pltpu.HBM), *order)


def _spread_wait(handles, after, per_peer, name):
    send_sems, recv_sems, src_thru, land_thru = handles

    def body(src_ref, land_ref, send_sems, recv_sems, after_ref, src_dead, got_ref):
        _, peers = _peers()
        for k, peer, peer_idx in peers:
            copy = pltpu.make_async_remote_copy(
                src_ref=src_ref.at[peer_idx] if per_peer else src_ref, dst_ref=land_ref.at[peer_idx],
                send_sem=send_sems.at[k], recv_sem=recv_sems.at[k],
                device_id=peer, device_id_type=pl.DeviceIdType.MESH)
            copy.wait_send()
            copy.wait_recv()

    return pl.pallas_call(
        body, name=name,
        out_shape=(pltpu.HBM(src_thru.shape, src_thru.dtype), pltpu.HBM(land_thru.shape, land_thru.dtype)),
        in_specs=(HBM_SPEC, HBM_SPEC, SEM_SPEC, SEM_SPEC, pl.BlockSpec(memory_space=pl.ANY)),
        out_specs=(HBM_SPEC, HBM_SPEC), input_output_aliases={0: 0, 1: 1},
        compiler_params=pltpu.CompilerParams(has_side_effects=SIDE_EFFECT),
    )(src_thru, land_thru, send_sems, recv_sems, after)


def _sum_slots_call(slots):
    _, rows, cols = slots.shape
    tr = _row_tile_ragged(rows, cols, budget=512 * 1024)

    def body(s_ref, o_ref):
        acc = s_ref[0].astype(F32)
        for j in range(1, N_DEV):
            acc = acc + s_ref[j].astype(F32)
        o_ref[...] = acc

    return pl.pallas_call(
        body, name="sum_slots", grid=(pl.cdiv(rows, tr),),
        in_specs=[pl.BlockSpec((N_DEV, tr, cols), lambda i: (0, i, 0))],
        out_specs=pl.BlockSpec((tr, cols), lambda i: (i, 0)),
        out_shape=jax.ShapeDtypeStruct((rows, cols), F32),
        compiler_params=pltpu.CompilerParams(dimension_semantics=("parallel",)),
    )(slots)


def _sum_adamw_call(got, own, w, m, v):
    rows, cols = w.shape
    tr = _row_tile_ragged(rows, cols, budget=512 * 1024)
    c1 = 1.0 / (1.0 - ADAM_B1 ** ADAM_STEP)
    c2 = 1.0 / (1.0 - ADAM_B2 ** ADAM_STEP)

    def body(got_ref, own_ref, w_ref, m_ref, v_ref, g_ref, d_ref, nm_ref, nv_ref):
        mine = _place_index(*_my_place())
        gv = jnp.zeros(w_ref.shape, F32)
        for j in range(N_DEV):
            gv = gv + jnp.where(mine == j, own_ref[...], got_ref[j]).astype(F32)
        nm = ADAM_B1 * m_ref[...] + (1.0 - ADAM_B1) * gv
        nv = ADAM_B2 * v_ref[...] + (1.0 - ADAM_B2) * (gv * gv)
        g_ref[...] = gv
        nm_ref[...] = nm
        nv_ref[...] = nv
        d_ref[...] = -ADAM_LR * ((nm * c1) / (jnp.sqrt(nv * c2) + ADAM_EPS) + ADAM_WD * w_ref[...])

    spec = pl.BlockSpec((tr, cols), lambda i: (i, 0))
    shape = jax.ShapeDtypeStruct((rows, cols), F32)
    return pl.pallas_call(
        body, name="sum_adamw", grid=(pl.cdiv(rows, tr),),
        in_specs=[pl.BlockSpec((N_DEV, tr, cols), lambda i: (0, i, 0))] + [spec] * 4,
        out_specs=[spec] * 4, out_shape=[shape] * 4,
        compiler_params=pltpu.CompilerParams(dimension_semantics=("parallel",)),
    )(got, own, w, m, v)


def _with_own_slot(got, own, mine):
    return lax.dynamic_update_index_in_dim(got, own, mine, 0)


def _pack(vectors, width):
    flat = jnp.concatenate([v.reshape(-1) for v in vectors])
    return jnp.pad(flat, (0, width - flat.shape[0])).reshape(width // 128, 128)


def _unpack(packed, like):
    flat = packed.reshape(-1)
    out, at = [], 0
    for v in like:
        out.append(flat[at:at + v.size].reshape(v.shape))
        at += v.size
    return tuple(out)


def _cols_from_slots(slots):
    n, rows, cols = slots.shape
    return slots.transpose(1, 0, 2).reshape(rows, n * cols)


def _rows_from_slots(slots):
    return slots.reshape(-1, slots.shape[2])


def _pad128(n):
    return -(-n // 128) * 128


def _pad_to_tiles(a, axis):
    n = a.shape[axis]
    pads = [(0, 0)] * a.ndim
    pads[axis] = (0, _pad128(n) - n)
    return jnp.pad(a, pads)


def _rwkv_group(take, zeros, rw, dl, al, gl):
    at = 3 * rw
    parts = take(0, at)
    for n in (dl, al, gl):
        parts += take(at, at + n)
        if _pad128(n) > n:
            parts.append(zeros(_pad128(n) - n))
        at += n
    return parts


def _in_proj_layout(slots, rw, fw, dl, al, gl, whole):
    n_slots, rows, d = slots.shape
    wt = slots.reshape(n_slots * rows, d)
    take = lambda lo, hi: [wt[lo:hi]]
    zeros = lambda n: jnp.zeros((n, d), wt.dtype)
    rcols = 3 * rw + dl + al + gl
    fcols = 3 * fw + fw // HEAD_DIM
    group_r = _rwkv_group(take, zeros, rw, dl, al, gl)
    group_f = take(rcols, rcols + fcols) + ([zeros(_pad128(fcols) - fcols)] if _pad128(fcols) > fcols else [])
    group_g = take(rcols + fcols, n_slots * rows)
    if whole:
        return jnp.concatenate(group_r + group_f + group_g, axis=0)
    return tuple(jnp.concatenate(g, axis=0) for g in (group_r, group_f, group_g))


def _low_rank_layout(slots):
    return _pad_to_tiles(_cols_from_slots(slots), 0)


def _stage_embed(meta, x, n1, lp):
    h0 = jnp.concatenate([meta, x, jnp.zeros((lp - meta.shape[0] - x.shape[0], x.shape[1]), F32)], axis=0)
    return h0, rmsnorm(h0, n1)


def _stage_mix(z_r, z_f, small, w2, a2, g2, dims):
    (mu, w0, a0, k_k, k_a, r_k, gn_w, gn_b, q_g, k_g, f_bias) = small
    rw, fw, dl, al, gl = dims
    fcols = 3 * fw + fw // HEAD_DIM

    mu_group = jnp.concatenate(_rwkv_group(lambda lo, hi: [mu[:, lo:hi]], lambda n: jnp.zeros((1, n), F32), rw, dl, al, gl), axis=1)
    r, lw, kf, v, na, b, g = rwkv_prep(z_r, mu_group, w0, a0, k_k, k_a, w2, a2, g2)
    y = wkv7(r, lw, kf, v, na, b)
    y_a = gn_bonus(y, r, kf, v, g, gn_w, gn_b, r_k.reshape(1, rw))

    fq, fk, fv, fl = z_f[:, :fw], z_f[:, fw:2 * fw], z_f[:, 2 * fw:3 * fw], z_f[:, 3 * fw:fcols]
    fq = head_rms(fq, jnp.tile(q_g, (1, fw // HEAD_DIM)) * (HEAD_DIM ** -0.5))
    fk = head_rms(fk, jnp.tile(k_g, (1, fw // HEAD_DIM)))
    cum = jnp.cumsum(jax.nn.log_sigmoid(badd(fl, f_bias)), axis=0)
    y_b = fox_attention(fq, fk, fv, cum)
    return y_a, y_b


def _stage_merge(h0, y_a, y_b, z_g, w_a, w_b, w_o):
    merged = gated_merge(z_g, dense_cols_bf16(y_a, w_a), dense_cols_bf16(y_b, w_b))
    return dense_add(merged, w_o, h0)


def _stage_ffn(h1, n2, w_gu, w_dn):
    return dense_add(swiglu(dense_cols_bf16(rmsnorm(h1, n2), w_gu)), w_dn, h1)


SHARDED = ("meta_tokens", "w_in", "rwkv_w2", "rwkv_a2", "rwkv_g2", "w_branch_a", "w_branch_b", "w_o", "w_gate_up", "w_down")
LOW_RANK = ("rwkv_w2", "rwkv_a2", "rwkv_g2")
SMALL = ("norm1_g", "rwkv_mu", "rwkv_w0", "rwkv_a0", "rwkv_k_k", "rwkv_k_a", "rwkv_r_k", "rwkv_gn_w", "rwkv_gn_b",
         "fox_q_norm_g", "fox_k_norm_g", "fox_f_bias", "norm2_g")
WEIGHTS = ("meta_tokens", "norm1_g", "w_in", "rwkv_mu", "rwkv_w0", "rwkv_w2", "rwkv_a0", "rwkv_a2", "rwkv_g2", "rwkv_k_k",
           "rwkv_k_a", "rwkv_r_k", "rwkv_gn_w", "rwkv_gn_b", "fox_q_norm_g", "fox_k_norm_g", "fox_f_bias", "w_branch_a",
           "w_branch_b", "w_o", "norm2_g", "w_gate_up", "w_down")


def _as2d(a):
    return a.reshape(-1, a.shape[-1])


def kernel(x, meta_tokens, norm1_g, w_in, rwkv_mu, rwkv_w0, rwkv_w2, rwkv_a0, rwkv_a2, rwkv_g2, rwkv_k_k, rwkv_k_a, rwkv_r_k, rwkv_gn_w, rwkv_gn_b, fox_q_norm_g, fox_k_norm_g, fox_f_bias, w_branch_a, w_branch_b, w_o, norm2_g, w_gate_up, w_down, loss_target, m_meta_tokens, m_norm1_g, m_w_in, m_rwkv_mu, m_rwkv_w0, m_rwkv_w2, m_rwkv_a0, m_rwkv_a2, m_rwkv_g2, m_rwkv_k_k, m_rwkv_k_a, m_rwkv_r_k, m_rwkv_gn_w, m_rwkv_gn_b, m_fox_q_norm_g, m_fox_k_norm_g, m_fox_f_bias, m_w_branch_a, m_w_branch_b, m_w_o, m_norm2_g, m_w_gate_up, m_w_down, v_meta_tokens, v_norm1_g, v_w_in, v_rwkv_mu, v_rwkv_w0, v_rwkv_w2, v_rwkv_a0, v_rwkv_a2, v_rwkv_g2, v_rwkv_k_k, v_rwkv_k_a, v_rwkv_r_k, v_rwkv_gn_w, v_rwkv_gn_b, v_fox_q_norm_g, v_fox_k_norm_g, v_fox_f_bias, v_w_branch_a, v_w_branch_b, v_w_o, v_norm2_g, v_w_gate_up, v_w_down):
    given = dict(locals())
    w = {n: given[n] for n in WEIGHTS}
    assert rwkv_r_k.shape[-1] == HEAD_DIM
    n_meta, seq = meta_tokens.shape[0], x.shape[1]
    tokens = n_meta + seq
    lp = -(-tokens // TOKEN_TILE) * TOKEN_TILE
    mine = _place_index(*(lax.axis_index(a) for a in MESH_AXES))
    x2 = x[0]

    local = {n: _as2d(given[n]) for n in given if n != "x" and n != "loss_target"}
    for n in ("w_in", "m_w_in", "v_w_in"):
        local[n] = jnp.transpose(given[n][0])
    blocks = {n: local[n].astype(F32 if n == "meta_tokens" else BF16) for n in SHARDED}
    for prefix in ("", "m_", "v_"):
        local[prefix + "low_rank"] = jnp.concatenate([local[prefix + n] for n in LOW_RANK], axis=0)
    blocks["low_rank"] = jnp.concatenate([blocks[n] for n in LOW_RANK], axis=0)
    low_rank_ends = [sum(local[n].shape[0] for n in LOW_RANK[:i + 1]) for i in range(len(LOW_RANK))]
    low_rank_rows = lambda a, axis: [lax.slice_in_dim(a, lo, hi, axis=axis) for lo, hi in zip([0] + low_rank_ends, low_rank_ends)]
    first = ("meta_tokens", "low_rank")
    started = {n: _spread_start(blocks[n], False, "gather_start_" + n) for n in first}
    zero = sum(started[n][4][0, 0] for n in first)

    def gathered(n, after):
        own, got = _spread_wait(started[n][:4], after, False, "gather_wait_" + n)
        return _with_own_slot(got, own, mine)

    sm = {n: _as2d(w[n]) for n in SMALL}
    small_mix = tuple(sm[n] for n in SMALL[1:-1])
    n1 = sm["norm1_g"] + zero
    rw, fw = w_branch_a.shape[-2], w_branch_b.shape[-2]
    dims = (rw, fw, rwkv_w2.shape[-2], rwkv_a2.shape[-2], rwkv_g2.shape[-2])
    same = lambda s: (s,)

    meta, un_meta = jax.vjp(_cols_from_slots, gathered("meta_tokens", x2))
    (h0, xn), vjp_embed = jax.vjp(lambda m, xs, g: _stage_embed(m, xs, g, lp), meta, x2, n1)
    in_slots = _all_gather_call(blocks["w_in"])
    later = [n for n in SHARDED if n not in first and n not in LOW_RANK and n != "w_in"]
    started.update({n: _spread_start(blocks[n], False, "gather_start_" + n, after=in_slots) for n in later})
    w_groups = _in_proj_layout(in_slots, *dims, whole=False)
    w_cat, un_in = jax.vjp(lambda s: _in_proj_layout(s, *dims, whole=True), in_slots)
    xn_b = xn.astype(BF16)
    behind = sum(started[n][4] for n in later)
    z_r, z_f, z_g = (_matmul(xn_b, wg, tb=True, name="in_proj_" + tag, after=behind, out_dtype=BF16 if tag == "g" else F32)
                     for wg, tag in zip(w_groups, "rfg"))
    (w2, un_w2), (a2, un_a2), (g2, un_g2) = (jax.vjp(_low_rank_layout, s) for s in low_rank_rows(gathered("low_rank", xn), 1))
    (y_a, y_b), vjp_mix = jax.vjp(lambda zr, zf, s, a, b, c: _stage_mix(zr, zf, s, a, b, c, dims),
                                  z_r, z_f, small_mix, w2, a2, g2)
    w_a, w_b = gathered("w_branch_a", y_a), gathered("w_branch_b", y_a)
    w_o_full, un_wo = jax.vjp(_rows_from_slots, gathered("w_o", y_a))
    h1, vjp_merge = jax.vjp(_stage_merge, h0, y_a, y_b, z_g, w_a, w_b, w_o_full)
    w_gu = gathered("w_gate_up", h1)
    w_dn, un_dn = jax.vjp(_rows_from_slots, gathered("w_down", h1))
    y, vjp_ffn = jax.vjp(_stage_ffn, h1, sm["norm2_g"], w_gu, w_dn)

    loss_part, dy_real = _loss_call(y[n_meta:tokens], loss_target[0])
    dy = jnp.pad(dy_real, ((n_meta, lp - tokens), (0, 0)))
    loss = lax.psum(loss_part[0, 0], MESH_AXES)

    sent = {}

    def send_grad(n, dmat, unlayout):
        sent[n] = _spread_start(unlayout(dmat)[0], True, "grad_start_" + n)
        return sent[n][4][0, 0]

    d_h1, d_n2, d_wgu, d_wdn = vjp_ffn(dy)
    behind = send_grad("w_gate_up", d_wgu, same) + send_grad("w_down", d_wdn, un_dn)
    d_h0, d_ya, d_yb, d_zg, d_wa, d_wb, d_wo = vjp_merge(d_h1 + behind)
    behind = send_grad("w_o", d_wo, un_wo) + send_grad("w_branch_a", d_wa, same) + send_grad("w_branch_b", d_wb, same)
    d_zr, d_zf, d_small_mix, d_w2, d_a2, d_g2 = vjp_mix((d_ya + behind.astype(d_ya.dtype), d_yb))
    dproj_b = jnp.concatenate([d_zr.astype(BF16), d_zf.astype(BF16), d_zg.astype(BF16)], axis=1)
    d_wcat = _matmul(dproj_b, xn_b, ta=True, out_dtype=BF16, name="in_proj_dw")
    send_grad("w_in", d_wcat, un_in)
    d_xn = _matmul(dproj_b, w_cat, out_dtype=BF16, name="in_proj_dx", after=sent["w_in"][4])
    send_grad("low_rank", jnp.concatenate([un_w2(d_w2)[0], un_a2(d_a2)[0], un_g2(d_g2)[0]], axis=1), same)
    d_meta, g_x, d_n1 = vjp_embed((d_h0, d_xn))
    send_grad("meta_tokens", d_meta, un_meta)

    small_grads = (d_n1, *d_small_mix, d_n2)
    n_small = sum(g.size for g in small_grads)
    width = -(-n_small // 1024) * 1024
    small_sent = _spread_start(_pack(small_grads, width), False, "small_grad_start")

    grads, delta, new_m, new_v = {}, {}, {}, {}
    after = g_x
    for n in ("w_gate_up", "w_down", "w_o", "w_branch_a", "w_branch_b", "low_rank", "meta_tokens", "w_in"):
        src, got = _spread_wait(sent[n][:4], after, True, "grad_wait_" + n)
        own = lax.dynamic_index_in_dim(src, mine, 0, keepdims=False)
        stepped = _sum_adamw_call(got, own, local[n], local["m_" + n], local["v_" + n])
        after = stepped[2]
        if n == "low_rank":
            for out, t in zip((grads, delta, new_m, new_v), stepped):
                out.update({name: part.reshape(w[name].shape) for name, part in zip(LOW_RANK, low_rank_rows(t, 0))})
            continue
        back = (lambda t: jnp.transpose(t)[None]) if n == "w_in" else (lambda t: t.reshape(w[n].shape))
        grads[n], delta[n], new_m[n], new_v[n] = (back(t) for t in stepped)
    own_small, got_small = _spread_wait(small_sent[:4], after, False, "small_grad_wait")
    small_total = _unpack(_sum_slots_call(_with_own_slot(got_small, own_small, mine)), small_grads)
    grads.update({n: g.reshape(w[n].shape) for n, g in zip(SMALL, small_total)})
    packs = [_pack([src[n] if p == "" else given[p + n] for n in SMALL], width)
             for p, src in (("", w), ("", grads), ("m_", None), ("v_", None))]
    like = [w[n] for n in SMALL]
    for out, packed in zip((delta, new_m, new_v), _adamw_call(*packs)):
        out.update(dict(zip(SMALL, _unpack(packed, like))))

    return (loss, g_x[None], *[grads[n] for n in WEIGHTS], *[delta[n] for n in WEIGHTS],
            *[new_m[n] for n in WEIGHTS], *[new_v[n] for n in WEIGHTS])
```

```python
import jax
import jax.numpy as jnp
from jax import lax
from jax.experimental import pallas as pl
from jax.experimental.pallas import tpu as pltpu

F32 = jnp.float32
BF16 = jnp.bfloat16

N_DEV = 8
MESH_AXES = ("x", "y", "c")
HEAD_DIM = 64
TOKEN_TILE = 128
WKV_CHUNK = 64
WKV_PAIRS_PER_STEP = 8
PAIR = 2 * HEAD_DIM
ATTN_BLOCK = 128
ATTN_BLOCK_BIG = 384
ATTN_PAIRS_PER_STEP = 2
RMS_EPS = 1e-6
GN_EPS = 64e-5
L2_FLOOR = 1e-12
NEG_BIG = -1e30
ADAM_LR, ADAM_B1, ADAM_B2, ADAM_EPS, ADAM_WD, ADAM_STEP = 0.001, 0.9, 0.999, 1e-08, 0.01, 10
VMEM_LIMIT_CAP = 56 * 1024 * 1024
VMEM_LIMIT_FLOOR = 32 * 1024 * 1024
MATMUL_VMEM_BUDGET = 36 * 1024 * 1024
GRID_STEP_BYTES = 1024 * 1024
ACC_BYTES_PER_HBM_BYTE = 6


def _vmem_limit(estimate_bytes):
    return int(min(max(estimate_bytes * 5 // 4, VMEM_LIMIT_FLOOR), VMEM_LIMIT_CAP))


def _row_tile(rows, width, itemsize=4, budget=2 * 1024 * 1024):
    for c in (1408, 1024, 704, 512, 384, 256, 128, 64, 32, 16, 8):
        if rows % c == 0 and c * width * itemsize <= budget:
            return c
    return rows


def _row_tile_ragged(rows, width, itemsize=4, budget=2 * 1024 * 1024):
    tile = _row_tile(rows, width, itemsize, budget)
    if tile * width * itemsize <= budget or rows < 16:
        return tile
    padded = -(-rows // 16) * 16
    for c in (1408, 1024, 704, 512, 384, 336, 256, 192, 128, 96, 64, 48, 32, 16):
        if padded % c == 0 and c * width * itemsize <= budget:
            return c
    return tile


def _dg(a, b, ta, tb):
    dims = (((0 if ta else 1,), (1 if tb else 0,)), ((), ()))
    return lax.dot_general(a, b, dims, preferred_element_type=F32)


def _split(x, n):
    parts = []
    for _ in range(n):
        h = x.astype(BF16)
        parts.append(h)
        x = x - h.astype(F32)
    return parts


def _mm(a, b, ta=False, tb=False):
    return _dg(a.astype(BF16), b.astype(BF16), ta, tb)


def _matmul(a, b, ta=False, tb=False, out_dtype=F32, name="matmul", after=None, b_slots=False, out_slots=0, add=None):
    if ta:
        kdim, m = a.shape
    else:
        m, kdim = a.shape
    if b_slots:
        n_slots, brows, bcols = b.shape
        n, k2 = (brows, n_slots * bcols) if tb else (n_slots * bcols, brows)
    elif tb:
        n, k2 = b.shape
    else:
        k2, n = b.shape
    assert kdim == k2, (a.shape, b.shape, ta, tb)
    sa, sb, so = a.dtype.itemsize, b.dtype.itemsize, jnp.dtype(out_dtype).itemsize
    n_unit = bcols if (b_slots and not tb) else (n // out_slots if out_slots else n)
    k_unit = bcols if (b_slots and tb) else kdim
    tm, tn, tk, n_outer = _matmul_tiles(m, n, kdim, ta, sa, sb, so, n_unit, k_unit)
    nk = kdim // tk
    ij = (lambda f: lambda j, i, k: f(i, j, k)) if n_outer else (lambda f: f)

    order = () if after is None else (after,)
    extra = () if add is None else (add,)

    def body(a_ref, b_ref, *rest):
        rest = rest[len(order):]
        add_ref = rest[0] if extra else None
        o_ref, acc = rest[len(extra)], rest[len(extra) + 1:]
        part = _dg(a_ref[...].astype(BF16), b_ref[...].astype(BF16), ta, tb)
        done = lambda total: (total if add_ref is None else total + add_ref[...]).astype(o_ref.dtype)
        if nk == 1:
            o_ref[...] = done(part)
            return
        kk = pl.program_id(2)

        @pl.when(kk == 0)
        def _():
            acc[0][...] = part

        @pl.when(kk > 0)
        def _():
            acc[0][...] += part

        @pl.when(kk == nk - 1)
        def _():
            o_ref[...] = done(acc[0][...])

    a_spec = pl.BlockSpec((tk, tm), ij(lambda i, j, k: (k, i))) if ta else pl.BlockSpec((tm, tk), ij(lambda i, j, k: (i, k)))
    if b_slots and tb:
        per = bcols // tk
        b_spec = pl.BlockSpec((None, tn, tk), ij(lambda i, j, k: (k // per, j, k % per)))
    elif b_slots:
        per = bcols // tn
        b_spec = pl.BlockSpec((None, tk, tn), ij(lambda i, j, k: (j // per, k, j % per)))
    elif tb:
        b_spec = pl.BlockSpec((tn, tk), ij(lambda i, j, k: (j, k)))
    else:
        b_spec = pl.BlockSpec((tk, tn), ij(lambda i, j, k: (k, j)))
    if out_slots:
        per_out = n // out_slots // tn
        out_spec = pl.BlockSpec((None, tm, tn), ij(lambda i, j, k: (j // per_out, i, j % per_out)))
        out_shape = jax.ShapeDtypeStruct((out_slots, m, n // out_slots), out_dtype)
    else:
        out_spec = pl.BlockSpec((tm, tn), ij(lambda i, j, k: (i, j)))
        out_shape = jax.ShapeDtypeStruct((m, n), out_dtype)
    return pl.pallas_call(
        body, name=name,
        grid=(n // tn, m // tm, nk) if n_outer else (m // tm, n // tn, nk),
        in_specs=[a_spec, b_spec] + [pl.BlockSpec(memory_space=pl.ANY)] * len(order)
                 + [pl.BlockSpec((tm, tn), ij(lambda i, j, k: (i, j)))] * len(extra),
        out_specs=out_spec,
        out_shape=out_shape,
        scratch_shapes=[pltpu.VMEM((tm, tn), F32)] if nk > 1 else [],
        compiler_params=pltpu.CompilerParams(
            dimension_semantics=("parallel", "parallel", "arbitrary"),
            vmem_limit_bytes=_vmem_limit(_matmul_vmem(tm, tn, tk, nk, sa, sb, so) + 2 * tm * tn * 4 * len(extra))),
    )(a, b, *order, *extra)


def _matmul_vmem(tm, tn, tk, nk, sa, sb, so):
    return 2 * (tm * tk * sa + tk * tn * sb + tm * tn * so) + tm * tn * 4 + (tm * tn * 4 if nk > 1 else 0)


def _matmul_tiles(m, n, kdim, ta, sa, sb, so, n_unit, k_unit):
    lane = (2816, 2176, 2048, 1408, 1024, 640, 512, 384, 256, 128)
    sublane = (2816, 2176, 2048, 1408, 1024, 704, 512, 384, 256, 128)
    divs = lambda dim, cands: [c for c in cands if dim % c == 0] or [dim]
    best = None
    for tm in divs(m, lane if ta else sublane):
        for tn in divs(n_unit, lane):
            for tk in divs(k_unit, sublane if ta else lane) + ([kdim] if k_unit == kdim and (ta or kdim <= 2048) else []):
                nk, nm, nn = kdim // tk, m // tm, n // tn
                if _matmul_vmem(tm, tn, tk, nk, sa, sb, so) > MATMUL_VMEM_BUDGET:
                    continue
                acc_bytes = m * n * 4 * 3 * nk // ACC_BYTES_PER_HBM_BYTE if nk > 1 else 0
                fixed = m * n * so + acc_bytes + nm * nn * nk * GRID_STEP_BYTES
                for n_outer in (False, True):
                    if n_outer:
                        a_reads, b_reads = (1 if (nk == 1 and nm == 1) else nn), (1 if nk == 1 else nm)
                    else:
                        a_reads, b_reads = (1 if nk == 1 else nn), (1 if (nk == 1 and nn == 1) else nm)
                    cost = m * kdim * sa * a_reads + kdim * n * sb * b_reads + fixed
                    if best is None or cost < best[0]:
                        best = (cost, tm, tn, tk, n_outer)
    return best[1:]


@jax.custom_vjp
def dense(x, w):
    return _matmul(x.astype(BF16), w, name="dense_fwd")


def _dense_fwd(x, w):
    return _matmul(x.astype(BF16), w, name="dense_fwd"), (x.astype(BF16), w, jnp.zeros((), x.dtype))


def _dense_bwd(res, dy):
    xb, w, like = res
    dyb = dy.astype(BF16)
    dx = _matmul(dyb, w, tb=True, out_dtype=like.dtype, name="dense_dx")
    dw = _matmul(xb, dyb, ta=True, out_dtype=w.dtype, name="dense_dw")
    return dx, dw


dense.defvjp(_dense_fwd, _dense_bwd)


@jax.custom_vjp
def dense_add(x, w, res):
    return _matmul(x.astype(BF16), w, name="dense_add_fwd", add=res)


def _dense_add_fwd(x, w, res):
    return _matmul(x.astype(BF16), w, name="dense_add_fwd", add=res), (x.astype(BF16), w, jnp.zeros((), x.dtype))


def _dense_add_bwd(res, dy):
    return (*_dense_bwd(res, dy), dy)


dense_add.defvjp(_dense_add_fwd, _dense_add_bwd)


def _make_dense_cols(out_dtype):
    @jax.custom_vjp
    def op(x, w_slots):
        return _matmul(x.astype(BF16), w_slots, b_slots=True, out_dtype=out_dtype, name="dense_cols_fwd")

    def fwd(x, w_slots):
        xb = x.astype(BF16)
        return (_matmul(xb, w_slots, b_slots=True, out_dtype=out_dtype, name="dense_cols_fwd"),
                (xb, w_slots, jnp.zeros((), x.dtype)))

    def bwd(res, dy):
        xb, w_slots, like = res
        dyb = dy.astype(BF16)
        dx = _matmul(dyb, w_slots, tb=True, b_slots=True, out_dtype=like.dtype, name="dense_cols_dx")
        dw = _matmul(xb, dyb, ta=True, out_slots=w_slots.shape[0], out_dtype=w_slots.dtype, name="dense_cols_dw")
        return dx, dw

    op.defvjp(fwd, bwd)
    return op


dense_cols_bf16 = _make_dense_cols(BF16)


def _swiglu_call(gu, d_act=None):
    rows, two_f = gu.shape
    f = two_f // 2
    tr = _row_tile(rows, two_f, itemsize=2, budget=3 * 1024 * 1024)
    half = lambda j: pl.BlockSpec((tr, f), lambda i, j=j: (i, j))
    ops = (gu, gu) if d_act is None else (gu, gu, d_act)

    def body(*refs):
        g, u = refs[0][...].astype(F32), refs[1][...].astype(F32)
        s = 1.0 / (1.0 + jnp.exp(-g))
        if d_act is None:
            refs[2][...] = (g * s * u).astype(BF16)
        else:
            d = refs[2][...].astype(F32)
            refs[3][:, :f] = (d * u * s * (1.0 + g * (1.0 - s))).astype(BF16)
            refs[3][:, f:] = (d * g * s).astype(BF16)

    width = f if d_act is None else two_f
    return pl.pallas_call(
        body, name="swiglu_fwd" if d_act is None else "swiglu_bwd", grid=(rows // tr,),
        in_specs=[half(0), half(1)] + ([half(0)] if d_act is not None else []),
        out_specs=pl.BlockSpec((tr, width), lambda i: (i, 0)),
        out_shape=jax.ShapeDtypeStruct((rows, width), BF16),
        compiler_params=pltpu.CompilerParams(dimension_semantics=("parallel",)),
    )(*ops)


@jax.custom_vjp
def swiglu(gu):
    return _swiglu_call(gu)


swiglu.defvjp(lambda gu: (_swiglu_call(gu), gu), lambda gu, d_act: (_swiglu_call(gu, d_act),))


def _merge_call(zg, a, b, dm=None):
    rows, d = a.shape
    tr = _row_tile(rows, d, budget=1024 * 1024)
    half = lambda j: pl.BlockSpec((tr, d), lambda i, j=j: (i, j))
    tile = half(0)

    def body(*refs):
        ga = 1.0 / (1.0 + jnp.exp(-refs[0][...].astype(F32)))
        gb = 1.0 / (1.0 + jnp.exp(-refs[1][...].astype(F32)))
        av, bv = refs[2][...].astype(F32), refs[3][...].astype(F32)
        if dm is None:
            refs[4][...] = (ga * av + gb * bv).astype(BF16)
        else:
            dv = refs[4][...].astype(F32)
            dzg_ref, da_ref, db_ref = refs[5:]
            dzg_ref[:, :d] = (dv * av * ga * (1.0 - ga)).astype(dzg_ref.dtype)
            dzg_ref[:, d:] = (dv * bv * gb * (1.0 - gb)).astype(dzg_ref.dtype)
            da_ref[...] = (dv * ga).astype(BF16)
            db_ref[...] = (dv * gb).astype(BF16)

    shape_b = jax.ShapeDtypeStruct((rows, d), BF16)
    if dm is None:
        out_specs, out_shape, ops = tile, shape_b, (zg, zg, a, b)
    else:
        out_specs = [pl.BlockSpec((tr, 2 * d), lambda i: (i, 0)), tile, tile]
        out_shape = [jax.ShapeDtypeStruct((rows, 2 * d), zg.dtype), shape_b, shape_b]
        ops = (zg, zg, a, b, dm)
    return pl.pallas_call(
        body, name="merge_fwd" if dm is None else "merge_bwd", grid=(rows // tr,),
        in_specs=[half(0), half(1)] + [tile] * (len(ops) - 2),
        out_specs=out_specs, out_shape=out_shape,
        compiler_params=pltpu.CompilerParams(dimension_semantics=("parallel",)),
    )(*ops)


@jax.custom_vjp
def gated_merge(zg, a, b):
    return _merge_call(zg, a, b)


gated_merge.defvjp(lambda zg, a, b: (_merge_call(zg, a, b), (zg, a, b)),
                   lambda res, dm: tuple(_merge_call(*res, dm)))


def _rms_fwd_call(x, g):
    rows, d = x.shape
    tr = _row_tile(rows, d)

    def body(x_ref, g_ref, y_ref):
        xv = x_ref[...]
        rstd = lax.rsqrt(jnp.mean(xv * xv, axis=1, keepdims=True) + RMS_EPS)
        y_ref[...] = ((xv * rstd) * g_ref[...]).astype(BF16)

    return pl.pallas_call(
        body, name="rms_fwd", grid=(rows // tr,),
        in_specs=[pl.BlockSpec((tr, d), lambda i: (i, 0)), pl.BlockSpec((1, d), lambda i: (0, 0))],
        out_specs=pl.BlockSpec((tr, d), lambda i: (i, 0)),
        out_shape=jax.ShapeDtypeStruct((rows, d), BF16),
        compiler_params=pltpu.CompilerParams(dimension_semantics=("parallel",)),
    )(x, g)


def _rms_bwd_call(x, g, dy):
    rows, d = x.shape
    tr = _row_tile(rows, d)

    def body(x_ref, g_ref, dy_ref, dx_ref, dg_ref):
        @pl.when(pl.program_id(0) == 0)
        def _():
            dg_ref[...] = jnp.zeros_like(dg_ref)

        xv = x_ref[...]
        dyv = dy_ref[...].astype(F32)
        rstd = lax.rsqrt(jnp.mean(xv * xv, axis=1, keepdims=True) + RMS_EPS)
        xhat = xv * rstd
        dxhat = dyv * g_ref[...]
        dx_ref[...] = rstd * (dxhat - xhat * jnp.mean(dxhat * xhat, axis=1, keepdims=True))
        dg_ref[...] += jnp.sum(dyv * xhat, axis=0, keepdims=True)

    return pl.pallas_call(
        body, name="rms_bwd", grid=(rows // tr,),
        in_specs=[pl.BlockSpec((tr, d), lambda i: (i, 0)), pl.BlockSpec((1, d), lambda i: (0, 0)),
                  pl.BlockSpec((tr, d), lambda i: (i, 0))],
        out_specs=[pl.BlockSpec((tr, d), lambda i: (i, 0)), pl.BlockSpec((1, d), lambda i: (0, 0))],
        out_shape=[jax.ShapeDtypeStruct((rows, d), F32), jax.ShapeDtypeStruct((1, d), F32)],
        compiler_params=pltpu.CompilerParams(dimension_semantics=("arbitrary",)),
    )(x, g, dy)


@jax.custom_vjp
def rmsnorm(x, g):
    return _rms_fwd_call(x, g)


rmsnorm.defvjp(lambda x, g: (_rms_fwd_call(x, g), (x, g)), lambda res, dy: tuple(_rms_bwd_call(res[0], res[1], dy)))


def _bcast_add_call(x, p):
    rows, d = x.shape
    tr = _row_tile(rows, d)

    def body(x_ref, p_ref, y_ref):
        y_ref[...] = x_ref[...] + p_ref[...]

    return pl.pallas_call(
        body, name="bcast_add", grid=(rows // tr,),
        in_specs=[pl.BlockSpec((tr, d), lambda i: (i, 0)), pl.BlockSpec((1, d), lambda i: (0, 0))],
        out_specs=pl.BlockSpec((tr, d), lambda i: (i, 0)),
        out_shape=jax.ShapeDtypeStruct((rows, d), F32),
        compiler_params=pltpu.CompilerParams(dimension_semantics=("parallel",)),
    )(x, p)


def _colsum_call(a):
    rows, d = a.shape
    tr = _row_tile(rows, d)

    def body(a_ref, o_ref):
        @pl.when(pl.program_id(0) == 0)
        def _():
            o_ref[...] = jnp.zeros_like(o_ref)

        o_ref[...] += jnp.sum(a_ref[...], axis=0, keepdims=True)

    return pl.pallas_call(
        body, name="colsum", grid=(rows // tr,),
        in_specs=[pl.BlockSpec((tr, d), lambda i: (i, 0))],
        out_specs=pl.BlockSpec((1, d), lambda i: (0, 0)),
        out_shape=jax.ShapeDtypeStruct((1, d), F32),
        compiler_params=pltpu.CompilerParams(dimension_semantics=("arbitrary",)),
    )(a)


@jax.custom_vjp
def badd(x, p):
    return _bcast_add_call(x, p)


badd.defvjp(lambda x, p: (_bcast_add_call(x, p), None), lambda res, dy: (dy, _colsum_call(dy)))


def _head_sums(x):
    i = lax.broadcasted_iota(jnp.int32, (PAIR, PAIR), 0) // HEAD_DIM
    j = lax.broadcasted_iota(jnp.int32, (PAIR, PAIR), 1) // HEAD_DIM
    ones = jnp.where(i == j, 1.0, 0.0).astype(BF16)
    hi, lo = _split(x, 2)
    cols = [slice(p * PAIR, (p + 1) * PAIR) for p in range(x.shape[1] // PAIR)]
    return jnp.concatenate([_dg(hi[:, c], ones, False, False) + _dg(lo[:, c], ones, False, False) for c in cols], axis=1)


def _head_rms_fwd_call(x, g):
    rows, w = x.shape
    tr = _row_tile(rows, w, budget=1024 * 1024)

    def body(x_ref, g_ref, y_ref):
        xv = x_ref[...]
        rstd = lax.rsqrt(_head_sums(xv * xv) * (1.0 / HEAD_DIM) + RMS_EPS)
        y_ref[...] = (xv * rstd) * g_ref[...]

    return pl.pallas_call(
        body, name="head_rms_fwd", grid=(rows // tr,),
        in_specs=[pl.BlockSpec((tr, w), lambda i: (i, 0)), pl.BlockSpec((1, w), lambda i: (0, 0))],
        out_specs=pl.BlockSpec((tr, w), lambda i: (i, 0)),
        out_shape=jax.ShapeDtypeStruct((rows, w), F32),
        compiler_params=pltpu.CompilerParams(dimension_semantics=("parallel",)),
    )(x, g)


def _head_rms_bwd_call(x, g, dy):
    rows, w = x.shape
    tr = _row_tile(rows, w, budget=1024 * 1024)

    def body(x_ref, g_ref, dy_ref, dx_ref, dg_ref):
        @pl.when(pl.program_id(0) == 0)
        def _():
            dg_ref[...] = jnp.zeros_like(dg_ref)

        xv, dyv = x_ref[...], dy_ref[...]
        rstd = lax.rsqrt(_head_sums(xv * xv) * (1.0 / HEAD_DIM) + RMS_EPS)
        xhat = xv * rstd
        dxhat = dyv * g_ref[...]
        dx_ref[...] = rstd * (dxhat - xhat * (_head_sums(dxhat * xhat) * (1.0 / HEAD_DIM)))
        dg_ref[...] += jnp.sum(dyv * xhat, axis=0, keepdims=True)

    return pl.pallas_call(
        body, name="head_rms_bwd", grid=(rows // tr,),
        in_specs=[pl.BlockSpec((tr, w), lambda i: (i, 0)), pl.BlockSpec((1, w), lambda i: (0, 0)),
                  pl.BlockSpec((tr, w), lambda i: (i, 0))],
        out_specs=[pl.BlockSpec((tr, w), lambda i: (i, 0)), pl.BlockSpec((1, w), lambda i: (0, 0))],
        out_shape=[jax.ShapeDtypeStruct((rows, w), F32), jax.ShapeDtypeStruct((1, w), F32)],
        compiler_params=pltpu.CompilerParams(dimension_semantics=("arbitrary",)),
    )(x, g, dy)


@jax.custom_vjp
def head_rms(x, g):
    return _head_rms_fwd_call(x, g)


head_rms.defvjp(lambda x, g: (_head_rms_fwd_call(x, g), (x, g)),
                lambda res, dy: tuple(_head_rms_bwd_call(res[0], res[1], dy)))


def _gn_fwd_call(y, r, kf, v, g, gw, gb, rk):
    rows, w = y.shape
    tr = _row_tile(rows, w, budget=512 * 1024)

    def body(y_ref, r_ref, kf_ref, v_ref, g_ref, gw_ref, gb_ref, rk_ref, o_ref):
        yv = y_ref[...]
        yc = yv - _head_sums(yv) * (1.0 / HEAD_DIM)
        rstd = lax.rsqrt(_head_sums(yc * yc) * (1.0 / HEAD_DIM) + GN_EPS)
        s = _head_sums(r_ref[...] * kf_ref[...] * rk_ref[...])
        o_ref[...] = (((yc * rstd) * gw_ref[...] + gb_ref[...] + s * v_ref[...]) * g_ref[...]).astype(BF16)

    tok = pl.BlockSpec((tr, w), lambda i: (i, 0))
    par = pl.BlockSpec((1, w), lambda i: (0, 0))
    return pl.pallas_call(
        body, name="gn_bonus_fwd", grid=(rows // tr,),
        in_specs=[tok] * 5 + [par] * 3, out_specs=tok,
        out_shape=jax.ShapeDtypeStruct((rows, w), BF16),
        compiler_params=pltpu.CompilerParams(dimension_semantics=("parallel",)),
    )(y, r, kf, v, g, gw, gb, rk)


def _gn_bwd_call(y, r, kf, v, g, gw, gb, rk, do):
    rows, w = y.shape
    tr = _row_tile(rows, w, budget=512 * 1024)

    def body(y_ref, r_ref, kf_ref, v_ref, g_ref, gw_ref, gb_ref, rk_ref, do_ref,
             dy_ref, dr_ref, dkf_ref, dv_ref, dg_ref, dgw_ref, dgb_ref, drk_ref):
        @pl.when(pl.program_id(0) == 0)
        def _():
            dgw_ref[...] = jnp.zeros_like(dgw_ref)
            dgb_ref[...] = jnp.zeros_like(dgb_ref)
            drk_ref[...] = jnp.zeros_like(drk_ref)

        yv, rv, kv, vv, rkv = y_ref[...], r_ref[...], kf_ref[...], v_ref[...], rk_ref[...]
        mean = lambda t: _head_sums(t) * (1.0 / HEAD_DIM)
        yc = yv - mean(yv)
        rstd = lax.rsqrt(mean(yc * yc) + GN_EPS)
        yhat = yc * rstd
        s = _head_sums(rv * kv * rkv)
        do = do_ref[...].astype(F32)
        dg_ref[...] = do * (yhat * gw_ref[...] + gb_ref[...] + s * vv)
        dov = do * g_ref[...]
        dyhat = dov * gw_ref[...]
        dy_ref[...] = rstd * (dyhat - mean(dyhat) - yhat * mean(dyhat * yhat))
        ds = _head_sums(dov * vv)
        dv_ref[...] = s * dov
        dr_ref[...] = ds * kv * rkv
        dkf_ref[...] = ds * rv * rkv
        dgw_ref[...] += jnp.sum(dov * yhat, axis=0, keepdims=True)
        dgb_ref[...] += jnp.sum(dov, axis=0, keepdims=True)
        drk_ref[...] += jnp.sum(ds * rv * kv, axis=0, keepdims=True)

    tok = pl.BlockSpec((tr, w), lambda i: (i, 0))
    par = pl.BlockSpec((1, w), lambda i: (0, 0))
    tshape = jax.ShapeDtypeStruct((rows, w), F32)
    pshape = jax.ShapeDtypeStruct((1, w), F32)
    return pl.pallas_call(
        body, name="gn_bonus_bwd", grid=(rows // tr,),
        in_specs=[tok] * 5 + [par] * 3 + [tok], out_specs=[tok] * 5 + [par] * 3,
        out_shape=[tshape] * 5 + [pshape] * 3,
        compiler_params=pltpu.CompilerParams(dimension_semantics=("arbitrary",)),
    )(y, r, kf, v, g, gw, gb, rk, do)


@jax.custom_vjp
def gn_bonus(y, r, kf, v, g, gw, gb, rk):
    return _gn_fwd_call(y, r, kf, v, g, gw, gb, rk)


def _gn_bwd(res, do):
    return tuple(_gn_bwd_call(*res, do))


gn_bonus.defvjp(lambda *a: (_gn_fwd_call(*a), a), _gn_bwd)


PREP_ROWS = 128


def _prep_segments(rw, lora_w, lora_a, lora_g):
    at = 3 * rw
    seg = {"r": (0, rw), "k": (rw, 2 * rw), "v": (2 * rw, 3 * rw)}
    for name, n in (("wd", lora_w), ("ad", lora_a), ("gd", lora_g)):
        seg[name] = (at, at + _pad128(n))
        at += _pad128(n)
    return seg, at


def _prep_shifted(z_ref, zlast_ref, mu_ref, seg, first_tile):
    lo, hi = seg
    zr = z_ref[:, lo:hi]
    rows = zr.shape[0]
    before = jnp.where(first_tile, 0.0, zlast_ref[7:8, lo:hi])
    row0 = lax.broadcasted_iota(jnp.int32, zr.shape, 0) == 0
    diff = jnp.where(row0, before, pltpu.roll(zr, 1, axis=0)) - zr
    return zr + diff * mu_ref[:, lo:hi], diff


def _prep_forward_values(z_ref, zlast_ref, mu_ref, w0_ref, a0_ref, kk_ref, ka_ref, w2_ref, a2_ref, g2_ref, segs, first_tile):
    z = {n: _prep_shifted(z_ref, zlast_ref, mu_ref, segs[n], first_tile) for n in segs}
    r, k, v, wd, ad, gd = (z[n][0] for n in ("r", "k", "v", "wd", "ad", "gd"))
    twd = jnp.tanh(wd)
    pw = _mm(twd, w2_ref[...]) + w0_ref[...]
    lw = -jnp.exp(-(jnp.maximum(-pw, 0.0) + jnp.log(1.0 + jnp.exp(-jnp.abs(pw)))) - 0.5)
    a_sig = 1.0 / (1.0 + jnp.exp(-(_mm(ad, a2_ref[...]) + a0_ref[...])))
    sg = 1.0 / (1.0 + jnp.exp(-gd))
    kx = k * kk_ref[...]
    nrm = jnp.sqrt(_head_sums(kx * kx))
    inv = 1.0 / jnp.maximum(nrm, L2_FLOOR)
    return dict(z=z, r=r, k=k, v=v, twd=twd, pw=pw, lw=lw, a_sig=a_sig, sg=sg, ad=ad, kk=kx * inv, inv=inv, live=nrm > L2_FLOOR)


def _prep_specs(tokens, rpad, rw, w2, a2, g2):
    tr = PREP_ROWS
    tile = lambda w: pl.BlockSpec((tr, w), lambda i: (i, 0))
    before = pl.BlockSpec((8, rpad), lambda i: (jnp.maximum(i * (tr // 8) - 1, 0), 0))
    whole = lambda a: pl.BlockSpec(a.shape, lambda i: (0, 0))
    par = pl.BlockSpec((1, rw), lambda i: (0, 0))
    return tile, before, whole, par, pl.BlockSpec((1, rpad), lambda i: (0, 0))


def _prep_fwd_call(zr, mu, w0, a0, k_k, k_a, w2, a2, g2):
    tokens, rpad = zr.shape
    rw = w0.shape[1]
    segs, _ = _prep_segments(rw, w2.shape[0], a2.shape[0], g2.shape[0])
    tile, before, whole, par, mu_spec = _prep_specs(tokens, rpad, rw, w2, a2, g2)

    def body(z_ref, zlast_ref, mu_ref, w0_ref, a0_ref, kk_ref, ka_ref, w2_ref, a2_ref, g2_ref,
             r_ref, lw_ref, kf_ref, v_ref, na_ref, b_ref, g_ref):
        f = _prep_forward_values(z_ref, zlast_ref, mu_ref, w0_ref, a0_ref, kk_ref, ka_ref, w2_ref, a2_ref, g2_ref,
                                 segs, pl.program_id(0) == 0)
        r_ref[...] = f["r"]
        v_ref[...] = f["v"]
        lw_ref[...] = f["lw"]
        kf_ref[...] = f["k"] * (1.0 + (f["a_sig"] - 1.0) * ka_ref[...])
        na_ref[...] = -f["kk"]
        b_ref[...] = f["kk"] * f["a_sig"]
        g_ref[...] = _mm(f["sg"], g2_ref[...])

    shape = jax.ShapeDtypeStruct((tokens, rw), F32)
    return pl.pallas_call(
        body, name="rwkv_prep_fwd", grid=(tokens // PREP_ROWS,),
        in_specs=[tile(rpad), before, mu_spec, par, par, par, par, whole(w2), whole(a2), whole(g2)],
        out_specs=[tile(rw)] * 7, out_shape=[shape] * 7,
        compiler_params=pltpu.CompilerParams(dimension_semantics=("parallel",), vmem_limit_bytes=VMEM_LIMIT_CAP),
    )(zr, zr, mu, w0, a0, k_k, k_a, w2, a2, g2)


def _prep_bwd_call(zr, mu, w0, a0, k_k, k_a, w2, a2, g2, cts):
    tokens, rpad = zr.shape
    rw = w0.shape[1]
    segs, _ = _prep_segments(rw, w2.shape[0], a2.shape[0], g2.shape[0])
    tile, before, whole, par, mu_spec = _prep_specs(tokens, rpad, rw, w2, a2, g2)
    nt = tokens // PREP_ROWS
    rev = lambda spec: pl.BlockSpec(spec.block_shape, lambda i, f=spec.index_map: f(nt - 1 - i))

    def body(z_ref, zlast_ref, mu_ref, w0_ref, a0_ref, kk_ref, ka_ref, w2_ref, a2_ref, g2_ref,
             dr_ref, dlw_ref, dkf_ref, dv_ref, dna_ref, db_ref, dg_ref,
             dz_ref, dmu_ref, dw0_ref, da0_ref, dkk_ref, dka_ref, dw2_ref, da2_ref, dg2_ref, carry):
        step = pl.program_id(0)

        @pl.when(step == 0)
        def _():
            for ref in (dmu_ref, dw0_ref, da0_ref, dkk_ref, dka_ref, dw2_ref, da2_ref, dg2_ref, carry):
                ref[...] = jnp.zeros_like(ref)

        f = _prep_forward_values(z_ref, zlast_ref, mu_ref, w0_ref, a0_ref, kk_ref, ka_ref, w2_ref, a2_ref, g2_ref,
                                 segs, step == nt - 1)
        k, kk, a_sig, sg, twd = f["k"], f["kk"], f["a_sig"], f["sg"], f["twd"]
        colsum = lambda t: jnp.sum(t, axis=0, keepdims=True)
        dkf, db, dg = dkf_ref[...], db_ref[...], dg_ref[...]
        ka = ka_ref[...]
        dgd = _mm(dg, g2_ref[...], tb=True) * sg * (1.0 - sg)
        dg2_ref[...] += _mm(sg, dg, ta=True)
        dkk = db * a_sig - dna_ref[...]
        da_sig = db * kk + dkf * k * ka
        dk = dkf * (1.0 + (a_sig - 1.0) * ka)
        dka_ref[...] += colsum(dkf * k * (a_sig - 1.0))
        along = jnp.where(f["live"], _head_sums(dkk * kk), 0.0)
        dkx = (dkk - kk * along) * f["inv"]
        dk = dk + dkx * kk_ref[...]
        dkk_ref[...] += colsum(dkx * k)
        dpa = da_sig * a_sig * (1.0 - a_sig)
        da0_ref[...] += colsum(dpa)
        dad = _mm(dpa, a2_ref[...], tb=True)
        da2_ref[...] += _mm(f["ad"], dpa, ta=True)
        dpw = dlw_ref[...] * f["lw"] / (1.0 + jnp.exp(f["pw"]))
        dw0_ref[...] += colsum(dpw)
        dwd = _mm(dpw, w2_ref[...], tb=True) * (1.0 - twd * twd)
        dw2_ref[...] += _mm(twd, dpw, ta=True)
        rows = PREP_ROWS
        last = lax.broadcasted_iota(jnp.int32, (rows, 1), 0) == rows - 1
        for name, dz in (("r", dr_ref[...]), ("k", dk), ("v", dv_ref[...]), ("wd", dwd), ("ad", dad), ("gd", dgd)):
            lo, hi = segs[name]
            mu_s = mu_ref[:, lo:hi]
            dmu_ref[:, lo:hi] += colsum(dz * f["z"][name][1])
            later = dz * mu_s
            dz_ref[:, lo:hi] = dz * (1.0 - mu_s) + jnp.where(last, carry[:, lo:hi], pltpu.roll(later, rows - 1, axis=0))
            carry[:, lo:hi] = later[0:1, :]

    tok = jax.ShapeDtypeStruct((tokens, rw), F32)
    acc = lambda a: jax.ShapeDtypeStruct(a.shape, F32)
    return pl.pallas_call(
        body, name="rwkv_prep_bwd", grid=(nt,),
        in_specs=[rev(tile(rpad)), rev(before), mu_spec, par, par, par, par, whole(w2), whole(a2), whole(g2)]
                 + [rev(tile(rw))] * 7,
        out_specs=[rev(tile(rpad)), mu_spec, par, par, par, par, whole(w2), whole(a2), whole(g2)],
        out_shape=[jax.ShapeDtypeStruct((tokens, rpad), F32), acc(mu), acc(w0), acc(a0), acc(k_k), acc(k_a), acc(w2), acc(a2), acc(g2)],
        scratch_shapes=[pltpu.VMEM((1, rpad), F32)],
        compiler_params=pltpu.CompilerParams(dimension_semantics=("arbitrary",), vmem_limit_bytes=VMEM_LIMIT_CAP),
    )(zr, zr, mu, w0, a0, k_k, k_a, w2, a2, g2, *cts)


@jax.custom_vjp
def rwkv_prep(zr, mu, w0, a0, k_k, k_a, w2, a2, g2):
    return tuple(_prep_fwd_call(zr, mu, w0, a0, k_k, k_a, w2, a2, g2))


def _rwkv_prep_bwd(res, cts):
    zr, mu, w0, a0, k_k, k_a, w2, a2, g2 = res
    dz, dmu, dw0, da0, dkk, dka, dw2, da2, dg2 = _prep_bwd_call(*res, cts)
    return dz, dmu, dw0, da0, dkk, dka, dw2.astype(w2.dtype), da2.astype(a2.dtype), dg2.astype(g2.dtype)


rwkv_prep.defvjp(lambda *a: (tuple(_prep_fwd_call(*a)), a), _rwkv_prep_bwd)


def _pair_masks(rows):
    lane = lax.broadcasted_iota(jnp.int32, (rows, PAIR), 1)
    return lane < HEAD_DIM, lane >= HEAD_DIM


def _bd(x):
    m0, m1 = _pair_masks(x.shape[0])
    return jnp.concatenate([jnp.where(m0, x, 0.0), jnp.where(m1, x, 0.0)], axis=0)


def _unbd(m, c):
    return jnp.where(_pair_masks(c)[0], m[:c], m[c:])


def _pair_a(l2, r2):
    return _mm(l2, _bd(r2), tb=True)


def _pair_mul(p2, x2):
    return _mm(p2, _bd(x2))


def _pair_mul_t(p2, x2):
    return _unbd(_mm(p2, x2, ta=True), p2.shape[0])


def _block_diag_mask():
    row = lax.broadcasted_iota(jnp.int32, (PAIR, PAIR), 0)
    lane = lax.broadcasted_iota(jnp.int32, (PAIR, PAIR), 1)
    return (row < HEAD_DIM) == (lane < HEAD_DIM), row == lane


def _wkv_pair_common(r, lw, k, a, b):
    c = r[0].shape[0]
    pairs = range(len(r))
    i = lax.broadcasted_iota(jnp.int32, (c, PAIR), 0)
    j = lax.broadcasted_iota(jnp.int32, (c, PAIR), 1) % c
    strict, incl = i > j, i >= j
    ti = lax.broadcasted_iota(jnp.int32, (c, c), 0)
    tj = lax.broadcasted_iota(jnp.int32, (c, c), 1)
    tri = jnp.where(ti >= tj, 1.0, 0.0).astype(BF16)
    lc = [sum(_dg(tri, part, False, False) for part in _split(lw[p], 3)) for p in pairs]
    lend = [lc[p][c - 1:c, :] for p in pairs]
    rt = [r[p] * jnp.exp(lc[p]) for p in pairs]
    at = [a[p] * jnp.exp(lc[p] - lw[p]) for p in pairs]
    pinv = [jnp.exp(-lc[p]) for p in pairs]
    kt = [k[p] * pinv[p] for p in pairs]
    bt = [b[p] * pinv[p] for p in pairs]
    e = [jnp.exp(lend[p] - lc[p]) for p in pairs]
    ktp = [k[p] * e[p] for p in pairs]
    btp = [b[p] * e[p] for p in pairs]
    a_ab = [jnp.where(strict, _pair_a(at[p], bt[p]), 0.0) for p in pairs]
    a_ak = [jnp.where(strict, _pair_a(at[p], kt[p]), 0.0) for p in pairs]
    a_rb = [jnp.where(incl, _pair_a(rt[p], bt[p]), 0.0) for p in pairs]
    a_rk = [jnp.where(incl, _pair_a(rt[p], kt[p]), 0.0) for p in pairs]
    t = [jnp.where(i == j, 1.0, 0.0) + a_ab[p] for p in pairs]
    xp = a_ab
    n = 2
    while n < c:
        xp = [_pair_mul(xp[p], xp[p]) for p in pairs]
        t = [t[p] + _pair_mul(t[p], xp[p]) for p in pairs]
        n *= 2
    bdm, eye = _block_diag_mask()
    pend_col = [jnp.sum(jnp.where(eye, jnp.exp(lend[p]), 0.0), axis=1, keepdims=True) for p in pairs]
    return dict(rt=rt, at=at, kt=kt, bt=bt, ktp=ktp, btp=btp, a_ak=a_ak, a_rb=a_rb, a_rk=a_rk, t=t,
                pend_col=pend_col, lend=lend, lc=lc, strict=strict, incl=incl, tri=tri, bdm=bdm)


def _wkv_group(width):
    npair = width // PAIR
    g = min(WKV_PAIRS_PER_STEP, npair)
    assert npair % g == 0
    return npair, g


def _wkv_fwd_call(r, lw, k, v, a, b):
    tokens, width = r.shape
    c = WKV_CHUNK
    nc = tokens // c
    npair, g = _wkv_group(width)

    def body(r_ref, lw_ref, k_ref, v_ref, a_ref, b_ref, y_ref, s_ref, st):
        @pl.when(pl.program_id(1) == 0)
        def _():
            st[...] = jnp.zeros_like(st)

        pairs = range(g)
        rv, lwv, kv, vv, av, bv = ([ref[:, p * PAIR:(p + 1) * PAIR] for p in pairs]
                                   for ref in (r_ref, lw_ref, k_ref, v_ref, a_ref, b_ref))
        s0 = [st[p] for p in pairs]
        q = _wkv_pair_common(rv, lwv, kv, av, bv)
        w1 = [_mm(q["at"][p], s0[p]) + _pair_mul(q["a_ak"][p], vv[p]) for p in pairs]
        u = [_pair_mul(q["t"][p], w1[p]) for p in pairs]
        y = [_mm(q["rt"][p], s0[p]) + _pair_mul(q["a_rb"][p], u[p]) + _pair_mul(q["a_rk"][p], vv[p]) for p in pairs]
        grow = [_mm(jnp.concatenate([q["btp"][p], q["ktp"][p]], axis=0), jnp.concatenate([u[p], vv[p]], axis=0), ta=True)
                for p in pairs]
        for p in pairs:
            y_ref[:, p * PAIR:(p + 1) * PAIR] = y[p]
            s_ref[0, p] = s0[p]
            st[p] = q["pend_col"][p] * s0[p] + jnp.where(q["bdm"], grow[p], 0.0)

    tok = pl.BlockSpec((c, g * PAIR), lambda gi, ci: (ci, gi))
    return pl.pallas_call(
        body, name="wkv_fwd", grid=(npair // g, nc),
        in_specs=[tok] * 6,
        out_specs=[tok, pl.BlockSpec((1, g, PAIR, PAIR), lambda gi, ci: (ci, gi, 0, 0))],
        out_shape=[jax.ShapeDtypeStruct((tokens, width), F32), jax.ShapeDtypeStruct((nc, npair, PAIR, PAIR), F32)],
        scratch_shapes=[pltpu.VMEM((g, PAIR, PAIR), F32)],
        compiler_params=pltpu.CompilerParams(dimension_semantics=("parallel", "arbitrary")),
    )(r, lw, k, v, a, b)


def _wkv_bwd_call(r, lw, k, v, a, b, s, dy):
    tokens, width = r.shape
    c = WKV_CHUNK
    nc = tokens // c
    npair, g = _wkv_group(width)

    def body(r_ref, lw_ref, k_ref, v_ref, a_ref, b_ref, s_ref, dy_ref,
             dr_ref, dlw_ref, dk_ref, dv_ref, da_ref, db_ref, dst):
        @pl.when(pl.program_id(1) == 0)
        def _():
            dst[...] = jnp.zeros_like(dst)

        pairs = range(g)
        rv, lwv, kv, vv, av, bv, dyv = ([ref[:, p * PAIR:(p + 1) * PAIR] for p in pairs]
                                        for ref in (r_ref, lw_ref, k_ref, v_ref, a_ref, b_ref, dy_ref))
        s0 = [s_ref[0, p] for p in pairs]
        dsc = [dst[p] for p in pairs]
        q = _wkv_pair_common(rv, lwv, kv, av, bv)
        rt, at, kt, bt, ktp, btp, t = (q[n] for n in ("rt", "at", "kt", "bt", "ktp", "btp", "t"))
        a_ak, a_rb, a_rk, strict, incl = (q[n] for n in ("a_ak", "a_rb", "a_rk", "strict", "incl"))
        w1 = [_mm(at[p], s0[p]) + _pair_mul(a_ak[p], vv[p]) for p in pairs]
        u = [_pair_mul(t[p], w1[p]) for p in pairs]
        du = [_pair_mul_t(a_rb[p], dyv[p]) + _mm(btp[p], dsc[p]) for p in pairs]
        dw1 = [_pair_mul_t(t[p], du[p]) for p in pairs]
        dv = [_pair_mul_t(a_rk[p], dyv[p]) + _mm(ktp[p], dsc[p]) + _pair_mul_t(a_ak[p], dw1[p]) for p in pairs]
        da_ab = [jnp.where(strict, _pair_a(dw1[p], u[p]), 0.0) for p in pairs]
        da_ak = [jnp.where(strict, _pair_a(dw1[p], vv[p]), 0.0) for p in pairs]
        da_rb = [jnp.where(incl, _pair_a(dyv[p], u[p]), 0.0) for p in pairs]
        da_rk = [jnp.where(incl, _pair_a(dyv[p], vv[p]), 0.0) for p in pairs]
        d_rt = [_mm(dyv[p], s0[p], tb=True) + _pair_mul(da_rb[p], bt[p]) + _pair_mul(da_rk[p], kt[p]) for p in pairs]
        d_at = [_mm(dw1[p], s0[p], tb=True) + _pair_mul(da_ab[p], bt[p]) + _pair_mul(da_ak[p], kt[p]) for p in pairs]
        d_bt = [_pair_mul_t(da_ab[p], at[p]) + _pair_mul_t(da_rb[p], rt[p]) for p in pairs]
        d_kt = [_pair_mul_t(da_ak[p], at[p]) + _pair_mul_t(da_rk[p], rt[p]) for p in pairs]
        d_btp = [_mm(u[p], dsc[p], tb=True) for p in pairs]
        d_ktp = [_mm(vv[p], dsc[p], tb=True) for p in pairs]
        ones = jnp.ones((8, PAIR), BF16)
        dpend = [sum(_dg(ones, part, False, True) for part in _split(dsc[p] * s0[p], 3))[0:1, :] * jnp.exp(q["lend"][p])
                 for p in pairs]
        grow = [_mm(jnp.concatenate([rt[p], at[p]], axis=0), jnp.concatenate([dyv[p], dw1[p]], axis=0), ta=True)
                for p in pairs]
        last = lax.broadcasted_iota(jnp.int32, (c, PAIR), 0) == c - 1
        for p in pairs:
            sl = slice(p * PAIR, (p + 1) * PAIR)
            dst[p] = q["pend_col"][p] * dsc[p] + jnp.where(q["bdm"], grow[p], 0.0)
            lc_e = d_ktp[p] * ktp[p] + d_btp[p] * btp[p]
            dlend = jnp.sum(lc_e, axis=0, keepdims=True) + dpend[p]
            dlc = d_rt[p] * rt[p] - d_kt[p] * kt[p] - d_bt[p] * bt[p] - lc_e + jnp.where(last, dlend, 0.0)
            dlp = d_at[p] * at[p]
            dlw_ref[:, sl] = sum(_dg(q["tri"], part, True, False) for part in _split(dlc + dlp, 3)) - dlp
            lc = q["lc"][p]
            pinv = jnp.exp(-lc)
            e = jnp.exp(q["lend"][p] - lc)
            dr_ref[:, sl] = d_rt[p] * jnp.exp(lc)
            da_ref[:, sl] = d_at[p] * jnp.exp(lc - lwv[p])
            dk_ref[:, sl] = d_kt[p] * pinv + d_ktp[p] * e
            db_ref[:, sl] = d_bt[p] * pinv + d_btp[p] * e
            dv_ref[:, sl] = dv[p]

    tok = pl.BlockSpec((c, g * PAIR), lambda gi, ci: (nc - 1 - ci, gi))
    tshape = jax.ShapeDtypeStruct((tokens, width), F32)
    return pl.pallas_call(
        body, name="wkv_bwd", grid=(npair // g, nc),
        in_specs=[tok] * 6 + [pl.BlockSpec((1, g, PAIR, PAIR), lambda gi, ci: (nc - 1 - ci, gi, 0, 0)), tok],
        out_specs=[tok] * 6, out_shape=[tshape] * 6,
        scratch_shapes=[pltpu.VMEM((g, PAIR, PAIR), F32)],
        compiler_params=pltpu.CompilerParams(dimension_semantics=("parallel", "arbitrary")),
    )(r, lw, k, v, a, b, s, dy)


@jax.custom_vjp
def wkv7(r, lw, k, v, a, b):
    return _wkv_fwd_call(r, lw, k, v, a, b)[0]


def _wkv7_fwd(r, lw, k, v, a, b):
    y, s = _wkv_fwd_call(r, lw, k, v, a, b)
    return y, (r, lw, k, v, a, b, s)


wkv7.defvjp(_wkv7_fwd, lambda res, dy: tuple(_wkv_bwd_call(*res, dy)))


def _attn_block(tokens):
    return ATTN_BLOCK_BIG if tokens % ATTN_BLOCK_BIG == 0 else ATTN_BLOCK


def _fox_layouts(cum):
    tokens, heads = cum.shape
    t = _attn_block(tokens)
    cq = cum.reshape(tokens, heads // 2, 2).transpose(1, 0, 2)
    ck = cum.T.reshape(heads // 2, 2, tokens // t, t).transpose(0, 2, 1, 3)
    return cq, ck


def _head_lane_masks(rows):
    lane = lax.broadcasted_iota(jnp.int32, (rows, 2 * HEAD_DIM), 1)
    return [lane < HEAD_DIM, lane >= HEAD_DIM]


def _fox_fwd_call(q, k, v, cq, ck):
    tokens, width = q.shape
    t = _attn_block(tokens)
    nb = tokens // t
    hd = HEAD_DIM
    npair = width // (2 * hd)

    g = ATTN_PAIRS_PER_STEP if npair % ATTN_PAIRS_PER_STEP == 0 else 1
    heads = [(pp, hh) for pp in range(g) for hh in range(2)]

    def body(q_ref, k_ref, v_ref, cq_ref, ck_ref, o_ref, lse_ref):
        i = pl.program_id(1)
        masks = _head_lane_masks(t)
        lanes = [slice(pp * PAIR, (pp + 1) * PAIR) for pp in range(g)]
        qs = [jnp.where(masks[hh], q_ref[:, lanes[pp]], 0.0).astype(BF16) for pp, hh in heads]
        cqs = [cq_ref[pp, :, hh:hh + 1] for pp, hh in heads]

        def block(j, carry, diagonal):
            off = pl.multiple_of(j * t, t)
            k2 = [k_ref[pl.ds(off, t), lanes[pp]].astype(BF16) for pp in range(g)]
            v2 = [v_ref[pl.ds(off, t), lanes[pp]].astype(BF16) for pp in range(g)]
            s = [_dg(qs[n], k2[pp], False, True) + (cqs[n] - ck_ref[pp, j][hh:hh + 1, :]) for n, (pp, hh) in enumerate(heads)]
            if diagonal:
                keep = lax.broadcasted_iota(jnp.int32, (t, t), 0) >= lax.broadcasted_iota(jnp.int32, (t, t), 1)
                s = [jnp.where(keep, x, NEG_BIG) for x in s]
            m_new = [jnp.maximum(carry[n][0], jnp.max(s[n], axis=1, keepdims=True)) for n in range(len(heads))]
            alpha = [jnp.exp(carry[n][0] - m_new[n]) for n in range(len(heads))]
            p = [jnp.exp(s[n] - m_new[n]) for n in range(len(heads))]
            l = [alpha[n] * carry[n][1] + jnp.sum(p[n], axis=1, keepdims=True) for n in range(len(heads))]
            acc = [alpha[n] * carry[n][2] + _dg(p[n].astype(BF16), v2[pp], False, False) for n, (pp, hh) in enumerate(heads)]
            return tuple(zip(m_new, l, acc))

        init = tuple((jnp.full((t, 1), NEG_BIG, F32), jnp.zeros((t, 1), F32), jnp.zeros((t, 2 * hd), F32)) for _ in heads)
        res = lax.fori_loop(0, i, lambda j, c: block(j, c, False), init)
        res = block(i, res, True)
        for pp in range(g):
            a, b = res[2 * pp], res[2 * pp + 1]
            o_ref[:, lanes[pp]] = jnp.where(masks[0], a[2] / a[1], b[2] / b[1])
        for n, (pp, hh) in enumerate(heads):
            lse_ref[pp, :, hh:hh + 1] = res[n][0] + jnp.log(res[n][1])

    blk = pl.BlockSpec((t, g * PAIR), lambda hp, i: (i, hp))
    full = pl.BlockSpec((tokens, g * PAIR), lambda hp, i: (0, hp))
    cq_spec = pl.BlockSpec((g, t, 2), lambda hp, i: (hp, i, 0))
    ck_spec = pl.BlockSpec((g, nb, 2, t), lambda hp, i: (hp, 0, 0, 0))
    return pl.pallas_call(
        body, name="fox_fwd", grid=(npair // g, nb),
        in_specs=[blk, full, full, cq_spec, ck_spec],
        out_specs=[blk, cq_spec],
        out_shape=[jax.ShapeDtypeStruct((tokens, width), F32), jax.ShapeDtypeStruct((npair, tokens, 2), F32)],
        compiler_params=pltpu.CompilerParams(dimension_semantics=("parallel", "arbitrary")),
    )(q, k, v, cq, ck)


def _fox_bwd_call(q, k, v, cq, ck, o, lse, do):
    tokens, width = q.shape
    t = _attn_block(tokens)
    nb = tokens // t
    hd = HEAD_DIM
    npair = width // (2 * hd)

    g = ATTN_PAIRS_PER_STEP if npair % ATTN_PAIRS_PER_STEP == 0 else 1
    heads = [(pp, hh) for pp in range(g) for hh in range(2)]
    nh = range(len(heads))

    def body(q_ref, k_ref, v_ref, cq_ref, ck_ref, o_ref, lse_ref, do_ref, dq_ref, dk_ref, dv_ref, dck_ref, dcq_ref):
        i = pl.program_id(1)

        @pl.when(i == 0)
        def _():
            dk_ref[...] = jnp.zeros_like(dk_ref)
            dv_ref[...] = jnp.zeros_like(dv_ref)
            dck_ref[...] = jnp.zeros_like(dck_ref)

        masks = _head_lane_masks(t)
        lanes = [slice(pp * PAIR, (pp + 1) * PAIR) for pp in range(g)]
        qs = [jnp.where(masks[hh], q_ref[:, lanes[pp]], 0.0).astype(BF16) for pp, hh in heads]
        dos = [jnp.where(masks[hh], do_ref[:, lanes[pp]], 0.0).astype(BF16) for pp, hh in heads]
        deltas = [jnp.sum(dos[n].astype(F32) * o_ref[:, lanes[pp]], axis=1, keepdims=True) for n, (pp, hh) in enumerate(heads)]
        bias = [cq_ref[pp, :, hh:hh + 1] - lse_ref[pp, :, hh:hh + 1] for pp, hh in heads]

        def block(j, carry, diagonal):
            off = pl.multiple_of(j * t, t)
            k2 = [k_ref[pl.ds(off, t), lanes[pp]].astype(BF16) for pp in range(g)]
            v2 = [v_ref[pl.ds(off, t), lanes[pp]].astype(BF16) for pp in range(g)]
            s = [_dg(qs[n], k2[pp], False, True) + (bias[n] - ck_ref[pp, j][hh:hh + 1, :]) for n, (pp, hh) in enumerate(heads)]
            if diagonal:
                keep = lax.broadcasted_iota(jnp.int32, (t, t), 0) >= lax.broadcasted_iota(jnp.int32, (t, t), 1)
                s = [jnp.where(keep, x, NEG_BIG) for x in s]
            p = [jnp.exp(x) for x in s]
            dp = [_dg(dos[n], v2[pp], False, True) for n, (pp, hh) in enumerate(heads)]
            ds = [p[n] * (dp[n] - deltas[n]) for n in nh]
            dsb = [x.astype(BF16) for x in ds]
            out = tuple((carry[n][0] + _dg(dsb[n], k2[pp], False, False), carry[n][1] + jnp.sum(ds[n], axis=1, keepdims=True))
                        for n, (pp, hh) in enumerate(heads))
            for pp in range(g):
                a, b = 2 * pp, 2 * pp + 1
                dk_ref[pl.ds(off, t), lanes[pp]] += _dg(dsb[a], qs[a], True, False) + _dg(dsb[b], qs[b], True, False)
                dv_ref[pl.ds(off, t), lanes[pp]] += (_dg(p[a].astype(BF16), dos[a], True, False)
                                                     + _dg(p[b].astype(BF16), dos[b], True, False))
            for n, (pp, hh) in enumerate(heads):
                dck_ref[pp, j, hh:hh + 1, :] -= jnp.sum(ds[n], axis=0, keepdims=True)
            return out

        init = tuple((jnp.zeros((t, 2 * hd), F32), jnp.zeros((t, 1), F32)) for _ in heads)
        res = lax.fori_loop(0, i, lambda j, c: block(j, c, False), init)
        res = block(i, res, True)
        for pp in range(g):
            dq_ref[:, lanes[pp]] = jnp.where(masks[0], res[2 * pp][0], res[2 * pp + 1][0])
        for n, (pp, hh) in enumerate(heads):
            dcq_ref[pp, :, hh:hh + 1] = res[n][1]

    blk = pl.BlockSpec((t, g * PAIR), lambda hp, i: (i, hp))
    full = pl.BlockSpec((tokens, g * PAIR), lambda hp, i: (0, hp))
    cq_spec = pl.BlockSpec((g, t, 2), lambda hp, i: (hp, i, 0))
    ck_spec = pl.BlockSpec((g, nb, 2, t), lambda hp, i: (hp, 0, 0, 0))
    tshape = jax.ShapeDtypeStruct((tokens, width), F32)
    return pl.pallas_call(
        body, name="fox_bwd", grid=(npair // g, nb),
        in_specs=[blk, full, full, cq_spec, ck_spec, blk, cq_spec, blk],
        out_specs=[blk, full, full, ck_spec, cq_spec],
        out_shape=[tshape, tshape, tshape, jax.ShapeDtypeStruct((npair, nb, 2, t), F32),
                   jax.ShapeDtypeStruct((npair, tokens, 2), F32)],
        compiler_params=pltpu.CompilerParams(dimension_semantics=("parallel", "arbitrary")),
    )(q, k, v, cq, ck, o, lse, do)


@jax.custom_vjp
def fox_attention(q, k, v, cum):
    return _fox_fwd(q, k, v, cum)[0]


def _fox_fwd(q, k, v, cum):
    cq, ck = _fox_layouts(cum)
    q, k, v = q.astype(BF16), k.astype(BF16), v.astype(BF16)
    o, lse = _fox_fwd_call(q, k, v, cq, ck)
    return o, (q, k, v, cq, ck, o, lse)


def _fox_bwd(res, do):
    q, k, v, cq, ck, o, lse = res
    dq, dk, dv, dck, dcq = _fox_bwd_call(q, k, v, cq, ck, o, lse, do)
    npair, nb, _, t = dck.shape
    dcum = dck.transpose(0, 2, 1, 3).reshape(2 * npair, nb * t).T + dcq.transpose(1, 0, 2).reshape(nb * t, 2 * npair)
    return dq, dk, dv, dcum


fox_attention.defvjp(_fox_fwd, _fox_bwd)


def _loss_call(y, target):
    rows, d = y.shape
    tr = _row_tile(rows, d)

    def body(y_ref, t_ref, loss_ref, dy_ref):
        @pl.when(pl.program_id(0) == 0)
        def _():
            loss_ref[...] = jnp.zeros_like(loss_ref)

        diff = y_ref[...] - t_ref[...]
        dy_ref[...] = diff * (1.0 / d)
        loss_ref[...] += (0.5 / d) * jnp.sum(jnp.sum(diff * diff, axis=1, keepdims=True), axis=0, keepdims=True)

    return pl.pallas_call(
        body, name="loss", grid=(rows // tr,),
        in_specs=[pl.BlockSpec((tr, d), lambda i: (i, 0))] * 2,
        out_specs=[pl.BlockSpec((1, 1), lambda i: (0, 0)), pl.BlockSpec((tr, d), lambda i: (i, 0))],
        out_shape=[jax.ShapeDtypeStruct((1, 1), F32), jax.ShapeDtypeStruct((rows, d), F32)],
        compiler_params=pltpu.CompilerParams(dimension_semantics=("arbitrary",)),
    )(y, target)


def _adamw_call(w, g, m, v):
    rows, cols = w.shape
    tr = _row_tile_ragged(rows, cols, budget=1024 * 1024)
    c1 = 1.0 / (1.0 - ADAM_B1 ** ADAM_STEP)
    c2 = 1.0 / (1.0 - ADAM_B2 ** ADAM_STEP)

    def body(w_ref, g_ref, m_ref, v_ref, d_ref, nm_ref, nv_ref):
        gv = g_ref[...]
        nm = ADAM_B1 * m_ref[...] + (1.0 - ADAM_B1) * gv
        nv = ADAM_B2 * v_ref[...] + (1.0 - ADAM_B2) * (gv * gv)
        nm_ref[...] = nm
        nv_ref[...] = nv
        d_ref[...] = -ADAM_LR * ((nm * c1) / (jnp.sqrt(nv * c2) + ADAM_EPS) + ADAM_WD * w_ref[...])

    spec = pl.BlockSpec((tr, cols), lambda i: (i, 0))
    shape = jax.ShapeDtypeStruct((rows, cols), F32)
    return pl.pallas_call(
        body, name="adamw", grid=(pl.cdiv(rows, tr),),
        in_specs=[spec] * 4, out_specs=[spec] * 3, out_shape=[shape] * 3,
        compiler_params=pltpu.CompilerParams(dimension_semantics=("parallel",)),
    )(w, g, m, v)


def _my_place():
    return lax.axis_index("x"), lax.axis_index("y"), lax.axis_index("c")


def _place_index(px, py, pc):
    return 4 * px + 2 * py + pc


HBM_SPEC = pl.BlockSpec(memory_space=pltpu.HBM)


def _all_gather_call(block):
    def body(x_ref, out_ref, send_sems, recv_sems, local_sem):
        x, y, c = _my_place()
        me, sibling = (x, y, c), (x, y, 1 - c)
        chips = [(1 - x, y), (x, 1 - y), (1 - x, 1 - y)]

        def slot(px, py, pc):
            return out_ref.at[_place_index(px, py, pc)]

        def copy(k, blk, to, src=None):
            return pltpu.make_async_remote_copy(
                src_ref=slot(*blk) if src is None else src, dst_ref=slot(*blk),
                send_sem=send_sems.at[k], recv_sem=recv_sems.at[k],
                device_id=to, device_id_type=pl.DeviceIdType.MESH)

        mine = pltpu.make_async_copy(x_ref, slot(*me), local_sem)
        mine.start()
        first = [copy(0, me, sibling, src=x_ref)]
        first += [copy(1 + j, me, (*chip, c), src=x_ref) for j, chip in enumerate(chips)]
        for cp in first:
            cp.start()
        passed = [copy(4 + j, (*chip, c), sibling) for j, chip in enumerate(chips)]
        for j, chip in enumerate(chips):
            copy(1 + j, (*chip, c), me).wait_recv()
            passed[j].start()
        copy(0, sibling, me).wait_recv()
        for j, chip in enumerate(chips):
            copy(4 + j, (*chip, 1 - c), me).wait_recv()
        for cp in first + passed:
            cp.wait_send()
        mine.wait()

    return pl.pallas_call(
        body, name="all_gather",
        out_shape=jax.ShapeDtypeStruct((N_DEV,) + block.shape, block.dtype),
        in_specs=[HBM_SPEC], out_specs=HBM_SPEC,
        scratch_shapes=[pltpu.SemaphoreType.DMA((7,)), pltpu.SemaphoreType.DMA((7,)), pltpu.SemaphoreType.DMA],
    )(block)


SEM_SPEC = pl.BlockSpec(memory_space=pltpu.SEMAPHORE)
SIDE_EFFECT = pltpu.SideEffectType.DATAFLOW_SIDE_EFFECTING


def _peers():
    x, y, c = _my_place()
    out = []
    for k in range(1, N_DEV):
        peer = (x ^ (k >> 2), y ^ ((k >> 1) & 1), c ^ (k & 1))
        out.append((k - 1, peer, _place_index(*peer)))
    return _place_index(x, y, c), out


def _spread_start(src, per_peer, name, after=None):
    slot = src.shape[1:] if per_peer else src.shape
    order = () if after is None else (after,)

    def body(src_ref, land_ref, *rest):
        send_sems, recv_sems, src_thru, land_thru, token = rest[len(order):]
        mine, peers = _peers()
        for k, peer, peer_idx in peers:
            pltpu.make_async_remote_copy(
                src_ref=src_ref.at[peer_idx] if per_peer else src_ref, dst_ref=land_ref.at[mine],
                send_sem=send_sems.at[k], recv_sem=recv_sems.at[k],
                device_id=peer, device_id_type=pl.DeviceIdType.MESH).start()
        token[...] = jnp.zeros_like(token)

    return pl.pallas_call(
        body, name=name,
        out_shape=(pltpu.SemaphoreType.DMA((N_DEV - 1,)), pltpu.SemaphoreType.DMA((N_DEV - 1,)),
                   pltpu.HBM(src.shape, src.dtype), pltpu.HBM((N_DEV,) + slot, src.dtype),
                   jax.ShapeDtypeStruct((8, 128), F32)),
        in_specs=(HBM_SPEC, HBM_SPEC) + (pl.BlockSpec(memory_space=pl.ANY),) * len(order),
        out_specs=(SEM_SPEC, SEM_SPEC, HBM_SPEC, HBM_SPEC, pl.BlockSpec(memory_space=pltpu.VMEM)),
        input_output_aliases={0: 2, 1: 3},
        compiler_params=pltpu.CompilerParams(has_side_effects=SIDE_EFFECT),
    )(pltpu.with_memory_space_constraint(src, pltpu.HBM),
      pltpu.with_memory_space_constraint(lax.empty((N_DEV,) + slot, src.dtype), pltpu.HBM), *order)


def _spread_wait(handles, after, per_peer, name):
    send_sems, recv_sems, src_thru, land_thru = handles

    def body(src_ref, land_ref, send_sems, recv_sems, after_ref, src_dead, got_ref):
        _, peers = _peers()
        for k, peer, peer_idx in peers:
            copy = pltpu.make_async_remote_copy(
                src_ref=src_ref.at[peer_idx] if per_peer else src_ref, dst_ref=land_ref.at[peer_idx],
                send_sem=send_sems.at[k], recv_sem=recv_sems.at[k],
                device_id=peer, device_id_type=pl.DeviceIdType.MESH)
            copy.wait_send()
            copy.wait_recv()

    return pl.pallas_call(
        body, name=name,
        out_shape=(pltpu.HBM(src_thru.shape, src_thru.dtype), pltpu.HBM(land_thru.shape, land_thru.dtype)),
        in_specs=(HBM_SPEC, HBM_SPEC, SEM_SPEC, SEM_SPEC, pl.BlockSpec(memory_space=pl.ANY)),
        out_specs=(HBM_SPEC, HBM_SPEC), input_output_aliases={0: 0, 1: 1},
        compiler_params=pltpu.CompilerParams(has_side_effects=SIDE_EFFECT),
    )(src_thru, land_thru, send_sems, recv_sems, after)


def _sum_slots_call(slots):
    _, rows, cols = slots.shape
    tr = _row_tile_ragged(rows, cols, budget=512 * 1024)

    def body(s_ref, o_ref):
        acc = s_ref[0].astype(F32)
        for j in range(1, N_DEV):
            acc = acc + s_ref[j].astype(F32)
        o_ref[...] = acc

    return pl.pallas_call(
        body, name="sum_slots", grid=(pl.cdiv(rows, tr),),
        in_specs=[pl.BlockSpec((N_DEV, tr, cols), lambda i: (0, i, 0))],
        out_specs=pl.BlockSpec((tr, cols), lambda i: (i, 0)),
        out_shape=jax.ShapeDtypeStruct((rows, cols), F32),
        compiler_params=pltpu.CompilerParams(dimension_semantics=("parallel",)),
    )(slots)


def _sum_adamw_call(got, own, w, m, v):
    rows, cols = w.shape
    tr = _row_tile_ragged(rows, cols, budget=512 * 1024)
    c1 = 1.0 / (1.0 - ADAM_B1 ** ADAM_STEP)
    c2 = 1.0 / (1.0 - ADAM_B2 ** ADAM_STEP)

    def body(got_ref, own_ref, w_ref, m_ref, v_ref, g_ref, d_ref, nm_ref, nv_ref):
        mine = _place_index(*_my_place())
        gv = jnp.zeros(w_ref.shape, F32)
        for j in range(N_DEV):
            gv = gv + jnp.where(mine == j, own_ref[...], got_ref[j]).astype(F32)
        nm = ADAM_B1 * m_ref[...] + (1.0 - ADAM_B1) * gv
        nv = ADAM_B2 * v_ref[...] + (1.0 - ADAM_B2) * (gv * gv)
        g_ref[...] = gv
        nm_ref[...] = nm
        nv_ref[...] = nv
        d_ref[...] = -ADAM_LR * ((nm * c1) / (jnp.sqrt(nv * c2) + ADAM_EPS) + ADAM_WD * w_ref[...])

    spec = pl.BlockSpec((tr, cols), lambda i: (i, 0))
    shape = jax.ShapeDtypeStruct((rows, cols), F32)
    return pl.pallas_call(
        body, name="sum_adamw", grid=(pl.cdiv(rows, tr),),
        in_specs=[pl.BlockSpec((N_DEV, tr, cols), lambda i: (0, i, 0))] + [spec] * 4,
        out_specs=[spec] * 4, out_shape=[shape] * 4,
        compiler_params=pltpu.CompilerParams(dimension_semantics=("parallel",)),
    )(got, own, w, m, v)


def _with_own_slot(got, own, mine):
    return lax.dynamic_update_index_in_dim(got, own, mine, 0)


def _pack(vectors, width):
    flat = jnp.concatenate([v.reshape(-1) for v in vectors])
    return jnp.pad(flat, (0, width - flat.shape[0])).reshape(width // 128, 128)


def _unpack(packed, like):
    flat = packed.reshape(-1)
    out, at = [], 0
    for v in like:
        out.append(flat[at:at + v.size].reshape(v.shape))
        at += v.size
    return tuple(out)


def _cols_from_slots(slots):
    n, rows, cols = slots.shape
    return slots.transpose(1, 0, 2).reshape(rows, n * cols)


def _rows_from_slots(slots):
    return slots.reshape(-1, slots.shape[2])


def _pad128(n):
    return -(-n // 128) * 128


def _pad_to_tiles(a, axis):
    n = a.shape[axis]
    pads = [(0, 0)] * a.ndim
    pads[axis] = (0, _pad128(n) - n)
    return jnp.pad(a, pads)


def _rwkv_group(take, zeros, rw, dl, al, gl):
    at = 3 * rw
    parts = take(0, at)
    for n in (dl, al, gl):
        parts += take(at, at + n)
        if _pad128(n) > n:
            parts.append(zeros(_pad128(n) - n))
        at += n
    return parts


def _in_proj_layout(slots, rw, fw, dl, al, gl, whole):
    n_slots, rows, d = slots.shape
    wt = slots.reshape(n_slots * rows, d)
    take = lambda lo, hi: [wt[lo:hi]]
    zeros = lambda n: jnp.zeros((n, d), wt.dtype)
    rcols = 3 * rw + dl + al + gl
    fcols = 3 * fw + fw // HEAD_DIM
    group_r = _rwkv_group(take, zeros, rw, dl, al, gl)
    group_f = take(rcols, rcols + fcols) + ([zeros(_pad128(fcols) - fcols)] if _pad128(fcols) > fcols else [])
    group_g = take(rcols + fcols, n_slots * rows)
    if whole:
        return jnp.concatenate(group_r + group_f + group_g, axis=0)
    return tuple(jnp.concatenate(g, axis=0) for g in (group_r, group_f, group_g))


def _low_rank_layout(slots):
    return _pad_to_tiles(_cols_from_slots(slots), 0)


def _stage_embed(meta, x, n1, lp):
    h0 = jnp.concatenate([meta, x, jnp.zeros((lp - meta.shape[0] - x.shape[0], x.shape[1]), F32)], axis=0)
    return h0, rmsnorm(h0, n1)


def _stage_mix(z_r, z_f, small, w2, a2, g2, dims):
    (mu, w0, a0, k_k, k_a, r_k, gn_w, gn_b, q_g, k_g, f_bias) = small
    rw, fw, dl, al, gl = dims
    fcols = 3 * fw + fw // HEAD_DIM

    mu_group = jnp.concatenate(_rwkv_group(lambda lo, hi: [mu[:, lo:hi]], lambda n: jnp.zeros((1, n), F32), rw, dl, al, gl), axis=1)
    r, lw, kf, v, na, b, g = rwkv_prep(z_r, mu_group, w0, a0, k_k, k_a, w2, a2, g2)
    y = wkv7(r, lw, kf, v, na, b)
    y_a = gn_bonus(y, r, kf, v, g, gn_w, gn_b, r_k.reshape(1, rw))

    fq, fk, fv, fl = z_f[:, :fw], z_f[:, fw:2 * fw], z_f[:, 2 * fw:3 * fw], z_f[:, 3 * fw:fcols]
    fq = head_rms(fq, jnp.tile(q_g, (1, fw // HEAD_DIM)) * (HEAD_DIM ** -0.5))
    fk = head_rms(fk, jnp.tile(k_g, (1, fw // HEAD_DIM)))
    cum = jnp.cumsum(jax.nn.log_sigmoid(badd(fl, f_bias)), axis=0)
    y_b = fox_attention(fq, fk, fv, cum)
    return y_a, y_b


def _stage_merge(h0, y_a, y_b, z_g, w_a, w_b, w_o):
    merged = gated_merge(z_g, dense_cols_bf16(y_a, w_a), dense_cols_bf16(y_b, w_b))
    return dense_add(merged, w_o, h0)


def _stage_ffn(h1, n2, w_gu, w_dn):
    return dense_add(swiglu(dense_cols_bf16(rmsnorm(h1, n2), w_gu)), w_dn, h1)


SHARDED = ("meta_tokens", "w_in", "rwkv_w2", "rwkv_a2", "rwkv_g2", "w_branch_a", "w_branch_b", "w_o", "w_gate_up", "w_down")
LOW_RANK = ("rwkv_w2", "rwkv_a2", "rwkv_g2")
SMALL = ("norm1_g", "rwkv_mu", "rwkv_w0", "rwkv_a0", "rwkv_k_k", "rwkv_k_a", "rwkv_r_k", "rwkv_gn_w", "rwkv_gn_b",
         "fox_q_norm_g", "fox_k_norm_g", "fox_f_bias", "norm2_g")
WEIGHTS = ("meta_tokens", "norm1_g", "w_in", "rwkv_mu", "rwkv_w0", "rwkv_w2", "rwkv_a0", "rwkv_a2", "rwkv_g2", "rwkv_k_k",
           "rwkv_k_a", "rwkv_r_k", "rwkv_gn_w", "rwkv_gn_b", "fox_q_norm_g", "fox_k_norm_g", "fox_f_bias", "w_branch_a",
           "w_branch_b", "w_o", "norm2_g", "w_gate_up", "w_down")


def _as2d(a):
    return a.reshape(-1, a.shape[-1])


def kernel(x, meta_tokens, norm1_g, w_in, rwkv_mu, rwkv_w0, rwkv_w2, rwkv_a0, rwkv_a2, rwkv_g2, rwkv_k_k, rwkv_k_a, rwkv_r_k, rwkv_gn_w, rwkv_gn_b, fox_q_norm_g, fox_k_norm_g, fox_f_bias, w_branch_a, w_branch_b, w_o, norm2_g, w_gate_up, w_down, loss_target, m_meta_tokens, m_norm1_g, m_w_in, m_rwkv_mu, m_rwkv_w0, m_rwkv_w2, m_rwkv_a0, m_rwkv_a2, m_rwkv_g2, m_rwkv_k_k, m_rwkv_k_a, m_rwkv_r_k, m_rwkv_gn_w, m_rwkv_gn_b, m_fox_q_norm_g, m_fox_k_norm_g, m_fox_f_bias, m_w_branch_a, m_w_branch_b, m_w_o, m_norm2_g, m_w_gate_up, m_w_down, v_meta_tokens, v_norm1_g, v_w_in, v_rwkv_mu, v_rwkv_w0, v_rwkv_w2, v_rwkv_a0, v_rwkv_a2, v_rwkv_g2, v_rwkv_k_k, v_rwkv_k_a, v_rwkv_r_k, v_rwkv_gn_w, v_rwkv_gn_b, v_fox_q_norm_g, v_fox_k_norm_g, v_fox_f_bias, v_w_branch_a, v_w_branch_b, v_w_o, v_norm2_g, v_w_gate_up, v_w_down):
    given = dict(locals())
    w = {n: given[n] for n in WEIGHTS}
    assert rwkv_r_k.shape[-1] == HEAD_DIM
    n_meta, seq = meta_tokens.shape[0], x.shape[1]
    tokens = n_meta + seq
    lp = -(-tokens // TOKEN_TILE) * TOKEN_TILE
    mine = _place_index(*(lax.axis_index(a) for a in MESH_AXES))
    x2 = x[0]

    local = {n: _as2d(given[n]) for n in given if n != "x" and n != "loss_target"}
    for n in ("w_in", "m_w_in", "v_w_in"):
        local[n] = jnp.transpose(given[n][0])
    blocks = {n: local[n].astype(F32 if n == "meta_tokens" else BF16) for n in SHARDED}
    for prefix in ("", "m_", "v_"):
        local[prefix + "low_rank"] = jnp.concatenate([local[prefix + n] for n in LOW_RANK], axis=0)
    blocks["low_rank"] = jnp.concatenate([blocks[n] for n in LOW_RANK], axis=0)
    low_rank_ends = [sum(local[n].shape[0] for n in LOW_RANK[:i + 1]) for i in range(len(LOW_RANK))]
    low_rank_rows = lambda a, axis: [lax.slice_in_dim(a, lo, hi, axis=axis) for lo, hi in zip([0] + low_rank_ends, low_rank_ends)]
    first = ("meta_tokens", "low_rank")
    started = {n: _spread_start(blocks[n], False, "gather_start_" + n) for n in first}
    zero = sum(started[n][4][0, 0] for n in first)

    def gathered(n, after):
        own, got = _spread_wait(started[n][:4], after, False, "gather_wait_" + n)
        return _with_own_slot(got, own, mine)

    sm = {n: _as2d(w[n]) for n in SMALL}
    small_mix = tuple(sm[n] for n in SMALL[1:-1])
    n1 = sm["norm1_g"] + zero
    rw, fw = w_branch_a.shape[-2], w_branch_b.shape[-2]
    dims = (rw, fw, rwkv_w2.shape[-2], rwkv_a2.shape[-2], rwkv_g2.shape[-2])
    same = lambda s: (s,)

    meta, un_meta = jax.vjp(_cols_from_slots, gathered("meta_tokens", x2))
    (h0, xn), vjp_embed = jax.vjp(lambda m, xs, g: _stage_embed(m, xs, g, lp), meta, x2, n1)
    in_slots = _all_gather_call(blocks["w_in"])
    later = [n for n in SHARDED if n not in first and n not in LOW_RANK and n != "w_in"]
    started.update({n: _spread_start(blocks[n], False, "gather_start_" + n, after=in_slots) for n in later})
    w_groups = _in_proj_layout(in_slots, *dims, whole=False)
    w_cat, un_in = jax.vjp(lambda s: _in_proj_layout(s, *dims, whole=True), in_slots)
    xn_b = xn.astype(BF16)
    behind = sum(started[n][4] for n in later)
    z_r, z_f, z_g = (_matmul(xn_b, wg, tb=True, name="in_proj_" + tag, after=behind, out_dtype=BF16 if tag == "g" else F32)
                     for wg, tag in zip(w_groups, "rfg"))
    (w2, un_w2), (a2, un_a2), (g2, un_g2) = (jax.vjp(_low_rank_layout, s) for s in low_rank_rows(gathered("low_rank", xn), 1))
    (y_a, y_b), vjp_mix = jax.vjp(lambda zr, zf, s, a, b, c: _stage_mix(zr, zf, s, a, b, c, dims),
                                  z_r, z_f, small_mix, w2, a2, g2)
    w_a, w_b = gathered("w_branch_a", y_a), gathered("w_branch_b", y_a)
    w_o_full, un_wo = jax.vjp(_rows_from_slots, gathered("w_o", y_a))
    h1, vjp_merge = jax.vjp(_stage_merge, h0, y_a, y_b, z_g, w_a, w_b, w_o_full)
    w_gu = gathered("w_gate_up", h1)
    w_dn, un_dn = jax.vjp(_rows_from_slots, gathered("w_down", h1))
    y, vjp_ffn = jax.vjp(_stage_ffn, h1, sm["norm2_g"], w_gu, w_dn)

    loss_part, dy_real = _loss_call(y[n_meta:tokens], loss_target[0])
    dy = jnp.pad(dy_real, ((n_meta, lp - tokens), (0, 0)))
    loss = lax.psum(loss_part[0, 0], MESH_AXES)

    sent = {}

    def send_grad(n, dmat, unlayout):
        sent[n] = _spread_start(unlayout(dmat)[0], True, "grad_start_" + n)
        return sent[n][4][0, 0]

    d_h1, d_n2, d_wgu, d_wdn = vjp_ffn(dy)
    behind = send_grad("w_gate_up", d_wgu, same) + send_grad("w_down", d_wdn, un_dn)
    d_h0, d_ya, d_yb, d_zg, d_wa, d_wb, d_wo = vjp_merge(d_h1 + behind)
    behind = send_grad("w_o", d_wo, un_wo) + send_grad("w_branch_a", d_wa, same) + send_grad("w_branch_b", d_wb, same)
    d_zr, d_zf, d_small_mix, d_w2, d_a2, d_g2 = vjp_mix((d_ya + behind.astype(d_ya.dtype), d_yb))
    dproj_b = jnp.concatenate([d_zr.astype(BF16), d_zf.astype(BF16), d_zg.astype(BF16)], axis=1)
    d_wcat = _matmul(dproj_b, xn_b, ta=True, out_dtype=BF16, name="in_proj_dw")
    send_grad("w_in", d_wcat, un_in)
    d_xn = _matmul(dproj_b, w_cat, out_dtype=BF16, name="in_proj_dx", after=sent["w_in"][4])
    send_grad("low_rank", jnp.concatenate([un_w2(d_w2)[0], un_a2(d_a2)[0], un_g2(d_g2)[0]], axis=1), same)
    d_meta, g_x, d_n1 = vjp_embed((d_h0, d_xn))
    send_grad("meta_tokens", d_meta, un_meta)

    small_grads = (d_n1, *d_small_mix, d_n2)
    n_small = sum(g.size for g in small_grads)
    width = -(-n_small // 1024) * 1024
    small_sent = _spread_start(_pack(small_grads, width), False, "small_grad_start")

    grads, delta, new_m, new_v = {}, {}, {}, {}
    after = g_x
    for n in ("w_gate_up", "w_down", "w_o", "w_branch_a", "w_branch_b", "low_rank", "meta_tokens", "w_in"):
        src, got = _spread_wait(sent[n][:4], after, True, "grad_wait_" + n)
        own = lax.dynamic_index_in_dim(src, mine, 0, keepdims=False)
        stepped = _sum_adamw_call(got, own, local[n], local["m_" + n], local["v_" + n])
        after = stepped[2]
        if n == "low_rank":
            for out, t in zip((grads, delta, new_m, new_v), stepped):
                out.update({name: part.reshape(w[name].shape) for name, part in zip(LOW_RANK, low_rank_rows(t, 0))})
            continue
        back = (lambda t: jnp.transpose(t)[None]) if n == "w_in" else (lambda t: t.reshape(w[n].shape))
        grads[n], delta[n], new_m[n], new_v[n] = (back(t) for t in stepped)
    own_small, got_small = _spread_wait(small_sent[:4], after, False, "small_grad_wait")
    small_total = _unpack(_sum_slots_call(_with_own_slot(got_small, own_small, mine)), small_grads)
    grads.update({n: g.reshape(w[n].shape) for n, g in zip(SMALL, small_total)})
    packs = [_pack([src[n] if p == "" else given[p + n] for n in SMALL], width)
             for p, src in (("", w), ("", grads), ("m_", None), ("v_", None))]
    like = [w[n] for n in SMALL]
    for out, packed in zip((delta, new_m, new_v), _adamw_call(*packs)):
        out.update(dict(zip(SMALL, _unpack(packed, like))))

    return (loss, g_x[None], *[grads[n] for n in WEIGHTS], *[delta[n] for n in WEIGHTS],
            *[new_m[n] for n in WEIGHTS], *[new_v[n] for n in WEIGHTS])
```

```python
import jax
import jax.numpy as jnp
from jax import lax
from jax.experimental import pallas as pl
from jax.experimental.pallas import tpu as pltpu

F32 = jnp.float32
BF16 = jnp.bfloat16

N_DEV = 8
MESH_AXES = ("x", "y", "c")
HEAD_DIM = 64
TOKEN_TILE = 128
WKV_CHUNK = 64
WKV_PAIRS_PER_STEP = 8
PAIR = 2 * HEAD_DIM
ATTN_BLOCK = 128
ATTN_BLOCK_BIG = 384
ATTN_PAIRS_PER_STEP = 2
RMS_EPS = 1e-6
GN_EPS = 64e-5
L2_FLOOR = 1e-12
NEG_BIG = -1e30
ADAM_LR, ADAM_B1, ADAM_B2, ADAM_EPS, ADAM_WD, ADAM_STEP = 0.001, 0.9, 0.999, 1e-08, 0.01, 10
VMEM_LIMIT_CAP = 56 * 1024 * 1024
VMEM_LIMIT_FLOOR = 32 * 1024 * 1024
MATMUL_VMEM_BUDGET = 36 * 1024 * 1024
GRID_STEP_BYTES = 1024 * 1024
ACC_BYTES_PER_HBM_BYTE = 6


def _vmem_limit(estimate_bytes):
    return int(min(max(estimate_bytes * 5 // 4, VMEM_LIMIT_FLOOR), VMEM_LIMIT_CAP))


def _row_tile(rows, width, itemsize=4, budget=2 * 1024 * 1024):
    for c in (1408, 1024, 704, 512, 384, 256, 128, 64, 32, 16, 8):
        if rows % c == 0 and c * width * itemsize <= budget:
            return c
    return rows


def _row_tile_ragged(rows, width, itemsize=4, budget=2 * 1024 * 1024):
    tile = _row_tile(rows, width, itemsize, budget)
    if tile * width * itemsize <= budget or rows < 16:
        return tile
    padded = -(-rows // 16) * 16
    for c in (1408, 1024, 704, 512, 384, 336, 256, 192, 128, 96, 64, 48, 32, 16):
        if padded % c == 0 and c * width * itemsize <= budget:
            return c
    return tile


def _dg(a, b, ta, tb):
    dims = (((0 if ta else 1,), (1 if tb else 0,)), ((), ()))
    return lax.dot_general(a, b, dims, preferred_element_type=F32)


def _split(x, n):
    parts = []
    for _ in range(n):
        h = x.astype(BF16)
        parts.append(h)
        x = x - h.astype(F32)
    return parts


def _mm(a, b, ta=False, tb=False):
    return _dg(a.astype(BF16), b.astype(BF16), ta, tb)


def _matmul(a, b, ta=False, tb=False, out_dtype=F32, name="matmul", after=None, b_slots=False, out_slots=0, add=None):
    if ta:
        kdim, m = a.shape
    else:
        m, kdim = a.shape
    if b_slots:
        n_slots, brows, bcols = b.shape
        n, k2 = (brows, n_slots * bcols) if tb else (n_slots * bcols, brows)
    elif tb:
        n, k2 = b.shape
    else:
        k2, n = b.shape
    assert kdim == k2, (a.shape, b.shape, ta, tb)
    sa, sb, so = a.dtype.itemsize, b.dtype.itemsize, jnp.dtype(out_dtype).itemsize
    n_unit = bcols if (b_slots and not tb) else (n // out_slots if out_slots else n)
    k_unit = bcols if (b_slots and tb) else kdim
    tm, tn, tk, n_outer = _matmul_tiles(m, n, kdim, ta, sa, sb, so, n_unit, k_unit)
    nk = kdim // tk
    ij = (lambda f: lambda j, i, k: f(i, j, k)) if n_outer else (lambda f: f)

    order = () if after is None else (after,)
    extra = () if add is None else (add,)

    def body(a_ref, b_ref, *rest):
        rest = rest[len(order):]
        add_ref = rest[0] if extra else None
        o_ref, acc = rest[len(extra)], rest[len(extra) + 1:]
        part = _dg(a_ref[...].astype(BF16), b_ref[...].astype(BF16), ta, tb)
        done = lambda total: (total if add_ref is None else total + add_ref[...]).astype(o_ref.dtype)
        if nk == 1:
            o_ref[...] = done(part)
            return
        kk = pl.program_id(2)

        @pl.when(kk == 0)
        def _():
            acc[0][...] = part

        @pl.when(kk > 0)
        def _():
            acc[0][...] += part

        @pl.when(kk == nk - 1)
        def _():
            o_ref[...] = done(acc[0][...])

    a_spec = pl.BlockSpec((tk, tm), ij(lambda i, j, k: (k, i))) if ta else pl.BlockSpec((tm, tk), ij(lambda i, j, k: (i, k)))
    if b_slots and tb:
        per = bcols // tk
        b_spec = pl.BlockSpec((None, tn, tk), ij(lambda i, j, k: (k // per, j, k % per)))
    elif b_slots:
        per = bcols // tn
        b_spec = pl.BlockSpec((None, tk, tn), ij(lambda i, j, k: (j // per, k, j % per)))
    elif tb:
        b_spec = pl.BlockSpec((tn, tk), ij(lambda i, j, k: (j, k)))
    else:
        b_spec = pl.BlockSpec((tk, tn), ij(lambda i, j, k: (k, j)))
    if out_slots:
        per_out = n // out_slots // tn
        out_spec = pl.BlockSpec((None, tm, tn), ij(lambda i, j, k: (j // per_out, i, j % per_out)))
        out_shape = jax.ShapeDtypeStruct((out_slots, m, n // out_slots), out_dtype)
    else:
        out_spec = pl.BlockSpec((tm, tn), ij(lambda i, j, k: (i, j)))
        out_shape = jax.ShapeDtypeStruct((m, n), out_dtype)
    return pl.pallas_call(
        body, name=name,
        grid=(n // tn, m // tm, nk) if n_outer else (m // tm, n // tn, nk),
        in_specs=[a_spec, b_spec] + [pl.BlockSpec(memory_space=pl.ANY)] * len(order)
                 + [pl.BlockSpec((tm, tn), ij(lambda i, j, k: (i, j)))] * len(extra),
        out_specs=out_spec,
        out_shape=out_shape,
        scratch_shapes=[pltpu.VMEM((tm, tn), F32)] if nk > 1 else [],
        compiler_params=pltpu.CompilerParams(
            dimension_semantics=("parallel", "parallel", "arbitrary"),
            vmem_limit_bytes=_vmem_limit(_matmul_vmem(tm, tn, tk, nk, sa, sb, so) + 2 * tm * tn * 4 * len(extra))),
    )(a, b, *order, *extra)


def _matmul_vmem(tm, tn, tk, nk, sa, sb, so):
    return 2 * (tm * tk * sa + tk * tn * sb + tm * tn * so) + tm * tn * 4 + (tm * tn * 4 if nk > 1 else 0)


def _matmul_tiles(m, n, kdim, ta, sa, sb, so, n_unit, k_unit):
    lane = (2816, 2176, 2048, 1408, 1024, 640, 512, 384, 256, 128)
    sublane = (2816, 2176, 2048, 1408, 1024, 704, 512, 384, 256, 128)
    divs = lambda dim, cands: [c for c in cands if dim % c == 0] or [dim]
    best = None
    for tm in divs(m, lane if ta else sublane):
        for tn in divs(n_unit, lane):
            for tk in divs(k_unit, sublane if ta else lane) + ([kdim] if k_unit == kdim and (ta or kdim <= 2048) else []):
                nk, nm, nn = kdim // tk, m // tm, n // tn
                if _matmul_vmem(tm, tn, tk, nk, sa, sb, so) > MATMUL_VMEM_BUDGET:
                    continue
                acc_bytes = m * n * 4 * 3 * nk // ACC_BYTES_PER_HBM_BYTE if nk > 1 else 0
                fixed = m * n * so + acc_bytes + nm * nn * nk * GRID_STEP_BYTES
                for n_outer in (False, True):
                    if n_outer:
                        a_reads, b_reads = (1 if (nk == 1 and nm == 1) else nn), (1 if nk == 1 else nm)
                    else:
                        a_reads, b_reads = (1 if nk == 1 else nn), (1 if (nk == 1 and nn == 1) else nm)
                    cost = m * kdim * sa * a_reads + kdim * n * sb * b_reads + fixed
                    if best is None or cost < best[0]:
                        best = (cost, tm, tn, tk, n_outer)
    return best[1:]


@jax.custom_vjp
def dense(x, w):
    return _matmul(x.astype(BF16), w, name="dense_fwd")


def _dense_fwd(x, w):
    return _matmul(x.astype(BF16), w, name="dense_fwd"), (x.astype(BF16), w, jnp.zeros((), x.dtype))


def _dense_bwd(res, dy):
    xb, w, like = res
    dyb = dy.astype(BF16)
    dx = _matmul(dyb, w, tb=True, out_dtype=like.dtype, name="dense_dx")
    dw = _matmul(xb, dyb, ta=True, out_dtype=w.dtype, name="dense_dw")
    return dx, dw


dense.defvjp(_dense_fwd, _dense_bwd)


@jax.custom_vjp
def dense_add(x, w, res):
    return _matmul(x.astype(BF16), w, name="dense_add_fwd", add=res)


def _dense_add_fwd(x, w, res):
    return _matmul(x.astype(BF16), w, name="dense_add_fwd", add=res), (x.astype(BF16), w, jnp.zeros((), x.dtype))


def _dense_add_bwd(res, dy):
    return (*_dense_bwd(res, dy), dy)


dense_add.defvjp(_dense_add_fwd, _dense_add_bwd)


def _make_dense_cols(out_dtype):
    @jax.custom_vjp
    def op(x, w_slots):
        return _matmul(x.astype(BF16), w_slots, b_slots=True, out_dtype=out_dtype, name="dense_cols_fwd")

    def fwd(x, w_slots):
        xb = x.astype(BF16)
        return (_matmul(xb, w_slots, b_slots=True, out_dtype=out_dtype, name="dense_cols_fwd"),
                (xb, w_slots, jnp.zeros((), x.dtype)))

    def bwd(res, dy):
        xb, w_slots, like = res
        dyb = dy.astype(BF16)
        dx = _matmul(dyb, w_slots, tb=True, b_slots=True, out_dtype=like.dtype, name="dense_cols_dx")
        dw = _matmul(xb, dyb, ta=True, out_slots=w_slots.shape[0], out_dtype=w_slots.dtype, name="dense_cols_dw")
        return dx, dw

    op.defvjp(fwd, bwd)
    return op


dense_cols_bf16 = _make_dense_cols(BF16)


def _swiglu_call(gu, d_act=None):
    rows, two_f = gu.shape
    f = two_f // 2
    tr = _row_tile(rows, two_f, itemsize=2, budget=3 * 1024 * 1024)
    half = lambda j: pl.BlockSpec((tr, f), lambda i, j=j: (i, j))
    ops = (gu, gu) if d_act is None else (gu, gu, d_act)

    def body(*refs):
        g, u = refs[0][...].astype(F32), refs[1][...].astype(F32)
        s = 1.0 / (1.0 + jnp.exp(-g))
        if d_act is None:
            refs[2][...] = (g * s * u).astype(BF16)
        else:
            d = refs[2][...].astype(F32)
            refs[3][:, :f] = (d * u * s * (1.0 + g * (1.0 - s))).astype(BF16)
            refs[3][:, f:] = (d * g * s).astype(BF16)

    width = f if d_act is None else two_f
    return pl.pallas_call(
        body, name="swiglu_fwd" if d_act is None else "swiglu_bwd", grid=(rows // tr,),
        in_specs=[half(0), half(1)] + ([half(0)] if d_act is not None else []),
        out_specs=pl.BlockSpec((tr, width), lambda i: (i, 0)),
        out_shape=jax.ShapeDtypeStruct((rows, width), BF16),
        compiler_params=pltpu.CompilerParams(dimension_semantics=("parallel",)),
    )(*ops)


@jax.custom_vjp
def swiglu(gu):
    return _swiglu_call(gu)


swiglu.defvjp(lambda gu: (_swiglu_call(gu), gu), lambda gu, d_act: (_swiglu_call(gu, d_act),))


def _merge_call(zg, a, b, dm=None):
    rows, d = a.shape
    tr = _row_tile(rows, d, budget=1024 * 1024)
    half = lambda j: pl.BlockSpec((tr, d), lambda i, j=j: (i, j))
    tile = half(0)

    def body(*refs):
        ga = 1.0 / (1.0 + jnp.exp(-refs[0][...].astype(F32)))
        gb = 1.0 / (1.0 + jnp.exp(-refs[1][...].astype(F32)))
        av, bv = refs[2][...].astype(F32), refs[3][...].astype(F32)
        if dm is None:
            refs[4][...] = (ga * av + gb * bv).astype(BF16)
        else:
            dv = refs[4][...].astype(F32)
            dzg_ref, da_ref, db_ref = refs[5:]
            dzg_ref[:, :d] = (dv * av * ga * (1.0 - ga)).astype(dzg_ref.dtype)
            dzg_ref[:, d:] = (dv * bv * gb * (1.0 - gb)).astype(dzg_ref.dtype)
            da_ref[...] = (dv * ga).astype(BF16)
            db_ref[...] = (dv * gb).astype(BF16)

    shape_b = jax.ShapeDtypeStruct((rows, d), BF16)
    if dm is None:
        out_specs, out_shape, ops = tile, shape_b, (zg, zg, a, b)
    else:
        out_specs = [pl.BlockSpec((tr, 2 * d), lambda i: (i, 0)), tile, tile]
        out_shape = [jax.ShapeDtypeStruct((rows, 2 * d), zg.dtype), shape_b, shape_b]
        ops = (zg, zg, a, b, dm)
    return pl.pallas_call(
        body, name="merge_fwd" if dm is None else "merge_bwd", grid=(rows // tr,),
        in_specs=[half(0), half(1)] + [tile] * (len(ops) - 2),
        out_specs=out_specs, out_shape=out_shape,
        compiler_params=pltpu.CompilerParams(dimension_semantics=("parallel",)),
    )(*ops)


@jax.custom_vjp
def gated_merge(zg, a, b):
    return _merge_call(zg, a, b)


gated_merge.defvjp(lambda zg, a, b: (_merge_call(zg, a, b), (zg, a, b)),
                   lambda res, dm: tuple(_merge_call(*res, dm)))


def _rms_fwd_call(x, g):
    rows, d = x.shape
    tr = _row_tile(rows, d)

    def body(x_ref, g_ref, y_ref):
        xv = x_ref[...]
        rstd = lax.rsqrt(jnp.mean(xv * xv, axis=1, keepdims=True) + RMS_EPS)
        y_ref[...] = ((xv * rstd) * g_ref[...]).astype(BF16)

    return pl.pallas_call(
        body, name="rms_fwd", grid=(rows // tr,),
        in_specs=[pl.BlockSpec((tr, d), lambda i: (i, 0)), pl.BlockSpec((1, d), lambda i: (0, 0))],
        out_specs=pl.BlockSpec((tr, d), lambda i: (i, 0)),
        out_shape=jax.ShapeDtypeStruct((rows, d), BF16),
        compiler_params=pltpu.CompilerParams(dimension_semantics=("parallel",)),
    )(x, g)


def _rms_bwd_call(x, g, dy):
    rows, d = x.shape
    tr = _row_tile(rows, d)

    def body(x_ref, g_ref, dy_ref, dx_ref, dg_ref):
        @pl.when(pl.program_id(0) == 0)
        def _():
            dg_ref[...] = jnp.zeros_like(dg_ref)

        xv = x_ref[...]
        dyv = dy_ref[...].astype(F32)
        rstd = lax.rsqrt(jnp.mean(xv * xv, axis=1, keepdims=True) + RMS_EPS)
        xhat = xv * rstd
        dxhat = dyv * g_ref[...]
        dx_ref[...] = rstd * (dxhat - xhat * jnp.mean(dxhat * xhat, axis=1, keepdims=True))
        dg_ref[...] += jnp.sum(dyv * xhat, axis=0, keepdims=True)

    return pl.pallas_call(
        body, name="rms_bwd", grid=(rows // tr,),
        in_specs=[pl.BlockSpec((tr, d), lambda i: (i, 0)), pl.BlockSpec((1, d), lambda i: (0, 0)),
                  pl.BlockSpec((tr, d), lambda i: (i, 0))],
        out_specs=[pl.BlockSpec((tr, d), lambda i: (i, 0)), pl.BlockSpec((1, d), lambda i: (0, 0))],
        out_shape=[jax.ShapeDtypeStruct((rows, d), F32), jax.ShapeDtypeStruct((1, d), F32)],
        compiler_params=pltpu.CompilerParams(dimension_semantics=("arbitrary",)),
    )(x, g, dy)


@jax.custom_vjp
def rmsnorm(x, g):
    return _rms_fwd_call(x, g)


rmsnorm.defvjp(lambda x, g: (_rms_fwd_call(x, g), (x, g)), lambda res, dy: tuple(_rms_bwd_call(res[0], res[1], dy)))


def _bcast_add_call(x, p):
    rows, d = x.shape
    tr = _row_tile(rows, d)

    def body(x_ref, p_ref, y_ref):
        y_ref[...] = x_ref[...] + p_ref[...]

    return pl.pallas_call(
        body, name="bcast_add", grid=(rows // tr,),
        in_specs=[pl.BlockSpec((tr, d), lambda i: (i, 0)), pl.BlockSpec((1, d), lambda i: (0, 0))],
        out_specs=pl.BlockSpec((tr, d), lambda i: (i, 0)),
        out_shape=jax.ShapeDtypeStruct((rows, d), F32),
        compiler_params=pltpu.CompilerParams(dimension_semantics=("parallel",)),
    )(x, p)


def _colsum_call(a):
    rows, d = a.shape
    tr = _row_tile(rows, d)

    def body(a_ref, o_ref):
        @pl.when(pl.program_id(0) == 0)
        def _():
            o_ref[...] = jnp.zeros_like(o_ref)

        o_ref[...] += jnp.sum(a_ref[...], axis=0, keepdims=True)

    return pl.pallas_call(
        body, name="colsum", grid=(rows // tr,),
        in_specs=[pl.BlockSpec((tr, d), lambda i: (i, 0))],
        out_specs=pl.BlockSpec((1, d), lambda i: (0, 0)),
        out_shape=jax.ShapeDtypeStruct((1, d), F32),
        compiler_params=pltpu.CompilerParams(dimension_semantics=("arbitrary",)),
    )(a)


@jax.custom_vjp
def badd(x, p):
    return _bcast_add_call(x, p)


badd.defvjp(lambda x, p: (_bcast_add_call(x, p), None), lambda res, dy: (dy, _colsum_call(dy)))


def _head_sums(x):
    i = lax.broadcasted_iota(jnp.int32, (PAIR, PAIR), 0) // HEAD_DIM
    j = lax.broadcasted_iota(jnp.int32, (PAIR, PAIR), 1) // HEAD_DIM
    ones = jnp.where(i == j, 1.0, 0.0).astype(BF16)
    hi, lo = _split(x, 2)
    cols = [slice(p * PAIR, (p + 1) * PAIR) for p in range(x.shape[1] // PAIR)]
    return jnp.concatenate([_dg(hi[:, c], ones, False, False) + _dg(lo[:, c], ones, False, False) for c in cols], axis=1)


def _head_rms_fwd_call(x, g):
    rows, w = x.shape
    tr = _row_tile(rows, w, budget=1024 * 1024)

    def body(x_ref, g_ref, y_ref):
        xv = x_ref[...]
        rstd = lax.rsqrt(_head_sums(xv * xv) * (1.0 / HEAD_DIM) + RMS_EPS)
        y_ref[...] = (xv * rstd) * g_ref[...]

    return pl.pallas_call(
        body, name="head_rms_fwd", grid=(rows // tr,),
        in_specs=[pl.BlockSpec((tr, w), lambda i: (i, 0)), pl.BlockSpec((1, w), lambda i: (0, 0))],
        out_specs=pl.BlockSpec((tr, w), lambda i: (i, 0)),
        out_shape=jax.ShapeDtypeStruct((rows, w), F32),
        compiler_params=pltpu.CompilerParams(dimension_semantics=("parallel",)),
    )(x, g)


def _head_rms_bwd_call(x, g, dy):
    rows, w = x.shape
    tr = _row_tile(rows, w, budget=1024 * 1024)

    def body(x_ref, g_ref, dy_ref, dx_ref, dg_ref):
        @pl.when(pl.program_id(0) == 0)
        def _():
            dg_ref[...] = jnp.zeros_like(dg_ref)

        xv, dyv = x_ref[...], dy_ref[...]
        rstd = lax.rsqrt(_head_sums(xv * xv) * (1.0 / HEAD_DIM) + RMS_EPS)
        xhat = xv * rstd
        dxhat = dyv * g_ref[...]
        dx_ref[...] = rstd * (dxhat - xhat * (_head_sums(dxhat * xhat) * (1.0 / HEAD_DIM)))
        dg_ref[...] += jnp.sum(dyv * xhat, axis=0, keepdims=True)

    return pl.pallas_call(
        body, name="head_rms_bwd", grid=(rows // tr,),
        in_specs=[pl.BlockSpec((tr, w), lambda i: (i, 0)), pl.BlockSpec((1, w), lambda i: (0, 0)),
                  pl.BlockSpec((tr, w), lambda i: (i, 0))],
        out_specs=[pl.BlockSpec((tr, w), lambda i: (i, 0)), pl.BlockSpec((1, w), lambda i: (0, 0))],
        out_shape=[jax.ShapeDtypeStruct((rows, w), F32), jax.ShapeDtypeStruct((1, w), F32)],
        compiler_params=pltpu.CompilerParams(dimension_semantics=("arbitrary",)),
    )(x, g, dy)


@jax.custom_vjp
def head_rms(x, g):
    return _head_rms_fwd_call(x, g)


head_rms.defvjp(lambda x, g: (_head_rms_fwd_call(x, g), (x, g)),
                lambda res, dy: tuple(_head_rms_bwd_call(res[0], res[1], dy)))


def _gn_fwd_call(y, r, kf, v, g, gw, gb, rk):
    rows, w = y.shape
    tr = _row_tile(rows, w, budget=512 * 1024)

    def body(y_ref, r_ref, kf_ref, v_ref, g_ref, gw_ref, gb_ref, rk_ref, o_ref):
        yv = y_ref[...]
        yc = yv - _head_sums(yv) * (1.0 / HEAD_DIM)
        rstd = lax.rsqrt(_head_sums(yc * yc) * (1.0 / HEAD_DIM) + GN_EPS)
        s = _head_sums(r_ref[...] * kf_ref[...] * rk_ref[...])
        o_ref[...] = (((yc * rstd) * gw_ref[...] + gb_ref[...] + s * v_ref[...]) * g_ref[...]).astype(BF16)

    tok = pl.BlockSpec((tr, w), lambda i: (i, 0))
    par = pl.BlockSpec((1, w), lambda i: (0, 0))
    return pl.pallas_call(
        body, name="gn_bonus_fwd", grid=(rows // tr,),
        in_specs=[tok] * 5 + [par] * 3, out_specs=tok,
        out_shape=jax.ShapeDtypeStruct((rows, w), BF16),
        compiler_params=pltpu.CompilerParams(dimension_semantics=("parallel",)),
    )(y, r, kf, v, g, gw, gb, rk)


def _gn_bwd_call(y, r, kf, v, g, gw, gb, rk, do):
    rows, w = y.shape
    tr = _row_tile(rows, w, budget=512 * 1024)

    def body(y_ref, r_ref, kf_ref, v_ref, g_ref, gw_ref, gb_ref, rk_ref, do_ref,
             dy_ref, dr_ref, dkf_ref, dv_ref, dg_ref, dgw_ref, dgb_ref, drk_ref):
        @pl.when(pl.program_id(0) == 0)
        def _():
            dgw_ref[...] = jnp.zeros_like(dgw_ref)
            dgb_ref[...] = jnp.zeros_like(dgb_ref)
            drk_ref[...] = jnp.zeros_like(drk_ref)

        yv, rv, kv, vv, rkv = y_ref[...], r_ref[...], kf_ref[...], v_ref[...], rk_ref[...]
        mean = lambda t: _head_sums(t) * (1.0 / HEAD_DIM)
        yc = yv - mean(yv)
        rstd = lax.rsqrt(mean(yc * yc) + GN_EPS)
        yhat = yc * rstd
        s = _head_sums(rv * kv * rkv)
        do = do_ref[...].astype(F32)
        dg_ref[...] = do * (yhat * gw_ref[...] + gb_ref[...] + s * vv)
        dov = do * g_ref[...]
        dyhat = dov * gw_ref[...]
        dy_ref[...] = rstd * (dyhat - mean(dyhat) - yhat * mean(dyhat * yhat))
        ds = _head_sums(dov * vv)
        dv_ref[...] = s * dov
        dr_ref[...] = ds * kv * rkv
        dkf_ref[...] = ds * rv * rkv
        dgw_ref[...] += jnp.sum(dov * yhat, axis=0, keepdims=True)
        dgb_ref[...] += jnp.sum(dov, axis=0, keepdims=True)
        drk_ref[...] += jnp.sum(ds * rv * kv, axis=0, keepdims=True)

    tok = pl.BlockSpec((tr, w), lambda i: (i, 0))
    par = pl.BlockSpec((1, w), lambda i: (0, 0))
    tshape = jax.ShapeDtypeStruct((rows, w), F32)
    pshape = jax.ShapeDtypeStruct((1, w), F32)
    return pl.pallas_call(
        body, name="gn_bonus_bwd", grid=(rows // tr,),
        in_specs=[tok] * 5 + [par] * 3 + [tok], out_specs=[tok] * 5 + [par] * 3,
        out_shape=[tshape] * 5 + [pshape] * 3,
        compiler_params=pltpu.CompilerParams(dimension_semantics=("arbitrary",)),
    )(y, r, kf, v, g, gw, gb, rk, do)


@jax.custom_vjp
def gn_bonus(y, r, kf, v, g, gw, gb, rk):
    return _gn_fwd_call(y, r, kf, v, g, gw, gb, rk)


def _gn_bwd(res, do):
    return tuple(_gn_bwd_call(*res, do))


gn_bonus.defvjp(lambda *a: (_gn_fwd_call(*a), a), _gn_bwd)


PREP_ROWS = 128


def _prep_segments(rw, lora_w, lora_a, lora_g):
    at = 3 * rw
    seg = {"r": (0, rw), "k": (rw, 2 * rw), "v": (2 * rw, 3 * rw)}
    for name, n in (("wd", lora_w), ("ad", lora_a), ("gd", lora_g)):
        seg[name] = (at, at + _pad128(n))
        at += _pad128(n)
    return seg, at


def _prep_shifted(z_ref, zlast_ref, mu_ref, seg, first_tile):
    lo, hi = seg
    zr = z_ref[:, lo:hi]
    rows = zr.shape[0]
    before = jnp.where(first_tile, 0.0, zlast_ref[7:8, lo:hi])
    row0 = lax.broadcasted_iota(jnp.int32, zr.shape, 0) == 0
    diff = jnp.where(row0, before, pltpu.roll(zr, 1, axis=0)) - zr
    return zr + diff * mu_ref[:, lo:hi], diff


def _prep_forward_values(z_ref, zlast_ref, mu_ref, w0_ref, a0_ref, kk_ref, ka_ref, w2_ref, a2_ref, g2_ref, segs, first_tile):
    z = {n: _prep_shifted(z_ref, zlast_ref, mu_ref, segs[n], first_tile) for n in segs}
    r, k, v, wd, ad, gd = (z[n][0] for n in ("r", "k", "v", "wd", "ad", "gd"))
    twd = jnp.tanh(wd)
    pw = _mm(twd, w2_ref[...]) + w0_ref[...]
    lw = -jnp.exp(-(jnp.maximum(-pw, 0.0) + jnp.log(1.0 + jnp.exp(-jnp.abs(pw)))) - 0.5)
    a_sig = 1.0 / (1.0 + jnp.exp(-(_mm(ad, a2_ref[...]) + a0_ref[...])))
    sg = 1.0 / (1.0 + jnp.exp(-gd))
    kx = k * kk_ref[...]
    nrm = jnp.sqrt(_head_sums(kx * kx))
    inv = 1.0 / jnp.maximum(nrm, L2_FLOOR)
    return dict(z=z, r=r, k=k, v=v, twd=twd, pw=pw, lw=lw, a_sig=a_sig, sg=sg, ad=ad, kk=kx * inv, inv=inv, live=nrm > L2_FLOOR)


def _prep_specs(tokens, rpad, rw, w2, a2, g2):
    tr = PREP_ROWS
    tile = lambda w: pl.BlockSpec((tr, w), lambda i: (i, 0))
    before = pl.BlockSpec((8, rpad), lambda i: (jnp.maximum(i * (tr // 8) - 1, 0), 0))
    whole = lambda a: pl.BlockSpec(a.shape, lambda i: (0, 0))
    par = pl.BlockSpec((1, rw), lambda i: (0, 0))
    return tile, before, whole, par, pl.BlockSpec((1, rpad), lambda i: (0, 0))


def _prep_fwd_call(zr, mu, w0, a0, k_k, k_a, w2, a2, g2):
    tokens, rpad = zr.shape
    rw = w0.shape[1]
    segs, _ = _prep_segments(rw, w2.shape[0], a2.shape[0], g2.shape[0])
    tile, before, whole, par, mu_spec = _prep_specs(tokens, rpad, rw, w2, a2, g2)

    def body(z_ref, zlast_ref, mu_ref, w0_ref, a0_ref, kk_ref, ka_ref, w2_ref, a2_ref, g2_ref,
             r_ref, lw_ref, kf_ref, v_ref, na_ref, b_ref, g_ref):
        f = _prep_forward_values(z_ref, zlast_ref, mu_ref, w0_ref, a0_ref, kk_ref, ka_ref, w2_ref, a2_ref, g2_ref,
                                 segs, pl.program_id(0) == 0)
        r_ref[...] = f["r"]
        v_ref[...] = f["v"]
        lw_ref[...] = f["lw"]
        kf_ref[...] = f["k"] * (1.0 + (f["a_sig"] - 1.0) * ka_ref[...])
        na_ref[...] = -f["kk"]
        b_ref[...] = f["kk"] * f["a_sig"]
        g_ref[...] = _mm(f["sg"], g2_ref[...])

    shape = jax.ShapeDtypeStruct((tokens, rw), F32)
    return pl.pallas_call(
        body, name="rwkv_prep_fwd", grid=(tokens // PREP_ROWS,),
        in_specs=[tile(rpad), before, mu_spec, par, par, par, par, whole(w2), whole(a2), whole(g2)],
        out_specs=[tile(rw)] * 7, out_shape=[shape] * 7,
        compiler_params=pltpu.CompilerParams(dimension_semantics=("parallel",), vmem_limit_bytes=VMEM_LIMIT_CAP),
    )(zr, zr, mu, w0, a0, k_k, k_a, w2, a2, g2)


def _prep_bwd_call(zr, mu, w0, a0, k_k, k_a, w2, a2, g2, cts):
    tokens, rpad = zr.shape
    rw = w0.shape[1]
    segs, _ = _prep_segments(rw, w2.shape[0], a2.shape[0], g2.shape[0])
    tile, before, whole, par, mu_spec = _prep_specs(tokens, rpad, rw, w2, a2, g2)
    nt = tokens // PREP_ROWS
    rev = lambda spec: pl.BlockSpec(spec.block_shape, lambda i, f=spec.index_map: f(nt - 1 - i))

    def body(z_ref, zlast_ref, mu_ref, w0_ref, a0_ref, kk_ref, ka_ref, w2_ref, a2_ref, g2_ref,
             dr_ref, dlw_ref, dkf_ref, dv_ref, dna_ref, db_ref, dg_ref,
             dz_ref, dmu_ref, dw0_ref, da0_ref, dkk_ref, dka_ref, dw2_ref, da2_ref, dg2_ref, carry):
        step = pl.program_id(0)

        @pl.when(step == 0)
        def _():
            for ref in (dmu_ref, dw0_ref, da0_ref, dkk_ref, dka_ref, dw2_ref, da2_ref, dg2_ref, carry):
                ref[...] = jnp.zeros_like(ref)

        f = _prep_forward_values(z_ref, zlast_ref, mu_ref, w0_ref, a0_ref, kk_ref, ka_ref, w2_ref, a2_ref, g2_ref,
                                 segs, step == nt - 1)
        k, kk, a_sig, sg, twd = f["k"], f["kk"], f["a_sig"], f["sg"], f["twd"]
        colsum = lambda t: jnp.sum(t, axis=0, keepdims=True)
        dkf, db, dg = dkf_ref[...], db_ref[...], dg_ref[...]
        ka = ka_ref[...]
        dgd = _mm(dg, g2_ref[...], tb=True) * sg * (1.0 - sg)
        dg2_ref[...] += _mm(sg, dg, ta=True)
        dkk = db * a_sig - dna_ref[...]
        da_sig = db * kk + dkf * k * ka
        dk = dkf * (1.0 + (a_sig - 1.0) * ka)
        dka_ref[...] += colsum(dkf * k * (a_sig - 1.0))
        along = jnp.where(f["live"], _head_sums(dkk * kk), 0.0)
        dkx = (dkk - kk * along) * f["inv"]
        dk = dk + dkx * kk_ref[...]
        dkk_ref[...] += colsum(dkx * k)
        dpa = da_sig * a_sig * (1.0 - a_sig)
        da0_ref[...] += colsum(dpa)
        dad = _mm(dpa, a2_ref[...], tb=True)
        da2_ref[...] += _mm(f["ad"], dpa, ta=True)
        dpw = dlw_ref[...] * f["lw"] / (1.0 + jnp.exp(f["pw"]))
        dw0_ref[...] += colsum(dpw)
        dwd = _mm(dpw, w2_ref[...], tb=True) * (1.0 - twd * twd)
        dw2_ref[...] += _mm(twd, dpw, ta=True)
        rows = PREP_ROWS
        last = lax.broadcasted_iota(jnp.int32, (rows, 1), 0) == rows - 1
        for name, dz in (("r", dr_ref[...]), ("k", dk), ("v", dv_ref[...]), ("wd", dwd), ("ad", dad), ("gd", dgd)):
            lo, hi = segs[name]
            mu_s = mu_ref[:, lo:hi]
            dmu_ref[:, lo:hi] += colsum(dz * f["z"][name][1])
            later = dz * mu_s
            dz_ref[:, lo:hi] = dz * (1.0 - mu_s) + jnp.where(last, carry[:, lo:hi], pltpu.roll(later, rows - 1, axis=0))
            carry[:, lo:hi] = later[0:1, :]

    tok = jax.ShapeDtypeStruct((tokens, rw), F32)
    acc = lambda a: jax.ShapeDtypeStruct(a.shape, F32)
    return pl.pallas_call(
        body, name="rwkv_prep_bwd", grid=(nt,),
        in_specs=[rev(tile(rpad)), rev(before), mu_spec, par, par, par, par, whole(w2), whole(a2), whole(g2)]
                 + [rev(tile(rw))] * 7,
        out_specs=[rev(tile(rpad)), mu_spec, par, par, par, par, whole(w2), whole(a2), whole(g2)],
        out_shape=[jax.ShapeDtypeStruct((tokens, rpad), F32), acc(mu), acc(w0), acc(a0), acc(k_k), acc(k_a), acc(w2), acc(a2), acc(g2)],
        scratch_shapes=[pltpu.VMEM((1, rpad), F32)],
        compiler_params=pltpu.CompilerParams(dimension_semantics=("arbitrary",), vmem_limit_bytes=VMEM_LIMIT_CAP),
    )(zr, zr, mu, w0, a0, k_k, k_a, w2, a2, g2, *cts)


@jax.custom_vjp
def rwkv_prep(zr, mu, w0, a0, k_k, k_a, w2, a2, g2):
    return tuple(_prep_fwd_call(zr, mu, w0, a0, k_k, k_a, w2, a2, g2))


def _rwkv_prep_bwd(res, cts):
    zr, mu, w0, a0, k_k, k_a, w2, a2, g2 = res
    dz, dmu, dw0, da0, dkk, dka, dw2, da2, dg2 = _prep_bwd_call(*res, cts)
    return dz, dmu, dw0, da0, dkk, dka, dw2.astype(w2.dtype), da2.astype(a2.dtype), dg2.astype(g2.dtype)


rwkv_prep.defvjp(lambda *a: (tuple(_prep_fwd_call(*a)), a), _rwkv_prep_bwd)


def _pair_masks(rows):
    lane = lax.broadcasted_iota(jnp.int32, (rows, PAIR), 1)
    return lane < HEAD_DIM, lane >= HEAD_DIM


def _bd(x):
    m0, m1 = _pair_masks(x.shape[0])
    return jnp.concatenate([jnp.where(m0, x, 0.0), jnp.where(m1, x, 0.0)], axis=0)


def _unbd(m, c):
    return jnp.where(_pair_masks(c)[0], m[:c], m[c:])


def _pair_a(l2, r2):
    return _mm(l2, _bd(r2), tb=True)


def _pair_mul(p2, x2):
    return _mm(p2, _bd(x2))


def _pair_mul_t(p2, x2):
    return _unbd(_mm(p2, x2, ta=True), p2.shape[0])


def _block_diag_mask():
    row = lax.broadcasted_iota(jnp.int32, (PAIR, PAIR), 0)
    lane = lax.broadcasted_iota(jnp.int32, (PAIR, PAIR), 1)
    return (row < HEAD_DIM) == (lane < HEAD_DIM), row == lane


def _wkv_pair_common(r, lw, k, a, b):
    c = r[0].shape[0]
    pairs = range(len(r))
    i = lax.broadcasted_iota(jnp.int32, (c, PAIR), 0)
    j = lax.broadcasted_iota(jnp.int32, (c, PAIR), 1) % c
    strict, incl = i > j, i >= j
    ti = lax.broadcasted_iota(jnp.int32, (c, c), 0)
    tj = lax.broadcasted_iota(jnp.int32, (c, c), 1)
    tri = jnp.where(ti >= tj, 1.0, 0.0).astype(BF16)
    lc = [sum(_dg(tri, part, False, False) for part in _split(lw[p], 3)) for p in pairs]
    lend = [lc[p][c - 1:c, :] for p in pairs]
    rt = [r[p] * jnp.exp(lc[p]) for p in pairs]
    at = [a[p] * jnp.exp(lc[p] - lw[p]) for p in pairs]
    pinv = [jnp.exp(-lc[p]) for p in pairs]
    kt = [k[p] * pinv[p] for p in pairs]
    bt = [b[p] * pinv[p] for p in pairs]
    e = [jnp.exp(lend[p] - lc[p]) for p in pairs]
    ktp = [k[p] * e[p] for p in pairs]
    btp = [b[p] * e[p] for p in pairs]
    a_ab = [jnp.where(strict, _pair_a(at[p], bt[p]), 0.0) for p in pairs]
    a_ak = [jnp.where(strict, _pair_a(at[p], kt[p]), 0.0) for p in pairs]
    a_rb = [jnp.where(incl, _pair_a(rt[p], bt[p]), 0.0) for p in pairs]
    a_rk = [jnp.where(incl, _pair_a(rt[p], kt[p]), 0.0) for p in pairs]
    t = [jnp.where(i == j, 1.0, 0.0) + a_ab[p] for p in pairs]
    xp = a_ab
    n = 2
    while n < c:
        xp = [_pair_mul(xp[p], xp[p]) for p in pairs]
        t = [t[p] + _pair_mul(t[p], xp[p]) for p in pairs]
        n *= 2
    bdm, eye = _block_diag_mask()
    pend_col = [jnp.sum(jnp.where(eye, jnp.exp(lend[p]), 0.0), axis=1, keepdims=True) for p in pairs]
    return dict(rt=rt, at=at, kt=kt, bt=bt, ktp=ktp, btp=btp, a_ak=a_ak, a_rb=a_rb, a_rk=a_rk, t=t,
                pend_col=pend_col, lend=lend, lc=lc, strict=strict, incl=incl, tri=tri, bdm=bdm)


def _wkv_group(width):
    npair = width // PAIR
    g = min(WKV_PAIRS_PER_STEP, npair)
    assert npair % g == 0
    return npair, g


def _wkv_fwd_call(r, lw, k, v, a, b):
    tokens, width = r.shape
    c = WKV_CHUNK
    nc = tokens // c
    npair, g = _wkv_group(width)

    def body(r_ref, lw_ref, k_ref, v_ref, a_ref, b_ref, y_ref, s_ref, st):
        @pl.when(pl.program_id(1) == 0)
        def _():
            st[...] = jnp.zeros_like(st)

        pairs = range(g)
        rv, lwv, kv, vv, av, bv = ([ref[:, p * PAIR:(p + 1) * PAIR] for p in pairs]
                                   for ref in (r_ref, lw_ref, k_ref, v_ref, a_ref, b_ref))
        s0 = [st[p] for p in pairs]
        q = _wkv_pair_common(rv, lwv, kv, av, bv)
        w1 = [_mm(q["at"][p], s0[p]) + _pair_mul(q["a_ak"][p], vv[p]) for p in pairs]
        u = [_pair_mul(q["t"][p], w1[p]) for p in pairs]
        y = [_mm(q["rt"][p], s0[p]) + _pair_mul(q["a_rb"][p], u[p]) + _pair_mul(q["a_rk"][p], vv[p]) for p in pairs]
        grow = [_mm(jnp.concatenate([q["btp"][p], q["ktp"][p]], axis=0), jnp.concatenate([u[p], vv[p]], axis=0), ta=True)
                for p in pairs]
        for p in pairs:
            y_ref[:, p * PAIR:(p + 1) * PAIR] = y[p]
            s_ref[0, p] = s0[p]
            st[p] = q["pend_col"][p] * s0[p] + jnp.where(q["bdm"], grow[p], 0.0)

    tok = pl.BlockSpec((c, g * PAIR), lambda gi, ci: (ci, gi))
    return pl.pallas_call(
        body, name="wkv_fwd", grid=(npair // g, nc),
        in_specs=[tok] * 6,
        out_specs=[tok, pl.BlockSpec((1, g, PAIR, PAIR), lambda gi, ci: (ci, gi, 0, 0))],
        out_shape=[jax.ShapeDtypeStruct((tokens, width), F32), jax.ShapeDtypeStruct((nc, npair, PAIR, PAIR), F32)],
        scratch_shapes=[pltpu.VMEM((g, PAIR, PAIR), F32)],
        compiler_params=pltpu.CompilerParams(dimension_semantics=("parallel", "arbitrary")),
    )(r, lw, k, v, a, b)


def _wkv_bwd_call(r, lw, k, v, a, b, s, dy):
    tokens, width = r.shape
    c = WKV_CHUNK
    nc = tokens // c
    npair, g = _wkv_group(width)

    def body(r_ref, lw_ref, k_ref, v_ref, a_ref, b_ref, s_ref, dy_ref,
             dr_ref, dlw_ref, dk_ref, dv_ref, da_ref, db_ref, dst):
        @pl.when(pl.program_id(1) == 0)
        def _():
            dst[...] = jnp.zeros_like(dst)

        pairs = range(g)
        rv, lwv, kv, vv, av, bv, dyv = ([ref[:, p * PAIR:(p + 1) * PAIR] for p in pairs]
                                        for ref in (r_ref, lw_ref, k_ref, v_ref, a_ref, b_ref, dy_ref))
        s0 = [s_ref[0, p] for p in pairs]
        dsc = [dst[p] for p in pairs]
        q = _wkv_pair_common(rv, lwv, kv, av, bv)
        rt, at, kt, bt, ktp, btp, t = (q[n] for n in ("rt", "at", "kt", "bt", "ktp", "btp", "t"))
        a_ak, a_rb, a_rk, strict, incl = (q[n] for n in ("a_ak", "a_rb", "a_rk", "strict", "incl"))
        w1 = [_mm(at[p], s0[p]) + _pair_mul(a_ak[p], vv[p]) for p in pairs]
        u = [_pair_mul(t[p], w1[p]) for p in pairs]
        du = [_pair_mul_t(a_rb[p], dyv[p]) + _mm(btp[p], dsc[p]) for p in pairs]
        dw1 = [_pair_mul_t(t[p], du[p]) for p in pairs]
        dv = [_pair_mul_t(a_rk[p], dyv[p]) + _mm(ktp[p], dsc[p]) + _pair_mul_t(a_ak[p], dw1[p]) for p in pairs]
        da_ab = [jnp.where(strict, _pair_a(dw1[p], u[p]), 0.0) for p in pairs]
        da_ak = [jnp.where(strict, _pair_a(dw1[p], vv[p]), 0.0) for p in pairs]
        da_rb = [jnp.where(incl, _pair_a(dyv[p], u[p]), 0.0) for p in pairs]
        da_rk = [jnp.where(incl, _pair_a(dyv[p], vv[p]), 0.0) for p in pairs]
        d_rt = [_mm(dyv[p], s0[p], tb=True) + _pair_mul(da_rb[p], bt[p]) + _pair_mul(da_rk[p], kt[p]) for p in pairs]
        d_at = [_mm(dw1[p], s0[p], tb=True) + _pair_mul(da_ab[p], bt[p]) + _pair_mul(da_ak[p], kt[p]) for p in pairs]
        d_bt = [_pair_mul_t(da_ab[p], at[p]) + _pair_mul_t(da_rb[p], rt[p]) for p in pairs]
        d_kt = [_pair_mul_t(da_ak[p], at[p]) + _pair_mul_t(da_rk[p], rt[p]) for p in pairs]
        d_btp = [_mm(u[p], dsc[p], tb=True) for p in pairs]
        d_ktp = [_mm(vv[p], dsc[p], tb=True) for p in pairs]
        ones = jnp.ones((8, PAIR), BF16)
        dpend = [sum(_dg(ones, part, False, True) for part in _split(dsc[p] * s0[p], 3))[0:1, :] * jnp.exp(q["lend"][p])
                 for p in pairs]
        grow = [_mm(jnp.concatenate([rt[p], at[p]], axis=0), jnp.concatenate([dyv[p], dw1[p]], axis=0), ta=True)
                for p in pairs]
        last = lax.broadcasted_iota(jnp.int32, (c, PAIR), 0) == c - 1
        for p in pairs:
            sl = slice(p * PAIR, (p + 1) * PAIR)
            dst[p] = q["pend_col"][p] * dsc[p] + jnp.where(q["bdm"], grow[p], 0.0)
            lc_e = d_ktp[p] * ktp[p] + d_btp[p] * btp[p]
            dlend = jnp.sum(lc_e, axis=0, keepdims=True) + dpend[p]
            dlc = d_rt[p] * rt[p] - d_kt[p] * kt[p] - d_bt[p] * bt[p] - lc_e + jnp.where(last, dlend, 0.0)
            dlp = d_at[p] * at[p]
            dlw_ref[:, sl] = sum(_dg(q["tri"], part, True, False) for part in _split(dlc + dlp, 3)) - dlp
            lc = q["lc"][p]
            pinv = jnp.exp(-lc)
            e = jnp.exp(q["lend"][p] - lc)
            dr_ref[:, sl] = d_rt[p] * jnp.exp(lc)
            da_ref[:, sl] = d_at[p] * jnp.exp(lc - lwv[p])
            dk_ref[:, sl] = d_kt[p] * pinv + d_ktp[p] * e
            db_ref[:, sl] = d_bt[p] * pinv + d_btp[p] * e
            dv_ref[:, sl] = dv[p]

    tok = pl.BlockSpec((c, g * PAIR), lambda gi, ci: (nc - 1 - ci, gi))
    tshape = jax.ShapeDtypeStruct((tokens, width), F32)
    return pl.pallas_call(
        body, name="wkv_bwd", grid=(npair // g, nc),
        in_specs=[tok] * 6 + [pl.BlockSpec((1, g, PAIR, PAIR), lambda gi, ci: (nc - 1 - ci, gi, 0, 0)), tok],
        out_specs=[tok] * 6, out_shape=[tshape] * 6,
        scratch_shapes=[pltpu.VMEM((g, PAIR, PAIR), F32)],
        compiler_params=pltpu.CompilerParams(dimension_semantics=("parallel", "arbitrary")),
    )(r, lw, k, v, a, b, s, dy)


@jax.custom_vjp
def wkv7(r, lw, k, v, a, b):
    return _wkv_fwd_call(r, lw, k, v, a, b)[0]


def _wkv7_fwd(r, lw, k, v, a, b):
    y, s = _wkv_fwd_call(r, lw, k, v, a, b)
    return y, (r, lw, k, v, a, b, s)


wkv7.defvjp(_wkv7_fwd, lambda res, dy: tuple(_wkv_bwd_call(*res, dy)))


def _attn_block(tokens):
    return ATTN_BLOCK_BIG if tokens % ATTN_BLOCK_BIG == 0 else ATTN_BLOCK


def _fox_layouts(cum):
    tokens, heads = cum.shape
    t = _attn_block(tokens)
    cq = cum.reshape(tokens, heads // 2, 2).transpose(1, 0, 2)
    ck = cum.T.reshape(heads // 2, 2, tokens // t, t).transpose(0, 2, 1, 3)
    return cq, ck


def _head_lane_masks(rows):
    lane = lax.broadcasted_iota(jnp.int32, (rows, 2 * HEAD_DIM), 1)
    return [lane < HEAD_DIM, lane >= HEAD_DIM]


def _fox_fwd_call(q, k, v, cq, ck):
    tokens, width = q.shape
    t = _attn_block(tokens)
    nb = tokens // t
    hd = HEAD_DIM
    npair = width // (2 * hd)

    g = ATTN_PAIRS_PER_STEP if npair % ATTN_PAIRS_PER_STEP == 0 else 1
    heads = [(pp, hh) for pp in range(g) for hh in range(2)]

    def body(q_ref, k_ref, v_ref, cq_ref, ck_ref, o_ref, lse_ref):
        i = pl.program_id(1)
        masks = _head_lane_masks(t)
        lanes = [slice(pp * PAIR, (pp + 1) * PAIR) for pp in range(g)]
        qs = [jnp.where(masks[hh], q_ref[:, lanes[pp]], 0.0).astype(BF16) for pp, hh in heads]
        cqs = [cq_ref[pp, :, hh:hh + 1] for pp, hh in heads]

        def block(j, carry, diagonal):
            off = pl.multiple_of(j * t, t)
            k2 = [k_ref[pl.ds(off, t), lanes[pp]].astype(BF16) for pp in range(g)]
            v2 = [v_ref[pl.ds(off, t), lanes[pp]].astype(BF16) for pp in range(g)]
            s = [_dg(qs[n], k2[pp], False, True) + (cqs[n] - ck_ref[pp, j][hh:hh + 1, :]) for n, (pp, hh) in enumerate(heads)]
            if diagonal:
                keep = lax.broadcasted_iota(jnp.int32, (t, t), 0) >= lax.broadcasted_iota(jnp.int32, (t, t), 1)
                s = [jnp.where(keep, x, NEG_BIG) for x in s]
            m_new = [jnp.maximum(carry[n][0], jnp.max(s[n], axis=1, keepdims=True)) for n in range(len(heads))]
            alpha = [jnp.exp(carry[n][0] - m_new[n]) for n in range(len(heads))]
            p = [jnp.exp(s[n] - m_new[n]) for n in range(len(heads))]
            l = [alpha[n] * carry[n][1] + jnp.sum(p[n], axis=1, keepdims=True) for n in range(len(heads))]
            acc = [alpha[n] * carry[n][2] + _dg(p[n].astype(BF16), v2[pp], False, False) for n, (pp, hh) in enumerate(heads)]
            return tuple(zip(m_new, l, acc))

        init = tuple((jnp.full((t, 1), NEG_BIG, F32), jnp.zeros((t, 1), F32), jnp.zeros((t, 2 * hd), F32)) for _ in heads)
        res = lax.fori_loop(0, i, lambda j, c: block(j, c, False), init)
        res = block(i, res, True)
        for pp in range(g):
            a, b = res[2 * pp], res[2 * pp + 1]
            o_ref[:, lanes[pp]] = jnp.where(masks[0], a[2] / a[1], b[2] / b[1])
        for n, (pp, hh) in enumerate(heads):
            lse_ref[pp, :, hh:hh + 1] = res[n][0] + jnp.log(res[n][1])

    blk = pl.BlockSpec((t, g * PAIR), lambda hp, i: (i, hp))
    full = pl.BlockSpec((tokens, g * PAIR), lambda hp, i: (0, hp))
    cq_spec = pl.BlockSpec((g, t, 2), lambda hp, i: (hp, i, 0))
    ck_spec = pl.BlockSpec((g, nb, 2, t), lambda hp, i: (hp, 0, 0, 0))
    return pl.pallas_call(
        body, name="fox_fwd", grid=(npair // g, nb),
        in_specs=[blk, full, full, cq_spec, ck_spec],
        out_specs=[blk, cq_spec],
        out_shape=[jax.ShapeDtypeStruct((tokens, width), F32), jax.ShapeDtypeStruct((npair, tokens, 2), F32)],
        compiler_params=pltpu.CompilerParams(dimension_semantics=("parallel", "arbitrary")),
    )(q, k, v, cq, ck)


def _fox_bwd_call(q, k, v, cq, ck, o, lse, do):
    tokens, width = q.shape
    t = _attn_block(tokens)
    nb = tokens // t
    hd = HEAD_DIM
    npair = width // (2 * hd)

    g = ATTN_PAIRS_PER_STEP if npair % ATTN_PAIRS_PER_STEP == 0 else 1
    heads = [(pp, hh) for pp in range(g) for hh in range(2)]
    nh = range(len(heads))

    def body(q_ref, k_ref, v_ref, cq_ref, ck_ref, o_ref, lse_ref, do_ref, dq_ref, dk_ref, dv_ref, dck_ref, dcq_ref,
             qs, dos):
        i = pl.program_id(1)

        @pl.when(i == 0)
        def _():
            dk_ref[...] = jnp.zeros_like(dk_ref)
            dv_ref[...] = jnp.zeros_like(dv_ref)
            dck_ref[...] = jnp.zeros_like(dck_ref)

        masks = _head_lane_masks(t)
        lanes = [slice(pp * PAIR, (pp + 1) * PAIR) for pp in range(g)]
        for n, (pp, hh) in enumerate(heads):
            qs[n] = jnp.where(masks[hh], q_ref[:, lanes[pp]], 0.0).astype(BF16)
            dos[n] = jnp.where(masks[hh], do_ref[:, lanes[pp]], 0.0).astype(BF16)
        deltas = [jnp.sum(dos[n].astype(F32) * o_ref[:, lanes[pp]], axis=1, keepdims=True) for n, (pp, hh) in enumerate(heads)]
        bias = [cq_ref[pp, :, hh:hh + 1] - lse_ref[pp, :, hh:hh + 1] for pp, hh in heads]

        def block(j, carry, diagonal):
            off = pl.multiple_of(j * t, t)
            k2 = [k_ref[pl.ds(off, t), lanes[pp]].astype(BF16) for pp in range(g)]
            v2 = [v_ref[pl.ds(off, t), lanes[pp]].astype(BF16) for pp in range(g)]
            s = [_dg(qs[n], k2[pp], False, True) + (bias[n] - ck_ref[pp, j][hh:hh + 1, :]) for n, (pp, hh) in enumerate(heads)]
            if diagonal:
                keep = lax.broadcasted_iota(jnp.int32, (t, t), 0) >= lax.broadcasted_iota(jnp.int32, (t, t), 1)
                s = [jnp.where(keep, x, NEG_BIG) for x in s]
            p = [jnp.exp(x) for x in s]
            dp = [_dg(dos[n], v2[pp], False, True) for n, (pp, hh) in enumerate(heads)]
            ds = [p[n] * (dp[n] - deltas[n]) for n in nh]
            dsb = [x.astype(BF16) for x in ds]
            out = tuple((carry[n][0] + _dg(dsb[n], k2[pp], False, False), carry[n][1] + jnp.sum(ds[n], axis=1, keepdims=True))
                        for n, (pp, hh) in enumerate(heads))
            for pp in range(g):
                a, b = 2 * pp, 2 * pp + 1
                dk_ref[pl.ds(off, t), lanes[pp]] += _dg(dsb[a], qs[a], True, False) + _dg(dsb[b], qs[b], True, False)
                dv_ref[pl.ds(off, t), lanes[pp]] += (_dg(p[a].astype(BF16), dos[a], True, False)
                                                     + _dg(p[b].astype(BF16), dos[b], True, False))
            for n, (pp, hh) in enumerate(heads):
                dck_ref[pp, j, hh:hh + 1, :] -= jnp.sum(ds[n], axis=0, keepdims=True)
            return out

        init = tuple((jnp.zeros((t, 2 * hd), F32), jnp.zeros((t, 1), F32)) for _ in heads)
        res = lax.fori_loop(0, i, lambda j, c: block(j, c, False), init)
        res = block(i, res, True)
        for pp in range(g):
            dq_ref[:, lanes[pp]] = jnp.where(masks[0], res[2 * pp][0], res[2 * pp + 1][0])
        for n, (pp, hh) in enumerate(heads):
            dcq_ref[pp, :, hh:hh + 1] = res[n][1]

    blk = pl.BlockSpec((t, g * PAIR), lambda hp, i: (i, hp))
    full = pl.BlockSpec((tokens, g * PAIR), lambda hp, i: (0, hp))
    cq_spec = pl.BlockSpec((g, t, 2), lambda hp, i: (hp, i, 0))
    ck_spec = pl.BlockSpec((g, nb, 2, t), lambda hp, i: (hp, 0, 0, 0))
    tshape = jax.ShapeDtypeStruct((tokens, width), F32)
    return pl.pallas_call(
        body, name="fox_bwd", grid=(npair // g, nb),
        in_specs=[blk, full, full, cq_spec, ck_spec, blk, cq_spec, blk],
        out_specs=[blk, full, full, ck_spec, cq_spec],
        out_shape=[tshape, tshape, tshape, jax.ShapeDtypeStruct((npair, nb, 2, t), F32),
                   jax.ShapeDtypeStruct((npair, tokens, 2), F32)],
        scratch_shapes=[pltpu.VMEM((len(heads), t, PAIR), BF16)] * 2,
        compiler_params=pltpu.CompilerParams(dimension_semantics=("parallel", "arbitrary")),
    )(q, k, v, cq, ck, o, lse, do)


@jax.custom_vjp
def fox_attention(q, k, v, cum):
    return _fox_fwd(q, k, v, cum)[0]


def _fox_fwd(q, k, v, cum):
    cq, ck = _fox_layouts(cum)
    q, k, v = q.astype(BF16), k.astype(BF16), v.astype(BF16)
    o, lse = _fox_fwd_call(q, k, v, cq, ck)
    return o, (q, k, v, cq, ck, o, lse)


def _fox_bwd(res, do):
    q, k, v, cq, ck, o, lse = res
    dq, dk, dv, dck, dcq = _fox_bwd_call(q, k, v, cq, ck, o, lse, do)
    npair, nb, _, t = dck.shape
    dcum = dck.transpose(0, 2, 1, 3).reshape(2 * npair, nb * t).T + dcq.transpose(1, 0, 2).reshape(nb * t, 2 * npair)
    return dq, dk, dv, dcum


fox_attention.defvjp(_fox_fwd, _fox_bwd)


def _loss_call(y, target):
    rows, d = y.shape
    tr = _row_tile(rows, d)

    def body(y_ref, t_ref, loss_ref, dy_ref):
        @pl.when(pl.program_id(0) == 0)
        def _():
            loss_ref[...] = jnp.zeros_like(loss_ref)

        diff = y_ref[...] - t_ref[...]
        dy_ref[...] = diff * (1.0 / d)
        loss_ref[...] += (0.5 / d) * jnp.sum(jnp.sum(diff * diff, axis=1, keepdims=True), axis=0, keepdims=True)

    return pl.pallas_call(
        body, name="loss", grid=(rows // tr,),
        in_specs=[pl.BlockSpec((tr, d), lambda i: (i, 0))] * 2,
        out_specs=[pl.BlockSpec((1, 1), lambda i: (0, 0)), pl.BlockSpec((tr, d), lambda i: (i, 0))],
        out_shape=[jax.ShapeDtypeStruct((1, 1), F32), jax.ShapeDtypeStruct((rows, d), F32)],
        compiler_params=pltpu.CompilerParams(dimension_semantics=("arbitrary",)),
    )(y, target)


def _adamw_call(w, g, m, v):
    rows, cols = w.shape
    tr = _row_tile_ragged(rows, cols, budget=1024 * 1024)
    c1 = 1.0 / (1.0 - ADAM_B1 ** ADAM_STEP)
    c2 = 1.0 / (1.0 - ADAM_B2 ** ADAM_STEP)

    def body(w_ref, g_ref, m_ref, v_ref, d_ref, nm_ref, nv_ref):
        gv = g_ref[...]
        nm = ADAM_B1 * m_ref[...] + (1.0 - ADAM_B1) * gv
        nv = ADAM_B2 * v_ref[...] + (1.0 - ADAM_B2) * (gv * gv)
        nm_ref[...] = nm
        nv_ref[...] = nv
        d_ref[...] = -ADAM_LR * ((nm * c1) / (jnp.sqrt(nv * c2) + ADAM_EPS) + ADAM_WD * w_ref[...])

    spec = pl.BlockSpec((tr, cols), lambda i: (i, 0))
    shape = jax.ShapeDtypeStruct((rows, cols), F32)
    return pl.pallas_call(
        body, name="adamw", grid=(pl.cdiv(rows, tr),),
        in_specs=[spec] * 4, out_specs=[spec] * 3, out_shape=[shape] * 3,
        compiler_params=pltpu.CompilerParams(dimension_semantics=("parallel",)),
    )(w, g, m, v)


def _my_place():
    return lax.axis_index("x"), lax.axis_index("y"), lax.axis_index("c")


def _place_index(px, py, pc):
    return 4 * px + 2 * py + pc


HBM_SPEC = pl.BlockSpec(memory_space=pltpu.HBM)


def _all_gather_call(block):
    def body(x_ref, out_ref, send_sems, recv_sems, local_sem):
        x, y, c = _my_place()
        me, sibling = (x, y, c), (x, y, 1 - c)
        chips = [(1 - x, y), (x, 1 - y), (1 - x, 1 - y)]

        def slot(px, py, pc):
            return out_ref.at[_place_index(px, py, pc)]

        def copy(k, blk, to, src=None):
            return pltpu.make_async_remote_copy(
                src_ref=slot(*blk) if src is None else src, dst_ref=slot(*blk),
                send_sem=send_sems.at[k], recv_sem=recv_sems.at[k],
                device_id=to, device_id_type=pl.DeviceIdType.MESH)

        mine = pltpu.make_async_copy(x_ref, slot(*me), local_sem)
        mine.start()
        first = [copy(0, me, sibling, src=x_ref)]
        first += [copy(1 + j, me, (*chip, c), src=x_ref) for j, chip in enumerate(chips)]
        for cp in first:
            cp.start()
        passed = [copy(4 + j, (*chip, c), sibling) for j, chip in enumerate(chips)]
        for j, chip in enumerate(chips):
            copy(1 + j, (*chip, c), me).wait_recv()
            passed[j].start()
        copy(0, sibling, me).wait_recv()
        for j, chip in enumerate(chips):
            copy(4 + j, (*chip, 1 - c), me).wait_recv()
        for cp in first + passed:
            cp.wait_send()
        mine.wait()

    return pl.pallas_call(
        body, name="all_gather",
        out_shape=jax.ShapeDtypeStruct((N_DEV,) + block.shape, block.dtype),
        in_specs=[HBM_SPEC], out_specs=HBM_SPEC,
        scratch_shapes=[pltpu.SemaphoreType.DMA((7,)), pltpu.SemaphoreType.DMA((7,)), pltpu.SemaphoreType.DMA],
    )(block)


SEM_SPEC = pl.BlockSpec(memory_space=pltpu.SEMAPHORE)
SIDE_EFFECT = pltpu.SideEffectType.DATAFLOW_SIDE_EFFECTING


def _peers():
    x, y, c = _my_place()
    out = []
    for k in range(1, N_DEV):
        peer = (x ^ (k >> 2), y ^ ((k >> 1) & 1), c ^ (k & 1))
        out.append((k - 1, peer, _place_index(*peer)))
    return _place_index(x, y, c), out


def _spread_start(src, per_peer, name, after=None):
    slot = src.shape[1:] if per_peer else src.shape
    order = () if after is None else (after,)

    def body(src_ref, land_ref, *rest):
        send_sems, recv_sems, src_thru, land_thru, token = rest[len(order):]
        mine, peers = _peers()
        for k, peer, peer_idx in peers:
            pltpu.make_async_remote_copy(
                src_ref=src_ref.at[peer_idx] if per_peer else src_ref, dst_ref=land_ref.at[mine],
                send_sem=send_sems.at[k], recv_sem=recv_sems.at[k],
                device_id=peer, device_id_type=pl.DeviceIdType.MESH).start()
        token[...] = jnp.zeros_like(token)

    return pl.pallas_call(
        body, name=name,
        out_shape=(pltpu.SemaphoreType.DMA((N_DEV - 1,)), pltpu.SemaphoreType.DMA((N_DEV - 1,)),
                   pltpu.HBM(src.shape, src.dtype), pltpu.HBM((N_DEV,) + slot, src.dtype),
                   jax.ShapeDtypeStruct((8, 128), F32)),
        in_specs=(HBM_SPEC, HBM_SPEC) + (pl.BlockSpec(memory_space=pl.ANY),) * len(order),
        out_specs=(SEM_SPEC, SEM_SPEC, HBM_SPEC, HBM_SPEC, pl.BlockSpec(memory_space=pltpu.VMEM)),
        input_output_aliases={0: 2, 1: 3},
        compiler_params=pltpu.CompilerParams(has_side_effects=SIDE_EFFECT),
    )(pltpu.with_memory_space_constraint(src, pltpu.HBM),
      pltpu.with_memory_space_constraint(lax.empty((N_DEV,) + slot, src.dtype), pltpu.HBM), *order)


def _spread_wait(handles, after, per_peer, name):
    send_sems, recv_sems, src_thru, land_thru = handles

    def body(src_ref, land_ref, send_sems, recv_sems, after_ref, src_dead, got_ref):
        _, peers = _peers()
        for k, peer, peer_idx in peers:
            copy = pltpu.make_async_remote_copy(
                src_ref=src_ref.at[peer_idx] if per_peer else src_ref, dst_ref=land_ref.at[peer_idx],
                send_sem=send_sems.at[k], recv_sem=recv_sems.at[k],
                device_id=peer, device_id_type=pl.DeviceIdType.MESH)
            copy.wait_send()
            copy.wait_recv()

    return pl.pallas_call(
        body, name=name,
        out_shape=(pltpu.HBM(src_thru.shape, src_thru.dtype), pltpu.HBM(land_thru.shape, land_thru.dtype)),
        in_specs=(HBM_SPEC, HBM_SPEC, SEM_SPEC, SEM_SPEC, pl.BlockSpec(memory_space=pl.ANY)),
        out_specs=(HBM_SPEC, HBM_SPEC), input_output_aliases={0: 0, 1: 1},
        compiler_params=pltpu.CompilerParams(has_side_effects=SIDE_EFFECT),
    )(src_thru, land_thru, send_sems, recv_sems, after)


def _sum_slots_call(slots):
    _, rows, cols = slots.shape
    tr = _row_tile_ragged(rows, cols, budget=512 * 1024)

    def body(s_ref, o_ref):
        acc = s_ref[0].astype(F32)
        for j in range(1, N_DEV):
            acc = acc + s_ref[j].astype(F32)
        o_ref[...] = acc

    return pl.pallas_call(
        body, name="sum_slots", grid=(pl.cdiv(rows, tr),),
        in_specs=[pl.BlockSpec((N_DEV, tr, cols), lambda i: (0, i, 0))],
        out_specs=pl.BlockSpec((tr, cols), lambda i: (i, 0)),
        out_shape=jax.ShapeDtypeStruct((rows, cols), F32),
        compiler_params=pltpu.CompilerParams(dimension_semantics=("parallel",)),
    )(slots)


def _sum_adamw_call(got, own, w, m, v):
    rows, cols = w.shape
    tr = _row_tile_ragged(rows, cols, budget=512 * 1024)
    c1 = 1.0 / (1.0 - ADAM_B1 ** ADAM_STEP)
    c2 = 1.0 / (1.0 - ADAM_B2 ** ADAM_STEP)

    def body(got_ref, own_ref, w_ref, m_ref, v_ref, g_ref, d_ref, nm_ref, nv_ref):
        mine = _place_index(*_my_place())
        gv = jnp.zeros(w_ref.shape, F32)
        for j in range(N_DEV):
            gv = gv + jnp.where(mine == j, own_ref[...], got_ref[j]).astype(F32)
        nm = ADAM_B1 * m_ref[...] + (1.0 - ADAM_B1) * gv
        nv = ADAM_B2 * v_ref[...] + (1.0 - ADAM_B2) * (gv * gv)
        g_ref[...] = gv
        nm_ref[...] = nm
        nv_ref[...] = nv
        d_ref[...] = -ADAM_LR * ((nm * c1) / (jnp.sqrt(nv * c2) + ADAM_EPS) + ADAM_WD * w_ref[...])

    spec = pl.BlockSpec((tr, cols), lambda i: (i, 0))
    shape = jax.ShapeDtypeStruct((rows, cols), F32)
    return pl.pallas_call(
        body, name="sum_adamw", grid=(pl.cdiv(rows, tr),),
        in_specs=[pl.BlockSpec((N_DEV, tr, cols), lambda i: (0, i, 0))] + [spec] * 4,
        out_specs=[spec] * 4, out_shape=[shape] * 4,
        compiler_params=pltpu.CompilerParams(dimension_semantics=("parallel",)),
    )(got, own, w, m, v)


def _with_own_slot(got, own, mine):
    return lax.dynamic_update_index_in_dim(got, own, mine, 0)


def _pack(vectors, width):
    flat = jnp.concatenate([v.reshape(-1) for v in vectors])
    return jnp.pad(flat, (0, width - flat.shape[0])).reshape(width // 128, 128)


def _unpack(packed, like):
    flat = packed.reshape(-1)
    out, at = [], 0
    for v in like:
        out.append(flat[at:at + v.size].reshape(v.shape))
        at += v.size
    return tuple(out)


def _cols_from_slots(slots):
    n, rows, cols = slots.shape
    return slots.transpose(1, 0, 2).reshape(rows, n * cols)


def _rows_from_slots(slots):
    return slots.reshape(-1, slots.shape[2])


def _pad128(n):
    return -(-n // 128) * 128


def _pad_to_tiles(a, axis):
    n = a.shape[axis]
    pads = [(0, 0)] * a.ndim
    pads[axis] = (0, _pad128(n) - n)
    return jnp.pad(a, pads)


def _rwkv_group(take, zeros, rw, dl, al, gl):
    at = 3 * rw
    parts = take(0, at)
    for n in (dl, al, gl):
        parts += take(at, at + n)
        if _pad128(n) > n:
            parts.append(zeros(_pad128(n) - n))
        at += n
    return parts


def _in_proj_layout(slots, rw, fw, dl, al, gl, whole):
    n_slots, rows, d = slots.shape
    wt = slots.reshape(n_slots * rows, d)
    take = lambda lo, hi: [wt[lo:hi]]
    zeros = lambda n: jnp.zeros((n, d), wt.dtype)
    rcols = 3 * rw + dl + al + gl
    fcols = 3 * fw + fw // HEAD_DIM
    group_r = _rwkv_group(take, zeros, rw, dl, al, gl)
    group_f = take(rcols, rcols + fcols) + ([zeros(_pad128(fcols) - fcols)] if _pad128(fcols) > fcols else [])
    group_g = take(rcols + fcols, n_slots * rows)
    if whole:
        return jnp.concatenate(group_r + group_f + group_g, axis=0)
    return tuple(jnp.concatenate(g, axis=0) for g in (group_r, group_f, group_g))


def _low_rank_layout(slots):
    return _pad_to_tiles(_cols_from_slots(slots), 0)


def _stage_embed(meta, x, n1, lp):
    h0 = jnp.concatenate([meta, x, jnp.zeros((lp - meta.shape[0] - x.shape[0], x.shape[1]), F32)], axis=0)
    return h0, rmsnorm(h0, n1)


def _stage_mix(z_r, z_f, small, w2, a2, g2, dims):
    (mu, w0, a0, k_k, k_a, r_k, gn_w, gn_b, q_g, k_g, f_bias) = small
    rw, fw, dl, al, gl = dims
    fcols = 3 * fw + fw // HEAD_DIM

    mu_group = jnp.concatenate(_rwkv_group(lambda lo, hi: [mu[:, lo:hi]], lambda n: jnp.zeros((1, n), F32), rw, dl, al, gl), axis=1)
    r, lw, kf, v, na, b, g = rwkv_prep(z_r, mu_group, w0, a0, k_k, k_a, w2, a2, g2)
    y = wkv7(r, lw, kf, v, na, b)
    y_a = gn_bonus(y, r, kf, v, g, gn_w, gn_b, r_k.reshape(1, rw))

    fq, fk, fv, fl = z_f[:, :fw], z_f[:, fw:2 * fw], z_f[:, 2 * fw:3 * fw], z_f[:, 3 * fw:fcols]
    fq = head_rms(fq, jnp.tile(q_g, (1, fw // HEAD_DIM)) * (HEAD_DIM ** -0.5))
    fk = head_rms(fk, jnp.tile(k_g, (1, fw // HEAD_DIM)))
    cum = jnp.cumsum(jax.nn.log_sigmoid(badd(fl, f_bias)), axis=0)
    y_b = fox_attention(fq, fk, fv, cum)
    return y_a, y_b


def _stage_merge(h0, y_a, y_b, z_g, w_a, w_b, w_o):
    merged = gated_merge(z_g, dense_cols_bf16(y_a, w_a), dense_cols_bf16(y_b, w_b))
    return dense_add(merged, w_o, h0)


def _stage_ffn(h1, n2, w_gu, w_dn):
    return dense_add(swiglu(dense_cols_bf16(rmsnorm(h1, n2), w_gu)), w_dn, h1)


SHARDED = ("meta_tokens", "w_in", "rwkv_w2", "rwkv_a2", "rwkv_g2", "w_branch_a", "w_branch_b", "w_o", "w_gate_up", "w_down")
LOW_RANK = ("rwkv_w2", "rwkv_a2", "rwkv_g2")
SMALL = ("norm1_g", "rwkv_mu", "rwkv_w0", "rwkv_a0", "rwkv_k_k", "rwkv_k_a", "rwkv_r_k", "rwkv_gn_w", "rwkv_gn_b",
         "fox_q_norm_g", "fox_k_norm_g", "fox_f_bias", "norm2_g")
WEIGHTS = ("meta_tokens", "norm1_g", "w_in", "rwkv_mu", "rwkv_w0", "rwkv_w2", "rwkv_a0", "rwkv_a2", "rwkv_g2", "rwkv_k_k",
           "rwkv_k_a", "rwkv_r_k", "rwkv_gn_w", "rwkv_gn_b", "fox_q_norm_g", "fox_k_norm_g", "fox_f_bias", "w_branch_a",
           "w_branch_b", "w_o", "norm2_g", "w_gate_up", "w_down")


def _as2d(a):
    return a.reshape(-1, a.shape[-1])


def kernel(x, meta_tokens, norm1_g, w_in, rwkv_mu, rwkv_w0, rwkv_w2, rwkv_a0, rwkv_a2, rwkv_g2, rwkv_k_k, rwkv_k_a, rwkv_r_k, rwkv_gn_w, rwkv_gn_b, fox_q_norm_g, fox_k_norm_g, fox_f_bias, w_branch_a, w_branch_b, w_o, norm2_g, w_gate_up, w_down, loss_target, m_meta_tokens, m_norm1_g, m_w_in, m_rwkv_mu, m_rwkv_w0, m_rwkv_w2, m_rwkv_a0, m_rwkv_a2, m_rwkv_g2, m_rwkv_k_k, m_rwkv_k_a, m_rwkv_r_k, m_rwkv_gn_w, m_rwkv_gn_b, m_fox_q_norm_g, m_fox_k_norm_g, m_fox_f_bias, m_w_branch_a, m_w_branch_b, m_w_o, m_norm2_g, m_w_gate_up, m_w_down, v_meta_tokens, v_norm1_g, v_w_in, v_rwkv_mu, v_rwkv_w0, v_rwkv_w2, v_rwkv_a0, v_rwkv_a2, v_rwkv_g2, v_rwkv_k_k, v_rwkv_k_a, v_rwkv_r_k, v_rwkv_gn_w, v_rwkv_gn_b, v_fox_q_norm_g, v_fox_k_norm_g, v_fox_f_bias, v_w_branch_a, v_w_branch_b, v_w_o, v_norm2_g, v_w_gate_up, v_w_down):
    given = dict(locals())
    w = {n: given[n] for n in WEIGHTS}
    assert rwkv_r_k.shape[-1] == HEAD_DIM
    n_meta, seq = meta_tokens.shape[0], x.shape[1]
    tokens = n_meta + seq
    lp = -(-tokens // TOKEN_TILE) * TOKEN_TILE
    mine = _place_index(*(lax.axis_index(a) for a in MESH_AXES))
    x2 = x[0]

    local = {n: _as2d(given[n]) for n in given if n != "x" and n != "loss_target"}
    for n in ("w_in", "m_w_in", "v_w_in"):
        local[n] = jnp.transpose(given[n][0])
    blocks = {n: local[n].astype(F32 if n == "meta_tokens" else BF16) for n in SHARDED}
    for prefix in ("", "m_", "v_"):
        local[prefix + "low_rank"] = jnp.concatenate([local[prefix + n] for n in LOW_RANK], axis=0)
    blocks["low_rank"] = jnp.concatenate([blocks[n] for n in LOW_RANK], axis=0)
    low_rank_ends = [sum(local[n].shape[0] for n in LOW_RANK[:i + 1]) for i in range(len(LOW_RANK))]
    low_rank_rows = lambda a, axis: [lax.slice_in_dim(a, lo, hi, axis=axis) for lo, hi in zip([0] + low_rank_ends, low_rank_ends)]
    first = ("meta_tokens", "low_rank")
    started = {n: _spread_start(blocks[n], False, "gather_start_" + n) for n in first}
    zero = sum(started[n][4][0, 0] for n in first)

    def gathered(n, after):
        own, got = _spread_wait(started[n][:4], after, False, "gather_wait_" + n)
        return _with_own_slot(got, own, mine)

    sm = {n: _as2d(w[n]) for n in SMALL}
    small_mix = tuple(sm[n] for n in SMALL[1:-1])
    n1 = sm["norm1_g"] + zero
    rw, fw = w_branch_a.shape[-2], w_branch_b.shape[-2]
    dims = (rw, fw, rwkv_w2.shape[-2], rwkv_a2.shape[-2], rwkv_g2.shape[-2])
    same = lambda s: (s,)

    meta, un_meta = jax.vjp(_cols_from_slots, gathered("meta_tokens", x2))
    (h0, xn), vjp_embed = jax.vjp(lambda m, xs, g: _stage_embed(m, xs, g, lp), meta, x2, n1)
    in_slots = _all_gather_call(blocks["w_in"])
    later = [n for n in SHARDED if n not in first and n not in LOW_RANK and n != "w_in"]
    started.update({n: _spread_start(blocks[n], False, "gather_start_" + n, after=in_slots) for n in later})
    w_groups = _in_proj_layout(in_slots, *dims, whole=False)
    w_cat, un_in = jax.vjp(lambda s: _in_proj_layout(s, *dims, whole=True), in_slots)
    xn_b = xn.astype(BF16)
    behind = sum(started[n][4] for n in later)
    z_r, z_f, z_g = (_matmul(xn_b, wg, tb=True, name="in_proj_" + tag, after=behind, out_dtype=BF16 if tag == "g" else F32)
                     for wg, tag in zip(w_groups, "rfg"))
    (w2, un_w2), (a2, un_a2), (g2, un_g2) = (jax.vjp(_low_rank_layout, s) for s in low_rank_rows(gathered("low_rank", xn), 1))
    (y_a, y_b), vjp_mix = jax.vjp(lambda zr, zf, s, a, b, c: _stage_mix(zr, zf, s, a, b, c, dims),
                                  z_r, z_f, small_mix, w2, a2, g2)
    w_a, w_b = gathered("w_branch_a", y_a), gathered("w_branch_b", y_a)
    w_o_full, un_wo = jax.vjp(_rows_from_slots, gathered("w_o", y_a))
    h1, vjp_merge = jax.vjp(_stage_merge, h0, y_a, y_b, z_g, w_a, w_b, w_o_full)
    w_gu = gathered("w_gate_up", h1)
    w_dn, un_dn = jax.vjp(_rows_from_slots, gathered("w_down", h1))
    y, vjp_ffn = jax.vjp(_stage_ffn, h1, sm["norm2_g"], w_gu, w_dn)

    loss_part, dy_real = _loss_call(y[n_meta:tokens], loss_target[0])
    dy = jnp.pad(dy_real, ((n_meta, lp - tokens), (0, 0)))
    loss = lax.psum(loss_part[0, 0], MESH_AXES)

    sent = {}

    def send_grad(n, dmat, unlayout):
        sent[n] = _spread_start(unlayout(dmat)[0], True, "grad_start_" + n)
        return sent[n][4][0, 0]

    d_h1, d_n2, d_wgu, d_wdn = vjp_ffn(dy)
    behind = send_grad("w_gate_up", d_wgu, same) + send_grad("w_down", d_wdn, un_dn)
    d_h0, d_ya, d_yb, d_zg, d_wa, d_wb, d_wo = vjp_merge(d_h1 + behind)
    behind = send_grad("w_o", d_wo, un_wo) + send_grad("w_branch_a", d_wa, same) + send_grad("w_branch_b", d_wb, same)
    d_zr, d_zf, d_small_mix, d_w2, d_a2, d_g2 = vjp_mix((d_ya + behind.astype(d_ya.dtype), d_yb))
    dproj_b = jnp.concatenate([d_zr.astype(BF16), d_zf.astype(BF16), d_zg.astype(BF16)], axis=1)
    d_wcat = _matmul(dproj_b, xn_b, ta=True, out_dtype=BF16, name="in_proj_dw")
    send_grad("w_in", d_wcat, un_in)
    d_xn = _matmul(dproj_b, w_cat, out_dtype=BF16, name="in_proj_dx", after=sent["w_in"][4])
    send_grad("low_rank", jnp.concatenate([un_w2(d_w2)[0], un_a2(d_a2)[0], un_g2(d_g2)[0]], axis=1), same)
    d_meta, g_x, d_n1 = vjp_embed((d_h0, d_xn))
    send_grad("meta_tokens", d_meta, un_meta)

    small_grads = (d_n1, *d_small_mix, d_n2)
    n_small = sum(g.size for g in small_grads)
    width = -(-n_small // 1024) * 1024
    small_sent = _spread_start(_pack(small_grads, width), False, "small_grad_start")

    grads, delta, new_m, new_v = {}, {}, {}, {}
    after = g_x
    for n in ("w_gate_up", "w_down", "w_o", "w_branch_a", "w_branch_b", "low_rank", "meta_tokens", "w_in"):
        src, got = _spread_wait(sent[n][:4], after, True, "grad_wait_" + n)
        own = lax.dynamic_index_in_dim(src, mine, 0, keepdims=False)
        stepped = _sum_adamw_call(got, own, local[n], local["m_" + n], local["v_" + n])
        after = stepped[2]
        if n == "low_rank":
            for out, t in zip((grads, delta, new_m, new_v), stepped):
                out.update({name: part.reshape(w[name].shape) for name, part in zip(LOW_RANK, low_rank_rows(t, 0))})
            continue
        back = (lambda t: jnp.transpose(t)[None]) if n == "w_in" else (lambda t: t.reshape(w[n].shape))
        grads[n], delta[n], new_m[n], new_v[n] = (back(t) for t in stepped)
    own_small, got_small = _spread_wait(small_sent[:4], after, False, "small_grad_wait")
    small_total = _unpack(_sum_slots_call(_with_own_slot(got_small, own_small, mine)), small_grads)
    grads.update({n: g.reshape(w[n].shape) for n, g in zip(SMALL, small_total)})
    packs = [_pack([src[n] if p == "" else given[p + n] for n in SMALL], width)
             for p, src in (("", w), ("", grads), ("m_", None), ("v_", None))]
    like = [w[n] for n in SMALL]
    for out, packed in zip((delta, new_m, new_v), _adamw_call(*packs)):
        out.update(dict(zip(SMALL, _unpack(packed, like))))

    return (loss, g_x[None], *[grads[n] for n in WEIGHTS], *[delta[n] for n in WEIGHTS],
            *[new_m[n] for n in WEIGHTS], *[new_v[n] for n in WEIGHTS])
```
